```python
import jax, jax.numpy as jnp
from jax import lax
import numpy as np

D_MODEL = 1024
BATCH = 8
SEQ = 8192
DEPTH = 2

N_MIXERS = 2
N_A_LAYERS = (DEPTH + 1) // 2
N_B_LAYERS = DEPTH // 2
CHUNK = 128
GM_WIDTH = 2 * D_MODEL
GM_GROUPS = 8
GM_GROUP_DIM = GM_WIDTH // GM_GROUPS
N_HEADS = 16
HEAD_DIM = D_MODEL // N_HEADS
Q_BLOCK = 128
FFN_DIM = 2 * D_MODEL
CONV_WIDTH = 3
RMS_EPS = 1e-6
LN_EPS = 1e-5
FORGET_BIAS_INIT = 4.0

kernel_name = "hybrid_gmlp_fox_convffn"


def rmsnorm(x, g):
    xf = x.astype(jnp.float32)
    y = xf * lax.rsqrt(jnp.mean(xf * xf, axis=-1, keepdims=True) + RMS_EPS)
    return (y * g.astype(jnp.float32)).astype(x.dtype)


def layernorm(x, g, b):
    xf = x.astype(jnp.float32)
    mu = jnp.mean(xf, axis=-1, keepdims=True)
    xc = xf - mu
    y = xc * lax.rsqrt(jnp.mean(xc * xc, axis=-1, keepdims=True) + LN_EPS)
    return (y * g.astype(jnp.float32) + b.astype(jnp.float32)).astype(x.dtype)


def chunked_spatial_gating_mixer(h, w_in, ln_g, ln_b, w_s, b_s, w_out):
    bsz, seq, _ = h.shape
    z = jax.nn.gelu(h @ w_in)
    u, v = jnp.split(z, 2, axis=-1)
    v = layernorm(v, ln_g, ln_b)
    n_chunks = seq // CHUNK
    v = v.reshape(bsz, n_chunks, CHUNK, GM_GROUPS, GM_GROUP_DIM)
    causal = jnp.tril(jnp.ones((CHUNK, CHUNK), dtype=bool))
    w_causal = jnp.where(causal[None], w_s, jnp.zeros((), w_s.dtype))
    s = jnp.einsum("gts,bnsgc->bntgc", w_causal, v)
    s = s + b_s.T[None, None, :, :, None]
    s = s.reshape(bsz, seq, GM_WIDTH)
    return (u * s) @ w_out


def forgetting_attention_mixer(h, w_qkvf, b_f, w_o):
    bsz, seq, _ = h.shape
    proj = h @ w_qkvf
    q, k, v, f_logit = jnp.split(proj, [D_MODEL, 2 * D_MODEL, 3 * D_MODEL], axis=-1)
    q = q.reshape(bsz, seq, N_HEADS, HEAD_DIM)
    k = k.reshape(bsz, seq, N_HEADS, HEAD_DIM)
    v = v.reshape(bsz, seq, N_HEADS, HEAD_DIM)
    log_f = jax.nn.log_sigmoid((f_logit + b_f).astype(jnp.float32))
    cum = jnp.cumsum(log_f, axis=1).transpose(0, 2, 1)
    scale = HEAD_DIM ** -0.5
    neg = jnp.finfo(jnp.float32).min
    outs = []
    for blk in range(seq // Q_BLOCK):
        q0 = blk * Q_BLOCK
        q1 = q0 + Q_BLOCK
        qb = q[:, q0:q1]
        kb = k[:, :q1]
        vb = v[:, :q1]
        logits = jnp.einsum("bqhd,bkhd->bhqk", qb, kb,
                            preferred_element_type=jnp.float32) * scale
        logits = logits + cum[:, :, q0:q1, None] - cum[:, :, None, :q1]
        mask = jnp.arange(q0, q1)[:, None] >= jnp.arange(q1)[None, :]
        logits = jnp.where(mask, logits, neg)
        p = jax.nn.softmax(logits, axis=-1)
        outs.append(jnp.einsum("bhqk,bkhd->bqhd", p.astype(vb.dtype), vb))
    o = jnp.concatenate(outs, axis=1).reshape(bsz, seq, D_MODEL)
    return o @ w_o


def conv_gated_ffn(h, w_gate, w_up, conv_w, conv_b, w_down):
    seq = h.shape[1]
    a = h @ w_gate
    a_pad = jnp.pad(a, ((0, 0), (CONV_WIDTH - 1, 0), (0, 0)))
    a = conv_b + a_pad[:, 0:seq] * conv_w[0]
    for i in range(1, CONV_WIDTH):
        a = a + a_pad[:, i:i + seq] * conv_w[i]
    return (jax.nn.silu(a) * (h @ w_up)) @ w_down


def _fwd_setup_inputs(seed: int = 0) -> dict:
    key = jax.random.key(seed)
    ks = jax.random.split(key, 20)

    def normal(k, shape, scale):
        return jax.random.normal(k, shape, jnp.float32) * scale

    D, E, G, C, H, F = D_MODEL, GM_WIDTH, GM_GROUPS, CHUNK, N_HEADS, FFN_DIM
    return {
        "x": normal(ks[0], (BATCH, SEQ, D), 1.0),
        "mix_norm_g": 1.0 + normal(ks[1], (DEPTH, D), 0.02),
        "ffn_norm_g": 1.0 + normal(ks[2], (DEPTH, D), 0.02),
        "gm_w_in": normal(ks[3], (N_A_LAYERS, D, 2 * E), D ** -0.5),
        "gm_ln_g": 1.0 + normal(ks[4], (N_A_LAYERS, E), 0.02),
        "gm_ln_b": normal(ks[5], (N_A_LAYERS, E), 0.02),
        "gm_w_s": normal(ks[6], (N_A_LAYERS, G, C, C), C ** -0.5),
        "gm_b_s": 1.0 + normal(ks[7], (N_A_LAYERS, G, C), 0.02),
        "gm_w_out": normal(ks[8], (N_A_LAYERS, E, D), E ** -0.5),
        "fox_w_qkvf": normal(ks[9], (N_B_LAYERS, D, 3 * D + H), D ** -0.5),
        "fox_b_f": FORGET_BIAS_INIT + normal(ks[10], (N_B_LAYERS, H), 0.5),
        "fox_w_o": normal(ks[11], (N_B_LAYERS, D, D), D ** -0.5),
        "ffn_w_gate": normal(ks[12], (DEPTH, D, F), D ** -0.5),
        "ffn_w_up": normal(ks[13], (DEPTH, D, F), D ** -0.5),
        "ffn_conv_w": normal(ks[14], (DEPTH, CONV_WIDTH, F), CONV_WIDTH ** -0.5),
        "ffn_conv_b": normal(ks[15], (DEPTH, F), 0.01),
        "ffn_w_down": normal(ks[16], (DEPTH, F, D), F ** -0.5),
        "final_norm_g": 1.0 + normal(ks[17], (D,), 0.02),
    }


def _fwd_reference(x, mix_norm_g, ffn_norm_g, gm_w_in, gm_ln_g, gm_ln_b, gm_w_s, gm_b_s,
              gm_w_out, fox_w_qkvf, fox_b_f, fox_w_o, ffn_w_gate, ffn_w_up,
              ffn_conv_w, ffn_conv_b, ffn_w_down, final_norm_g):
    h = x
    for i in range(DEPTH):
        hn = rmsnorm(h, mix_norm_g[i])
        j = i // N_MIXERS
        if i % N_MIXERS == 0:
            mix = chunked_spatial_gating_mixer(hn, gm_w_in[j], gm_ln_g[j], gm_ln_b[j],
                                               gm_w_s[j], gm_b_s[j], gm_w_out[j])
        else:
            mix = forgetting_attention_mixer(hn, fox_w_qkvf[j], fox_b_f[j], fox_w_o[j])
        h = h + mix
        hn = rmsnorm(h, ffn_norm_g[i])
        h = h + conv_gated_ffn(hn, ffn_w_gate[i], ffn_w_up[i], ffn_conv_w[i],
                               ffn_conv_b[i], ffn_w_down[i])
    return rmsnorm(h, final_norm_g)


import jax as _jax
import jax.numpy as _jnp

TWIN_FORMAT = 'train_step'
FWD_PARAMS = ['x', 'mix_norm_g', 'ffn_norm_g', 'gm_w_in', 'gm_ln_g', 'gm_ln_b', 'gm_w_s', 'gm_b_s', 'gm_w_out', 'fox_w_qkvf', 'fox_b_f', 'fox_w_o', 'ffn_w_gate', 'ffn_w_up', 'ffn_conv_w', 'ffn_conv_b', 'ffn_w_down', 'final_norm_g']
TWIN_WEIGHTS = ['mix_norm_g', 'ffn_norm_g', 'gm_w_in', 'gm_ln_g', 'gm_ln_b', 'gm_w_s', 'gm_b_s', 'gm_w_out', 'fox_w_qkvf', 'fox_b_f', 'fox_w_o', 'ffn_w_gate', 'ffn_w_up', 'ffn_conv_w', 'ffn_conv_b', 'ffn_w_down', 'final_norm_g']
TWIN_DIFF_INPUT = 'x'
TWIN_INPUTS = ['x', 'mix_norm_g', 'ffn_norm_g', 'gm_w_in', 'gm_ln_g', 'gm_ln_b', 'gm_w_s', 'gm_b_s', 'gm_w_out', 'fox_w_qkvf', 'fox_b_f', 'fox_w_o', 'ffn_w_gate', 'ffn_w_up', 'ffn_conv_w', 'ffn_conv_b', 'ffn_w_down', 'final_norm_g', 'loss_target', 'm_mix_norm_g', 'm_ffn_norm_g', 'm_gm_w_in', 'm_gm_ln_g', 'm_gm_ln_b', 'm_gm_w_s', 'm_gm_b_s', 'm_gm_w_out', 'm_fox_w_qkvf', 'm_fox_b_f', 'm_fox_w_o', 'm_ffn_w_gate', 'm_ffn_w_up', 'm_ffn_conv_w', 'm_ffn_conv_b', 'm_ffn_w_down', 'm_final_norm_g', 'v_mix_norm_g', 'v_ffn_norm_g', 'v_gm_w_in', 'v_gm_ln_g', 'v_gm_ln_b', 'v_gm_w_s', 'v_gm_b_s', 'v_gm_w_out', 'v_fox_w_qkvf', 'v_fox_b_f', 'v_fox_w_o', 'v_ffn_w_gate', 'v_ffn_w_up', 'v_ffn_conv_w', 'v_ffn_conv_b', 'v_ffn_w_down', 'v_final_norm_g']
TWIN_OUTPUTS = ['loss', 'grad_x', 'grad_mix_norm_g', 'grad_ffn_norm_g', 'grad_gm_w_in', 'grad_gm_ln_g', 'grad_gm_ln_b', 'grad_gm_w_s', 'grad_gm_b_s', 'grad_gm_w_out', 'grad_fox_w_qkvf', 'grad_fox_b_f', 'grad_fox_w_o', 'grad_ffn_w_gate', 'grad_ffn_w_up', 'grad_ffn_conv_w', 'grad_ffn_conv_b', 'grad_ffn_w_down', 'grad_final_norm_g', 'delta_mix_norm_g', 'delta_ffn_norm_g', 'delta_gm_w_in', 'delta_gm_ln_g', 'delta_gm_ln_b', 'delta_gm_w_s', 'delta_gm_b_s', 'delta_gm_w_out', 'delta_fox_w_qkvf', 'delta_fox_b_f', 'delta_fox_w_o', 'delta_ffn_w_gate', 'delta_ffn_w_up', 'delta_ffn_conv_w', 'delta_ffn_conv_b', 'delta_ffn_w_down', 'delta_final_norm_g', 'new_m_mix_norm_g', 'new_m_ffn_norm_g', 'new_m_gm_w_in', 'new_m_gm_ln_g', 'new_m_gm_ln_b', 'new_m_gm_w_s', 'new_m_gm_b_s', 'new_m_gm_w_out', 'new_m_fox_w_qkvf', 'new_m_fox_b_f', 'new_m_fox_w_o', 'new_m_ffn_w_gate', 'new_m_ffn_w_up', 'new_m_ffn_conv_w', 'new_m_ffn_conv_b', 'new_m_ffn_w_down', 'new_m_final_norm_g', 'new_v_mix_norm_g', 'new_v_ffn_norm_g', 'new_v_gm_w_in', 'new_v_gm_ln_g', 'new_v_gm_ln_b', 'new_v_gm_w_s', 'new_v_gm_b_s', 'new_v_gm_w_out', 'new_v_fox_w_qkvf', 'new_v_fox_b_f', 'new_v_fox_w_o', 'new_v_ffn_w_gate', 'new_v_ffn_w_up', 'new_v_ffn_conv_w', 'new_v_ffn_conv_b', 'new_v_ffn_w_down', 'new_v_final_norm_g']
TWIN_LEAF_KINDS = {'loss': 'loss', 'grad_x': 'grad_x', 'grad_mix_norm_g': 'grad_w', 'grad_ffn_norm_g': 'grad_w', 'grad_gm_w_in': 'grad_w', 'grad_gm_ln_g': 'grad_w', 'grad_gm_ln_b': 'grad_w', 'grad_gm_w_s': 'grad_w', 'grad_gm_b_s': 'grad_w', 'grad_gm_w_out': 'grad_w', 'grad_fox_w_qkvf': 'grad_w', 'grad_fox_b_f': 'grad_w', 'grad_fox_w_o': 'grad_w', 'grad_ffn_w_gate': 'grad_w', 'grad_ffn_w_up': 'grad_w', 'grad_ffn_conv_w': 'grad_w', 'grad_ffn_conv_b': 'grad_w', 'grad_ffn_w_down': 'grad_w', 'grad_final_norm_g': 'grad_w', 'delta_mix_norm_g': 'delta_w', 'delta_ffn_norm_g': 'delta_w', 'delta_gm_w_in': 'delta_w', 'delta_gm_ln_g': 'delta_w', 'delta_gm_ln_b': 'delta_w', 'delta_gm_w_s': 'delta_w', 'delta_gm_b_s': 'delta_w', 'delta_gm_w_out': 'delta_w', 'delta_fox_w_qkvf': 'delta_w', 'delta_fox_b_f': 'delta_w', 'delta_fox_w_o': 'delta_w', 'delta_ffn_w_gate': 'delta_w', 'delta_ffn_w_up': 'delta_w', 'delta_ffn_conv_w': 'delta_w', 'delta_ffn_conv_b': 'delta_w', 'delta_ffn_w_down': 'delta_w', 'delta_final_norm_g': 'delta_w', 'new_m_mix_norm_g': 'new_m', 'new_m_ffn_norm_g': 'new_m', 'new_m_gm_w_in': 'new_m', 'new_m_gm_ln_g': 'new_m', 'new_m_gm_ln_b': 'new_m', 'new_m_gm_w_s': 'new_m', 'new_m_gm_b_s': 'new_m', 'new_m_gm_w_out': 'new_m', 'new_m_fox_w_qkvf': 'new_m', 'new_m_fox_b_f': 'new_m', 'new_m_fox_w_o': 'new_m', 'new_m_ffn_w_gate': 'new_m', 'new_m_ffn_w_up': 'new_m', 'new_m_ffn_conv_w': 'new_m', 'new_m_ffn_conv_b': 'new_m', 'new_m_ffn_w_down': 'new_m', 'new_m_final_norm_g': 'new_m', 'new_v_mix_norm_g': 'new_v', 'new_v_ffn_norm_g': 'new_v', 'new_v_gm_w_in': 'new_v', 'new_v_gm_ln_g': 'new_v', 'new_v_gm_ln_b': 'new_v', 'new_v_gm_w_s': 'new_v', 'new_v_gm_b_s': 'new_v', 'new_v_gm_w_out': 'new_v', 'new_v_fox_w_qkvf': 'new_v', 'new_v_fox_b_f': 'new_v', 'new_v_fox_w_o': 'new_v', 'new_v_ffn_w_gate': 'new_v', 'new_v_ffn_w_up': 'new_v', 'new_v_ffn_conv_w': 'new_v', 'new_v_ffn_conv_b': 'new_v', 'new_v_ffn_w_down': 'new_v', 'new_v_final_norm_g': 'new_v'}


def _forward(args):
    return _fwd_reference(*[args[k] for k in FWD_PARAMS])


def _output_shape():
    out = _jax.eval_shape(lambda: _forward(_fwd_setup_inputs(0)))
    return out.shape, out.dtype

N_MICROBATCH = 1
ADAM_LR = 0.001
ADAM_B1 = 0.9
ADAM_B2 = 0.999
ADAM_EPS = 1e-08
ADAM_WD = 0.01
ADAM_STEP = 10
PER_EXAMPLE_BATCH_AXIS = {'x': 0, 'loss_target': 0}
SHARED_INPUTS = []
_WEIGHT_DTYPES = {'mix_norm_g': _jnp.float32, 'ffn_norm_g': _jnp.float32, 'gm_w_in': _jnp.float32, 'gm_ln_g': _jnp.float32, 'gm_ln_b': _jnp.float32, 'gm_w_s': _jnp.float32, 'gm_b_s': _jnp.float32, 'gm_w_out': _jnp.float32, 'fox_w_qkvf': _jnp.float32, 'fox_b_f': _jnp.float32, 'fox_w_o': _jnp.float32, 'ffn_w_gate': _jnp.float32, 'ffn_w_up': _jnp.float32, 'ffn_conv_w': _jnp.float32, 'ffn_conv_b': _jnp.float32, 'ffn_w_down': _jnp.float32, 'final_norm_g': _jnp.float32}
MOMENT_SCALE = {'mix_norm_g': 1.702987e-01, 'ffn_norm_g': 1.649044e-01, 'gm_w_in': 1.131012e-01, 'gm_ln_g': 7.502497e-02, 'gm_ln_b': 7.422417e-02, 'gm_w_s': 1.078378e-01, 'gm_b_s': 1.584339e-01, 'gm_w_out': 1.963876e-01, 'fox_w_qkvf': 4.687307e-02, 'fox_b_f': 3.315149e-01, 'fox_w_o': 6.110269e-02, 'ffn_w_gate': 7.949941e-02, 'ffn_w_up': 7.652148e-02, 'ffn_conv_w': 8.056003e-02, 'ffn_conv_b': 8.181914e-02, 'ffn_w_down': 1.091467e-01, 'final_norm_g': 6.423394e+01}


def _to_microbatches(a, axis):
    t = _jnp.moveaxis(a, axis, 0)
    t = t.reshape((N_MICROBATCH, t.shape[0] // N_MICROBATCH) + t.shape[1:])
    return _jnp.moveaxis(t, 1, axis + 1)


def setup_inputs(seed: int = 0) -> dict:
    inp = _fwd_setup_inputs(seed)
    key = _jax.random.fold_in(_jax.random.key(seed), 7919)
    shape, _ = _output_shape()
    out = dict(inp)
    out["loss_target"] = _jax.random.normal(_jax.random.fold_in(key, 0), shape, _jnp.float32)
    for i, name in enumerate(TWIN_WEIGHTS):
        w = inp[name].astype(_jnp.float32)
        if MOMENT_SCALE is None:
            s = _jnp.sqrt(_jnp.mean(_jnp.square(w)) + 1e-30)
        else:
            s = MOMENT_SCALE[name]
        km, kv = _jax.random.split(_jax.random.fold_in(key, i + 1))
        out[name] = w
        out["m_" + name] = s * _jax.random.normal(km, w.shape, _jnp.float32)
        out["v_" + name] = (s * s) * _jax.random.uniform(kv, w.shape, _jnp.float32, 0.5, 1.5)
    if N_MICROBATCH > 1:
        for name, axis in PER_EXAMPLE_BATCH_AXIS.items():
            out[name] = _to_microbatches(out[name], axis)
    return {'x': out['x'], 'mix_norm_g': out['mix_norm_g'], 'ffn_norm_g': out['ffn_norm_g'], 'gm_w_in': out['gm_w_in'], 'gm_ln_g': out['gm_ln_g'], 'gm_ln_b': out['gm_ln_b'], 'gm_w_s': out['gm_w_s'], 'gm_b_s': out['gm_b_s'], 'gm_w_out': out['gm_w_out'], 'fox_w_qkvf': out['fox_w_qkvf'], 'fox_b_f': out['fox_b_f'], 'fox_w_o': out['fox_w_o'], 'ffn_w_gate': out['ffn_w_gate'], 'ffn_w_up': out['ffn_w_up'], 'ffn_conv_w': out['ffn_conv_w'], 'ffn_conv_b': out['ffn_conv_b'], 'ffn_w_down': out['ffn_w_down'], 'final_norm_g': out['final_norm_g'], 'loss_target': out['loss_target'], 'm_mix_norm_g': out['m_mix_norm_g'], 'm_ffn_norm_g': out['m_ffn_norm_g'], 'm_gm_w_in': out['m_gm_w_in'], 'm_gm_ln_g': out['m_gm_ln_g'], 'm_gm_ln_b': out['m_gm_ln_b'], 'm_gm_w_s': out['m_gm_w_s'], 'm_gm_b_s': out['m_gm_b_s'], 'm_gm_w_out': out['m_gm_w_out'], 'm_fox_w_qkvf': out['m_fox_w_qkvf'], 'm_fox_b_f': out['m_fox_b_f'], 'm_fox_w_o': out['m_fox_w_o'], 'm_ffn_w_gate': out['m_ffn_w_gate'], 'm_ffn_w_up': out['m_ffn_w_up'], 'm_ffn_conv_w': out['m_ffn_conv_w'], 'm_ffn_conv_b': out['m_ffn_conv_b'], 'm_ffn_w_down': out['m_ffn_w_down'], 'm_final_norm_g': out['m_final_norm_g'], 'v_mix_norm_g': out['v_mix_norm_g'], 'v_ffn_norm_g': out['v_ffn_norm_g'], 'v_gm_w_in': out['v_gm_w_in'], 'v_gm_ln_g': out['v_gm_ln_g'], 'v_gm_ln_b': out['v_gm_ln_b'], 'v_gm_w_s': out['v_gm_w_s'], 'v_gm_b_s': out['v_gm_b_s'], 'v_gm_w_out': out['v_gm_w_out'], 'v_fox_w_qkvf': out['v_fox_w_qkvf'], 'v_fox_b_f': out['v_fox_b_f'], 'v_fox_w_o': out['v_fox_w_o'], 'v_ffn_w_gate': out['v_ffn_w_gate'], 'v_ffn_w_up': out['v_ffn_w_up'], 'v_ffn_conv_w': out['v_ffn_conv_w'], 'v_ffn_conv_b': out['v_ffn_conv_b'], 'v_ffn_w_down': out['v_ffn_w_down'], 'v_final_norm_g': out['v_final_norm_g']}


def _loss(weights, diff, rest, loss_target):
    with _jax.named_scope("forward"):
        args = {**rest, TWIN_DIFF_INPUT: diff, **{k: w.astype(_WEIGHT_DTYPES[k]) for k, w in weights.items()}}
        y = _forward(args)
    with _jax.named_scope("loss_head"):
        err = _jnp.square(y.astype(_jnp.float32) - loss_target)
        return 0.5 * _jnp.sum(_jnp.mean(err, axis=-1)) if err.ndim else 0.5 * err


def _adamw(w, g, m, v):
    m = ADAM_B1 * m + (1.0 - ADAM_B1) * g
    v = ADAM_B2 * v + (1.0 - ADAM_B2) * _jnp.square(g)
    m_hat = m / (1.0 - ADAM_B1 ** ADAM_STEP)
    v_hat = v / (1.0 - ADAM_B2 ** ADAM_STEP)
    delta = -ADAM_LR * (m_hat / (_jnp.sqrt(v_hat) + ADAM_EPS) + ADAM_WD * w)
    return delta, m, v


def reference(x, mix_norm_g, ffn_norm_g, gm_w_in, gm_ln_g, gm_ln_b, gm_w_s, gm_b_s, gm_w_out, fox_w_qkvf, fox_b_f, fox_w_o, ffn_w_gate, ffn_w_up, ffn_conv_w, ffn_conv_b, ffn_w_down, final_norm_g, loss_target, m_mix_norm_g, m_ffn_norm_g, m_gm_w_in, m_gm_ln_g, m_gm_ln_b, m_gm_w_s, m_gm_b_s, m_gm_w_out, m_fox_w_qkvf, m_fox_b_f, m_fox_w_o, m_ffn_w_gate, m_ffn_w_up, m_ffn_conv_w, m_ffn_conv_b, m_ffn_w_down, m_final_norm_g, v_mix_norm_g, v_ffn_norm_g, v_gm_w_in, v_gm_ln_g, v_gm_ln_b, v_gm_w_s, v_gm_b_s, v_gm_w_out, v_fox_w_qkvf, v_fox_b_f, v_fox_w_o, v_ffn_w_gate, v_ffn_w_up, v_ffn_conv_w, v_ffn_conv_b, v_ffn_w_down, v_final_norm_g):
    given = dict(x=x, mix_norm_g=mix_norm_g, ffn_norm_g=ffn_norm_g, gm_w_in=gm_w_in, gm_ln_g=gm_ln_g, gm_ln_b=gm_ln_b, gm_w_s=gm_w_s, gm_b_s=gm_b_s, gm_w_out=gm_w_out, fox_w_qkvf=fox_w_qkvf, fox_b_f=fox_b_f, fox_w_o=fox_w_o, ffn_w_gate=ffn_w_gate, ffn_w_up=ffn_w_up, ffn_conv_w=ffn_conv_w, ffn_conv_b=ffn_conv_b, ffn_w_down=ffn_w_down, final_norm_g=final_norm_g, loss_target=loss_target, m_mix_norm_g=m_mix_norm_g, m_ffn_norm_g=m_ffn_norm_g, m_gm_w_in=m_gm_w_in, m_gm_ln_g=m_gm_ln_g, m_gm_ln_b=m_gm_ln_b, m_gm_w_s=m_gm_w_s, m_gm_b_s=m_gm_b_s, m_gm_w_out=m_gm_w_out, m_fox_w_qkvf=m_fox_w_qkvf, m_fox_b_f=m_fox_b_f, m_fox_w_o=m_fox_w_o, m_ffn_w_gate=m_ffn_w_gate, m_ffn_w_up=m_ffn_w_up, m_ffn_conv_w=m_ffn_conv_w, m_ffn_conv_b=m_ffn_conv_b, m_ffn_w_down=m_ffn_w_down, m_final_norm_g=m_final_norm_g, v_mix_norm_g=v_mix_norm_g, v_ffn_norm_g=v_ffn_norm_g, v_gm_w_in=v_gm_w_in, v_gm_ln_g=v_gm_ln_g, v_gm_ln_b=v_gm_ln_b, v_gm_w_s=v_gm_w_s, v_gm_b_s=v_gm_b_s, v_gm_w_out=v_gm_w_out, v_fox_w_qkvf=v_fox_w_qkvf, v_fox_b_f=v_fox_b_f, v_fox_w_o=v_fox_w_o, v_ffn_w_gate=v_ffn_w_gate, v_ffn_w_up=v_ffn_w_up, v_ffn_conv_w=v_ffn_conv_w, v_ffn_conv_b=v_ffn_conv_b, v_ffn_w_down=v_ffn_w_down, v_final_norm_g=v_final_norm_g)
    weights = {n: given[n] for n in TWIN_WEIGHTS}
    shared = {n: given[n] for n in SHARED_INPUTS}
    per_example = {n: given[n] for n in ['x']}
    grad_fn = _jax.value_and_grad(_loss, argnums=(0, 1))

    def one_microbatch(ex, loss_target):
        ex = dict(ex)
        diff = ex.pop(TWIN_DIFF_INPUT)
        return grad_fn(weights, diff, {**shared, **ex}, loss_target)

    if N_MICROBATCH == 1:
        loss, (grad_w, grad_x) = one_microbatch(per_example, given["loss_target"])
    else:
        def body(carry, xs):
            loss_sum, grad_sum = carry
            l_k, (gw_k, gx_k) = one_microbatch(xs[0], xs[1])
            with _jax.named_scope("update"):
                return (loss_sum + l_k, _jax.tree.map(_jnp.add, grad_sum, gw_k)), gx_k

        init = (_jnp.zeros((), _jnp.float32), _jax.tree.map(_jnp.zeros_like, weights))
        (loss, grad_w), grad_x = _jax.lax.scan(body, init, (per_example, given["loss_target"]))
    with _jax.named_scope("update"):
        delta_w, new_m, new_v = {}, {}, {}
        for n in TWIN_WEIGHTS:
            delta_w[n], new_m[n], new_v[n] = _adamw(weights[n], grad_w[n], given["m_" + n], given["v_" + n])
    return (loss, grad_x, *[grad_w[n] for n in TWIN_WEIGHTS], *[delta_w[n] for n in TWIN_WEIGHTS],
            *[new_m[n] for n in TWIN_WEIGHTS], *[new_v[n] for n in TWIN_WEIGHTS])
```

```python
import functools
import math

import numpy as np
import jax
import jax.numpy as jnp
from jax import lax
from jax.experimental import pallas as pl
from jax.experimental.pallas import tpu as pltpu

F32 = jnp.float32
BF16 = jnp.bfloat16
MESH = pl.DeviceIdType.MESH

N_HEADS = 16
HEAD_DIM = 64
CHUNK = 128
GM_GROUPS = 8
RMS_EPS = 1e-6
LN_EPS = 1e-5
ADAM_LR = 0.001
ADAM_B1 = 0.9
ADAM_B2 = 0.999
ADAM_EPS = 1e-08
ADAM_WD = 0.01
ADAM_STEP = 10
N_DEV = 8

LANES = 128
VMEM_BYTES_V7X = 64 * 1024 * 1024
VMEM_LIMIT = 56 * 1024 * 1024

TM = 512
TM_MM = 1024
TT = 1024
TQ = 512
TF = 512
NEG = -1e30


def _cp(sem=None, vmem=VMEM_LIMIT):
    return pltpu.CompilerParams(dimension_semantics=sem, vmem_limit_bytes=vmem)


def _gelu(x):
    c = math.sqrt(2.0 / math.pi)
    return x * (0.5 * (1.0 + jnp.tanh(c * (x + 0.044715 * (x * x * x)))))


def _gelu_grad(x):
    c = math.sqrt(2.0 / math.pi)
    t = jnp.tanh(c * (x + 0.044715 * (x * x * x)))
    return 0.5 * (1.0 + t) + x * (0.5 * (1.0 - t * t)) * (c * (1.0 + 3.0 * 0.044715 * (x * x)))


def _sigmoid(x):
    return 1.0 / (1.0 + jnp.exp(-x))


def _dot_nt(a, b):
    return lax.dot_general(a, b, (((1,), (1,)), ((), ())), preferred_element_type=F32)


def _dot_tn(a, b):
    return lax.dot_general(a, b, (((0,), (0,)), ((), ())), preferred_element_type=F32)


def _rms_fwd(h, g, name):
    T, D = h.shape
    tm = min(TM, T)

    def body(h_ref, g_ref, o_ref):
        x = h_ref[...]
        r = lax.rsqrt(jnp.mean(x * x, axis=-1, keepdims=True) + RMS_EPS)
        o_ref[...] = ((x * r) * g_ref[...]).astype(BF16)

    return pl.pallas_call(
        body, name=name, grid=(T // tm,),
        in_specs=[pl.BlockSpec((tm, D), lambda i: (i, 0)), pl.BlockSpec((1, D), lambda i: (0, 0))],
        out_specs=pl.BlockSpec((tm, D), lambda i: (i, 0)),
        out_shape=jax.ShapeDtypeStruct((T, D), BF16),
        compiler_params=_cp(("parallel",)),
    )(h, g)


def _rms_bwd(dhn, h, g, dres, name):
    T, D = h.shape
    tm = min(TM, T)

    def body(d_ref, h_ref, g_ref, r_ref, o_ref, ob_ref, dg_ref):
        x = h_ref[...]
        d = d_ref[...]
        r = lax.rsqrt(jnp.mean(x * x, axis=-1, keepdims=True) + RMS_EPS)
        dyg = d * g_ref[...]
        dot = jnp.mean(dyg * x, axis=-1, keepdims=True)
        dh = r_ref[...] + (r * dyg - x * ((r * r * r) * dot))
        o_ref[...] = dh
        ob_ref[...] = dh.astype(BF16)
        part = jnp.sum(d * (x * r), axis=0, keepdims=True)

        @pl.when(pl.program_id(0) == 0)
        def _():
            dg_ref[...] = part

        @pl.when(pl.program_id(0) != 0)
        def _():
            dg_ref[...] += part

    blk = pl.BlockSpec((tm, D), lambda i: (i, 0))
    row = pl.BlockSpec((1, D), lambda i: (0, 0))
    return pl.pallas_call(
        body, name=name, grid=(T // tm,),
        in_specs=[blk, blk, row, blk],
        out_specs=[blk, blk, row],
        out_shape=[jax.ShapeDtypeStruct((T, D), F32), jax.ShapeDtypeStruct((T, D), BF16),
                   jax.ShapeDtypeStruct((1, D), F32)],
        compiler_params=_cp(("arbitrary",)),
    )(dhn, h, g, dres)


def _loss_head(h, tgt, g, name):
    T, D = h.shape
    tm = min(TM, T)

    def body(h_ref, t_ref, g_ref, o_ref, ob_ref, dg_ref, l_ref):
        x = h_ref[...]
        gg = g_ref[...]
        r = lax.rsqrt(jnp.mean(x * x, axis=-1, keepdims=True) + RMS_EPS)
        xr = x * r
        e = xr * gg - t_ref[...]
        lpart = 0.5 * jnp.sum(jnp.mean(e * e, axis=-1, keepdims=True), axis=0, keepdims=True)
        dy = e * (1.0 / D)
        dyg = dy * gg
        dot = jnp.mean(dyg * x, axis=-1, keepdims=True)
        dh = r * dyg - x * ((r * r * r) * dot)
        o_ref[...] = dh
        ob_ref[...] = dh.astype(BF16)
        part = jnp.sum(dy * xr, axis=0, keepdims=True)
        lrow = jnp.broadcast_to(lpart, (1, LANES))

        @pl.when(pl.program_id(0) == 0)
        def _():
            dg_ref[...] = part
            l_ref[...] = lrow

        @pl.when(pl.program_id(0) != 0)
        def _():
            dg_ref[...] += part
            l_ref[...] += lrow

    blk = pl.BlockSpec((tm, D), lambda i: (i, 0))
    row = pl.BlockSpec((1, D), lambda i: (0, 0))
    return pl.pallas_call(
        body, name=name, grid=(T // tm,),
        in_specs=[blk, blk, row],
        out_specs=[blk, blk, row, pl.BlockSpec((1, LANES), lambda i: (0, 0))],
        out_shape=[jax.ShapeDtypeStruct((T, D), F32), jax.ShapeDtypeStruct((T, D), BF16),
                   jax.ShapeDtypeStruct((1, D), F32), jax.ShapeDtypeStruct((1, LANES), F32)],
        compiler_params=_cp(("arbitrary",)),
    )(h, tgt, g)


def _mm_nn(a, b, name, out_dtype=F32, res=None):
    M, K = a.shape
    tm = min(TM_MM, M)
    if b.ndim == 3:
        nb, _, w = b.shape
        N, tn = nb * w, w
        b_spec = pl.BlockSpec((None, K, w), lambda i, j: (j, 0, 0))
    else:
        N = b.shape[1]
        tn = min(N, 1024)
        b_spec = pl.BlockSpec((K, tn), lambda i, j: (0, j))
    o_spec = pl.BlockSpec((tm, tn), lambda i, j: (i, j))
    in_specs = [pl.BlockSpec((tm, K), lambda i, j: (i, 0)), b_spec]
    args = [a, b]
    if res is not None:
        in_specs.append(o_spec)
        args.append(res)

    def body(*refs):
        a_ref, b_ref = refs[0], refs[1]
        o_ref = refs[-1]
        acc = jnp.dot(a_ref[...], b_ref[...], preferred_element_type=F32)
        if res is not None:
            acc = refs[2][...] + acc
        o_ref[...] = acc.astype(out_dtype)

    return pl.pallas_call(
        body, name=name, grid=(M // tm, N // tn),
        in_specs=in_specs, out_specs=o_spec,
        out_shape=jax.ShapeDtypeStruct((M, N), out_dtype),
        compiler_params=_cp(("parallel", "arbitrary")),
    )(*args)


def _mm_nt(a_list, b, name, out_dtype=F32, add=None):
    M = a_list[0].shape[0]
    tm = min(TM, M)
    na = len(a_list)
    if b.ndim == 3:
        nb, N, w = b.shape
        nk = nb
        a_specs = [pl.BlockSpec((tm, w), lambda i, k: (i, k))]
        b_spec = pl.BlockSpec((None, N, w), lambda i, k: (k, 0, 0))
    else:
        N, ktot = b.shape
        nk = na
        kw = ktot // na
        a_specs = [pl.BlockSpec((tm, kw), lambda i, k: (i, 0)) for _ in a_list]
        b_spec = pl.BlockSpec((N, kw), lambda i, k: (0, k))
    o_spec = pl.BlockSpec((tm, N), lambda i, k: (i, 0))
    in_specs = a_specs + [b_spec]
    args = list(a_list) + [b]
    if add is not None:
        in_specs.append(o_spec)
        args.append(add)

    def body(*refs):
        a_refs = refs[:na]
        b_ref = refs[na]
        o_ref, acc_ref = refs[-2], refs[-1]
        k = pl.program_id(1)

        @pl.when(k == 0)
        def _():
            if add is not None:
                acc_ref[...] = refs[na + 1][...]
            else:
                acc_ref[...] = jnp.zeros_like(acc_ref)

        if na == 1:
            acc_ref[...] += _dot_nt(a_refs[0][...], b_ref[...])
        else:
            for s in range(na):
                @pl.when(k == s)
                def _(s=s):
                    acc_ref[...] += _dot_nt(a_refs[s][...], b_ref[...])

        @pl.when(k == nk - 1)
        def _():
            o_ref[...] = acc_ref[...].astype(out_dtype)

    return pl.pallas_call(
        body, name=name, grid=(M // tm, nk),
        in_specs=in_specs, out_specs=o_spec,
        out_shape=jax.ShapeDtypeStruct((M, N), out_dtype),
        scratch_shapes=[pltpu.VMEM((tm, N), F32)],
        compiler_params=_cp(("parallel", "arbitrary")),
    )(*args)


def _mm_tn(x, y, name, blocked_w=None):
    T, Kx = x.shape
    N = y.shape[1]
    tt = min(TT, T)
    tkx = min(Kx, 1024)
    if blocked_w is not None:
        tn = blocked_w
        out_shape = jax.ShapeDtypeStruct((N // tn, Kx, tn), F32)
        o_spec = pl.BlockSpec((None, tkx, tn), lambda i, j, t: (j, i, 0))
    else:
        tn = min(N, 512)
        out_shape = jax.ShapeDtypeStruct((Kx, N), F32)
        o_spec = pl.BlockSpec((tkx, tn), lambda i, j, t: (i, j))

    def body(x_ref, y_ref, o_ref):
        part = _dot_tn(x_ref[...], y_ref[...])

        @pl.when(pl.program_id(2) == 0)
        def _():
            o_ref[...] = part

        @pl.when(pl.program_id(2) != 0)
        def _():
            o_ref[...] += part

    return pl.pallas_call(
        body, name=name, grid=(Kx // tkx, N // tn, T // tt),
        in_specs=[pl.BlockSpec((tt, tkx), lambda i, j, t: (t, i)),
                  pl.BlockSpec((tt, tn), lambda i, j, t: (t, j))],
        out_specs=o_spec, out_shape=out_shape,
        compiler_params=_cp(("parallel", "parallel", "arbitrary")),
    )(x, y)


def _sgu_pieces(z, lng, lnb, wc, bs_ref):
    E = z.shape[1] // 2
    gd = E // GM_GROUPS
    zu, zv = z[:, :E], z[:, E:]
    u = _gelu(zu)
    v = _gelu(zv)
    mu = jnp.mean(v, axis=-1, keepdims=True)
    xc = v - mu
    rs = lax.rsqrt(jnp.mean(xc * xc, axis=-1, keepdims=True) + LN_EPS)
    xhat = xc * rs
    vln = xhat * lng + lnb
    s = []
    for g in range(GM_GROUPS):
        vg = vln[:, g * gd:(g + 1) * gd].astype(BF16)
        s.append(jnp.dot(wc[g], vg, preferred_element_type=F32) + bs_ref[g])
    return zu, zv, u, xhat, rs, vln, s


def _causal_ws(ws_ref):
    t = lax.broadcasted_iota(jnp.int32, (CHUNK, CHUNK), 0)
    s = lax.broadcasted_iota(jnp.int32, (CHUNK, CHUNK), 1)
    tri = t >= s
    return [jnp.where(tri, ws_ref[g], 0.0).astype(BF16) for g in range(GM_GROUPS)], tri


def _sgu_fwd(z, lng, lnb, ws, bs, name):
    T, E2 = z.shape
    E = E2 // 2
    gd = E // GM_GROUPS
    tm = min(2 * CHUNK, T)

    def body(z_ref, lng_ref, lnb_ref, ws_ref, bs_ref, o_ref):
        wc, _ = _causal_ws(ws_ref)
        for c in range(tm // CHUNK):
            rows = slice(c * CHUNK, (c + 1) * CHUNK)
            _, _, u, _, _, _, s = _sgu_pieces(z_ref[rows, :], lng_ref[...], lnb_ref[...], wc, bs_ref)
            for g in range(GM_GROUPS):
                cols = slice(g * gd, (g + 1) * gd)
                o_ref[rows, cols] = (u[:, cols] * s[g]).astype(BF16)

    full = lambda shape: pl.BlockSpec(shape, lambda i: (0,) * len(shape))
    return pl.pallas_call(
        body, name=name, grid=(T // tm,),
        in_specs=[pl.BlockSpec((tm, E2), lambda i: (i, 0)), full((1, E)), full((1, E)),
                  full((GM_GROUPS, CHUNK, CHUNK)), full((GM_GROUPS, CHUNK, 1))],
        out_specs=pl.BlockSpec((tm, E), lambda i: (i, 0)),
        out_shape=jax.ShapeDtypeStruct((T, E), BF16),
        compiler_params=_cp(("parallel",)),
    )(z, lng, lnb, ws, bs)


def _sgu_bwd(z, dg, lng, lnb, ws, bs, name):
    T, E2 = z.shape
    E = E2 // 2
    gd = E // GM_GROUPS
    tm = min(2 * CHUNK, T)
    nsteps = T // tm

    def body(z_ref, dg_ref, lng_ref, lnb_ref, ws_ref, bs_ref, dz_ref, dlng_ref, dlnb_ref, dws_ref, dbs_ref):
        i = pl.program_id(0)

        @pl.when(i == 0)
        def _():
            dlng_ref[...] = jnp.zeros_like(dlng_ref)
            dlnb_ref[...] = jnp.zeros_like(dlnb_ref)
            dws_ref[...] = jnp.zeros_like(dws_ref)
            dbs_ref[...] = jnp.zeros_like(dbs_ref)

        wc, tri = _causal_ws(ws_ref)
        lng_v = lng_ref[...]
        for c in range(tm // CHUNK):
            rows = slice(c * CHUNK, (c + 1) * CHUNK)
            zu, zv, u, xhat, rs, vln, s = _sgu_pieces(z_ref[rows, :], lng_v, lnb_ref[...], wc, bs_ref)
            dgc = dg_ref[rows, :].astype(F32)
            du, dvln = [], []
            for g in range(GM_GROUPS):
                cols = slice(g * gd, (g + 1) * gd)
                dgg = dgc[:, cols]
                du.append(dgg * s[g])
                ds = dgg * u[:, cols]
                dsb = ds.astype(BF16)
                dws_ref[g] += _dot_nt(dsb, vln[:, cols].astype(BF16))
                dbs_ref[g] += jnp.sum(ds, axis=-1, keepdims=True)
                dvln.append(_dot_tn(wc[g], dsb))
            du = jnp.concatenate(du, axis=1)
            dvln = jnp.concatenate(dvln, axis=1)
            dlng_ref[...] += jnp.sum(dvln * xhat, axis=0, keepdims=True)
            dlnb_ref[...] += jnp.sum(dvln, axis=0, keepdims=True)
            dxh = dvln * lng_v
            m1 = jnp.mean(dxh, axis=-1, keepdims=True)
            m2 = jnp.mean(dxh * xhat, axis=-1, keepdims=True)
            dv = rs * (dxh - m1 - xhat * m2)
            dz_ref[rows, :E] = (du * _gelu_grad(zu)).astype(BF16)
            dz_ref[rows, E:] = (dv * _gelu_grad(zv)).astype(BF16)

        @pl.when(i == nsteps - 1)
        def _():
            for g in range(GM_GROUPS):
                dws_ref[g] = jnp.where(tri, dws_ref[g], 0.0)

    full = lambda shape: pl.BlockSpec(shape, lambda i: (0,) * len(shape))
    return pl.pallas_call(
        body, name=name, grid=(nsteps,),
        in_specs=[pl.BlockSpec((tm, E2), lambda i: (i, 0)), pl.BlockSpec((tm, E), lambda i: (i, 0)),
                  full((1, E)), full((1, E)), full((GM_GROUPS, CHUNK, CHUNK)), full((GM_GROUPS, CHUNK, 1))],
        out_specs=[pl.BlockSpec((tm, E2), lambda i: (i, 0)), full((1, E)), full((1, E)),
                   full((GM_GROUPS, CHUNK, CHUNK)), full((GM_GROUPS, CHUNK, 1))],
        out_shape=[jax.ShapeDtypeStruct((T, E2), BF16), jax.ShapeDtypeStruct((1, E), F32),
                   jax.ShapeDtypeStruct((1, E), F32), jax.ShapeDtypeStruct((GM_GROUPS, CHUNK, CHUNK), F32),
                   jax.ShapeDtypeStruct((GM_GROUPS, CHUNK, 1), F32)],
        compiler_params=_cp(("arbitrary",)),
    )(z, dg, lng, lnb, ws, bs)


HALO = 16


def _conv_taps(a_ext, w_ref, b_ref):
    n = a_ext.shape[0]
    am1 = pltpu.roll(a_ext, 1, 0)
    am2 = pltpu.roll(a_ext, 2, 0)
    del n
    return ((b_ref[...] + am2 * w_ref[0:1, :]) + am1 * w_ref[1:2, :]) + a_ext * w_ref[2:3, :], am1, am2


def _ffn_mid_fwd(a, up, cw, cb, name):
    T, F = a.shape
    tm, tf = min(TM, T), min(TF, F)
    hb = tm // HALO

    def body(a_ref, ap_ref, u_ref, w_ref, b_ref, o_ref):
        i = pl.program_id(1)
        prev = jnp.where(i == 0, 0.0, ap_ref[...])
        ext = jnp.concatenate([prev, a_ref[...]], axis=0)
        conv, _, _ = _conv_taps(ext, w_ref, b_ref)
        conv = conv[HALO:, :]
        o_ref[...] = ((conv * _sigmoid(conv)) * u_ref[...]).astype(BF16)

    main = pl.BlockSpec((tm, tf), lambda f, i: (i, f))
    return pl.pallas_call(
        body, name=name, grid=(F // tf, T // tm),
        in_specs=[main, pl.BlockSpec((HALO, tf), lambda f, i: (jnp.maximum(i * hb - 1, 0), f)), main,
                  pl.BlockSpec((3, tf), lambda f, i: (0, f)), pl.BlockSpec((1, tf), lambda f, i: (0, f))],
        out_specs=main, out_shape=jax.ShapeDtypeStruct((T, F), BF16),
        compiler_params=_cp(("parallel", "parallel")),
    )(a, a, up, cw, cb)


def _ffn_mid_bwd(a, up, dact, cw, cb, name):
    T, F = a.shape
    tm, tf = min(TM, T), min(TF, F)
    hb = tm // HALO
    nt = T // tm
    last_h = T // HALO - 1

    def body(a_ref, ap_ref, an_ref, u_ref, un_ref, d_ref, dn_ref, w_ref, b_ref, da_ref, du_ref, dcw_ref, dcb_ref):
        i = pl.program_id(1)
        prev = jnp.where(i == 0, 0.0, ap_ref[...])
        a_main = a_ref[...]
        a_ext = jnp.concatenate([prev, a_main, an_ref[...]], axis=0)
        conv, am1, am2 = _conv_taps(a_ext, w_ref, b_ref)
        conv = conv[HALO:, :]
        sig = _sigmoid(conv)
        u_ext = jnp.concatenate([u_ref[...], un_ref[...]], axis=0)
        d_ext = jnp.concatenate([d_ref[...], dn_ref[...]], axis=0).astype(F32)
        n = tm + HALO
        row = lax.broadcasted_iota(jnp.int32, (n, 1), 0)
        live = jnp.logical_or(row < tm, i < nt - 1)
        dconv = jnp.where(live, d_ext * u_ext * (sig * (1.0 + conv * (1.0 - sig))), 0.0)
        du_ref[...] = (d_ext[:tm, :] * (conv[:tm, :] * sig[:tm, :])).astype(BF16)
        dp1 = pltpu.roll(dconv, n - 1, 0)[:tm, :]
        dp2 = pltpu.roll(dconv, n - 2, 0)[:tm, :]
        dc = dconv[:tm, :]
        da_ref[...] = ((dc * w_ref[2:3, :] + dp1 * w_ref[1:2, :]) + dp2 * w_ref[0:1, :]).astype(BF16)
        g2 = jnp.sum(dc * a_main, axis=0, keepdims=True)
        g1 = jnp.sum(dc * am1[HALO:HALO + tm, :], axis=0, keepdims=True)
        g0 = jnp.sum(dc * am2[HALO:HALO + tm, :], axis=0, keepdims=True)
        gb = jnp.sum(dc, axis=0, keepdims=True)

        @pl.when(i == 0)
        def _():
            dcw_ref[...] = jnp.zeros_like(dcw_ref)
            dcb_ref[...] = jnp.zeros_like(dcb_ref)

        dcw_ref[0:1, :] += g0
        dcw_ref[1:2, :] += g1
        dcw_ref[2:3, :] += g2
        dcb_ref[...] += gb

    main = pl.BlockSpec((tm, tf), lambda f, i: (i, f))
    prev = pl.BlockSpec((HALO, tf), lambda f, i: (jnp.maximum(i * hb - 1, 0), f))
    nxt = pl.BlockSpec((HALO, tf), lambda f, i: (jnp.minimum((i + 1) * hb, last_h), f))
    return pl.pallas_call(
        body, name=name, grid=(F // tf, nt),
        in_specs=[main, prev, nxt, main, nxt, main, nxt,
                  pl.BlockSpec((3, tf), lambda f, i: (0, f)), pl.BlockSpec((1, tf), lambda f, i: (0, f))],
        out_specs=[main, main, pl.BlockSpec((3, tf), lambda f, i: (0, f)), pl.BlockSpec((1, tf), lambda f, i: (0, f))],
        out_shape=[jax.ShapeDtypeStruct((T, F), BF16), jax.ShapeDtypeStruct((T, F), BF16),
                   jax.ShapeDtypeStruct((3, F), F32), jax.ShapeDtypeStruct((1, F), F32)],
        compiler_params=_cp(("parallel", "arbitrary")),
    )(a, a, a, up, up, dact, dact, cw, cb)


def _split3(x):
    hi = x.astype(BF16)
    r1 = x - hi.astype(F32)
    mid = r1.astype(BF16)
    lo = (r1 - mid.astype(F32)).astype(BF16)
    return hi, mid, lo


def _tri_ones(n, upper):
    r = lax.broadcasted_iota(jnp.int32, (n, n), 0)
    c = lax.broadcasted_iota(jnp.int32, (n, n), 1)
    return jnp.where((r <= c) if upper else (r >= c), 1.0, 0.0).astype(BF16)


def _gate_scan(f, bf, name):
    T = f.shape[0]
    tm = min(256, T)

    def body(f_ref, b_ref, cp_ref, sn_ref, carry_ref):
        i = pl.program_id(0)

        @pl.when(i == 0)
        def _():
            carry_ref[...] = jnp.zeros_like(carry_ref)

        x = f_ref[...] + b_ref[...]
        e = jnp.exp(-jnp.abs(x))
        logf = jnp.minimum(x, 0.0) - jnp.log(1.0 + e)
        sn_ref[...] = jnp.where(x >= 0.0, e / (1.0 + e), 1.0 / (1.0 + e))
        tri = _tri_ones(tm, upper=False)
        c = carry_ref[...]
        for piece in _split3(logf):
            c = c + jnp.dot(tri, piece, preferred_element_type=F32)
        carry_ref[...] += jnp.sum(logf, axis=0, keepdims=True)
        hi, mid, lo = _split3(c)
        cp_ref[:, 0:LANES] = hi
        cp_ref[:, LANES:2 * LANES] = mid
        cp_ref[:, 2 * LANES:3 * LANES] = lo

    return pl.pallas_call(
        body, name=name, grid=(T // tm,),
        in_specs=[pl.BlockSpec((tm, LANES), lambda i: (i, 0)), pl.BlockSpec((1, LANES), lambda i: (0, 0))],
        out_specs=[pl.BlockSpec((tm, 3 * LANES), lambda i: (i, 0)), pl.BlockSpec((tm, LANES), lambda i: (i, 0))],
        out_shape=[jax.ShapeDtypeStruct((T, 3 * LANES), BF16), jax.ShapeDtypeStruct((T, LANES), F32)],
        scratch_shapes=[pltpu.VMEM((1, LANES), F32)],
        compiler_params=_cp(("arbitrary",)),
    )(f, bf)


def _gate_scan_bwd(dcq, dck, sneg, name):
    T = dcq.shape[0]
    tm = min(256, T)
    n = T // tm

    def body(dcq_ref, dck_ref, sn_ref, df_ref, db_ref, carry_ref):
        i = pl.program_id(0)

        @pl.when(i == 0)
        def _():
            carry_ref[...] = jnp.zeros_like(carry_ref)
            db_ref[...] = jnp.zeros_like(db_ref)

        tri = _tri_ones(tm, upper=True)
        dcb = dcq_ref[...] - dck_ref[...]
        acc = carry_ref[...]
        for piece in _split3(dcb):
            acc = acc + jnp.dot(tri, piece, preferred_element_type=F32)
        carry_ref[...] += jnp.sum(dcb, axis=0, keepdims=True)
        df = acc * sn_ref[...]
        df_ref[...] = df.astype(BF16)
        db_ref[...] += jnp.sum(df, axis=0, keepdims=True)

    rev = pl.BlockSpec((tm, LANES), lambda i: (n - 1 - i, 0))
    return pl.pallas_call(
        body, name=name, grid=(n,),
        in_specs=[rev, rev, rev],
        out_specs=[rev, pl.BlockSpec((1, LANES), lambda i: (0, 0))],
        out_shape=[jax.ShapeDtypeStruct((T, LANES), BF16), jax.ShapeDtypeStruct((1, LANES), F32)],
        scratch_shapes=[pltpu.VMEM((1, LANES), F32)],
        compiler_params=_cp(("arbitrary",)),
    )(dcq, dck, sneg)


def _qk_proj(hn, w_pad, cp, sel, const, scale, name):
    T, D = hn.shape
    H = w_pad.shape[1] // LANES
    tm = min(TM_MM, T)

    def body(a_ref, w_ref, cp_ref, sel_ref, c_ref, o_ref):
        acc = jnp.dot(a_ref[...], w_ref[...], preferred_element_type=F32)
        if scale != 1.0:
            acc = acc * scale
        acc = acc + jnp.dot(cp_ref[...], sel_ref[...], preferred_element_type=F32) + c_ref[...]
        o_ref[0] = acc[:, :LANES].astype(BF16)
        o_ref[1] = acc[:, LANES:].astype(BF16)

    return pl.pallas_call(
        body, name=name, grid=(T // tm, H // 2),
        in_specs=[pl.BlockSpec((tm, D), lambda i, p: (i, 0)), pl.BlockSpec((D, 2 * LANES), lambda i, p: (0, p)),
                  pl.BlockSpec((tm, 3 * LANES), lambda i, p: (i, 0)),
                  pl.BlockSpec((None, 3 * LANES, 2 * LANES), lambda i, p: (p, 0, 0)),
                  pl.BlockSpec((None, 1, 2 * LANES), lambda i, p: (p, 0, 0))],
        out_specs=pl.BlockSpec((2, tm, LANES), lambda i, p: (p, i, 0)),
        out_shape=jax.ShapeDtypeStruct((H, T, LANES), BF16),
        compiler_params=_cp(("parallel", "arbitrary")),
    )(hn, w_pad, cp, sel, const)


def _lane_lo():
    return lax.broadcasted_iota(jnp.int32, (1, LANES), 1) < HEAD_DIM


def _attn_fwd(qp, kp, v, name):
    H, T, _ = qp.shape
    tq = min(TQ, T)
    nrep = tq // LANES

    def body(q_ref, k_ref, v_ref, o_ref, o32_ref, lse_ref, m_sc, l_sc, acc_sc):
        i = pl.program_id(1)
        m_sc[...] = jnp.full(m_sc.shape, NEG, F32)
        l_sc[...] = jnp.zeros_like(l_sc)
        acc_sc[...] = jnp.zeros_like(acc_sc)

        def step(j, masked):
            off = pl.multiple_of(j * tq, tq)
            vblk = v_ref[pl.ds(off, tq), :]
            for h in range(2):
                s = _dot_nt(q_ref[h], k_ref[h, pl.ds(off, tq), :])
                if masked:
                    r = lax.broadcasted_iota(jnp.int32, (tq, tq), 0)
                    c = lax.broadcasted_iota(jnp.int32, (tq, tq), 1)
                    s = jnp.where(r >= c, s, NEG)
                m_prev = m_sc[h]
                m_new = jnp.maximum(m_prev, jnp.max(s, axis=-1, keepdims=True))
                alpha = jnp.exp(m_prev - m_new)
                p = jnp.exp(s - jnp.concatenate([m_new] * nrep, axis=1))
                l_sc[h] = alpha * l_sc[h] + jnp.sum(p, axis=-1, keepdims=True)
                acc_sc[h] = alpha * acc_sc[h] + jnp.dot(p.astype(BF16), vblk, preferred_element_type=F32)
                m_sc[h] = m_new

        def loop_body(j, carry):
            step(j, False)
            return carry

        lax.fori_loop(0, i, loop_body, 0)
        step(i, True)
        lo = _lane_lo()
        o = jnp.where(lo, acc_sc[0] / l_sc[0], acc_sc[1] / l_sc[1])
        o_ref[...] = o.astype(BF16)
        o32_ref[...] = o
        lse_ref[...] = jnp.where(lo, m_sc[0] + jnp.log(l_sc[0]), m_sc[1] + jnp.log(l_sc[1]))

    oblk = pl.BlockSpec((tq, LANES), lambda p, i: (i, p))
    return pl.pallas_call(
        body, name=name, grid=(H // 2, T // tq),
        in_specs=[pl.BlockSpec((2, tq, LANES), lambda p, i: (p, i, 0)),
                  pl.BlockSpec((2, T, LANES), lambda p, i: (p, 0, 0)),
                  pl.BlockSpec((T, LANES), lambda p, i: (0, p))],
        out_specs=[oblk, oblk, pl.BlockSpec((None, tq, LANES), lambda p, i: (p, i, 0))],
        out_shape=[jax.ShapeDtypeStruct((T, H * HEAD_DIM), BF16), jax.ShapeDtypeStruct((T, H * HEAD_DIM), F32),
                   jax.ShapeDtypeStruct((H // 2, T, LANES), F32)],
        scratch_shapes=[pltpu.VMEM((2, tq, LANES), F32), pltpu.VMEM((2, tq, LANES), F32),
                        pltpu.VMEM((2, tq, LANES), F32)],
        compiler_params=_cp(("parallel", "arbitrary")),
    )(qp, kp, v)


def _attn_bwd(qp, kp, v, o, do, lse, scale, name):
    H, T, _ = qp.shape
    tq = min(TQ, T)
    nq = T // tq
    nrep = tq // LANES

    def body(q_ref, k_ref, v_ref, o_ref, do_ref, lse_ref, dq_ref, dk_ref, dv_ref, dqe_ref, dke_ref, dk_sc, dv_sc, dq_sc):
        i = pl.program_id(1)

        @pl.when(i == 0)
        def _():
            dk_sc[...] = jnp.zeros_like(dk_sc)
            dv_sc[...] = jnp.zeros_like(dv_sc)

        dq_sc[...] = jnp.zeros_like(dq_sc)

        lo = _lane_lo()
        dob = do_ref[...]
        dof = dob.astype(F32)
        prod = dof * o_ref[...].astype(F32)
        lse2 = lse_ref[...]
        lse2_sw = pltpu.roll(lse2, HEAD_DIM, 1)
        zero = jnp.zeros_like(dob)
        do_h = [jnp.where(lo, dob, zero), jnp.where(lo, zero, dob)]
        delta = [jnp.sum(jnp.where(lo, prod, 0.0), axis=-1, keepdims=True),
                 jnp.sum(jnp.where(lo, 0.0, prod), axis=-1, keepdims=True)]
        lse_h = [jnp.concatenate([jnp.where(lo, lse2, lse2_sw)] * nrep, axis=1),
                 jnp.concatenate([jnp.where(lo, lse2_sw, lse2)] * nrep, axis=1)]
        qs = [q_ref[0], q_ref[1]]

        def step(j, masked):
            off = pl.multiple_of(j * tq, tq)
            vblk = v_ref[pl.ds(off, tq), :]
            dv_add = None
            for h in range(2):
                kblk = k_ref[h, pl.ds(off, tq), :]
                s = _dot_nt(qs[h], kblk)
                p = jnp.exp(s - lse_h[h])
                if masked:
                    r = lax.broadcasted_iota(jnp.int32, (tq, tq), 0)
                    c = lax.broadcasted_iota(jnp.int32, (tq, tq), 1)
                    p = jnp.where(r >= c, p, 0.0)
                dp = _dot_nt(do_h[h], vblk)
                ds = p * (dp - delta[h])
                dsb = ds.astype(BF16)
                dq_sc[h] += jnp.dot(dsb, kblk, preferred_element_type=F32)
                dk_sc[h, pl.ds(off, tq), :] += _dot_tn(dsb, qs[h])
                pv = _dot_tn(p.astype(BF16), do_h[h])
                dv_add = pv if dv_add is None else dv_add + pv
            dv_sc[pl.ds(off, tq), :] += dv_add

        def loop_body(j, carry):
            step(j, False)
            return carry

        lax.fori_loop(0, i, loop_body, 0)
        step(i, True)
        dq0, dq1 = dq_sc[0], dq_sc[1]
        dq_ref[...] = (jnp.where(lo, dq0, pltpu.roll(dq1, HEAD_DIM, 1)) * scale).astype(BF16)
        dqe_ref[...] = jnp.where(lo, pltpu.roll(dq0, HEAD_DIM, 1), dq1)

        @pl.when(i == nq - 1)
        def _():
            dk0, dk1 = dk_sc[0], dk_sc[1]
            dk_ref[...] = jnp.where(lo, dk0, pltpu.roll(dk1, HEAD_DIM, 1)).astype(BF16)
            dke_ref[...] = jnp.where(lo, pltpu.roll(dk0, HEAD_DIM, 1), dk1)
            dv_ref[...] = dv_sc[...].astype(BF16)

    qblk = pl.BlockSpec((tq, LANES), lambda p, i: (i, p))
    pair = pl.BlockSpec((T, LANES), lambda p, i: (0, p))
    tok16 = jax.ShapeDtypeStruct((T, H * HEAD_DIM), BF16)
    tok32 = jax.ShapeDtypeStruct((T, H * HEAD_DIM), F32)
    return pl.pallas_call(
        body, name=name, grid=(H // 2, nq),
        in_specs=[pl.BlockSpec((2, tq, LANES), lambda p, i: (p, i, 0)),
                  pl.BlockSpec((2, T, LANES), lambda p, i: (p, 0, 0)),
                  pair, qblk, qblk,
                  pl.BlockSpec((None, tq, LANES), lambda p, i: (p, i, 0))],
        out_specs=[qblk, pair, pair, qblk, pair],
        out_shape=[tok16, tok16, tok16, tok32, tok32],
        scratch_shapes=[pltpu.VMEM((2, T, LANES), F32), pltpu.VMEM((T, LANES), F32),
                        pltpu.VMEM((2, tq, LANES), F32)],
        compiler_params=_cp(("parallel", "arbitrary")),
    )(qp, kp, v, o, do, lse)


def _mesh_pos():
    return lax.axis_index("x"), lax.axis_index("y"), lax.axis_index("c")


def _all_gather(arrs, name):
    n = len(arrs)

    def body(*refs):
        ins, outs = refs[:n], refs[n:2 * n]
        send_sems, recv_sems, local_sems = refs[2 * n:]
        x, y, c = _mesh_pos()
        me, sib = (x, y, c), (x, y, 1 - c)
        chips = [(1 - x, y), (x, 1 - y), (1 - x, 1 - y)]

        def slot(px, py, pc):
            return 4 * px + 2 * py + pc

        def copy(a, k, block, to, src=None):
            dst = outs[a].at[slot(*block)]
            return pltpu.make_async_remote_copy(
                src_ref=dst if src is None else src, dst_ref=dst,
                send_sem=send_sems.at[a, k], recv_sem=recv_sems.at[a, k], device_id=to, device_id_type=MESH)

        mine = [pltpu.make_async_copy(ins[a], outs[a].at[slot(*me)], local_sems.at[a]) for a in range(n)]
        for cp in mine:
            cp.start()
        first = []
        for a in range(n):
            first.append(copy(a, 0, me, sib, src=ins[a]))
            first += [copy(a, 1 + j, me, (*chip, c), src=ins[a]) for j, chip in enumerate(chips)]
        for cp in first:
            cp.start()
        passed = []
        for j, chip in enumerate(chips):
            for a in range(n):
                copy(a, 1 + j, (*chip, c), me).wait_recv()
                fwd = copy(a, 4 + j, (*chip, c), sib)
                fwd.start()
                passed.append(fwd)
        for a in range(n):
            copy(a, 0, sib, me).wait_recv()
            for j, chip in enumerate(chips):
                copy(a, 4 + j, (*chip, 1 - c), me).wait_recv()
        for cp in first + passed:
            cp.wait_send()
        for cp in mine:
            cp.wait()

    any_spec = pl.BlockSpec(memory_space=pl.ANY)
    return pl.pallas_call(
        body, name=name,
        in_specs=[any_spec] * n, out_specs=[any_spec] * n,
        out_shape=[jax.ShapeDtypeStruct((N_DEV,) + a.shape, a.dtype) for a in arrs],
        scratch_shapes=[pltpu.SemaphoreType.DMA((n, 7)), pltpu.SemaphoreType.DMA((n, 7)),
                        pltpu.SemaphoreType.DMA((n,))],
    )(*arrs)


def _pair_exchange(g, name):
    def body(g_ref, o_ref, send_sems, recv_sems):
        x, y, c = _mesh_pos()
        sib = (x, y, 1 - c)
        copies = []
        for j in range(4):
            copies.append(pltpu.make_async_remote_copy(
                src_ref=g_ref.at[2 * j + (1 - c)], dst_ref=o_ref.at[j],
                send_sem=send_sems.at[j], recv_sem=recv_sems.at[j], device_id=sib, device_id_type=MESH))
        for cp in copies:
            cp.start()
        for cp in copies:
            cp.wait_recv()
        for cp in copies:
            cp.wait_send()

    any_spec = pl.BlockSpec(memory_space=pl.ANY)
    return pl.pallas_call(
        body, name=name, in_specs=[any_spec], out_specs=any_spec,
        out_shape=jax.ShapeDtypeStruct((4,) + g.shape[1:], g.dtype),
        scratch_shapes=[pltpu.SemaphoreType.DMA((4,)), pltpu.SemaphoreType.DMA((4,))],
    )(g)


def _chip_exchange(part, name):
    def body(p_ref, o_ref, send_sems, recv_sems):
        x, y, c = _mesh_pos()
        chips = [(1 - x, y), (x, 1 - y), (1 - x, 1 - y)]
        copies = []
        for k, (px, py) in enumerate(chips):
            copies.append(pltpu.make_async_remote_copy(
                src_ref=p_ref.at[2 * px + py], dst_ref=o_ref.at[k],
                send_sem=send_sems.at[k], recv_sem=recv_sems.at[k], device_id=(px, py, c), device_id_type=MESH))
        for cp in copies:
            cp.start()
        for cp in copies:
            cp.wait_recv()
        for cp in copies:
            cp.wait_send()

    any_spec = pl.BlockSpec(memory_space=pl.ANY)
    return pl.pallas_call(
        body, name=name, in_specs=[any_spec], out_specs=any_spec,
        out_shape=jax.ShapeDtypeStruct((3,) + part.shape[1:], part.dtype),
        scratch_shapes=[pltpu.SemaphoreType.DMA((3,)), pltpu.SemaphoreType.DMA((3,))],
    )(part)


def _row_block(R, C):
    best = None
    for d in range(16, R + 1, 16):
        if R % d == 0 and d * C <= 256 * 1024:
            best = d
    return best if best is not None else R


def _pair_add(g, recv, cidx, name):
    _, R, C = g.shape
    tr = _row_block(R, C)

    def body(c_ref, g_ref, r_ref, o_ref):
        del c_ref
        o_ref[...] = (g_ref[...].astype(F32) + r_ref[...].astype(F32)).astype(BF16)

    grid_spec = pltpu.PrefetchScalarGridSpec(
        num_scalar_prefetch=1, grid=(4, R // tr),
        in_specs=[pl.BlockSpec((None, tr, C), lambda j, i, c: (2 * j + c[0], i, 0)),
                  pl.BlockSpec((None, tr, C), lambda j, i, c: (j, i, 0))],
        out_specs=pl.BlockSpec((None, tr, C), lambda j, i, c: (j, i, 0)))
    return pl.pallas_call(
        body, name=name, grid_spec=grid_spec,
        out_shape=jax.ShapeDtypeStruct((4, R, C), BF16),
        compiler_params=_cp(("parallel", "parallel")),
    )(cidx, g, recv)


def _adamw_math(w, g, m, v):
    m = ADAM_B1 * m + (1.0 - ADAM_B1) * g
    v = ADAM_B2 * v + (1.0 - ADAM_B2) * (g * g)
    m_hat = m / (1.0 - ADAM_B1 ** ADAM_STEP)
    v_hat = v / (1.0 - ADAM_B2 ** ADAM_STEP)
    delta = -ADAM_LR * (m_hat / (jnp.sqrt(v_hat) + ADAM_EPS) + ADAM_WD * w)
    return delta, m, v


def _sum_adamw(parts, w, m, v, name, sel=None):
    R, C = w.shape
    tr = _row_block(R, C)
    specs, args = [], []
    for arr, idxs in parts:
        for idx in idxs:
            if idx < 0:
                specs.append(pl.BlockSpec((None, tr, C), lambda i, s: (s[0], i, 0)))
            else:
                specs.append(pl.BlockSpec((None, tr, C), lambda i, s, idx=idx: (idx, i, 0)))
            args.append(arr)
    npart = len(args)
    blk = pl.BlockSpec((tr, C), lambda i, s: (i, 0))

    def body(s_ref, *refs):
        del s_ref
        g = refs[0][...].astype(F32)
        for r in refs[1:npart]:
            g = g + r[...].astype(F32)
        w_ref, m_ref, v_ref, g_out, d_out, m_out, v_out = refs[npart:]
        delta, mm, vv = _adamw_math(w_ref[...], g, m_ref[...], v_ref[...])
        g_out[...] = g
        d_out[...] = delta
        m_out[...] = mm
        v_out[...] = vv

    grid_spec = pltpu.PrefetchScalarGridSpec(
        num_scalar_prefetch=1, grid=(R // tr,),
        in_specs=specs + [blk, blk, blk], out_specs=[blk] * 4)
    if sel is None:
        sel = jnp.zeros((1,), jnp.int32)
    return pl.pallas_call(
        body, name=name, grid_spec=grid_spec,
        out_shape=[jax.ShapeDtypeStruct((R, C), F32)] * 4,
        compiler_params=_cp(("parallel",)),
    )(sel, *args, w, m, v)


def _rows(a, c):
    return a.reshape(-1, c)


def _pad_rows(a, r):
    return jnp.pad(a, ((0, r - a.shape[0]), (0, 0))) if a.shape[0] != r else a


def _pad16(n):
    return -(-n // 16) * 16


def _gate_tables():
    hp = N_HEADS // 2
    sel_q = np.zeros((hp, 3 * LANES, 2 * LANES), np.float32)
    sel_k = np.zeros((hp, 3 * LANES, 2 * LANES), np.float32)
    const_q = np.zeros((hp, 1, 2 * LANES), np.float32)
    const_k = np.zeros((hp, 1, 2 * LANES), np.float32)
    for p in range(hp):
        for hh in range(2):
            h = 2 * p + hh
            base = hh * LANES + HEAD_DIM
            for piece in range(3):
                sel_q[p, piece * LANES + h, base + piece] = 1.0
                sel_k[p, piece * LANES + h, base + 3 + piece] = -1.0
            const_k[p, 0, base:base + 3] = 1.0
            const_q[p, 0, base + 3:base + 6] = 1.0
    as_bf = lambda t: jnp.asarray(t, BF16)
    return as_bf(sel_q), as_bf(sel_k), jnp.asarray(const_q), jnp.asarray(const_k)


def _pad_heads(w):
    d = w.shape[0]
    w3 = w.reshape(d, N_HEADS, HEAD_DIM)
    return jnp.pad(w3, ((0, 0), (0, 0), (0, LANES - HEAD_DIM))).reshape(d, N_HEADS * LANES)


def kernel(x, mix_norm_g, ffn_norm_g, gm_w_in, gm_ln_g, gm_ln_b, gm_w_s, gm_b_s, gm_w_out, fox_w_qkvf, fox_b_f, fox_w_o, ffn_w_gate, ffn_w_up, ffn_conv_w, ffn_conv_b, ffn_w_down, final_norm_g, loss_target, m_mix_norm_g, m_ffn_norm_g, m_gm_w_in, m_gm_ln_g, m_gm_ln_b, m_gm_w_s, m_gm_b_s, m_gm_w_out, m_fox_w_qkvf, m_fox_b_f, m_fox_w_o, m_ffn_w_gate, m_ffn_w_up, m_ffn_conv_w, m_ffn_conv_b, m_ffn_w_down, m_final_norm_g, v_mix_norm_g, v_ffn_norm_g, v_gm_w_in, v_gm_ln_g, v_gm_ln_b, v_gm_w_s, v_gm_b_s, v_gm_w_out, v_fox_w_qkvf, v_fox_b_f, v_fox_w_o, v_ffn_w_gate, v_ffn_w_up, v_ffn_conv_w, v_ffn_conv_b, v_ffn_w_down, v_final_norm_g):
    T, D = x.shape[1], x.shape[2]
    E = gm_ln_g.shape[1]
    FF = ffn_conv_b.shape[1]
    NQKVF = 3 * D + N_HEADS
    xi, yi, ci = _mesh_pos()
    me = 4 * xi + 2 * yi + ci
    h0 = x.reshape(T, D)
    tgt = loss_target.reshape(T, D)

    big = [("gm_w_in", gm_w_in, m_gm_w_in, v_gm_w_in), ("gm_w_out", gm_w_out, m_gm_w_out, v_gm_w_out),
           ("fox_w_qkvf", fox_w_qkvf, m_fox_w_qkvf, v_fox_w_qkvf), ("fox_w_o", fox_w_o, m_fox_w_o, v_fox_w_o),
           ("ffn_w_gate", ffn_w_gate, m_ffn_w_gate, v_ffn_w_gate), ("ffn_w_up", ffn_w_up, m_ffn_w_up, v_ffn_w_up),
           ("ffn_w_down", ffn_w_down, m_ffn_w_down, v_ffn_w_down)]
    big_rows = [p[1].size // D for p in big]
    big_pad = [_pad16(r) for r in big_rows]
    big_off = np.concatenate([[0], np.cumsum(big_pad)]).tolist()
    R = -(-big_off[-1] // 80) * 80

    def flat_big(arrs, dtype):
        parts = [_pad_rows(_rows(a, D).astype(dtype), pr) for a, pr in zip(arrs, big_pad)]
        parts.append(jnp.zeros((R - big_off[-1], D), dtype))
        return jnp.concatenate(parts, axis=0)

    def unflat_big(flat, k):
        return flat[..., big_off[k]:big_off[k] + big_rows[k], :]

    w_flat16 = flat_big([p[1] for p in big], BF16)
    cw_rows = _pad_rows(_rows(ffn_conv_w.astype(F32), LANES), 16)
    wg, cwg = _all_gather([w_flat16, cw_rows], "ag_weights")

    nl = ffn_w_gate.shape[0]
    w_in_g = unflat_big(wg, 0).reshape(N_DEV, D, 2 * E // N_DEV)
    w_out_g = unflat_big(wg, 1).reshape(E, D)
    w_qkvf = jnp.transpose(unflat_big(wg, 2).reshape(N_DEV, D, NQKVF // N_DEV), (1, 0, 2)).reshape(D, NQKVF)
    w_o_g = unflat_big(wg, 3).reshape(D, D)
    gate_g = unflat_big(wg, 4).reshape(N_DEV, nl, D, FF // N_DEV)
    up_g = unflat_big(wg, 5).reshape(N_DEV, nl, D, FF // N_DEV)
    down_g = unflat_big(wg, 6).reshape(N_DEV, nl, FF // N_DEV, D)
    n_cw_rows = ffn_conv_w.size // LANES
    conv_w_full = jnp.transpose(cwg[:, :n_cw_rows].reshape(N_DEV, nl, 3, FF // N_DEV), (1, 2, 0, 3)).reshape(nl, 3, FF)

    def ffn_weights(l):
        return gate_g[:, l], up_g[:, l], down_g[:, l].reshape(FF, D)

    saved = {}

    def ffn_fwd(l, h_in):
        wgl, wul, wdl = ffn_weights(l)
        hn = _rms_fwd(h_in, ffn_norm_g[l:l + 1], f"ffn{l}_norm")
        a = _mm_nn(hn, wgl, f"ffn{l}_gate")
        up = _mm_nn(hn, wul, f"ffn{l}_up")
        act = _ffn_mid_fwd(a, up, conv_w_full[l], ffn_conv_b[l:l + 1], f"ffn{l}_mid")
        h_out = _mm_nn(act, wdl, f"ffn{l}_down", res=h_in)
        saved[f"ffn{l}"] = (h_in, hn, a, up, act)
        return h_out

    bs_col = gm_b_s[0].reshape(GM_GROUPS, CHUNK, 1)
    hn0 = _rms_fwd(h0, mix_norm_g[0:1], "mix0_norm")
    z = _mm_nn(hn0, w_in_g, "gm_in")
    gu = _sgu_fwd(z, gm_ln_g, gm_ln_b, gm_w_s[0], bs_col, "gm_sgu")
    h1 = _mm_nn(gu, w_out_g, "gm_out", res=h0)
    h2 = ffn_fwd(0, h1)

    w_q, w_k, w_v = w_qkvf[:, :D], w_qkvf[:, D:2 * D], w_qkvf[:, 2 * D:3 * D]
    w_f = jnp.pad(w_qkvf[:, 3 * D:], ((0, 0), (0, LANES - N_HEADS)))
    bf_row = jnp.pad(fox_b_f, ((0, 0), (0, LANES - N_HEADS)))
    sel_q, sel_k, const_q, const_k = _gate_tables()
    scale = HEAD_DIM ** -0.5
    hn2 = _rms_fwd(h2, mix_norm_g[1:2], "mix1_norm")
    f_logit = _mm_nn(hn2, w_f, "fox_f")
    cp, sneg = _gate_scan(f_logit, bf_row, "fox_scan")
    qp = _qk_proj(hn2, _pad_heads(w_q), cp, sel_q, const_q, scale, "fox_q")
    kp = _qk_proj(hn2, _pad_heads(w_k), cp, sel_k, const_k, 1.0, "fox_k")
    vv = _mm_nn(hn2, w_v, "fox_v", out_dtype=BF16)
    o, o32, lse = _attn_fwd(qp, kp, vv, "fox_attn")
    h3 = _mm_nn(o, w_o_g, "fox_o", res=h2)
    h4 = ffn_fwd(1, h3)

    dh, dh16, d_final, loss_row = _loss_head(h4, tgt, final_norm_g.reshape(1, D), "loss_head")
    loss = lax.psum(loss_row[0, 0], ("x", "y", "c"))

    def ffn_bwd(l, dh, dh16):
        wgl, wul, wdl = ffn_weights(l)
        h_in, hn, a, up, act = saved[f"ffn{l}"]
        dact = _mm_nt([dh16], wdl, f"ffn{l}_dact", out_dtype=BF16)
        d_wd = _mm_tn(act, dh16, f"ffn{l}_dwd")
        da, dup, d_cw, d_cb = _ffn_mid_bwd(a, up, dact, conv_w_full[l], ffn_conv_b[l:l + 1], f"ffn{l}_dmid")
        dhn = _mm_nt([da], wgl, f"ffn{l}_dhn_g")
        dhn = _mm_nt([dup], wul, f"ffn{l}_dhn_u", add=dhn)
        d_wg = _mm_tn(hn, da, f"ffn{l}_dwg", blocked_w=FF // N_DEV)
        d_wu = _mm_tn(hn, dup, f"ffn{l}_dwu", blocked_w=FF // N_DEV)
        dh_in, dh_in16, d_norm = _rms_bwd(dhn, h_in, ffn_norm_g[l:l + 1], dh, f"ffn{l}_dnorm")
        return dh_in, dh_in16, dict(wd=d_wd, wg=d_wg, wu=d_wu, cw=d_cw, cb=d_cb, norm=d_norm)

    dh, dh16, g_ffn1 = ffn_bwd(1, dh, dh16)

    do = _mm_nt([dh16], w_o_g, "fox_do", out_dtype=BF16)
    d_wo = _mm_tn(o, dh16, "fox_dwo")
    dq, dk, dv, dqe, dke = _attn_bwd(qp, kp, vv, o32, do, lse, scale, "fox_dattn")
    gate_lane = lambda e, lane: jnp.pad(e.reshape(T, N_HEADS, HEAD_DIM)[:, :, lane], ((0, 0), (0, LANES - N_HEADS)))
    df, d_bf = _gate_scan_bwd(gate_lane(dqe, 0), gate_lane(dke, 3), sneg, "fox_dscan")
    dhn = _mm_nt([df], w_f, "fox_dhn_f")
    dhn = _mm_nt([dq, dk, dv], w_qkvf[:, :3 * D], "fox_dhn_qkv", add=dhn)
    d_wq = _mm_tn(hn2, dq, "fox_dwq")
    d_wk = _mm_tn(hn2, dk, "fox_dwk")
    d_wv = _mm_tn(hn2, dv, "fox_dwv")
    d_wf = _mm_tn(hn2, df, "fox_dwf")
    d_wqkvf = jnp.concatenate([d_wq, d_wk, d_wv, d_wf[:, :N_HEADS]], axis=1)
    dh, dh16, d_mix1 = _rms_bwd(dhn, h2, mix_norm_g[1:2], dh, "mix1_dnorm")

    dh, dh16, g_ffn0 = ffn_bwd(0, dh, dh16)

    dgu = _mm_nt([dh16], w_out_g, "gm_dgu", out_dtype=BF16)
    d_wout = _mm_tn(gu, dh16, "gm_dwout")
    dz, d_lng, d_lnb, d_ws, d_bs = _sgu_bwd(z, dgu, gm_ln_g, gm_ln_b, gm_w_s[0], bs_col, "gm_dsgu")
    dhn = _mm_nt([dz], w_in_g, "gm_dhn")
    d_win = _mm_tn(hn0, dz, "gm_dwin", blocked_w=2 * E // N_DEV)
    dx, _, d_mix0 = _rms_bwd(dhn, h0, mix_norm_g[0:1], dh, "mix0_dnorm")

    g_big = [d_win,
             d_wout.reshape(N_DEV, E // N_DEV, D),
             jnp.transpose(d_wqkvf.reshape(D, N_DEV, NQKVF // N_DEV), (1, 0, 2)),
             d_wo.reshape(N_DEV, D // N_DEV, D),
             jnp.stack([g_ffn0["wg"], g_ffn1["wg"]], axis=1),
             jnp.stack([g_ffn0["wu"], g_ffn1["wu"]], axis=1),
             jnp.stack([g_ffn0["wd"].reshape(N_DEV, FF // N_DEV, D), g_ffn1["wd"].reshape(N_DEV, FF // N_DEV, D)], axis=1)]
    parts = [jnp.pad(a.reshape(N_DEV, -1, D).astype(BF16), ((0, 0), (0, pr - r), (0, 0)))
             for a, r, pr in zip(g_big, big_rows, big_pad)]
    parts.append(jnp.zeros((N_DEV, R - big_off[-1], D), BF16))
    g_flat = jnp.concatenate(parts, axis=1)
    cidx = ci.astype(jnp.int32).reshape(1)
    chip_idx = (2 * xi + yi).astype(jnp.int32).reshape(1)
    recv1 = _pair_exchange(g_flat, "rs_pair")
    part = _pair_add(g_flat, recv1, cidx, "rs_pair_add")
    recv2 = _chip_exchange(part, "rs_chip")
    w_flat = flat_big([p[1] for p in big], F32)
    m_flat = flat_big([p[2] for p in big], F32)
    v_flat = flat_big([p[3] for p in big], F32)
    gb, db, mb, vb = _sum_adamw([(part, [-1]), (recv2, [0, 1, 2])], w_flat, m_flat, v_flat, "adamw_big", sel=chip_idx)

    small = [("mix_norm_g", mix_norm_g, m_mix_norm_g, v_mix_norm_g, jnp.concatenate([d_mix0, d_mix1], axis=0)),
             ("ffn_norm_g", ffn_norm_g, m_ffn_norm_g, v_ffn_norm_g, jnp.concatenate([g_ffn0["norm"], g_ffn1["norm"]], axis=0)),
             ("gm_ln_g", gm_ln_g, m_gm_ln_g, v_gm_ln_g, d_lng),
             ("gm_ln_b", gm_ln_b, m_gm_ln_b, v_gm_ln_b, d_lnb),
             ("gm_w_s", gm_w_s, m_gm_w_s, v_gm_w_s, d_ws),
             ("gm_b_s", gm_b_s, m_gm_b_s, v_gm_b_s, d_bs),
             ("fox_b_f", fox_b_f, m_fox_b_f, v_fox_b_f, d_bf[:, :N_HEADS]),
             ("ffn_conv_b", ffn_conv_b, m_ffn_conv_b, v_ffn_conv_b, jnp.concatenate([g_ffn0["cb"], g_ffn1["cb"]], axis=0)),
             ("final_norm_g", final_norm_g, m_final_norm_g, v_final_norm_g, d_final)]
    d_cw_full = jnp.stack([g_ffn0["cw"], g_ffn1["cw"]], axis=0)

    def small_rows(a):
        flat = a.astype(F32).reshape(-1)
        n = -(-flat.size // (8 * LANES)) * (8 * LANES)
        return jnp.pad(flat, (0, n - flat.size)).reshape(-1, LANES)

    s_rows = [small_rows(p[1]).shape[0] for p in small]
    s_off = np.concatenate([[0], np.cumsum(s_rows)]).tolist()
    cw_g_rows = small_rows(d_cw_full)
    zeros_cw = jnp.zeros_like(cw_g_rows)
    cat = lambda k: jnp.concatenate([small_rows(p[k]) for p in small] + [zeros_cw], axis=0)
    g_small = jnp.concatenate([small_rows(p[4]) for p in small] + [cw_g_rows], axis=0)
    (gs_all,) = _all_gather([g_small], "ag_small_grads")
    gs, ds_, ms, vs = _sum_adamw([(gs_all, list(range(N_DEV)))], cat(1), cat(2), cat(3), "adamw_small")

    g_cw_full = gs[s_off[-1]:].reshape(-1)[:d_cw_full.size].reshape(d_cw_full.shape)
    g_cw = lax.dynamic_slice_in_dim(g_cw_full, me * (FF // N_DEV), FF // N_DEV, axis=2)
    cw2 = lambda a: _pad_rows(_rows(a.astype(F32), LANES), 16)
    g_cw_o, d_cw_o, m_cw_o, v_cw_o = _sum_adamw([(cw2(g_cw)[None], [0])], cw2(ffn_conv_w), cw2(m_ffn_conv_w),
                                                cw2(v_ffn_conv_w), "adamw_conv_w")

    names = ["mix_norm_g", "ffn_norm_g", "gm_w_in", "gm_ln_g", "gm_ln_b", "gm_w_s", "gm_b_s", "gm_w_out", "fox_w_qkvf",
             "fox_b_f", "fox_w_o", "ffn_w_gate", "ffn_w_up", "ffn_conv_w", "ffn_conv_b", "ffn_w_down", "final_norm_g"]
    big_idx = {p[0]: k for k, p in enumerate(big)}
    small_idx = {p[0]: k for k, p in enumerate(small)}
    shapes = dict(mix_norm_g=mix_norm_g.shape, ffn_norm_g=ffn_norm_g.shape, gm_w_in=gm_w_in.shape, gm_ln_g=gm_ln_g.shape,
                  gm_ln_b=gm_ln_b.shape, gm_w_s=gm_w_s.shape, gm_b_s=gm_b_s.shape, gm_w_out=gm_w_out.shape,
                  fox_w_qkvf=fox_w_qkvf.shape, fox_b_f=fox_b_f.shape, fox_w_o=fox_w_o.shape, ffn_w_gate=ffn_w_gate.shape,
                  ffn_w_up=ffn_w_up.shape, ffn_conv_w=ffn_conv_w.shape, ffn_conv_b=ffn_conv_b.shape,
                  ffn_w_down=ffn_w_down.shape, final_norm_g=final_norm_g.shape)

    def pick(kind_big, kind_small, kind_cw, name):
        shp = shapes[name]
        if name in big_idx:
            return unflat_big(kind_big, big_idx[name]).reshape(shp)
        if name == "ffn_conv_w":
            return kind_cw[:n_cw_rows].reshape(shp)
        k = small_idx[name]
        n = int(np.prod(shp))
        return kind_small[s_off[k]:s_off[k + 1]].reshape(-1)[:n].reshape(shp)

    outs = [loss, dx.reshape(x.shape)]
    for kb, ks, kc in ((gb, gs, g_cw_o), (db, ds_, d_cw_o), (mb, ms, m_cw_o), (vb, vs, v_cw_o)):
        outs += [pick(kb, ks, kc, n) for n in names]
    return tuple(outs)
```

```python
import functools
import math

import numpy as np
import jax
import jax.numpy as jnp
from jax import lax
from jax.experimental import pallas as pl
from jax.experimental.pallas import tpu as pltpu

F32 = jnp.float32
BF16 = jnp.bfloat16
MESH = pl.DeviceIdType.MESH

N_HEADS = 16
HEAD_DIM = 64
CHUNK = 128
GM_GROUPS = 8
RMS_EPS = 1e-6
LN_EPS = 1e-5
ADAM_LR = 0.001
ADAM_B1 = 0.9
ADAM_B2 = 0.999
ADAM_EPS = 1e-08
ADAM_WD = 0.01
ADAM_STEP = 10
N_DEV = 8

LANES = 128
VMEM_BYTES_V7X = 64 * 1024 * 1024
VMEM_LIMIT = 56 * 1024 * 1024

TM = 512
TM_MM = 1024
TT = 1024
TQ = 512
TF = 512
MM_BLOCK_BYTES = 8 * 1024 * 1024
NEG = -1e30


def _cp(sem=None, vmem=VMEM_LIMIT):
    return pltpu.CompilerParams(dimension_semantics=sem, vmem_limit_bytes=vmem)


def _gelu(x):
    c = math.sqrt(2.0 / math.pi)
    return x * (0.5 * (1.0 + jnp.tanh(c * (x + 0.044715 * (x * x * x)))))


def _gelu_grad(x):
    c = math.sqrt(2.0 / math.pi)
    t = jnp.tanh(c * (x + 0.044715 * (x * x * x)))
    return 0.5 * (1.0 + t) + x * (0.5 * (1.0 - t * t)) * (c * (1.0 + 3.0 * 0.044715 * (x * x)))


def _sigmoid(x):
    return 1.0 / (1.0 + jnp.exp(-x))


def _dot_nt(a, b):
    return lax.dot_general(a, b, (((1,), (1,)), ((), ())), preferred_element_type=F32)


def _dot_tn(a, b):
    return lax.dot_general(a, b, (((0,), (0,)), ((), ())), preferred_element_type=F32)


def _rms_fwd(h, g, name):
    T, D = h.shape
    tm = min(TM, T)

    def body(h_ref, g_ref, o_ref):
        x = h_ref[...]
        r = lax.rsqrt(jnp.mean(x * x, axis=-1, keepdims=True) + RMS_EPS)
        o_ref[...] = ((x * r) * g_ref[...]).astype(BF16)

    return pl.pallas_call(
        body, name=name, grid=(T // tm,),
        in_specs=[pl.BlockSpec((tm, D), lambda i: (i, 0)), pl.BlockSpec((1, D), lambda i: (0, 0))],
        out_specs=pl.BlockSpec((tm, D), lambda i: (i, 0)),
        out_shape=jax.ShapeDtypeStruct((T, D), BF16),
        compiler_params=_cp(("parallel",)),
    )(h, g)


def _rms_bwd(dhn, h, g, dres, name):
    T, D = h.shape
    tm = min(TM, T)

    def body(d_ref, h_ref, g_ref, r_ref, o_ref, ob_ref, dg_ref):
        x = h_ref[...]
        d = d_ref[...]
        r = lax.rsqrt(jnp.mean(x * x, axis=-1, keepdims=True) + RMS_EPS)
        dyg = d * g_ref[...]
        dot = jnp.mean(dyg * x, axis=-1, keepdims=True)
        dh = r_ref[...] + (r * dyg - x * ((r * r * r) * dot))
        o_ref[...] = dh
        ob_ref[...] = dh.astype(BF16)
        part = jnp.sum(d * (x * r), axis=0, keepdims=True)

        @pl.when(pl.program_id(0) == 0)
        def _():
            dg_ref[...] = part

        @pl.when(pl.program_id(0) != 0)
        def _():
            dg_ref[...] += part

    blk = pl.BlockSpec((tm, D), lambda i: (i, 0))
    row = pl.BlockSpec((1, D), lambda i: (0, 0))
    return pl.pallas_call(
        body, name=name, grid=(T // tm,),
        in_specs=[blk, blk, row, blk],
        out_specs=[blk, blk, row],
        out_shape=[jax.ShapeDtypeStruct((T, D), F32), jax.ShapeDtypeStruct((T, D), BF16),
                   jax.ShapeDtypeStruct((1, D), F32)],
        compiler_params=_cp(("arbitrary",)),
    )(dhn, h, g, dres)


def _loss_head(h, tgt, g, name):
    T, D = h.shape
    tm = min(TM, T)

    def body(h_ref, t_ref, g_ref, o_ref, ob_ref, dg_ref, l_ref):
        x = h_ref[...]
        gg = g_ref[...]
        r = lax.rsqrt(jnp.mean(x * x, axis=-1, keepdims=True) + RMS_EPS)
        xr = x * r
        e = xr * gg - t_ref[...]
        lpart = 0.5 * jnp.sum(jnp.mean(e * e, axis=-1, keepdims=True), axis=0, keepdims=True)
        dy = e * (1.0 / D)
        dyg = dy * gg
        dot = jnp.mean(dyg * x, axis=-1, keepdims=True)
        dh = r * dyg - x * ((r * r * r) * dot)
        o_ref[...] = dh
        ob_ref[...] = dh.astype(BF16)
        part = jnp.sum(dy * xr, axis=0, keepdims=True)
        lrow = jnp.broadcast_to(lpart, (1, LANES))

        @pl.when(pl.program_id(0) == 0)
        def _():
            dg_ref[...] = part
            l_ref[...] = lrow

        @pl.when(pl.program_id(0) != 0)
        def _():
            dg_ref[...] += part
            l_ref[...] += lrow

    blk = pl.BlockSpec((tm, D), lambda i: (i, 0))
    row = pl.BlockSpec((1, D), lambda i: (0, 0))
    return pl.pallas_call(
        body, name=name, grid=(T // tm,),
        in_specs=[blk, blk, row],
        out_specs=[blk, blk, row, pl.BlockSpec((1, LANES), lambda i: (0, 0))],
        out_shape=[jax.ShapeDtypeStruct((T, D), F32), jax.ShapeDtypeStruct((T, D), BF16),
                   jax.ShapeDtypeStruct((1, D), F32), jax.ShapeDtypeStruct((1, LANES), F32)],
        compiler_params=_cp(("arbitrary",)),
    )(h, tgt, g)


def _mm_nn(a, b, name, out_dtype=F32, res=None):
    M, K = a.shape
    b3 = b if b.ndim == 3 else b[None]
    nb, _, w = b3.shape
    N = nb * w
    tm = min(TM_MM, M, max(256, MM_BLOCK_BYTES // (4 * N)))
    o_spec = pl.BlockSpec((tm, N), lambda i: (i, 0))
    in_specs = [pl.BlockSpec((tm, K), lambda i: (i, 0)), pl.BlockSpec((nb, K, w), lambda i: (0, 0, 0))]
    args = [a, b3]
    if res is not None:
        in_specs.append(o_spec)
        args.append(res)

    def body(*refs):
        a_ref, b_ref = refs[0], refs[1]
        o_ref = refs[-1]
        av = a_ref[...]
        for j in range(nb):
            cols = slice(j * w, (j + 1) * w)
            acc = jnp.dot(av, b_ref[j], preferred_element_type=F32)
            if res is not None:
                acc = refs[2][:, cols] + acc
            o_ref[:, cols] = acc.astype(out_dtype)

    return pl.pallas_call(
        body, name=name, grid=(M // tm,),
        in_specs=in_specs, out_specs=o_spec,
        out_shape=jax.ShapeDtypeStruct((M, N), out_dtype),
        compiler_params=_cp(("parallel",)),
    )(*args)


def _mm_nt(a_list, b, name, out_dtype=F32, add=None):
    M, kw = a_list[0].shape
    tm = min(TM, M)
    na = len(a_list)
    blocked = b.ndim == 3
    N = b.shape[1] if blocked else b.shape[0]
    b_spec = pl.BlockSpec(b.shape, lambda i: (0,) * b.ndim)
    o_spec = pl.BlockSpec((tm, N), lambda i: (i, 0))
    in_specs = [pl.BlockSpec((tm, kw), lambda i: (i, 0)) for _ in a_list] + [b_spec]
    args = list(a_list) + [b]
    if add is not None:
        in_specs.append(o_spec)
        args.append(add)

    def body(*refs):
        a_refs = refs[:na]
        b_ref = refs[na]
        o_ref = refs[-1]
        acc = refs[na + 1][...] if add is not None else None
        for s, a_ref in enumerate(a_refs):
            if blocked:
                w = b_ref.shape[2]
                per = kw // w
                parts = [_dot_nt(a_ref[:, jj * w:(jj + 1) * w], b_ref[s * per + jj]) for jj in range(per)]
            else:
                parts = [_dot_nt(a_ref[...], b_ref[:, s * kw:(s + 1) * kw])]
            for part in parts:
                acc = part if acc is None else acc + part
        o_ref[...] = acc.astype(out_dtype)

    return pl.pallas_call(
        body, name=name, grid=(M // tm,),
        in_specs=in_specs, out_specs=o_spec,
        out_shape=jax.ShapeDtypeStruct((M, N), out_dtype),
        compiler_params=_cp(("parallel",)),
    )(*args)


def _mm_tn(x, y, name, blocked_w=None):
    T, Kx = x.shape
    N = y.shape[1]
    tt = min(TT, T)
    tkx = min(Kx, max(LANES, MM_BLOCK_BYTES // (4 * N)))
    if blocked_w is not None:
        w = blocked_w
        out_shape = jax.ShapeDtypeStruct((N // w, Kx, w), F32)
        o_spec = pl.BlockSpec((N // w, tkx, w), lambda i, t: (0, i, 0))
    else:
        out_shape = jax.ShapeDtypeStruct((Kx, N), F32)
        o_spec = pl.BlockSpec((tkx, N), lambda i, t: (i, 0))

    def body(x_ref, y_ref, o_ref):
        part = _dot_tn(x_ref[...], y_ref[...])
        first = pl.program_id(1) == 0
        if blocked_w is None:
            pieces = [(slice(None), part)]
        else:
            pieces = [(j, part[:, j * blocked_w:(j + 1) * blocked_w]) for j in range(N // blocked_w)]

        @pl.when(first)
        def _():
            for idx, pj in pieces:
                o_ref[idx] = pj

        @pl.when(jnp.logical_not(first))
        def _():
            for idx, pj in pieces:
                o_ref[idx] += pj

    return pl.pallas_call(
        body, name=name, grid=(Kx // tkx, T // tt),
        in_specs=[pl.BlockSpec((tt, tkx), lambda i, t: (t, i)),
                  pl.BlockSpec((tt, N), lambda i, t: (t, 0))],
        out_specs=o_spec, out_shape=out_shape,
        compiler_params=_cp(("parallel", "arbitrary")),
    )(x, y)


def _sgu_pieces(z, lng, lnb, wc, bs_ref):
    E = z.shape[1] // 2
    gd = E // GM_GROUPS
    zu, zv = z[:, :E], z[:, E:]
    u = _gelu(zu)
    v = _gelu(zv)
    mu = jnp.mean(v, axis=-1, keepdims=True)
    xc = v - mu
    rs = lax.rsqrt(jnp.mean(xc * xc, axis=-1, keepdims=True) + LN_EPS)
    xhat = xc * rs
    vln = xhat * lng + lnb
    s = []
    for g in range(GM_GROUPS):
        vg = vln[:, g * gd:(g + 1) * gd].astype(BF16)
        s.append(jnp.dot(wc[g], vg, preferred_element_type=F32) + bs_ref[g])
    return zu, zv, u, xhat, rs, vln, s


def _causal_ws(ws_ref):
    t = lax.broadcasted_iota(jnp.int32, (CHUNK, CHUNK), 0)
    s = lax.broadcasted_iota(jnp.int32, (CHUNK, CHUNK), 1)
    tri = t >= s
    return [jnp.where(tri, ws_ref[g], 0.0).astype(BF16) for g in range(GM_GROUPS)], tri


def _sgu_fwd(z, lng, lnb, ws, bs, name):
    T, E2 = z.shape
    E = E2 // 2
    gd = E // GM_GROUPS
    tm = min(2 * CHUNK, T)

    def body(z_ref, lng_ref, lnb_ref, ws_ref, bs_ref, o_ref):
        wc, _ = _causal_ws(ws_ref)
        for c in range(tm // CHUNK):
            rows = slice(c * CHUNK, (c + 1) * CHUNK)
            _, _, u, _, _, _, s = _sgu_pieces(z_ref[rows, :], lng_ref[...], lnb_ref[...], wc, bs_ref)
            for g in range(GM_GROUPS):
                cols = slice(g * gd, (g + 1) * gd)
                o_ref[rows, cols] = (u[:, cols] * s[g]).astype(BF16)

    full = lambda shape: pl.BlockSpec(shape, lambda i: (0,) * len(shape))
    return pl.pallas_call(
        body, name=name, grid=(T // tm,),
        in_specs=[pl.BlockSpec((tm, E2), lambda i: (i, 0)), full((1, E)), full((1, E)),
                  full((GM_GROUPS, CHUNK, CHUNK)), full((GM_GROUPS, CHUNK, 1))],
        out_specs=pl.BlockSpec((tm, E), lambda i: (i, 0)),
        out_shape=jax.ShapeDtypeStruct((T, E), BF16),
        compiler_params=_cp(("parallel",)),
    )(z, lng, lnb, ws, bs)


def _sgu_bwd(z, dg, lng, lnb, ws, bs, name):
    T, E2 = z.shape
    E = E2 // 2
    gd = E // GM_GROUPS
    tm = min(2 * CHUNK, T)
    nsteps = T // tm

    def body(z_ref, dg_ref, lng_ref, lnb_ref, ws_ref, bs_ref, dz_ref, dlng_ref, dlnb_ref, dws_ref, dbs_ref):
        i = pl.program_id(0)

        @pl.when(i == 0)
        def _():
            dlng_ref[...] = jnp.zeros_like(dlng_ref)
            dlnb_ref[...] = jnp.zeros_like(dlnb_ref)
            dws_ref[...] = jnp.zeros_like(dws_ref)
            dbs_ref[...] = jnp.zeros_like(dbs_ref)

        wc, tri = _causal_ws(ws_ref)
        lng_v = lng_ref[...]
        for c in range(tm // CHUNK):
            rows = slice(c * CHUNK, (c + 1) * CHUNK)
            zu, zv, u, xhat, rs, vln, s = _sgu_pieces(z_ref[rows, :], lng_v, lnb_ref[...], wc, bs_ref)
            dgc = dg_ref[rows, :].astype(F32)
            du, dvln = [], []
            for g in range(GM_GROUPS):
                cols = slice(g * gd, (g + 1) * gd)
                dgg = dgc[:, cols]
                du.append(dgg * s[g])
                ds = dgg * u[:, cols]
                dsb = ds.astype(BF16)
                dws_ref[g] += _dot_nt(dsb, vln[:, cols].astype(BF16))
                dbs_ref[g] += jnp.sum(ds, axis=-1, keepdims=True)
                dvln.append(_dot_tn(wc[g], dsb))
            du = jnp.concatenate(du, axis=1)
            dvln = jnp.concatenate(dvln, axis=1)
            dlng_ref[...] += jnp.sum(dvln * xhat, axis=0, keepdims=True)
            dlnb_ref[...] += jnp.sum(dvln, axis=0, keepdims=True)
            dxh = dvln * lng_v
            m1 = jnp.mean(dxh, axis=-1, keepdims=True)
            m2 = jnp.mean(dxh * xhat, axis=-1, keepdims=True)
            dv = rs * (dxh - m1 - xhat * m2)
            dz_ref[rows, :E] = (du * _gelu_grad(zu)).astype(BF16)
            dz_ref[rows, E:] = (dv * _gelu_grad(zv)).astype(BF16)

        @pl.when(i == nsteps - 1)
        def _():
            for g in range(GM_GROUPS):
                dws_ref[g] = jnp.where(tri, dws_ref[g], 0.0)

    full = lambda shape: pl.BlockSpec(shape, lambda i: (0,) * len(shape))
    return pl.pallas_call(
        body, name=name, grid=(nsteps,),
        in_specs=[pl.BlockSpec((tm, E2), lambda i: (i, 0)), pl.BlockSpec((tm, E), lambda i: (i, 0)),
                  full((1, E)), full((1, E)), full((GM_GROUPS, CHUNK, CHUNK)), full((GM_GROUPS, CHUNK, 1))],
        out_specs=[pl.BlockSpec((tm, E2), lambda i: (i, 0)), full((1, E)), full((1, E)),
                   full((GM_GROUPS, CHUNK, CHUNK)), full((GM_GROUPS, CHUNK, 1))],
        out_shape=[jax.ShapeDtypeStruct((T, E2), BF16), jax.ShapeDtypeStruct((1, E), F32),
                   jax.ShapeDtypeStruct((1, E), F32), jax.ShapeDtypeStruct((GM_GROUPS, CHUNK, CHUNK), F32),
                   jax.ShapeDtypeStruct((GM_GROUPS, CHUNK, 1), F32)],
        compiler_params=_cp(("arbitrary",)),
    )(z, dg, lng, lnb, ws, bs)


HALO = 16


def _conv_taps(a_ext, w_ref, b_ref):
    n = a_ext.shape[0]
    am1 = pltpu.roll(a_ext, 1, 0)
    am2 = pltpu.roll(a_ext, 2, 0)
    del n
    return ((b_ref[...] + am2 * w_ref[0:1, :]) + am1 * w_ref[1:2, :]) + a_ext * w_ref[2:3, :], am1, am2


def _ffn_mid_fwd(au, cw, cb, name):
    T, F = au.shape[0], au.shape[1] // 2
    tm, tf = min(TM, T), min(TF, F)
    hb = tm // HALO
    nf = F // tf

    def body(a_ref, ap_ref, u_ref, w_ref, b_ref, o_ref):
        i = pl.program_id(1)
        prev = jnp.where(i == 0, 0.0, ap_ref[...])
        ext = jnp.concatenate([prev, a_ref[...]], axis=0)
        conv, _, _ = _conv_taps(ext, w_ref, b_ref)
        conv = conv[HALO:, :]
        o_ref[...] = ((conv * _sigmoid(conv)) * u_ref[...]).astype(BF16)

    main = pl.BlockSpec((tm, tf), lambda f, i: (i, f))
    return pl.pallas_call(
        body, name=name, grid=(nf, T // tm),
        in_specs=[main, pl.BlockSpec((HALO, tf), lambda f, i: (jnp.maximum(i * hb - 1, 0), f)),
                  pl.BlockSpec((tm, tf), lambda f, i: (i, nf + f)),
                  pl.BlockSpec((3, tf), lambda f, i: (0, f)), pl.BlockSpec((1, tf), lambda f, i: (0, f))],
        out_specs=main, out_shape=jax.ShapeDtypeStruct((T, F), BF16),
        compiler_params=_cp(("parallel", "parallel")),
    )(au, au, au, cw, cb)


def _ffn_mid_bwd(au, dact, cw, cb, name):
    T, F = au.shape[0], au.shape[1] // 2
    tm, tf = min(TM, T), min(TF, F)
    hb = tm // HALO
    nt = T // tm
    nf = F // tf
    last_h = T // HALO - 1

    def body(a_ref, ap_ref, an_ref, u_ref, un_ref, d_ref, dn_ref, w_ref, b_ref, da_ref, du_ref, dcw_ref, dcb_ref):
        i = pl.program_id(1)
        prev = jnp.where(i == 0, 0.0, ap_ref[...])
        a_main = a_ref[...]
        a_ext = jnp.concatenate([prev, a_main, an_ref[...]], axis=0)
        conv, am1, am2 = _conv_taps(a_ext, w_ref, b_ref)
        conv = conv[HALO:, :]
        sig = _sigmoid(conv)
        u_ext = jnp.concatenate([u_ref[...], un_ref[...]], axis=0)
        d_ext = jnp.concatenate([d_ref[...], dn_ref[...]], axis=0).astype(F32)
        n = tm + HALO
        row = lax.broadcasted_iota(jnp.int32, (n, 1), 0)
        live = jnp.logical_or(row < tm, i < nt - 1)
        dconv = jnp.where(live, d_ext * u_ext * (sig * (1.0 + conv * (1.0 - sig))), 0.0)
        du_ref[...] = (d_ext[:tm, :] * (conv[:tm, :] * sig[:tm, :])).astype(BF16)
        dp1 = pltpu.roll(dconv, n - 1, 0)[:tm, :]
        dp2 = pltpu.roll(dconv, n - 2, 0)[:tm, :]
        dc = dconv[:tm, :]
        da_ref[...] = ((dc * w_ref[2:3, :] + dp1 * w_ref[1:2, :]) + dp2 * w_ref[0:1, :]).astype(BF16)
        g2 = jnp.sum(dc * a_main, axis=0, keepdims=True)
        g1 = jnp.sum(dc * am1[HALO:HALO + tm, :], axis=0, keepdims=True)
        g0 = jnp.sum(dc * am2[HALO:HALO + tm, :], axis=0, keepdims=True)
        gb = jnp.sum(dc, axis=0, keepdims=True)

        @pl.when(i == 0)
        def _():
            dcw_ref[...] = jnp.zeros_like(dcw_ref)
            dcb_ref[...] = jnp.zeros_like(dcb_ref)

        dcw_ref[0:1, :] += g0
        dcw_ref[1:2, :] += g1
        dcw_ref[2:3, :] += g2
        dcb_ref[...] += gb

    main = pl.BlockSpec((tm, tf), lambda f, i: (i, f))
    prev = pl.BlockSpec((HALO, tf), lambda f, i: (jnp.maximum(i * hb - 1, 0), f))
    nxt = pl.BlockSpec((HALO, tf), lambda f, i: (jnp.minimum((i + 1) * hb, last_h), f))
    main_u = pl.BlockSpec((tm, tf), lambda f, i: (i, nf + f))
    nxt_u = pl.BlockSpec((HALO, tf), lambda f, i: (jnp.minimum((i + 1) * hb, last_h), nf + f))
    return pl.pallas_call(
        body, name=name, grid=(nf, nt),
        in_specs=[main, prev, nxt, main_u, nxt_u, main, nxt,
                  pl.BlockSpec((3, tf), lambda f, i: (0, f)), pl.BlockSpec((1, tf), lambda f, i: (0, f))],
        out_specs=[main, main, pl.BlockSpec((3, tf), lambda f, i: (0, f)), pl.BlockSpec((1, tf), lambda f, i: (0, f))],
        out_shape=[jax.ShapeDtypeStruct((T, F), BF16), jax.ShapeDtypeStruct((T, F), BF16),
                   jax.ShapeDtypeStruct((3, F), F32), jax.ShapeDtypeStruct((1, F), F32)],
        compiler_params=_cp(("parallel", "arbitrary")),
    )(au, au, au, au, au, dact, dact, cw, cb)


def _split3(x):
    hi = x.astype(BF16)
    r1 = x - hi.astype(F32)
    mid = r1.astype(BF16)
    lo = (r1 - mid.astype(F32)).astype(BF16)
    return hi, mid, lo


def _tri_ones(n, upper):
    r = lax.broadcasted_iota(jnp.int32, (n, n), 0)
    c = lax.broadcasted_iota(jnp.int32, (n, n), 1)
    return jnp.where((r <= c) if upper else (r >= c), 1.0, 0.0).astype(BF16)


def _gate_scan(f, bf, name):
    T = f.shape[0]
    tm = min(256, T)

    def body(f_ref, b_ref, cp_ref, sn_ref, carry_ref):
        i = pl.program_id(0)

        @pl.when(i == 0)
        def _():
            carry_ref[...] = jnp.zeros_like(carry_ref)

        x = f_ref[...] + b_ref[...]
        e = jnp.exp(-jnp.abs(x))
        logf = jnp.minimum(x, 0.0) - jnp.log(1.0 + e)
        sn_ref[...] = jnp.where(x >= 0.0, e / (1.0 + e), 1.0 / (1.0 + e))
        tri = _tri_ones(tm, upper=False)
        c = carry_ref[...]
        for piece in _split3(logf):
            c = c + jnp.dot(tri, piece, preferred_element_type=F32)
        carry_ref[...] += jnp.sum(logf, axis=0, keepdims=True)
        hi, mid, lo = _split3(c)
        cp_ref[:, 0:LANES] = hi
        cp_ref[:, LANES:2 * LANES] = mid
        cp_ref[:, 2 * LANES:3 * LANES] = lo

    return pl.pallas_call(
        body, name=name, grid=(T // tm,),
        in_specs=[pl.BlockSpec((tm, LANES), lambda i: (i, 0)), pl.BlockSpec((1, LANES), lambda i: (0, 0))],
        out_specs=[pl.BlockSpec((tm, 3 * LANES), lambda i: (i, 0)), pl.BlockSpec((tm, LANES), lambda i: (i, 0))],
        out_shape=[jax.ShapeDtypeStruct((T, 3 * LANES), BF16), jax.ShapeDtypeStruct((T, LANES), F32)],
        scratch_shapes=[pltpu.VMEM((1, LANES), F32)],
        compiler_params=_cp(("arbitrary",)),
    )(f, bf)


def _gate_scan_bwd(dcq, dck, sneg, name):
    T = dcq.shape[0]
    tm = min(256, T)
    n = T // tm

    def body(dcq_ref, dck_ref, sn_ref, df_ref, db_ref, carry_ref):
        i = pl.program_id(0)

        @pl.when(i == 0)
        def _():
            carry_ref[...] = jnp.zeros_like(carry_ref)
            db_ref[...] = jnp.zeros_like(db_ref)

        tri = _tri_ones(tm, upper=True)
        dcb = dcq_ref[...] - dck_ref[...]
        acc = carry_ref[...]
        for piece in _split3(dcb):
            acc = acc + jnp.dot(tri, piece, preferred_element_type=F32)
        carry_ref[...] += jnp.sum(dcb, axis=0, keepdims=True)
        df = acc * sn_ref[...]
        df_ref[...] = df.astype(BF16)
        db_ref[...] += jnp.sum(df, axis=0, keepdims=True)

    rev = pl.BlockSpec((tm, LANES), lambda i: (n - 1 - i, 0))
    return pl.pallas_call(
        body, name=name, grid=(n,),
        in_specs=[rev, rev, rev],
        out_specs=[rev, pl.BlockSpec((1, LANES), lambda i: (0, 0))],
        out_shape=[jax.ShapeDtypeStruct((T, LANES), BF16), jax.ShapeDtypeStruct((1, LANES), F32)],
        scratch_shapes=[pltpu.VMEM((1, LANES), F32)],
        compiler_params=_cp(("arbitrary",)),
    )(dcq, dck, sneg)


def _qk_proj(hn, w_pad, cp, sel, const, scale, name):
    T, D = hn.shape
    H = w_pad.shape[1] // LANES
    tm = min(TM_MM, T)

    def body(a_ref, w_ref, cp_ref, sel_ref, c_ref, o_ref):
        acc = jnp.dot(a_ref[...], w_ref[...], preferred_element_type=F32)
        if scale != 1.0:
            acc = acc * scale
        acc = acc + jnp.dot(cp_ref[...], sel_ref[...], preferred_element_type=F32) + c_ref[...]
        o_ref[0] = acc[:, :LANES].astype(BF16)
        o_ref[1] = acc[:, LANES:].astype(BF16)

    return pl.pallas_call(
        body, name=name, grid=(T // tm, H // 2),
        in_specs=[pl.BlockSpec((tm, D), lambda i, p: (i, 0)), pl.BlockSpec((D, 2 * LANES), lambda i, p: (0, p)),
                  pl.BlockSpec((tm, 3 * LANES), lambda i, p: (i, 0)),
                  pl.BlockSpec((None, 3 * LANES, 2 * LANES), lambda i, p: (p, 0, 0)),
                  pl.BlockSpec((None, 1, 2 * LANES), lambda i, p: (p, 0, 0))],
        out_specs=pl.BlockSpec((2, tm, LANES), lambda i, p: (p, i, 0)),
        out_shape=jax.ShapeDtypeStruct((H, T, LANES), BF16),
        compiler_params=_cp(("parallel", "arbitrary")),
    )(hn, w_pad, cp, sel, const)


def _lane_lo():
    return lax.broadcasted_iota(jnp.int32, (1, LANES), 1) < HEAD_DIM


def _attn_fwd(qp, kp, v, name):
    H, T, _ = qp.shape
    tq = min(TQ, T)
    nrep = tq // LANES

    def body(q_ref, k_ref, v_ref, o_ref, o32_ref, lse_ref, m_sc, l_sc, acc_sc):
        i = pl.program_id(1)
        m_sc[...] = jnp.full(m_sc.shape, NEG, F32)
        l_sc[...] = jnp.zeros_like(l_sc)
        acc_sc[...] = jnp.zeros_like(acc_sc)

        def step(j, masked):
            off = pl.multiple_of(j * tq, tq)
            vblk = v_ref[pl.ds(off, tq), :]
            s_all = [_dot_nt(q_ref[h], k_ref[h, pl.ds(off, tq), :]) for h in range(2)]
            for h in range(2):
                s = s_all[h]
                tiles = [s[:, c * LANES:(c + 1) * LANES] for c in range(nrep)]
                if masked:
                    r = lax.broadcasted_iota(jnp.int32, (tq, LANES), 0)
                    cc = lax.broadcasted_iota(jnp.int32, (tq, LANES), 1)
                    tiles = [jnp.where(r >= cc + c * LANES, t, NEG) for c, t in enumerate(tiles)]
                mt = tiles[0]
                for t in tiles[1:]:
                    mt = jnp.maximum(mt, t)
                m_prev = m_sc[h]
                m_new = jnp.maximum(m_prev, jnp.max(mt, axis=-1, keepdims=True))
                alpha = jnp.exp(m_prev - m_new)
                ps = [jnp.exp(t - m_new) for t in tiles]
                psum = ps[0]
                for pt in ps[1:]:
                    psum = psum + pt
                p16 = jnp.concatenate([pt.astype(BF16) for pt in ps], axis=1)
                l_sc[h] = alpha * l_sc[h] + psum
                acc_sc[h] = alpha * acc_sc[h] + jnp.dot(p16, vblk, preferred_element_type=F32)
                m_sc[h] = m_new

        def loop_body(j, carry):
            step(j, False)
            return carry

        lax.fori_loop(0, i, loop_body, 0)
        step(i, True)
        lo = _lane_lo()
        l0 = jnp.sum(l_sc[0], axis=-1, keepdims=True)
        l1 = jnp.sum(l_sc[1], axis=-1, keepdims=True)
        o = jnp.where(lo, acc_sc[0] / l0, acc_sc[1] / l1)
        o_ref[...] = o.astype(BF16)
        o32_ref[...] = o
        lse_ref[...] = jnp.where(lo, m_sc[0] + jnp.log(l0), m_sc[1] + jnp.log(l1))

    oblk = pl.BlockSpec((tq, LANES), lambda p, i: (i, p))
    return pl.pallas_call(
        body, name=name, grid=(H // 2, T // tq),
        in_specs=[pl.BlockSpec((2, tq, LANES), lambda p, i: (p, i, 0)),
                  pl.BlockSpec((2, T, LANES), lambda p, i: (p, 0, 0)),
                  pl.BlockSpec((T, LANES), lambda p, i: (0, p))],
        out_specs=[oblk, oblk, pl.BlockSpec((None, tq, LANES), lambda p, i: (p, i, 0))],
        out_shape=[jax.ShapeDtypeStruct((T, H * HEAD_DIM), BF16), jax.ShapeDtypeStruct((T, H * HEAD_DIM), F32),
                   jax.ShapeDtypeStruct((H // 2, T, LANES), F32)],
        scratch_shapes=[pltpu.VMEM((2, tq, LANES), F32), pltpu.VMEM((2, tq, LANES), F32),
                        pltpu.VMEM((2, tq, LANES), F32)],
        compiler_params=_cp(("parallel", "arbitrary")),
    )(qp, kp, v)


def _attn_bwd(qp, kp, v, o, do, lse, scale, name):
    H, T, _ = qp.shape
    tq = min(TQ, T)
    nq = T // tq
    nrep = tq // LANES

    def body(q_ref, k_ref, v_ref, o_ref, do_ref, lse_ref, dq_ref, dk_ref, dv_ref, dqe_ref, dke_ref, dk_sc, dv_sc, dq_sc):
        i = pl.program_id(1)

        @pl.when(i == 0)
        def _():
            dk_sc[...] = jnp.zeros_like(dk_sc)
            dv_sc[...] = jnp.zeros_like(dv_sc)

        dq_sc[...] = jnp.zeros_like(dq_sc)

        lo = _lane_lo()
        dob = do_ref[...]
        dof = dob.astype(F32)
        prod = dof * o_ref[...].astype(F32)
        lse2 = lse_ref[...]
        lse2_sw = pltpu.roll(lse2, HEAD_DIM, 1)
        zero = jnp.zeros_like(dob)
        do_h = [jnp.where(lo, dob, zero), jnp.where(lo, zero, dob)]
        rep = lambda col: jnp.broadcast_to(col, (tq, LANES))
        delta = [rep(jnp.sum(jnp.where(lo, prod, 0.0), axis=-1, keepdims=True)),
                 rep(jnp.sum(jnp.where(lo, 0.0, prod), axis=-1, keepdims=True))]
        lse_h = [jnp.where(lo, lse2, lse2_sw), jnp.where(lo, lse2_sw, lse2)]
        qs = [q_ref[0], q_ref[1]]

        def step(j, masked):
            off = pl.multiple_of(j * tq, tq)
            vblk = v_ref[pl.ds(off, tq), :]
            dv_add = None
            for h in range(2):
                kblk = k_ref[h, pl.ds(off, tq), :]
                s = _dot_nt(qs[h], kblk)
                dp = _dot_nt(do_h[h], vblk)
                p16, ds16 = [], []
                for c in range(nrep):
                    cols = slice(c * LANES, (c + 1) * LANES)
                    p = jnp.exp(s[:, cols] - lse_h[h])
                    if masked:
                        r = lax.broadcasted_iota(jnp.int32, (tq, LANES), 0)
                        cc = lax.broadcasted_iota(jnp.int32, (tq, LANES), 1)
                        p = jnp.where(r >= cc + c * LANES, p, 0.0)
                    p16.append(p.astype(BF16))
                    ds16.append((p * (dp[:, cols] - delta[h])).astype(BF16))
                p16 = jnp.concatenate(p16, axis=1)
                dsb = jnp.concatenate(ds16, axis=1)
                dq_sc[h] += jnp.dot(dsb, kblk, preferred_element_type=F32)
                dk_sc[h, pl.ds(off, tq), :] += _dot_tn(dsb, qs[h])
                pv = _dot_tn(p16, do_h[h])
                dv_add = pv if dv_add is None else dv_add + pv
            dv_sc[pl.ds(off, tq), :] += dv_add

        def loop_body(j, carry):
            step(j, False)
            return carry

        lax.fori_loop(0, i, loop_body, 0)
        step(i, True)
        dq0, dq1 = dq_sc[0], dq_sc[1]
        dq_ref[...] = (jnp.where(lo, dq0, pltpu.roll(dq1, HEAD_DIM, 1)) * scale).astype(BF16)
        dqe_ref[...] = jnp.where(lo, pltpu.roll(dq0, HEAD_DIM, 1), dq1)

        @pl.when(i == nq - 1)
        def _():
            dk0, dk1 = dk_sc[0], dk_sc[1]
            dk_ref[...] = jnp.where(lo, dk0, pltpu.roll(dk1, HEAD_DIM, 1)).astype(BF16)
            dke_ref[...] = jnp.where(lo, pltpu.roll(dk0, HEAD_DIM, 1), dk1)
            dv_ref[...] = dv_sc[...].astype(BF16)

    qblk = pl.BlockSpec((tq, LANES), lambda p, i: (i, p))
    pair = pl.BlockSpec((T, LANES), lambda p, i: (0, p))
    tok16 = jax.ShapeDtypeStruct((T, H * HEAD_DIM), BF16)
    tok32 = jax.ShapeDtypeStruct((T, H * HEAD_DIM), F32)
    return pl.pallas_call(
        body, name=name, grid=(H // 2, nq),
        in_specs=[pl.BlockSpec((2, tq, LANES), lambda p, i: (p, i, 0)),
                  pl.BlockSpec((2, T, LANES), lambda p, i: (p, 0, 0)),
                  pair, qblk, qblk,
                  pl.BlockSpec((None, tq, LANES), lambda p, i: (p, i, 0))],
        out_specs=[qblk, pair, pair, qblk, pair],
        out_shape=[tok16, tok16, tok16, tok32, tok32],
        scratch_shapes=[pltpu.VMEM((2, T, LANES), F32), pltpu.VMEM((T, LANES), F32),
                        pltpu.VMEM((2, tq, LANES), F32)],
        compiler_params=_cp(("parallel", "arbitrary")),
    )(qp, kp, v, o, do, lse)


def _mesh_pos():
    return lax.axis_index("x"), lax.axis_index("y"), lax.axis_index("c")


def _all_gather(arrs, name):
    n = len(arrs)

    def body(*refs):
        ins, outs = refs[:n], refs[n:2 * n]
        send_sems, recv_sems, local_sems = refs[2 * n:]
        x, y, c = _mesh_pos()
        me, sib = (x, y, c), (x, y, 1 - c)
        chips = [(1 - x, y), (x, 1 - y), (1 - x, 1 - y)]

        def slot(px, py, pc):
            return 4 * px + 2 * py + pc

        def copy(a, k, block, to, src=None):
            dst = outs[a].at[slot(*block)]
            return pltpu.make_async_remote_copy(
                src_ref=dst if src is None else src, dst_ref=dst,
                send_sem=send_sems.at[a, k], recv_sem=recv_sems.at[a, k], device_id=to, device_id_type=MESH)

        mine = [pltpu.make_async_copy(ins[a], outs[a].at[slot(*me)], local_sems.at[a]) for a in range(n)]
        for cp in mine:
            cp.start()
        first = []
        for a in range(n):
            first.append(copy(a, 0, me, sib, src=ins[a]))
            first += [copy(a, 1 + j, me, (*chip, c), src=ins[a]) for j, chip in enumerate(chips)]
        for cp in first:
            cp.start()
        passed = []
        for j, chip in enumerate(chips):
            for a in range(n):
                copy(a, 1 + j, (*chip, c), me).wait_recv()
                fwd = copy(a, 4 + j, (*chip, c), sib)
                fwd.start()
                passed.append(fwd)
        for a in range(n):
            copy(a, 0, sib, me).wait_recv()
            for j, chip in enumerate(chips):
                copy(a, 4 + j, (*chip, 1 - c), me).wait_recv()
        for cp in first + passed:
            cp.wait_send()
        for cp in mine:
            cp.wait()

    any_spec = pl.BlockSpec(memory_space=pl.ANY)
    return pl.pallas_call(
        body, name=name,
        in_specs=[any_spec] * n, out_specs=[any_spec] * n,
        out_shape=[jax.ShapeDtypeStruct((N_DEV,) + a.shape, a.dtype) for a in arrs],
        scratch_shapes=[pltpu.SemaphoreType.DMA((n, 7)), pltpu.SemaphoreType.DMA((n, 7)),
                        pltpu.SemaphoreType.DMA((n,))],
    )(*arrs)


def _pair_exchange(g, name):
    def body(g_ref, o_ref, send_sems, recv_sems):
        x, y, c = _mesh_pos()
        sib = (x, y, 1 - c)
        copies = []
        for j in range(4):
            copies.append(pltpu.make_async_remote_copy(
                src_ref=g_ref.at[2 * j + (1 - c)], dst_ref=o_ref.at[j],
                send_sem=send_sems.at[j], recv_sem=recv_sems.at[j], device_id=sib, device_id_type=MESH))
        for cp in copies:
            cp.start()
        for cp in copies:
            cp.wait_recv()
        for cp in copies:
            cp.wait_send()

    any_spec = pl.BlockSpec(memory_space=pl.ANY)
    return pl.pallas_call(
        body, name=name, in_specs=[any_spec], out_specs=any_spec,
        out_shape=jax.ShapeDtypeStruct((4,) + g.shape[1:], g.dtype),
        scratch_shapes=[pltpu.SemaphoreType.DMA((4,)), pltpu.SemaphoreType.DMA((4,))],
    )(g)


def _chip_exchange(part, name):
    def body(p_ref, o_ref, send_sems, recv_sems):
        x, y, c = _mesh_pos()
        chips = [(1 - x, y), (x, 1 - y), (1 - x, 1 - y)]
        copies = []
        for k, (px, py) in enumerate(chips):
            copies.append(pltpu.make_async_remote_copy(
                src_ref=p_ref.at[2 * px + py], dst_ref=o_ref.at[k],
                send_sem=send_sems.at[k], recv_sem=recv_sems.at[k], device_id=(px, py, c), device_id_type=MESH))
        for cp in copies:
            cp.start()
        for cp in copies:
            cp.wait_recv()
        for cp in copies:
            cp.wait_send()

    any_spec = pl.BlockSpec(memory_space=pl.ANY)
    return pl.pallas_call(
        body, name=name, in_specs=[any_spec], out_specs=any_spec,
        out_shape=jax.ShapeDtypeStruct((3,) + part.shape[1:], part.dtype),
        scratch_shapes=[pltpu.SemaphoreType.DMA((3,)), pltpu.SemaphoreType.DMA((3,))],
    )(part)


def _row_block(R, C):
    best = None
    for d in range(16, R + 1, 16):
        if R % d == 0 and d * C <= 256 * 1024:
            best = d
    return best if best is not None else R


def _pair_add(g, recv, cidx, name):
    _, R, C = g.shape
    tr = _row_block(R, C)

    def body(c_ref, g_ref, r_ref, o_ref):
        del c_ref
        o_ref[...] = (g_ref[...].astype(F32) + r_ref[...].astype(F32)).astype(BF16)

    grid_spec = pltpu.PrefetchScalarGridSpec(
        num_scalar_prefetch=1, grid=(4, R // tr),
        in_specs=[pl.BlockSpec((None, tr, C), lambda j, i, c: (2 * j + c[0], i, 0)),
                  pl.BlockSpec((None, tr, C), lambda j, i, c: (j, i, 0))],
        out_specs=pl.BlockSpec((None, tr, C), lambda j, i, c: (j, i, 0)))
    return pl.pallas_call(
        body, name=name, grid_spec=grid_spec,
        out_shape=jax.ShapeDtypeStruct((4, R, C), BF16),
        compiler_params=_cp(("parallel", "parallel")),
    )(cidx, g, recv)


def _adamw_math(w, g, m, v):
    m = ADAM_B1 * m + (1.0 - ADAM_B1) * g
    v = ADAM_B2 * v + (1.0 - ADAM_B2) * (g * g)
    m_hat = m / (1.0 - ADAM_B1 ** ADAM_STEP)
    v_hat = v / (1.0 - ADAM_B2 ** ADAM_STEP)
    delta = -ADAM_LR * (m_hat / (jnp.sqrt(v_hat) + ADAM_EPS) + ADAM_WD * w)
    return delta, m, v


def _sum_adamw(parts, w, m, v, name, sel=None):
    R, C = w.shape
    tr = _row_block(R, C)
    specs, args = [], []
    for arr, idxs in parts:
        for idx in idxs:
            if idx < 0:
                specs.append(pl.BlockSpec((None, tr, C), lambda i, s: (s[0], i, 0)))
            else:
                specs.append(pl.BlockSpec((None, tr, C), lambda i, s, idx=idx: (idx, i, 0)))
            args.append(arr)
    npart = len(args)
    blk = pl.BlockSpec((tr, C), lambda i, s: (i, 0))

    def body(s_ref, *refs):
        del s_ref
        g = refs[0][...].astype(F32)
        for r in refs[1:npart]:
            g = g + r[...].astype(F32)
        w_ref, m_ref, v_ref, g_out, d_out, m_out, v_out = refs[npart:]
        delta, mm, vv = _adamw_math(w_ref[...], g, m_ref[...], v_ref[...])
        g_out[...] = g
        d_out[...] = delta
        m_out[...] = mm
        v_out[...] = vv

    grid_spec = pltpu.PrefetchScalarGridSpec(
        num_scalar_prefetch=1, grid=(R // tr,),
        in_specs=specs + [blk, blk, blk], out_specs=[blk] * 4)
    if sel is None:
        sel = jnp.zeros((1,), jnp.int32)
    return pl.pallas_call(
        body, name=name, grid_spec=grid_spec,
        out_shape=[jax.ShapeDtypeStruct((R, C), F32)] * 4,
        compiler_params=_cp(("parallel",)),
    )(sel, *args, w, m, v)


def _rows(a, c):
    return a.reshape(-1, c)


def _pad_rows(a, r):
    return jnp.pad(a, ((0, r - a.shape[0]), (0, 0))) if a.shape[0] != r else a


def _pad16(n):
    return -(-n // 16) * 16


def _gate_tables():
    hp = N_HEADS // 2
    sel_q = np.zeros((hp, 3 * LANES, 2 * LANES), np.float32)
    sel_k = np.zeros((hp, 3 * LANES, 2 * LANES), np.float32)
    const_q = np.zeros((hp, 1, 2 * LANES), np.float32)
    const_k = np.zeros((hp, 1, 2 * LANES), np.float32)
    for p in range(hp):
        for hh in range(2):
            h = 2 * p + hh
            base = hh * LANES + HEAD_DIM
            for piece in range(3):
                sel_q[p, piece * LANES + h, base + piece] = 1.0
                sel_k[p, piece * LANES + h, base + 3 + piece] = -1.0
            const_k[p, 0, base:base + 3] = 1.0
            const_q[p, 0, base + 3:base + 6] = 1.0
    as_bf = lambda t: jnp.asarray(t, BF16)
    return as_bf(sel_q), as_bf(sel_k), jnp.asarray(const_q), jnp.asarray(const_k)


def _pad_heads(w):
    d = w.shape[0]
    w3 = w.reshape(d, N_HEADS, HEAD_DIM)
    return jnp.pad(w3, ((0, 0), (0, 0), (0, LANES - HEAD_DIM))).reshape(d, N_HEADS * LANES)


def kernel(x, mix_norm_g, ffn_norm_g, gm_w_in, gm_ln_g, gm_ln_b, gm_w_s, gm_b_s, gm_w_out, fox_w_qkvf, fox_b_f, fox_w_o, ffn_w_gate, ffn_w_up, ffn_conv_w, ffn_conv_b, ffn_w_down, final_norm_g, loss_target, m_mix_norm_g, m_ffn_norm_g, m_gm_w_in, m_gm_ln_g, m_gm_ln_b, m_gm_w_s, m_gm_b_s, m_gm_w_out, m_fox_w_qkvf, m_fox_b_f, m_fox_w_o, m_ffn_w_gate, m_ffn_w_up, m_ffn_conv_w, m_ffn_conv_b, m_ffn_w_down, m_final_norm_g, v_mix_norm_g, v_ffn_norm_g, v_gm_w_in, v_gm_ln_g, v_gm_ln_b, v_gm_w_s, v_gm_b_s, v_gm_w_out, v_fox_w_qkvf, v_fox_b_f, v_fox_w_o, v_ffn_w_gate, v_ffn_w_up, v_ffn_conv_w, v_ffn_conv_b, v_ffn_w_down, v_final_norm_g):
    T, D = x.shape[1], x.shape[2]
    E = gm_ln_g.shape[1]
    FF = ffn_conv_b.shape[1]
    NQKVF = 3 * D + N_HEADS
    xi, yi, ci = _mesh_pos()
    me = 4 * xi + 2 * yi + ci
    h0 = x.reshape(T, D)
    tgt = loss_target.reshape(T, D)

    big = [("gm_w_in", gm_w_in, m_gm_w_in, v_gm_w_in), ("gm_w_out", gm_w_out, m_gm_w_out, v_gm_w_out),
           ("fox_w_qkvf", fox_w_qkvf, m_fox_w_qkvf, v_fox_w_qkvf), ("fox_w_o", fox_w_o, m_fox_w_o, v_fox_w_o),
           ("ffn_w_gate", ffn_w_gate, m_ffn_w_gate, v_ffn_w_gate), ("ffn_w_up", ffn_w_up, m_ffn_w_up, v_ffn_w_up),
           ("ffn_w_down", ffn_w_down, m_ffn_w_down, v_ffn_w_down)]
    big_rows = [p[1].size // D for p in big]
    big_pad = [_pad16(r) for r in big_rows]
    big_off = np.concatenate([[0], np.cumsum(big_pad)]).tolist()
    R = -(-big_off[-1] // 80) * 80

    def flat_big(arrs, dtype):
        parts = [_pad_rows(_rows(a, D).astype(dtype), pr) for a, pr in zip(arrs, big_pad)]
        parts.append(jnp.zeros((R - big_off[-1], D), dtype))
        return jnp.concatenate(parts, axis=0)

    def unflat_big(flat, k):
        return flat[..., big_off[k]:big_off[k] + big_rows[k], :]

    w_flat16 = flat_big([p[1] for p in big], BF16)
    cw_rows = _pad_rows(_rows(ffn_conv_w.astype(F32), LANES), 16)
    wg, cwg = _all_gather([w_flat16, cw_rows], "ag_weights")

    nl = ffn_w_gate.shape[0]
    w_in_g = unflat_big(wg, 0).reshape(N_DEV, D, 2 * E // N_DEV)
    w_out_g = unflat_big(wg, 1).reshape(E, D)
    w_qkvf = jnp.transpose(unflat_big(wg, 2).reshape(N_DEV, D, NQKVF // N_DEV), (1, 0, 2)).reshape(D, NQKVF)
    w_o_g = unflat_big(wg, 3).reshape(D, D)
    gate_g = unflat_big(wg, 4).reshape(N_DEV, nl, D, FF // N_DEV)
    up_g = unflat_big(wg, 5).reshape(N_DEV, nl, D, FF // N_DEV)
    down_g = unflat_big(wg, 6).reshape(N_DEV, nl, FF // N_DEV, D)
    n_cw_rows = ffn_conv_w.size // LANES
    conv_w_full = jnp.transpose(cwg[:, :n_cw_rows].reshape(N_DEV, nl, 3, FF // N_DEV), (1, 2, 0, 3)).reshape(nl, 3, FF)

    def ffn_weights(l):
        return jnp.concatenate([gate_g[:, l], up_g[:, l]], axis=0), down_g[:, l].reshape(FF, D)

    saved = {}

    def ffn_fwd(l, h_in):
        wgul, wdl = ffn_weights(l)
        hn = _rms_fwd(h_in, ffn_norm_g[l:l + 1], f"ffn{l}_norm")
        au = _mm_nn(hn, wgul, f"ffn{l}_gate_up")
        act = _ffn_mid_fwd(au, conv_w_full[l], ffn_conv_b[l:l + 1], f"ffn{l}_mid")
        h_out = _mm_nn(act, wdl, f"ffn{l}_down", res=h_in)
        saved[f"ffn{l}"] = (h_in, hn, au, act)
        return h_out

    bs_col = gm_b_s[0].reshape(GM_GROUPS, CHUNK, 1)
    hn0 = _rms_fwd(h0, mix_norm_g[0:1], "mix0_norm")
    z = _mm_nn(hn0, w_in_g, "gm_in")
    gu = _sgu_fwd(z, gm_ln_g, gm_ln_b, gm_w_s[0], bs_col, "gm_sgu")
    h1 = _mm_nn(gu, w_out_g, "gm_out", res=h0)
    h2 = ffn_fwd(0, h1)

    w_q, w_k, w_v = w_qkvf[:, :D], w_qkvf[:, D:2 * D], w_qkvf[:, 2 * D:3 * D]
    w_f = jnp.pad(w_qkvf[:, 3 * D:], ((0, 0), (0, LANES - N_HEADS)))
    bf_row = jnp.pad(fox_b_f, ((0, 0), (0, LANES - N_HEADS)))
    sel_q, sel_k, const_q, const_k = _gate_tables()
    scale = HEAD_DIM ** -0.5
    hn2 = _rms_fwd(h2, mix_norm_g[1:2], "mix1_norm")
    f_logit = _mm_nn(hn2, w_f, "fox_f")
    cp, sneg = _gate_scan(f_logit, bf_row, "fox_scan")
    qp = _qk_proj(hn2, _pad_heads(w_q), cp, sel_q, const_q, scale, "fox_q")
    kp = _qk_proj(hn2, _pad_heads(w_k), cp, sel_k, const_k, 1.0, "fox_k")
    vv = _mm_nn(hn2, w_v, "fox_v", out_dtype=BF16)
    o, o32, lse = _attn_fwd(qp, kp, vv, "fox_attn")
    h3 = _mm_nn(o, w_o_g, "fox_o", res=h2)
    h4 = ffn_fwd(1, h3)

    dh, dh16, d_final, loss_row = _loss_head(h4, tgt, final_norm_g.reshape(1, D), "loss_head")
    loss = lax.psum(loss_row[0, 0], ("x", "y", "c"))

    def ffn_bwd(l, dh, dh16):
        wgul, wdl = ffn_weights(l)
        h_in, hn, au, act = saved[f"ffn{l}"]
        dact = _mm_nt([dh16], wdl, f"ffn{l}_dact", out_dtype=BF16)
        d_wd = _mm_tn(act, dh16, f"ffn{l}_dwd")
        da, dup, d_cw, d_cb = _ffn_mid_bwd(au, dact, conv_w_full[l], ffn_conv_b[l:l + 1], f"ffn{l}_dmid")
        dhn = _mm_nt([da, dup], wgul, f"ffn{l}_dhn")
        d_wg = _mm_tn(hn, da, f"ffn{l}_dwg", blocked_w=FF // N_DEV)
        d_wu = _mm_tn(hn, dup, f"ffn{l}_dwu", blocked_w=FF // N_DEV)
        dh_in, dh_in16, d_norm = _rms_bwd(dhn, h_in, ffn_norm_g[l:l + 1], dh, f"ffn{l}_dnorm")
        return dh_in, dh_in16, dict(wd=d_wd, wg=d_wg, wu=d_wu, cw=d_cw, cb=d_cb, norm=d_norm)

    dh, dh16, g_ffn1 = ffn_bwd(1, dh, dh16)

    do = _mm_nt([dh16], w_o_g, "fox_do", out_dtype=BF16)
    d_wo = _mm_tn(o, dh16, "fox_dwo")
    dq, dk, dv, dqe, dke = _attn_bwd(qp, kp, vv, o32, do, lse, scale, "fox_dattn")
    gate_lane = lambda e, lane: jnp.pad(e.reshape(T, N_HEADS, HEAD_DIM)[:, :, lane], ((0, 0), (0, LANES - N_HEADS)))
    df, d_bf = _gate_scan_bwd(gate_lane(dqe, 0), gate_lane(dke, 3), sneg, "fox_dscan")
    dhn = _mm_nt([df], w_f, "fox_dhn_f")
    dhn = _mm_nt([dq, dk, dv], w_qkvf[:, :3 * D], "fox_dhn_qkv", add=dhn)
    d_wq = _mm_tn(hn2, dq, "fox_dwq")
    d_wk = _mm_tn(hn2, dk, "fox_dwk")
    d_wv = _mm_tn(hn2, dv, "fox_dwv")
    d_wf = _mm_tn(hn2, df, "fox_dwf")
    d_wqkvf = jnp.concatenate([d_wq, d_wk, d_wv, d_wf[:, :N_HEADS]], axis=1)
    dh, dh16, d_mix1 = _rms_bwd(dhn, h2, mix_norm_g[1:2], dh, "mix1_dnorm")

    dh, dh16, g_ffn0 = ffn_bwd(0, dh, dh16)

    dgu = _mm_nt([dh16], w_out_g, "gm_dgu", out_dtype=BF16)
    d_wout = _mm_tn(gu, dh16, "gm_dwout")
    dz, d_lng, d_lnb, d_ws, d_bs = _sgu_bwd(z, dgu, gm_ln_g, gm_ln_b, gm_w_s[0], bs_col, "gm_dsgu")
    dhn = _mm_nt([dz], w_in_g, "gm_dhn")
    d_win = _mm_tn(hn0, dz, "gm_dwin", blocked_w=2 * E // N_DEV)
    dx, _, d_mix0 = _rms_bwd(dhn, h0, mix_norm_g[0:1], dh, "mix0_dnorm")

    g_big = [d_win,
             d_wout.reshape(N_DEV, E // N_DEV, D),
             jnp.transpose(d_wqkvf.reshape(D, N_DEV, NQKVF // N_DEV), (1, 0, 2)),
             d_wo.reshape(N_DEV, D // N_DEV, D),
             jnp.stack([g_ffn0["wg"], g_ffn1["wg"]], axis=1),
             jnp.stack([g_ffn0["wu"], g_ffn1["wu"]], axis=1),
             jnp.stack([g_ffn0["wd"].reshape(N_DEV, FF // N_DEV, D), g_ffn1["wd"].reshape(N_DEV, FF // N_DEV, D)], axis=1)]
    parts = [jnp.pad(a.reshape(N_DEV, -1, D).astype(BF16), ((0, 0), (0, pr - r), (0, 0)))
             for a, r, pr in zip(g_big, big_rows, big_pad)]
    parts.append(jnp.zeros((N_DEV, R - big_off[-1], D), BF16))
    g_flat = jnp.concatenate(parts, axis=1)
    cidx = ci.astype(jnp.int32).reshape(1)
    chip_idx = (2 * xi + yi).astype(jnp.int32).reshape(1)
    recv1 = _pair_exchange(g_flat, "rs_pair")
    part = _pair_add(g_flat, recv1, cidx, "rs_pair_add")
    recv2 = _chip_exchange(part, "rs_chip")
    w_flat = flat_big([p[1] for p in big], F32)
    m_flat = flat_big([p[2] for p in big], F32)
    v_flat = flat_big([p[3] for p in big], F32)
    gb, db, mb, vb = _sum_adamw([(part, [-1]), (recv2, [0, 1, 2])], w_flat, m_flat, v_flat, "adamw_big", sel=chip_idx)

    small = [("mix_norm_g", mix_norm_g, m_mix_norm_g, v_mix_norm_g, jnp.concatenate([d_mix0, d_mix1], axis=0)),
             ("ffn_norm_g", ffn_norm_g, m_ffn_norm_g, v_ffn_norm_g, jnp.concatenate([g_ffn0["norm"], g_ffn1["norm"]], axis=0)),
             ("gm_ln_g", gm_ln_g, m_gm_ln_g, v_gm_ln_g, d_lng),
             ("gm_ln_b", gm_ln_b, m_gm_ln_b, v_gm_ln_b, d_lnb),
             ("gm_w_s", gm_w_s, m_gm_w_s, v_gm_w_s, d_ws),
             ("gm_b_s", gm_b_s, m_gm_b_s, v_gm_b_s, d_bs),
             ("fox_b_f", fox_b_f, m_fox_b_f, v_fox_b_f, d_bf[:, :N_HEADS]),
             ("ffn_conv_b", ffn_conv_b, m_ffn_conv_b, v_ffn_conv_b, jnp.concatenate([g_ffn0["cb"], g_ffn1["cb"]], axis=0)),
             ("final_norm_g", final_norm_g, m_final_norm_g, v_final_norm_g, d_final)]
    d_cw_full = jnp.stack([g_ffn0["cw"], g_ffn1["cw"]], axis=0)

    def small_rows(a):
        flat = a.astype(F32).reshape(-1)
        n = -(-flat.size // (8 * LANES)) * (8 * LANES)
        return jnp.pad(flat, (0, n - flat.size)).reshape(-1, LANES)

    s_rows = [small_rows(p[1]).shape[0] for p in small]
    s_off = np.concatenate([[0], np.cumsum(s_rows)]).tolist()
    cw_g_rows = small_rows(d_cw_full)
    zeros_cw = jnp.zeros_like(cw_g_rows)
    cat = lambda k: jnp.concatenate([small_rows(p[k]) for p in small] + [zeros_cw], axis=0)
    g_small = jnp.concatenate([small_rows(p[4]) for p in small] + [cw_g_rows], axis=0)
    (gs_all,) = _all_gather([g_small], "ag_small_grads")
    gs, ds_, ms, vs = _sum_adamw([(gs_all, list(range(N_DEV)))], cat(1), cat(2), cat(3), "adamw_small")

    g_cw_full = gs[s_off[-1]:].reshape(-1)[:d_cw_full.size].reshape(d_cw_full.shape)
    g_cw = lax.dynamic_slice_in_dim(g_cw_full, me * (FF // N_DEV), FF // N_DEV, axis=2)
    cw2 = lambda a: _pad_rows(_rows(a.astype(F32), LANES), 16)
    g_cw_o, d_cw_o, m_cw_o, v_cw_o = _sum_adamw([(cw2(g_cw)[None], [0])], cw2(ffn_conv_w), cw2(m_ffn_conv_w),
                                                cw2(v_ffn_conv_w), "adamw_conv_w")

    names = ["mix_norm_g", "ffn_norm_g", "gm_w_in", "gm_ln_g", "gm_ln_b", "gm_w_s", "gm_b_s", "gm_w_out", "fox_w_qkvf",
             "fox_b_f", "fox_w_o", "ffn_w_gate", "ffn_w_up", "ffn_conv_w", "ffn_conv_b", "ffn_w_down", "final_norm_g"]
    big_idx = {p[0]: k for k, p in enumerate(big)}
    small_idx = {p[0]: k for k, p in enumerate(small)}
    shapes = dict(mix_norm_g=mix_norm_g.shape, ffn_norm_g=ffn_norm_g.shape, gm_w_in=gm_w_in.shape, gm_ln_g=gm_ln_g.shape,
                  gm_ln_b=gm_ln_b.shape, gm_w_s=gm_w_s.shape, gm_b_s=gm_b_s.shape, gm_w_out=gm_w_out.shape,
                  fox_w_qkvf=fox_w_qkvf.shape, fox_b_f=fox_b_f.shape, fox_w_o=fox_w_o.shape, ffn_w_gate=ffn_w_gate.shape,
                  ffn_w_up=ffn_w_up.shape, ffn_conv_w=ffn_conv_w.shape, ffn_conv_b=ffn_conv_b.shape,
                  ffn_w_down=ffn_w_down.shape, final_norm_g=final_norm_g.shape)

    def pick(kind_big, kind_small, kind_cw, name):
        shp = shapes[name]
        if name in big_idx:
            return unflat_big(kind_big, big_idx[name]).reshape(shp)
        if name == "ffn_conv_w":
            return kind_cw[:n_cw_rows].reshape(shp)
        k = small_idx[name]
        n = int(np.prod(shp))
        return kind_small[s_off[k]:s_off[k + 1]].reshape(-1)[:n].reshape(shp)

    outs = [loss, dx.reshape(x.shape)]
    for kb, ks, kc in ((gb, gs, g_cw_o), (db, ds_, d_cw_o), (mb, ms, m_cw_o), (vb, vs, v_cw_o)):
        outs += [pick(kb, ks, kc, n) for n in names]
    return tuple(outs)
```

```python
import functools
import math

import numpy as np
import jax
import jax.numpy as jnp
from jax import lax
from jax.experimental import pallas as pl
from jax.experimental.pallas import tpu as pltpu

F32 = jnp.float32
BF16 = jnp.bfloat16
MESH = pl.DeviceIdType.MESH

N_HEADS = 16
HEAD_DIM = 64
CHUNK = 128
GM_GROUPS = 8
RMS_EPS = 1e-6
LN_EPS = 1e-5
ADAM_LR = 0.001
ADAM_B1 = 0.9
ADAM_B2 = 0.999
ADAM_EPS = 1e-08
ADAM_WD = 0.01
ADAM_STEP = 10
N_DEV = 8

LANES = 128
VMEM_BYTES_V7X = 64 * 1024 * 1024
VMEM_LIMIT = 56 * 1024 * 1024

TM = 512
TM_MM = 1024
TT = 1024
TQ = 512
TF = 512
MM_BLOCK_BYTES = 8 * 1024 * 1024
NEG = -1e30


def _cp(sem=None, vmem=VMEM_LIMIT):
    return pltpu.CompilerParams(dimension_semantics=sem, vmem_limit_bytes=vmem)


def _gelu(x):
    c = math.sqrt(2.0 / math.pi)
    return x * (0.5 * (1.0 + jnp.tanh(c * (x + 0.044715 * (x * x * x)))))


def _gelu_grad(x):
    c = math.sqrt(2.0 / math.pi)
    t = jnp.tanh(c * (x + 0.044715 * (x * x * x)))
    return 0.5 * (1.0 + t) + x * (0.5 * (1.0 - t * t)) * (c * (1.0 + 3.0 * 0.044715 * (x * x)))


def _sigmoid(x):
    return 1.0 / (1.0 + jnp.exp(-x))


def _dot_nt(a, b):
    return lax.dot_general(a, b, (((1,), (1,)), ((), ())), preferred_element_type=F32)


def _dot_tn(a, b):
    return lax.dot_general(a, b, (((0,), (0,)), ((), ())), preferred_element_type=F32)


def _rms_fwd(h, g, name):
    T, D = h.shape
    tm = min(TM, T)

    def body(h_ref, g_ref, o_ref):
        x = h_ref[...]
        r = lax.rsqrt(jnp.mean(x * x, axis=-1, keepdims=True) + RMS_EPS)
        o_ref[...] = ((x * r) * g_ref[...]).astype(BF16)

    return pl.pallas_call(
        body, name=name, grid=(T // tm,),
        in_specs=[pl.BlockSpec((tm, D), lambda i: (i, 0)), pl.BlockSpec((1, D), lambda i: (0, 0))],
        out_specs=pl.BlockSpec((tm, D), lambda i: (i, 0)),
        out_shape=jax.ShapeDtypeStruct((T, D), BF16),
        compiler_params=_cp(("parallel",)),
    )(h, g)


def _rms_bwd(dhn, h, g, dres, name):
    T, D = h.shape
    tm = min(TM, T)

    def body(d_ref, h_ref, g_ref, r_ref, o_ref, ob_ref, dg_ref):
        x = h_ref[...]
        d = d_ref[...]
        r = lax.rsqrt(jnp.mean(x * x, axis=-1, keepdims=True) + RMS_EPS)
        dyg = d * g_ref[...]
        dot = jnp.mean(dyg * x, axis=-1, keepdims=True)
        dh = r_ref[...] + (r * dyg - x * ((r * r * r) * dot))
        o_ref[...] = dh
        ob_ref[...] = dh.astype(BF16)
        part = jnp.sum(d * (x * r), axis=0, keepdims=True)

        @pl.when(pl.program_id(0) == 0)
        def _():
            dg_ref[...] = part

        @pl.when(pl.program_id(0) != 0)
        def _():
            dg_ref[...] += part

    blk = pl.BlockSpec((tm, D), lambda i: (i, 0))
    row = pl.BlockSpec((1, D), lambda i: (0, 0))
    return pl.pallas_call(
        body, name=name, grid=(T // tm,),
        in_specs=[blk, blk, row, blk],
        out_specs=[blk, blk, row],
        out_shape=[jax.ShapeDtypeStruct((T, D), F32), jax.ShapeDtypeStruct((T, D), BF16),
                   jax.ShapeDtypeStruct((1, D), F32)],
        compiler_params=_cp(("arbitrary",)),
    )(dhn, h, g, dres)


def _loss_head(h, tgt, g, name):
    T, D = h.shape
    tm = min(TM, T)

    def body(h_ref, t_ref, g_ref, o_ref, ob_ref, dg_ref, l_ref):
        x = h_ref[...]
        gg = g_ref[...]
        r = lax.rsqrt(jnp.mean(x * x, axis=-1, keepdims=True) + RMS_EPS)
        xr = x * r
        e = xr * gg - t_ref[...]
        lpart = 0.5 * jnp.sum(jnp.mean(e * e, axis=-1, keepdims=True), axis=0, keepdims=True)
        dy = e * (1.0 / D)
        dyg = dy * gg
        dot = jnp.mean(dyg * x, axis=-1, keepdims=True)
        dh = r * dyg - x * ((r * r * r) * dot)
        o_ref[...] = dh
        ob_ref[...] = dh.astype(BF16)
        part = jnp.sum(dy * xr, axis=0, keepdims=True)
        lrow = jnp.broadcast_to(lpart, (1, LANES))

        @pl.when(pl.program_id(0) == 0)
        def _():
            dg_ref[...] = part
            l_ref[...] = lrow

        @pl.when(pl.program_id(0) != 0)
        def _():
            dg_ref[...] += part
            l_ref[...] += lrow

    blk = pl.BlockSpec((tm, D), lambda i: (i, 0))
    row = pl.BlockSpec((1, D), lambda i: (0, 0))
    return pl.pallas_call(
        body, name=name, grid=(T // tm,),
        in_specs=[blk, blk, row],
        out_specs=[blk, blk, row, pl.BlockSpec((1, LANES), lambda i: (0, 0))],
        out_shape=[jax.ShapeDtypeStruct((T, D), F32), jax.ShapeDtypeStruct((T, D), BF16),
                   jax.ShapeDtypeStruct((1, D), F32), jax.ShapeDtypeStruct((1, LANES), F32)],
        compiler_params=_cp(("arbitrary",)),
    )(h, tgt, g)


def _mm_nn(a, b, name, out_dtype=F32, res=None):
    M, K = a.shape
    b3 = b if b.ndim == 3 else b[None]
    nb, _, w = b3.shape
    N = nb * w
    tm = min(TM_MM, M, max(256, MM_BLOCK_BYTES // (4 * N)))
    o_spec = pl.BlockSpec((tm, N), lambda i: (i, 0))
    in_specs = [pl.BlockSpec((tm, K), lambda i: (i, 0)), pl.BlockSpec((nb, K, w), lambda i: (0, 0, 0))]
    args = [a, b3]
    if res is not None:
        in_specs.append(o_spec)
        args.append(res)

    def body(*refs):
        a_ref, b_ref = refs[0], refs[1]
        o_ref = refs[-1]
        av = a_ref[...]
        for j in range(nb):
            cols = slice(j * w, (j + 1) * w)
            acc = jnp.dot(av, b_ref[j], preferred_element_type=F32)
            if res is not None:
                acc = refs[2][:, cols] + acc
            o_ref[:, cols] = acc.astype(out_dtype)

    return pl.pallas_call(
        body, name=name, grid=(M // tm,),
        in_specs=in_specs, out_specs=o_spec,
        out_shape=jax.ShapeDtypeStruct((M, N), out_dtype),
        compiler_params=_cp(("parallel",)),
    )(*args)


def _mm_nt(a_list, b, name, out_dtype=F32, add=None):
    M, kw = a_list[0].shape
    tm = min(TM, M)
    na = len(a_list)
    blocked = b.ndim == 3
    N = b.shape[1] if blocked else b.shape[0]
    b_spec = pl.BlockSpec(b.shape, lambda i: (0,) * b.ndim)
    o_spec = pl.BlockSpec((tm, N), lambda i: (i, 0))
    in_specs = [pl.BlockSpec((tm, kw), lambda i: (i, 0)) for _ in a_list] + [b_spec]
    args = list(a_list) + [b]
    if add is not None:
        in_specs.append(o_spec)
        args.append(add)

    def body(*refs):
        a_refs = refs[:na]
        b_ref = refs[na]
        o_ref = refs[-1]
        acc = refs[na + 1][...] if add is not None else None
        for s, a_ref in enumerate(a_refs):
            if blocked:
                w = b_ref.shape[2]
                per = kw // w
                parts = [_dot_nt(a_ref[:, jj * w:(jj + 1) * w], b_ref[s * per + jj]) for jj in range(per)]
            else:
                parts = [_dot_nt(a_ref[...], b_ref[:, s * kw:(s + 1) * kw])]
            for part in parts:
                acc = part if acc is None else acc + part
        o_ref[...] = acc.astype(out_dtype)

    return pl.pallas_call(
        body, name=name, grid=(M // tm,),
        in_specs=in_specs, out_specs=o_spec,
        out_shape=jax.ShapeDtypeStruct((M, N), out_dtype),
        compiler_params=_cp(("parallel",)),
    )(*args)


def _mm_tn(x, y, name, blocked_w=None, out_dtype=F32):
    T, Kx = x.shape
    N = y.shape[1]
    tt = min(TT, T)
    nt = T // tt
    tkx = min(Kx, max(LANES, MM_BLOCK_BYTES // (4 * N)))
    if blocked_w is not None:
        blk_shape, full_shape = (N // blocked_w, tkx, blocked_w), (N // blocked_w, Kx, blocked_w)
        o_spec = pl.BlockSpec(blk_shape, lambda i, t: (0, i, 0))
    else:
        blk_shape, full_shape = (tkx, N), (Kx, N)
        o_spec = pl.BlockSpec(blk_shape, lambda i, t: (i, 0))

    def body(x_ref, y_ref, o_ref, acc_ref):
        part = _dot_tn(x_ref[...], y_ref[...])
        t = pl.program_id(1)
        if blocked_w is None:
            pieces = [(slice(None), part)]
        else:
            pieces = [(j, part[:, j * blocked_w:(j + 1) * blocked_w]) for j in range(N // blocked_w)]

        @pl.when(t == 0)
        def _():
            for idx, pj in pieces:
                acc_ref[idx] = pj

        @pl.when(t != 0)
        def _():
            for idx, pj in pieces:
                acc_ref[idx] += pj

        @pl.when(t == nt - 1)
        def _():
            o_ref[...] = acc_ref[...].astype(out_dtype)

    return pl.pallas_call(
        body, name=name, grid=(Kx // tkx, nt),
        in_specs=[pl.BlockSpec((tt, tkx), lambda i, t: (t, i)),
                  pl.BlockSpec((tt, N), lambda i, t: (t, 0))],
        out_specs=o_spec, out_shape=jax.ShapeDtypeStruct(full_shape, out_dtype),
        scratch_shapes=[pltpu.VMEM(blk_shape, F32)],
        compiler_params=_cp(("parallel", "arbitrary")),
    )(x, y)


def _sgu_pieces(z, lng, lnb, wc, bs_ref):
    E = z.shape[1] // 2
    gd = E // GM_GROUPS
    zu, zv = z[:, :E], z[:, E:]
    u = _gelu(zu)
    v = _gelu(zv)
    mu = jnp.mean(v, axis=-1, keepdims=True)
    xc = v - mu
    rs = lax.rsqrt(jnp.mean(xc * xc, axis=-1, keepdims=True) + LN_EPS)
    xhat = xc * rs
    vln = xhat * lng + lnb
    s = []
    for g in range(GM_GROUPS):
        vg = vln[:, g * gd:(g + 1) * gd].astype(BF16)
        s.append(jnp.dot(wc[g], vg, preferred_element_type=F32) + bs_ref[g])
    return zu, zv, u, xhat, rs, vln, s


def _causal_ws(ws_ref):
    t = lax.broadcasted_iota(jnp.int32, (CHUNK, CHUNK), 0)
    s = lax.broadcasted_iota(jnp.int32, (CHUNK, CHUNK), 1)
    tri = t >= s
    return [jnp.where(tri, ws_ref[g], 0.0).astype(BF16) for g in range(GM_GROUPS)], tri


def _sgu_fwd(z, lng, lnb, ws, bs, name):
    T, E2 = z.shape
    E = E2 // 2
    gd = E // GM_GROUPS
    tm = min(2 * CHUNK, T)

    def body(z_ref, lng_ref, lnb_ref, ws_ref, bs_ref, o_ref):
        wc, _ = _causal_ws(ws_ref)
        for c in range(tm // CHUNK):
            rows = slice(c * CHUNK, (c + 1) * CHUNK)
            _, _, u, _, _, _, s = _sgu_pieces(z_ref[rows, :], lng_ref[...], lnb_ref[...], wc, bs_ref)
            for g in range(GM_GROUPS):
                cols = slice(g * gd, (g + 1) * gd)
                o_ref[rows, cols] = (u[:, cols] * s[g]).astype(BF16)

    full = lambda shape: pl.BlockSpec(shape, lambda i: (0,) * len(shape))
    return pl.pallas_call(
        body, name=name, grid=(T // tm,),
        in_specs=[pl.BlockSpec((tm, E2), lambda i: (i, 0)), full((1, E)), full((1, E)),
                  full((GM_GROUPS, CHUNK, CHUNK)), full((GM_GROUPS, CHUNK, 1))],
        out_specs=pl.BlockSpec((tm, E), lambda i: (i, 0)),
        out_shape=jax.ShapeDtypeStruct((T, E), BF16),
        compiler_params=_cp(("parallel",)),
    )(z, lng, lnb, ws, bs)


def _sgu_bwd(z, dg, lng, lnb, ws, bs, name):
    T, E2 = z.shape
    E = E2 // 2
    gd = E // GM_GROUPS
    tm = min(2 * CHUNK, T)
    nsteps = T // tm

    def body(z_ref, dg_ref, lng_ref, lnb_ref, ws_ref, bs_ref, dz_ref, dlng_ref, dlnb_ref, dws_ref, dbs_ref):
        i = pl.program_id(0)

        @pl.when(i == 0)
        def _():
            dlng_ref[...] = jnp.zeros_like(dlng_ref)
            dlnb_ref[...] = jnp.zeros_like(dlnb_ref)
            dws_ref[...] = jnp.zeros_like(dws_ref)
            dbs_ref[...] = jnp.zeros_like(dbs_ref)

        wc, tri = _causal_ws(ws_ref)
        lng_v = lng_ref[...]
        for c in range(tm // CHUNK):
            rows = slice(c * CHUNK, (c + 1) * CHUNK)
            zu, zv, u, xhat, rs, vln, s = _sgu_pieces(z_ref[rows, :], lng_v, lnb_ref[...], wc, bs_ref)
            dgc = dg_ref[rows, :].astype(F32)
            du, dvln = [], []
            for g in range(GM_GROUPS):
                cols = slice(g * gd, (g + 1) * gd)
                dgg = dgc[:, cols]
                du.append(dgg * s[g])
                ds = dgg * u[:, cols]
                dsb = ds.astype(BF16)
                dws_ref[g] += _dot_nt(dsb, vln[:, cols].astype(BF16))
                dbs_ref[g] += jnp.sum(ds, axis=-1, keepdims=True)
                dvln.append(_dot_tn(wc[g], dsb))
            du = jnp.concatenate(du, axis=1)
            dvln = jnp.concatenate(dvln, axis=1)
            dlng_ref[...] += jnp.sum(dvln * xhat, axis=0, keepdims=True)
            dlnb_ref[...] += jnp.sum(dvln, axis=0, keepdims=True)
            dxh = dvln * lng_v
            m1 = jnp.mean(dxh, axis=-1, keepdims=True)
            m2 = jnp.mean(dxh * xhat, axis=-1, keepdims=True)
            dv = rs * (dxh - m1 - xhat * m2)
            dz_ref[rows, :E] = (du * _gelu_grad(zu)).astype(BF16)
            dz_ref[rows, E:] = (dv * _gelu_grad(zv)).astype(BF16)

        @pl.when(i == nsteps - 1)
        def _():
            for g in range(GM_GROUPS):
                dws_ref[g] = jnp.where(tri, dws_ref[g], 0.0)

    full = lambda shape: pl.BlockSpec(shape, lambda i: (0,) * len(shape))
    return pl.pallas_call(
        body, name=name, grid=(nsteps,),
        in_specs=[pl.BlockSpec((tm, E2), lambda i: (i, 0)), pl.BlockSpec((tm, E), lambda i: (i, 0)),
                  full((1, E)), full((1, E)), full((GM_GROUPS, CHUNK, CHUNK)), full((GM_GROUPS, CHUNK, 1))],
        out_specs=[pl.BlockSpec((tm, E2), lambda i: (i, 0)), full((1, E)), full((1, E)),
                   full((GM_GROUPS, CHUNK, CHUNK)), full((GM_GROUPS, CHUNK, 1))],
        out_shape=[jax.ShapeDtypeStruct((T, E2), BF16), jax.ShapeDtypeStruct((1, E), F32),
                   jax.ShapeDtypeStruct((1, E), F32), jax.ShapeDtypeStruct((GM_GROUPS, CHUNK, CHUNK), F32),
                   jax.ShapeDtypeStruct((GM_GROUPS, CHUNK, 1), F32)],
        compiler_params=_cp(("arbitrary",)),
    )(z, dg, lng, lnb, ws, bs)


HALO = 16


def _conv_taps(a_ext, w_ref, b_ref):
    n = a_ext.shape[0]
    am1 = pltpu.roll(a_ext, 1, 0)
    am2 = pltpu.roll(a_ext, 2, 0)
    del n
    return ((b_ref[...] + am2 * w_ref[0:1, :]) + am1 * w_ref[1:2, :]) + a_ext * w_ref[2:3, :], am1, am2


def _ffn_mid_fwd(au, cw, cb, name):
    T, F = au.shape[0], au.shape[1] // 2
    tm, tf = min(TM, T), min(TF, F)
    hb = tm // HALO
    nf = F // tf

    def body(a_ref, ap_ref, u_ref, w_ref, b_ref, o_ref):
        i = pl.program_id(1)
        prev = jnp.where(i == 0, 0.0, ap_ref[...])
        ext = jnp.concatenate([prev, a_ref[...]], axis=0)
        conv, _, _ = _conv_taps(ext, w_ref, b_ref)
        conv = conv[HALO:, :]
        o_ref[...] = ((conv * _sigmoid(conv)) * u_ref[...]).astype(BF16)

    main = pl.BlockSpec((tm, tf), lambda f, i: (i, f))
    return pl.pallas_call(
        body, name=name, grid=(nf, T // tm),
        in_specs=[main, pl.BlockSpec((HALO, tf), lambda f, i: (jnp.maximum(i * hb - 1, 0), f)),
                  pl.BlockSpec((tm, tf), lambda f, i: (i, nf + f)),
                  pl.BlockSpec((3, tf), lambda f, i: (0, f)), pl.BlockSpec((1, tf), lambda f, i: (0, f))],
        out_specs=main, out_shape=jax.ShapeDtypeStruct((T, F), BF16),
        compiler_params=_cp(("parallel", "parallel")),
    )(au, au, au, cw, cb)


def _ffn_mid_bwd(au, dact, cw, cb, name):
    T, F = au.shape[0], au.shape[1] // 2
    tm, tf = min(TM, T), min(TF, F)
    hb = tm // HALO
    nt = T // tm
    nf = F // tf
    last_h = T // HALO - 1

    def body(a_ref, ap_ref, an_ref, u_ref, un_ref, d_ref, dn_ref, w_ref, b_ref, da_ref, du_ref, dcw_ref, dcb_ref):
        i = pl.program_id(1)
        prev = jnp.where(i == 0, 0.0, ap_ref[...])
        a_main = a_ref[...]
        a_ext = jnp.concatenate([prev, a_main, an_ref[...]], axis=0)
        conv, am1, am2 = _conv_taps(a_ext, w_ref, b_ref)
        conv = conv[HALO:, :]
        sig = _sigmoid(conv)
        u_ext = jnp.concatenate([u_ref[...], un_ref[...]], axis=0)
        d_ext = jnp.concatenate([d_ref[...], dn_ref[...]], axis=0).astype(F32)
        n = tm + HALO
        row = lax.broadcasted_iota(jnp.int32, (n, 1), 0)
        live = jnp.logical_or(row < tm, i < nt - 1)
        dconv = jnp.where(live, d_ext * u_ext * (sig * (1.0 + conv * (1.0 - sig))), 0.0)
        du_ref[...] = (d_ext[:tm, :] * (conv[:tm, :] * sig[:tm, :])).astype(BF16)
        dp1 = pltpu.roll(dconv, n - 1, 0)[:tm, :]
        dp2 = pltpu.roll(dconv, n - 2, 0)[:tm, :]
        dc = dconv[:tm, :]
        da_ref[...] = ((dc * w_ref[2:3, :] + dp1 * w_ref[1:2, :]) + dp2 * w_ref[0:1, :]).astype(BF16)
        g2 = jnp.sum(dc * a_main, axis=0, keepdims=True)
        g1 = jnp.sum(dc * am1[HALO:HALO + tm, :], axis=0, keepdims=True)
        g0 = jnp.sum(dc * am2[HALO:HALO + tm, :], axis=0, keepdims=True)
        gb = jnp.sum(dc, axis=0, keepdims=True)

        @pl.when(i == 0)
        def _():
            dcw_ref[...] = jnp.zeros_like(dcw_ref)
            dcb_ref[...] = jnp.zeros_like(dcb_ref)

        dcw_ref[0:1, :] += g0
        dcw_ref[1:2, :] += g1
        dcw_ref[2:3, :] += g2
        dcb_ref[...] += gb

    main = pl.BlockSpec((tm, tf), lambda f, i: (i, f))
    prev = pl.BlockSpec((HALO, tf), lambda f, i: (jnp.maximum(i * hb - 1, 0), f))
    nxt = pl.BlockSpec((HALO, tf), lambda f, i: (jnp.minimum((i + 1) * hb, last_h), f))
    main_u = pl.BlockSpec((tm, tf), lambda f, i: (i, nf + f))
    nxt_u = pl.BlockSpec((HALO, tf), lambda f, i: (jnp.minimum((i + 1) * hb, last_h), nf + f))
    return pl.pallas_call(
        body, name=name, grid=(nf, nt),
        in_specs=[main, prev, nxt, main_u, nxt_u, main, nxt,
                  pl.BlockSpec((3, tf), lambda f, i: (0, f)), pl.BlockSpec((1, tf), lambda f, i: (0, f))],
        out_specs=[main, main, pl.BlockSpec((3, tf), lambda f, i: (0, f)), pl.BlockSpec((1, tf), lambda f, i: (0, f))],
        out_shape=[jax.ShapeDtypeStruct((T, F), BF16), jax.ShapeDtypeStruct((T, F), BF16),
                   jax.ShapeDtypeStruct((3, F), F32), jax.ShapeDtypeStruct((1, F), F32)],
        compiler_params=_cp(("parallel", "arbitrary")),
    )(au, au, au, au, au, dact, dact, cw, cb)


def _split3(x):
    hi = x.astype(BF16)
    r1 = x - hi.astype(F32)
    mid = r1.astype(BF16)
    lo = (r1 - mid.astype(F32)).astype(BF16)
    return hi, mid, lo


def _tri_ones(n, upper):
    r = lax.broadcasted_iota(jnp.int32, (n, n), 0)
    c = lax.broadcasted_iota(jnp.int32, (n, n), 1)
    return jnp.where((r <= c) if upper else (r >= c), 1.0, 0.0).astype(BF16)


def _gate_scan(f, bf, name):
    T = f.shape[0]
    tm = min(256, T)

    def body(f_ref, b_ref, cp_ref, sn_ref, carry_ref):
        i = pl.program_id(0)

        @pl.when(i == 0)
        def _():
            carry_ref[...] = jnp.zeros_like(carry_ref)

        x = f_ref[...] + b_ref[...]
        e = jnp.exp(-jnp.abs(x))
        logf = jnp.minimum(x, 0.0) - jnp.log(1.0 + e)
        sn_ref[...] = jnp.where(x >= 0.0, e / (1.0 + e), 1.0 / (1.0 + e))
        tri = _tri_ones(tm, upper=False)
        c = carry_ref[...]
        for piece in _split3(logf):
            c = c + jnp.dot(tri, piece, preferred_element_type=F32)
        carry_ref[...] += jnp.sum(logf, axis=0, keepdims=True)
        hi, mid, lo = _split3(c)
        cp_ref[:, 0:LANES] = hi
        cp_ref[:, LANES:2 * LANES] = mid
        cp_ref[:, 2 * LANES:3 * LANES] = lo

    return pl.pallas_call(
        body, name=name, grid=(T // tm,),
        in_specs=[pl.BlockSpec((tm, LANES), lambda i: (i, 0)), pl.BlockSpec((1, LANES), lambda i: (0, 0))],
        out_specs=[pl.BlockSpec((tm, 3 * LANES), lambda i: (i, 0)), pl.BlockSpec((tm, LANES), lambda i: (i, 0))],
        out_shape=[jax.ShapeDtypeStruct((T, 3 * LANES), BF16), jax.ShapeDtypeStruct((T, LANES), F32)],
        scratch_shapes=[pltpu.VMEM((1, LANES), F32)],
        compiler_params=_cp(("arbitrary",)),
    )(f, bf)


def _gate_scan_bwd(dcq, dck, sneg, name):
    T = dcq.shape[0]
    tm = min(256, T)
    n = T // tm

    def body(dcq_ref, dck_ref, sn_ref, df_ref, db_ref, carry_ref):
        i = pl.program_id(0)

        @pl.when(i == 0)
        def _():
            carry_ref[...] = jnp.zeros_like(carry_ref)
            db_ref[...] = jnp.zeros_like(db_ref)

        tri = _tri_ones(tm, upper=True)
        dcb = dcq_ref[...] - dck_ref[...]
        acc = carry_ref[...]
        for piece in _split3(dcb):
            acc = acc + jnp.dot(tri, piece, preferred_element_type=F32)
        carry_ref[...] += jnp.sum(dcb, axis=0, keepdims=True)
        df = acc * sn_ref[...]
        df_ref[...] = df.astype(BF16)
        db_ref[...] += jnp.sum(df, axis=0, keepdims=True)

    rev = pl.BlockSpec((tm, LANES), lambda i: (n - 1 - i, 0))
    return pl.pallas_call(
        body, name=name, grid=(n,),
        in_specs=[rev, rev, rev],
        out_specs=[rev, pl.BlockSpec((1, LANES), lambda i: (0, 0))],
        out_shape=[jax.ShapeDtypeStruct((T, LANES), BF16), jax.ShapeDtypeStruct((1, LANES), F32)],
        scratch_shapes=[pltpu.VMEM((1, LANES), F32)],
        compiler_params=_cp(("arbitrary",)),
    )(dcq, dck, sneg)


def _qk_proj(hn, w_pad, cp, sel, const, scale, name):
    T, D = hn.shape
    H = w_pad.shape[1] // LANES
    tm = min(TM_MM, T)

    def body(a_ref, w_ref, cp_ref, sel_ref, c_ref, o_ref):
        acc = jnp.dot(a_ref[...], w_ref[...], preferred_element_type=F32)
        if scale != 1.0:
            acc = acc * scale
        acc = acc + jnp.dot(cp_ref[...], sel_ref[...], preferred_element_type=F32) + c_ref[...]
        o_ref[0] = acc[:, :LANES].astype(BF16)
        o_ref[1] = acc[:, LANES:].astype(BF16)

    return pl.pallas_call(
        body, name=name, grid=(T // tm, H // 2),
        in_specs=[pl.BlockSpec((tm, D), lambda i, p: (i, 0)), pl.BlockSpec((D, 2 * LANES), lambda i, p: (0, p)),
                  pl.BlockSpec((tm, 3 * LANES), lambda i, p: (i, 0)),
                  pl.BlockSpec((None, 3 * LANES, 2 * LANES), lambda i, p: (p, 0, 0)),
                  pl.BlockSpec((None, 1, 2 * LANES), lambda i, p: (p, 0, 0))],
        out_specs=pl.BlockSpec((2, tm, LANES), lambda i, p: (p, i, 0)),
        out_shape=jax.ShapeDtypeStruct((H, T, LANES), BF16),
        compiler_params=_cp(("parallel", "arbitrary")),
    )(hn, w_pad, cp, sel, const)


def _lane_lo():
    return lax.broadcasted_iota(jnp.int32, (1, LANES), 1) < HEAD_DIM


def _attn_fwd(qp, kp, v, name):
    H, T, _ = qp.shape
    tq = min(TQ, T)
    nrep = tq // LANES

    def body(q_ref, k_ref, v_ref, o_ref, o32_ref, lse_ref, m_sc, l_sc, acc_sc):
        i = pl.program_id(1)
        m_sc[...] = jnp.full(m_sc.shape, NEG, F32)
        l_sc[...] = jnp.zeros_like(l_sc)
        acc_sc[...] = jnp.zeros_like(acc_sc)

        def step(j, masked):
            off = pl.multiple_of(j * tq, tq)
            vblk = v_ref[pl.ds(off, tq), :]
            s_all = [_dot_nt(q_ref[h], k_ref[h, pl.ds(off, tq), :]) for h in range(2)]
            for h in range(2):
                s = s_all[h]
                tiles = [s[:, c * LANES:(c + 1) * LANES] for c in range(nrep)]
                if masked:
                    r = lax.broadcasted_iota(jnp.int32, (tq, LANES), 0)
                    cc = lax.broadcasted_iota(jnp.int32, (tq, LANES), 1)
                    tiles = [jnp.where(r >= cc + c * LANES, t, NEG) for c, t in enumerate(tiles)]
                mt = tiles[0]
                for t in tiles[1:]:
                    mt = jnp.maximum(mt, t)
                m_prev = m_sc[h]
                m_new = jnp.maximum(m_prev, jnp.max(mt, axis=-1, keepdims=True))
                alpha = jnp.exp(m_prev - m_new)
                ps = [jnp.exp(t - m_new) for t in tiles]
                psum = ps[0]
                for pt in ps[1:]:
                    psum = psum + pt
                p16 = jnp.concatenate([pt.astype(BF16) for pt in ps], axis=1)
                l_sc[h] = alpha * l_sc[h] + psum
                acc_sc[h] = alpha * acc_sc[h] + jnp.dot(p16, vblk, preferred_element_type=F32)
                m_sc[h] = m_new

        def loop_body(j, carry):
            step(j, False)
            return carry

        lax.fori_loop(0, i, loop_body, 0)
        step(i, True)
        lo = _lane_lo()
        l0 = jnp.sum(l_sc[0], axis=-1, keepdims=True)
        l1 = jnp.sum(l_sc[1], axis=-1, keepdims=True)
        o = jnp.where(lo, acc_sc[0] / l0, acc_sc[1] / l1)
        o_ref[...] = o.astype(BF16)
        o32_ref[...] = o
        lse_ref[...] = jnp.where(lo, m_sc[0] + jnp.log(l0), m_sc[1] + jnp.log(l1))

    oblk = pl.BlockSpec((tq, LANES), lambda p, i: (i, p))
    return pl.pallas_call(
        body, name=name, grid=(H // 2, T // tq),
        in_specs=[pl.BlockSpec((2, tq, LANES), lambda p, i: (p, i, 0)),
                  pl.BlockSpec((2, T, LANES), lambda p, i: (p, 0, 0)),
                  pl.BlockSpec((T, LANES), lambda p, i: (0, p))],
        out_specs=[oblk, oblk, pl.BlockSpec((None, tq, LANES), lambda p, i: (p, i, 0))],
        out_shape=[jax.ShapeDtypeStruct((T, H * HEAD_DIM), BF16), jax.ShapeDtypeStruct((T, H * HEAD_DIM), F32),
                   jax.ShapeDtypeStruct((H // 2, T, LANES), F32)],
        scratch_shapes=[pltpu.VMEM((2, tq, LANES), F32), pltpu.VMEM((2, tq, LANES), F32),
                        pltpu.VMEM((2, tq, LANES), F32)],
        compiler_params=_cp(("parallel", "arbitrary")),
    )(qp, kp, v)


def _attn_bwd(qp, kp, v, o, do, lse, scale, name):
    H, T, _ = qp.shape
    tq = min(TQ, T)
    nq = T // tq
    nrep = tq // LANES

    def body(q_ref, k_ref, v_ref, o_ref, do_ref, lse_ref, dq_ref, dk_ref, dv_ref, dqe_ref, dke_ref, dk_sc, dv_sc, dq_sc):
        i = pl.program_id(1)

        @pl.when(i == 0)
        def _():
            dk_sc[...] = jnp.zeros_like(dk_sc)
            dv_sc[...] = jnp.zeros_like(dv_sc)

        dq_sc[...] = jnp.zeros_like(dq_sc)

        lo = _lane_lo()
        dob = do_ref[...]
        dof = dob.astype(F32)
        prod = dof * o_ref[...].astype(F32)
        lse2 = lse_ref[...]
        lse2_sw = pltpu.roll(lse2, HEAD_DIM, 1)
        zero = jnp.zeros_like(dob)
        do_h = [jnp.where(lo, dob, zero), jnp.where(lo, zero, dob)]
        rep = lambda col: jnp.broadcast_to(col, (tq, LANES))
        delta = [rep(jnp.sum(jnp.where(lo, prod, 0.0), axis=-1, keepdims=True)),
                 rep(jnp.sum(jnp.where(lo, 0.0, prod), axis=-1, keepdims=True))]
        lse_h = [jnp.where(lo, lse2, lse2_sw), jnp.where(lo, lse2_sw, lse2)]
        qs = [q_ref[0], q_ref[1]]

        def step(j, masked):
            off = pl.multiple_of(j * tq, tq)
            vblk = v_ref[pl.ds(off, tq), :]
            dv_add = None
            for h in range(2):
                kblk = k_ref[h, pl.ds(off, tq), :]
                s = _dot_nt(qs[h], kblk)
                dp = _dot_nt(do_h[h], vblk)
                p16, ds16 = [], []
                for c in range(nrep):
                    cols = slice(c * LANES, (c + 1) * LANES)
                    p = jnp.exp(s[:, cols] - lse_h[h])
                    if masked:
                        r = lax.broadcasted_iota(jnp.int32, (tq, LANES), 0)
                        cc = lax.broadcasted_iota(jnp.int32, (tq, LANES), 1)
                        p = jnp.where(r >= cc + c * LANES, p, 0.0)
                    p16.append(p.astype(BF16))
                    ds16.append((p * (dp[:, cols] - delta[h])).astype(BF16))
                p16 = jnp.concatenate(p16, axis=1)
                dsb = jnp.concatenate(ds16, axis=1)
                dq_sc[h] += jnp.dot(dsb, kblk, preferred_element_type=F32)
                dk_sc[h, pl.ds(off, tq), :] += _dot_tn(dsb, qs[h])
                pv = _dot_tn(p16, do_h[h])
                dv_add = pv if dv_add is None else dv_add + pv
            dv_sc[pl.ds(off, tq), :] += dv_add

        def loop_body(j, carry):
            step(j, False)
            return carry

        lax.fori_loop(0, i, loop_body, 0)
        step(i, True)
        dq0, dq1 = dq_sc[0], dq_sc[1]
        dq_ref[...] = (jnp.where(lo, dq0, pltpu.roll(dq1, HEAD_DIM, 1)) * scale).astype(BF16)
        dqe_ref[0:8, :] = jnp.transpose(dq0)[HEAD_DIM:HEAD_DIM + 8, :]
        dqe_ref[8:16, :] = jnp.transpose(dq1)[HEAD_DIM:HEAD_DIM + 8, :]

        @pl.when(i == nq - 1)
        def _():
            dk0, dk1 = dk_sc[0], dk_sc[1]
            dk_ref[...] = jnp.where(lo, dk0, pltpu.roll(dk1, HEAD_DIM, 1)).astype(BF16)
            for h in range(2):
                for cb in range(nq):
                    blk = jnp.transpose(dk_sc[h, cb * tq:(cb + 1) * tq, :])
                    dke_ref[8 * h:8 * h + 8, cb * tq:(cb + 1) * tq] = blk[HEAD_DIM:HEAD_DIM + 8, :]
            dv_ref[...] = dv_sc[...].astype(BF16)

    qblk = pl.BlockSpec((tq, LANES), lambda p, i: (i, p))
    pair = pl.BlockSpec((T, LANES), lambda p, i: (0, p))
    tok16 = jax.ShapeDtypeStruct((T, H * HEAD_DIM), BF16)
    gate32 = jax.ShapeDtypeStruct((H // 2, 16, T), F32)
    return pl.pallas_call(
        body, name=name, grid=(H // 2, nq),
        in_specs=[pl.BlockSpec((2, tq, LANES), lambda p, i: (p, i, 0)),
                  pl.BlockSpec((2, T, LANES), lambda p, i: (p, 0, 0)),
                  pair, qblk, qblk,
                  pl.BlockSpec((None, tq, LANES), lambda p, i: (p, i, 0))],
        out_specs=[qblk, pair, pair, pl.BlockSpec((None, 16, tq), lambda p, i: (p, 0, i)),
                   pl.BlockSpec((None, 16, T), lambda p, i: (p, 0, 0))],
        out_shape=[tok16, tok16, tok16, gate32, gate32],
        scratch_shapes=[pltpu.VMEM((2, T, LANES), F32), pltpu.VMEM((T, LANES), F32),
                        pltpu.VMEM((2, tq, LANES), F32)],
        compiler_params=_cp(("parallel", "arbitrary")),
    )(qp, kp, v, o, do, lse)


def _mesh_pos():
    return lax.axis_index("x"), lax.axis_index("y"), lax.axis_index("c")


def _all_gather(arrs, name, groups=None):
    n = len(arrs)
    if groups is None:
        groups = [(a, 0) for a in range(n)]
    ng = 1 + max(g for g, _ in groups)
    per_group = [sum(1 for g, _ in groups if g == gi) for gi in range(ng)]
    first_of = [next(a for a in range(n) if groups[a][0] == gi) for gi in range(ng)]

    def body(*refs):
        ins, outs = refs[:n], refs[n:n + ng]
        send_sems, recv_sems, local_sems = refs[n + ng:]
        x, y, c = _mesh_pos()
        me, sib = (x, y, c), (x, y, 1 - c)
        chips = [(1 - x, y), (x, 1 - y), (1 - x, 1 - y)]

        def dst_of(a, px, py, pc):
            g, k = groups[a]
            return outs[g].at[N_DEV * k + 4 * px + 2 * py + pc]

        def copy(a, k, block, to, src=None):
            dst = dst_of(a, *block)
            return pltpu.make_async_remote_copy(
                src_ref=dst if src is None else src, dst_ref=dst,
                send_sem=send_sems.at[a, k], recv_sem=recv_sems.at[a, k], device_id=to, device_id_type=MESH)

        mine = [pltpu.make_async_copy(ins[a], dst_of(a, *me), local_sems.at[a]) for a in range(n)]
        for cp in mine:
            cp.start()
        first = []
        for a in range(n):
            first.append(copy(a, 0, me, sib, src=ins[a]))
            first += [copy(a, 1 + j, me, (*chip, c), src=ins[a]) for j, chip in enumerate(chips)]
        for cp in first:
            cp.start()
        passed = []
        for j, chip in enumerate(chips):
            for a in range(n):
                copy(a, 1 + j, (*chip, c), me).wait_recv()
                fwd = copy(a, 4 + j, (*chip, c), sib)
                fwd.start()
                passed.append(fwd)
        for a in range(n):
            copy(a, 0, sib, me).wait_recv()
            for j, chip in enumerate(chips):
                copy(a, 4 + j, (*chip, 1 - c), me).wait_recv()
        for cp in first + passed:
            cp.wait_send()
        for cp in mine:
            cp.wait()

    any_spec = pl.BlockSpec(memory_space=pl.ANY)
    return pl.pallas_call(
        body, name=name,
        in_specs=[any_spec] * n, out_specs=[any_spec] * ng,
        out_shape=[jax.ShapeDtypeStruct((N_DEV * per_group[gi],) + arrs[first_of[gi]].shape, arrs[first_of[gi]].dtype)
                   for gi in range(ng)],
        scratch_shapes=[pltpu.SemaphoreType.DMA((n, 7)), pltpu.SemaphoreType.DMA((n, 7)),
                        pltpu.SemaphoreType.DMA((n,))],
    )(*arrs)


def _pair_exchange(gs, name):
    n = len(gs)

    def body(*refs):
        g_refs, o_refs = refs[:n], refs[n:2 * n]
        send_sems, recv_sems = refs[2 * n:]
        x, y, c = _mesh_pos()
        sib = (x, y, 1 - c)
        copies = []
        for a in range(n):
            for j in range(4):
                copies.append(pltpu.make_async_remote_copy(
                    src_ref=g_refs[a].at[2 * j + (1 - c)], dst_ref=o_refs[a].at[j],
                    send_sem=send_sems.at[a, j], recv_sem=recv_sems.at[a, j], device_id=sib, device_id_type=MESH))
        for cp in copies:
            cp.start()
        for cp in copies:
            cp.wait_recv()
        for cp in copies:
            cp.wait_send()

    any_spec = pl.BlockSpec(memory_space=pl.ANY)
    return pl.pallas_call(
        body, name=name, in_specs=[any_spec] * n, out_specs=[any_spec] * n,
        out_shape=[jax.ShapeDtypeStruct((4,) + g.shape[1:], g.dtype) for g in gs],
        scratch_shapes=[pltpu.SemaphoreType.DMA((n, 4)), pltpu.SemaphoreType.DMA((n, 4))],
    )(*gs)


def _chip_exchange(parts, name):
    n = len(parts)

    def body(*refs):
        p_refs, o_refs = refs[:n], refs[n:2 * n]
        send_sems, recv_sems = refs[2 * n:]
        x, y, c = _mesh_pos()
        chips = [(1 - x, y), (x, 1 - y), (1 - x, 1 - y)]
        copies = []
        for a in range(n):
            for k, (px, py) in enumerate(chips):
                copies.append(pltpu.make_async_remote_copy(
                    src_ref=p_refs[a].at[2 * px + py], dst_ref=o_refs[a].at[k],
                    send_sem=send_sems.at[a, k], recv_sem=recv_sems.at[a, k], device_id=(px, py, c),
                    device_id_type=MESH))
        for cp in copies:
            cp.start()
        for cp in copies:
            cp.wait_recv()
        for cp in copies:
            cp.wait_send()

    any_spec = pl.BlockSpec(memory_space=pl.ANY)
    return pl.pallas_call(
        body, name=name, in_specs=[any_spec] * n, out_specs=[any_spec] * n,
        out_shape=[jax.ShapeDtypeStruct((3,) + p.shape[1:], p.dtype) for p in parts],
        scratch_shapes=[pltpu.SemaphoreType.DMA((n, 3)), pltpu.SemaphoreType.DMA((n, 3))],
    )(*parts)


def _row_block(R, C):
    best = None
    for d in range(16, R + 1, 16):
        if R % d == 0 and d * C <= 256 * 1024:
            best = d
    return best if best is not None else R


def _pair_add(g, recv, cidx, name):
    _, R, C = g.shape
    tr = _row_block(R, C)

    def body(c_ref, g_ref, r_ref, o_ref):
        del c_ref
        o_ref[...] = (g_ref[...].astype(F32) + r_ref[...].astype(F32)).astype(BF16)

    grid_spec = pltpu.PrefetchScalarGridSpec(
        num_scalar_prefetch=1, grid=(4, R // tr),
        in_specs=[pl.BlockSpec((None, tr, C), lambda j, i, c: (2 * j + c[0], i, 0)),
                  pl.BlockSpec((None, tr, C), lambda j, i, c: (j, i, 0))],
        out_specs=pl.BlockSpec((None, tr, C), lambda j, i, c: (j, i, 0)))
    return pl.pallas_call(
        body, name=name, grid_spec=grid_spec,
        out_shape=jax.ShapeDtypeStruct((4, R, C), BF16),
        compiler_params=_cp(("parallel", "parallel")),
    )(cidx, g, recv)


def _adamw_math(w, g, m, v):
    m = ADAM_B1 * m + (1.0 - ADAM_B1) * g
    v = ADAM_B2 * v + (1.0 - ADAM_B2) * (g * g)
    m_hat = m / (1.0 - ADAM_B1 ** ADAM_STEP)
    v_hat = v / (1.0 - ADAM_B2 ** ADAM_STEP)
    delta = -ADAM_LR * (m_hat / (jnp.sqrt(v_hat) + ADAM_EPS) + ADAM_WD * w)
    return delta, m, v


def _sum_adamw(parts, w, m, v, name, sel=None):
    R, C = w.shape
    tr = _row_block(R, C)
    specs, args = [], []
    for arr, idxs in parts:
        for idx in idxs:
            if idx < 0:
                specs.append(pl.BlockSpec((None, tr, C), lambda i, s: (s[0], i, 0)))
            else:
                specs.append(pl.BlockSpec((None, tr, C), lambda i, s, idx=idx: (idx, i, 0)))
            args.append(arr)
    npart = len(args)
    blk = pl.BlockSpec((tr, C), lambda i, s: (i, 0))

    def body(s_ref, *refs):
        del s_ref
        g = refs[0][...].astype(F32)
        for r in refs[1:npart]:
            g = g + r[...].astype(F32)
        w_ref, m_ref, v_ref, g_out, d_out, m_out, v_out = refs[npart:]
        delta, mm, vv = _adamw_math(w_ref[...], g, m_ref[...], v_ref[...])
        g_out[...] = g
        d_out[...] = delta
        m_out[...] = mm
        v_out[...] = vv

    grid_spec = pltpu.PrefetchScalarGridSpec(
        num_scalar_prefetch=1, grid=(R // tr,),
        in_specs=specs + [blk, blk, blk], out_specs=[blk] * 4)
    if sel is None:
        sel = jnp.zeros((1,), jnp.int32)
    return pl.pallas_call(
        body, name=name, grid_spec=grid_spec,
        out_shape=[jax.ShapeDtypeStruct((R, C), F32)] * 4,
        compiler_params=_cp(("parallel",)),
    )(sel, *args, w, m, v)


def _rows(a, c):
    return a.reshape(-1, c)


def _pad_rows(a, r):
    return jnp.pad(a, ((0, r - a.shape[0]), (0, 0))) if a.shape[0] != r else a


def _gate_tables():
    hp = N_HEADS // 2
    sel_q = np.zeros((hp, 3 * LANES, 2 * LANES), np.float32)
    sel_k = np.zeros((hp, 3 * LANES, 2 * LANES), np.float32)
    const_q = np.zeros((hp, 1, 2 * LANES), np.float32)
    const_k = np.zeros((hp, 1, 2 * LANES), np.float32)
    for p in range(hp):
        for hh in range(2):
            h = 2 * p + hh
            base = hh * LANES + HEAD_DIM
            for piece in range(3):
                sel_q[p, piece * LANES + h, base + piece] = 1.0
                sel_k[p, piece * LANES + h, base + 3 + piece] = -1.0
            const_k[p, 0, base:base + 3] = 1.0
            const_q[p, 0, base + 3:base + 6] = 1.0
    as_bf = lambda t: jnp.asarray(t, BF16)
    return as_bf(sel_q), as_bf(sel_k), jnp.asarray(const_q), jnp.asarray(const_k)


def _pad_heads(w):
    d = w.shape[0]
    w3 = w.reshape(d, N_HEADS, HEAD_DIM)
    return jnp.pad(w3, ((0, 0), (0, 0), (0, LANES - HEAD_DIM))).reshape(d, N_HEADS * LANES)


def kernel(x, mix_norm_g, ffn_norm_g, gm_w_in, gm_ln_g, gm_ln_b, gm_w_s, gm_b_s, gm_w_out, fox_w_qkvf, fox_b_f, fox_w_o, ffn_w_gate, ffn_w_up, ffn_conv_w, ffn_conv_b, ffn_w_down, final_norm_g, loss_target, m_mix_norm_g, m_ffn_norm_g, m_gm_w_in, m_gm_ln_g, m_gm_ln_b, m_gm_w_s, m_gm_b_s, m_gm_w_out, m_fox_w_qkvf, m_fox_b_f, m_fox_w_o, m_ffn_w_gate, m_ffn_w_up, m_ffn_conv_w, m_ffn_conv_b, m_ffn_w_down, m_final_norm_g, v_mix_norm_g, v_ffn_norm_g, v_gm_w_in, v_gm_ln_g, v_gm_ln_b, v_gm_w_s, v_gm_b_s, v_gm_w_out, v_fox_w_qkvf, v_fox_b_f, v_fox_w_o, v_ffn_w_gate, v_ffn_w_up, v_ffn_conv_w, v_ffn_conv_b, v_ffn_w_down, v_final_norm_g):
    T, D = x.shape[1], x.shape[2]
    E = gm_ln_g.shape[1]
    FF = ffn_conv_b.shape[1]
    NQKVF = 3 * D + N_HEADS
    xi, yi, ci = _mesh_pos()
    me = 4 * xi + 2 * yi + ci
    h0 = x.reshape(T, D)
    tgt = loss_target.reshape(T, D)

    nl = ffn_w_gate.shape[0]
    to16 = lambda a: a.astype(BF16)
    ag_in = [to16(gm_w_in[0]), to16(gm_w_out[0]), to16(fox_w_qkvf[0]), to16(fox_w_o[0])]
    groups = [(0, 0), (1, 0), (2, 0), (3, 0)]
    for l in range(nl):
        ag_in += [to16(ffn_w_gate[l]), to16(ffn_w_up[l])]
        groups += [(4 + l, 0), (4 + l, 1)]
    for l in range(nl):
        ag_in.append(to16(ffn_w_down[l]))
        groups.append((4 + nl + l, 0))
    n_cw_rows = ffn_conv_w.size // LANES
    cw_rows = _pad_rows(_rows(ffn_conv_w.astype(F32), LANES), 16)
    ag_in.append(cw_rows)
    groups.append((4 + 2 * nl, 0))
    gathered = _all_gather(ag_in, "ag_weights", groups)
    w_in_g = gathered[0]
    w_out_g = gathered[1].reshape(E, D)
    w_qkvf = jnp.transpose(gathered[2], (1, 0, 2)).reshape(D, NQKVF)
    w_o_g = gathered[3].reshape(D, D)
    wgu_g = gathered[4:4 + nl]
    wd_g = gathered[4 + nl:4 + 2 * nl]
    cwg = gathered[4 + 2 * nl]
    conv_w_full = jnp.transpose(cwg[:, :n_cw_rows].reshape(N_DEV, nl, 3, FF // N_DEV), (1, 2, 0, 3)).reshape(nl, 3, FF)

    def ffn_weights(l):
        return wgu_g[l], wd_g[l].reshape(FF, D)

    saved = {}

    def ffn_fwd(l, h_in):
        wgul, wdl = ffn_weights(l)
        hn = _rms_fwd(h_in, ffn_norm_g[l:l + 1], f"ffn{l}_norm")
        au = _mm_nn(hn, wgul, f"ffn{l}_gate_up")
        act = _ffn_mid_fwd(au, conv_w_full[l], ffn_conv_b[l:l + 1], f"ffn{l}_mid")
        h_out = _mm_nn(act, wdl, f"ffn{l}_down", res=h_in)
        saved[f"ffn{l}"] = (h_in, hn, au, act)
        return h_out

    bs_col = gm_b_s[0].reshape(GM_GROUPS, CHUNK, 1)
    hn0 = _rms_fwd(h0, mix_norm_g[0:1], "mix0_norm")
    z = _mm_nn(hn0, w_in_g, "gm_in")
    gu = _sgu_fwd(z, gm_ln_g, gm_ln_b, gm_w_s[0], bs_col, "gm_sgu")
    h1 = _mm_nn(gu, w_out_g, "gm_out", res=h0)
    h2 = ffn_fwd(0, h1)

    w_q, w_k, w_v = w_qkvf[:, :D], w_qkvf[:, D:2 * D], w_qkvf[:, 2 * D:3 * D]
    w_f = jnp.pad(w_qkvf[:, 3 * D:], ((0, 0), (0, LANES - N_HEADS)))
    bf_row = jnp.pad(fox_b_f, ((0, 0), (0, LANES - N_HEADS)))
    sel_q, sel_k, const_q, const_k = _gate_tables()
    scale = HEAD_DIM ** -0.5
    hn2 = _rms_fwd(h2, mix_norm_g[1:2], "mix1_norm")
    f_logit = _mm_nn(hn2, w_f, "fox_f")
    cp, sneg = _gate_scan(f_logit, bf_row, "fox_scan")
    qp = _qk_proj(hn2, _pad_heads(w_q), cp, sel_q, const_q, scale, "fox_q")
    kp = _qk_proj(hn2, _pad_heads(w_k), cp, sel_k, const_k, 1.0, "fox_k")
    vv = _mm_nn(hn2, w_v, "fox_v", out_dtype=BF16)
    o, o32, lse = _attn_fwd(qp, kp, vv, "fox_attn")
    h3 = _mm_nn(o, w_o_g, "fox_o", res=h2)
    h4 = ffn_fwd(1, h3)

    dh, dh16, d_final, loss_row = _loss_head(h4, tgt, final_norm_g.reshape(1, D), "loss_head")
    loss = lax.psum(loss_row[0, 0], ("x", "y", "c"))

    def ffn_bwd(l, dh, dh16):
        wgul, wdl = ffn_weights(l)
        h_in, hn, au, act = saved[f"ffn{l}"]
        dact = _mm_nt([dh16], wdl, f"ffn{l}_dact", out_dtype=BF16)
        d_wd = _mm_tn(act, dh16, f"ffn{l}_dwd", out_dtype=BF16)
        da, dup, d_cw, d_cb = _ffn_mid_bwd(au, dact, conv_w_full[l], ffn_conv_b[l:l + 1], f"ffn{l}_dmid")
        dhn = _mm_nt([da, dup], wgul, f"ffn{l}_dhn")
        d_wg = _mm_tn(hn, da, f"ffn{l}_dwg", blocked_w=FF // N_DEV, out_dtype=BF16)
        d_wu = _mm_tn(hn, dup, f"ffn{l}_dwu", blocked_w=FF // N_DEV, out_dtype=BF16)
        dh_in, dh_in16, d_norm = _rms_bwd(dhn, h_in, ffn_norm_g[l:l + 1], dh, f"ffn{l}_dnorm")
        return dh_in, dh_in16, dict(wd=d_wd, wg=d_wg, wu=d_wu, cw=d_cw, cb=d_cb, norm=d_norm)

    dh, dh16, g_ffn1 = ffn_bwd(1, dh, dh16)

    do = _mm_nt([dh16], w_o_g, "fox_do", out_dtype=BF16)
    d_wo = _mm_tn(o, dh16, "fox_dwo", out_dtype=BF16)
    dq, dk, dv, dqe, dke = _attn_bwd(qp, kp, vv, o32, do, lse, scale, "fox_dattn")
    gate_lane = lambda e, r: jnp.pad(jnp.transpose(e[:, r::8, :].reshape(N_HEADS, T)), ((0, 0), (0, LANES - N_HEADS)))
    df, d_bf = _gate_scan_bwd(gate_lane(dqe, 0), gate_lane(dke, 3), sneg, "fox_dscan")
    dhn = _mm_nt([df], w_f, "fox_dhn_f")
    dhn = _mm_nt([dq, dk, dv], w_qkvf[:, :3 * D], "fox_dhn_qkv", add=dhn)
    d_wq = _mm_tn(hn2, dq, "fox_dwq", out_dtype=BF16)
    d_wk = _mm_tn(hn2, dk, "fox_dwk", out_dtype=BF16)
    d_wv = _mm_tn(hn2, dv, "fox_dwv", out_dtype=BF16)
    d_wf = _mm_tn(hn2, df, "fox_dwf", out_dtype=BF16)
    d_wqkvf = jnp.concatenate([d_wq, d_wk, d_wv, d_wf[:, :N_HEADS]], axis=1)
    dh, dh16, d_mix1 = _rms_bwd(dhn, h2, mix_norm_g[1:2], dh, "mix1_dnorm")

    dh, dh16, g_ffn0 = ffn_bwd(0, dh, dh16)

    dgu = _mm_nt([dh16], w_out_g, "gm_dgu", out_dtype=BF16)
    d_wout = _mm_tn(gu, dh16, "gm_dwout", out_dtype=BF16)
    dz, d_lng, d_lnb, d_ws, d_bs = _sgu_bwd(z, dgu, gm_ln_g, gm_ln_b, gm_w_s[0], bs_col, "gm_dsgu")
    dhn = _mm_nt([dz], w_in_g, "gm_dhn")
    d_win = _mm_tn(hn0, dz, "gm_dwin", blocked_w=2 * E // N_DEV, out_dtype=BF16)
    dx, _, d_mix0 = _rms_bwd(dhn, h0, mix_norm_g[0:1], dh, "mix0_dnorm")

    big = [("gm_w_in", gm_w_in, m_gm_w_in, v_gm_w_in, d_win),
           ("gm_w_out", gm_w_out, m_gm_w_out, v_gm_w_out, d_wout.reshape(N_DEV, E // N_DEV, D)),
           ("fox_w_qkvf", fox_w_qkvf, m_fox_w_qkvf, v_fox_w_qkvf,
            jnp.transpose(d_wqkvf.reshape(D, N_DEV, NQKVF // N_DEV), (1, 0, 2))),
           ("fox_w_o", fox_w_o, m_fox_w_o, v_fox_w_o, d_wo.reshape(N_DEV, D // N_DEV, D)),
           ("ffn_w_gate", ffn_w_gate, m_ffn_w_gate, v_ffn_w_gate, jnp.concatenate([g_ffn0["wg"], g_ffn1["wg"]], axis=1)),
           ("ffn_w_up", ffn_w_up, m_ffn_w_up, v_ffn_w_up, jnp.concatenate([g_ffn0["wu"], g_ffn1["wu"]], axis=1)),
           ("ffn_w_down", ffn_w_down, m_ffn_w_down, v_ffn_w_down,
            jnp.concatenate([g_ffn0["wd"].reshape(N_DEV, FF // N_DEV, D), g_ffn1["wd"].reshape(N_DEV, FF // N_DEV, D)], axis=1))]
    cidx = ci.astype(jnp.int32).reshape(1)
    chip_idx = (2 * xi + yi).astype(jnp.int32).reshape(1)
    g_dev = [p[4] for p in big]
    recv1 = _pair_exchange(g_dev, "rs_pair")
    part = [_pair_add(g, r, cidx, f"rs_pair_add_{p[0]}") for p, g, r in zip(big, g_dev, recv1)]
    recv2 = _chip_exchange(part, "rs_chip")
    big_out = {}
    for p, pt, r2 in zip(big, part, recv2):
        shard2d = lambda a, c=pt.shape[2]: a.reshape(-1, c)
        res = _sum_adamw([(pt, [-1]), (r2, [0, 1, 2])], shard2d(p[1]), shard2d(p[2]), shard2d(p[3]),
                         f"adamw_{p[0]}", sel=chip_idx)
        big_out[p[0]] = [t.reshape(p[1].shape) for t in res]

    small = [("mix_norm_g", mix_norm_g, m_mix_norm_g, v_mix_norm_g, jnp.concatenate([d_mix0, d_mix1], axis=0)),
             ("ffn_norm_g", ffn_norm_g, m_ffn_norm_g, v_ffn_norm_g, jnp.concatenate([g_ffn0["norm"], g_ffn1["norm"]], axis=0)),
             ("gm_ln_g", gm_ln_g, m_gm_ln_g, v_gm_ln_g, d_lng),
             ("gm_ln_b", gm_ln_b, m_gm_ln_b, v_gm_ln_b, d_lnb),
             ("gm_w_s", gm_w_s, m_gm_w_s, v_gm_w_s, d_ws),
             ("gm_b_s", gm_b_s, m_gm_b_s, v_gm_b_s, d_bs),
             ("fox_b_f", fox_b_f, m_fox_b_f, v_fox_b_f, d_bf[:, :N_HEADS]),
             ("ffn_conv_b", ffn_conv_b, m_ffn_conv_b, v_ffn_conv_b, jnp.concatenate([g_ffn0["cb"], g_ffn1["cb"]], axis=0)),
             ("final_norm_g", final_norm_g, m_final_norm_g, v_final_norm_g, d_final)]
    d_cw_full = jnp.stack([g_ffn0["cw"], g_ffn1["cw"]], axis=0)

    def small_rows(a):
        flat = a.astype(F32).reshape(-1)
        n = -(-flat.size // (8 * LANES)) * (8 * LANES)
        return jnp.pad(flat, (0, n - flat.size)).reshape(-1, LANES)

    s_rows = [small_rows(p[1]).shape[0] for p in small]
    s_off = np.concatenate([[0], np.cumsum(s_rows)]).tolist()
    cw_g_rows = small_rows(d_cw_full)
    zeros_cw = jnp.zeros_like(cw_g_rows)
    cat = lambda k: jnp.concatenate([small_rows(p[k]) for p in small] + [zeros_cw], axis=0)
    g_small = jnp.concatenate([small_rows(p[4]) for p in small] + [cw_g_rows], axis=0)
    (gs_all,) = _all_gather([g_small], "ag_small_grads")
    small_out = _sum_adamw([(gs_all, list(range(N_DEV)))], cat(1), cat(2), cat(3), "adamw_small")
    gs = small_out[0]

    g_cw_full = gs[s_off[-1]:].reshape(-1)[:d_cw_full.size].reshape(d_cw_full.shape)
    g_cw = lax.dynamic_slice_in_dim(g_cw_full, me * (FF // N_DEV), FF // N_DEV, axis=2)
    cw2 = lambda a: _pad_rows(_rows(a.astype(F32), LANES), 16)
    cw_out = _sum_adamw([(cw2(g_cw)[None], [0])], cw2(ffn_conv_w), cw2(m_ffn_conv_w), cw2(v_ffn_conv_w), "adamw_conv_w")

    names = ["mix_norm_g", "ffn_norm_g", "gm_w_in", "gm_ln_g", "gm_ln_b", "gm_w_s", "gm_b_s", "gm_w_out", "fox_w_qkvf",
             "fox_b_f", "fox_w_o", "ffn_w_gate", "ffn_w_up", "ffn_conv_w", "ffn_conv_b", "ffn_w_down", "final_norm_g"]
    small_idx = {p[0]: k for k, p in enumerate(small)}

    def pick(kind, name):
        if name in big_out:
            return big_out[name][kind]
        if name == "ffn_conv_w":
            return cw_out[kind][:n_cw_rows].reshape(ffn_conv_w.shape)
        k = small_idx[name]
        shp = small[k][1].shape
        return small_out[kind][s_off[k]:s_off[k + 1]].reshape(-1)[:int(np.prod(shp))].reshape(shp)

    outs = [loss, dx.reshape(x.shape)]
    for kind in range(4):
        outs += [pick(kind, n) for n in names]
    return tuple(outs)
```

```python
import functools
import math

import numpy as np
import jax
import jax.numpy as jnp
from jax import lax
from jax.experimental import pallas as pl
from jax.experimental.pallas import tpu as pltpu

F32 = jnp.float32
BF16 = jnp.bfloat16
MESH = pl.DeviceIdType.MESH

N_HEADS = 16
HEAD_DIM = 64
CHUNK = 128
GM_GROUPS = 8
RMS_EPS = 1e-6
LN_EPS = 1e-5
ADAM_LR = 0.001
ADAM_B1 = 0.9
ADAM_B2 = 0.999
ADAM_EPS = 1e-08
ADAM_WD = 0.01
ADAM_STEP = 10
N_DEV = 8

LANES = 128
VMEM_BYTES_V7X = 64 * 1024 * 1024
VMEM_LIMIT = 56 * 1024 * 1024

TM = 512
TM_MM = 1024
TT = 1024
TQ = 512
TF = 512
MM_BLOCK_BYTES = 8 * 1024 * 1024
NEG = -1e30


def _cp(sem=None, vmem=VMEM_LIMIT):
    return pltpu.CompilerParams(dimension_semantics=sem, vmem_limit_bytes=vmem)


def _gelu(x):
    c = math.sqrt(2.0 / math.pi)
    return x * (0.5 * (1.0 + jnp.tanh(c * (x + 0.044715 * (x * x * x)))))


def _gelu_grad(x):
    c = math.sqrt(2.0 / math.pi)
    t = jnp.tanh(c * (x + 0.044715 * (x * x * x)))
    return 0.5 * (1.0 + t) + x * (0.5 * (1.0 - t * t)) * (c * (1.0 + 3.0 * 0.044715 * (x * x)))


def _sigmoid(x):
    return 1.0 / (1.0 + jnp.exp(-x))


def _dot_nt(a, b):
    return lax.dot_general(a, b, (((1,), (1,)), ((), ())), preferred_element_type=F32)


def _dot_tn(a, b):
    return lax.dot_general(a, b, (((0,), (0,)), ((), ())), preferred_element_type=F32)


def _rms_fwd(h, g, name, after=None):
    T, D = h.shape
    tm = min(TM, T)

    def body(h_ref, g_ref, *rest):
        o_ref = rest[-1]
        x = h_ref[...]
        r = lax.rsqrt(jnp.mean(x * x, axis=-1, keepdims=True) + RMS_EPS)
        o_ref[...] = ((x * r) * g_ref[...]).astype(BF16)

    in_specs = [pl.BlockSpec((tm, D), lambda i: (i, 0)), pl.BlockSpec((1, D), lambda i: (0, 0))]
    args = [h, g]
    if after is not None:
        in_specs.append(pl.BlockSpec(memory_space=pl.ANY))
        args.append(after)
    return pl.pallas_call(
        body, name=name, grid=(T // tm,),
        in_specs=in_specs,
        out_specs=pl.BlockSpec((tm, D), lambda i: (i, 0)),
        out_shape=jax.ShapeDtypeStruct((T, D), BF16),
        compiler_params=_cp(("parallel",)),
    )(*args)


def _rms_bwd(dhn, h, g, dres, name):
    T, D = h.shape
    tm = min(TM, T)

    def body(d_ref, h_ref, g_ref, r_ref, o_ref, ob_ref, dg_ref):
        x = h_ref[...]
        d = d_ref[...]
        r = lax.rsqrt(jnp.mean(x * x, axis=-1, keepdims=True) + RMS_EPS)
        dyg = d * g_ref[...]
        dot = jnp.mean(dyg * x, axis=-1, keepdims=True)
        dh = r_ref[...] + (r * dyg - x * ((r * r * r) * dot))
        o_ref[...] = dh
        ob_ref[...] = dh.astype(BF16)
        part = jnp.sum(d * (x * r), axis=0, keepdims=True)

        @pl.when(pl.program_id(0) == 0)
        def _():
            dg_ref[...] = part

        @pl.when(pl.program_id(0) != 0)
        def _():
            dg_ref[...] += part

    blk = pl.BlockSpec((tm, D), lambda i: (i, 0))
    row = pl.BlockSpec((1, D), lambda i: (0, 0))
    return pl.pallas_call(
        body, name=name, grid=(T // tm,),
        in_specs=[blk, blk, row, blk],
        out_specs=[blk, blk, row],
        out_shape=[jax.ShapeDtypeStruct((T, D), F32), jax.ShapeDtypeStruct((T, D), BF16),
                   jax.ShapeDtypeStruct((1, D), F32)],
        compiler_params=_cp(("arbitrary",)),
    )(dhn, h, g, dres)


def _loss_head(h, tgt, g, name):
    T, D = h.shape
    tm = min(TM, T)

    def body(h_ref, t_ref, g_ref, o_ref, ob_ref, dg_ref, l_ref):
        x = h_ref[...]
        gg = g_ref[...]
        r = lax.rsqrt(jnp.mean(x * x, axis=-1, keepdims=True) + RMS_EPS)
        xr = x * r
        e = xr * gg - t_ref[...]
        lpart = 0.5 * jnp.sum(jnp.mean(e * e, axis=-1, keepdims=True), axis=0, keepdims=True)
        dy = e * (1.0 / D)
        dyg = dy * gg
        dot = jnp.mean(dyg * x, axis=-1, keepdims=True)
        dh = r * dyg - x * ((r * r * r) * dot)
        o_ref[...] = dh
        ob_ref[...] = dh.astype(BF16)
        part = jnp.sum(dy * xr, axis=0, keepdims=True)
        lrow = jnp.broadcast_to(lpart, (1, LANES))

        @pl.when(pl.program_id(0) == 0)
        def _():
            dg_ref[...] = part
            l_ref[...] = lrow

        @pl.when(pl.program_id(0) != 0)
        def _():
            dg_ref[...] += part
            l_ref[...] += lrow

    blk = pl.BlockSpec((tm, D), lambda i: (i, 0))
    row = pl.BlockSpec((1, D), lambda i: (0, 0))
    return pl.pallas_call(
        body, name=name, grid=(T // tm,),
        in_specs=[blk, blk, row],
        out_specs=[blk, blk, row, pl.BlockSpec((1, LANES), lambda i: (0, 0))],
        out_shape=[jax.ShapeDtypeStruct((T, D), F32), jax.ShapeDtypeStruct((T, D), BF16),
                   jax.ShapeDtypeStruct((1, D), F32), jax.ShapeDtypeStruct((1, LANES), F32)],
        compiler_params=_cp(("arbitrary",)),
    )(h, tgt, g)


def _mm_nn(a, b, name, out_dtype=F32, res=None):
    M, K = a.shape
    b3 = b if b.ndim == 3 else b[None]
    nb, _, w = b3.shape
    N = nb * w
    tm = min(TM_MM, M, max(256, MM_BLOCK_BYTES // (4 * N)))
    o_spec = pl.BlockSpec((tm, N), lambda i: (i, 0))
    in_specs = [pl.BlockSpec((tm, K), lambda i: (i, 0)), pl.BlockSpec((nb, K, w), lambda i: (0, 0, 0))]
    args = [a, b3]
    if res is not None:
        in_specs.append(o_spec)
        args.append(res)

    def body(*refs):
        a_ref, b_ref = refs[0], refs[1]
        o_ref = refs[-1]
        av = a_ref[...]
        for j in range(nb):
            cols = slice(j * w, (j + 1) * w)
            acc = jnp.dot(av, b_ref[j], preferred_element_type=F32)
            if res is not None:
                acc = refs[2][:, cols] + acc
            o_ref[:, cols] = acc.astype(out_dtype)

    return pl.pallas_call(
        body, name=name, grid=(M // tm,),
        in_specs=in_specs, out_specs=o_spec,
        out_shape=jax.ShapeDtypeStruct((M, N), out_dtype),
        compiler_params=_cp(("parallel",)),
    )(*args)


def _mm_nt(a_list, b, name, out_dtype=F32, add=None):
    M, kw = a_list[0].shape
    tm = min(TM, M)
    na = len(a_list)
    blocked = b.ndim == 3
    N = b.shape[1] if blocked else b.shape[0]
    b_spec = pl.BlockSpec(b.shape, lambda i: (0,) * b.ndim)
    o_spec = pl.BlockSpec((tm, N), lambda i: (i, 0))
    in_specs = [pl.BlockSpec((tm, kw), lambda i: (i, 0)) for _ in a_list] + [b_spec]
    args = list(a_list) + [b]
    if add is not None:
        in_specs.append(o_spec)
        args.append(add)

    def body(*refs):
        a_refs = refs[:na]
        b_ref = refs[na]
        o_ref = refs[-1]
        acc = refs[na + 1][...] if add is not None else None
        for s, a_ref in enumerate(a_refs):
            if blocked:
                w = b_ref.shape[2]
                per = kw // w
                parts = [_dot_nt(a_ref[:, jj * w:(jj + 1) * w], b_ref[s * per + jj]) for jj in range(per)]
            else:
                parts = [_dot_nt(a_ref[...], b_ref[:, s * kw:(s + 1) * kw])]
            for part in parts:
                acc = part if acc is None else acc + part
        o_ref[...] = acc.astype(out_dtype)

    return pl.pallas_call(
        body, name=name, grid=(M // tm,),
        in_specs=in_specs, out_specs=o_spec,
        out_shape=jax.ShapeDtypeStruct((M, N), out_dtype),
        compiler_params=_cp(("parallel",)),
    )(*args)


def _mm_tn(x, y, name, blocked_w=None, out_dtype=F32):
    T, Kx = x.shape
    N = y.shape[1]
    tt = min(TT, T)
    nt = T // tt
    tkx = min(Kx, max(LANES, MM_BLOCK_BYTES // (4 * N)))
    if blocked_w is not None:
        blk_shape, full_shape = (N // blocked_w, tkx, blocked_w), (N // blocked_w, Kx, blocked_w)
        o_spec = pl.BlockSpec(blk_shape, lambda i, t: (0, i, 0))
    else:
        blk_shape, full_shape = (tkx, N), (Kx, N)
        o_spec = pl.BlockSpec(blk_shape, lambda i, t: (i, 0))

    def body(x_ref, y_ref, o_ref, acc_ref):
        part = _dot_tn(x_ref[...], y_ref[...])
        t = pl.program_id(1)
        if blocked_w is None:
            pieces = [(slice(None), part)]
        else:
            pieces = [(j, part[:, j * blocked_w:(j + 1) * blocked_w]) for j in range(N // blocked_w)]

        @pl.when(t == 0)
        def _():
            for idx, pj in pieces:
                acc_ref[idx] = pj

        @pl.when(t != 0)
        def _():
            for idx, pj in pieces:
                acc_ref[idx] += pj

        @pl.when(t == nt - 1)
        def _():
            o_ref[...] = acc_ref[...].astype(out_dtype)

    return pl.pallas_call(
        body, name=name, grid=(Kx // tkx, nt),
        in_specs=[pl.BlockSpec((tt, tkx), lambda i, t: (t, i)),
                  pl.BlockSpec((tt, N), lambda i, t: (t, 0))],
        out_specs=o_spec, out_shape=jax.ShapeDtypeStruct(full_shape, out_dtype),
        scratch_shapes=[pltpu.VMEM(blk_shape, F32)],
        compiler_params=_cp(("parallel", "arbitrary")),
    )(x, y)


def _sgu_pieces(z, lng, lnb, wc, bs_ref):
    E = z.shape[1] // 2
    gd = E // GM_GROUPS
    zu, zv = z[:, :E], z[:, E:]
    u = _gelu(zu)
    v = _gelu(zv)
    mu = jnp.mean(v, axis=-1, keepdims=True)
    xc = v - mu
    rs = lax.rsqrt(jnp.mean(xc * xc, axis=-1, keepdims=True) + LN_EPS)
    xhat = xc * rs
    vln = xhat * lng + lnb
    s = []
    for g in range(GM_GROUPS):
        vg = vln[:, g * gd:(g + 1) * gd].astype(BF16)
        s.append(jnp.dot(wc[g], vg, preferred_element_type=F32) + bs_ref[g])
    return zu, zv, u, xhat, rs, vln, s


def _causal_ws(ws_ref):
    t = lax.broadcasted_iota(jnp.int32, (CHUNK, CHUNK), 0)
    s = lax.broadcasted_iota(jnp.int32, (CHUNK, CHUNK), 1)
    tri = t >= s
    return [jnp.where(tri, ws_ref[g], 0.0).astype(BF16) for g in range(GM_GROUPS)], tri


def _sgu_fwd(z, lng, lnb, ws, bs, name):
    T, E2 = z.shape
    E = E2 // 2
    gd = E // GM_GROUPS
    tm = min(2 * CHUNK, T)

    def body(z_ref, lng_ref, lnb_ref, ws_ref, bs_ref, o_ref):
        wc, _ = _causal_ws(ws_ref)
        for c in range(tm // CHUNK):
            rows = slice(c * CHUNK, (c + 1) * CHUNK)
            _, _, u, _, _, _, s = _sgu_pieces(z_ref[rows, :], lng_ref[...], lnb_ref[...], wc, bs_ref)
            for g in range(GM_GROUPS):
                cols = slice(g * gd, (g + 1) * gd)
                o_ref[rows, cols] = (u[:, cols] * s[g]).astype(BF16)

    full = lambda shape: pl.BlockSpec(shape, lambda i: (0,) * len(shape))
    return pl.pallas_call(
        body, name=name, grid=(T // tm,),
        in_specs=[pl.BlockSpec((tm, E2), lambda i: (i, 0)), full((1, E)), full((1, E)),
                  full((GM_GROUPS, CHUNK, CHUNK)), full((GM_GROUPS, CHUNK, 1))],
        out_specs=pl.BlockSpec((tm, E), lambda i: (i, 0)),
        out_shape=jax.ShapeDtypeStruct((T, E), BF16),
        compiler_params=_cp(("parallel",)),
    )(z, lng, lnb, ws, bs)


def _sgu_bwd(z, dg, lng, lnb, ws, bs, name):
    T, E2 = z.shape
    E = E2 // 2
    gd = E // GM_GROUPS
    tm = min(2 * CHUNK, T)
    nsteps = T // tm

    def body(z_ref, dg_ref, lng_ref, lnb_ref, ws_ref, bs_ref, dz_ref, dlng_ref, dlnb_ref, dws_ref, dbs_ref):
        i = pl.program_id(0)

        @pl.when(i == 0)
        def _():
            dlng_ref[...] = jnp.zeros_like(dlng_ref)
            dlnb_ref[...] = jnp.zeros_like(dlnb_ref)
            dws_ref[...] = jnp.zeros_like(dws_ref)
            dbs_ref[...] = jnp.zeros_like(dbs_ref)

        wc, tri = _causal_ws(ws_ref)
        lng_v = lng_ref[...]
        for c in range(tm // CHUNK):
            rows = slice(c * CHUNK, (c + 1) * CHUNK)
            zu, zv, u, xhat, rs, vln, s = _sgu_pieces(z_ref[rows, :], lng_v, lnb_ref[...], wc, bs_ref)
            dgc = dg_ref[rows, :].astype(F32)
            du, dvln = [], []
            for g in range(GM_GROUPS):
                cols = slice(g * gd, (g + 1) * gd)
                dgg = dgc[:, cols]
                du.append(dgg * s[g])
                ds = dgg * u[:, cols]
                dsb = ds.astype(BF16)
                dws_ref[g] += _dot_nt(dsb, vln[:, cols].astype(BF16))
                dbs_ref[g] += jnp.sum(ds, axis=-1, keepdims=True)
                dvln.append(_dot_tn(wc[g], dsb))
            du = jnp.concatenate(du, axis=1)
            dvln = jnp.concatenate(dvln, axis=1)
            dlng_ref[...] += jnp.sum(dvln * xhat, axis=0, keepdims=True)
            dlnb_ref[...] += jnp.sum(dvln, axis=0, keepdims=True)
            dxh = dvln * lng_v
            m1 = jnp.mean(dxh, axis=-1, keepdims=True)
            m2 = jnp.mean(dxh * xhat, axis=-1, keepdims=True)
            dv = rs * (dxh - m1 - xhat * m2)
            dz_ref[rows, :E] = (du * _gelu_grad(zu)).astype(BF16)
            dz_ref[rows, E:] = (dv * _gelu_grad(zv)).astype(BF16)

        @pl.when(i == nsteps - 1)
        def _():
            for g in range(GM_GROUPS):
                dws_ref[g] = jnp.where(tri, dws_ref[g], 0.0)

    full = lambda shape: pl.BlockSpec(shape, lambda i: (0,) * len(shape))
    return pl.pallas_call(
        body, name=name, grid=(nsteps,),
        in_specs=[pl.BlockSpec((tm, E2), lambda i: (i, 0)), pl.BlockSpec((tm, E), lambda i: (i, 0)),
                  full((1, E)), full((1, E)), full((GM_GROUPS, CHUNK, CHUNK)), full((GM_GROUPS, CHUNK, 1))],
        out_specs=[pl.BlockSpec((tm, E2), lambda i: (i, 0)), full((1, E)), full((1, E)),
                   full((GM_GROUPS, CHUNK, CHUNK)), full((GM_GROUPS, CHUNK, 1))],
        out_shape=[jax.ShapeDtypeStruct((T, E2), BF16), jax.ShapeDtypeStruct((1, E), F32),
                   jax.ShapeDtypeStruct((1, E), F32), jax.ShapeDtypeStruct((GM_GROUPS, CHUNK, CHUNK), F32),
                   jax.ShapeDtypeStruct((GM_GROUPS, CHUNK, 1), F32)],
        compiler_params=_cp(("arbitrary",)),
    )(z, dg, lng, lnb, ws, bs)


HALO = 16


def _conv_taps(a_ext, w_ref, b_ref):
    n = a_ext.shape[0]
    am1 = pltpu.roll(a_ext, 1, 0)
    am2 = pltpu.roll(a_ext, 2, 0)
    del n
    return ((b_ref[...] + am2 * w_ref[0:1, :]) + am1 * w_ref[1:2, :]) + a_ext * w_ref[2:3, :], am1, am2


def _ffn_mid_fwd(au, cw, cb, name):
    T, F = au.shape[0], au.shape[1] // 2
    tm, tf = min(TM, T), min(TF, F)
    hb = tm // HALO
    nf = F // tf

    def body(a_ref, ap_ref, u_ref, w_ref, b_ref, o_ref):
        i = pl.program_id(1)
        prev = jnp.where(i == 0, 0.0, ap_ref[...])
        ext = jnp.concatenate([prev, a_ref[...]], axis=0)
        conv, _, _ = _conv_taps(ext, w_ref, b_ref)
        conv = conv[HALO:, :]
        o_ref[...] = ((conv * _sigmoid(conv)) * u_ref[...]).astype(BF16)

    main = pl.BlockSpec((tm, tf), lambda f, i: (i, f))
    return pl.pallas_call(
        body, name=name, grid=(nf, T // tm),
        in_specs=[main, pl.BlockSpec((HALO, tf), lambda f, i: (jnp.maximum(i * hb - 1, 0), f)),
                  pl.BlockSpec((tm, tf), lambda f, i: (i, nf + f)),
                  pl.BlockSpec((3, tf), lambda f, i: (0, f)), pl.BlockSpec((1, tf), lambda f, i: (0, f))],
        out_specs=main, out_shape=jax.ShapeDtypeStruct((T, F), BF16),
        compiler_params=_cp(("parallel", "parallel")),
    )(au, au, au, cw, cb)


def _ffn_mid_bwd(au, dact, cw, cb, name):
    T, F = au.shape[0], au.shape[1] // 2
    tm, tf = min(TM, T), min(TF, F)
    hb = tm // HALO
    nt = T // tm
    nf = F // tf
    last_h = T // HALO - 1

    def body(a_ref, ap_ref, an_ref, u_ref, un_ref, d_ref, dn_ref, w_ref, b_ref, da_ref, du_ref, dcw_ref, dcb_ref):
        i = pl.program_id(1)
        prev = jnp.where(i == 0, 0.0, ap_ref[...])
        a_main = a_ref[...]
        a_ext = jnp.concatenate([prev, a_main, an_ref[...]], axis=0)
        conv, am1, am2 = _conv_taps(a_ext, w_ref, b_ref)
        conv = conv[HALO:, :]
        sig = _sigmoid(conv)
        u_ext = jnp.concatenate([u_ref[...], un_ref[...]], axis=0)
        d_ext = jnp.concatenate([d_ref[...], dn_ref[...]], axis=0).astype(F32)
        n = tm + HALO
        row = lax.broadcasted_iota(jnp.int32, (n, 1), 0)
        live = jnp.logical_or(row < tm, i < nt - 1)
        dconv = jnp.where(live, d_ext * u_ext * (sig * (1.0 + conv * (1.0 - sig))), 0.0)
        du_ref[...] = (d_ext[:tm, :] * (conv[:tm, :] * sig[:tm, :])).astype(BF16)
        dp1 = pltpu.roll(dconv, n - 1, 0)[:tm, :]
        dp2 = pltpu.roll(dconv, n - 2, 0)[:tm, :]
        dc = dconv[:tm, :]
        da_ref[...] = ((dc * w_ref[2:3, :] + dp1 * w_ref[1:2, :]) + dp2 * w_ref[0:1, :]).astype(BF16)
        g2 = jnp.sum(dc * a_main, axis=0, keepdims=True)
        g1 = jnp.sum(dc * am1[HALO:HALO + tm, :], axis=0, keepdims=True)
        g0 = jnp.sum(dc * am2[HALO:HALO + tm, :], axis=0, keepdims=True)
        gb = jnp.sum(dc, axis=0, keepdims=True)

        @pl.when(i == 0)
        def _():
            dcw_ref[...] = jnp.zeros_like(dcw_ref)
            dcb_ref[...] = jnp.zeros_like(dcb_ref)

        dcw_ref[0:1, :] += g0
        dcw_ref[1:2, :] += g1
        dcw_ref[2:3, :] += g2
        dcb_ref[...] += gb

    main = pl.BlockSpec((tm, tf), lambda f, i: (i, f))
    prev = pl.BlockSpec((HALO, tf), lambda f, i: (jnp.maximum(i * hb - 1, 0), f))
    nxt = pl.BlockSpec((HALO, tf), lambda f, i: (jnp.minimum((i + 1) * hb, last_h), f))
    main_u = pl.BlockSpec((tm, tf), lambda f, i: (i, nf + f))
    nxt_u = pl.BlockSpec((HALO, tf), lambda f, i: (jnp.minimum((i + 1) * hb, last_h), nf + f))
    return pl.pallas_call(
        body, name=name, grid=(nf, nt),
        in_specs=[main, prev, nxt, main_u, nxt_u, main, nxt,
                  pl.BlockSpec((3, tf), lambda f, i: (0, f)), pl.BlockSpec((1, tf), lambda f, i: (0, f))],
        out_specs=[main, main, pl.BlockSpec((3, tf), lambda f, i: (0, f)), pl.BlockSpec((1, tf), lambda f, i: (0, f))],
        out_shape=[jax.ShapeDtypeStruct((T, F), BF16), jax.ShapeDtypeStruct((T, F), BF16),
                   jax.ShapeDtypeStruct((3, F), F32), jax.ShapeDtypeStruct((1, F), F32)],
        compiler_params=_cp(("parallel", "arbitrary")),
    )(au, au, au, au, au, dact, dact, cw, cb)


def _split3(x):
    hi = x.astype(BF16)
    r1 = x - hi.astype(F32)
    mid = r1.astype(BF16)
    lo = (r1 - mid.astype(F32)).astype(BF16)
    return hi, mid, lo


def _tri_ones(n, upper):
    r = lax.broadcasted_iota(jnp.int32, (n, n), 0)
    c = lax.broadcasted_iota(jnp.int32, (n, n), 1)
    return jnp.where((r <= c) if upper else (r >= c), 1.0, 0.0).astype(BF16)


def _gate_scan(f, bf, name):
    T = f.shape[0]
    tm = min(256, T)

    def body(f_ref, b_ref, cp_ref, sn_ref, carry_ref):
        i = pl.program_id(0)

        @pl.when(i == 0)
        def _():
            carry_ref[...] = jnp.zeros_like(carry_ref)

        x = f_ref[...] + b_ref[...]
        e = jnp.exp(-jnp.abs(x))
        logf = jnp.minimum(x, 0.0) - jnp.log(1.0 + e)
        sn_ref[...] = jnp.where(x >= 0.0, e / (1.0 + e), 1.0 / (1.0 + e))
        tri = _tri_ones(tm, upper=False)
        c = carry_ref[...]
        for piece in _split3(logf):
            c = c + jnp.dot(tri, piece, preferred_element_type=F32)
        carry_ref[...] += jnp.sum(logf, axis=0, keepdims=True)
        hi, mid, lo = _split3(c)
        cp_ref[:, 0:LANES] = hi
        cp_ref[:, LANES:2 * LANES] = mid
        cp_ref[:, 2 * LANES:3 * LANES] = lo

    return pl.pallas_call(
        body, name=name, grid=(T // tm,),
        in_specs=[pl.BlockSpec((tm, LANES), lambda i: (i, 0)), pl.BlockSpec((1, LANES), lambda i: (0, 0))],
        out_specs=[pl.BlockSpec((tm, 3 * LANES), lambda i: (i, 0)), pl.BlockSpec((tm, LANES), lambda i: (i, 0))],
        out_shape=[jax.ShapeDtypeStruct((T, 3 * LANES), BF16), jax.ShapeDtypeStruct((T, LANES), F32)],
        scratch_shapes=[pltpu.VMEM((1, LANES), F32)],
        compiler_params=_cp(("arbitrary",)),
    )(f, bf)


def _gate_scan_bwd(dcq, dck, sneg, name):
    T = dcq.shape[0]
    tm = min(256, T)
    n = T // tm

    def body(dcq_ref, dck_ref, sn_ref, df_ref, db_ref, carry_ref):
        i = pl.program_id(0)

        @pl.when(i == 0)
        def _():
            carry_ref[...] = jnp.zeros_like(carry_ref)
            db_ref[...] = jnp.zeros_like(db_ref)

        tri = _tri_ones(tm, upper=True)
        dcb = dcq_ref[...] - dck_ref[...]
        acc = carry_ref[...]
        for piece in _split3(dcb):
            acc = acc + jnp.dot(tri, piece, preferred_element_type=F32)
        carry_ref[...] += jnp.sum(dcb, axis=0, keepdims=True)
        df = acc * sn_ref[...]
        df_ref[...] = df.astype(BF16)
        db_ref[...] += jnp.sum(df, axis=0, keepdims=True)

    rev = pl.BlockSpec((tm, LANES), lambda i: (n - 1 - i, 0))
    return pl.pallas_call(
        body, name=name, grid=(n,),
        in_specs=[rev, rev, rev],
        out_specs=[rev, pl.BlockSpec((1, LANES), lambda i: (0, 0))],
        out_shape=[jax.ShapeDtypeStruct((T, LANES), BF16), jax.ShapeDtypeStruct((1, LANES), F32)],
        scratch_shapes=[pltpu.VMEM((1, LANES), F32)],
        compiler_params=_cp(("arbitrary",)),
    )(dcq, dck, sneg)


def _qk_proj(hn, w_pad, cp, sel, const, scale, name):
    T, D = hn.shape
    H = w_pad.shape[1] // LANES
    tm = min(TM_MM, T)

    def body(a_ref, w_ref, cp_ref, sel_ref, c_ref, o_ref):
        acc = jnp.dot(a_ref[...], w_ref[...], preferred_element_type=F32)
        if scale != 1.0:
            acc = acc * scale
        acc = acc + jnp.dot(cp_ref[...], sel_ref[...], preferred_element_type=F32) + c_ref[...]
        o_ref[0] = acc[:, :LANES].astype(BF16)
        o_ref[1] = acc[:, LANES:].astype(BF16)

    return pl.pallas_call(
        body, name=name, grid=(T // tm, H // 2),
        in_specs=[pl.BlockSpec((tm, D), lambda i, p: (i, 0)), pl.BlockSpec((D, 2 * LANES), lambda i, p: (0, p)),
                  pl.BlockSpec((tm, 3 * LANES), lambda i, p: (i, 0)),
                  pl.BlockSpec((None, 3 * LANES, 2 * LANES), lambda i, p: (p, 0, 0)),
                  pl.BlockSpec((None, 1, 2 * LANES), lambda i, p: (p, 0, 0))],
        out_specs=pl.BlockSpec((2, tm, LANES), lambda i, p: (p, i, 0)),
        out_shape=jax.ShapeDtypeStruct((H, T, LANES), BF16),
        compiler_params=_cp(("parallel", "arbitrary")),
    )(hn, w_pad, cp, sel, const)


def _lane_lo():
    return lax.broadcasted_iota(jnp.int32, (1, LANES), 1) < HEAD_DIM


def _attn_fwd(qp, kp, v, name):
    H, T, _ = qp.shape
    tq = min(TQ, T)
    nrep = tq // LANES

    def body(q_ref, k_ref, v_ref, o_ref, o32_ref, lse_ref, m_sc, l_sc, acc_sc):
        i = pl.program_id(1)
        m_sc[...] = jnp.full(m_sc.shape, NEG, F32)
        l_sc[...] = jnp.zeros_like(l_sc)
        acc_sc[...] = jnp.zeros_like(acc_sc)

        def step(j, masked):
            off = pl.multiple_of(j * tq, tq)
            vblk = v_ref[pl.ds(off, tq), :]
            s_all = [_dot_nt(q_ref[h], k_ref[h, pl.ds(off, tq), :]) for h in range(2)]
            for h in range(2):
                s = s_all[h]
                tiles = [s[:, c * LANES:(c + 1) * LANES] for c in range(nrep)]
                if masked:
                    r = lax.broadcasted_iota(jnp.int32, (tq, LANES), 0)
                    cc = lax.broadcasted_iota(jnp.int32, (tq, LANES), 1)
                    tiles = [jnp.where(r >= cc + c * LANES, t, NEG) for c, t in enumerate(tiles)]
                mt = tiles[0]
                for t in tiles[1:]:
                    mt = jnp.maximum(mt, t)
                m_prev = m_sc[h]
                m_new = jnp.maximum(m_prev, jnp.max(mt, axis=-1, keepdims=True))
                alpha = jnp.exp(m_prev - m_new)
                ps = [jnp.exp(t - m_new) for t in tiles]
                psum = ps[0]
                for pt in ps[1:]:
                    psum = psum + pt
                p16 = jnp.concatenate([pt.astype(BF16) for pt in ps], axis=1)
                l_sc[h] = alpha * l_sc[h] + psum
                acc_sc[h] = alpha * acc_sc[h] + jnp.dot(p16, vblk, preferred_element_type=F32)
                m_sc[h] = m_new

        def loop_body(j, carry):
            step(j, False)
            return carry

        lax.fori_loop(0, i, loop_body, 0)
        step(i, True)
        lo = _lane_lo()
        l0 = jnp.sum(l_sc[0], axis=-1, keepdims=True)
        l1 = jnp.sum(l_sc[1], axis=-1, keepdims=True)
        o = jnp.where(lo, acc_sc[0] / l0, acc_sc[1] / l1)
        o_ref[...] = o.astype(BF16)
        o32_ref[...] = o
        lse_ref[...] = jnp.where(lo, m_sc[0] + jnp.log(l0), m_sc[1] + jnp.log(l1))

    oblk = pl.BlockSpec((tq, LANES), lambda p, i: (i, p))
    return pl.pallas_call(
        body, name=name, grid=(H // 2, T // tq),
        in_specs=[pl.BlockSpec((2, tq, LANES), lambda p, i: (p, i, 0)),
                  pl.BlockSpec((2, T, LANES), lambda p, i: (p, 0, 0)),
                  pl.BlockSpec((T, LANES), lambda p, i: (0, p))],
        out_specs=[oblk, oblk, pl.BlockSpec((None, tq, LANES), lambda p, i: (p, i, 0))],
        out_shape=[jax.ShapeDtypeStruct((T, H * HEAD_DIM), BF16), jax.ShapeDtypeStruct((T, H * HEAD_DIM), F32),
                   jax.ShapeDtypeStruct((H // 2, T, LANES), F32)],
        scratch_shapes=[pltpu.VMEM((2, tq, LANES), F32), pltpu.VMEM((2, tq, LANES), F32),
                        pltpu.VMEM((2, tq, LANES), F32)],
        compiler_params=_cp(("parallel", "arbitrary")),
    )(qp, kp, v)


def _attn_bwd(qp, kp, v, o, do, lse, scale, name):
    H, T, _ = qp.shape
    tq = min(TQ, T)
    nq = T // tq
    nrep = tq // LANES

    def body(q_ref, k_ref, v_ref, o_ref, do_ref, lse_ref, dq_ref, dk_ref, dv_ref, dqe_ref, dke_ref, dk_sc, dv_sc, dq_sc):
        i = pl.program_id(1)

        @pl.when(i == 0)
        def _():
            dk_sc[...] = jnp.zeros_like(dk_sc)
            dv_sc[...] = jnp.zeros_like(dv_sc)

        dq_sc[...] = jnp.zeros_like(dq_sc)

        lo = _lane_lo()
        dob = do_ref[...]
        dof = dob.astype(F32)
        prod = dof * o_ref[...].astype(F32)
        lse2 = lse_ref[...]
        lse2_sw = pltpu.roll(lse2, HEAD_DIM, 1)
        zero = jnp.zeros_like(dob)
        do_h = [jnp.where(lo, dob, zero), jnp.where(lo, zero, dob)]
        rep = lambda col: jnp.broadcast_to(col, (tq, LANES))
        delta = [rep(jnp.sum(jnp.where(lo, prod, 0.0), axis=-1, keepdims=True)),
                 rep(jnp.sum(jnp.where(lo, 0.0, prod), axis=-1, keepdims=True))]
        lse_h = [jnp.where(lo, lse2, lse2_sw), jnp.where(lo, lse2_sw, lse2)]
        qs = [q_ref[0], q_ref[1]]

        def step(j, masked):
            off = pl.multiple_of(j * tq, tq)
            vblk = v_ref[pl.ds(off, tq), :]
            dv_add = None
            for h in range(2):
                kblk = k_ref[h, pl.ds(off, tq), :]
                s = _dot_nt(qs[h], kblk)
                dp = _dot_nt(do_h[h], vblk)
                p16, ds16 = [], []
                for c in range(nrep):
                    cols = slice(c * LANES, (c + 1) * LANES)
                    p = jnp.exp(s[:, cols] - lse_h[h])
                    if masked:
                        r = lax.broadcasted_iota(jnp.int32, (tq, LANES), 0)
                        cc = lax.broadcasted_iota(jnp.int32, (tq, LANES), 1)
                        p = jnp.where(r >= cc + c * LANES, p, 0.0)
                    p16.append(p.astype(BF16))
                    ds16.append((p * (dp[:, cols] - delta[h])).astype(BF16))
                p16 = jnp.concatenate(p16, axis=1)
                dsb = jnp.concatenate(ds16, axis=1)
                dq_sc[h] += jnp.dot(dsb, kblk, preferred_element_type=F32)
                dk_sc[h, pl.ds(off, tq), :] += _dot_tn(dsb, qs[h])
                pv = _dot_tn(p16, do_h[h])
                dv_add = pv if dv_add is None else dv_add + pv
            dv_sc[pl.ds(off, tq), :] += dv_add

        def loop_body(j, carry):
            step(j, False)
            return carry

        lax.fori_loop(0, i, loop_body, 0)
        step(i, True)
        dq0, dq1 = dq_sc[0], dq_sc[1]
        dq_ref[...] = (jnp.where(lo, dq0, pltpu.roll(dq1, HEAD_DIM, 1)) * scale).astype(BF16)
        dqe_ref[0:8, :] = jnp.transpose(dq0)[HEAD_DIM:HEAD_DIM + 8, :]
        dqe_ref[8:16, :] = jnp.transpose(dq1)[HEAD_DIM:HEAD_DIM + 8, :]

        @pl.when(i == nq - 1)
        def _():
            dk0, dk1 = dk_sc[0], dk_sc[1]
            dk_ref[...] = jnp.where(lo, dk0, pltpu.roll(dk1, HEAD_DIM, 1)).astype(BF16)
            for h in range(2):
                for cb in range(nq):
                    blk = jnp.transpose(dk_sc[h, cb * tq:(cb + 1) * tq, :])
                    dke_ref[8 * h:8 * h + 8, cb * tq:(cb + 1) * tq] = blk[HEAD_DIM:HEAD_DIM + 8, :]
            dv_ref[...] = dv_sc[...].astype(BF16)

    qblk = pl.BlockSpec((tq, LANES), lambda p, i: (i, p))
    pair = pl.BlockSpec((T, LANES), lambda p, i: (0, p))
    tok16 = jax.ShapeDtypeStruct((T, H * HEAD_DIM), BF16)
    gate32 = jax.ShapeDtypeStruct((H // 2, 16, T), F32)
    return pl.pallas_call(
        body, name=name, grid=(H // 2, nq),
        in_specs=[pl.BlockSpec((2, tq, LANES), lambda p, i: (p, i, 0)),
                  pl.BlockSpec((2, T, LANES), lambda p, i: (p, 0, 0)),
                  pair, qblk, qblk,
                  pl.BlockSpec((None, tq, LANES), lambda p, i: (p, i, 0))],
        out_specs=[qblk, pair, pair, pl.BlockSpec((None, 16, tq), lambda p, i: (p, 0, i)),
                   pl.BlockSpec((None, 16, T), lambda p, i: (p, 0, 0))],
        out_shape=[tok16, tok16, tok16, gate32, gate32],
        scratch_shapes=[pltpu.VMEM((2, T, LANES), F32), pltpu.VMEM((T, LANES), F32),
                        pltpu.VMEM((2, tq, LANES), F32)],
        compiler_params=_cp(("parallel", "arbitrary")),
    )(qp, kp, v, o, do, lse)


def _mesh_pos():
    return lax.axis_index("x"), lax.axis_index("y"), lax.axis_index("c")


def _all_gather(arrs, name, groups=None):
    n = len(arrs)
    if groups is None:
        groups = [(a, 0) for a in range(n)]
    ng = 1 + max(g for g, _ in groups)
    per_group = [sum(1 for g, _ in groups if g == gi) for gi in range(ng)]
    first_of = [next(a for a in range(n) if groups[a][0] == gi) for gi in range(ng)]

    def body(*refs):
        ins, outs = refs[:n], refs[n:n + ng]
        send_sems, recv_sems, local_sems = refs[n + ng:]
        x, y, c = _mesh_pos()
        me, sib = (x, y, c), (x, y, 1 - c)
        chips = [(1 - x, y), (x, 1 - y), (1 - x, 1 - y)]

        def dst_of(a, px, py, pc):
            g, k = groups[a]
            return outs[g].at[N_DEV * k + 4 * px + 2 * py + pc]

        def copy(a, k, block, to, src=None):
            dst = dst_of(a, *block)
            return pltpu.make_async_remote_copy(
                src_ref=dst if src is None else src, dst_ref=dst,
                send_sem=send_sems.at[a, k], recv_sem=recv_sems.at[a, k], device_id=to, device_id_type=MESH)

        mine = [pltpu.make_async_copy(ins[a], dst_of(a, *me), local_sems.at[a]) for a in range(n)]
        for cp in mine:
            cp.start()
        first = []
        for a in range(n):
            first.append(copy(a, 0, me, sib, src=ins[a]))
            first += [copy(a, 1 + j, me, (*chip, c), src=ins[a]) for j, chip in enumerate(chips)]
        for cp in first:
            cp.start()
        passed = []
        for j, chip in enumerate(chips):
            for a in range(n):
                copy(a, 1 + j, (*chip, c), me).wait_recv()
                fwd = copy(a, 4 + j, (*chip, c), sib)
                fwd.start()
                passed.append(fwd)
        for a in range(n):
            copy(a, 0, sib, me).wait_recv()
            for j, chip in enumerate(chips):
                copy(a, 4 + j, (*chip, 1 - c), me).wait_recv()
        for cp in first + passed:
            cp.wait_send()
        for cp in mine:
            cp.wait()

    any_spec = pl.BlockSpec(memory_space=pl.ANY)
    return pl.pallas_call(
        body, name=name,
        in_specs=[any_spec] * n, out_specs=[any_spec] * ng,
        out_shape=[jax.ShapeDtypeStruct((N_DEV * per_group[gi],) + arrs[first_of[gi]].shape, arrs[first_of[gi]].dtype)
                   for gi in range(ng)],
        scratch_shapes=[pltpu.SemaphoreType.DMA((n, 7)), pltpu.SemaphoreType.DMA((n, 7)),
                        pltpu.SemaphoreType.DMA((n,))],
    )(*arrs)


def _pair_exchange(gs, name):
    n = len(gs)

    def body(*refs):
        g_refs, o_refs = refs[:n], refs[n:2 * n]
        send_sems, recv_sems = refs[2 * n:]
        x, y, c = _mesh_pos()
        sib = (x, y, 1 - c)
        copies = []
        for a in range(n):
            for j in range(4):
                copies.append(pltpu.make_async_remote_copy(
                    src_ref=g_refs[a].at[2 * j + (1 - c)], dst_ref=o_refs[a].at[j],
                    send_sem=send_sems.at[a, j], recv_sem=recv_sems.at[a, j], device_id=sib, device_id_type=MESH))
        for cp in copies:
            cp.start()
        for cp in copies:
            cp.wait_recv()
        for cp in copies:
            cp.wait_send()

    any_spec = pl.BlockSpec(memory_space=pl.ANY)
    return pl.pallas_call(
        body, name=name, in_specs=[any_spec] * n, out_specs=[any_spec] * n,
        out_shape=[jax.ShapeDtypeStruct((4,) + g.shape[1:], g.dtype) for g in gs],
        scratch_shapes=[pltpu.SemaphoreType.DMA((n, 4)), pltpu.SemaphoreType.DMA((n, 4))],
    )(*gs)


def _chip_exchange(parts, name):
    n = len(parts)

    def body(*refs):
        p_refs, o_refs = refs[:n], refs[n:2 * n]
        send_sems, recv_sems = refs[2 * n:]
        x, y, c = _mesh_pos()
        chips = [(1 - x, y), (x, 1 - y), (1 - x, 1 - y)]
        copies = []
        for a in range(n):
            for k, (px, py) in enumerate(chips):
                copies.append(pltpu.make_async_remote_copy(
                    src_ref=p_refs[a].at[2 * px + py], dst_ref=o_refs[a].at[k],
                    send_sem=send_sems.at[a, k], recv_sem=recv_sems.at[a, k], device_id=(px, py, c),
                    device_id_type=MESH))
        for cp in copies:
            cp.start()
        for cp in copies:
            cp.wait_recv()
        for cp in copies:
            cp.wait_send()

    any_spec = pl.BlockSpec(memory_space=pl.ANY)
    return pl.pallas_call(
        body, name=name, in_specs=[any_spec] * n, out_specs=[any_spec] * n,
        out_shape=[jax.ShapeDtypeStruct((3,) + p.shape[1:], p.dtype) for p in parts],
        scratch_shapes=[pltpu.SemaphoreType.DMA((n, 3)), pltpu.SemaphoreType.DMA((n, 3))],
    )(*parts)


HBM_SPEC = pl.BlockSpec(memory_space=pltpu.HBM)
SEM_SPEC = pl.BlockSpec(memory_space=pltpu.SEMAPHORE)
ANY_SPEC = pl.BlockSpec(memory_space=pl.ANY)
DATAFLOW_EFFECT = pltpu.SideEffectType.DATAFLOW_SIDE_EFFECTING


def _peers():
    x, y, c = _mesh_pos()
    flip = lambda v, b: 1 - v if b else v
    return [(flip(x, (k >> 2) & 1), flip(y, (k >> 1) & 1), flip(c, k & 1)) for k in range(1, N_DEV)]


def _slot(p):
    return 4 * p[0] + 2 * p[1] + p[2]


def _direct_copy(src_refs, land_refs, sems, a, k, p, land_of, dst_slot, src_slot):
    s = src_slot(a, p)
    return pltpu.make_async_remote_copy(
        src_ref=src_refs[a] if s is None else src_refs[a].at[s], dst_ref=land_refs[land_of[a]].at[dst_slot(a, k)],
        send_sem=sems[0].at[a * (N_DEV - 1) + k], recv_sem=sems[1].at[a * (N_DEV - 1) + k], device_id=p,
        device_id_type=MESH)


def _direct_start(srcs, lands, land_of, dst_slot, src_slot, after, name, collective_id):
    n, nl = len(srcs), len(lands)

    def body(*refs):
        src_refs, land_refs = refs[:n], refs[n:n + nl]
        sems = (refs[n + nl + 1], refs[n + nl + 2])
        token = refs[-1]
        peers = _peers()
        barrier = pltpu.get_barrier_semaphore()
        for p in peers:
            pl.semaphore_signal(barrier, inc=1, device_id=p, device_id_type=MESH)
        pl.semaphore_wait(barrier, N_DEV - 1)
        for a in range(n):
            for k, p in enumerate(peers):
                _direct_copy(src_refs, land_refs, sems, a, k, p, land_of, dst_slot, src_slot).start()
        token[...] = jnp.zeros_like(token)

    hbm = lambda t: pltpu.HBM(t.shape, t.dtype)
    sem_t = pltpu.SemaphoreType.DMA((n * (N_DEV - 1),))
    outs = pl.pallas_call(
        body, name=name,
        out_shape=(sem_t, sem_t, *[hbm(t) for t in srcs], *[hbm(t) for t in lands], jax.ShapeDtypeStruct((8, LANES), F32)),
        in_specs=[HBM_SPEC] * (n + nl) + [ANY_SPEC],
        out_specs=(SEM_SPEC, SEM_SPEC, *([HBM_SPEC] * (n + nl)), pl.BlockSpec(memory_space=pltpu.VMEM)),
        input_output_aliases={i: 2 + i for i in range(n + nl)},
        compiler_params=pltpu.CompilerParams(has_side_effects=DATAFLOW_EFFECT, collective_id=collective_id),
    )(*[pltpu.with_memory_space_constraint(t, pltpu.HBM) for t in srcs],
      *[pltpu.with_memory_space_constraint(t, pltpu.HBM) for t in lands], after)
    return outs[0], outs[1], list(outs[2:2 + n]), list(outs[2 + n:2 + n + nl]), outs[-1]


def _direct_wait(send_sems, recv_sems, srcs, lands, land_of, idxs, dst_slot, src_slot, after, name):
    land_ids = []
    for a in idxs:
        if land_of[a] not in land_ids:
            land_ids.append(land_of[a])
    m, ml = len(idxs), len(land_ids)
    sub_land_of = {j: land_ids.index(land_of[a]) for j, a in enumerate(idxs)}

    def body(*refs):
        src_refs, land_refs = refs[:m], refs[m:m + ml]
        ssem, rsem = refs[m + ml], refs[m + ml + 1]
        for j, a in enumerate(idxs):
            for k, p in enumerate(_peers()):
                s = src_slot(a, p)
                cp = pltpu.make_async_remote_copy(
                    src_ref=src_refs[j] if s is None else src_refs[j].at[s],
                    dst_ref=land_refs[sub_land_of[j]].at[dst_slot(a, k)],
                    send_sem=ssem.at[a * (N_DEV - 1) + k], recv_sem=rsem.at[a * (N_DEV - 1) + k], device_id=p,
                    device_id_type=MESH)
                cp.wait_send()
                cp.wait_recv()

    hbm = lambda t: pltpu.HBM(t.shape, t.dtype)
    sub_s, sub_l = [srcs[a] for a in idxs], [lands[g] for g in land_ids]
    outs = pl.pallas_call(
        body, name=name,
        out_shape=(*[hbm(t) for t in sub_s], *[hbm(t) for t in sub_l]),
        in_specs=[HBM_SPEC] * (m + ml) + [SEM_SPEC, SEM_SPEC, ANY_SPEC],
        out_specs=tuple([HBM_SPEC] * (m + ml)),
        input_output_aliases={i: i for i in range(m + ml)},
        compiler_params=pltpu.CompilerParams(has_side_effects=DATAFLOW_EFFECT),
    )(*sub_s, *sub_l, send_sems, recv_sems, after)
    return list(outs[m:])


def _row_block(R, C):
    best = None
    for d in range(16, R + 1, 16):
        if R % d == 0 and d * C <= 256 * 1024:
            best = d
    return best if best is not None else R


def _pair_add(g, recv, cidx, name):
    _, R, C = g.shape
    tr = _row_block(R, C)

    def body(c_ref, g_ref, r_ref, o_ref):
        del c_ref
        o_ref[...] = (g_ref[...].astype(F32) + r_ref[...].astype(F32)).astype(BF16)

    grid_spec = pltpu.PrefetchScalarGridSpec(
        num_scalar_prefetch=1, grid=(4, R // tr),
        in_specs=[pl.BlockSpec((None, tr, C), lambda j, i, c: (2 * j + c[0], i, 0)),
                  pl.BlockSpec((None, tr, C), lambda j, i, c: (j, i, 0))],
        out_specs=pl.BlockSpec((None, tr, C), lambda j, i, c: (j, i, 0)))
    return pl.pallas_call(
        body, name=name, grid_spec=grid_spec,
        out_shape=jax.ShapeDtypeStruct((4, R, C), BF16),
        compiler_params=_cp(("parallel", "parallel")),
    )(cidx, g, recv)


def _adamw_math(w, g, m, v):
    m = ADAM_B1 * m + (1.0 - ADAM_B1) * g
    v = ADAM_B2 * v + (1.0 - ADAM_B2) * (g * g)
    m_hat = m / (1.0 - ADAM_B1 ** ADAM_STEP)
    v_hat = v / (1.0 - ADAM_B2 ** ADAM_STEP)
    delta = -ADAM_LR * (m_hat / (jnp.sqrt(v_hat) + ADAM_EPS) + ADAM_WD * w)
    return delta, m, v


def _sum_adamw(parts, w, m, v, name, sel=None):
    R, C = w.shape
    tr = _row_block(R, C)
    specs, args = [], []
    for arr, idxs in parts:
        for idx in idxs:
            if idx < 0:
                specs.append(pl.BlockSpec((None, tr, C), lambda i, s: (s[0], i, 0)))
            else:
                specs.append(pl.BlockSpec((None, tr, C), lambda i, s, idx=idx: (idx, i, 0)))
            args.append(arr)
    npart = len(args)
    blk = pl.BlockSpec((tr, C), lambda i, s: (i, 0))

    def body(s_ref, *refs):
        del s_ref
        g = refs[0][...].astype(F32)
        for r in refs[1:npart]:
            g = g + r[...].astype(F32)
        w_ref, m_ref, v_ref, g_out, d_out, m_out, v_out = refs[npart:]
        delta, mm, vv = _adamw_math(w_ref[...], g, m_ref[...], v_ref[...])
        g_out[...] = g
        d_out[...] = delta
        m_out[...] = mm
        v_out[...] = vv

    grid_spec = pltpu.PrefetchScalarGridSpec(
        num_scalar_prefetch=1, grid=(R // tr,),
        in_specs=specs + [blk, blk, blk], out_specs=[blk] * 4)
    if sel is None:
        sel = jnp.zeros((1,), jnp.int32)
    return pl.pallas_call(
        body, name=name, grid_spec=grid_spec,
        out_shape=[jax.ShapeDtypeStruct((R, C), F32)] * 4,
        compiler_params=_cp(("parallel",)),
    )(sel, *args, w, m, v)


def _rows(a, c):
    return a.reshape(-1, c)


def _pad_rows(a, r):
    return jnp.pad(a, ((0, r - a.shape[0]), (0, 0))) if a.shape[0] != r else a


def _gate_tables():
    hp = N_HEADS // 2
    sel_q = np.zeros((hp, 3 * LANES, 2 * LANES), np.float32)
    sel_k = np.zeros((hp, 3 * LANES, 2 * LANES), np.float32)
    const_q = np.zeros((hp, 1, 2 * LANES), np.float32)
    const_k = np.zeros((hp, 1, 2 * LANES), np.float32)
    for p in range(hp):
        for hh in range(2):
            h = 2 * p + hh
            base = hh * LANES + HEAD_DIM
            for piece in range(3):
                sel_q[p, piece * LANES + h, base + piece] = 1.0
                sel_k[p, piece * LANES + h, base + 3 + piece] = -1.0
            const_k[p, 0, base:base + 3] = 1.0
            const_q[p, 0, base + 3:base + 6] = 1.0
    as_bf = lambda t: jnp.asarray(t, BF16)
    return as_bf(sel_q), as_bf(sel_k), jnp.asarray(const_q), jnp.asarray(const_k)


def _pad_heads(w):
    d = w.shape[0]
    w3 = w.reshape(d, N_HEADS, HEAD_DIM)
    return jnp.pad(w3, ((0, 0), (0, 0), (0, LANES - HEAD_DIM))).reshape(d, N_HEADS * LANES)


def kernel(x, mix_norm_g, ffn_norm_g, gm_w_in, gm_ln_g, gm_ln_b, gm_w_s, gm_b_s, gm_w_out, fox_w_qkvf, fox_b_f, fox_w_o, ffn_w_gate, ffn_w_up, ffn_conv_w, ffn_conv_b, ffn_w_down, final_norm_g, loss_target, m_mix_norm_g, m_ffn_norm_g, m_gm_w_in, m_gm_ln_g, m_gm_ln_b, m_gm_w_s, m_gm_b_s, m_gm_w_out, m_fox_w_qkvf, m_fox_b_f, m_fox_w_o, m_ffn_w_gate, m_ffn_w_up, m_ffn_conv_w, m_ffn_conv_b, m_ffn_w_down, m_final_norm_g, v_mix_norm_g, v_ffn_norm_g, v_gm_w_in, v_gm_ln_g, v_gm_ln_b, v_gm_w_s, v_gm_b_s, v_gm_w_out, v_fox_w_qkvf, v_fox_b_f, v_fox_w_o, v_ffn_w_gate, v_ffn_w_up, v_ffn_conv_w, v_ffn_conv_b, v_ffn_w_down, v_final_norm_g):
    T, D = x.shape[1], x.shape[2]
    E = gm_ln_g.shape[1]
    FF = ffn_conv_b.shape[1]
    NQKVF = 3 * D + N_HEADS
    xi, yi, ci = _mesh_pos()
    me = 4 * xi + 2 * yi + ci
    h0 = x.reshape(T, D)
    tgt = loss_target.reshape(T, D)

    nl = ffn_w_gate.shape[0]
    to16 = lambda a: a.astype(BF16)
    n_cw_rows = ffn_conv_w.size // LANES
    cw_rows = _pad_rows(_rows(ffn_conv_w.astype(F32), LANES), 16)
    w_in_g, w_out_g8, cwg = _all_gather([to16(gm_w_in[0]), to16(gm_w_out[0]), cw_rows], "ag_weights")
    w_out_g = w_out_g8.reshape(E, D)
    later, land_of, land_off, lands = [], [], [], []
    for l in range(nl):
        later += [to16(ffn_w_gate[l]), to16(ffn_w_up[l]), to16(ffn_w_down[l])]
        land_of += [2 * l, 2 * l, 2 * l + 1]
        land_off += [0, N_DEV, 0]
        lands += [lax.empty((2 * N_DEV, D, FF // N_DEV), BF16), lax.empty((N_DEV, FF // N_DEV, D), BF16)]
    later += [to16(fox_w_qkvf[0]), to16(fox_w_o[0])]
    land_of += [2 * nl, 2 * nl + 1]
    land_off += [0, 0]
    lands += [lax.empty((N_DEV, D, NQKVF // N_DEV), BF16), lax.empty((N_DEV, D // N_DEV, D), BF16)]
    ag_dst = lambda a, k: land_off[a] + _slot(_mesh_pos())
    ag_src = lambda a, p: None
    ag_send, ag_recv, later, lands, ag_token = _direct_start(later, lands, land_of, ag_dst, ag_src, w_in_g,
                                                             "ag_later_start", collective_id=1)

    def own_blocks(landed, srcs, offs):
        for s, o in zip(srcs, offs):
            landed = lax.dynamic_update_index_in_dim(landed, later[s], o + me, 0)
        return landed

    def gather_wait(idxs, after, name):
        return _direct_wait(ag_send, ag_recv, later, lands, land_of, idxs, ag_dst, ag_src, after, name)

    conv_w_full = jnp.transpose(cwg[:, :n_cw_rows].reshape(N_DEV, nl, 3, FF // N_DEV), (1, 2, 0, 3)).reshape(nl, 3, FF)

    ffn_w = {}

    def ffn_weights(l):
        return ffn_w[l]

    def land_ffn(l, gu_land, dn_land):
        ffn_w[l] = (own_blocks(gu_land, [3 * l, 3 * l + 1], [0, N_DEV]),
                    own_blocks(dn_land, [3 * l + 2], [0]).reshape(FF, D))

    saved = {}

    def ffn_fwd(l, h_in):
        wgul, wdl = ffn_weights(l)
        hn = _rms_fwd(h_in, ffn_norm_g[l:l + 1], f"ffn{l}_norm")
        au = _mm_nn(hn, wgul, f"ffn{l}_gate_up")
        act = _ffn_mid_fwd(au, conv_w_full[l], ffn_conv_b[l:l + 1], f"ffn{l}_mid")
        h_out = _mm_nn(act, wdl, f"ffn{l}_down", res=h_in)
        saved[f"ffn{l}"] = (h_in, hn, au, act)
        return h_out

    bs_col = gm_b_s[0].reshape(GM_GROUPS, CHUNK, 1)
    hn0 = _rms_fwd(h0, mix_norm_g[0:1], "mix0_norm", after=ag_token)
    z = _mm_nn(hn0, w_in_g, "gm_in")
    gu = _sgu_fwd(z, gm_ln_g, gm_ln_b, gm_w_s[0], bs_col, "gm_sgu")
    h1 = _mm_nn(gu, w_out_g, "gm_out", res=h0)
    land_ffn(0, *gather_wait([0, 1, 2], h1, "ag_ffn0_wait"))
    h2 = ffn_fwd(0, h1)

    rest = gather_wait(list(range(3, 3 * nl + 2)), h2, "ag_layer1_wait")
    for l in range(1, nl):
        land_ffn(l, rest[2 * (l - 1)], rest[2 * (l - 1) + 1])
    w_qkvf = jnp.transpose(own_blocks(rest[-2], [3 * nl], [0]), (1, 0, 2)).reshape(D, NQKVF)
    w_o_g = own_blocks(rest[-1], [3 * nl + 1], [0]).reshape(D, D)
    w_q, w_k, w_v = w_qkvf[:, :D], w_qkvf[:, D:2 * D], w_qkvf[:, 2 * D:3 * D]
    w_f = jnp.pad(w_qkvf[:, 3 * D:], ((0, 0), (0, LANES - N_HEADS)))
    bf_row = jnp.pad(fox_b_f, ((0, 0), (0, LANES - N_HEADS)))
    sel_q, sel_k, const_q, const_k = _gate_tables()
    scale = HEAD_DIM ** -0.5
    hn2 = _rms_fwd(h2, mix_norm_g[1:2], "mix1_norm")
    f_logit = _mm_nn(hn2, w_f, "fox_f")
    cp, sneg = _gate_scan(f_logit, bf_row, "fox_scan")
    qp = _qk_proj(hn2, _pad_heads(w_q), cp, sel_q, const_q, scale, "fox_q")
    kp = _qk_proj(hn2, _pad_heads(w_k), cp, sel_k, const_k, 1.0, "fox_k")
    vv = _mm_nn(hn2, w_v, "fox_v", out_dtype=BF16)
    o, o32, lse = _attn_fwd(qp, kp, vv, "fox_attn")
    h3 = _mm_nn(o, w_o_g, "fox_o", res=h2)
    h4 = ffn_fwd(1, h3)

    dh, dh16, d_final, loss_row = _loss_head(h4, tgt, final_norm_g.reshape(1, D), "loss_head")
    loss = lax.psum(loss_row[0, 0], ("x", "y", "c"))

    def ffn_bwd(l, dh, dh16):
        wgul, wdl = ffn_weights(l)
        h_in, hn, au, act = saved[f"ffn{l}"]
        dact = _mm_nt([dh16], wdl, f"ffn{l}_dact", out_dtype=BF16)
        d_wd = _mm_tn(act, dh16, f"ffn{l}_dwd", out_dtype=BF16)
        da, dup, d_cw, d_cb = _ffn_mid_bwd(au, dact, conv_w_full[l], ffn_conv_b[l:l + 1], f"ffn{l}_dmid")
        dhn = _mm_nt([da, dup], wgul, f"ffn{l}_dhn")
        d_wg = _mm_tn(hn, da, f"ffn{l}_dwg", blocked_w=FF // N_DEV, out_dtype=BF16)
        d_wu = _mm_tn(hn, dup, f"ffn{l}_dwu", blocked_w=FF // N_DEV, out_dtype=BF16)
        dh_in, dh_in16, d_norm = _rms_bwd(dhn, h_in, ffn_norm_g[l:l + 1], dh, f"ffn{l}_dnorm")
        return dh_in, dh_in16, dict(wd=d_wd, wg=d_wg, wu=d_wu, cw=d_cw, cb=d_cb, norm=d_norm)

    dh, dh16, g_ffn1 = ffn_bwd(1, dh, dh16)

    do = _mm_nt([dh16], w_o_g, "fox_do", out_dtype=BF16)
    d_wo = _mm_tn(o, dh16, "fox_dwo", out_dtype=BF16)
    dq, dk, dv, dqe, dke = _attn_bwd(qp, kp, vv, o32, do, lse, scale, "fox_dattn")
    gate_lane = lambda e, r: jnp.pad(jnp.transpose(e[:, r::8, :].reshape(N_HEADS, T)), ((0, 0), (0, LANES - N_HEADS)))
    df, d_bf = _gate_scan_bwd(gate_lane(dqe, 0), gate_lane(dke, 3), sneg, "fox_dscan")
    dhn = _mm_nt([df], w_f, "fox_dhn_f")
    dhn = _mm_nt([dq, dk, dv], w_qkvf[:, :3 * D], "fox_dhn_qkv", add=dhn)
    d_wq = _mm_tn(hn2, dq, "fox_dwq", out_dtype=BF16)
    d_wk = _mm_tn(hn2, dk, "fox_dwk", out_dtype=BF16)
    d_wv = _mm_tn(hn2, dv, "fox_dwv", out_dtype=BF16)
    d_wf = _mm_tn(hn2, df, "fox_dwf", out_dtype=BF16)
    d_wqkvf = jnp.concatenate([d_wq, d_wk, d_wv, d_wf[:, :N_HEADS]], axis=1)
    dh, dh16, d_mix1 = _rms_bwd(dhn, h2, mix_norm_g[1:2], dh, "mix1_dnorm")

    dh, dh16, g_ffn0 = ffn_bwd(0, dh, dh16)

    dgu = _mm_nt([dh16], w_out_g, "gm_dgu", out_dtype=BF16)
    d_wout = _mm_tn(gu, dh16, "gm_dwout", out_dtype=BF16)
    dz, d_lng, d_lnb, d_ws, d_bs = _sgu_bwd(z, dgu, gm_ln_g, gm_ln_b, gm_w_s[0], bs_col, "gm_dsgu")
    dhn = _mm_nt([dz], w_in_g, "gm_dhn")
    d_win = _mm_tn(hn0, dz, "gm_dwin", blocked_w=2 * E // N_DEV, out_dtype=BF16)
    dx, _, d_mix0 = _rms_bwd(dhn, h0, mix_norm_g[0:1], dh, "mix0_dnorm")

    big = [("gm_w_in", gm_w_in, m_gm_w_in, v_gm_w_in, d_win),
           ("gm_w_out", gm_w_out, m_gm_w_out, v_gm_w_out, d_wout.reshape(N_DEV, E // N_DEV, D)),
           ("fox_w_qkvf", fox_w_qkvf, m_fox_w_qkvf, v_fox_w_qkvf,
            jnp.transpose(d_wqkvf.reshape(D, N_DEV, NQKVF // N_DEV), (1, 0, 2))),
           ("fox_w_o", fox_w_o, m_fox_w_o, v_fox_w_o, d_wo.reshape(N_DEV, D // N_DEV, D)),
           ("ffn_w_gate", ffn_w_gate, m_ffn_w_gate, v_ffn_w_gate, jnp.concatenate([g_ffn0["wg"], g_ffn1["wg"]], axis=1)),
           ("ffn_w_up", ffn_w_up, m_ffn_w_up, v_ffn_w_up, jnp.concatenate([g_ffn0["wu"], g_ffn1["wu"]], axis=1)),
           ("ffn_w_down", ffn_w_down, m_ffn_w_down, v_ffn_w_down,
            jnp.concatenate([g_ffn0["wd"].reshape(N_DEV, FF // N_DEV, D), g_ffn1["wd"].reshape(N_DEV, FF // N_DEV, D)], axis=1))]
    cidx = ci.astype(jnp.int32).reshape(1)
    chip_idx = (2 * xi + yi).astype(jnp.int32).reshape(1)
    g_dev = [p[4] for p in big]
    recv1 = _pair_exchange(g_dev, "rs_pair")
    part = [_pair_add(g, r, cidx, f"rs_pair_add_{p[0]}") for p, g, r in zip(big, g_dev, recv1)]
    recv2 = _chip_exchange(part, "rs_chip")
    big_out = {}
    for p, pt, r2 in zip(big, part, recv2):
        shard2d = lambda a, c=pt.shape[2]: a.reshape(-1, c)
        res = _sum_adamw([(pt, [-1]), (r2, [0, 1, 2])], shard2d(p[1]), shard2d(p[2]), shard2d(p[3]),
                         f"adamw_{p[0]}", sel=chip_idx)
        big_out[p[0]] = [t.reshape(p[1].shape) for t in res]

    small = [("mix_norm_g", mix_norm_g, m_mix_norm_g, v_mix_norm_g, jnp.concatenate([d_mix0, d_mix1], axis=0)),
             ("ffn_norm_g", ffn_norm_g, m_ffn_norm_g, v_ffn_norm_g, jnp.concatenate([g_ffn0["norm"], g_ffn1["norm"]], axis=0)),
             ("gm_ln_g", gm_ln_g, m_gm_ln_g, v_gm_ln_g, d_lng),
             ("gm_ln_b", gm_ln_b, m_gm_ln_b, v_gm_ln_b, d_lnb),
             ("gm_w_s", gm_w_s, m_gm_w_s, v_gm_w_s, d_ws),
             ("gm_b_s", gm_b_s, m_gm_b_s, v_gm_b_s, d_bs),
             ("fox_b_f", fox_b_f, m_fox_b_f, v_fox_b_f, d_bf[:, :N_HEADS]),
             ("ffn_conv_b", ffn_conv_b, m_ffn_conv_b, v_ffn_conv_b, jnp.concatenate([g_ffn0["cb"], g_ffn1["cb"]], axis=0)),
             ("final_norm_g", final_norm_g, m_final_norm_g, v_final_norm_g, d_final)]
    d_cw_full = jnp.stack([g_ffn0["cw"], g_ffn1["cw"]], axis=0)

    def small_rows(a):
        flat = a.astype(F32).reshape(-1)
        n = -(-flat.size // (8 * LANES)) * (8 * LANES)
        return jnp.pad(flat, (0, n - flat.size)).reshape(-1, LANES)

    s_rows = [small_rows(p[1]).shape[0] for p in small]
    s_off = np.concatenate([[0], np.cumsum(s_rows)]).tolist()
    cw_g_rows = small_rows(d_cw_full)
    zeros_cw = jnp.zeros_like(cw_g_rows)
    cat = lambda k: jnp.concatenate([small_rows(p[k]) for p in small] + [zeros_cw], axis=0)
    g_small = jnp.concatenate([small_rows(p[4]) for p in small] + [cw_g_rows], axis=0)
    (gs_all,) = _all_gather([g_small], "ag_small_grads")
    small_out = _sum_adamw([(gs_all, list(range(N_DEV)))], cat(1), cat(2), cat(3), "adamw_small")
    gs = small_out[0]

    g_cw_full = gs[s_off[-1]:].reshape(-1)[:d_cw_full.size].reshape(d_cw_full.shape)
    g_cw = lax.dynamic_slice_in_dim(g_cw_full, me * (FF // N_DEV), FF // N_DEV, axis=2)
    cw2 = lambda a: _pad_rows(_rows(a.astype(F32), LANES), 16)
    cw_out = _sum_adamw([(cw2(g_cw)[None], [0])], cw2(ffn_conv_w), cw2(m_ffn_conv_w), cw2(v_ffn_conv_w), "adamw_conv_w")

    names = ["mix_norm_g", "ffn_norm_g", "gm_w_in", "gm_ln_g", "gm_ln_b", "gm_w_s", "gm_b_s", "gm_w_out", "fox_w_qkvf",
             "fox_b_f", "fox_w_o", "ffn_w_gate", "ffn_w_up", "ffn_conv_w", "ffn_conv_b", "ffn_w_down", "final_norm_g"]
    small_idx = {p[0]: k for k, p in enumerate(small)}

    def pick(kind, name):
        if name in big_out:
            return big_out[name][kind]
        if name == "ffn_conv_w":
            return cw_out[kind][:n_cw_rows].reshape(ffn_conv_w.shape)
        k = small_idx[name]
        shp = small[k][1].shape
        return small_out[kind][s_off[k]:s_off[k + 1]].reshape(-1)[:int(np.prod(shp))].reshape(shp)

    outs = [loss, dx.reshape(x.shape)]
    for kind in range(4):
        outs += [pick(kind, n) for n in names]
    return tuple(outs)
```

```python
import functools
import math

import numpy as np
import jax
import jax.numpy as jnp
from jax import lax
from jax.experimental import pallas as pl
from jax.experimental.pallas import tpu as pltpu

F32 = jnp.float32
BF16 = jnp.bfloat16
MESH = pl.DeviceIdType.MESH

N_HEADS = 16
HEAD_DIM = 64
CHUNK = 128
GM_GROUPS = 8
RMS_EPS = 1e-6
LN_EPS = 1e-5
ADAM_LR = 0.001
ADAM_B1 = 0.9
ADAM_B2 = 0.999
ADAM_EPS = 1e-08
ADAM_WD = 0.01
ADAM_STEP = 10
N_DEV = 8

LANES = 128
VMEM_BYTES_V7X = 64 * 1024 * 1024
VMEM_LIMIT = 56 * 1024 * 1024

TM = 512
TM_MM = 1024
TT = 1024
TQ = 512
TF = 512
MM_BLOCK_BYTES = 8 * 1024 * 1024
NEG = -1e30


def _cp(sem=None, vmem=VMEM_LIMIT):
    return pltpu.CompilerParams(dimension_semantics=sem, vmem_limit_bytes=vmem)


def _gelu(x):
    c = math.sqrt(2.0 / math.pi)
    return x * (0.5 * (1.0 + jnp.tanh(c * (x + 0.044715 * (x * x * x)))))


def _gelu_grad(x):
    c = math.sqrt(2.0 / math.pi)
    t = jnp.tanh(c * (x + 0.044715 * (x * x * x)))
    return 0.5 * (1.0 + t) + x * (0.5 * (1.0 - t * t)) * (c * (1.0 + 3.0 * 0.044715 * (x * x)))


def _sigmoid(x):
    return 1.0 / (1.0 + jnp.exp(-x))


def _dot_nt(a, b):
    return lax.dot_general(a, b, (((1,), (1,)), ((), ())), preferred_element_type=F32)


def _dot_tn(a, b):
    return lax.dot_general(a, b, (((0,), (0,)), ((), ())), preferred_element_type=F32)


def _rms_fwd(h, g, name, after=None):
    T, D = h.shape
    tm = min(TM, T)

    def body(h_ref, g_ref, *rest):
        o_ref = rest[-1]
        x = h_ref[...]
        r = lax.rsqrt(jnp.mean(x * x, axis=-1, keepdims=True) + RMS_EPS)
        o_ref[...] = ((x * r) * g_ref[...]).astype(BF16)

    in_specs = [pl.BlockSpec((tm, D), lambda i: (i, 0)), pl.BlockSpec((1, D), lambda i: (0, 0))]
    args = [h, g]
    if after is not None:
        in_specs.append(pl.BlockSpec(memory_space=pl.ANY))
        args.append(after)
    return pl.pallas_call(
        body, name=name, grid=(T // tm,),
        in_specs=in_specs,
        out_specs=pl.BlockSpec((tm, D), lambda i: (i, 0)),
        out_shape=jax.ShapeDtypeStruct((T, D), BF16),
        compiler_params=_cp(("parallel",)),
    )(*args)


def _rms_bwd(dhn, h, g, dres, name):
    T, D = h.shape
    tm = min(TM, T)

    def body(d_ref, h_ref, g_ref, r_ref, o_ref, ob_ref, dg_ref):
        x = h_ref[...]
        d = d_ref[...]
        r = lax.rsqrt(jnp.mean(x * x, axis=-1, keepdims=True) + RMS_EPS)
        dyg = d * g_ref[...]
        dot = jnp.mean(dyg * x, axis=-1, keepdims=True)
        dh = r_ref[...] + (r * dyg - x * ((r * r * r) * dot))
        o_ref[...] = dh
        ob_ref[...] = dh.astype(BF16)
        part = jnp.sum(d * (x * r), axis=0, keepdims=True)

        @pl.when(pl.program_id(0) == 0)
        def _():
            dg_ref[...] = part

        @pl.when(pl.program_id(0) != 0)
        def _():
            dg_ref[...] += part

    blk = pl.BlockSpec((tm, D), lambda i: (i, 0))
    row = pl.BlockSpec((1, D), lambda i: (0, 0))
    return pl.pallas_call(
        body, name=name, grid=(T // tm,),
        in_specs=[blk, blk, row, blk],
        out_specs=[blk, blk, row],
        out_shape=[jax.ShapeDtypeStruct((T, D), F32), jax.ShapeDtypeStruct((T, D), BF16),
                   jax.ShapeDtypeStruct((1, D), F32)],
        compiler_params=_cp(("arbitrary",)),
    )(dhn, h, g, dres)


def _loss_head(h, tgt, g, name):
    T, D = h.shape
    tm = min(TM, T)

    def body(h_ref, t_ref, g_ref, o_ref, ob_ref, dg_ref, l_ref):
        x = h_ref[...]
        gg = g_ref[...]
        r = lax.rsqrt(jnp.mean(x * x, axis=-1, keepdims=True) + RMS_EPS)
        xr = x * r
        e = xr * gg - t_ref[...]
        lpart = 0.5 * jnp.sum(jnp.mean(e * e, axis=-1, keepdims=True), axis=0, keepdims=True)
        dy = e * (1.0 / D)
        dyg = dy * gg
        dot = jnp.mean(dyg * x, axis=-1, keepdims=True)
        dh = r * dyg - x * ((r * r * r) * dot)
        o_ref[...] = dh
        ob_ref[...] = dh.astype(BF16)
        part = jnp.sum(dy * xr, axis=0, keepdims=True)
        lrow = jnp.broadcast_to(lpart, (1, LANES))

        @pl.when(pl.program_id(0) == 0)
        def _():
            dg_ref[...] = part
            l_ref[...] = lrow

        @pl.when(pl.program_id(0) != 0)
        def _():
            dg_ref[...] += part
            l_ref[...] += lrow

    blk = pl.BlockSpec((tm, D), lambda i: (i, 0))
    row = pl.BlockSpec((1, D), lambda i: (0, 0))
    return pl.pallas_call(
        body, name=name, grid=(T // tm,),
        in_specs=[blk, blk, row],
        out_specs=[blk, blk, row, pl.BlockSpec((1, LANES), lambda i: (0, 0))],
        out_shape=[jax.ShapeDtypeStruct((T, D), F32), jax.ShapeDtypeStruct((T, D), BF16),
                   jax.ShapeDtypeStruct((1, D), F32), jax.ShapeDtypeStruct((1, LANES), F32)],
        compiler_params=_cp(("arbitrary",)),
    )(h, tgt, g)


def _mm_nn(a, b, name, out_dtype=F32, res=None):
    M, K = a.shape
    b3 = b if b.ndim == 3 else b[None]
    nb, _, w = b3.shape
    N = nb * w
    tm = min(TM_MM, M, max(256, MM_BLOCK_BYTES // (4 * N)))
    o_spec = pl.BlockSpec((tm, N), lambda i: (i, 0))
    in_specs = [pl.BlockSpec((tm, K), lambda i: (i, 0)), pl.BlockSpec((nb, K, w), lambda i: (0, 0, 0))]
    args = [a, b3]
    if res is not None:
        in_specs.append(o_spec)
        args.append(res)

    def body(*refs):
        a_ref, b_ref = refs[0], refs[1]
        o_ref = refs[-1]
        av = a_ref[...]
        for j in range(nb):
            cols = slice(j * w, (j + 1) * w)
            acc = jnp.dot(av, b_ref[j], preferred_element_type=F32)
            if res is not None:
                acc = refs[2][:, cols] + acc
            o_ref[:, cols] = acc.astype(out_dtype)

    return pl.pallas_call(
        body, name=name, grid=(M // tm,),
        in_specs=in_specs, out_specs=o_spec,
        out_shape=jax.ShapeDtypeStruct((M, N), out_dtype),
        compiler_params=_cp(("parallel",)),
    )(*args)


def _mm_nt(a_list, b, name, out_dtype=F32, add=None, after=None):
    M, kw = a_list[0].shape
    tm = min(TM, M)
    na = len(a_list)
    blocked = b.ndim == 3
    N = b.shape[1] if blocked else b.shape[0]
    b_spec = pl.BlockSpec(b.shape, lambda i: (0,) * b.ndim)
    o_spec = pl.BlockSpec((tm, N), lambda i: (i, 0))
    in_specs = [pl.BlockSpec((tm, kw), lambda i: (i, 0)) for _ in a_list] + [b_spec]
    args = list(a_list) + [b]
    if add is not None:
        in_specs.append(o_spec)
        args.append(add)
    if after is not None:
        in_specs.append(pl.BlockSpec(memory_space=pl.ANY))
        args.append(after)

    def body(*refs):
        a_refs = refs[:na]
        b_ref = refs[na]
        o_ref = refs[-1]
        acc = refs[na + 1][...] if add is not None else None
        for s, a_ref in enumerate(a_refs):
            if blocked:
                w = b_ref.shape[2]
                per = kw // w
                parts = [_dot_nt(a_ref[:, jj * w:(jj + 1) * w], b_ref[s * per + jj]) for jj in range(per)]
            else:
                parts = [_dot_nt(a_ref[...], b_ref[:, s * kw:(s + 1) * kw])]
            for part in parts:
                acc = part if acc is None else acc + part
        o_ref[...] = acc.astype(out_dtype)

    return pl.pallas_call(
        body, name=name, grid=(M // tm,),
        in_specs=in_specs, out_specs=o_spec,
        out_shape=jax.ShapeDtypeStruct((M, N), out_dtype),
        compiler_params=_cp(("parallel",)),
    )(*args)


def _mm_tn(x, y, name, blocked_w=None, out_dtype=F32):
    T, Kx = x.shape
    N = y.shape[1]
    tt = min(TT, T)
    nt = T // tt
    tkx = min(Kx, max(LANES, MM_BLOCK_BYTES // (4 * N)))
    if blocked_w is not None:
        blk_shape, full_shape = (N // blocked_w, tkx, blocked_w), (N // blocked_w, Kx, blocked_w)
        o_spec = pl.BlockSpec(blk_shape, lambda i, t: (0, i, 0))
    else:
        blk_shape, full_shape = (tkx, N), (Kx, N)
        o_spec = pl.BlockSpec(blk_shape, lambda i, t: (i, 0))

    def body(x_ref, y_ref, o_ref, acc_ref):
        part = _dot_tn(x_ref[...], y_ref[...])
        t = pl.program_id(1)
        if blocked_w is None:
            pieces = [(slice(None), part)]
        else:
            pieces = [(j, part[:, j * blocked_w:(j + 1) * blocked_w]) for j in range(N // blocked_w)]

        @pl.when(t == 0)
        def _():
            for idx, pj in pieces:
                acc_ref[idx] = pj

        @pl.when(t != 0)
        def _():
            for idx, pj in pieces:
                acc_ref[idx] += pj

        @pl.when(t == nt - 1)
        def _():
            o_ref[...] = acc_ref[...].astype(out_dtype)

    return pl.pallas_call(
        body, name=name, grid=(Kx // tkx, nt),
        in_specs=[pl.BlockSpec((tt, tkx), lambda i, t: (t, i)),
                  pl.BlockSpec((tt, N), lambda i, t: (t, 0))],
        out_specs=o_spec, out_shape=jax.ShapeDtypeStruct(full_shape, out_dtype),
        scratch_shapes=[pltpu.VMEM(blk_shape, F32)],
        compiler_params=_cp(("parallel", "arbitrary")),
    )(x, y)


def _sgu_pieces(z, lng, lnb, wc, bs_ref):
    E = z.shape[1] // 2
    gd = E // GM_GROUPS
    zu, zv = z[:, :E], z[:, E:]
    u = _gelu(zu)
    v = _gelu(zv)
    mu = jnp.mean(v, axis=-1, keepdims=True)
    xc = v - mu
    rs = lax.rsqrt(jnp.mean(xc * xc, axis=-1, keepdims=True) + LN_EPS)
    xhat = xc * rs
    vln = xhat * lng + lnb
    s = []
    for g in range(GM_GROUPS):
        vg = vln[:, g * gd:(g + 1) * gd].astype(BF16)
        s.append(jnp.dot(wc[g], vg, preferred_element_type=F32) + bs_ref[g])
    return zu, zv, u, xhat, rs, vln, s


def _causal_ws(ws_ref):
    t = lax.broadcasted_iota(jnp.int32, (CHUNK, CHUNK), 0)
    s = lax.broadcasted_iota(jnp.int32, (CHUNK, CHUNK), 1)
    tri = t >= s
    return [jnp.where(tri, ws_ref[g], 0.0).astype(BF16) for g in range(GM_GROUPS)], tri


def _sgu_fwd(z, lng, lnb, ws, bs, name):
    T, E2 = z.shape
    E = E2 // 2
    gd = E // GM_GROUPS
    tm = min(2 * CHUNK, T)

    def body(z_ref, lng_ref, lnb_ref, ws_ref, bs_ref, o_ref):
        wc, _ = _causal_ws(ws_ref)
        for c in range(tm // CHUNK):
            rows = slice(c * CHUNK, (c + 1) * CHUNK)
            _, _, u, _, _, _, s = _sgu_pieces(z_ref[rows, :], lng_ref[...], lnb_ref[...], wc, bs_ref)
            for g in range(GM_GROUPS):
                cols = slice(g * gd, (g + 1) * gd)
                o_ref[rows, cols] = (u[:, cols] * s[g]).astype(BF16)

    full = lambda shape: pl.BlockSpec(shape, lambda i: (0,) * len(shape))
    return pl.pallas_call(
        body, name=name, grid=(T // tm,),
        in_specs=[pl.BlockSpec((tm, E2), lambda i: (i, 0)), full((1, E)), full((1, E)),
                  full((GM_GROUPS, CHUNK, CHUNK)), full((GM_GROUPS, CHUNK, 1))],
        out_specs=pl.BlockSpec((tm, E), lambda i: (i, 0)),
        out_shape=jax.ShapeDtypeStruct((T, E), BF16),
        compiler_params=_cp(("parallel",)),
    )(z, lng, lnb, ws, bs)


def _sgu_bwd(z, dg, lng, lnb, ws, bs, name):
    T, E2 = z.shape
    E = E2 // 2
    gd = E // GM_GROUPS
    tm = min(2 * CHUNK, T)
    nsteps = T // tm

    def body(z_ref, dg_ref, lng_ref, lnb_ref, ws_ref, bs_ref, dz_ref, dlng_ref, dlnb_ref, dws_ref, dbs_ref):
        i = pl.program_id(0)

        @pl.when(i == 0)
        def _():
            dlng_ref[...] = jnp.zeros_like(dlng_ref)
            dlnb_ref[...] = jnp.zeros_like(dlnb_ref)
            dws_ref[...] = jnp.zeros_like(dws_ref)
            dbs_ref[...] = jnp.zeros_like(dbs_ref)

        wc, tri = _causal_ws(ws_ref)
        lng_v = lng_ref[...]
        for c in range(tm // CHUNK):
            rows = slice(c * CHUNK, (c + 1) * CHUNK)
            zu, zv, u, xhat, rs, vln, s = _sgu_pieces(z_ref[rows, :], lng_v, lnb_ref[...], wc, bs_ref)
            dgc = dg_ref[rows, :].astype(F32)
            du, dvln = [], []
            for g in range(GM_GROUPS):
                cols = slice(g * gd, (g + 1) * gd)
                dgg = dgc[:, cols]
                du.append(dgg * s[g])
                ds = dgg * u[:, cols]
                dsb = ds.astype(BF16)
                dws_ref[g] += _dot_nt(dsb, vln[:, cols].astype(BF16))
                dbs_ref[g] += jnp.sum(ds, axis=-1, keepdims=True)
                dvln.append(_dot_tn(wc[g], dsb))
            du = jnp.concatenate(du, axis=1)
            dvln = jnp.concatenate(dvln, axis=1)
            dlng_ref[...] += jnp.sum(dvln * xhat, axis=0, keepdims=True)
            dlnb_ref[...] += jnp.sum(dvln, axis=0, keepdims=True)
            dxh = dvln * lng_v
            m1 = jnp.mean(dxh, axis=-1, keepdims=True)
            m2 = jnp.mean(dxh * xhat, axis=-1, keepdims=True)
            dv = rs * (dxh - m1 - xhat * m2)
            dz_ref[rows, :E] = (du * _gelu_grad(zu)).astype(BF16)
            dz_ref[rows, E:] = (dv * _gelu_grad(zv)).astype(BF16)

        @pl.when(i == nsteps - 1)
        def _():
            for g in range(GM_GROUPS):
                dws_ref[g] = jnp.where(tri, dws_ref[g], 0.0)

    full = lambda shape: pl.BlockSpec(shape, lambda i: (0,) * len(shape))
    return pl.pallas_call(
        body, name=name, grid=(nsteps,),
        in_specs=[pl.BlockSpec((tm, E2), lambda i: (i, 0)), pl.BlockSpec((tm, E), lambda i: (i, 0)),
                  full((1, E)), full((1, E)), full((GM_GROUPS, CHUNK, CHUNK)), full((GM_GROUPS, CHUNK, 1))],
        out_specs=[pl.BlockSpec((tm, E2), lambda i: (i, 0)), full((1, E)), full((1, E)),
                   full((GM_GROUPS, CHUNK, CHUNK)), full((GM_GROUPS, CHUNK, 1))],
        out_shape=[jax.ShapeDtypeStruct((T, E2), BF16), jax.ShapeDtypeStruct((1, E), F32),
                   jax.ShapeDtypeStruct((1, E), F32), jax.ShapeDtypeStruct((GM_GROUPS, CHUNK, CHUNK), F32),
                   jax.ShapeDtypeStruct((GM_GROUPS, CHUNK, 1), F32)],
        compiler_params=_cp(("arbitrary",)),
    )(z, dg, lng, lnb, ws, bs)


HALO = 16


def _conv_taps(a_ext, w_ref, b_ref):
    n = a_ext.shape[0]
    am1 = pltpu.roll(a_ext, 1, 0)
    am2 = pltpu.roll(a_ext, 2, 0)
    del n
    return ((b_ref[...] + am2 * w_ref[0:1, :]) + am1 * w_ref[1:2, :]) + a_ext * w_ref[2:3, :], am1, am2


def _ffn_mid_fwd(au, cw, cb, name):
    T, F = au.shape[0], au.shape[1] // 2
    tm, tf = min(TM, T), min(TF, F)
    hb = tm // HALO
    nf = F // tf

    def body(a_ref, ap_ref, u_ref, w_ref, b_ref, o_ref):
        i = pl.program_id(1)
        prev = jnp.where(i == 0, 0.0, ap_ref[...])
        ext = jnp.concatenate([prev, a_ref[...]], axis=0)
        conv, _, _ = _conv_taps(ext, w_ref, b_ref)
        conv = conv[HALO:, :]
        o_ref[...] = ((conv * _sigmoid(conv)) * u_ref[...]).astype(BF16)

    main = pl.BlockSpec((tm, tf), lambda f, i: (i, f))
    return pl.pallas_call(
        body, name=name, grid=(nf, T // tm),
        in_specs=[main, pl.BlockSpec((HALO, tf), lambda f, i: (jnp.maximum(i * hb - 1, 0), f)),
                  pl.BlockSpec((tm, tf), lambda f, i: (i, nf + f)),
                  pl.BlockSpec((3, tf), lambda f, i: (0, f)), pl.BlockSpec((1, tf), lambda f, i: (0, f))],
        out_specs=main, out_shape=jax.ShapeDtypeStruct((T, F), BF16),
        compiler_params=_cp(("parallel", "parallel")),
    )(au, au, au, cw, cb)


def _ffn_mid_bwd(au, dact, cw, cb, name):
    T, F = au.shape[0], au.shape[1] // 2
    tm, tf = min(TM, T), min(TF, F)
    hb = tm // HALO
    nt = T // tm
    nf = F // tf
    last_h = T // HALO - 1

    def body(a_ref, ap_ref, an_ref, u_ref, un_ref, d_ref, dn_ref, w_ref, b_ref, da_ref, du_ref, dcw_ref, dcb_ref):
        i = pl.program_id(1)
        prev = jnp.where(i == 0, 0.0, ap_ref[...])
        a_main = a_ref[...]
        a_ext = jnp.concatenate([prev, a_main, an_ref[...]], axis=0)
        conv, am1, am2 = _conv_taps(a_ext, w_ref, b_ref)
        conv = conv[HALO:, :]
        sig = _sigmoid(conv)
        u_ext = jnp.concatenate([u_ref[...], un_ref[...]], axis=0)
        d_ext = jnp.concatenate([d_ref[...], dn_ref[...]], axis=0).astype(F32)
        n = tm + HALO
        row = lax.broadcasted_iota(jnp.int32, (n, 1), 0)
        live = jnp.logical_or(row < tm, i < nt - 1)
        dconv = jnp.where(live, d_ext * u_ext * (sig * (1.0 + conv * (1.0 - sig))), 0.0)
        du_ref[...] = (d_ext[:tm, :] * (conv[:tm, :] * sig[:tm, :])).astype(BF16)
        dp1 = pltpu.roll(dconv, n - 1, 0)[:tm, :]
        dp2 = pltpu.roll(dconv, n - 2, 0)[:tm, :]
        dc = dconv[:tm, :]
        da_ref[...] = ((dc * w_ref[2:3, :] + dp1 * w_ref[1:2, :]) + dp2 * w_ref[0:1, :]).astype(BF16)
        g2 = jnp.sum(dc * a_main, axis=0, keepdims=True)
        g1 = jnp.sum(dc * am1[HALO:HALO + tm, :], axis=0, keepdims=True)
        g0 = jnp.sum(dc * am2[HALO:HALO + tm, :], axis=0, keepdims=True)
        gb = jnp.sum(dc, axis=0, keepdims=True)

        @pl.when(i == 0)
        def _():
            dcw_ref[...] = jnp.zeros_like(dcw_ref)
            dcb_ref[...] = jnp.zeros_like(dcb_ref)

        dcw_ref[0:1, :] += g0
        dcw_ref[1:2, :] += g1
        dcw_ref[2:3, :] += g2
        dcb_ref[...] += gb

    main = pl.BlockSpec((tm, tf), lambda f, i: (i, f))
    prev = pl.BlockSpec((HALO, tf), lambda f, i: (jnp.maximum(i * hb - 1, 0), f))
    nxt = pl.BlockSpec((HALO, tf), lambda f, i: (jnp.minimum((i + 1) * hb, last_h), f))
    main_u = pl.BlockSpec((tm, tf), lambda f, i: (i, nf + f))
    nxt_u = pl.BlockSpec((HALO, tf), lambda f, i: (jnp.minimum((i + 1) * hb, last_h), nf + f))
    return pl.pallas_call(
        body, name=name, grid=(nf, nt),
        in_specs=[main, prev, nxt, main_u, nxt_u, main, nxt,
                  pl.BlockSpec((3, tf), lambda f, i: (0, f)), pl.BlockSpec((1, tf), lambda f, i: (0, f))],
        out_specs=[main, main, pl.BlockSpec((3, tf), lambda f, i: (0, f)), pl.BlockSpec((1, tf), lambda f, i: (0, f))],
        out_shape=[jax.ShapeDtypeStruct((T, F), BF16), jax.ShapeDtypeStruct((T, F), BF16),
                   jax.ShapeDtypeStruct((3, F), F32), jax.ShapeDtypeStruct((1, F), F32)],
        compiler_params=_cp(("parallel", "arbitrary")),
    )(au, au, au, au, au, dact, dact, cw, cb)


def _split3(x):
    hi = x.astype(BF16)
    r1 = x - hi.astype(F32)
    mid = r1.astype(BF16)
    lo = (r1 - mid.astype(F32)).astype(BF16)
    return hi, mid, lo


def _tri_ones(n, upper):
    r = lax.broadcasted_iota(jnp.int32, (n, n), 0)
    c = lax.broadcasted_iota(jnp.int32, (n, n), 1)
    return jnp.where((r <= c) if upper else (r >= c), 1.0, 0.0).astype(BF16)


def _gate_scan(f, bf, name):
    T = f.shape[0]
    tm = min(256, T)

    def body(f_ref, b_ref, cp_ref, sn_ref, carry_ref):
        i = pl.program_id(0)

        @pl.when(i == 0)
        def _():
            carry_ref[...] = jnp.zeros_like(carry_ref)

        x = f_ref[...] + b_ref[...]
        e = jnp.exp(-jnp.abs(x))
        logf = jnp.minimum(x, 0.0) - jnp.log(1.0 + e)
        sn_ref[...] = jnp.where(x >= 0.0, e / (1.0 + e), 1.0 / (1.0 + e))
        tri = _tri_ones(tm, upper=False)
        c = carry_ref[...]
        for piece in _split3(logf):
            c = c + jnp.dot(tri, piece, preferred_element_type=F32)
        carry_ref[...] += jnp.sum(logf, axis=0, keepdims=True)
        hi, mid, lo = _split3(c)
        cp_ref[:, 0:LANES] = hi
        cp_ref[:, LANES:2 * LANES] = mid
        cp_ref[:, 2 * LANES:3 * LANES] = lo

    return pl.pallas_call(
        body, name=name, grid=(T // tm,),
        in_specs=[pl.BlockSpec((tm, LANES), lambda i: (i, 0)), pl.BlockSpec((1, LANES), lambda i: (0, 0))],
        out_specs=[pl.BlockSpec((tm, 3 * LANES), lambda i: (i, 0)), pl.BlockSpec((tm, LANES), lambda i: (i, 0))],
        out_shape=[jax.ShapeDtypeStruct((T, 3 * LANES), BF16), jax.ShapeDtypeStruct((T, LANES), F32)],
        scratch_shapes=[pltpu.VMEM((1, LANES), F32)],
        compiler_params=_cp(("arbitrary",)),
    )(f, bf)


def _gate_scan_bwd(dcq, dck, sneg, name):
    T = dcq.shape[0]
    tm = min(256, T)
    n = T // tm

    def body(dcq_ref, dck_ref, sn_ref, df_ref, db_ref, carry_ref):
        i = pl.program_id(0)

        @pl.when(i == 0)
        def _():
            carry_ref[...] = jnp.zeros_like(carry_ref)
            db_ref[...] = jnp.zeros_like(db_ref)

        tri = _tri_ones(tm, upper=True)
        dcb = dcq_ref[...] - dck_ref[...]
        acc = carry_ref[...]
        for piece in _split3(dcb):
            acc = acc + jnp.dot(tri, piece, preferred_element_type=F32)
        carry_ref[...] += jnp.sum(dcb, axis=0, keepdims=True)
        df = acc * sn_ref[...]
        df_ref[...] = df.astype(BF16)
        db_ref[...] += jnp.sum(df, axis=0, keepdims=True)

    rev = pl.BlockSpec((tm, LANES), lambda i: (n - 1 - i, 0))
    return pl.pallas_call(
        body, name=name, grid=(n,),
        in_specs=[rev, rev, rev],
        out_specs=[rev, pl.BlockSpec((1, LANES), lambda i: (0, 0))],
        out_shape=[jax.ShapeDtypeStruct((T, LANES), BF16), jax.ShapeDtypeStruct((1, LANES), F32)],
        scratch_shapes=[pltpu.VMEM((1, LANES), F32)],
        compiler_params=_cp(("arbitrary",)),
    )(dcq, dck, sneg)


def _qk_proj(hn, w_pad, cp, sel, const, scale, name):
    T, D = hn.shape
    H = w_pad.shape[1] // LANES
    tm = min(TM_MM, T)

    def body(a_ref, w_ref, cp_ref, sel_ref, c_ref, o_ref):
        acc = jnp.dot(a_ref[...], w_ref[...], preferred_element_type=F32)
        if scale != 1.0:
            acc = acc * scale
        acc = acc + jnp.dot(cp_ref[...], sel_ref[...], preferred_element_type=F32) + c_ref[...]
        o_ref[0] = acc[:, :LANES].astype(BF16)
        o_ref[1] = acc[:, LANES:].astype(BF16)

    return pl.pallas_call(
        body, name=name, grid=(T // tm, H // 2),
        in_specs=[pl.BlockSpec((tm, D), lambda i, p: (i, 0)), pl.BlockSpec((D, 2 * LANES), lambda i, p: (0, p)),
                  pl.BlockSpec((tm, 3 * LANES), lambda i, p: (i, 0)),
                  pl.BlockSpec((None, 3 * LANES, 2 * LANES), lambda i, p: (p, 0, 0)),
                  pl.BlockSpec((None, 1, 2 * LANES), lambda i, p: (p, 0, 0))],
        out_specs=pl.BlockSpec((2, tm, LANES), lambda i, p: (p, i, 0)),
        out_shape=jax.ShapeDtypeStruct((H, T, LANES), BF16),
        compiler_params=_cp(("parallel", "arbitrary")),
    )(hn, w_pad, cp, sel, const)


def _lane_lo():
    return lax.broadcasted_iota(jnp.int32, (1, LANES), 1) < HEAD_DIM


def _attn_fwd(qp, kp, v, name):
    H, T, _ = qp.shape
    tq = min(TQ, T)
    nrep = tq // LANES

    def body(q_ref, k_ref, v_ref, o_ref, o32_ref, lse_ref, m_sc, l_sc, acc_sc):
        i = pl.program_id(1)
        m_sc[...] = jnp.full(m_sc.shape, NEG, F32)
        l_sc[...] = jnp.zeros_like(l_sc)
        acc_sc[...] = jnp.zeros_like(acc_sc)

        def step(j, masked):
            off = pl.multiple_of(j * tq, tq)
            vblk = v_ref[pl.ds(off, tq), :]
            s_all = [_dot_nt(q_ref[h], k_ref[h, pl.ds(off, tq), :]) for h in range(2)]
            for h in range(2):
                s = s_all[h]
                tiles = [s[:, c * LANES:(c + 1) * LANES] for c in range(nrep)]
                if masked:
                    r = lax.broadcasted_iota(jnp.int32, (tq, LANES), 0)
                    cc = lax.broadcasted_iota(jnp.int32, (tq, LANES), 1)
                    tiles = [jnp.where(r >= cc + c * LANES, t, NEG) for c, t in enumerate(tiles)]
                mt = tiles[0]
                for t in tiles[1:]:
                    mt = jnp.maximum(mt, t)
                m_prev = m_sc[h]
                m_new = jnp.maximum(m_prev, jnp.max(mt, axis=-1, keepdims=True))
                alpha = jnp.exp(m_prev - m_new)
                ps = [jnp.exp(t - m_new) for t in tiles]
                psum = ps[0]
                for pt in ps[1:]:
                    psum = psum + pt
                p16 = jnp.concatenate([pt.astype(BF16) for pt in ps], axis=1)
                l_sc[h] = alpha * l_sc[h] + psum
                acc_sc[h] = alpha * acc_sc[h] + jnp.dot(p16, vblk, preferred_element_type=F32)
                m_sc[h] = m_new

        def loop_body(j, carry):
            step(j, False)
            return carry

        lax.fori_loop(0, i, loop_body, 0)
        step(i, True)
        lo = _lane_lo()
        l0 = jnp.sum(l_sc[0], axis=-1, keepdims=True)
        l1 = jnp.sum(l_sc[1], axis=-1, keepdims=True)
        o = jnp.where(lo, acc_sc[0] / l0, acc_sc[1] / l1)
        o_ref[...] = o.astype(BF16)
        o32_ref[...] = o
        lse_ref[...] = jnp.where(lo, m_sc[0] + jnp.log(l0), m_sc[1] + jnp.log(l1))

    oblk = pl.BlockSpec((tq, LANES), lambda p, i: (i, p))
    return pl.pallas_call(
        body, name=name, grid=(H // 2, T // tq),
        in_specs=[pl.BlockSpec((2, tq, LANES), lambda p, i: (p, i, 0)),
                  pl.BlockSpec((2, T, LANES), lambda p, i: (p, 0, 0)),
                  pl.BlockSpec((T, LANES), lambda p, i: (0, p))],
        out_specs=[oblk, oblk, pl.BlockSpec((None, tq, LANES), lambda p, i: (p, i, 0))],
        out_shape=[jax.ShapeDtypeStruct((T, H * HEAD_DIM), BF16), jax.ShapeDtypeStruct((T, H * HEAD_DIM), F32),
                   jax.ShapeDtypeStruct((H // 2, T, LANES), F32)],
        scratch_shapes=[pltpu.VMEM((2, tq, LANES), F32), pltpu.VMEM((2, tq, LANES), F32),
                        pltpu.VMEM((2, tq, LANES), F32)],
        compiler_params=_cp(("parallel", "arbitrary")),
    )(qp, kp, v)


def _attn_bwd(qp, kp, v, o, do, lse, scale, name):
    H, T, _ = qp.shape
    tq = min(TQ, T)
    nq = T // tq
    nrep = tq // LANES

    def body(q_ref, k_ref, v_ref, o_ref, do_ref, lse_ref, dq_ref, dk_ref, dv_ref, dqe_ref, dke_ref, dk_sc, dv_sc, dq_sc):
        i = pl.program_id(1)

        @pl.when(i == 0)
        def _():
            dk_sc[...] = jnp.zeros_like(dk_sc)
            dv_sc[...] = jnp.zeros_like(dv_sc)

        dq_sc[...] = jnp.zeros_like(dq_sc)

        lo = _lane_lo()
        dob = do_ref[...]
        dof = dob.astype(F32)
        prod = dof * o_ref[...].astype(F32)
        lse2 = lse_ref[...]
        lse2_sw = pltpu.roll(lse2, HEAD_DIM, 1)
        zero = jnp.zeros_like(dob)
        do_h = [jnp.where(lo, dob, zero), jnp.where(lo, zero, dob)]
        rep = lambda col: jnp.broadcast_to(col, (tq, LANES))
        delta = [rep(jnp.sum(jnp.where(lo, prod, 0.0), axis=-1, keepdims=True)),
                 rep(jnp.sum(jnp.where(lo, 0.0, prod), axis=-1, keepdims=True))]
        lse_h = [jnp.where(lo, lse2, lse2_sw), jnp.where(lo, lse2_sw, lse2)]
        qs = [q_ref[0], q_ref[1]]

        def step(j, masked):
            off = pl.multiple_of(j * tq, tq)
            vblk = v_ref[pl.ds(off, tq), :]
            dv_add = None
            for h in range(2):
                kblk = k_ref[h, pl.ds(off, tq), :]
                s = _dot_nt(qs[h], kblk)
                dp = _dot_nt(do_h[h], vblk)
                p16, ds16 = [], []
                for c in range(nrep):
                    cols = slice(c * LANES, (c + 1) * LANES)
                    p = jnp.exp(s[:, cols] - lse_h[h])
                    if masked:
                        r = lax.broadcasted_iota(jnp.int32, (tq, LANES), 0)
                        cc = lax.broadcasted_iota(jnp.int32, (tq, LANES), 1)
                        p = jnp.where(r >= cc + c * LANES, p, 0.0)
                    p16.append(p.astype(BF16))
                    ds16.append((p * (dp[:, cols] - delta[h])).astype(BF16))
                p16 = jnp.concatenate(p16, axis=1)
                dsb = jnp.concatenate(ds16, axis=1)
                dq_sc[h] += jnp.dot(dsb, kblk, preferred_element_type=F32)
                dk_sc[h, pl.ds(off, tq), :] += _dot_tn(dsb, qs[h])
                pv = _dot_tn(p16, do_h[h])
                dv_add = pv if dv_add is None else dv_add + pv
            dv_sc[pl.ds(off, tq), :] += dv_add

        def loop_body(j, carry):
            step(j, False)
            return carry

        lax.fori_loop(0, i, loop_body, 0)
        step(i, True)
        dq0, dq1 = dq_sc[0], dq_sc[1]
        dq_ref[...] = (jnp.where(lo, dq0, pltpu.roll(dq1, HEAD_DIM, 1)) * scale).astype(BF16)
        dqe_ref[0:8, :] = jnp.transpose(dq0)[HEAD_DIM:HEAD_DIM + 8, :]
        dqe_ref[8:16, :] = jnp.transpose(dq1)[HEAD_DIM:HEAD_DIM + 8, :]

        @pl.when(i == nq - 1)
        def _():
            dk0, dk1 = dk_sc[0], dk_sc[1]
            dk_ref[...] = jnp.where(lo, dk0, pltpu.roll(dk1, HEAD_DIM, 1)).astype(BF16)
            for h in range(2):
                for cb in range(nq):
                    blk = jnp.transpose(dk_sc[h, cb * tq:(cb + 1) * tq, :])
                    dke_ref[8 * h:8 * h + 8, cb * tq:(cb + 1) * tq] = blk[HEAD_DIM:HEAD_DIM + 8, :]
            dv_ref[...] = dv_sc[...].astype(BF16)

    qblk = pl.BlockSpec((tq, LANES), lambda p, i: (i, p))
    pair = pl.BlockSpec((T, LANES), lambda p, i: (0, p))
    tok16 = jax.ShapeDtypeStruct((T, H * HEAD_DIM), BF16)
    gate32 = jax.ShapeDtypeStruct((H // 2, 16, T), F32)
    return pl.pallas_call(
        body, name=name, grid=(H // 2, nq),
        in_specs=[pl.BlockSpec((2, tq, LANES), lambda p, i: (p, i, 0)),
                  pl.BlockSpec((2, T, LANES), lambda p, i: (p, 0, 0)),
                  pair, qblk, qblk,
                  pl.BlockSpec((None, tq, LANES), lambda p, i: (p, i, 0))],
        out_specs=[qblk, pair, pair, pl.BlockSpec((None, 16, tq), lambda p, i: (p, 0, i)),
                   pl.BlockSpec((None, 16, T), lambda p, i: (p, 0, 0))],
        out_shape=[tok16, tok16, tok16, gate32, gate32],
        scratch_shapes=[pltpu.VMEM((2, T, LANES), F32), pltpu.VMEM((T, LANES), F32),
                        pltpu.VMEM((2, tq, LANES), F32)],
        compiler_params=_cp(("parallel", "arbitrary")),
    )(qp, kp, v, o, do, lse)


def _mesh_pos():
    return lax.axis_index("x"), lax.axis_index("y"), lax.axis_index("c")


def _all_gather(arrs, name, groups=None):
    n = len(arrs)
    if groups is None:
        groups = [(a, 0) for a in range(n)]
    ng = 1 + max(g for g, _ in groups)
    per_group = [sum(1 for g, _ in groups if g == gi) for gi in range(ng)]
    first_of = [next(a for a in range(n) if groups[a][0] == gi) for gi in range(ng)]

    def body(*refs):
        ins, outs = refs[:n], refs[n:n + ng]
        send_sems, recv_sems, local_sems = refs[n + ng:]
        x, y, c = _mesh_pos()
        me, sib = (x, y, c), (x, y, 1 - c)
        chips = [(1 - x, y), (x, 1 - y), (1 - x, 1 - y)]

        def dst_of(a, px, py, pc):
            g, k = groups[a]
            return outs[g].at[N_DEV * k + 4 * px + 2 * py + pc]

        def copy(a, k, block, to, src=None):
            dst = dst_of(a, *block)
            return pltpu.make_async_remote_copy(
                src_ref=dst if src is None else src, dst_ref=dst,
                send_sem=send_sems.at[a, k], recv_sem=recv_sems.at[a, k], device_id=to, device_id_type=MESH)

        mine = [pltpu.make_async_copy(ins[a], dst_of(a, *me), local_sems.at[a]) for a in range(n)]
        for cp in mine:
            cp.start()
        first = []
        for a in range(n):
            first.append(copy(a, 0, me, sib, src=ins[a]))
            first += [copy(a, 1 + j, me, (*chip, c), src=ins[a]) for j, chip in enumerate(chips)]
        for cp in first:
            cp.start()
        passed = []
        for j, chip in enumerate(chips):
            for a in range(n):
                copy(a, 1 + j, (*chip, c), me).wait_recv()
                fwd = copy(a, 4 + j, (*chip, c), sib)
                fwd.start()
                passed.append(fwd)
        for a in range(n):
            copy(a, 0, sib, me).wait_recv()
            for j, chip in enumerate(chips):
                copy(a, 4 + j, (*chip, 1 - c), me).wait_recv()
        for cp in first + passed:
            cp.wait_send()
        for cp in mine:
            cp.wait()

    any_spec = pl.BlockSpec(memory_space=pl.ANY)
    return pl.pallas_call(
        body, name=name,
        in_specs=[any_spec] * n, out_specs=[any_spec] * ng,
        out_shape=[jax.ShapeDtypeStruct((N_DEV * per_group[gi],) + arrs[first_of[gi]].shape, arrs[first_of[gi]].dtype)
                   for gi in range(ng)],
        scratch_shapes=[pltpu.SemaphoreType.DMA((n, 7)), pltpu.SemaphoreType.DMA((n, 7)),
                        pltpu.SemaphoreType.DMA((n,))],
    )(*arrs)


def _pair_exchange(gs, name):
    n = len(gs)

    def body(*refs):
        g_refs, o_refs = refs[:n], refs[n:2 * n]
        send_sems, recv_sems = refs[2 * n:]
        x, y, c = _mesh_pos()
        sib = (x, y, 1 - c)
        copies = []
        for a in range(n):
            for j in range(4):
                copies.append(pltpu.make_async_remote_copy(
                    src_ref=g_refs[a].at[2 * j + (1 - c)], dst_ref=o_refs[a].at[j],
                    send_sem=send_sems.at[a, j], recv_sem=recv_sems.at[a, j], device_id=sib, device_id_type=MESH))
        for cp in copies:
            cp.start()
        for cp in copies:
            cp.wait_recv()
        for cp in copies:
            cp.wait_send()

    any_spec = pl.BlockSpec(memory_space=pl.ANY)
    return pl.pallas_call(
        body, name=name, in_specs=[any_spec] * n, out_specs=[any_spec] * n,
        out_shape=[jax.ShapeDtypeStruct((4,) + g.shape[1:], g.dtype) for g in gs],
        scratch_shapes=[pltpu.SemaphoreType.DMA((n, 4)), pltpu.SemaphoreType.DMA((n, 4))],
    )(*gs)


def _chip_exchange(parts, name):
    n = len(parts)

    def body(*refs):
        p_refs, o_refs = refs[:n], refs[n:2 * n]
        send_sems, recv_sems = refs[2 * n:]
        x, y, c = _mesh_pos()
        chips = [(1 - x, y), (x, 1 - y), (1 - x, 1 - y)]
        copies = []
        for a in range(n):
            for k, (px, py) in enumerate(chips):
                copies.append(pltpu.make_async_remote_copy(
                    src_ref=p_refs[a].at[2 * px + py], dst_ref=o_refs[a].at[k],
                    send_sem=send_sems.at[a, k], recv_sem=recv_sems.at[a, k], device_id=(px, py, c),
                    device_id_type=MESH))
        for cp in copies:
            cp.start()
        for cp in copies:
            cp.wait_recv()
        for cp in copies:
            cp.wait_send()

    any_spec = pl.BlockSpec(memory_space=pl.ANY)
    return pl.pallas_call(
        body, name=name, in_specs=[any_spec] * n, out_specs=[any_spec] * n,
        out_shape=[jax.ShapeDtypeStruct((3,) + p.shape[1:], p.dtype) for p in parts],
        scratch_shapes=[pltpu.SemaphoreType.DMA((n, 3)), pltpu.SemaphoreType.DMA((n, 3))],
    )(*parts)


HBM_SPEC = pl.BlockSpec(memory_space=pltpu.HBM)
SEM_SPEC = pl.BlockSpec(memory_space=pltpu.SEMAPHORE)
ANY_SPEC = pl.BlockSpec(memory_space=pl.ANY)
DATAFLOW_EFFECT = pltpu.SideEffectType.DATAFLOW_SIDE_EFFECTING


def _peers():
    x, y, c = _mesh_pos()
    flip = lambda v, b: 1 - v if b else v
    return [(flip(x, (k >> 2) & 1), flip(y, (k >> 1) & 1), flip(c, k & 1)) for k in range(1, N_DEV)]


def _slot(p):
    return 4 * p[0] + 2 * p[1] + p[2]


def _direct_copy(src_refs, land_refs, sems, a, k, p, land_of, dst_slot, src_slot):
    s = src_slot(a, p)
    return pltpu.make_async_remote_copy(
        src_ref=src_refs[a] if s is None else src_refs[a].at[s], dst_ref=land_refs[land_of[a]].at[dst_slot(a, k)],
        send_sem=sems[0].at[a * (N_DEV - 1) + k], recv_sem=sems[1].at[a * (N_DEV - 1) + k], device_id=p,
        device_id_type=MESH)


def _direct_start(srcs, lands, land_of, dst_slot, src_slot, after, name, collective_id):
    n, nl = len(srcs), len(lands)

    def body(*refs):
        src_refs, land_refs = refs[:n], refs[n:n + nl]
        sems = (refs[n + nl + 1], refs[n + nl + 2])
        token = refs[-1]
        peers = _peers()
        barrier = pltpu.get_barrier_semaphore()
        for p in peers:
            pl.semaphore_signal(barrier, inc=1, device_id=p, device_id_type=MESH)
        pl.semaphore_wait(barrier, N_DEV - 1)
        for a in range(n):
            for k, p in enumerate(peers):
                _direct_copy(src_refs, land_refs, sems, a, k, p, land_of, dst_slot, src_slot).start()
        token[...] = jnp.zeros_like(token)

    hbm = lambda t: pltpu.HBM(t.shape, t.dtype)
    sem_t = pltpu.SemaphoreType.DMA((n * (N_DEV - 1),))
    outs = pl.pallas_call(
        body, name=name,
        out_shape=(sem_t, sem_t, *[hbm(t) for t in srcs], *[hbm(t) for t in lands], jax.ShapeDtypeStruct((8, LANES), F32)),
        in_specs=[HBM_SPEC] * (n + nl) + [ANY_SPEC],
        out_specs=(SEM_SPEC, SEM_SPEC, *([HBM_SPEC] * (n + nl)), pl.BlockSpec(memory_space=pltpu.VMEM)),
        input_output_aliases={i: 2 + i for i in range(n + nl)},
        compiler_params=pltpu.CompilerParams(has_side_effects=DATAFLOW_EFFECT, collective_id=collective_id),
    )(*[pltpu.with_memory_space_constraint(t, pltpu.HBM) for t in srcs],
      *[pltpu.with_memory_space_constraint(t, pltpu.HBM) for t in lands], after)
    return outs[0], outs[1], list(outs[2:2 + n]), list(outs[2 + n:2 + n + nl]), outs[-1]


def _direct_wait(send_sems, recv_sems, srcs, lands, land_of, idxs, dst_slot, src_slot, after, name):
    land_ids = []
    for a in idxs:
        if land_of[a] not in land_ids:
            land_ids.append(land_of[a])
    m, ml = len(idxs), len(land_ids)
    sub_land_of = {j: land_ids.index(land_of[a]) for j, a in enumerate(idxs)}

    def body(*refs):
        src_refs, land_refs = refs[:m], refs[m:m + ml]
        ssem, rsem = refs[m + ml], refs[m + ml + 1]
        for j, a in enumerate(idxs):
            for k, p in enumerate(_peers()):
                s = src_slot(a, p)
                cp = pltpu.make_async_remote_copy(
                    src_ref=src_refs[j] if s is None else src_refs[j].at[s],
                    dst_ref=land_refs[sub_land_of[j]].at[dst_slot(a, k)],
                    send_sem=ssem.at[a * (N_DEV - 1) + k], recv_sem=rsem.at[a * (N_DEV - 1) + k], device_id=p,
                    device_id_type=MESH)
                cp.wait_send()
                cp.wait_recv()

    hbm = lambda t: pltpu.HBM(t.shape, t.dtype)
    sub_s, sub_l = [srcs[a] for a in idxs], [lands[g] for g in land_ids]
    outs = pl.pallas_call(
        body, name=name,
        out_shape=(*[hbm(t) for t in sub_s], *[hbm(t) for t in sub_l]),
        in_specs=[HBM_SPEC] * (m + ml) + [SEM_SPEC, SEM_SPEC, ANY_SPEC],
        out_specs=tuple([HBM_SPEC] * (m + ml)),
        input_output_aliases={i: i for i in range(m + ml)},
        compiler_params=pltpu.CompilerParams(has_side_effects=DATAFLOW_EFFECT),
    )(*sub_s, *sub_l, send_sems, recv_sems, after)
    return list(outs[:m]), list(outs[m:])


def _row_block(R, C):
    best = None
    for d in range(16, R + 1, 16):
        if R % d == 0 and d * C <= 256 * 1024:
            best = d
    return best if best is not None else R


def _pair_add(g, recv, cidx, name):
    _, R, C = g.shape
    tr = _row_block(R, C)

    def body(c_ref, g_ref, r_ref, o_ref):
        del c_ref
        o_ref[...] = (g_ref[...].astype(F32) + r_ref[...].astype(F32)).astype(BF16)

    grid_spec = pltpu.PrefetchScalarGridSpec(
        num_scalar_prefetch=1, grid=(4, R // tr),
        in_specs=[pl.BlockSpec((None, tr, C), lambda j, i, c: (2 * j + c[0], i, 0)),
                  pl.BlockSpec((None, tr, C), lambda j, i, c: (j, i, 0))],
        out_specs=pl.BlockSpec((None, tr, C), lambda j, i, c: (j, i, 0)))
    return pl.pallas_call(
        body, name=name, grid_spec=grid_spec,
        out_shape=jax.ShapeDtypeStruct((4, R, C), BF16),
        compiler_params=_cp(("parallel", "parallel")),
    )(cidx, g, recv)


def _adamw_math(w, g, m, v):
    m = ADAM_B1 * m + (1.0 - ADAM_B1) * g
    v = ADAM_B2 * v + (1.0 - ADAM_B2) * (g * g)
    m_hat = m / (1.0 - ADAM_B1 ** ADAM_STEP)
    v_hat = v / (1.0 - ADAM_B2 ** ADAM_STEP)
    delta = -ADAM_LR * (m_hat / (jnp.sqrt(v_hat) + ADAM_EPS) + ADAM_WD * w)
    return delta, m, v


def _sum_adamw(parts, w, m, v, name, sel=None):
    R, C = w.shape
    tr = _row_block(R, C)
    specs, args = [], []
    for arr, idxs in parts:
        for idx in idxs:
            if idx < 0:
                specs.append(pl.BlockSpec((None, tr, C), lambda i, s: (s[0], i, 0)))
            else:
                specs.append(pl.BlockSpec((None, tr, C), lambda i, s, idx=idx: (idx, i, 0)))
            args.append(arr)
    npart = len(args)
    blk = pl.BlockSpec((tr, C), lambda i, s: (i, 0))

    def body(s_ref, *refs):
        del s_ref
        g = refs[0][...].astype(F32)
        for r in refs[1:npart]:
            g = g + r[...].astype(F32)
        w_ref, m_ref, v_ref, g_out, d_out, m_out, v_out = refs[npart:]
        delta, mm, vv = _adamw_math(w_ref[...], g, m_ref[...], v_ref[...])
        g_out[...] = g
        d_out[...] = delta
        m_out[...] = mm
        v_out[...] = vv

    grid_spec = pltpu.PrefetchScalarGridSpec(
        num_scalar_prefetch=1, grid=(R // tr,),
        in_specs=specs + [blk, blk, blk], out_specs=[blk] * 4)
    if sel is None:
        sel = jnp.zeros((1,), jnp.int32)
    return pl.pallas_call(
        body, name=name, grid_spec=grid_spec,
        out_shape=[jax.ShapeDtypeStruct((R, C), F32)] * 4,
        compiler_params=_cp(("parallel",)),
    )(sel, *args, w, m, v)


def _rows(a, c):
    return a.reshape(-1, c)


def _pad_rows(a, r):
    return jnp.pad(a, ((0, r - a.shape[0]), (0, 0))) if a.shape[0] != r else a


def _gate_tables():
    hp = N_HEADS // 2
    sel_q = np.zeros((hp, 3 * LANES, 2 * LANES), np.float32)
    sel_k = np.zeros((hp, 3 * LANES, 2 * LANES), np.float32)
    const_q = np.zeros((hp, 1, 2 * LANES), np.float32)
    const_k = np.zeros((hp, 1, 2 * LANES), np.float32)
    for p in range(hp):
        for hh in range(2):
            h = 2 * p + hh
            base = hh * LANES + HEAD_DIM
            for piece in range(3):
                sel_q[p, piece * LANES + h, base + piece] = 1.0
                sel_k[p, piece * LANES + h, base + 3 + piece] = -1.0
            const_k[p, 0, base:base + 3] = 1.0
            const_q[p, 0, base + 3:base + 6] = 1.0
    as_bf = lambda t: jnp.asarray(t, BF16)
    return as_bf(sel_q), as_bf(sel_k), jnp.asarray(const_q), jnp.asarray(const_k)


def _pad_heads(w):
    d = w.shape[0]
    w3 = w.reshape(d, N_HEADS, HEAD_DIM)
    return jnp.pad(w3, ((0, 0), (0, 0), (0, LANES - HEAD_DIM))).reshape(d, N_HEADS * LANES)


def kernel(x, mix_norm_g, ffn_norm_g, gm_w_in, gm_ln_g, gm_ln_b, gm_w_s, gm_b_s, gm_w_out, fox_w_qkvf, fox_b_f, fox_w_o, ffn_w_gate, ffn_w_up, ffn_conv_w, ffn_conv_b, ffn_w_down, final_norm_g, loss_target, m_mix_norm_g, m_ffn_norm_g, m_gm_w_in, m_gm_ln_g, m_gm_ln_b, m_gm_w_s, m_gm_b_s, m_gm_w_out, m_fox_w_qkvf, m_fox_b_f, m_fox_w_o, m_ffn_w_gate, m_ffn_w_up, m_ffn_conv_w, m_ffn_conv_b, m_ffn_w_down, m_final_norm_g, v_mix_norm_g, v_ffn_norm_g, v_gm_w_in, v_gm_ln_g, v_gm_ln_b, v_gm_w_s, v_gm_b_s, v_gm_w_out, v_fox_w_qkvf, v_fox_b_f, v_fox_w_o, v_ffn_w_gate, v_ffn_w_up, v_ffn_conv_w, v_ffn_conv_b, v_ffn_w_down, v_final_norm_g):
    T, D = x.shape[1], x.shape[2]
    E = gm_ln_g.shape[1]
    FF = ffn_conv_b.shape[1]
    NQKVF = 3 * D + N_HEADS
    xi, yi, ci = _mesh_pos()
    me = 4 * xi + 2 * yi + ci
    h0 = x.reshape(T, D)
    tgt = loss_target.reshape(T, D)

    nl = ffn_w_gate.shape[0]
    to16 = lambda a: a.astype(BF16)
    n_cw_rows = ffn_conv_w.size // LANES
    cw_rows = _pad_rows(_rows(ffn_conv_w.astype(F32), LANES), 16)
    w_in_g, w_out_g8, cwg = _all_gather([to16(gm_w_in[0]), to16(gm_w_out[0]), cw_rows], "ag_weights")
    w_out_g = w_out_g8.reshape(E, D)
    later, land_of, land_off, lands = [], [], [], []
    for l in range(nl):
        later += [to16(ffn_w_gate[l]), to16(ffn_w_up[l]), to16(ffn_w_down[l])]
        land_of += [2 * l, 2 * l, 2 * l + 1]
        land_off += [0, N_DEV, 0]
        lands += [lax.empty((2 * N_DEV, D, FF // N_DEV), BF16), lax.empty((N_DEV, FF // N_DEV, D), BF16)]
    later += [to16(fox_w_qkvf[0]), to16(fox_w_o[0])]
    land_of += [2 * nl, 2 * nl + 1]
    land_off += [0, 0]
    lands += [lax.empty((N_DEV, D, NQKVF // N_DEV), BF16), lax.empty((N_DEV, D // N_DEV, D), BF16)]
    ag_dst = lambda a, k: land_off[a] + _slot(_mesh_pos())
    ag_src = lambda a, p: None
    ag_send, ag_recv, later, lands, ag_token = _direct_start(later, lands, land_of, ag_dst, ag_src, w_in_g,
                                                             "ag_later_start", collective_id=1)

    def own_blocks(landed, shards, offs):
        for s, o in zip(shards, offs):
            landed = lax.dynamic_update_index_in_dim(landed, s, o + me, 0)
        return landed

    def gather_wait(idxs, after, name):
        return _direct_wait(ag_send, ag_recv, later, lands, land_of, idxs, ag_dst, ag_src, after, name)

    conv_w_full = jnp.transpose(cwg[:, :n_cw_rows].reshape(N_DEV, nl, 3, FF // N_DEV), (1, 2, 0, 3)).reshape(nl, 3, FF)

    ffn_w = {}

    def ffn_weights(l):
        return ffn_w[l]

    def land_ffn(l, shards, gu_land, dn_land):
        ffn_w[l] = (own_blocks(gu_land, shards[:2], [0, N_DEV]), own_blocks(dn_land, shards[2:3], [0]).reshape(FF, D))

    saved = {}

    def ffn_fwd(l, h_in):
        wgul, wdl = ffn_weights(l)
        hn = _rms_fwd(h_in, ffn_norm_g[l:l + 1], f"ffn{l}_norm")
        au = _mm_nn(hn, wgul, f"ffn{l}_gate_up")
        act = _ffn_mid_fwd(au, conv_w_full[l], ffn_conv_b[l:l + 1], f"ffn{l}_mid")
        h_out = _mm_nn(act, wdl, f"ffn{l}_down", res=h_in)
        saved[f"ffn{l}"] = (h_in, hn, au, act)
        return h_out

    bs_col = gm_b_s[0].reshape(GM_GROUPS, CHUNK, 1)
    hn0 = _rms_fwd(h0, mix_norm_g[0:1], "mix0_norm", after=ag_token)
    z = _mm_nn(hn0, w_in_g, "gm_in")
    gu = _sgu_fwd(z, gm_ln_g, gm_ln_b, gm_w_s[0], bs_col, "gm_sgu")
    h1 = _mm_nn(gu, w_out_g, "gm_out", res=h0)
    mine0, land0 = gather_wait([0, 1, 2], h1, "ag_ffn0_wait")
    land_ffn(0, mine0, *land0)
    h2 = ffn_fwd(0, h1)

    mine1, rest = gather_wait(list(range(3, 3 * nl + 2)), h2, "ag_layer1_wait")
    for l in range(1, nl):
        land_ffn(l, mine1[3 * (l - 1):3 * l], rest[2 * (l - 1)], rest[2 * (l - 1) + 1])
    w_qkvf = jnp.transpose(own_blocks(rest[-2], mine1[-2:-1], [0]), (1, 0, 2)).reshape(D, NQKVF)
    w_o_g = own_blocks(rest[-1], mine1[-1:], [0]).reshape(D, D)
    w_q, w_k, w_v = w_qkvf[:, :D], w_qkvf[:, D:2 * D], w_qkvf[:, 2 * D:3 * D]
    w_f = jnp.pad(w_qkvf[:, 3 * D:], ((0, 0), (0, LANES - N_HEADS)))
    bf_row = jnp.pad(fox_b_f, ((0, 0), (0, LANES - N_HEADS)))
    sel_q, sel_k, const_q, const_k = _gate_tables()
    scale = HEAD_DIM ** -0.5
    hn2 = _rms_fwd(h2, mix_norm_g[1:2], "mix1_norm")
    f_logit = _mm_nn(hn2, w_f, "fox_f")
    cp, sneg = _gate_scan(f_logit, bf_row, "fox_scan")
    qp = _qk_proj(hn2, _pad_heads(w_q), cp, sel_q, const_q, scale, "fox_q")
    kp = _qk_proj(hn2, _pad_heads(w_k), cp, sel_k, const_k, 1.0, "fox_k")
    vv = _mm_nn(hn2, w_v, "fox_v", out_dtype=BF16)
    o, o32, lse = _attn_fwd(qp, kp, vv, "fox_attn")
    h3 = _mm_nn(o, w_o_g, "fox_o", res=h2)
    h4 = ffn_fwd(1, h3)

    dh, dh16, d_final, loss_row = _loss_head(h4, tgt, final_norm_g.reshape(1, D), "loss_head")
    loss = lax.psum(loss_row[0, 0], ("x", "y", "c"))

    rs_dst = lambda a, k: k
    rs_src = lambda a, p: _slot(p)
    me_idx = me.astype(jnp.int32).reshape(1)

    def rs_start(grads, name, cid):
        lands = [lax.empty((N_DEV - 1,) + g.shape[1:], BF16) for g in grads]
        return _direct_start(grads, lands, list(range(len(grads))), rs_dst, rs_src, loss_row, name, collective_id=cid)

    def rs_wait(st, after, name):
        n = len(st[2])
        return _direct_wait(st[0], st[1], st[2], st[3], list(range(n)), list(range(n)), rs_dst, rs_src, after, name)

    def ffn_bwd(l, dh, dh16, after=None):
        wgul, wdl = ffn_weights(l)
        h_in, hn, au, act = saved[f"ffn{l}"]
        dact = _mm_nt([dh16], wdl, f"ffn{l}_dact", out_dtype=BF16, after=after)
        d_wd = _mm_tn(act, dh16, f"ffn{l}_dwd", out_dtype=BF16)
        da, dup, d_cw, d_cb = _ffn_mid_bwd(au, dact, conv_w_full[l], ffn_conv_b[l:l + 1], f"ffn{l}_dmid")
        dhn = _mm_nt([da, dup], wgul, f"ffn{l}_dhn")
        d_wg = _mm_tn(hn, da, f"ffn{l}_dwg", blocked_w=FF // N_DEV, out_dtype=BF16)
        d_wu = _mm_tn(hn, dup, f"ffn{l}_dwu", blocked_w=FF // N_DEV, out_dtype=BF16)
        dh_in, dh_in16, d_norm = _rms_bwd(dhn, h_in, ffn_norm_g[l:l + 1], dh, f"ffn{l}_dnorm")
        big_g = [d_wg, d_wu, d_wd.reshape(N_DEV, FF // N_DEV, D)]
        return dh_in, dh_in16, big_g, dict(cw=d_cw, cb=d_cb, norm=d_norm)

    dh, dh16, big_ffn1, g_ffn1 = ffn_bwd(1, dh, dh16)

    do = _mm_nt([dh16], w_o_g, "fox_do", out_dtype=BF16)
    d_wo = _mm_tn(o, dh16, "fox_dwo", out_dtype=BF16)
    dq, dk, dv, dqe, dke = _attn_bwd(qp, kp, vv, o32, do, lse, scale, "fox_dattn")
    gate_lane = lambda e, r: jnp.pad(jnp.transpose(e[:, r::8, :].reshape(N_HEADS, T)), ((0, 0), (0, LANES - N_HEADS)))
    df, d_bf = _gate_scan_bwd(gate_lane(dqe, 0), gate_lane(dke, 3), sneg, "fox_dscan")
    dhn = _mm_nt([df], w_f, "fox_dhn_f")
    dhn = _mm_nt([dq, dk, dv], w_qkvf[:, :3 * D], "fox_dhn_qkv", add=dhn)
    d_wq = _mm_tn(hn2, dq, "fox_dwq", out_dtype=BF16)
    d_wk = _mm_tn(hn2, dk, "fox_dwk", out_dtype=BF16)
    d_wv = _mm_tn(hn2, dv, "fox_dwv", out_dtype=BF16)
    d_wf = _mm_tn(hn2, df, "fox_dwf", out_dtype=BF16)
    d_wqkvf = jnp.concatenate([d_wq, d_wk, d_wv, d_wf[:, :N_HEADS]], axis=1)
    dh, dh16, d_mix1 = _rms_bwd(dhn, h2, mix_norm_g[1:2], dh, "mix1_dnorm")
    st1 = rs_start([jnp.transpose(d_wqkvf.reshape(D, N_DEV, NQKVF // N_DEV), (1, 0, 2)),
                    d_wo.reshape(N_DEV, D // N_DEV, D)] + big_ffn1, "rs1_start", 2)

    dh, dh16, big_ffn0, g_ffn0 = ffn_bwd(0, dh, dh16, after=st1[4])
    st2 = rs_start(big_ffn0, "rs2_start", 3)

    dgu = _mm_nt([dh16], w_out_g, "gm_dgu", out_dtype=BF16, after=st2[4])
    d_wout = _mm_tn(gu, dh16, "gm_dwout", out_dtype=BF16)
    dz, d_lng, d_lnb, d_ws, d_bs = _sgu_bwd(z, dgu, gm_ln_g, gm_ln_b, gm_w_s[0], bs_col, "gm_dsgu")
    d_win = _mm_tn(hn0, dz, "gm_dwin", blocked_w=2 * E // N_DEV, out_dtype=BF16)
    st3 = rs_start([d_win, d_wout.reshape(N_DEV, E // N_DEV, D)], "rs3_start", 4)
    dhn = _mm_nt([dz], w_in_g, "gm_dhn", after=st3[4])
    dx, _, d_mix0 = _rms_bwd(dhn, h0, mix_norm_g[0:1], dh, "mix0_dnorm")

    own1, land1 = rs_wait(st1, dx, "rs1_wait")
    own2, land2 = rs_wait(st2, land1[0], "rs2_wait")
    cat1 = lambda a, b: jnp.concatenate([a, b], axis=1)
    big_out = {}

    def big_adamw(name, w, m, v, own, landed):
        shard2d = lambda a, c=own.shape[2]: a.reshape(-1, c)
        res = _sum_adamw([(own, [-1]), (landed, list(range(N_DEV - 1)))], shard2d(w), shard2d(m), shard2d(v),
                         f"adamw_{name}", sel=me_idx)
        big_out[name] = [t.reshape(w.shape) for t in res]

    big_adamw("fox_w_qkvf", fox_w_qkvf, m_fox_w_qkvf, v_fox_w_qkvf, own1[0], land1[0])
    big_adamw("fox_w_o", fox_w_o, m_fox_w_o, v_fox_w_o, own1[1], land1[1])
    big_adamw("ffn_w_gate", ffn_w_gate, m_ffn_w_gate, v_ffn_w_gate, cat1(own2[0], own1[2]), cat1(land2[0], land1[2]))
    big_adamw("ffn_w_up", ffn_w_up, m_ffn_w_up, v_ffn_w_up, cat1(own2[1], own1[3]), cat1(land2[1], land1[3]))
    big_adamw("ffn_w_down", ffn_w_down, m_ffn_w_down, v_ffn_w_down, cat1(own2[2], own1[4]), cat1(land2[2], land1[4]))

    small = [("mix_norm_g", mix_norm_g, m_mix_norm_g, v_mix_norm_g, jnp.concatenate([d_mix0, d_mix1], axis=0)),
             ("ffn_norm_g", ffn_norm_g, m_ffn_norm_g, v_ffn_norm_g, jnp.concatenate([g_ffn0["norm"], g_ffn1["norm"]], axis=0)),
             ("gm_ln_g", gm_ln_g, m_gm_ln_g, v_gm_ln_g, d_lng),
             ("gm_ln_b", gm_ln_b, m_gm_ln_b, v_gm_ln_b, d_lnb),
             ("gm_w_s", gm_w_s, m_gm_w_s, v_gm_w_s, d_ws),
             ("gm_b_s", gm_b_s, m_gm_b_s, v_gm_b_s, d_bs),
             ("fox_b_f", fox_b_f, m_fox_b_f, v_fox_b_f, d_bf[:, :N_HEADS]),
             ("ffn_conv_b", ffn_conv_b, m_ffn_conv_b, v_ffn_conv_b, jnp.concatenate([g_ffn0["cb"], g_ffn1["cb"]], axis=0)),
             ("final_norm_g", final_norm_g, m_final_norm_g, v_final_norm_g, d_final)]
    d_cw_full = jnp.stack([g_ffn0["cw"], g_ffn1["cw"]], axis=0)

    def small_rows(a):
        flat = a.astype(F32).reshape(-1)
        n = -(-flat.size // (8 * LANES)) * (8 * LANES)
        return jnp.pad(flat, (0, n - flat.size)).reshape(-1, LANES)

    s_rows = [small_rows(p[1]).shape[0] for p in small]
    s_off = np.concatenate([[0], np.cumsum(s_rows)]).tolist()
    cw_g_rows = small_rows(d_cw_full)
    zeros_cw = jnp.zeros_like(cw_g_rows)
    cat = lambda k: jnp.concatenate([small_rows(p[k]) for p in small] + [zeros_cw], axis=0)
    g_small = jnp.concatenate([small_rows(p[4]) for p in small] + [cw_g_rows], axis=0)
    (gs_all,) = _all_gather([g_small], "ag_small_grads")
    small_out = _sum_adamw([(gs_all, list(range(N_DEV)))], cat(1), cat(2), cat(3), "adamw_small")
    gs = small_out[0]

    g_cw_full = gs[s_off[-1]:].reshape(-1)[:d_cw_full.size].reshape(d_cw_full.shape)
    g_cw = lax.dynamic_slice_in_dim(g_cw_full, me * (FF // N_DEV), FF // N_DEV, axis=2)
    cw2 = lambda a: _pad_rows(_rows(a.astype(F32), LANES), 16)
    cw_out = _sum_adamw([(cw2(g_cw)[None], [0])], cw2(ffn_conv_w), cw2(m_ffn_conv_w), cw2(v_ffn_conv_w), "adamw_conv_w")

    own3, land3 = rs_wait(st3, cw_out[0], "rs3_wait")
    big_adamw("gm_w_in", gm_w_in, m_gm_w_in, v_gm_w_in, own3[0], land3[0])
    big_adamw("gm_w_out", gm_w_out, m_gm_w_out, v_gm_w_out, own3[1], land3[1])

    names = ["mix_norm_g", "ffn_norm_g", "gm_w_in", "gm_ln_g", "gm_ln_b", "gm_w_s", "gm_b_s", "gm_w_out", "fox_w_qkvf",
             "fox_b_f", "fox_w_o", "ffn_w_gate", "ffn_w_up", "ffn_conv_w", "ffn_conv_b", "ffn_w_down", "final_norm_g"]
    small_idx = {p[0]: k for k, p in enumerate(small)}

    def pick(kind, name):
        if name in big_out:
            return big_out[name][kind]
        if name == "ffn_conv_w":
            return cw_out[kind][:n_cw_rows].reshape(ffn_conv_w.shape)
        k = small_idx[name]
        shp = small[k][1].shape
        return small_out[kind][s_off[k]:s_off[k + 1]].reshape(-1)[:int(np.prod(shp))].reshape(shp)

    outs = [loss, dx.reshape(x.shape)]
    for kind in range(4):
        outs += [pick(kind, n) for n in names]
    return tuple(outs)
```

```python
import functools
import math

import numpy as np
import jax
import jax.numpy as jnp
from jax import lax
from jax.experimental import pallas as pl
from jax.experimental.pallas import tpu as pltpu

F32 = jnp.float32
BF16 = jnp.bfloat16
MESH = pl.DeviceIdType.MESH

N_HEADS = 16
HEAD_DIM = 64
CHUNK = 128
GM_GROUPS = 8
RMS_EPS = 1e-6
LN_EPS = 1e-5
ADAM_LR = 0.001
ADAM_B1 = 0.9
ADAM_B2 = 0.999
ADAM_EPS = 1e-08
ADAM_WD = 0.01
ADAM_STEP = 10
N_DEV = 8

LANES = 128
VMEM_BYTES_V7X = 64 * 1024 * 1024
VMEM_LIMIT = 56 * 1024 * 1024

TM = 512
TM_MM = 1024
TT = 1024
TQ = 512
TF = 512
MM_BLOCK_BYTES = 8 * 1024 * 1024
NEG = -1e30
LOG2E = math.log2(math.e)
LN2 = math.log(2.0)


def _cp(sem=None, vmem=VMEM_LIMIT):
    return pltpu.CompilerParams(dimension_semantics=sem, vmem_limit_bytes=vmem)


def _gelu(x):
    c = math.sqrt(2.0 / math.pi)
    return x * (0.5 * (1.0 + jnp.tanh(c * (x + 0.044715 * (x * x * x)))))


def _gelu_grad(x):
    c = math.sqrt(2.0 / math.pi)
    t = jnp.tanh(c * (x + 0.044715 * (x * x * x)))
    return 0.5 * (1.0 + t) + x * (0.5 * (1.0 - t * t)) * (c * (1.0 + 3.0 * 0.044715 * (x * x)))


def _sigmoid(x):
    return 1.0 / (1.0 + jnp.exp(-x))


def _dot_nt(a, b):
    return lax.dot_general(a, b, (((1,), (1,)), ((), ())), preferred_element_type=F32)


def _dot_tn(a, b):
    return lax.dot_general(a, b, (((0,), (0,)), ((), ())), preferred_element_type=F32)


def _rms_fwd(h, g, name, after=None):
    T, D = h.shape
    tm = min(TM, T)

    def body(h_ref, g_ref, *rest):
        o_ref = rest[-1]
        x = h_ref[...]
        r = lax.rsqrt(jnp.mean(x * x, axis=-1, keepdims=True) + RMS_EPS)
        o_ref[...] = ((x * r) * g_ref[...]).astype(BF16)

    in_specs = [pl.BlockSpec((tm, D), lambda i: (i, 0)), pl.BlockSpec((1, D), lambda i: (0, 0))]
    args = [h, g]
    if after is not None:
        in_specs.append(pl.BlockSpec(memory_space=pl.ANY))
        args.append(after)
    return pl.pallas_call(
        body, name=name, grid=(T // tm,),
        in_specs=in_specs,
        out_specs=pl.BlockSpec((tm, D), lambda i: (i, 0)),
        out_shape=jax.ShapeDtypeStruct((T, D), BF16),
        compiler_params=_cp(("parallel",)),
    )(*args)


def _rms_bwd(dhn, h, g, dres, name):
    T, D = h.shape
    tm = min(TM, T)

    def body(d_ref, h_ref, g_ref, r_ref, o_ref, ob_ref, dg_ref):
        x = h_ref[...]
        d = d_ref[...]
        r = lax.rsqrt(jnp.mean(x * x, axis=-1, keepdims=True) + RMS_EPS)
        dyg = d * g_ref[...]
        dot = jnp.mean(dyg * x, axis=-1, keepdims=True)
        dh = r_ref[...] + (r * dyg - x * ((r * r * r) * dot))
        o_ref[...] = dh
        ob_ref[...] = dh.astype(BF16)
        part = jnp.sum(d * (x * r), axis=0, keepdims=True)

        @pl.when(pl.program_id(0) == 0)
        def _():
            dg_ref[...] = part

        @pl.when(pl.program_id(0) != 0)
        def _():
            dg_ref[...] += part

    blk = pl.BlockSpec((tm, D), lambda i: (i, 0))
    row = pl.BlockSpec((1, D), lambda i: (0, 0))
    return pl.pallas_call(
        body, name=name, grid=(T // tm,),
        in_specs=[blk, blk, row, blk],
        out_specs=[blk, blk, row],
        out_shape=[jax.ShapeDtypeStruct((T, D), F32), jax.ShapeDtypeStruct((T, D), BF16),
                   jax.ShapeDtypeStruct((1, D), F32)],
        compiler_params=_cp(("arbitrary",)),
    )(dhn, h, g, dres)


def _loss_head(h, tgt, g, name):
    T, D = h.shape
    tm = min(TM, T)

    def body(h_ref, t_ref, g_ref, o_ref, ob_ref, dg_ref, l_ref):
        x = h_ref[...]
        gg = g_ref[...]
        r = lax.rsqrt(jnp.mean(x * x, axis=-1, keepdims=True) + RMS_EPS)
        xr = x * r
        e = xr * gg - t_ref[...]
        lpart = 0.5 * jnp.sum(jnp.mean(e * e, axis=-1, keepdims=True), axis=0, keepdims=True)
        dy = e * (1.0 / D)
        dyg = dy * gg
        dot = jnp.mean(dyg * x, axis=-1, keepdims=True)
        dh = r * dyg - x * ((r * r * r) * dot)
        o_ref[...] = dh
        ob_ref[...] = dh.astype(BF16)
        part = jnp.sum(dy * xr, axis=0, keepdims=True)
        lrow = jnp.broadcast_to(lpart, (1, LANES))

        @pl.when(pl.program_id(0) == 0)
        def _():
            dg_ref[...] = part
            l_ref[...] = lrow

        @pl.when(pl.program_id(0) != 0)
        def _():
            dg_ref[...] += part
            l_ref[...] += lrow

    blk = pl.BlockSpec((tm, D), lambda i: (i, 0))
    row = pl.BlockSpec((1, D), lambda i: (0, 0))
    return pl.pallas_call(
        body, name=name, grid=(T // tm,),
        in_specs=[blk, blk, row],
        out_specs=[blk, blk, row, pl.BlockSpec((1, LANES), lambda i: (0, 0))],
        out_shape=[jax.ShapeDtypeStruct((T, D), F32), jax.ShapeDtypeStruct((T, D), BF16),
                   jax.ShapeDtypeStruct((1, D), F32), jax.ShapeDtypeStruct((1, LANES), F32)],
        compiler_params=_cp(("arbitrary",)),
    )(h, tgt, g)


def _mm_nn(a, b, name, out_dtype=F32, res=None):
    M, K = a.shape
    b3 = b if b.ndim == 3 else b[None]
    nb, _, w = b3.shape
    N = nb * w
    tm = min(TM_MM, M, max(256, MM_BLOCK_BYTES // (4 * N)))
    o_spec = pl.BlockSpec((tm, N), lambda i: (i, 0))
    in_specs = [pl.BlockSpec((tm, K), lambda i: (i, 0)), pl.BlockSpec((nb, K, w), lambda i: (0, 0, 0))]
    args = [a, b3]
    if res is not None:
        in_specs.append(o_spec)
        args.append(res)

    def body(*refs):
        a_ref, b_ref = refs[0], refs[1]
        o_ref = refs[-1]
        av = a_ref[...]
        for j in range(nb):
            cols = slice(j * w, (j + 1) * w)
            acc = jnp.dot(av, b_ref[j], preferred_element_type=F32)
            if res is not None:
                acc = refs[2][:, cols] + acc
            o_ref[:, cols] = acc.astype(out_dtype)

    return pl.pallas_call(
        body, name=name, grid=(M // tm,),
        in_specs=in_specs, out_specs=o_spec,
        out_shape=jax.ShapeDtypeStruct((M, N), out_dtype),
        compiler_params=_cp(("parallel",)),
    )(*args)


def _mm_nt(a_list, b, name, out_dtype=F32, add=None, after=None):
    M, kw = a_list[0].shape
    tm = min(TM, M)
    na = len(a_list)
    blocked = b.ndim == 3
    N = b.shape[1] if blocked else b.shape[0]
    b_spec = pl.BlockSpec(b.shape, lambda i: (0,) * b.ndim)
    o_spec = pl.BlockSpec((tm, N), lambda i: (i, 0))
    in_specs = [pl.BlockSpec((tm, kw), lambda i: (i, 0)) for _ in a_list] + [b_spec]
    args = list(a_list) + [b]
    if add is not None:
        in_specs.append(o_spec)
        args.append(add)
    if after is not None:
        in_specs.append(pl.BlockSpec(memory_space=pl.ANY))
        args.append(after)

    def body(*refs):
        a_refs = refs[:na]
        b_ref = refs[na]
        o_ref = refs[-1]
        acc = refs[na + 1][...] if add is not None else None
        for s, a_ref in enumerate(a_refs):
            if blocked:
                w = b_ref.shape[2]
                per = kw // w
                parts = [_dot_nt(a_ref[:, jj * w:(jj + 1) * w], b_ref[s * per + jj]) for jj in range(per)]
            else:
                parts = [_dot_nt(a_ref[...], b_ref[:, s * kw:(s + 1) * kw])]
            for part in parts:
                acc = part if acc is None else acc + part
        o_ref[...] = acc.astype(out_dtype)

    return pl.pallas_call(
        body, name=name, grid=(M // tm,),
        in_specs=in_specs, out_specs=o_spec,
        out_shape=jax.ShapeDtypeStruct((M, N), out_dtype),
        compiler_params=_cp(("parallel",)),
    )(*args)


def _mm_tn(x, y, name, blocked_w=None, out_dtype=F32):
    T, Kx = x.shape
    N = y.shape[1]
    tt = min(TT, T)
    nt = T // tt
    tkx = min(Kx, max(LANES, MM_BLOCK_BYTES // (4 * N)))
    if blocked_w is not None:
        blk_shape, full_shape = (N // blocked_w, tkx, blocked_w), (N // blocked_w, Kx, blocked_w)
        o_spec = pl.BlockSpec(blk_shape, lambda i, t: (0, i, 0))
    else:
        blk_shape, full_shape = (tkx, N), (Kx, N)
        o_spec = pl.BlockSpec(blk_shape, lambda i, t: (i, 0))

    def body(x_ref, y_ref, o_ref, acc_ref):
        part = _dot_tn(x_ref[...], y_ref[...])
        t = pl.program_id(1)
        if blocked_w is None:
            pieces = [(slice(None), part)]
        else:
            pieces = [(j, part[:, j * blocked_w:(j + 1) * blocked_w]) for j in range(N // blocked_w)]

        @pl.when(t == 0)
        def _():
            for idx, pj in pieces:
                acc_ref[idx] = pj

        @pl.when(t != 0)
        def _():
            for idx, pj in pieces:
                acc_ref[idx] += pj

        @pl.when(t == nt - 1)
        def _():
            o_ref[...] = acc_ref[...].astype(out_dtype)

    return pl.pallas_call(
        body, name=name, grid=(Kx // tkx, nt),
        in_specs=[pl.BlockSpec((tt, tkx), lambda i, t: (t, i)),
                  pl.BlockSpec((tt, N), lambda i, t: (t, 0))],
        out_specs=o_spec, out_shape=jax.ShapeDtypeStruct(full_shape, out_dtype),
        scratch_shapes=[pltpu.VMEM(blk_shape, F32)],
        compiler_params=_cp(("parallel", "arbitrary")),
    )(x, y)


def _sgu_pieces(z, lng, lnb, wc, bs_ref):
    E = z.shape[1] // 2
    gd = E // GM_GROUPS
    zu, zv = z[:, :E], z[:, E:]
    u = _gelu(zu)
    v = _gelu(zv)
    mu = jnp.mean(v, axis=-1, keepdims=True)
    xc = v - mu
    rs = lax.rsqrt(jnp.mean(xc * xc, axis=-1, keepdims=True) + LN_EPS)
    xhat = xc * rs
    vln = xhat * lng + lnb
    s = []
    for g in range(GM_GROUPS):
        vg = vln[:, g * gd:(g + 1) * gd].astype(BF16)
        s.append(jnp.dot(wc[g], vg, preferred_element_type=F32) + bs_ref[g])
    return zu, zv, u, xhat, rs, vln, s


def _causal_ws(ws_ref):
    t = lax.broadcasted_iota(jnp.int32, (CHUNK, CHUNK), 0)
    s = lax.broadcasted_iota(jnp.int32, (CHUNK, CHUNK), 1)
    tri = t >= s
    return [jnp.where(tri, ws_ref[g], 0.0).astype(BF16) for g in range(GM_GROUPS)], tri


def _sgu_fwd(z, lng, lnb, ws, bs, name):
    T, E2 = z.shape
    E = E2 // 2
    gd = E // GM_GROUPS
    tm = min(2 * CHUNK, T)

    def body(z_ref, lng_ref, lnb_ref, ws_ref, bs_ref, o_ref):
        wc, _ = _causal_ws(ws_ref)
        for c in range(tm // CHUNK):
            rows = slice(c * CHUNK, (c + 1) * CHUNK)
            _, _, u, _, _, _, s = _sgu_pieces(z_ref[rows, :], lng_ref[...], lnb_ref[...], wc, bs_ref)
            for g in range(GM_GROUPS):
                cols = slice(g * gd, (g + 1) * gd)
                o_ref[rows, cols] = (u[:, cols] * s[g]).astype(BF16)

    full = lambda shape: pl.BlockSpec(shape, lambda i: (0,) * len(shape))
    return pl.pallas_call(
        body, name=name, grid=(T // tm,),
        in_specs=[pl.BlockSpec((tm, E2), lambda i: (i, 0)), full((1, E)), full((1, E)),
                  full((GM_GROUPS, CHUNK, CHUNK)), full((GM_GROUPS, CHUNK, 1))],
        out_specs=pl.BlockSpec((tm, E), lambda i: (i, 0)),
        out_shape=jax.ShapeDtypeStruct((T, E), BF16),
        compiler_params=_cp(("parallel",)),
    )(z, lng, lnb, ws, bs)


def _sgu_bwd(z, dg, lng, lnb, ws, bs, name):
    T, E2 = z.shape
    E = E2 // 2
    gd = E // GM_GROUPS
    tm = min(2 * CHUNK, T)
    nsteps = T // tm

    def body(z_ref, dg_ref, lng_ref, lnb_ref, ws_ref, bs_ref, dz_ref, dlng_ref, dlnb_ref, dws_ref, dbs_ref):
        i = pl.program_id(0)

        @pl.when(i == 0)
        def _():
            dlng_ref[...] = jnp.zeros_like(dlng_ref)
            dlnb_ref[...] = jnp.zeros_like(dlnb_ref)
            dws_ref[...] = jnp.zeros_like(dws_ref)
            dbs_ref[...] = jnp.zeros_like(dbs_ref)

        wc, tri = _causal_ws(ws_ref)
        lng_v = lng_ref[...]
        for c in range(tm // CHUNK):
            rows = slice(c * CHUNK, (c + 1) * CHUNK)
            zu, zv, u, xhat, rs, vln, s = _sgu_pieces(z_ref[rows, :], lng_v, lnb_ref[...], wc, bs_ref)
            dgc = dg_ref[rows, :].astype(F32)
            du, dvln = [], []
            for g in range(GM_GROUPS):
                cols = slice(g * gd, (g + 1) * gd)
                dgg = dgc[:, cols]
                du.append(dgg * s[g])
                ds = dgg * u[:, cols]
                dsb = ds.astype(BF16)
                dws_ref[g] += _dot_nt(dsb, vln[:, cols].astype(BF16))
                dbs_ref[g] += jnp.sum(ds, axis=-1, keepdims=True)
                dvln.append(_dot_tn(wc[g], dsb))
            du = jnp.concatenate(du, axis=1)
            dvln = jnp.concatenate(dvln, axis=1)
            dlng_ref[...] += jnp.sum(dvln * xhat, axis=0, keepdims=True)
            dlnb_ref[...] += jnp.sum(dvln, axis=0, keepdims=True)
            dxh = dvln * lng_v
            m1 = jnp.mean(dxh, axis=-1, keepdims=True)
            m2 = jnp.mean(dxh * xhat, axis=-1, keepdims=True)
            dv = rs * (dxh - m1 - xhat * m2)
            dz_ref[rows, :E] = (du * _gelu_grad(zu)).astype(BF16)
            dz_ref[rows, E:] = (dv * _gelu_grad(zv)).astype(BF16)

        @pl.when(i == nsteps - 1)
        def _():
            for g in range(GM_GROUPS):
                dws_ref[g] = jnp.where(tri, dws_ref[g], 0.0)

    full = lambda shape: pl.BlockSpec(shape, lambda i: (0,) * len(shape))
    return pl.pallas_call(
        body, name=name, grid=(nsteps,),
        in_specs=[pl.BlockSpec((tm, E2), lambda i: (i, 0)), pl.BlockSpec((tm, E), lambda i: (i, 0)),
                  full((1, E)), full((1, E)), full((GM_GROUPS, CHUNK, CHUNK)), full((GM_GROUPS, CHUNK, 1))],
        out_specs=[pl.BlockSpec((tm, E2), lambda i: (i, 0)), full((1, E)), full((1, E)),
                   full((GM_GROUPS, CHUNK, CHUNK)), full((GM_GROUPS, CHUNK, 1))],
        out_shape=[jax.ShapeDtypeStruct((T, E2), BF16), jax.ShapeDtypeStruct((1, E), F32),
                   jax.ShapeDtypeStruct((1, E), F32), jax.ShapeDtypeStruct((GM_GROUPS, CHUNK, CHUNK), F32),
                   jax.ShapeDtypeStruct((GM_GROUPS, CHUNK, 1), F32)],
        compiler_params=_cp(("arbitrary",)),
    )(z, dg, lng, lnb, ws, bs)


HALO = 16


def _conv_taps(a_ext, w_ref, b_ref):
    n = a_ext.shape[0]
    am1 = pltpu.roll(a_ext, 1, 0)
    am2 = pltpu.roll(a_ext, 2, 0)
    del n
    return ((b_ref[...] + am2 * w_ref[0:1, :]) + am1 * w_ref[1:2, :]) + a_ext * w_ref[2:3, :], am1, am2


def _ffn_mid_fwd(au, cw, cb, name):
    T, F = au.shape[0], au.shape[1] // 2
    tm, tf = min(TM, T), min(TF, F)
    hb = tm // HALO
    nf = F // tf

    def body(a_ref, ap_ref, u_ref, w_ref, b_ref, o_ref):
        i = pl.program_id(1)
        prev = jnp.where(i == 0, 0.0, ap_ref[...])
        ext = jnp.concatenate([prev, a_ref[...]], axis=0)
        conv, _, _ = _conv_taps(ext, w_ref, b_ref)
        conv = conv[HALO:, :]
        o_ref[...] = ((conv * _sigmoid(conv)) * u_ref[...]).astype(BF16)

    main = pl.BlockSpec((tm, tf), lambda f, i: (i, f))
    return pl.pallas_call(
        body, name=name, grid=(nf, T // tm),
        in_specs=[main, pl.BlockSpec((HALO, tf), lambda f, i: (jnp.maximum(i * hb - 1, 0), f)),
                  pl.BlockSpec((tm, tf), lambda f, i: (i, nf + f)),
                  pl.BlockSpec((3, tf), lambda f, i: (0, f)), pl.BlockSpec((1, tf), lambda f, i: (0, f))],
        out_specs=main, out_shape=jax.ShapeDtypeStruct((T, F), BF16),
        compiler_params=_cp(("parallel", "parallel")),
    )(au, au, au, cw, cb)


def _ffn_mid_bwd(au, dact, cw, cb, name):
    T, F = au.shape[0], au.shape[1] // 2
    tm, tf = min(TM, T), min(TF, F)
    hb = tm // HALO
    nt = T // tm
    nf = F // tf
    last_h = T // HALO - 1

    def body(a_ref, ap_ref, an_ref, u_ref, un_ref, d_ref, dn_ref, w_ref, b_ref, da_ref, du_ref, dcw_ref, dcb_ref):
        i = pl.program_id(1)
        prev = jnp.where(i == 0, 0.0, ap_ref[...])
        a_main = a_ref[...]
        a_ext = jnp.concatenate([prev, a_main, an_ref[...]], axis=0)
        conv, am1, am2 = _conv_taps(a_ext, w_ref, b_ref)
        conv = conv[HALO:, :]
        sig = _sigmoid(conv)
        u_ext = jnp.concatenate([u_ref[...], un_ref[...]], axis=0)
        d_ext = jnp.concatenate([d_ref[...], dn_ref[...]], axis=0).astype(F32)
        n = tm + HALO
        row = lax.broadcasted_iota(jnp.int32, (n, 1), 0)
        live = jnp.logical_or(row < tm, i < nt - 1)
        dconv = jnp.where(live, d_ext * u_ext * (sig * (1.0 + conv * (1.0 - sig))), 0.0)
        du_ref[...] = (d_ext[:tm, :] * (conv[:tm, :] * sig[:tm, :])).astype(BF16)
        dp1 = pltpu.roll(dconv, n - 1, 0)[:tm, :]
        dp2 = pltpu.roll(dconv, n - 2, 0)[:tm, :]
        dc = dconv[:tm, :]
        da_ref[...] = ((dc * w_ref[2:3, :] + dp1 * w_ref[1:2, :]) + dp2 * w_ref[0:1, :]).astype(BF16)
        g2 = jnp.sum(dc * a_main, axis=0, keepdims=True)
        g1 = jnp.sum(dc * am1[HALO:HALO + tm, :], axis=0, keepdims=True)
        g0 = jnp.sum(dc * am2[HALO:HALO + tm, :], axis=0, keepdims=True)
        gb = jnp.sum(dc, axis=0, keepdims=True)

        @pl.when(i == 0)
        def _():
            dcw_ref[...] = jnp.zeros_like(dcw_ref)
            dcb_ref[...] = jnp.zeros_like(dcb_ref)

        dcw_ref[0:1, :] += g0
        dcw_ref[1:2, :] += g1
        dcw_ref[2:3, :] += g2
        dcb_ref[...] += gb

    main = pl.BlockSpec((tm, tf), lambda f, i: (i, f))
    prev = pl.BlockSpec((HALO, tf), lambda f, i: (jnp.maximum(i * hb - 1, 0), f))
    nxt = pl.BlockSpec((HALO, tf), lambda f, i: (jnp.minimum((i + 1) * hb, last_h), f))
    main_u = pl.BlockSpec((tm, tf), lambda f, i: (i, nf + f))
    nxt_u = pl.BlockSpec((HALO, tf), lambda f, i: (jnp.minimum((i + 1) * hb, last_h), nf + f))
    return pl.pallas_call(
        body, name=name, grid=(nf, nt),
        in_specs=[main, prev, nxt, main_u, nxt_u, main, nxt,
                  pl.BlockSpec((3, tf), lambda f, i: (0, f)), pl.BlockSpec((1, tf), lambda f, i: (0, f))],
        out_specs=[main, main, pl.BlockSpec((3, tf), lambda f, i: (0, f)), pl.BlockSpec((1, tf), lambda f, i: (0, f))],
        out_shape=[jax.ShapeDtypeStruct((T, F), BF16), jax.ShapeDtypeStruct((T, F), BF16),
                   jax.ShapeDtypeStruct((3, F), F32), jax.ShapeDtypeStruct((1, F), F32)],
        compiler_params=_cp(("parallel", "arbitrary")),
    )(au, au, au, au, au, dact, dact, cw, cb)


def _split3(x):
    hi = x.astype(BF16)
    r1 = x - hi.astype(F32)
    mid = r1.astype(BF16)
    lo = (r1 - mid.astype(F32)).astype(BF16)
    return hi, mid, lo


def _tri_ones(n, upper):
    r = lax.broadcasted_iota(jnp.int32, (n, n), 0)
    c = lax.broadcasted_iota(jnp.int32, (n, n), 1)
    return jnp.where((r <= c) if upper else (r >= c), 1.0, 0.0).astype(BF16)


def _gate_scan(f, bf, name):
    T = f.shape[0]
    tm = min(256, T)

    def body(f_ref, b_ref, cp_ref, sn_ref, carry_ref):
        i = pl.program_id(0)

        @pl.when(i == 0)
        def _():
            carry_ref[...] = jnp.zeros_like(carry_ref)

        x = f_ref[...] + b_ref[...]
        e = jnp.exp(-jnp.abs(x))
        logf = jnp.minimum(x, 0.0) - jnp.log(1.0 + e)
        sn_ref[...] = jnp.where(x >= 0.0, e / (1.0 + e), 1.0 / (1.0 + e))
        tri = _tri_ones(tm, upper=False)
        c = carry_ref[...]
        for piece in _split3(logf):
            c = c + jnp.dot(tri, piece, preferred_element_type=F32)
        carry_ref[...] += jnp.sum(logf, axis=0, keepdims=True)
        hi, mid, lo = _split3(c * LOG2E)
        cp_ref[:, 0:LANES] = hi
        cp_ref[:, LANES:2 * LANES] = mid
        cp_ref[:, 2 * LANES:3 * LANES] = lo

    return pl.pallas_call(
        body, name=name, grid=(T // tm,),
        in_specs=[pl.BlockSpec((tm, LANES), lambda i: (i, 0)), pl.BlockSpec((1, LANES), lambda i: (0, 0))],
        out_specs=[pl.BlockSpec((tm, 3 * LANES), lambda i: (i, 0)), pl.BlockSpec((tm, LANES), lambda i: (i, 0))],
        out_shape=[jax.ShapeDtypeStruct((T, 3 * LANES), BF16), jax.ShapeDtypeStruct((T, LANES), F32)],
        scratch_shapes=[pltpu.VMEM((1, LANES), F32)],
        compiler_params=_cp(("arbitrary",)),
    )(f, bf)


def _gate_scan_bwd(dcq, dck, sneg, name):
    T = dcq.shape[0]
    tm = min(256, T)
    n = T // tm

    def body(dcq_ref, dck_ref, sn_ref, df_ref, db_ref, carry_ref):
        i = pl.program_id(0)

        @pl.when(i == 0)
        def _():
            carry_ref[...] = jnp.zeros_like(carry_ref)
            db_ref[...] = jnp.zeros_like(db_ref)

        tri = _tri_ones(tm, upper=True)
        dcb = dcq_ref[...] - dck_ref[...]
        acc = carry_ref[...]
        for piece in _split3(dcb):
            acc = acc + jnp.dot(tri, piece, preferred_element_type=F32)
        carry_ref[...] += jnp.sum(dcb, axis=0, keepdims=True)
        df = acc * sn_ref[...]
        df_ref[...] = df.astype(BF16)
        db_ref[...] += jnp.sum(df, axis=0, keepdims=True)

    rev = pl.BlockSpec((tm, LANES), lambda i: (n - 1 - i, 0))
    return pl.pallas_call(
        body, name=name, grid=(n,),
        in_specs=[rev, rev, rev],
        out_specs=[rev, pl.BlockSpec((1, LANES), lambda i: (0, 0))],
        out_shape=[jax.ShapeDtypeStruct((T, LANES), BF16), jax.ShapeDtypeStruct((1, LANES), F32)],
        scratch_shapes=[pltpu.VMEM((1, LANES), F32)],
        compiler_params=_cp(("arbitrary",)),
    )(dcq, dck, sneg)


def _qk_proj(hn, w_pad, cp, sel, const, scale, name):
    T, D = hn.shape
    H = w_pad.shape[1] // LANES
    tm = min(TM_MM, T)

    def body(a_ref, w_ref, cp_ref, sel_ref, c_ref, o_ref):
        acc = jnp.dot(a_ref[...], w_ref[...], preferred_element_type=F32)
        if scale != 1.0:
            acc = acc * scale
        acc = acc + jnp.dot(cp_ref[...], sel_ref[...], preferred_element_type=F32) + c_ref[...]
        o_ref[0] = acc[:, :LANES].astype(BF16)
        o_ref[1] = acc[:, LANES:].astype(BF16)

    return pl.pallas_call(
        body, name=name, grid=(T // tm, H // 2),
        in_specs=[pl.BlockSpec((tm, D), lambda i, p: (i, 0)), pl.BlockSpec((D, 2 * LANES), lambda i, p: (0, p)),
                  pl.BlockSpec((tm, 3 * LANES), lambda i, p: (i, 0)),
                  pl.BlockSpec((None, 3 * LANES, 2 * LANES), lambda i, p: (p, 0, 0)),
                  pl.BlockSpec((None, 1, 2 * LANES), lambda i, p: (p, 0, 0))],
        out_specs=pl.BlockSpec((2, tm, LANES), lambda i, p: (p, i, 0)),
        out_shape=jax.ShapeDtypeStruct((H, T, LANES), BF16),
        compiler_params=_cp(("parallel", "arbitrary")),
    )(hn, w_pad, cp, sel, const)


def _lane_lo():
    return lax.broadcasted_iota(jnp.int32, (1, LANES), 1) < HEAD_DIM


def _attn_fwd(qp, kp, v, name):
    H, T, _ = qp.shape
    tq = min(TQ, T)
    nrep = tq // LANES
    n_parts = 4 if tq % 512 == 0 else 1
    rows = tq // n_parts

    def body(q_ref, k_ref, v_ref, o_ref, o32_ref, lse_ref, m_sc, acc_sc):
        i = pl.program_id(1)
        m_sc[...] = jnp.full(m_sc.shape, NEG, F32)
        acc_sc[...] = jnp.zeros_like(acc_sc)
        ones_col = jnp.where(lax.broadcasted_iota(jnp.int32, (tq, LANES), 1) == 0, 1.0, 0.0).astype(BF16)

        def step(j, masked):
            off = pl.multiple_of(j * tq, tq)
            vaug = jnp.concatenate([v_ref[pl.ds(off, tq), :], ones_col], axis=1)
            chains = [(h, rp) for h in range(2) for rp in range(n_parts)]
            s_all = [_dot_nt(q_ref[h, rp * rows:(rp + 1) * rows, :], k_ref[h, pl.ds(off, tq), :]) for h, rp in chains]
            for (h, rp), s in zip(chains, s_all):
                rsl = slice(rp * rows, (rp + 1) * rows)
                tiles = [s[:, c * LANES:(c + 1) * LANES] for c in range(nrep)]
                if masked:
                    r = lax.broadcasted_iota(jnp.int32, (rows, LANES), 0) + rp * rows
                    cc = lax.broadcasted_iota(jnp.int32, (rows, LANES), 1)
                    tiles = [jnp.where(r >= cc + c * LANES, t, NEG) for c, t in enumerate(tiles)]
                mt = tiles[0]
                for t in tiles[1:]:
                    mt = jnp.maximum(mt, t)
                m_prev = m_sc[h, rsl, :]
                m_new = jnp.maximum(m_prev, jnp.max(mt, axis=-1, keepdims=True))
                alpha = jnp.exp2(m_prev - m_new)
                p16 = jnp.concatenate([jnp.exp2(t - m_new).astype(BF16) for t in tiles], axis=1)
                pv = jnp.dot(p16, vaug, preferred_element_type=F32)
                acc_sc[h, rsl, :] = jnp.concatenate([alpha, alpha], axis=1) * acc_sc[h, rsl, :] + pv
                m_sc[h, rsl, :] = m_new

        def loop_body(j, carry):
            step(j, False)
            return carry

        lax.fori_loop(0, i, loop_body, 0)
        step(i, True)
        lo = _lane_lo()
        acc0, acc1 = acc_sc[0], acc_sc[1]
        l0 = jnp.sum(acc0[:, LANES:], axis=-1, keepdims=True)
        l1 = jnp.sum(acc1[:, LANES:], axis=-1, keepdims=True)
        o = jnp.where(lo, acc0[:, :LANES] / l0, acc1[:, :LANES] / l1)
        o_ref[...] = o.astype(BF16)
        o32_ref[...] = o
        lse_ref[...] = jnp.where(lo, m_sc[0] + jnp.log(l0) * LOG2E, m_sc[1] + jnp.log(l1) * LOG2E)

    oblk = pl.BlockSpec((tq, LANES), lambda p, i: (i, p))
    return pl.pallas_call(
        body, name=name, grid=(H // 2, T // tq),
        in_specs=[pl.BlockSpec((2, tq, LANES), lambda p, i: (p, i, 0)),
                  pl.BlockSpec((2, T, LANES), lambda p, i: (p, 0, 0)),
                  pl.BlockSpec((T, LANES), lambda p, i: (0, p))],
        out_specs=[oblk, oblk, pl.BlockSpec((None, tq, LANES), lambda p, i: (p, i, 0))],
        out_shape=[jax.ShapeDtypeStruct((T, H * HEAD_DIM), BF16), jax.ShapeDtypeStruct((T, H * HEAD_DIM), F32),
                   jax.ShapeDtypeStruct((H // 2, T, LANES), F32)],
        scratch_shapes=[pltpu.VMEM((2, tq, LANES), F32), pltpu.VMEM((2, tq, 2 * LANES), F32)],
        compiler_params=_cp(("parallel", "arbitrary")),
    )(qp, kp, v)


def _attn_bwd(qp, kp, v, o, do, lse, scale, name):
    H, T, _ = qp.shape
    tq = min(TQ, T)
    nq = T // tq
    nrep = tq // LANES

    def body(q_ref, k_ref, v_ref, o_ref, do_ref, lse_ref, dq_ref, dk_ref, dv_ref, dqe_ref, dke_ref, dk_sc, dv_sc, dq_sc):
        i = pl.program_id(1)

        @pl.when(i == 0)
        def _():
            dk_sc[...] = jnp.zeros_like(dk_sc)
            dv_sc[...] = jnp.zeros_like(dv_sc)

        dq_sc[...] = jnp.zeros_like(dq_sc)

        lo = _lane_lo()
        dob = do_ref[...]
        dof = dob.astype(F32)
        prod = dof * o_ref[...].astype(F32)
        lse2 = lse_ref[...]
        lse2_sw = pltpu.roll(lse2, HEAD_DIM, 1)
        zero = jnp.zeros_like(dob)
        do_h = [jnp.where(lo, dob, zero), jnp.where(lo, zero, dob)]
        rep = lambda col: jnp.broadcast_to(col, (tq, LANES))
        delta = [rep(jnp.sum(jnp.where(lo, prod, 0.0), axis=-1, keepdims=True)),
                 rep(jnp.sum(jnp.where(lo, 0.0, prod), axis=-1, keepdims=True))]
        lse_h = [jnp.where(lo, lse2, lse2_sw), jnp.where(lo, lse2_sw, lse2)]
        qs = [q_ref[0], q_ref[1]]

        def step(j, masked):
            off = pl.multiple_of(j * tq, tq)
            vblk = v_ref[pl.ds(off, tq), :]
            dv_add = None
            kblks = [k_ref[h, pl.ds(off, tq), :] for h in range(2)]
            s_all = [_dot_nt(qs[h], kblks[h]) for h in range(2)]
            dp_all = [_dot_nt(do_h[h], vblk) for h in range(2)]
            for h in range(2):
                kblk, s, dp = kblks[h], s_all[h], dp_all[h]
                p16, ds16 = [], []
                for c in range(nrep):
                    cols = slice(c * LANES, (c + 1) * LANES)
                    p = jnp.exp2(s[:, cols] - lse_h[h])
                    if masked:
                        r = lax.broadcasted_iota(jnp.int32, (tq, LANES), 0)
                        cc = lax.broadcasted_iota(jnp.int32, (tq, LANES), 1)
                        p = jnp.where(r >= cc + c * LANES, p, 0.0)
                    p16.append(p.astype(BF16))
                    ds16.append((p * (dp[:, cols] - delta[h])).astype(BF16))
                p16 = jnp.concatenate(p16, axis=1)
                dsb = jnp.concatenate(ds16, axis=1)
                dq_sc[h] += jnp.dot(dsb, kblk, preferred_element_type=F32)
                dk_sc[h, pl.ds(off, tq), :] += _dot_tn(dsb, qs[h])
                pv = _dot_tn(p16, do_h[h])
                dv_add = pv if dv_add is None else dv_add + pv
            dv_sc[pl.ds(off, tq), :] += dv_add

        def loop_body(j, carry):
            step(j, False)
            return carry

        lax.fori_loop(0, i, loop_body, 0)
        step(i, True)
        dq0, dq1 = dq_sc[0], dq_sc[1]
        dq_ref[...] = (jnp.where(lo, dq0, pltpu.roll(dq1, HEAD_DIM, 1)) * scale).astype(BF16)
        dqe_ref[0:8, :] = jnp.transpose(dq0)[HEAD_DIM:HEAD_DIM + 8, :]
        dqe_ref[8:16, :] = jnp.transpose(dq1)[HEAD_DIM:HEAD_DIM + 8, :]

        @pl.when(i == nq - 1)
        def _():
            dk0, dk1 = dk_sc[0], dk_sc[1]
            dk_ref[...] = (jnp.where(lo, dk0, pltpu.roll(dk1, HEAD_DIM, 1)) * LN2).astype(BF16)
            for h in range(2):
                for cb in range(nq):
                    blk = jnp.transpose(dk_sc[h, cb * tq:(cb + 1) * tq, :])
                    dke_ref[8 * h:8 * h + 8, cb * tq:(cb + 1) * tq] = blk[HEAD_DIM:HEAD_DIM + 8, :]
            dv_ref[...] = dv_sc[...].astype(BF16)

    qblk = pl.BlockSpec((tq, LANES), lambda p, i: (i, p))
    pair = pl.BlockSpec((T, LANES), lambda p, i: (0, p))
    tok16 = jax.ShapeDtypeStruct((T, H * HEAD_DIM), BF16)
    gate32 = jax.ShapeDtypeStruct((H // 2, 16, T), F32)
    return pl.pallas_call(
        body, name=name, grid=(H // 2, nq),
        in_specs=[pl.BlockSpec((2, tq, LANES), lambda p, i: (p, i, 0)),
                  pl.BlockSpec((2, T, LANES), lambda p, i: (p, 0, 0)),
                  pair, qblk, qblk,
                  pl.BlockSpec((None, tq, LANES), lambda p, i: (p, i, 0))],
        out_specs=[qblk, pair, pair, pl.BlockSpec((None, 16, tq), lambda p, i: (p, 0, i)),
                   pl.BlockSpec((None, 16, T), lambda p, i: (p, 0, 0))],
        out_shape=[tok16, tok16, tok16, gate32, gate32],
        scratch_shapes=[pltpu.VMEM((2, T, LANES), F32), pltpu.VMEM((T, LANES), F32),
                        pltpu.VMEM((2, tq, LANES), F32)],
        compiler_params=_cp(("parallel", "arbitrary")),
    )(qp, kp, v, o, do, lse)


def _mesh_pos():
    return lax.axis_index("x"), lax.axis_index("y"), lax.axis_index("c")


def _all_gather(arrs, name, groups=None):
    n = len(arrs)
    if groups is None:
        groups = [(a, 0) for a in range(n)]
    ng = 1 + max(g for g, _ in groups)
    per_group = [sum(1 for g, _ in groups if g == gi) for gi in range(ng)]
    first_of = [next(a for a in range(n) if groups[a][0] == gi) for gi in range(ng)]

    def body(*refs):
        ins, outs = refs[:n], refs[n:n + ng]
        send_sems, recv_sems, local_sems = refs[n + ng:]
        x, y, c = _mesh_pos()
        me, sib = (x, y, c), (x, y, 1 - c)
        chips = [(1 - x, y), (x, 1 - y), (1 - x, 1 - y)]

        def dst_of(a, px, py, pc):
            g, k = groups[a]
            return outs[g].at[N_DEV * k + 4 * px + 2 * py + pc]

        def copy(a, k, block, to, src=None):
            dst = dst_of(a, *block)
            return pltpu.make_async_remote_copy(
                src_ref=dst if src is None else src, dst_ref=dst,
                send_sem=send_sems.at[a, k], recv_sem=recv_sems.at[a, k], device_id=to, device_id_type=MESH)

        mine = [pltpu.make_async_copy(ins[a], dst_of(a, *me), local_sems.at[a]) for a in range(n)]
        for cp in mine:
            cp.start()
        first = []
        for a in range(n):
            first.append(copy(a, 0, me, sib, src=ins[a]))
            first += [copy(a, 1 + j, me, (*chip, c), src=ins[a]) for j, chip in enumerate(chips)]
        for cp in first:
            cp.start()
        passed = []
        for j, chip in enumerate(chips):
            for a in range(n):
                copy(a, 1 + j, (*chip, c), me).wait_recv()
                fwd = copy(a, 4 + j, (*chip, c), sib)
                fwd.start()
                passed.append(fwd)
        for a in range(n):
            copy(a, 0, sib, me).wait_recv()
            for j, chip in enumerate(chips):
                copy(a, 4 + j, (*chip, 1 - c), me).wait_recv()
        for cp in first + passed:
            cp.wait_send()
        for cp in mine:
            cp.wait()

    any_spec = pl.BlockSpec(memory_space=pl.ANY)
    return pl.pallas_call(
        body, name=name,
        in_specs=[any_spec] * n, out_specs=[any_spec] * ng,
        out_shape=[jax.ShapeDtypeStruct((N_DEV * per_group[gi],) + arrs[first_of[gi]].shape, arrs[first_of[gi]].dtype)
                   for gi in range(ng)],
        scratch_shapes=[pltpu.SemaphoreType.DMA((n, 7)), pltpu.SemaphoreType.DMA((n, 7)),
                        pltpu.SemaphoreType.DMA((n,))],
    )(*arrs)


def _pair_exchange(gs, name):
    n = len(gs)

    def body(*refs):
        g_refs, o_refs = refs[:n], refs[n:2 * n]
        send_sems, recv_sems = refs[2 * n:]
        x, y, c = _mesh_pos()
        sib = (x, y, 1 - c)
        copies = []
        for a in range(n):
            for j in range(4):
                copies.append(pltpu.make_async_remote_copy(
                    src_ref=g_refs[a].at[2 * j + (1 - c)], dst_ref=o_refs[a].at[j],
                    send_sem=send_sems.at[a, j], recv_sem=recv_sems.at[a, j], device_id=sib, device_id_type=MESH))
        for cp in copies:
            cp.start()
        for cp in copies:
            cp.wait_recv()
        for cp in copies:
            cp.wait_send()

    any_spec = pl.BlockSpec(memory_space=pl.ANY)
    return pl.pallas_call(
        body, name=name, in_specs=[any_spec] * n, out_specs=[any_spec] * n,
        out_shape=[jax.ShapeDtypeStruct((4,) + g.shape[1:], g.dtype) for g in gs],
        scratch_shapes=[pltpu.SemaphoreType.DMA((n, 4)), pltpu.SemaphoreType.DMA((n, 4))],
    )(*gs)


def _chip_exchange(parts, name):
    n = len(parts)

    def body(*refs):
        p_refs, o_refs = refs[:n], refs[n:2 * n]
        send_sems, recv_sems = refs[2 * n:]
        x, y, c = _mesh_pos()
        chips = [(1 - x, y), (x, 1 - y), (1 - x, 1 - y)]
        copies = []
        for a in range(n):
            for k, (px, py) in enumerate(chips):
                copies.append(pltpu.make_async_remote_copy(
                    src_ref=p_refs[a].at[2 * px + py], dst_ref=o_refs[a].at[k],
                    send_sem=send_sems.at[a, k], recv_sem=recv_sems.at[a, k], device_id=(px, py, c),
                    device_id_type=MESH))
        for cp in copies:
            cp.start()
        for cp in copies:
            cp.wait_recv()
        for cp in copies:
            cp.wait_send()

    any_spec = pl.BlockSpec(memory_space=pl.ANY)
    return pl.pallas_call(
        body, name=name, in_specs=[any_spec] * n, out_specs=[any_spec] * n,
        out_shape=[jax.ShapeDtypeStruct((3,) + p.shape[1:], p.dtype) for p in parts],
        scratch_shapes=[pltpu.SemaphoreType.DMA((n, 3)), pltpu.SemaphoreType.DMA((n, 3))],
    )(*parts)


HBM_SPEC = pl.BlockSpec(memory_space=pltpu.HBM)
SEM_SPEC = pl.BlockSpec(memory_space=pltpu.SEMAPHORE)
ANY_SPEC = pl.BlockSpec(memory_space=pl.ANY)
DATAFLOW_EFFECT = pltpu.SideEffectType.DATAFLOW_SIDE_EFFECTING


def _peers():
    x, y, c = _mesh_pos()
    flip = lambda v, b: 1 - v if b else v
    return [(flip(x, (k >> 2) & 1), flip(y, (k >> 1) & 1), flip(c, k & 1)) for k in range(1, N_DEV)]


def _slot(p):
    return 4 * p[0] + 2 * p[1] + p[2]


def _direct_copy(src_refs, land_refs, sems, a, k, p, land_of, dst_slot, src_slot):
    s = src_slot(a, p)
    return pltpu.make_async_remote_copy(
        src_ref=src_refs[a] if s is None else src_refs[a].at[s], dst_ref=land_refs[land_of[a]].at[dst_slot(a, k)],
        send_sem=sems[0].at[a * (N_DEV - 1) + k], recv_sem=sems[1].at[a * (N_DEV - 1) + k], device_id=p,
        device_id_type=MESH)


def _direct_start(srcs, lands, land_of, dst_slot, src_slot, after, name, collective_id):
    n, nl = len(srcs), len(lands)

    def body(*refs):
        src_refs, land_refs = refs[:n], refs[n:n + nl]
        sems = (refs[n + nl + 1], refs[n + nl + 2])
        token = refs[-1]
        peers = _peers()
        barrier = pltpu.get_barrier_semaphore()
        for p in peers:
            pl.semaphore_signal(barrier, inc=1, device_id=p, device_id_type=MESH)
        pl.semaphore_wait(barrier, N_DEV - 1)
        for a in range(n):
            for k, p in enumerate(peers):
                _direct_copy(src_refs, land_refs, sems, a, k, p, land_of, dst_slot, src_slot).start()
        token[...] = jnp.zeros_like(token)

    hbm = lambda t: pltpu.HBM(t.shape, t.dtype)
    sem_t = pltpu.SemaphoreType.DMA((n * (N_DEV - 1),))
    outs = pl.pallas_call(
        body, name=name,
        out_shape=(sem_t, sem_t, *[hbm(t) for t in srcs], *[hbm(t) for t in lands], jax.ShapeDtypeStruct((8, LANES), F32)),
        in_specs=[HBM_SPEC] * (n + nl) + [ANY_SPEC],
        out_specs=(SEM_SPEC, SEM_SPEC, *([HBM_SPEC] * (n + nl)), pl.BlockSpec(memory_space=pltpu.VMEM)),
        input_output_aliases={i: 2 + i for i in range(n + nl)},
        compiler_params=pltpu.CompilerParams(has_side_effects=DATAFLOW_EFFECT, collective_id=collective_id),
    )(*[pltpu.with_memory_space_constraint(t, pltpu.HBM) for t in srcs],
      *[pltpu.with_memory_space_constraint(t, pltpu.HBM) for t in lands], after)
    return outs[0], outs[1], list(outs[2:2 + n]), list(outs[2 + n:2 + n + nl]), outs[-1]


def _direct_wait(send_sems, recv_sems, srcs, lands, land_of, idxs, dst_slot, src_slot, after, name):
    land_ids = []
    for a in idxs:
        if land_of[a] not in land_ids:
            land_ids.append(land_of[a])
    m, ml = len(idxs), len(land_ids)
    sub_land_of = {j: land_ids.index(land_of[a]) for j, a in enumerate(idxs)}

    def body(*refs):
        src_refs, land_refs = refs[:m], refs[m:m + ml]
        ssem, rsem = refs[m + ml], refs[m + ml + 1]
        for j, a in enumerate(idxs):
            for k, p in enumerate(_peers()):
                s = src_slot(a, p)
                cp = pltpu.make_async_remote_copy(
                    src_ref=src_refs[j] if s is None else src_refs[j].at[s],
                    dst_ref=land_refs[sub_land_of[j]].at[dst_slot(a, k)],
                    send_sem=ssem.at[a * (N_DEV - 1) + k], recv_sem=rsem.at[a * (N_DEV - 1) + k], device_id=p,
                    device_id_type=MESH)
                cp.wait_send()
                cp.wait_recv()

    hbm = lambda t: pltpu.HBM(t.shape, t.dtype)
    sub_s, sub_l = [srcs[a] for a in idxs], [lands[g] for g in land_ids]
    outs = pl.pallas_call(
        body, name=name,
        out_shape=(*[hbm(t) for t in sub_s], *[hbm(t) for t in sub_l]),
        in_specs=[HBM_SPEC] * (m + ml) + [SEM_SPEC, SEM_SPEC, ANY_SPEC],
        out_specs=tuple([HBM_SPEC] * (m + ml)),
        input_output_aliases={i: i for i in range(m + ml)},
        compiler_params=pltpu.CompilerParams(has_side_effects=DATAFLOW_EFFECT),
    )(*sub_s, *sub_l, send_sems, recv_sems, after)
    return list(outs[:m]), list(outs[m:])


def _row_block(R, C):
    best = None
    for d in range(16, R + 1, 16):
        if R % d == 0 and d * C <= 256 * 1024:
            best = d
    return best if best is not None else R


def _pair_add(g, recv, cidx, name):
    _, R, C = g.shape
    tr = _row_block(R, C)

    def body(c_ref, g_ref, r_ref, o_ref):
        del c_ref
        o_ref[...] = (g_ref[...].astype(F32) + r_ref[...].astype(F32)).astype(BF16)

    grid_spec = pltpu.PrefetchScalarGridSpec(
        num_scalar_prefetch=1, grid=(4, R // tr),
        in_specs=[pl.BlockSpec((None, tr, C), lambda j, i, c: (2 * j + c[0], i, 0)),
                  pl.BlockSpec((None, tr, C), lambda j, i, c: (j, i, 0))],
        out_specs=pl.BlockSpec((None, tr, C), lambda j, i, c: (j, i, 0)))
    return pl.pallas_call(
        body, name=name, grid_spec=grid_spec,
        out_shape=jax.ShapeDtypeStruct((4, R, C), BF16),
        compiler_params=_cp(("parallel", "parallel")),
    )(cidx, g, recv)


def _adamw_math(w, g, m, v):
    m = ADAM_B1 * m + (1.0 - ADAM_B1) * g
    v = ADAM_B2 * v + (1.0 - ADAM_B2) * (g * g)
    m_hat = m / (1.0 - ADAM_B1 ** ADAM_STEP)
    v_hat = v / (1.0 - ADAM_B2 ** ADAM_STEP)
    delta = -ADAM_LR * (m_hat / (jnp.sqrt(v_hat) + ADAM_EPS) + ADAM_WD * w)
    return delta, m, v


def _sum_adamw(parts, w, m, v, name, sel=None):
    R, C = w.shape
    tr = _row_block(R, C)
    specs, args = [], []
    for arr, idxs in parts:
        for idx in idxs:
            if idx < 0:
                specs.append(pl.BlockSpec((None, tr, C), lambda i, s: (s[0], i, 0)))
            else:
                specs.append(pl.BlockSpec((None, tr, C), lambda i, s, idx=idx: (idx, i, 0)))
            args.append(arr)
    npart = len(args)
    blk = pl.BlockSpec((tr, C), lambda i, s: (i, 0))

    def body(s_ref, *refs):
        del s_ref
        g = refs[0][...].astype(F32)
        for r in refs[1:npart]:
            g = g + r[...].astype(F32)
        w_ref, m_ref, v_ref, g_out, d_out, m_out, v_out = refs[npart:]
        delta, mm, vv = _adamw_math(w_ref[...], g, m_ref[...], v_ref[...])
        g_out[...] = g
        d_out[...] = delta
        m_out[...] = mm
        v_out[...] = vv

    grid_spec = pltpu.PrefetchScalarGridSpec(
        num_scalar_prefetch=1, grid=(R // tr,),
        in_specs=specs + [blk, blk, blk], out_specs=[blk] * 4)
    if sel is None:
        sel = jnp.zeros((1,), jnp.int32)
    return pl.pallas_call(
        body, name=name, grid_spec=grid_spec,
        out_shape=[jax.ShapeDtypeStruct((R, C), F32)] * 4,
        compiler_params=_cp(("parallel",)),
    )(sel, *args, w, m, v)


def _rows(a, c):
    return a.reshape(-1, c)


def _pad_rows(a, r):
    return jnp.pad(a, ((0, r - a.shape[0]), (0, 0))) if a.shape[0] != r else a


def _gate_tables():
    hp = N_HEADS // 2
    sel_q = np.zeros((hp, 3 * LANES, 2 * LANES), np.float32)
    sel_k = np.zeros((hp, 3 * LANES, 2 * LANES), np.float32)
    const_q = np.zeros((hp, 1, 2 * LANES), np.float32)
    const_k = np.zeros((hp, 1, 2 * LANES), np.float32)
    for p in range(hp):
        for hh in range(2):
            h = 2 * p + hh
            base = hh * LANES + HEAD_DIM
            for piece in range(3):
                sel_q[p, piece * LANES + h, base + piece] = 1.0
                sel_k[p, piece * LANES + h, base + 3 + piece] = -1.0
            const_k[p, 0, base:base + 3] = 1.0
            const_q[p, 0, base + 3:base + 6] = 1.0
    as_bf = lambda t: jnp.asarray(t, BF16)
    return as_bf(sel_q), as_bf(sel_k), jnp.asarray(const_q), jnp.asarray(const_k)


def _pad_heads(w):
    d = w.shape[0]
    w3 = w.reshape(d, N_HEADS, HEAD_DIM)
    return jnp.pad(w3, ((0, 0), (0, 0), (0, LANES - HEAD_DIM))).reshape(d, N_HEADS * LANES)


def kernel(x, mix_norm_g, ffn_norm_g, gm_w_in, gm_ln_g, gm_ln_b, gm_w_s, gm_b_s, gm_w_out, fox_w_qkvf, fox_b_f, fox_w_o, ffn_w_gate, ffn_w_up, ffn_conv_w, ffn_conv_b, ffn_w_down, final_norm_g, loss_target, m_mix_norm_g, m_ffn_norm_g, m_gm_w_in, m_gm_ln_g, m_gm_ln_b, m_gm_w_s, m_gm_b_s, m_gm_w_out, m_fox_w_qkvf, m_fox_b_f, m_fox_w_o, m_ffn_w_gate, m_ffn_w_up, m_ffn_conv_w, m_ffn_conv_b, m_ffn_w_down, m_final_norm_g, v_mix_norm_g, v_ffn_norm_g, v_gm_w_in, v_gm_ln_g, v_gm_ln_b, v_gm_w_s, v_gm_b_s, v_gm_w_out, v_fox_w_qkvf, v_fox_b_f, v_fox_w_o, v_ffn_w_gate, v_ffn_w_up, v_ffn_conv_w, v_ffn_conv_b, v_ffn_w_down, v_final_norm_g):
    T, D = x.shape[1], x.shape[2]
    E = gm_ln_g.shape[1]
    FF = ffn_conv_b.shape[1]
    NQKVF = 3 * D + N_HEADS
    xi, yi, ci = _mesh_pos()
    me = 4 * xi + 2 * yi + ci
    h0 = x.reshape(T, D)
    tgt = loss_target.reshape(T, D)

    nl = ffn_w_gate.shape[0]
    to16 = lambda a: a.astype(BF16)
    n_cw_rows = ffn_conv_w.size // LANES
    cw_rows = _pad_rows(_rows(ffn_conv_w.astype(F32), LANES), 16)
    w_in_g, w_out_g8, cwg = _all_gather([to16(gm_w_in[0]), to16(gm_w_out[0]), cw_rows], "ag_weights")
    w_out_g = w_out_g8.reshape(E, D)
    later, land_of, land_off, lands = [], [], [], []
    for l in range(nl):
        later += [to16(ffn_w_gate[l]), to16(ffn_w_up[l]), to16(ffn_w_down[l])]
        land_of += [2 * l, 2 * l, 2 * l + 1]
        land_off += [0, N_DEV, 0]
        lands += [lax.empty((2 * N_DEV, D, FF // N_DEV), BF16), lax.empty((N_DEV, FF // N_DEV, D), BF16)]
    later += [to16(fox_w_qkvf[0]), to16(fox_w_o[0])]
    land_of += [2 * nl, 2 * nl + 1]
    land_off += [0, 0]
    lands += [lax.empty((N_DEV, D, NQKVF // N_DEV), BF16), lax.empty((N_DEV, D // N_DEV, D), BF16)]
    ag_dst = lambda a, k: land_off[a] + _slot(_mesh_pos())
    ag_src = lambda a, p: None
    ag_send, ag_recv, later, lands, ag_token = _direct_start(later, lands, land_of, ag_dst, ag_src, w_in_g,
                                                             "ag_later_start", collective_id=1)

    def own_blocks(landed, shards, offs):
        for s, o in zip(shards, offs):
            landed = lax.dynamic_update_index_in_dim(landed, s, o + me, 0)
        return landed

    def gather_wait(idxs, after, name):
        return _direct_wait(ag_send, ag_recv, later, lands, land_of, idxs, ag_dst, ag_src, after, name)

    conv_w_full = jnp.transpose(cwg[:, :n_cw_rows].reshape(N_DEV, nl, 3, FF // N_DEV), (1, 2, 0, 3)).reshape(nl, 3, FF)

    ffn_w = {}

    def ffn_weights(l):
        return ffn_w[l]

    def land_ffn(l, shards, gu_land, dn_land):
        ffn_w[l] = (own_blocks(gu_land, shards[:2], [0, N_DEV]), own_blocks(dn_land, shards[2:3], [0]).reshape(FF, D))

    saved = {}

    def ffn_fwd(l, h_in):
        wgul, wdl = ffn_weights(l)
        hn = _rms_fwd(h_in, ffn_norm_g[l:l + 1], f"ffn{l}_norm")
        au = _mm_nn(hn, wgul, f"ffn{l}_gate_up")
        act = _ffn_mid_fwd(au, conv_w_full[l], ffn_conv_b[l:l + 1], f"ffn{l}_mid")
        h_out = _mm_nn(act, wdl, f"ffn{l}_down", res=h_in)
        saved[f"ffn{l}"] = (h_in, hn, au, act)
        return h_out

    bs_col = gm_b_s[0].reshape(GM_GROUPS, CHUNK, 1)
    hn0 = _rms_fwd(h0, mix_norm_g[0:1], "mix0_norm", after=ag_token)
    z = _mm_nn(hn0, w_in_g, "gm_in")
    gu = _sgu_fwd(z, gm_ln_g, gm_ln_b, gm_w_s[0], bs_col, "gm_sgu")
    h1 = _mm_nn(gu, w_out_g, "gm_out", res=h0)
    mine0, land0 = gather_wait([0, 1, 2], h1, "ag_ffn0_wait")
    land_ffn(0, mine0, *land0)
    h2 = ffn_fwd(0, h1)

    mine1, rest = gather_wait(list(range(3, 3 * nl + 2)), h2, "ag_layer1_wait")
    for l in range(1, nl):
        land_ffn(l, mine1[3 * (l - 1):3 * l], rest[2 * (l - 1)], rest[2 * (l - 1) + 1])
    w_qkvf = jnp.transpose(own_blocks(rest[-2], mine1[-2:-1], [0]), (1, 0, 2)).reshape(D, NQKVF)
    w_o_g = own_blocks(rest[-1], mine1[-1:], [0]).reshape(D, D)
    w_q, w_k, w_v = w_qkvf[:, :D], w_qkvf[:, D:2 * D], w_qkvf[:, 2 * D:3 * D]
    w_f = jnp.pad(w_qkvf[:, 3 * D:], ((0, 0), (0, LANES - N_HEADS)))
    bf_row = jnp.pad(fox_b_f, ((0, 0), (0, LANES - N_HEADS)))
    sel_q, sel_k, const_q, const_k = _gate_tables()
    scale = HEAD_DIM ** -0.5
    hn2 = _rms_fwd(h2, mix_norm_g[1:2], "mix1_norm")
    f_logit = _mm_nn(hn2, w_f, "fox_f")
    cp, sneg = _gate_scan(f_logit, bf_row, "fox_scan")
    qp = _qk_proj(hn2, _pad_heads(w_q), cp, sel_q, const_q, scale * LOG2E, "fox_q")
    kp = _qk_proj(hn2, _pad_heads(w_k), cp, sel_k, const_k, 1.0, "fox_k")
    vv = _mm_nn(hn2, w_v, "fox_v", out_dtype=BF16)
    o, o32, lse = _attn_fwd(qp, kp, vv, "fox_attn")
    h3 = _mm_nn(o, w_o_g, "fox_o", res=h2)
    h4 = ffn_fwd(1, h3)

    dh, dh16, d_final, loss_row = _loss_head(h4, tgt, final_norm_g.reshape(1, D), "loss_head")
    loss = lax.psum(loss_row[0, 0], ("x", "y", "c"))

    rs_dst = lambda a, k: k
    rs_src = lambda a, p: _slot(p)
    me_idx = me.astype(jnp.int32).reshape(1)

    def rs_start(grads, name, cid):
        lands = [lax.empty((N_DEV - 1,) + g.shape[1:], BF16) for g in grads]
        return _direct_start(grads, lands, list(range(len(grads))), rs_dst, rs_src, loss_row, name, collective_id=cid)

    def rs_wait(st, after, name):
        n = len(st[2])
        return _direct_wait(st[0], st[1], st[2], st[3], list(range(n)), list(range(n)), rs_dst, rs_src, after, name)

    def ffn_bwd(l, dh, dh16, after=None):
        wgul, wdl = ffn_weights(l)
        h_in, hn, au, act = saved[f"ffn{l}"]
        dact = _mm_nt([dh16], wdl, f"ffn{l}_dact", out_dtype=BF16, after=after)
        d_wd = _mm_tn(act, dh16, f"ffn{l}_dwd", out_dtype=BF16)
        da, dup, d_cw, d_cb = _ffn_mid_bwd(au, dact, conv_w_full[l], ffn_conv_b[l:l + 1], f"ffn{l}_dmid")
        dhn = _mm_nt([da, dup], wgul, f"ffn{l}_dhn")
        d_wg = _mm_tn(hn, da, f"ffn{l}_dwg", blocked_w=FF // N_DEV, out_dtype=BF16)
        d_wu = _mm_tn(hn, dup, f"ffn{l}_dwu", blocked_w=FF // N_DEV, out_dtype=BF16)
        dh_in, dh_in16, d_norm = _rms_bwd(dhn, h_in, ffn_norm_g[l:l + 1], dh, f"ffn{l}_dnorm")
        big_g = [d_wg, d_wu, d_wd.reshape(N_DEV, FF // N_DEV, D)]
        return dh_in, dh_in16, big_g, dict(cw=d_cw, cb=d_cb, norm=d_norm)

    dh, dh16, big_ffn1, g_ffn1 = ffn_bwd(1, dh, dh16)

    do = _mm_nt([dh16], w_o_g, "fox_do", out_dtype=BF16)
    d_wo = _mm_tn(o, dh16, "fox_dwo", out_dtype=BF16)
    dq, dk, dv, dqe, dke = _attn_bwd(qp, kp, vv, o32, do, lse, scale, "fox_dattn")
    gate_lane = lambda e, r: jnp.pad(jnp.transpose(e[:, r::8, :].reshape(N_HEADS, T)), ((0, 0), (0, LANES - N_HEADS)))
    df, d_bf = _gate_scan_bwd(gate_lane(dqe, 0), gate_lane(dke, 3), sneg, "fox_dscan")
    dhn = _mm_nt([df], w_f, "fox_dhn_f")
    dhn = _mm_nt([dq, dk, dv], w_qkvf[:, :3 * D], "fox_dhn_qkv", add=dhn)
    d_wq = _mm_tn(hn2, dq, "fox_dwq", out_dtype=BF16)
    d_wk = _mm_tn(hn2, dk, "fox_dwk", out_dtype=BF16)
    d_wv = _mm_tn(hn2, dv, "fox_dwv", out_dtype=BF16)
    d_wf = _mm_tn(hn2, df, "fox_dwf", out_dtype=BF16)
    d_wqkvf = jnp.concatenate([d_wq, d_wk, d_wv, d_wf[:, :N_HEADS]], axis=1)
    dh, dh16, d_mix1 = _rms_bwd(dhn, h2, mix_norm_g[1:2], dh, "mix1_dnorm")
    st1 = rs_start([jnp.transpose(d_wqkvf.reshape(D, N_DEV, NQKVF // N_DEV), (1, 0, 2)),
                    d_wo.reshape(N_DEV, D // N_DEV, D)] + big_ffn1, "rs1_start", 2)

    dh, dh16, big_ffn0, g_ffn0 = ffn_bwd(0, dh, dh16, after=st1[4])
    st2 = rs_start(big_ffn0, "rs2_start", 3)

    dgu = _mm_nt([dh16], w_out_g, "gm_dgu", out_dtype=BF16, after=st2[4])
    d_wout = _mm_tn(gu, dh16, "gm_dwout", out_dtype=BF16)
    dz, d_lng, d_lnb, d_ws, d_bs = _sgu_bwd(z, dgu, gm_ln_g, gm_ln_b, gm_w_s[0], bs_col, "gm_dsgu")
    d_win = _mm_tn(hn0, dz, "gm_dwin", blocked_w=2 * E // N_DEV, out_dtype=BF16)
    st3 = rs_start([d_win, d_wout.reshape(N_DEV, E // N_DEV, D)], "rs3_start", 4)
    dhn = _mm_nt([dz], w_in_g, "gm_dhn", after=st3[4])
    dx, _, d_mix0 = _rms_bwd(dhn, h0, mix_norm_g[0:1], dh, "mix0_dnorm")

    own1, land1 = rs_wait(st1, dx, "rs1_wait")
    own2, land2 = rs_wait(st2, land1[0], "rs2_wait")
    cat1 = lambda a, b: jnp.concatenate([a, b], axis=1)
    big_out = {}

    def big_adamw(name, w, m, v, own, landed):
        shard2d = lambda a, c=own.shape[2]: a.reshape(-1, c)
        res = _sum_adamw([(own, [-1]), (landed, list(range(N_DEV - 1)))], shard2d(w), shard2d(m), shard2d(v),
                         f"adamw_{name}", sel=me_idx)
        big_out[name] = [t.reshape(w.shape) for t in res]

    big_adamw("fox_w_qkvf", fox_w_qkvf, m_fox_w_qkvf, v_fox_w_qkvf, own1[0], land1[0])
    big_adamw("fox_w_o", fox_w_o, m_fox_w_o, v_fox_w_o, own1[1], land1[1])
    big_adamw("ffn_w_gate", ffn_w_gate, m_ffn_w_gate, v_ffn_w_gate, cat1(own2[0], own1[2]), cat1(land2[0], land1[2]))
    big_adamw("ffn_w_up", ffn_w_up, m_ffn_w_up, v_ffn_w_up, cat1(own2[1], own1[3]), cat1(land2[1], land1[3]))
    big_adamw("ffn_w_down", ffn_w_down, m_ffn_w_down, v_ffn_w_down, cat1(own2[2], own1[4]), cat1(land2[2], land1[4]))

    small = [("mix_norm_g", mix_norm_g, m_mix_norm_g, v_mix_norm_g, jnp.concatenate([d_mix0, d_mix1], axis=0)),
             ("ffn_norm_g", ffn_norm_g, m_ffn_norm_g, v_ffn_norm_g, jnp.concatenate([g_ffn0["norm"], g_ffn1["norm"]], axis=0)),
             ("gm_ln_g", gm_ln_g, m_gm_ln_g, v_gm_ln_g, d_lng),
             ("gm_ln_b", gm_ln_b, m_gm_ln_b, v_gm_ln_b, d_lnb),
             ("gm_w_s", gm_w_s, m_gm_w_s, v_gm_w_s, d_ws),
             ("gm_b_s", gm_b_s, m_gm_b_s, v_gm_b_s, d_bs),
             ("fox_b_f", fox_b_f, m_fox_b_f, v_fox_b_f, d_bf[:, :N_HEADS]),
             ("ffn_conv_b", ffn_conv_b, m_ffn_conv_b, v_ffn_conv_b, jnp.concatenate([g_ffn0["cb"], g_ffn1["cb"]], axis=0)),
             ("final_norm_g", final_norm_g, m_final_norm_g, v_final_norm_g, d_final)]
    d_cw_full = jnp.stack([g_ffn0["cw"], g_ffn1["cw"]], axis=0)

    def small_rows(a):
        flat = a.astype(F32).reshape(-1)
        n = -(-flat.size // (8 * LANES)) * (8 * LANES)
        return jnp.pad(flat, (0, n - flat.size)).reshape(-1, LANES)

    s_rows = [small_rows(p[1]).shape[0] for p in small]
    s_off = np.concatenate([[0], np.cumsum(s_rows)]).tolist()
    cw_g_rows = small_rows(d_cw_full)
    zeros_cw = jnp.zeros_like(cw_g_rows)
    cat = lambda k: jnp.concatenate([small_rows(p[k]) for p in small] + [zeros_cw], axis=0)
    g_small = jnp.concatenate([small_rows(p[4]) for p in small] + [cw_g_rows], axis=0)
    (gs_all,) = _all_gather([g_small], "ag_small_grads")
    small_out = _sum_adamw([(gs_all, list(range(N_DEV)))], cat(1), cat(2), cat(3), "adamw_small")
    gs = small_out[0]

    g_cw_full = gs[s_off[-1]:].reshape(-1)[:d_cw_full.size].reshape(d_cw_full.shape)
    g_cw = lax.dynamic_slice_in_dim(g_cw_full, me * (FF // N_DEV), FF // N_DEV, axis=2)
    cw2 = lambda a: _pad_rows(_rows(a.astype(F32), LANES), 16)
    cw_out = _sum_adamw([(cw2(g_cw)[None], [0])], cw2(ffn_conv_w), cw2(m_ffn_conv_w), cw2(v_ffn_conv_w), "adamw_conv_w")

    own3, land3 = rs_wait(st3, cw_out[0], "rs3_wait")
    big_adamw("gm_w_in", gm_w_in, m_gm_w_in, v_gm_w_in, own3[0], land3[0])
    big_adamw("gm_w_out", gm_w_out, m_gm_w_out, v_gm_w_out, own3[1], land3[1])

    names = ["mix_norm_g", "ffn_norm_g", "gm_w_in", "gm_ln_g", "gm_ln_b", "gm_w_s", "gm_b_s", "gm_w_out", "fox_w_qkvf",
             "fox_b_f", "fox_w_o", "ffn_w_gate", "ffn_w_up", "ffn_conv_w", "ffn_conv_b", "ffn_w_down", "final_norm_g"]
    small_idx = {p[0]: k for k, p in enumerate(small)}

    def pick(kind, name):
        if name in big_out:
            return big_out[name][kind]
        if name == "ffn_conv_w":
            return cw_out[kind][:n_cw_rows].reshape(ffn_conv_w.shape)
        k = small_idx[name]
        shp = small[k][1].shape
        return small_out[kind][s_off[k]:s_off[k + 1]].reshape(-1)[:int(np.prod(shp))].reshape(shp)

    outs = [loss, dx.reshape(x.shape)]
    for kind in range(4):
        outs += [pick(kind, n) for n in names]
    return tuple(outs)
```

```python
import functools
import math

import numpy as np
import jax
import jax.numpy as jnp
from jax import lax
from jax.experimental import pallas as pl
from jax.experimental.pallas import tpu as pltpu

F32 = jnp.float32
BF16 = jnp.bfloat16
MESH = pl.DeviceIdType.MESH

N_HEADS = 16
HEAD_DIM = 64
CHUNK = 128
GM_GROUPS = 8
RMS_EPS = 1e-6
LN_EPS = 1e-5
ADAM_LR = 0.001
ADAM_B1 = 0.9
ADAM_B2 = 0.999
ADAM_EPS = 1e-08
ADAM_WD = 0.01
ADAM_STEP = 10
N_DEV = 8

LANES = 128
VMEM_BYTES_V7X = 64 * 1024 * 1024
VMEM_LIMIT = 56 * 1024 * 1024

TM = 512
TM_MM = 1024
TT = 1024
TQ = 512
TF = 512
MM_BLOCK_BYTES = 8 * 1024 * 1024
NEG = -1e30
LOG2E = math.log2(math.e)
LN2 = math.log(2.0)


def _cp(sem=None, vmem=VMEM_LIMIT):
    return pltpu.CompilerParams(dimension_semantics=sem, vmem_limit_bytes=vmem)


def _gelu(x):
    c = math.sqrt(2.0 / math.pi)
    return x * (0.5 * (1.0 + jnp.tanh(c * (x + 0.044715 * (x * x * x)))))


def _gelu_grad(x):
    c = math.sqrt(2.0 / math.pi)
    t = jnp.tanh(c * (x + 0.044715 * (x * x * x)))
    return 0.5 * (1.0 + t) + x * (0.5 * (1.0 - t * t)) * (c * (1.0 + 3.0 * 0.044715 * (x * x)))


def _sigmoid(x):
    return 1.0 / (1.0 + jnp.exp(-x))


def _dot_nt(a, b):
    return lax.dot_general(a, b, (((1,), (1,)), ((), ())), preferred_element_type=F32)


def _dot_tn(a, b):
    return lax.dot_general(a, b, (((0,), (0,)), ((), ())), preferred_element_type=F32)


def _rms_fwd(h, g, name, after=None):
    T, D = h.shape
    tm = min(TM, T)

    def body(h_ref, g_ref, *rest):
        o_ref = rest[-1]
        x = h_ref[...]
        r = lax.rsqrt(jnp.mean(x * x, axis=-1, keepdims=True) + RMS_EPS)
        o_ref[...] = ((x * r) * g_ref[...]).astype(BF16)

    in_specs = [pl.BlockSpec((tm, D), lambda i: (i, 0)), pl.BlockSpec((1, D), lambda i: (0, 0))]
    args = [h, g]
    if after is not None:
        in_specs.append(pl.BlockSpec(memory_space=pl.ANY))
        args.append(after)
    return pl.pallas_call(
        body, name=name, grid=(T // tm,),
        in_specs=in_specs,
        out_specs=pl.BlockSpec((tm, D), lambda i: (i, 0)),
        out_shape=jax.ShapeDtypeStruct((T, D), BF16),
        compiler_params=_cp(("parallel",)),
    )(*args)


def _rms_bwd(dhn, h, g, dres, name):
    T, D = h.shape
    tm = min(TM, T)

    def body(d_ref, h_ref, g_ref, r_ref, o_ref, ob_ref, dg_ref):
        x = h_ref[...]
        d = d_ref[...]
        r = lax.rsqrt(jnp.mean(x * x, axis=-1, keepdims=True) + RMS_EPS)
        dyg = d * g_ref[...]
        dot = jnp.mean(dyg * x, axis=-1, keepdims=True)
        dh = r_ref[...] + (r * dyg - x * ((r * r * r) * dot))
        o_ref[...] = dh
        ob_ref[...] = dh.astype(BF16)
        part = jnp.sum(d * (x * r), axis=0, keepdims=True)

        @pl.when(pl.program_id(0) == 0)
        def _():
            dg_ref[...] = part

        @pl.when(pl.program_id(0) != 0)
        def _():
            dg_ref[...] += part

    blk = pl.BlockSpec((tm, D), lambda i: (i, 0))
    row = pl.BlockSpec((1, D), lambda i: (0, 0))
    return pl.pallas_call(
        body, name=name, grid=(T // tm,),
        in_specs=[blk, blk, row, blk],
        out_specs=[blk, blk, row],
        out_shape=[jax.ShapeDtypeStruct((T, D), F32), jax.ShapeDtypeStruct((T, D), BF16),
                   jax.ShapeDtypeStruct((1, D), F32)],
        compiler_params=_cp(("arbitrary",)),
    )(dhn, h, g, dres)


def _loss_head(h, tgt, g, name):
    T, D = h.shape
    tm = min(TM, T)

    def body(h_ref, t_ref, g_ref, o_ref, ob_ref, dg_ref, l_ref):
        x = h_ref[...]
        gg = g_ref[...]
        r = lax.rsqrt(jnp.mean(x * x, axis=-1, keepdims=True) + RMS_EPS)
        xr = x * r
        e = xr * gg - t_ref[...]
        lpart = 0.5 * jnp.sum(jnp.mean(e * e, axis=-1, keepdims=True), axis=0, keepdims=True)
        dy = e * (1.0 / D)
        dyg = dy * gg
        dot = jnp.mean(dyg * x, axis=-1, keepdims=True)
        dh = r * dyg - x * ((r * r * r) * dot)
        o_ref[...] = dh
        ob_ref[...] = dh.astype(BF16)
        part = jnp.sum(dy * xr, axis=0, keepdims=True)
        lrow = jnp.broadcast_to(lpart, (1, LANES))

        @pl.when(pl.program_id(0) == 0)
        def _():
            dg_ref[...] = part
            l_ref[...] = lrow

        @pl.when(pl.program_id(0) != 0)
        def _():
            dg_ref[...] += part
            l_ref[...] += lrow

    blk = pl.BlockSpec((tm, D), lambda i: (i, 0))
    row = pl.BlockSpec((1, D), lambda i: (0, 0))
    return pl.pallas_call(
        body, name=name, grid=(T // tm,),
        in_specs=[blk, blk, row],
        out_specs=[blk, blk, row, pl.BlockSpec((1, LANES), lambda i: (0, 0))],
        out_shape=[jax.ShapeDtypeStruct((T, D), F32), jax.ShapeDtypeStruct((T, D), BF16),
                   jax.ShapeDtypeStruct((1, D), F32), jax.ShapeDtypeStruct((1, LANES), F32)],
        compiler_params=_cp(("arbitrary",)),
    )(h, tgt, g)


def _mm_nn(a, b, name, out_dtype=F32, res=None):
    M, K = a.shape
    b3 = b if b.ndim == 3 else b[None]
    nb, _, w = b3.shape
    N = nb * w
    tm = min(TM_MM, M, max(256, MM_BLOCK_BYTES // (4 * N)))
    o_spec = pl.BlockSpec((tm, N), lambda i: (i, 0))
    in_specs = [pl.BlockSpec((tm, K), lambda i: (i, 0)), pl.BlockSpec((nb, K, w), lambda i: (0, 0, 0))]
    args = [a, b3]
    if res is not None:
        in_specs.append(o_spec)
        args.append(res)

    def body(*refs):
        a_ref, b_ref = refs[0], refs[1]
        o_ref = refs[-1]
        av = a_ref[...]
        for j in range(nb):
            cols = slice(j * w, (j + 1) * w)
            acc = jnp.dot(av, b_ref[j], preferred_element_type=F32)
            if res is not None:
                acc = refs[2][:, cols] + acc
            o_ref[:, cols] = acc.astype(out_dtype)

    return pl.pallas_call(
        body, name=name, grid=(M // tm,),
        in_specs=in_specs, out_specs=o_spec,
        out_shape=jax.ShapeDtypeStruct((M, N), out_dtype),
        compiler_params=_cp(("parallel",)),
    )(*args)


def _mm_nt(a_list, b, name, out_dtype=F32, add=None, after=None):
    M, kw = a_list[0].shape
    tm = min(TM, M)
    na = len(a_list)
    blocked = b.ndim == 3
    N = b.shape[1] if blocked else b.shape[0]
    b_spec = pl.BlockSpec(b.shape, lambda i: (0,) * b.ndim)
    o_spec = pl.BlockSpec((tm, N), lambda i: (i, 0))
    in_specs = [pl.BlockSpec((tm, kw), lambda i: (i, 0)) for _ in a_list] + [b_spec]
    args = list(a_list) + [b]
    if add is not None:
        in_specs.append(o_spec)
        args.append(add)
    if after is not None:
        in_specs.append(pl.BlockSpec(memory_space=pl.ANY))
        args.append(after)

    def body(*refs):
        a_refs = refs[:na]
        b_ref = refs[na]
        o_ref = refs[-1]
        acc = refs[na + 1][...] if add is not None else None
        for s, a_ref in enumerate(a_refs):
            if blocked:
                w = b_ref.shape[2]
                per = kw // w
                parts = [_dot_nt(a_ref[:, jj * w:(jj + 1) * w], b_ref[s * per + jj]) for jj in range(per)]
            else:
                parts = [_dot_nt(a_ref[...], b_ref[:, s * kw:(s + 1) * kw])]
            for part in parts:
                acc = part if acc is None else acc + part
        o_ref[...] = acc.astype(out_dtype)

    return pl.pallas_call(
        body, name=name, grid=(M // tm,),
        in_specs=in_specs, out_specs=o_spec,
        out_shape=jax.ShapeDtypeStruct((M, N), out_dtype),
        compiler_params=_cp(("parallel",)),
    )(*args)


def _mm_tn(x, y, name, blocked_w=None, out_dtype=F32):
    T, Kx = x.shape
    N = y.shape[1]
    tt = min(TT, T)
    nt = T // tt
    tkx = min(Kx, max(LANES, MM_BLOCK_BYTES // (4 * N)))
    if blocked_w is not None:
        blk_shape, full_shape = (N // blocked_w, tkx, blocked_w), (N // blocked_w, Kx, blocked_w)
        o_spec = pl.BlockSpec(blk_shape, lambda i, t: (0, i, 0))
    else:
        blk_shape, full_shape = (tkx, N), (Kx, N)
        o_spec = pl.BlockSpec(blk_shape, lambda i, t: (i, 0))

    def body(x_ref, y_ref, o_ref, acc_ref):
        part = _dot_tn(x_ref[...], y_ref[...])
        t = pl.program_id(1)
        if blocked_w is None:
            pieces = [(slice(None), part)]
        else:
            pieces = [(j, part[:, j * blocked_w:(j + 1) * blocked_w]) for j in range(N // blocked_w)]

        @pl.when(t == 0)
        def _():
            for idx, pj in pieces:
                acc_ref[idx] = pj

        @pl.when(t != 0)
        def _():
            for idx, pj in pieces:
                acc_ref[idx] += pj

        @pl.when(t == nt - 1)
        def _():
            o_ref[...] = acc_ref[...].astype(out_dtype)

    return pl.pallas_call(
        body, name=name, grid=(Kx // tkx, nt),
        in_specs=[pl.BlockSpec((tt, tkx), lambda i, t: (t, i)),
                  pl.BlockSpec((tt, N), lambda i, t: (t, 0))],
        out_specs=o_spec, out_shape=jax.ShapeDtypeStruct(full_shape, out_dtype),
        scratch_shapes=[pltpu.VMEM(blk_shape, F32)],
        compiler_params=_cp(("parallel", "arbitrary")),
    )(x, y)


def _sgu_pieces(z, lng, lnb, wc, bs_ref):
    E = z.shape[1] // 2
    gd = E // GM_GROUPS
    zu, zv = z[:, :E], z[:, E:]
    u = _gelu(zu)
    v = _gelu(zv)
    mu = jnp.mean(v, axis=-1, keepdims=True)
    xc = v - mu
    rs = lax.rsqrt(jnp.mean(xc * xc, axis=-1, keepdims=True) + LN_EPS)
    xhat = xc * rs
    vln = xhat * lng + lnb
    s = []
    for g in range(GM_GROUPS):
        vg = vln[:, g * gd:(g + 1) * gd].astype(BF16)
        s.append(jnp.dot(wc[g], vg, preferred_element_type=F32) + bs_ref[g])
    return zu, zv, u, xhat, rs, vln, s


def _causal_ws(ws_ref):
    t = lax.broadcasted_iota(jnp.int32, (CHUNK, CHUNK), 0)
    s = lax.broadcasted_iota(jnp.int32, (CHUNK, CHUNK), 1)
    tri = t >= s
    return [jnp.where(tri, ws_ref[g], 0.0).astype(BF16) for g in range(GM_GROUPS)], tri


def _sgu_fwd(z, lng, lnb, ws, bs, name):
    T, E2 = z.shape
    E = E2 // 2
    gd = E // GM_GROUPS
    tm = min(2 * CHUNK, T)

    def body(z_ref, lng_ref, lnb_ref, ws_ref, bs_ref, o_ref):
        wc, _ = _causal_ws(ws_ref)
        for c in range(tm // CHUNK):
            rows = slice(c * CHUNK, (c + 1) * CHUNK)
            _, _, u, _, _, _, s = _sgu_pieces(z_ref[rows, :], lng_ref[...], lnb_ref[...], wc, bs_ref)
            for g in range(GM_GROUPS):
                cols = slice(g * gd, (g + 1) * gd)
                o_ref[rows, cols] = (u[:, cols] * s[g]).astype(BF16)

    full = lambda shape: pl.BlockSpec(shape, lambda i: (0,) * len(shape))
    return pl.pallas_call(
        body, name=name, grid=(T // tm,),
        in_specs=[pl.BlockSpec((tm, E2), lambda i: (i, 0)), full((1, E)), full((1, E)),
                  full((GM_GROUPS, CHUNK, CHUNK)), full((GM_GROUPS, CHUNK, 1))],
        out_specs=pl.BlockSpec((tm, E), lambda i: (i, 0)),
        out_shape=jax.ShapeDtypeStruct((T, E), BF16),
        compiler_params=_cp(("parallel",)),
    )(z, lng, lnb, ws, bs)


def _sgu_bwd(z, dg, lng, lnb, ws, bs, name):
    T, E2 = z.shape
    E = E2 // 2
    gd = E // GM_GROUPS
    tm = min(2 * CHUNK, T)
    nsteps = T // tm

    def body(z_ref, dg_ref, lng_ref, lnb_ref, ws_ref, bs_ref, dz_ref, dlng_ref, dlnb_ref, dws_ref, dbs_ref):
        i = pl.program_id(0)

        @pl.when(i == 0)
        def _():
            dlng_ref[...] = jnp.zeros_like(dlng_ref)
            dlnb_ref[...] = jnp.zeros_like(dlnb_ref)
            dws_ref[...] = jnp.zeros_like(dws_ref)
            dbs_ref[...] = jnp.zeros_like(dbs_ref)

        wc, tri = _causal_ws(ws_ref)
        lng_v = lng_ref[...]
        for c in range(tm // CHUNK):
            rows = slice(c * CHUNK, (c + 1) * CHUNK)
            zu, zv, u, xhat, rs, vln, s = _sgu_pieces(z_ref[rows, :], lng_v, lnb_ref[...], wc, bs_ref)
            dgc = dg_ref[rows, :].astype(F32)
            du, dvln = [], []
            for g in range(GM_GROUPS):
                cols = slice(g * gd, (g + 1) * gd)
                dgg = dgc[:, cols]
                du.append(dgg * s[g])
                ds = dgg * u[:, cols]
                dsb = ds.astype(BF16)
                dws_ref[g] += _dot_nt(dsb, vln[:, cols].astype(BF16))
                dbs_ref[g] += jnp.sum(ds, axis=-1, keepdims=True)
                dvln.append(_dot_tn(wc[g], dsb))
            du = jnp.concatenate(du, axis=1)
            dvln = jnp.concatenate(dvln, axis=1)
            dlng_ref[...] += jnp.sum(dvln * xhat, axis=0, keepdims=True)
            dlnb_ref[...] += jnp.sum(dvln, axis=0, keepdims=True)
            dxh = dvln * lng_v
            m1 = jnp.mean(dxh, axis=-1, keepdims=True)
            m2 = jnp.mean(dxh * xhat, axis=-1, keepdims=True)
            dv = rs * (dxh - m1 - xhat * m2)
            dz_ref[rows, :E] = (du * _gelu_grad(zu)).astype(BF16)
            dz_ref[rows, E:] = (dv * _gelu_grad(zv)).astype(BF16)

        @pl.when(i == nsteps - 1)
        def _():
            for g in range(GM_GROUPS):
                dws_ref[g] = jnp.where(tri, dws_ref[g], 0.0)

    full = lambda shape: pl.BlockSpec(shape, lambda i: (0,) * len(shape))
    return pl.pallas_call(
        body, name=name, grid=(nsteps,),
        in_specs=[pl.BlockSpec((tm, E2), lambda i: (i, 0)), pl.BlockSpec((tm, E), lambda i: (i, 0)),
                  full((1, E)), full((1, E)), full((GM_GROUPS, CHUNK, CHUNK)), full((GM_GROUPS, CHUNK, 1))],
        out_specs=[pl.BlockSpec((tm, E2), lambda i: (i, 0)), full((1, E)), full((1, E)),
                   full((GM_GROUPS, CHUNK, CHUNK)), full((GM_GROUPS, CHUNK, 1))],
        out_shape=[jax.ShapeDtypeStruct((T, E2), BF16), jax.ShapeDtypeStruct((1, E), F32),
                   jax.ShapeDtypeStruct((1, E), F32), jax.ShapeDtypeStruct((GM_GROUPS, CHUNK, CHUNK), F32),
                   jax.ShapeDtypeStruct((GM_GROUPS, CHUNK, 1), F32)],
        compiler_params=_cp(("arbitrary",)),
    )(z, dg, lng, lnb, ws, bs)


HALO = 16


def _conv_taps(a_ext, w_ref, b_ref):
    n = a_ext.shape[0]
    am1 = pltpu.roll(a_ext, 1, 0)
    am2 = pltpu.roll(a_ext, 2, 0)
    del n
    return ((b_ref[...] + am2 * w_ref[0:1, :]) + am1 * w_ref[1:2, :]) + a_ext * w_ref[2:3, :], am1, am2


def _ffn_mid_fwd(au, cw, cb, name):
    T, F = au.shape[0], au.shape[1] // 2
    tm, tf = min(TM, T), min(TF, F)
    hb = tm // HALO
    nf = F // tf

    def body(a_ref, ap_ref, u_ref, w_ref, b_ref, o_ref):
        i = pl.program_id(1)
        prev = jnp.where(i == 0, 0.0, ap_ref[...])
        ext = jnp.concatenate([prev, a_ref[...]], axis=0)
        conv, _, _ = _conv_taps(ext, w_ref, b_ref)
        conv = conv[HALO:, :]
        o_ref[...] = ((conv * _sigmoid(conv)) * u_ref[...]).astype(BF16)

    main = pl.BlockSpec((tm, tf), lambda f, i: (i, f))
    return pl.pallas_call(
        body, name=name, grid=(nf, T // tm),
        in_specs=[main, pl.BlockSpec((HALO, tf), lambda f, i: (jnp.maximum(i * hb - 1, 0), f)),
                  pl.BlockSpec((tm, tf), lambda f, i: (i, nf + f)),
                  pl.BlockSpec((3, tf), lambda f, i: (0, f)), pl.BlockSpec((1, tf), lambda f, i: (0, f))],
        out_specs=main, out_shape=jax.ShapeDtypeStruct((T, F), BF16),
        compiler_params=_cp(("parallel", "parallel")),
    )(au, au, au, cw, cb)


def _ffn_mid_bwd(au, dact, cw, cb, name):
    T, F = au.shape[0], au.shape[1] // 2
    tm, tf = min(TM, T), min(TF, F)
    hb = tm // HALO
    nt = T // tm
    nf = F // tf
    last_h = T // HALO - 1

    def body(a_ref, ap_ref, an_ref, u_ref, un_ref, d_ref, dn_ref, w_ref, b_ref, da_ref, du_ref, dcw_ref, dcb_ref):
        i = pl.program_id(1)
        prev = jnp.where(i == 0, 0.0, ap_ref[...])
        a_main = a_ref[...]
        a_ext = jnp.concatenate([prev, a_main, an_ref[...]], axis=0)
        conv, am1, am2 = _conv_taps(a_ext, w_ref, b_ref)
        conv = conv[HALO:, :]
        sig = _sigmoid(conv)
        u_ext = jnp.concatenate([u_ref[...], un_ref[...]], axis=0)
        d_ext = jnp.concatenate([d_ref[...], dn_ref[...]], axis=0).astype(F32)
        n = tm + HALO
        row = lax.broadcasted_iota(jnp.int32, (n, 1), 0)
        live = jnp.logical_or(row < tm, i < nt - 1)
        dconv = jnp.where(live, d_ext * u_ext * (sig * (1.0 + conv * (1.0 - sig))), 0.0)
        du_ref[...] = (d_ext[:tm, :] * (conv[:tm, :] * sig[:tm, :])).astype(BF16)
        dp1 = pltpu.roll(dconv, n - 1, 0)[:tm, :]
        dp2 = pltpu.roll(dconv, n - 2, 0)[:tm, :]
        dc = dconv[:tm, :]
        da_ref[...] = ((dc * w_ref[2:3, :] + dp1 * w_ref[1:2, :]) + dp2 * w_ref[0:1, :]).astype(BF16)
        g2 = jnp.sum(dc * a_main, axis=0, keepdims=True)
        g1 = jnp.sum(dc * am1[HALO:HALO + tm, :], axis=0, keepdims=True)
        g0 = jnp.sum(dc * am2[HALO:HALO + tm, :], axis=0, keepdims=True)
        gb = jnp.sum(dc, axis=0, keepdims=True)

        @pl.when(i == 0)
        def _():
            dcw_ref[...] = jnp.zeros_like(dcw_ref)
            dcb_ref[...] = jnp.zeros_like(dcb_ref)

        dcw_ref[0:1, :] += g0
        dcw_ref[1:2, :] += g1
        dcw_ref[2:3, :] += g2
        dcb_ref[...] += gb

    main = pl.BlockSpec((tm, tf), lambda f, i: (i, f))
    prev = pl.BlockSpec((HALO, tf), lambda f, i: (jnp.maximum(i * hb - 1, 0), f))
    nxt = pl.BlockSpec((HALO, tf), lambda f, i: (jnp.minimum((i + 1) * hb, last_h), f))
    main_u = pl.BlockSpec((tm, tf), lambda f, i: (i, nf + f))
    nxt_u = pl.BlockSpec((HALO, tf), lambda f, i: (jnp.minimum((i + 1) * hb, last_h), nf + f))
    return pl.pallas_call(
        body, name=name, grid=(nf, nt),
        in_specs=[main, prev, nxt, main_u, nxt_u, main, nxt,
                  pl.BlockSpec((3, tf), lambda f, i: (0, f)), pl.BlockSpec((1, tf), lambda f, i: (0, f))],
        out_specs=[main, main, pl.BlockSpec((3, tf), lambda f, i: (0, f)), pl.BlockSpec((1, tf), lambda f, i: (0, f))],
        out_shape=[jax.ShapeDtypeStruct((T, F), BF16), jax.ShapeDtypeStruct((T, F), BF16),
                   jax.ShapeDtypeStruct((3, F), F32), jax.ShapeDtypeStruct((1, F), F32)],
        compiler_params=_cp(("parallel", "arbitrary")),
    )(au, au, au, au, au, dact, dact, cw, cb)


def _split3(x):
    hi = x.astype(BF16)
    r1 = x - hi.astype(F32)
    mid = r1.astype(BF16)
    lo = (r1 - mid.astype(F32)).astype(BF16)
    return hi, mid, lo


def _tri_ones(n, upper):
    r = lax.broadcasted_iota(jnp.int32, (n, n), 0)
    c = lax.broadcasted_iota(jnp.int32, (n, n), 1)
    return jnp.where((r <= c) if upper else (r >= c), 1.0, 0.0).astype(BF16)


def _gate_scan(f, bf, name):
    T = f.shape[0]
    tm = min(256, T)

    def body(f_ref, b_ref, cp_ref, sn_ref, carry_ref):
        i = pl.program_id(0)

        @pl.when(i == 0)
        def _():
            carry_ref[...] = jnp.zeros_like(carry_ref)

        x = f_ref[...] + b_ref[...]
        e = jnp.exp(-jnp.abs(x))
        logf = jnp.minimum(x, 0.0) - jnp.log(1.0 + e)
        sn_ref[...] = jnp.where(x >= 0.0, e / (1.0 + e), 1.0 / (1.0 + e))
        tri = _tri_ones(tm, upper=False)
        c = carry_ref[...]
        for piece in _split3(logf):
            c = c + jnp.dot(tri, piece, preferred_element_type=F32)
        carry_ref[...] += jnp.sum(logf, axis=0, keepdims=True)
        hi, mid, lo = _split3(c * LOG2E)
        cp_ref[:, 0:LANES] = hi
        cp_ref[:, LANES:2 * LANES] = mid
        cp_ref[:, 2 * LANES:3 * LANES] = lo

    return pl.pallas_call(
        body, name=name, grid=(T // tm,),
        in_specs=[pl.BlockSpec((tm, LANES), lambda i: (i, 0)), pl.BlockSpec((1, LANES), lambda i: (0, 0))],
        out_specs=[pl.BlockSpec((tm, 3 * LANES), lambda i: (i, 0)), pl.BlockSpec((tm, LANES), lambda i: (i, 0))],
        out_shape=[jax.ShapeDtypeStruct((T, 3 * LANES), BF16), jax.ShapeDtypeStruct((T, LANES), F32)],
        scratch_shapes=[pltpu.VMEM((1, LANES), F32)],
        compiler_params=_cp(("arbitrary",)),
    )(f, bf)


def _gate_scan_bwd(dcq, dck, sneg, name):
    T = dcq.shape[0]
    tm = min(256, T)
    n = T // tm

    def body(dcq_ref, dck_ref, sn_ref, df_ref, db_ref, carry_ref):
        i = pl.program_id(0)

        @pl.when(i == 0)
        def _():
            carry_ref[...] = jnp.zeros_like(carry_ref)
            db_ref[...] = jnp.zeros_like(db_ref)

        tri = _tri_ones(tm, upper=True)
        dcb = dcq_ref[...] - dck_ref[...]
        acc = carry_ref[...]
        for piece in _split3(dcb):
            acc = acc + jnp.dot(tri, piece, preferred_element_type=F32)
        carry_ref[...] += jnp.sum(dcb, axis=0, keepdims=True)
        df = acc * sn_ref[...]
        df_ref[...] = df.astype(BF16)
        db_ref[...] += jnp.sum(df, axis=0, keepdims=True)

    rev = pl.BlockSpec((tm, LANES), lambda i: (n - 1 - i, 0))
    return pl.pallas_call(
        body, name=name, grid=(n,),
        in_specs=[rev, rev, rev],
        out_specs=[rev, pl.BlockSpec((1, LANES), lambda i: (0, 0))],
        out_shape=[jax.ShapeDtypeStruct((T, LANES), BF16), jax.ShapeDtypeStruct((1, LANES), F32)],
        scratch_shapes=[pltpu.VMEM((1, LANES), F32)],
        compiler_params=_cp(("arbitrary",)),
    )(dcq, dck, sneg)


def _qk_proj(hn, w_pad, cp, sel, const, scale, name):
    T, D = hn.shape
    H = w_pad.shape[1] // LANES
    tm = min(TM_MM, T)

    def body(a_ref, w_ref, cp_ref, sel_ref, c_ref, o_ref):
        acc = jnp.dot(a_ref[...], w_ref[...], preferred_element_type=F32)
        if scale != 1.0:
            acc = acc * scale
        acc = acc + jnp.dot(cp_ref[...], sel_ref[...], preferred_element_type=F32) + c_ref[...]
        o_ref[0] = acc[:, :LANES].astype(BF16)
        o_ref[1] = acc[:, LANES:].astype(BF16)

    return pl.pallas_call(
        body, name=name, grid=(T // tm, H // 2),
        in_specs=[pl.BlockSpec((tm, D), lambda i, p: (i, 0)), pl.BlockSpec((D, 2 * LANES), lambda i, p: (0, p)),
                  pl.BlockSpec((tm, 3 * LANES), lambda i, p: (i, 0)),
                  pl.BlockSpec((None, 3 * LANES, 2 * LANES), lambda i, p: (p, 0, 0)),
                  pl.BlockSpec((None, 1, 2 * LANES), lambda i, p: (p, 0, 0))],
        out_specs=pl.BlockSpec((2, tm, LANES), lambda i, p: (p, i, 0)),
        out_shape=jax.ShapeDtypeStruct((H, T, LANES), BF16),
        compiler_params=_cp(("parallel", "arbitrary")),
    )(hn, w_pad, cp, sel, const)


def _lane_lo():
    return lax.broadcasted_iota(jnp.int32, (1, LANES), 1) < HEAD_DIM


def _attn_fwd(qp, kp, v, name):
    H, T, _ = qp.shape
    tq = min(TQ, T)
    nrep = tq // LANES
    n_parts = 4 if tq % 512 == 0 else 1
    rows = tq // n_parts

    def body(q_ref, k_ref, v_ref, o_ref, o32_ref, lse_ref, m_sc, acc_sc):
        i = pl.program_id(1)
        m_sc[...] = jnp.full(m_sc.shape, NEG, F32)
        acc_sc[...] = jnp.zeros_like(acc_sc)
        ones_col = jnp.where(lax.broadcasted_iota(jnp.int32, (tq, LANES), 1) == 0, 1.0, 0.0).astype(BF16)

        def step(j, masked):
            off = pl.multiple_of(j * tq, tq)
            vaug = jnp.concatenate([v_ref[pl.ds(off, tq), :], ones_col], axis=1)
            chains = [(h, rp) for h in range(2) for rp in range(n_parts)]
            s_all = [_dot_nt(q_ref[h, rp * rows:(rp + 1) * rows, :], k_ref[h, pl.ds(off, tq), :]) for h, rp in chains]
            for (h, rp), s in zip(chains, s_all):
                rsl = slice(rp * rows, (rp + 1) * rows)
                tiles = [s[:, c * LANES:(c + 1) * LANES] for c in range(nrep)]
                if masked:
                    r = lax.broadcasted_iota(jnp.int32, (rows, LANES), 0) + rp * rows
                    cc = lax.broadcasted_iota(jnp.int32, (rows, LANES), 1)
                    tiles = [jnp.where(r >= cc + c * LANES, t, NEG) for c, t in enumerate(tiles)]
                mt = tiles[0]
                for t in tiles[1:]:
                    mt = jnp.maximum(mt, t)
                m_prev = m_sc[h, rsl, :]
                m_new = jnp.maximum(m_prev, jnp.max(mt, axis=-1, keepdims=True))
                alpha = jnp.exp2(m_prev - m_new)
                p16 = jnp.concatenate([jnp.exp2(t - m_new).astype(BF16) for t in tiles], axis=1)
                pv = jnp.dot(p16, vaug, preferred_element_type=F32)
                acc_sc[h, rsl, :] = jnp.concatenate([alpha, alpha], axis=1) * acc_sc[h, rsl, :] + pv
                m_sc[h, rsl, :] = m_new

        def loop_body(j, carry):
            step(j, False)
            return carry

        lax.fori_loop(0, i, loop_body, 0)
        step(i, True)
        lo = _lane_lo()
        acc0, acc1 = acc_sc[0], acc_sc[1]
        l0 = jnp.sum(acc0[:, LANES:], axis=-1, keepdims=True)
        l1 = jnp.sum(acc1[:, LANES:], axis=-1, keepdims=True)
        o = jnp.where(lo, acc0[:, :LANES] / l0, acc1[:, :LANES] / l1)
        o_ref[...] = o.astype(BF16)
        o32_ref[...] = o
        lse_ref[...] = jnp.where(lo, m_sc[0] + jnp.log(l0) * LOG2E, m_sc[1] + jnp.log(l1) * LOG2E)

    oblk = pl.BlockSpec((tq, LANES), lambda p, i: (i, p))
    return pl.pallas_call(
        body, name=name, grid=(H // 2, T // tq),
        in_specs=[pl.BlockSpec((2, tq, LANES), lambda p, i: (p, i, 0)),
                  pl.BlockSpec((2, T, LANES), lambda p, i: (p, 0, 0)),
                  pl.BlockSpec((T, LANES), lambda p, i: (0, p))],
        out_specs=[oblk, oblk, pl.BlockSpec((None, tq, LANES), lambda p, i: (p, i, 0))],
        out_shape=[jax.ShapeDtypeStruct((T, H * HEAD_DIM), BF16), jax.ShapeDtypeStruct((T, H * HEAD_DIM), F32),
                   jax.ShapeDtypeStruct((H // 2, T, LANES), F32)],
        scratch_shapes=[pltpu.VMEM((2, tq, LANES), F32), pltpu.VMEM((2, tq, 2 * LANES), F32)],
        compiler_params=_cp(("parallel", "arbitrary")),
    )(qp, kp, v)


def _attn_bwd(qp, kp, v, o, do, lse, scale, name):
    H, T, _ = qp.shape
    tq = min(TQ, T)
    nq = T // tq
    nrep = tq // LANES

    def body(q_ref, k_ref, v_ref, o_ref, do_ref, lse_ref, dq_ref, dk_ref, dv_ref, dqe_ref, dke_ref, dk_sc, dv_sc, dq_sc):
        i = pl.program_id(1)

        @pl.when(i == 0)
        def _():
            dk_sc[...] = jnp.zeros_like(dk_sc)
            dv_sc[...] = jnp.zeros_like(dv_sc)

        dq_sc[...] = jnp.zeros_like(dq_sc)

        lo = _lane_lo()
        dob = do_ref[...]
        dof = dob.astype(F32)
        prod = dof * o_ref[...].astype(F32)
        lse2 = lse_ref[...]
        lse2_sw = pltpu.roll(lse2, HEAD_DIM, 1)
        zero = jnp.zeros_like(dob)
        do_h = [jnp.where(lo, dob, zero), jnp.where(lo, zero, dob)]
        rep = lambda col: jnp.broadcast_to(col, (tq, LANES))
        delta = [rep(jnp.sum(jnp.where(lo, prod, 0.0), axis=-1, keepdims=True)),
                 rep(jnp.sum(jnp.where(lo, 0.0, prod), axis=-1, keepdims=True))]
        lse_h = [jnp.where(lo, lse2, lse2_sw), jnp.where(lo, lse2_sw, lse2)]
        qs = [q_ref[0], q_ref[1]]
        tr16 = lambda a: jnp.transpose(a.astype(F32)).astype(BF16)
        q_t = [tr16(qs[0]), tr16(qs[1])]
        do_t = [tr16(do_h[0]), tr16(do_h[1])]

        def step(j, masked):
            off = pl.multiple_of(j * tq, tq)
            vblk = v_ref[pl.ds(off, tq), :]
            dv_add = None
            kblks = [k_ref[h, pl.ds(off, tq), :] for h in range(2)]
            s_all = [_dot_nt(qs[h], kblks[h]) for h in range(2)]
            dp_all = [_dot_nt(do_h[h], vblk) for h in range(2)]
            for h in range(2):
                kblk, s, dp = kblks[h], s_all[h], dp_all[h]
                p16, ds16 = [], []
                for c in range(nrep):
                    cols = slice(c * LANES, (c + 1) * LANES)
                    p = jnp.exp2(s[:, cols] - lse_h[h])
                    if masked:
                        r = lax.broadcasted_iota(jnp.int32, (tq, LANES), 0)
                        cc = lax.broadcasted_iota(jnp.int32, (tq, LANES), 1)
                        p = jnp.where(r >= cc + c * LANES, p, 0.0)
                    p16.append(p.astype(BF16))
                    ds16.append((p * (dp[:, cols] - delta[h])).astype(BF16))
                p16 = jnp.concatenate(p16, axis=1)
                dsb = jnp.concatenate(ds16, axis=1)
                dq_sc[h] += jnp.dot(dsb, kblk, preferred_element_type=F32)
                dk_sc[h, :, pl.ds(off, tq)] += jnp.dot(q_t[h], dsb, preferred_element_type=F32)
                pv = jnp.dot(do_t[h], p16, preferred_element_type=F32)
                dv_add = pv if dv_add is None else dv_add + pv
            dv_sc[:, pl.ds(off, tq)] += dv_add

        def loop_body(j, carry):
            step(j, False)
            return carry

        lax.fori_loop(0, i, loop_body, 0)
        step(i, True)
        dq0, dq1 = dq_sc[0], dq_sc[1]
        dq_ref[...] = (jnp.where(lo, dq0, pltpu.roll(dq1, HEAD_DIM, 1)) * scale).astype(BF16)
        dqe_ref[0:8, :] = jnp.transpose(dq0)[HEAD_DIM:HEAD_DIM + 8, :]
        dqe_ref[8:16, :] = jnp.transpose(dq1)[HEAD_DIM:HEAD_DIM + 8, :]

        @pl.when(i == nq - 1)
        def _():
            for h in range(2):
                dke_ref[8 * h:8 * h + 8, :] = dk_sc[h, HEAD_DIM:HEAD_DIM + 8, :]
            for cb in range(nq):
                tok = slice(cb * tq, (cb + 1) * tq)
                dk0 = jnp.transpose(dk_sc[0, :, tok])
                dk1 = jnp.transpose(dk_sc[1, :, tok])
                dk_ref[tok, :] = (jnp.where(lo, dk0, pltpu.roll(dk1, HEAD_DIM, 1)) * LN2).astype(BF16)
                dv_ref[tok, :] = jnp.transpose(dv_sc[:, tok]).astype(BF16)

    qblk = pl.BlockSpec((tq, LANES), lambda p, i: (i, p))
    pair = pl.BlockSpec((T, LANES), lambda p, i: (0, p))
    tok16 = jax.ShapeDtypeStruct((T, H * HEAD_DIM), BF16)
    gate32 = jax.ShapeDtypeStruct((H // 2, 16, T), F32)
    return pl.pallas_call(
        body, name=name, grid=(H // 2, nq),
        in_specs=[pl.BlockSpec((2, tq, LANES), lambda p, i: (p, i, 0)),
                  pl.BlockSpec((2, T, LANES), lambda p, i: (p, 0, 0)),
                  pair, qblk, qblk,
                  pl.BlockSpec((None, tq, LANES), lambda p, i: (p, i, 0))],
        out_specs=[qblk, pair, pair, pl.BlockSpec((None, 16, tq), lambda p, i: (p, 0, i)),
                   pl.BlockSpec((None, 16, T), lambda p, i: (p, 0, 0))],
        out_shape=[tok16, tok16, tok16, gate32, gate32],
        scratch_shapes=[pltpu.VMEM((2, LANES, T), F32), pltpu.VMEM((LANES, T), F32),
                        pltpu.VMEM((2, tq, LANES), F32)],
        compiler_params=_cp(("parallel", "arbitrary")),
    )(qp, kp, v, o, do, lse)


def _mesh_pos():
    return lax.axis_index("x"), lax.axis_index("y"), lax.axis_index("c")


def _all_gather(arrs, name, groups=None):
    n = len(arrs)
    if groups is None:
        groups = [(a, 0) for a in range(n)]
    ng = 1 + max(g for g, _ in groups)
    per_group = [sum(1 for g, _ in groups if g == gi) for gi in range(ng)]
    first_of = [next(a for a in range(n) if groups[a][0] == gi) for gi in range(ng)]

    def body(*refs):
        ins, outs = refs[:n], refs[n:n + ng]
        send_sems, recv_sems, local_sems = refs[n + ng:]
        x, y, c = _mesh_pos()
        me, sib = (x, y, c), (x, y, 1 - c)
        chips = [(1 - x, y), (x, 1 - y), (1 - x, 1 - y)]

        def dst_of(a, px, py, pc):
            g, k = groups[a]
            return outs[g].at[N_DEV * k + 4 * px + 2 * py + pc]

        def copy(a, k, block, to, src=None):
            dst = dst_of(a, *block)
            return pltpu.make_async_remote_copy(
                src_ref=dst if src is None else src, dst_ref=dst,
                send_sem=send_sems.at[a, k], recv_sem=recv_sems.at[a, k], device_id=to, device_id_type=MESH)

        mine = [pltpu.make_async_copy(ins[a], dst_of(a, *me), local_sems.at[a]) for a in range(n)]
        for cp in mine:
            cp.start()
        first = []
        for a in range(n):
            first.append(copy(a, 0, me, sib, src=ins[a]))
            first += [copy(a, 1 + j, me, (*chip, c), src=ins[a]) for j, chip in enumerate(chips)]
        for cp in first:
            cp.start()
        passed = []
        for j, chip in enumerate(chips):
            for a in range(n):
                copy(a, 1 + j, (*chip, c), me).wait_recv()
                fwd = copy(a, 4 + j, (*chip, c), sib)
                fwd.start()
                passed.append(fwd)
        for a in range(n):
            copy(a, 0, sib, me).wait_recv()
            for j, chip in enumerate(chips):
                copy(a, 4 + j, (*chip, 1 - c), me).wait_recv()
        for cp in first + passed:
            cp.wait_send()
        for cp in mine:
            cp.wait()

    any_spec = pl.BlockSpec(memory_space=pl.ANY)
    return pl.pallas_call(
        body, name=name,
        in_specs=[any_spec] * n, out_specs=[any_spec] * ng,
        out_shape=[jax.ShapeDtypeStruct((N_DEV * per_group[gi],) + arrs[first_of[gi]].shape, arrs[first_of[gi]].dtype)
                   for gi in range(ng)],
        scratch_shapes=[pltpu.SemaphoreType.DMA((n, 7)), pltpu.SemaphoreType.DMA((n, 7)),
                        pltpu.SemaphoreType.DMA((n,))],
    )(*arrs)


def _pair_exchange(gs, name):
    n = len(gs)

    def body(*refs):
        g_refs, o_refs = refs[:n], refs[n:2 * n]
        send_sems, recv_sems = refs[2 * n:]
        x, y, c = _mesh_pos()
        sib = (x, y, 1 - c)
        copies = []
        for a in range(n):
            for j in range(4):
                copies.append(pltpu.make_async_remote_copy(
                    src_ref=g_refs[a].at[2 * j + (1 - c)], dst_ref=o_refs[a].at[j],
                    send_sem=send_sems.at[a, j], recv_sem=recv_sems.at[a, j], device_id=sib, device_id_type=MESH))
        for cp in copies:
            cp.start()
        for cp in copies:
            cp.wait_recv()
        for cp in copies:
            cp.wait_send()

    any_spec = pl.BlockSpec(memory_space=pl.ANY)
    return pl.pallas_call(
        body, name=name, in_specs=[any_spec] * n, out_specs=[any_spec] * n,
        out_shape=[jax.ShapeDtypeStruct((4,) + g.shape[1:], g.dtype) for g in gs],
        scratch_shapes=[pltpu.SemaphoreType.DMA((n, 4)), pltpu.SemaphoreType.DMA((n, 4))],
    )(*gs)


def _chip_exchange(parts, name):
    n = len(parts)

    def body(*refs):
        p_refs, o_refs = refs[:n], refs[n:2 * n]
        send_sems, recv_sems = refs[2 * n:]
        x, y, c = _mesh_pos()
        chips = [(1 - x, y), (x, 1 - y), (1 - x, 1 - y)]
        copies = []
        for a in range(n):
            for k, (px, py) in enumerate(chips):
                copies.append(pltpu.make_async_remote_copy(
                    src_ref=p_refs[a].at[2 * px + py], dst_ref=o_refs[a].at[k],
                    send_sem=send_sems.at[a, k], recv_sem=recv_sems.at[a, k], device_id=(px, py, c),
                    device_id_type=MESH))
        for cp in copies:
            cp.start()
        for cp in copies:
            cp.wait_recv()
        for cp in copies:
            cp.wait_send()

    any_spec = pl.BlockSpec(memory_space=pl.ANY)
    return pl.pallas_call(
        body, name=name, in_specs=[any_spec] * n, out_specs=[any_spec] * n,
        out_shape=[jax.ShapeDtypeStruct((3,) + p.shape[1:], p.dtype) for p in parts],
        scratch_shapes=[pltpu.SemaphoreType.DMA((n, 3)), pltpu.SemaphoreType.DMA((n, 3))],
    )(*parts)


HBM_SPEC = pl.BlockSpec(memory_space=pltpu.HBM)
SEM_SPEC = pl.BlockSpec(memory_space=pltpu.SEMAPHORE)
ANY_SPEC = pl.BlockSpec(memory_space=pl.ANY)
DATAFLOW_EFFECT = pltpu.SideEffectType.DATAFLOW_SIDE_EFFECTING


def _peers():
    x, y, c = _mesh_pos()
    flip = lambda v, b: 1 - v if b else v
    return [(flip(x, (k >> 2) & 1), flip(y, (k >> 1) & 1), flip(c, k & 1)) for k in range(1, N_DEV)]


def _slot(p):
    return 4 * p[0] + 2 * p[1] + p[2]


def _direct_copy(src_refs, land_refs, sems, a, k, p, land_of, dst_slot, src_slot):
    s = src_slot(a, p)
    return pltpu.make_async_remote_copy(
        src_ref=src_refs[a] if s is None else src_refs[a].at[s], dst_ref=land_refs[land_of[a]].at[dst_slot(a, k)],
        send_sem=sems[0].at[a * (N_DEV - 1) + k], recv_sem=sems[1].at[a * (N_DEV - 1) + k], device_id=p,
        device_id_type=MESH)


def _direct_start(srcs, lands, land_of, dst_slot, src_slot, after, name, collective_id):
    n, nl = len(srcs), len(lands)

    def body(*refs):
        src_refs, land_refs = refs[:n], refs[n:n + nl]
        sems = (refs[n + nl + 1], refs[n + nl + 2])
        token = refs[-1]
        peers = _peers()
        barrier = pltpu.get_barrier_semaphore()
        for p in peers:
            pl.semaphore_signal(barrier, inc=1, device_id=p, device_id_type=MESH)
        pl.semaphore_wait(barrier, N_DEV - 1)
        for a in range(n):
            for k, p in enumerate(peers):
                _direct_copy(src_refs, land_refs, sems, a, k, p, land_of, dst_slot, src_slot).start()
        token[...] = jnp.zeros_like(token)

    hbm = lambda t: pltpu.HBM(t.shape, t.dtype)
    sem_t = pltpu.SemaphoreType.DMA((n * (N_DEV - 1),))
    outs = pl.pallas_call(
        body, name=name,
        out_shape=(sem_t, sem_t, *[hbm(t) for t in srcs], *[hbm(t) for t in lands], jax.ShapeDtypeStruct((8, LANES), F32)),
        in_specs=[HBM_SPEC] * (n + nl) + [ANY_SPEC],
        out_specs=(SEM_SPEC, SEM_SPEC, *([HBM_SPEC] * (n + nl)), pl.BlockSpec(memory_space=pltpu.VMEM)),
        input_output_aliases={i: 2 + i for i in range(n + nl)},
        compiler_params=pltpu.CompilerParams(has_side_effects=DATAFLOW_EFFECT, collective_id=collective_id),
    )(*[pltpu.with_memory_space_constraint(t, pltpu.HBM) for t in srcs],
      *[pltpu.with_memory_space_constraint(t, pltpu.HBM) for t in lands], after)
    return outs[0], outs[1], list(outs[2:2 + n]), list(outs[2 + n:2 + n + nl]), outs[-1]


def _direct_wait(send_sems, recv_sems, srcs, lands, land_of, idxs, dst_slot, src_slot, after, name):
    land_ids = []
    for a in idxs:
        if land_of[a] not in land_ids:
            land_ids.append(land_of[a])
    m, ml = len(idxs), len(land_ids)
    sub_land_of = {j: land_ids.index(land_of[a]) for j, a in enumerate(idxs)}

    def body(*refs):
        src_refs, land_refs = refs[:m], refs[m:m + ml]
        ssem, rsem = refs[m + ml], refs[m + ml + 1]
        for j, a in enumerate(idxs):
            for k, p in enumerate(_peers()):
                s = src_slot(a, p)
                cp = pltpu.make_async_remote_copy(
                    src_ref=src_refs[j] if s is None else src_refs[j].at[s],
                    dst_ref=land_refs[sub_land_of[j]].at[dst_slot(a, k)],
                    send_sem=ssem.at[a * (N_DEV - 1) + k], recv_sem=rsem.at[a * (N_DEV - 1) + k], device_id=p,
                    device_id_type=MESH)
                cp.wait_send()
                cp.wait_recv()

    hbm = lambda t: pltpu.HBM(t.shape, t.dtype)
    sub_s, sub_l = [srcs[a] for a in idxs], [lands[g] for g in land_ids]
    outs = pl.pallas_call(
        body, name=name,
        out_shape=(*[hbm(t) for t in sub_s], *[hbm(t) for t in sub_l]),
        in_specs=[HBM_SPEC] * (m + ml) + [SEM_SPEC, SEM_SPEC, ANY_SPEC],
        out_specs=tuple([HBM_SPEC] * (m + ml)),
        input_output_aliases={i: i for i in range(m + ml)},
        compiler_params=pltpu.CompilerParams(has_side_effects=DATAFLOW_EFFECT),
    )(*sub_s, *sub_l, send_sems, recv_sems, after)
    return list(outs[:m]), list(outs[m:])


def _row_block(R, C):
    best = None
    for d in range(16, R + 1, 16):
        if R % d == 0 and d * C <= 256 * 1024:
            best = d
    return best if best is not None else R


def _pair_add(g, recv, cidx, name):
    _, R, C = g.shape
    tr = _row_block(R, C)

    def body(c_ref, g_ref, r_ref, o_ref):
        del c_ref
        o_ref[...] = (g_ref[...].astype(F32) + r_ref[...].astype(F32)).astype(BF16)

    grid_spec = pltpu.PrefetchScalarGridSpec(
        num_scalar_prefetch=1, grid=(4, R // tr),
        in_specs=[pl.BlockSpec((None, tr, C), lambda j, i, c: (2 * j + c[0], i, 0)),
                  pl.BlockSpec((None, tr, C), lambda j, i, c: (j, i, 0))],
        out_specs=pl.BlockSpec((None, tr, C), lambda j, i, c: (j, i, 0)))
    return pl.pallas_call(
        body, name=name, grid_spec=grid_spec,
        out_shape=jax.ShapeDtypeStruct((4, R, C), BF16),
        compiler_params=_cp(("parallel", "parallel")),
    )(cidx, g, recv)


def _adamw_math(w, g, m, v):
    m = ADAM_B1 * m + (1.0 - ADAM_B1) * g
    v = ADAM_B2 * v + (1.0 - ADAM_B2) * (g * g)
    m_hat = m / (1.0 - ADAM_B1 ** ADAM_STEP)
    v_hat = v / (1.0 - ADAM_B2 ** ADAM_STEP)
    delta = -ADAM_LR * (m_hat / (jnp.sqrt(v_hat) + ADAM_EPS) + ADAM_WD * w)
    return delta, m, v


def _sum_adamw(parts, w, m, v, name, sel=None):
    R, C = w.shape
    tr = _row_block(R, C)
    specs, args = [], []
    for arr, idxs in parts:
        for idx in idxs:
            if idx < 0:
                specs.append(pl.BlockSpec((None, tr, C), lambda i, s: (s[0], i, 0)))
            else:
                specs.append(pl.BlockSpec((None, tr, C), lambda i, s, idx=idx: (idx, i, 0)))
            args.append(arr)
    npart = len(args)
    blk = pl.BlockSpec((tr, C), lambda i, s: (i, 0))

    def body(s_ref, *refs):
        del s_ref
        g = refs[0][...].astype(F32)
        for r in refs[1:npart]:
            g = g + r[...].astype(F32)
        w_ref, m_ref, v_ref, g_out, d_out, m_out, v_out = refs[npart:]
        delta, mm, vv = _adamw_math(w_ref[...], g, m_ref[...], v_ref[...])
        g_out[...] = g
        d_out[...] = delta
        m_out[...] = mm
        v_out[...] = vv

    grid_spec = pltpu.PrefetchScalarGridSpec(
        num_scalar_prefetch=1, grid=(R // tr,),
        in_specs=specs + [blk, blk, blk], out_specs=[blk] * 4)
    if sel is None:
        sel = jnp.zeros((1,), jnp.int32)
    return pl.pallas_call(
        body, name=name, grid_spec=grid_spec,
        out_shape=[jax.ShapeDtypeStruct((R, C), F32)] * 4,
        compiler_params=_cp(("parallel",)),
    )(sel, *args, w, m, v)


def _rows(a, c):
    return a.reshape(-1, c)


def _pad_rows(a, r):
    return jnp.pad(a, ((0, r - a.shape[0]), (0, 0))) if a.shape[0] != r else a


def _gate_tables():
    hp = N_HEADS // 2
    sel_q = np.zeros((hp, 3 * LANES, 2 * LANES), np.float32)
    sel_k = np.zeros((hp, 3 * LANES, 2 * LANES), np.float32)
    const_q = np.zeros((hp, 1, 2 * LANES), np.float32)
    const_k = np.zeros((hp, 1, 2 * LANES), np.float32)
    for p in range(hp):
        for hh in range(2):
            h = 2 * p + hh
            base = hh * LANES + HEAD_DIM
            for piece in range(3):
                sel_q[p, piece * LANES + h, base + piece] = 1.0
                sel_k[p, piece * LANES + h, base + 3 + piece] = -1.0
            const_k[p, 0, base:base + 3] = 1.0
            const_q[p, 0, base + 3:base + 6] = 1.0
    as_bf = lambda t: jnp.asarray(t, BF16)
    return as_bf(sel_q), as_bf(sel_k), jnp.asarray(const_q), jnp.asarray(const_k)


def _pad_heads(w):
    d = w.shape[0]
    w3 = w.reshape(d, N_HEADS, HEAD_DIM)
    return jnp.pad(w3, ((0, 0), (0, 0), (0, LANES - HEAD_DIM))).reshape(d, N_HEADS * LANES)


def kernel(x, mix_norm_g, ffn_norm_g, gm_w_in, gm_ln_g, gm_ln_b, gm_w_s, gm_b_s, gm_w_out, fox_w_qkvf, fox_b_f, fox_w_o, ffn_w_gate, ffn_w_up, ffn_conv_w, ffn_conv_b, ffn_w_down, final_norm_g, loss_target, m_mix_norm_g, m_ffn_norm_g, m_gm_w_in, m_gm_ln_g, m_gm_ln_b, m_gm_w_s, m_gm_b_s, m_gm_w_out, m_fox_w_qkvf, m_fox_b_f, m_fox_w_o, m_ffn_w_gate, m_ffn_w_up, m_ffn_conv_w, m_ffn_conv_b, m_ffn_w_down, m_final_norm_g, v_mix_norm_g, v_ffn_norm_g, v_gm_w_in, v_gm_ln_g, v_gm_ln_b, v_gm_w_s, v_gm_b_s, v_gm_w_out, v_fox_w_qkvf, v_fox_b_f, v_fox_w_o, v_ffn_w_gate, v_ffn_w_up, v_ffn_conv_w, v_ffn_conv_b, v_ffn_w_down, v_final_norm_g):
    T, D = x.shape[1], x.shape[2]
    E = gm_ln_g.shape[1]
    FF = ffn_conv_b.shape[1]
    NQKVF = 3 * D + N_HEADS
    xi, yi, ci = _mesh_pos()
    me = 4 * xi + 2 * yi + ci
    h0 = x.reshape(T, D)
    tgt = loss_target.reshape(T, D)

    nl = ffn_w_gate.shape[0]
    to16 = lambda a: a.astype(BF16)
    n_cw_rows = ffn_conv_w.size // LANES
    cw_rows = _pad_rows(_rows(ffn_conv_w.astype(F32), LANES), 16)
    w_in_g, w_out_g8, cwg = _all_gather([to16(gm_w_in[0]), to16(gm_w_out[0]), cw_rows], "ag_weights")
    w_out_g = w_out_g8.reshape(E, D)
    later, land_of, land_off, lands = [], [], [], []
    for l in range(nl):
        later += [to16(ffn_w_gate[l]), to16(ffn_w_up[l]), to16(ffn_w_down[l])]
        land_of += [2 * l, 2 * l, 2 * l + 1]
        land_off += [0, N_DEV, 0]
        lands += [lax.empty((2 * N_DEV, D, FF // N_DEV), BF16), lax.empty((N_DEV, FF // N_DEV, D), BF16)]
    later += [to16(fox_w_qkvf[0]), to16(fox_w_o[0])]
    land_of += [2 * nl, 2 * nl + 1]
    land_off += [0, 0]
    lands += [lax.empty((N_DEV, D, NQKVF // N_DEV), BF16), lax.empty((N_DEV, D // N_DEV, D), BF16)]
    ag_dst = lambda a, k: land_off[a] + _slot(_mesh_pos())
    ag_src = lambda a, p: None
    ag_send, ag_recv, later, lands, ag_token = _direct_start(later, lands, land_of, ag_dst, ag_src, w_in_g,
                                                             "ag_later_start", collective_id=1)

    def own_blocks(landed, shards, offs):
        for s, o in zip(shards, offs):
            landed = lax.dynamic_update_index_in_dim(landed, s, o + me, 0)
        return landed

    def gather_wait(idxs, after, name):
        return _direct_wait(ag_send, ag_recv, later, lands, land_of, idxs, ag_dst, ag_src, after, name)

    conv_w_full = jnp.transpose(cwg[:, :n_cw_rows].reshape(N_DEV, nl, 3, FF // N_DEV), (1, 2, 0, 3)).reshape(nl, 3, FF)

    ffn_w = {}

    def ffn_weights(l):
        return ffn_w[l]

    def land_ffn(l, shards, gu_land, dn_land):
        ffn_w[l] = (own_blocks(gu_land, shards[:2], [0, N_DEV]), own_blocks(dn_land, shards[2:3], [0]).reshape(FF, D))

    saved = {}

    def ffn_fwd(l, h_in):
        wgul, wdl = ffn_weights(l)
        hn = _rms_fwd(h_in, ffn_norm_g[l:l + 1], f"ffn{l}_norm")
        au = _mm_nn(hn, wgul, f"ffn{l}_gate_up")
        act = _ffn_mid_fwd(au, conv_w_full[l], ffn_conv_b[l:l + 1], f"ffn{l}_mid")
        h_out = _mm_nn(act, wdl, f"ffn{l}_down", res=h_in)
        saved[f"ffn{l}"] = (h_in, hn, au, act)
        return h_out

    bs_col = gm_b_s[0].reshape(GM_GROUPS, CHUNK, 1)
    hn0 = _rms_fwd(h0, mix_norm_g[0:1], "mix0_norm", after=ag_token)
    z = _mm_nn(hn0, w_in_g, "gm_in")
    gu = _sgu_fwd(z, gm_ln_g, gm_ln_b, gm_w_s[0], bs_col, "gm_sgu")
    h1 = _mm_nn(gu, w_out_g, "gm_out", res=h0)
    mine0, land0 = gather_wait([0, 1, 2], h1, "ag_ffn0_wait")
    land_ffn(0, mine0, *land0)
    h2 = ffn_fwd(0, h1)

    mine1, rest = gather_wait(list(range(3, 3 * nl + 2)), h2, "ag_layer1_wait")
    for l in range(1, nl):
        land_ffn(l, mine1[3 * (l - 1):3 * l], rest[2 * (l - 1)], rest[2 * (l - 1) + 1])
    w_qkvf = jnp.transpose(own_blocks(rest[-2], mine1[-2:-1], [0]), (1, 0, 2)).reshape(D, NQKVF)
    w_o_g = own_blocks(rest[-1], mine1[-1:], [0]).reshape(D, D)
    w_q, w_k, w_v = w_qkvf[:, :D], w_qkvf[:, D:2 * D], w_qkvf[:, 2 * D:3 * D]
    w_f = jnp.pad(w_qkvf[:, 3 * D:], ((0, 0), (0, LANES - N_HEADS)))
    bf_row = jnp.pad(fox_b_f, ((0, 0), (0, LANES - N_HEADS)))
    sel_q, sel_k, const_q, const_k = _gate_tables()
    scale = HEAD_DIM ** -0.5
    hn2 = _rms_fwd(h2, mix_norm_g[1:2], "mix1_norm")
    f_logit = _mm_nn(hn2, w_f, "fox_f")
    cp, sneg = _gate_scan(f_logit, bf_row, "fox_scan")
    qp = _qk_proj(hn2, _pad_heads(w_q), cp, sel_q, const_q, scale * LOG2E, "fox_q")
    kp = _qk_proj(hn2, _pad_heads(w_k), cp, sel_k, const_k, 1.0, "fox_k")
    vv = _mm_nn(hn2, w_v, "fox_v", out_dtype=BF16)
    o, o32, lse = _attn_fwd(qp, kp, vv, "fox_attn")
    h3 = _mm_nn(o, w_o_g, "fox_o", res=h2)
    h4 = ffn_fwd(1, h3)

    dh, dh16, d_final, loss_row = _loss_head(h4, tgt, final_norm_g.reshape(1, D), "loss_head")
    loss = lax.psum(loss_row[0, 0], ("x", "y", "c"))

    rs_dst = lambda a, k: k
    rs_src = lambda a, p: _slot(p)
    me_idx = me.astype(jnp.int32).reshape(1)

    def rs_start(grads, name, cid):
        lands = [lax.empty((N_DEV - 1,) + g.shape[1:], BF16) for g in grads]
        return _direct_start(grads, lands, list(range(len(grads))), rs_dst, rs_src, loss_row, name, collective_id=cid)

    def rs_wait(st, after, name):
        n = len(st[2])
        return _direct_wait(st[0], st[1], st[2], st[3], list(range(n)), list(range(n)), rs_dst, rs_src, after, name)

    def ffn_bwd(l, dh, dh16, after=None):
        wgul, wdl = ffn_weights(l)
        h_in, hn, au, act = saved[f"ffn{l}"]
        dact = _mm_nt([dh16], wdl, f"ffn{l}_dact", out_dtype=BF16, after=after)
        d_wd = _mm_tn(act, dh16, f"ffn{l}_dwd", out_dtype=BF16)
        da, dup, d_cw, d_cb = _ffn_mid_bwd(au, dact, conv_w_full[l], ffn_conv_b[l:l + 1], f"ffn{l}_dmid")
        dhn = _mm_nt([da, dup], wgul, f"ffn{l}_dhn")
        d_wg = _mm_tn(hn, da, f"ffn{l}_dwg", blocked_w=FF // N_DEV, out_dtype=BF16)
        d_wu = _mm_tn(hn, dup, f"ffn{l}_dwu", blocked_w=FF // N_DEV, out_dtype=BF16)
        dh_in, dh_in16, d_norm = _rms_bwd(dhn, h_in, ffn_norm_g[l:l + 1], dh, f"ffn{l}_dnorm")
        big_g = [d_wg, d_wu, d_wd.reshape(N_DEV, FF // N_DEV, D)]
        return dh_in, dh_in16, big_g, dict(cw=d_cw, cb=d_cb, norm=d_norm)

    dh, dh16, big_ffn1, g_ffn1 = ffn_bwd(1, dh, dh16)

    do = _mm_nt([dh16], w_o_g, "fox_do", out_dtype=BF16)
    d_wo = _mm_tn(o, dh16, "fox_dwo", out_dtype=BF16)
    dq, dk, dv, dqe, dke = _attn_bwd(qp, kp, vv, o32, do, lse, scale, "fox_dattn")
    gate_lane = lambda e, r: jnp.pad(jnp.transpose(e[:, r::8, :].reshape(N_HEADS, T)), ((0, 0), (0, LANES - N_HEADS)))
    df, d_bf = _gate_scan_bwd(gate_lane(dqe, 0), gate_lane(dke, 3), sneg, "fox_dscan")
    dhn = _mm_nt([df], w_f, "fox_dhn_f")
    dhn = _mm_nt([dq, dk, dv], w_qkvf[:, :3 * D], "fox_dhn_qkv", add=dhn)
    d_wq = _mm_tn(hn2, dq, "fox_dwq", out_dtype=BF16)
    d_wk = _mm_tn(hn2, dk, "fox_dwk", out_dtype=BF16)
    d_wv = _mm_tn(hn2, dv, "fox_dwv", out_dtype=BF16)
    d_wf = _mm_tn(hn2, df, "fox_dwf", out_dtype=BF16)
    d_wqkvf = jnp.concatenate([d_wq, d_wk, d_wv, d_wf[:, :N_HEADS]], axis=1)
    dh, dh16, d_mix1 = _rms_bwd(dhn, h2, mix_norm_g[1:2], dh, "mix1_dnorm")
    st1 = rs_start([jnp.transpose(d_wqkvf.reshape(D, N_DEV, NQKVF // N_DEV), (1, 0, 2)),
                    d_wo.reshape(N_DEV, D // N_DEV, D)] + big_ffn1, "rs1_start", 2)

    dh, dh16, big_ffn0, g_ffn0 = ffn_bwd(0, dh, dh16, after=st1[4])
    st2 = rs_start(big_ffn0, "rs2_start", 3)

    dgu = _mm_nt([dh16], w_out_g, "gm_dgu", out_dtype=BF16, after=st2[4])
    d_wout = _mm_tn(gu, dh16, "gm_dwout", out_dtype=BF16)
    dz, d_lng, d_lnb, d_ws, d_bs = _sgu_bwd(z, dgu, gm_ln_g, gm_ln_b, gm_w_s[0], bs_col, "gm_dsgu")
    d_win = _mm_tn(hn0, dz, "gm_dwin", blocked_w=2 * E // N_DEV, out_dtype=BF16)
    st3 = rs_start([d_win, d_wout.reshape(N_DEV, E // N_DEV, D)], "rs3_start", 4)
    dhn = _mm_nt([dz], w_in_g, "gm_dhn", after=st3[4])
    dx, _, d_mix0 = _rms_bwd(dhn, h0, mix_norm_g[0:1], dh, "mix0_dnorm")

    own1, land1 = rs_wait(st1, dx, "rs1_wait")
    own2, land2 = rs_wait(st2, land1[0], "rs2_wait")
    cat1 = lambda a, b: jnp.concatenate([a, b], axis=1)
    big_out = {}

    def big_adamw(name, w, m, v, own, landed):
        shard2d = lambda a, c=own.shape[2]: a.reshape(-1, c)
        res = _sum_adamw([(own, [-1]), (landed, list(range(N_DEV - 1)))], shard2d(w), shard2d(m), shard2d(v),
                         f"adamw_{name}", sel=me_idx)
        big_out[name] = [t.reshape(w.shape) for t in res]

    big_adamw("fox_w_qkvf", fox_w_qkvf, m_fox_w_qkvf, v_fox_w_qkvf, own1[0], land1[0])
    big_adamw("fox_w_o", fox_w_o, m_fox_w_o, v_fox_w_o, own1[1], land1[1])
    big_adamw("ffn_w_gate", ffn_w_gate, m_ffn_w_gate, v_ffn_w_gate, cat1(own2[0], own1[2]), cat1(land2[0], land1[2]))
    big_adamw("ffn_w_up", ffn_w_up, m_ffn_w_up, v_ffn_w_up, cat1(own2[1], own1[3]), cat1(land2[1], land1[3]))
    big_adamw("ffn_w_down", ffn_w_down, m_ffn_w_down, v_ffn_w_down, cat1(own2[2], own1[4]), cat1(land2[2], land1[4]))

    small = [("mix_norm_g", mix_norm_g, m_mix_norm_g, v_mix_norm_g, jnp.concatenate([d_mix0, d_mix1], axis=0)),
             ("ffn_norm_g", ffn_norm_g, m_ffn_norm_g, v_ffn_norm_g, jnp.concatenate([g_ffn0["norm"], g_ffn1["norm"]], axis=0)),
             ("gm_ln_g", gm_ln_g, m_gm_ln_g, v_gm_ln_g, d_lng),
             ("gm_ln_b", gm_ln_b, m_gm_ln_b, v_gm_ln_b, d_lnb),
             ("gm_w_s", gm_w_s, m_gm_w_s, v_gm_w_s, d_ws),
             ("gm_b_s", gm_b_s, m_gm_b_s, v_gm_b_s, d_bs),
             ("fox_b_f", fox_b_f, m_fox_b_f, v_fox_b_f, d_bf[:, :N_HEADS]),
             ("ffn_conv_b", ffn_conv_b, m_ffn_conv_b, v_ffn_conv_b, jnp.concatenate([g_ffn0["cb"], g_ffn1["cb"]], axis=0)),
             ("final_norm_g", final_norm_g, m_final_norm_g, v_final_norm_g, d_final)]
    d_cw_full = jnp.stack([g_ffn0["cw"], g_ffn1["cw"]], axis=0)

    def small_rows(a):
        flat = a.astype(F32).reshape(-1)
        n = -(-flat.size // (8 * LANES)) * (8 * LANES)
        return jnp.pad(flat, (0, n - flat.size)).reshape(-1, LANES)

    s_rows = [small_rows(p[1]).shape[0] for p in small]
    s_off = np.concatenate([[0], np.cumsum(s_rows)]).tolist()
    cw_g_rows = small_rows(d_cw_full)
    zeros_cw = jnp.zeros_like(cw_g_rows)
    cat = lambda k: jnp.concatenate([small_rows(p[k]) for p in small] + [zeros_cw], axis=0)
    g_small = jnp.concatenate([small_rows(p[4]) for p in small] + [cw_g_rows], axis=0)
    (gs_all,) = _all_gather([g_small], "ag_small_grads")
    small_out = _sum_adamw([(gs_all, list(range(N_DEV)))], cat(1), cat(2), cat(3), "adamw_small")
    gs = small_out[0]

    g_cw_full = gs[s_off[-1]:].reshape(-1)[:d_cw_full.size].reshape(d_cw_full.shape)
    g_cw = lax.dynamic_slice_in_dim(g_cw_full, me * (FF // N_DEV), FF // N_DEV, axis=2)
    cw2 = lambda a: _pad_rows(_rows(a.astype(F32), LANES), 16)
    cw_out = _sum_adamw([(cw2(g_cw)[None], [0])], cw2(ffn_conv_w), cw2(m_ffn_conv_w), cw2(v_ffn_conv_w), "adamw_conv_w")

    own3, land3 = rs_wait(st3, cw_out[0], "rs3_wait")
    big_adamw("gm_w_in", gm_w_in, m_gm_w_in, v_gm_w_in, own3[0], land3[0])
    big_adamw("gm_w_out", gm_w_out, m_gm_w_out, v_gm_w_out, own3[1], land3[1])

    names = ["mix_norm_g", "ffn_norm_g", "gm_w_in", "gm_ln_g", "gm_ln_b", "gm_w_s", "gm_b_s", "gm_w_out", "fox_w_qkvf",
             "fox_b_f", "fox_w_o", "ffn_w_gate", "ffn_w_up", "ffn_conv_w", "ffn_conv_b", "ffn_w_down", "final_norm_g"]
    small_idx = {p[0]: k for k, p in enumerate(small)}

    def pick(kind, name):
        if name in big_out:
            return big_out[name][kind]
        if name == "ffn_conv_w":
            return cw_out[kind][:n_cw_rows].reshape(ffn_conv_w.shape)
        k = small_idx[name]
        shp = small[k][1].shape
        return small_out[kind][s_off[k]:s_off[k + 1]].reshape(-1)[:int(np.prod(shp))].reshape(shp)

    outs = [loss, dx.reshape(x.shape)]
    for kind in range(4):
        outs += [pick(kind, n) for n in names]
    return tuple(outs)
```

```python
import functools
import math

import numpy as np
import jax
import jax.numpy as jnp
from jax import lax
from jax.experimental import pallas as pl
from jax.experimental.pallas import tpu as pltpu

F32 = jnp.float32
BF16 = jnp.bfloat16
MESH = pl.DeviceIdType.MESH

N_HEADS = 16
HEAD_DIM = 64
CHUNK = 128
GM_GROUPS = 8
RMS_EPS = 1e-6
LN_EPS = 1e-5
ADAM_LR = 0.001
ADAM_B1 = 0.9
ADAM_B2 = 0.999
ADAM_EPS = 1e-08
ADAM_WD = 0.01
ADAM_STEP = 10
N_DEV = 8

LANES = 128
VMEM_BYTES_V7X = 64 * 1024 * 1024
VMEM_LIMIT = 56 * 1024 * 1024

TM = 512
TM_MM = 1024
TT = 1024
TQ = 512
TF = 512
KV_UNROLL = 2
MM_BLOCK_BYTES = 8 * 1024 * 1024
NEG = -1e30
LOG2E = math.log2(math.e)
LN2 = math.log(2.0)


def _cp(sem=None, vmem=VMEM_LIMIT):
    return pltpu.CompilerParams(dimension_semantics=sem, vmem_limit_bytes=vmem)


def _gelu(x):
    c = math.sqrt(2.0 / math.pi)
    return x * (0.5 * (1.0 + jnp.tanh(c * (x + 0.044715 * (x * x * x)))))


def _gelu_grad(x):
    c = math.sqrt(2.0 / math.pi)
    t = jnp.tanh(c * (x + 0.044715 * (x * x * x)))
    return 0.5 * (1.0 + t) + x * (0.5 * (1.0 - t * t)) * (c * (1.0 + 3.0 * 0.044715 * (x * x)))


def _sigmoid(x):
    return 1.0 / (1.0 + jnp.exp(-x))


def _dot_nt(a, b):
    return lax.dot_general(a, b, (((1,), (1,)), ((), ())), preferred_element_type=F32)


def _dot_tn(a, b):
    return lax.dot_general(a, b, (((0,), (0,)), ((), ())), preferred_element_type=F32)


def _rms_fwd(h, g, name, after=None):
    T, D = h.shape
    tm = min(TM, T)

    def body(h_ref, g_ref, *rest):
        o_ref = rest[-1]
        x = h_ref[...]
        r = lax.rsqrt(jnp.mean(x * x, axis=-1, keepdims=True) + RMS_EPS)
        o_ref[...] = ((x * r) * g_ref[...]).astype(BF16)

    in_specs = [pl.BlockSpec((tm, D), lambda i: (i, 0)), pl.BlockSpec((1, D), lambda i: (0, 0))]
    args = [h, g]
    if after is not None:
        in_specs.append(pl.BlockSpec(memory_space=pl.ANY))
        args.append(after)
    return pl.pallas_call(
        body, name=name, grid=(T // tm,),
        in_specs=in_specs,
        out_specs=pl.BlockSpec((tm, D), lambda i: (i, 0)),
        out_shape=jax.ShapeDtypeStruct((T, D), BF16),
        compiler_params=_cp(("parallel",)),
    )(*args)


def _rms_bwd(dhn, h, g, dres, name):
    T, D = h.shape
    tm = min(TM, T)

    def body(d_ref, h_ref, g_ref, r_ref, o_ref, ob_ref, dg_ref):
        x = h_ref[...]
        d = d_ref[...]
        r = lax.rsqrt(jnp.mean(x * x, axis=-1, keepdims=True) + RMS_EPS)
        dyg = d * g_ref[...]
        dot = jnp.mean(dyg * x, axis=-1, keepdims=True)
        dh = r_ref[...] + (r * dyg - x * ((r * r * r) * dot))
        o_ref[...] = dh
        ob_ref[...] = dh.astype(BF16)
        part = jnp.sum(d * (x * r), axis=0, keepdims=True)

        @pl.when(pl.program_id(0) == 0)
        def _():
            dg_ref[...] = part

        @pl.when(pl.program_id(0) != 0)
        def _():
            dg_ref[...] += part

    blk = pl.BlockSpec((tm, D), lambda i: (i, 0))
    row = pl.BlockSpec((1, D), lambda i: (0, 0))
    return pl.pallas_call(
        body, name=name, grid=(T // tm,),
        in_specs=[blk, blk, row, blk],
        out_specs=[blk, blk, row],
        out_shape=[jax.ShapeDtypeStruct((T, D), F32), jax.ShapeDtypeStruct((T, D), BF16),
                   jax.ShapeDtypeStruct((1, D), F32)],
        compiler_params=_cp(("arbitrary",)),
    )(dhn, h, g, dres)


def _loss_head(h, tgt, g, name):
    T, D = h.shape
    tm = min(TM, T)

    def body(h_ref, t_ref, g_ref, o_ref, ob_ref, dg_ref, l_ref):
        x = h_ref[...]
        gg = g_ref[...]
        r = lax.rsqrt(jnp.mean(x * x, axis=-1, keepdims=True) + RMS_EPS)
        xr = x * r
        e = xr * gg - t_ref[...]
        lpart = 0.5 * jnp.sum(jnp.mean(e * e, axis=-1, keepdims=True), axis=0, keepdims=True)
        dy = e * (1.0 / D)
        dyg = dy * gg
        dot = jnp.mean(dyg * x, axis=-1, keepdims=True)
        dh = r * dyg - x * ((r * r * r) * dot)
        o_ref[...] = dh
        ob_ref[...] = dh.astype(BF16)
        part = jnp.sum(dy * xr, axis=0, keepdims=True)
        lrow = jnp.broadcast_to(lpart, (1, LANES))

        @pl.when(pl.program_id(0) == 0)
        def _():
            dg_ref[...] = part
            l_ref[...] = lrow

        @pl.when(pl.program_id(0) != 0)
        def _():
            dg_ref[...] += part
            l_ref[...] += lrow

    blk = pl.BlockSpec((tm, D), lambda i: (i, 0))
    row = pl.BlockSpec((1, D), lambda i: (0, 0))
    return pl.pallas_call(
        body, name=name, grid=(T // tm,),
        in_specs=[blk, blk, row],
        out_specs=[blk, blk, row, pl.BlockSpec((1, LANES), lambda i: (0, 0))],
        out_shape=[jax.ShapeDtypeStruct((T, D), F32), jax.ShapeDtypeStruct((T, D), BF16),
                   jax.ShapeDtypeStruct((1, D), F32), jax.ShapeDtypeStruct((1, LANES), F32)],
        compiler_params=_cp(("arbitrary",)),
    )(h, tgt, g)


def _mm_nn(a, b, name, out_dtype=F32, res=None):
    M, K = a.shape
    b3 = b if b.ndim == 3 else b[None]
    nb, _, w = b3.shape
    N = nb * w
    tm = min(TM_MM, M, max(256, MM_BLOCK_BYTES // (4 * N)))
    o_spec = pl.BlockSpec((tm, N), lambda i: (i, 0))
    in_specs = [pl.BlockSpec((tm, K), lambda i: (i, 0)), pl.BlockSpec((nb, K, w), lambda i: (0, 0, 0))]
    args = [a, b3]
    if res is not None:
        in_specs.append(o_spec)
        args.append(res)

    def body(*refs):
        a_ref, b_ref = refs[0], refs[1]
        o_ref = refs[-1]
        av = a_ref[...]
        for j in range(nb):
            cols = slice(j * w, (j + 1) * w)
            acc = jnp.dot(av, b_ref[j], preferred_element_type=F32)
            if res is not None:
                acc = refs[2][:, cols] + acc
            o_ref[:, cols] = acc.astype(out_dtype)

    return pl.pallas_call(
        body, name=name, grid=(M // tm,),
        in_specs=in_specs, out_specs=o_spec,
        out_shape=jax.ShapeDtypeStruct((M, N), out_dtype),
        compiler_params=_cp(("parallel",)),
    )(*args)


def _mm_nt(a_list, b, name, out_dtype=F32, add=None, after=None):
    M, kw = a_list[0].shape
    tm = min(TM, M)
    na = len(a_list)
    blocked = b.ndim == 3
    N = b.shape[1] if blocked else b.shape[0]
    b_spec = pl.BlockSpec(b.shape, lambda i: (0,) * b.ndim)
    o_spec = pl.BlockSpec((tm, N), lambda i: (i, 0))
    in_specs = [pl.BlockSpec((tm, kw), lambda i: (i, 0)) for _ in a_list] + [b_spec]
    args = list(a_list) + [b]
    if add is not None:
        in_specs.append(o_spec)
        args.append(add)
    if after is not None:
        in_specs.append(pl.BlockSpec(memory_space=pl.ANY))
        args.append(after)

    def body(*refs):
        a_refs = refs[:na]
        b_ref = refs[na]
        o_ref = refs[-1]
        acc = refs[na + 1][...] if add is not None else None
        for s, a_ref in enumerate(a_refs):
            if blocked:
                w = b_ref.shape[2]
                per = kw // w
                parts = [_dot_nt(a_ref[:, jj * w:(jj + 1) * w], b_ref[s * per + jj]) for jj in range(per)]
            else:
                parts = [_dot_nt(a_ref[...], b_ref[:, s * kw:(s + 1) * kw])]
            for part in parts:
                acc = part if acc is None else acc + part
        o_ref[...] = acc.astype(out_dtype)

    return pl.pallas_call(
        body, name=name, grid=(M // tm,),
        in_specs=in_specs, out_specs=o_spec,
        out_shape=jax.ShapeDtypeStruct((M, N), out_dtype),
        compiler_params=_cp(("parallel",)),
    )(*args)


def _mm_tn(x, y, name, blocked_w=None, out_dtype=F32):
    T, Kx = x.shape
    N = y.shape[1]
    tt = min(TT, T)
    nt = T // tt
    tkx = min(Kx, max(LANES, MM_BLOCK_BYTES // (4 * N)))
    if blocked_w is not None:
        blk_shape, full_shape = (N // blocked_w, tkx, blocked_w), (N // blocked_w, Kx, blocked_w)
        o_spec = pl.BlockSpec(blk_shape, lambda i, t: (0, i, 0))
    else:
        blk_shape, full_shape = (tkx, N), (Kx, N)
        o_spec = pl.BlockSpec(blk_shape, lambda i, t: (i, 0))

    def body(x_ref, y_ref, o_ref, acc_ref):
        part = _dot_tn(x_ref[...], y_ref[...])
        t = pl.program_id(1)
        if blocked_w is None:
            pieces = [(slice(None), part)]
        else:
            pieces = [(j, part[:, j * blocked_w:(j + 1) * blocked_w]) for j in range(N // blocked_w)]

        @pl.when(t == 0)
        def _():
            for idx, pj in pieces:
                acc_ref[idx] = pj

        @pl.when(t != 0)
        def _():
            for idx, pj in pieces:
                acc_ref[idx] += pj

        @pl.when(t == nt - 1)
        def _():
            o_ref[...] = acc_ref[...].astype(out_dtype)

    return pl.pallas_call(
        body, name=name, grid=(Kx // tkx, nt),
        in_specs=[pl.BlockSpec((tt, tkx), lambda i, t: (t, i)),
                  pl.BlockSpec((tt, N), lambda i, t: (t, 0))],
        out_specs=o_spec, out_shape=jax.ShapeDtypeStruct(full_shape, out_dtype),
        scratch_shapes=[pltpu.VMEM(blk_shape, F32)],
        compiler_params=_cp(("parallel", "arbitrary")),
    )(x, y)


def _sgu_pieces(z, lng, lnb, wc, bs_ref):
    E = z.shape[1] // 2
    gd = E // GM_GROUPS
    zu, zv = z[:, :E], z[:, E:]
    u = _gelu(zu)
    v = _gelu(zv)
    mu = jnp.mean(v, axis=-1, keepdims=True)
    xc = v - mu
    rs = lax.rsqrt(jnp.mean(xc * xc, axis=-1, keepdims=True) + LN_EPS)
    xhat = xc * rs
    vln = xhat * lng + lnb
    s = []
    for g in range(GM_GROUPS):
        vg = vln[:, g * gd:(g + 1) * gd].astype(BF16)
        s.append(jnp.dot(wc[g], vg, preferred_element_type=F32) + bs_ref[g])
    return zu, zv, u, xhat, rs, vln, s


def _causal_ws(ws_ref):
    t = lax.broadcasted_iota(jnp.int32, (CHUNK, CHUNK), 0)
    s = lax.broadcasted_iota(jnp.int32, (CHUNK, CHUNK), 1)
    tri = t >= s
    return [jnp.where(tri, ws_ref[g], 0.0).astype(BF16) for g in range(GM_GROUPS)], tri


def _sgu_fwd(z, lng, lnb, ws, bs, name):
    T, E2 = z.shape
    E = E2 // 2
    gd = E // GM_GROUPS
    tm = min(2 * CHUNK, T)

    def body(z_ref, lng_ref, lnb_ref, ws_ref, bs_ref, o_ref):
        wc, _ = _causal_ws(ws_ref)
        for c in range(tm // CHUNK):
            rows = slice(c * CHUNK, (c + 1) * CHUNK)
            _, _, u, _, _, _, s = _sgu_pieces(z_ref[rows, :], lng_ref[...], lnb_ref[...], wc, bs_ref)
            for g in range(GM_GROUPS):
                cols = slice(g * gd, (g + 1) * gd)
                o_ref[rows, cols] = (u[:, cols] * s[g]).astype(BF16)

    full = lambda shape: pl.BlockSpec(shape, lambda i: (0,) * len(shape))
    return pl.pallas_call(
        body, name=name, grid=(T // tm,),
        in_specs=[pl.BlockSpec((tm, E2), lambda i: (i, 0)), full((1, E)), full((1, E)),
                  full((GM_GROUPS, CHUNK, CHUNK)), full((GM_GROUPS, CHUNK, 1))],
        out_specs=pl.BlockSpec((tm, E), lambda i: (i, 0)),
        out_shape=jax.ShapeDtypeStruct((T, E), BF16),
        compiler_params=_cp(("parallel",)),
    )(z, lng, lnb, ws, bs)


def _sgu_bwd(z, dg, lng, lnb, ws, bs, name):
    T, E2 = z.shape
    E = E2 // 2
    gd = E // GM_GROUPS
    tm = min(2 * CHUNK, T)
    nsteps = T // tm

    def body(z_ref, dg_ref, lng_ref, lnb_ref, ws_ref, bs_ref, dz_ref, dlng_ref, dlnb_ref, dws_ref, dbs_ref):
        i = pl.program_id(0)

        @pl.when(i == 0)
        def _():
            dlng_ref[...] = jnp.zeros_like(dlng_ref)
            dlnb_ref[...] = jnp.zeros_like(dlnb_ref)
            dws_ref[...] = jnp.zeros_like(dws_ref)
            dbs_ref[...] = jnp.zeros_like(dbs_ref)

        wc, tri = _causal_ws(ws_ref)
        lng_v = lng_ref[...]
        for c in range(tm // CHUNK):
            rows = slice(c * CHUNK, (c + 1) * CHUNK)
            zu, zv, u, xhat, rs, vln, s = _sgu_pieces(z_ref[rows, :], lng_v, lnb_ref[...], wc, bs_ref)
            dgc = dg_ref[rows, :].astype(F32)
            du, dvln = [], []
            for g in range(GM_GROUPS):
                cols = slice(g * gd, (g + 1) * gd)
                dgg = dgc[:, cols]
                du.append(dgg * s[g])
                ds = dgg * u[:, cols]
                dsb = ds.astype(BF16)
                dws_ref[g] += _dot_nt(dsb, vln[:, cols].astype(BF16))
                dbs_ref[g] += jnp.sum(ds, axis=-1, keepdims=True)
                dvln.append(_dot_tn(wc[g], dsb))
            du = jnp.concatenate(du, axis=1)
            dvln = jnp.concatenate(dvln, axis=1)
            dlng_ref[...] += jnp.sum(dvln * xhat, axis=0, keepdims=True)
            dlnb_ref[...] += jnp.sum(dvln, axis=0, keepdims=True)
            dxh = dvln * lng_v
            m1 = jnp.mean(dxh, axis=-1, keepdims=True)
            m2 = jnp.mean(dxh * xhat, axis=-1, keepdims=True)
            dv = rs * (dxh - m1 - xhat * m2)
            dz_ref[rows, :E] = (du * _gelu_grad(zu)).astype(BF16)
            dz_ref[rows, E:] = (dv * _gelu_grad(zv)).astype(BF16)

        @pl.when(i == nsteps - 1)
        def _():
            for g in range(GM_GROUPS):
                dws_ref[g] = jnp.where(tri, dws_ref[g], 0.0)

    full = lambda shape: pl.BlockSpec(shape, lambda i: (0,) * len(shape))
    return pl.pallas_call(
        body, name=name, grid=(nsteps,),
        in_specs=[pl.BlockSpec((tm, E2), lambda i: (i, 0)), pl.BlockSpec((tm, E), lambda i: (i, 0)),
                  full((1, E)), full((1, E)), full((GM_GROUPS, CHUNK, CHUNK)), full((GM_GROUPS, CHUNK, 1))],
        out_specs=[pl.BlockSpec((tm, E2), lambda i: (i, 0)), full((1, E)), full((1, E)),
                   full((GM_GROUPS, CHUNK, CHUNK)), full((GM_GROUPS, CHUNK, 1))],
        out_shape=[jax.ShapeDtypeStruct((T, E2), BF16), jax.ShapeDtypeStruct((1, E), F32),
                   jax.ShapeDtypeStruct((1, E), F32), jax.ShapeDtypeStruct((GM_GROUPS, CHUNK, CHUNK), F32),
                   jax.ShapeDtypeStruct((GM_GROUPS, CHUNK, 1), F32)],
        compiler_params=_cp(("arbitrary",)),
    )(z, dg, lng, lnb, ws, bs)


HALO = 16


def _conv_taps(a_ext, w_ref, b_ref):
    n = a_ext.shape[0]
    am1 = pltpu.roll(a_ext, 1, 0)
    am2 = pltpu.roll(a_ext, 2, 0)
    del n
    return ((b_ref[...] + am2 * w_ref[0:1, :]) + am1 * w_ref[1:2, :]) + a_ext * w_ref[2:3, :], am1, am2


def _ffn_mid_fwd(au, cw, cb, name):
    T, F = au.shape[0], au.shape[1] // 2
    tm, tf = min(TM, T), min(TF, F)
    hb = tm // HALO
    nf = F // tf

    def body(a_ref, ap_ref, u_ref, w_ref, b_ref, o_ref):
        i = pl.program_id(1)
        prev = jnp.where(i == 0, 0.0, ap_ref[...])
        ext = jnp.concatenate([prev, a_ref[...]], axis=0)
        conv, _, _ = _conv_taps(ext, w_ref, b_ref)
        conv = conv[HALO:, :]
        o_ref[...] = ((conv * _sigmoid(conv)) * u_ref[...]).astype(BF16)

    main = pl.BlockSpec((tm, tf), lambda f, i: (i, f))
    return pl.pallas_call(
        body, name=name, grid=(nf, T // tm),
        in_specs=[main, pl.BlockSpec((HALO, tf), lambda f, i: (jnp.maximum(i * hb - 1, 0), f)),
                  pl.BlockSpec((tm, tf), lambda f, i: (i, nf + f)),
                  pl.BlockSpec((3, tf), lambda f, i: (0, f)), pl.BlockSpec((1, tf), lambda f, i: (0, f))],
        out_specs=main, out_shape=jax.ShapeDtypeStruct((T, F), BF16),
        compiler_params=_cp(("parallel", "parallel")),
    )(au, au, au, cw, cb)


def _ffn_mid_bwd(au, dact, cw, cb, name):
    T, F = au.shape[0], au.shape[1] // 2
    tm, tf = min(TM, T), min(TF, F)
    hb = tm // HALO
    nt = T // tm
    nf = F // tf
    last_h = T // HALO - 1

    def body(a_ref, ap_ref, an_ref, u_ref, un_ref, d_ref, dn_ref, w_ref, b_ref, da_ref, du_ref, dcw_ref, dcb_ref):
        i = pl.program_id(1)
        prev = jnp.where(i == 0, 0.0, ap_ref[...])
        a_main = a_ref[...]
        a_ext = jnp.concatenate([prev, a_main, an_ref[...]], axis=0)
        conv, am1, am2 = _conv_taps(a_ext, w_ref, b_ref)
        conv = conv[HALO:, :]
        sig = _sigmoid(conv)
        u_ext = jnp.concatenate([u_ref[...], un_ref[...]], axis=0)
        d_ext = jnp.concatenate([d_ref[...], dn_ref[...]], axis=0).astype(F32)
        n = tm + HALO
        row = lax.broadcasted_iota(jnp.int32, (n, 1), 0)
        live = jnp.logical_or(row < tm, i < nt - 1)
        dconv = jnp.where(live, d_ext * u_ext * (sig * (1.0 + conv * (1.0 - sig))), 0.0)
        du_ref[...] = (d_ext[:tm, :] * (conv[:tm, :] * sig[:tm, :])).astype(BF16)
        dp1 = pltpu.roll(dconv, n - 1, 0)[:tm, :]
        dp2 = pltpu.roll(dconv, n - 2, 0)[:tm, :]
        dc = dconv[:tm, :]
        da_ref[...] = ((dc * w_ref[2:3, :] + dp1 * w_ref[1:2, :]) + dp2 * w_ref[0:1, :]).astype(BF16)
        g2 = jnp.sum(dc * a_main, axis=0, keepdims=True)
        g1 = jnp.sum(dc * am1[HALO:HALO + tm, :], axis=0, keepdims=True)
        g0 = jnp.sum(dc * am2[HALO:HALO + tm, :], axis=0, keepdims=True)
        gb = jnp.sum(dc, axis=0, keepdims=True)

        @pl.when(i == 0)
        def _():
            dcw_ref[...] = jnp.zeros_like(dcw_ref)
            dcb_ref[...] = jnp.zeros_like(dcb_ref)

        dcw_ref[0:1, :] += g0
        dcw_ref[1:2, :] += g1
        dcw_ref[2:3, :] += g2
        dcb_ref[...] += gb

    main = pl.BlockSpec((tm, tf), lambda f, i: (i, f))
    prev = pl.BlockSpec((HALO, tf), lambda f, i: (jnp.maximum(i * hb - 1, 0), f))
    nxt = pl.BlockSpec((HALO, tf), lambda f, i: (jnp.minimum((i + 1) * hb, last_h), f))
    main_u = pl.BlockSpec((tm, tf), lambda f, i: (i, nf + f))
    nxt_u = pl.BlockSpec((HALO, tf), lambda f, i: (jnp.minimum((i + 1) * hb, last_h), nf + f))
    return pl.pallas_call(
        body, name=name, grid=(nf, nt),
        in_specs=[main, prev, nxt, main_u, nxt_u, main, nxt,
                  pl.BlockSpec((3, tf), lambda f, i: (0, f)), pl.BlockSpec((1, tf), lambda f, i: (0, f))],
        out_specs=[main, main, pl.BlockSpec((3, tf), lambda f, i: (0, f)), pl.BlockSpec((1, tf), lambda f, i: (0, f))],
        out_shape=[jax.ShapeDtypeStruct((T, F), BF16), jax.ShapeDtypeStruct((T, F), BF16),
                   jax.ShapeDtypeStruct((3, F), F32), jax.ShapeDtypeStruct((1, F), F32)],
        compiler_params=_cp(("parallel", "arbitrary")),
    )(au, au, au, au, au, dact, dact, cw, cb)


def _split3(x):
    hi = x.astype(BF16)
    r1 = x - hi.astype(F32)
    mid = r1.astype(BF16)
    lo = (r1 - mid.astype(F32)).astype(BF16)
    return hi, mid, lo


def _tri_ones(n, upper):
    r = lax.broadcasted_iota(jnp.int32, (n, n), 0)
    c = lax.broadcasted_iota(jnp.int32, (n, n), 1)
    return jnp.where((r <= c) if upper else (r >= c), 1.0, 0.0).astype(BF16)


def _gate_scan(f, bf, name):
    T = f.shape[0]
    tm = min(256, T)

    def body(f_ref, b_ref, cp_ref, sn_ref, carry_ref):
        i = pl.program_id(0)

        @pl.when(i == 0)
        def _():
            carry_ref[...] = jnp.zeros_like(carry_ref)

        x = f_ref[...] + b_ref[...]
        e = jnp.exp(-jnp.abs(x))
        logf = jnp.minimum(x, 0.0) - jnp.log(1.0 + e)
        sn_ref[...] = jnp.where(x >= 0.0, e / (1.0 + e), 1.0 / (1.0 + e))
        tri = _tri_ones(tm, upper=False)
        c = carry_ref[...]
        for piece in _split3(logf):
            c = c + jnp.dot(tri, piece, preferred_element_type=F32)
        carry_ref[...] += jnp.sum(logf, axis=0, keepdims=True)
        hi, mid, lo = _split3(c * LOG2E)
        cp_ref[:, 0:LANES] = hi
        cp_ref[:, LANES:2 * LANES] = mid
        cp_ref[:, 2 * LANES:3 * LANES] = lo

    return pl.pallas_call(
        body, name=name, grid=(T // tm,),
        in_specs=[pl.BlockSpec((tm, LANES), lambda i: (i, 0)), pl.BlockSpec((1, LANES), lambda i: (0, 0))],
        out_specs=[pl.BlockSpec((tm, 3 * LANES), lambda i: (i, 0)), pl.BlockSpec((tm, LANES), lambda i: (i, 0))],
        out_shape=[jax.ShapeDtypeStruct((T, 3 * LANES), BF16), jax.ShapeDtypeStruct((T, LANES), F32)],
        scratch_shapes=[pltpu.VMEM((1, LANES), F32)],
        compiler_params=_cp(("arbitrary",)),
    )(f, bf)


def _gate_scan_bwd(dcq, dck, sneg, name):
    T = dcq.shape[0]
    tm = min(256, T)
    n = T // tm

    def body(dcq_ref, dck_ref, sn_ref, df_ref, db_ref, carry_ref):
        i = pl.program_id(0)

        @pl.when(i == 0)
        def _():
            carry_ref[...] = jnp.zeros_like(carry_ref)
            db_ref[...] = jnp.zeros_like(db_ref)

        tri = _tri_ones(tm, upper=True)
        dcb = dcq_ref[...] - dck_ref[...]
        acc = carry_ref[...]
        for piece in _split3(dcb):
            acc = acc + jnp.dot(tri, piece, preferred_element_type=F32)
        carry_ref[...] += jnp.sum(dcb, axis=0, keepdims=True)
        df = acc * sn_ref[...]
        df_ref[...] = df.astype(BF16)
        db_ref[...] += jnp.sum(df, axis=0, keepdims=True)

    rev = pl.BlockSpec((tm, LANES), lambda i: (n - 1 - i, 0))
    return pl.pallas_call(
        body, name=name, grid=(n,),
        in_specs=[rev, rev, rev],
        out_specs=[rev, pl.BlockSpec((1, LANES), lambda i: (0, 0))],
        out_shape=[jax.ShapeDtypeStruct((T, LANES), BF16), jax.ShapeDtypeStruct((1, LANES), F32)],
        scratch_shapes=[pltpu.VMEM((1, LANES), F32)],
        compiler_params=_cp(("arbitrary",)),
    )(dcq, dck, sneg)


def _qk_proj(hn, w_pad, cp, sel, const, scale, name):
    T, D = hn.shape
    H = w_pad.shape[1] // LANES
    tm = min(TM_MM, T)

    def body(a_ref, w_ref, cp_ref, sel_ref, c_ref, o_ref):
        acc = jnp.dot(a_ref[...], w_ref[...], preferred_element_type=F32)
        if scale != 1.0:
            acc = acc * scale
        acc = acc + jnp.dot(cp_ref[...], sel_ref[...], preferred_element_type=F32) + c_ref[...]
        o_ref[0] = acc[:, :LANES].astype(BF16)
        o_ref[1] = acc[:, LANES:].astype(BF16)

    return pl.pallas_call(
        body, name=name, grid=(T // tm, H // 2),
        in_specs=[pl.BlockSpec((tm, D), lambda i, p: (i, 0)), pl.BlockSpec((D, 2 * LANES), lambda i, p: (0, p)),
                  pl.BlockSpec((tm, 3 * LANES), lambda i, p: (i, 0)),
                  pl.BlockSpec((None, 3 * LANES, 2 * LANES), lambda i, p: (p, 0, 0)),
                  pl.BlockSpec((None, 1, 2 * LANES), lambda i, p: (p, 0, 0))],
        out_specs=pl.BlockSpec((2, tm, LANES), lambda i, p: (p, i, 0)),
        out_shape=jax.ShapeDtypeStruct((H, T, LANES), BF16),
        compiler_params=_cp(("parallel", "arbitrary")),
    )(hn, w_pad, cp, sel, const)


def _lane_lo():
    return lax.broadcasted_iota(jnp.int32, (1, LANES), 1) < HEAD_DIM


def _attn_fwd(qp, kp, v, name):
    H, T, _ = qp.shape
    tq = min(TQ, T)
    nrep = tq // LANES
    n_parts = 4 if tq % 512 == 0 else 1
    rows = tq // n_parts

    def body(q_ref, k_ref, v_ref, o_ref, o32_ref, lse_ref, m_sc, acc_sc):
        i = pl.program_id(1)
        m_sc[...] = jnp.full(m_sc.shape, NEG, F32)
        acc_sc[...] = jnp.zeros_like(acc_sc)
        ones_col = jnp.where(lax.broadcasted_iota(jnp.int32, (tq, LANES), 1) == 0, 1.0, 0.0).astype(BF16)

        def step(j, masked):
            off = pl.multiple_of(j * tq, tq)
            vaug = jnp.concatenate([v_ref[pl.ds(off, tq), :], ones_col], axis=1)
            chains = [(h, rp) for h in range(2) for rp in range(n_parts)]
            s_all = [_dot_nt(q_ref[h, rp * rows:(rp + 1) * rows, :], k_ref[h, pl.ds(off, tq), :]) for h, rp in chains]
            for (h, rp), s in zip(chains, s_all):
                rsl = slice(rp * rows, (rp + 1) * rows)
                tiles = [s[:, c * LANES:(c + 1) * LANES] for c in range(nrep)]
                if masked:
                    r = lax.broadcasted_iota(jnp.int32, (rows, LANES), 0) + rp * rows
                    cc = lax.broadcasted_iota(jnp.int32, (rows, LANES), 1)
                    tiles = [jnp.where(r >= cc + c * LANES, t, NEG) for c, t in enumerate(tiles)]
                mt = tiles[0]
                for t in tiles[1:]:
                    mt = jnp.maximum(mt, t)
                m_prev = m_sc[h, rsl, :]
                m_new = jnp.maximum(m_prev, jnp.max(mt, axis=-1, keepdims=True))
                alpha = jnp.exp2(m_prev - m_new)
                p16 = jnp.concatenate([jnp.exp2(t - m_new).astype(BF16) for t in tiles], axis=1)
                pv = jnp.dot(p16, vaug, preferred_element_type=F32)
                acc_sc[h, rsl, :] = jnp.concatenate([alpha, alpha], axis=1) * acc_sc[h, rsl, :] + pv
                m_sc[h, rsl, :] = m_new

        def loop_body(j, carry):
            step(j, False)
            return carry

        lax.fori_loop(0, i, loop_body, 0)
        step(i, True)
        lo = _lane_lo()
        acc0, acc1 = acc_sc[0], acc_sc[1]
        l0 = jnp.sum(acc0[:, LANES:], axis=-1, keepdims=True)
        l1 = jnp.sum(acc1[:, LANES:], axis=-1, keepdims=True)
        o = jnp.where(lo, acc0[:, :LANES] / l0, acc1[:, :LANES] / l1)
        o_ref[...] = o.astype(BF16)
        o32_ref[...] = o
        lse_ref[...] = jnp.where(lo, m_sc[0] + jnp.log(l0) * LOG2E, m_sc[1] + jnp.log(l1) * LOG2E)

    oblk = pl.BlockSpec((tq, LANES), lambda p, i: (i, p))
    return pl.pallas_call(
        body, name=name, grid=(H // 2, T // tq),
        in_specs=[pl.BlockSpec((2, tq, LANES), lambda p, i: (p, i, 0)),
                  pl.BlockSpec((2, T, LANES), lambda p, i: (p, 0, 0)),
                  pl.BlockSpec((T, LANES), lambda p, i: (0, p))],
        out_specs=[oblk, oblk, pl.BlockSpec((None, tq, LANES), lambda p, i: (p, i, 0))],
        out_shape=[jax.ShapeDtypeStruct((T, H * HEAD_DIM), BF16), jax.ShapeDtypeStruct((T, H * HEAD_DIM), F32),
                   jax.ShapeDtypeStruct((H // 2, T, LANES), F32)],
        scratch_shapes=[pltpu.VMEM((2, tq, LANES), F32), pltpu.VMEM((2, tq, 2 * LANES), F32)],
        compiler_params=_cp(("parallel", "arbitrary")),
    )(qp, kp, v)


def _attn_fwd_t(qp, kp, vt, name):
    H, T, _ = qp.shape
    tq = min(TQ, T)
    hd = HEAD_DIM
    ext = hd + 16

    def body(q_ref, k_ref, vt_ref, o_ref, o32_ref, lse_ref, m_sc, acc_sc):
        i = pl.program_id(1)
        m_sc[...] = jnp.full(m_sc.shape, NEG, F32)
        acc_sc[...] = jnp.zeros_like(acc_sc)
        q_t = [jnp.transpose(q_ref[h].astype(F32)).astype(BF16) for h in range(2)]
        ones_rows = jnp.where(lax.broadcasted_iota(jnp.int32, (16, tq), 0) == 0, 1.0, 0.0).astype(BF16)

        def steps(blocks):
            offs = [pl.multiple_of(j * tq, tq) for j, _ in blocks]
            s_all = [[jnp.dot(k_ref[h, pl.ds(off, tq), :], q_t[h], preferred_element_type=F32) for h in range(2)]
                     for off in offs]
            for (j, masked), off, s_blk in zip(blocks, offs, s_all):
                for h in range(2):
                    s = s_blk[h]
                    if masked:
                        kr = lax.broadcasted_iota(jnp.int32, (tq, tq), 0)
                        qc = lax.broadcasted_iota(jnp.int32, (tq, tq), 1)
                        s = jnp.where(qc >= kr, s, NEG)
                    m_prev = m_sc[h]
                    m_new = jnp.maximum(m_prev, jnp.max(s, axis=0, keepdims=True))
                    alpha = jnp.exp2(m_prev - m_new)
                    p16 = jnp.exp2(s - m_new).astype(BF16)
                    v_aug = jnp.concatenate([vt_ref[h * hd:(h + 1) * hd, pl.ds(off, tq)], ones_rows], axis=0)
                    pv = jnp.dot(v_aug, p16, preferred_element_type=F32)
                    acc_sc[h] = alpha * acc_sc[h] + pv
                    m_sc[h] = m_new

        def group_body(t, carry):
            steps([(KV_UNROLL * t + u, False) for u in range(KV_UNROLL)])
            return carry

        lax.fori_loop(0, i // KV_UNROLL, group_body, 0)
        for rem in range(KV_UNROLL):

            @pl.when(i % KV_UNROLL == rem)
            def _(rem=rem):
                steps([(i - rem + u, u == rem) for u in range(rem + 1)])

        o_t, lse_t = [], []
        for h in range(2):
            acc = acc_sc[h]
            l = acc[hd:hd + 1, :]
            o_t.append(acc[:hd, :] / l)
            lse_t.append(jnp.broadcast_to(m_sc[h] + jnp.log(l) * LOG2E, (hd, tq)))
        o = jnp.transpose(jnp.concatenate(o_t, axis=0))
        o_ref[...] = o.astype(BF16)
        o32_ref[...] = o
        lse_ref[...] = jnp.transpose(jnp.concatenate(lse_t, axis=0))

    oblk = pl.BlockSpec((tq, LANES), lambda p, i: (i, p))
    return pl.pallas_call(
        body, name=name, grid=(H // 2, T // tq),
        in_specs=[pl.BlockSpec((2, tq, LANES), lambda p, i: (p, i, 0)),
                  pl.BlockSpec((2, T, LANES), lambda p, i: (p, 0, 0)),
                  pl.BlockSpec((2 * hd, T), lambda p, i: (p, 0))],
        out_specs=[oblk, oblk, pl.BlockSpec((None, tq, LANES), lambda p, i: (p, i, 0))],
        out_shape=[jax.ShapeDtypeStruct((T, H * HEAD_DIM), BF16), jax.ShapeDtypeStruct((T, H * HEAD_DIM), F32),
                   jax.ShapeDtypeStruct((H // 2, T, LANES), F32)],
        scratch_shapes=[pltpu.VMEM((2, 1, tq), F32), pltpu.VMEM((2, ext, tq), F32)],
        compiler_params=_cp(("parallel", "arbitrary")),
    )(qp, kp, vt)


def _attn_bwd(qp, kp, v, o, do, lse, scale, name):
    H, T, _ = qp.shape
    tq = min(TQ, T)
    nq = T // tq
    nrep = tq // LANES

    def body(q_ref, k_ref, v_ref, o_ref, do_ref, lse_ref, dq_ref, dk_ref, dv_ref, dqe_ref, dke_ref, dk_sc, dv_sc, dq_sc):
        i = pl.program_id(1)

        @pl.when(i == 0)
        def _():
            dk_sc[...] = jnp.zeros_like(dk_sc)
            dv_sc[...] = jnp.zeros_like(dv_sc)

        dq_sc[...] = jnp.zeros_like(dq_sc)

        lo = _lane_lo()
        dob = do_ref[...]
        dof = dob.astype(F32)
        prod = dof * o_ref[...].astype(F32)
        lse2 = lse_ref[...]
        lse2_sw = pltpu.roll(lse2, HEAD_DIM, 1)
        zero = jnp.zeros_like(dob)
        do_h = [jnp.where(lo, dob, zero), jnp.where(lo, zero, dob)]
        rep = lambda col: jnp.broadcast_to(col, (tq, LANES))
        delta = [rep(jnp.sum(jnp.where(lo, prod, 0.0), axis=-1, keepdims=True)),
                 rep(jnp.sum(jnp.where(lo, 0.0, prod), axis=-1, keepdims=True))]
        lse_h = [jnp.where(lo, lse2, lse2_sw), jnp.where(lo, lse2_sw, lse2)]
        qs = [q_ref[0], q_ref[1]]
        tr16 = lambda a: jnp.transpose(a.astype(F32)).astype(BF16)
        q_t = [tr16(qs[0]), tr16(qs[1])]
        do_t = [tr16(do_h[0]), tr16(do_h[1])]

        def step(j, masked):
            off = pl.multiple_of(j * tq, tq)
            vblk = v_ref[pl.ds(off, tq), :]
            dv_add = None
            kblks = [k_ref[h, pl.ds(off, tq), :] for h in range(2)]
            s_all = [_dot_nt(qs[h], kblks[h]) for h in range(2)]
            dp_all = [_dot_nt(do_h[h], vblk) for h in range(2)]
            for h in range(2):
                kblk, s, dp = kblks[h], s_all[h], dp_all[h]
                p16, ds16 = [], []
                for c in range(nrep):
                    cols = slice(c * LANES, (c + 1) * LANES)
                    p = jnp.exp2(s[:, cols] - lse_h[h])
                    if masked:
                        r = lax.broadcasted_iota(jnp.int32, (tq, LANES), 0)
                        cc = lax.broadcasted_iota(jnp.int32, (tq, LANES), 1)
                        p = jnp.where(r >= cc + c * LANES, p, 0.0)
                    p16.append(p.astype(BF16))
                    ds16.append((p * (dp[:, cols] - delta[h])).astype(BF16))
                p16 = jnp.concatenate(p16, axis=1)
                dsb = jnp.concatenate(ds16, axis=1)
                dq_sc[h] += jnp.dot(dsb, kblk, preferred_element_type=F32)
                dk_sc[h, :, pl.ds(off, tq)] += jnp.dot(q_t[h], dsb, preferred_element_type=F32)
                pv = jnp.dot(do_t[h], p16, preferred_element_type=F32)
                dv_add = pv if dv_add is None else dv_add + pv
            dv_sc[:, pl.ds(off, tq)] += dv_add

        def loop_body(j, carry):
            step(j, False)
            return carry

        lax.fori_loop(0, i, loop_body, 0)
        step(i, True)
        dq0, dq1 = dq_sc[0], dq_sc[1]
        dq_ref[...] = (jnp.where(lo, dq0, pltpu.roll(dq1, HEAD_DIM, 1)) * scale).astype(BF16)
        dqe_ref[0:8, :] = jnp.transpose(dq0)[HEAD_DIM:HEAD_DIM + 8, :]
        dqe_ref[8:16, :] = jnp.transpose(dq1)[HEAD_DIM:HEAD_DIM + 8, :]

        @pl.when(i == nq - 1)
        def _():
            for h in range(2):
                dke_ref[8 * h:8 * h + 8, :] = dk_sc[h, HEAD_DIM:HEAD_DIM + 8, :]
            for cb in range(nq):
                tok = slice(cb * tq, (cb + 1) * tq)
                dk0 = jnp.transpose(dk_sc[0, :, tok])
                dk1 = jnp.transpose(dk_sc[1, :, tok])
                dk_ref[tok, :] = (jnp.where(lo, dk0, pltpu.roll(dk1, HEAD_DIM, 1)) * LN2).astype(BF16)
                dv_ref[tok, :] = jnp.transpose(dv_sc[:, tok]).astype(BF16)

    qblk = pl.BlockSpec((tq, LANES), lambda p, i: (i, p))
    pair = pl.BlockSpec((T, LANES), lambda p, i: (0, p))
    tok16 = jax.ShapeDtypeStruct((T, H * HEAD_DIM), BF16)
    gate32 = jax.ShapeDtypeStruct((H // 2, 16, T), F32)
    return pl.pallas_call(
        body, name=name, grid=(H // 2, nq),
        in_specs=[pl.BlockSpec((2, tq, LANES), lambda p, i: (p, i, 0)),
                  pl.BlockSpec((2, T, LANES), lambda p, i: (p, 0, 0)),
                  pair, qblk, qblk,
                  pl.BlockSpec((None, tq, LANES), lambda p, i: (p, i, 0))],
        out_specs=[qblk, pair, pair, pl.BlockSpec((None, 16, tq), lambda p, i: (p, 0, i)),
                   pl.BlockSpec((None, 16, T), lambda p, i: (p, 0, 0))],
        out_shape=[tok16, tok16, tok16, gate32, gate32],
        scratch_shapes=[pltpu.VMEM((2, LANES, T), F32), pltpu.VMEM((LANES, T), F32),
                        pltpu.VMEM((2, tq, LANES), F32)],
        compiler_params=_cp(("parallel", "arbitrary")),
    )(qp, kp, v, o, do, lse)


def _mesh_pos():
    return lax.axis_index("x"), lax.axis_index("y"), lax.axis_index("c")


def _all_gather(arrs, name, groups=None):
    n = len(arrs)
    if groups is None:
        groups = [(a, 0) for a in range(n)]
    ng = 1 + max(g for g, _ in groups)
    per_group = [sum(1 for g, _ in groups if g == gi) for gi in range(ng)]
    first_of = [next(a for a in range(n) if groups[a][0] == gi) for gi in range(ng)]

    def body(*refs):
        ins, outs = refs[:n], refs[n:n + ng]
        send_sems, recv_sems, local_sems = refs[n + ng:]
        x, y, c = _mesh_pos()
        me, sib = (x, y, c), (x, y, 1 - c)
        chips = [(1 - x, y), (x, 1 - y), (1 - x, 1 - y)]

        def dst_of(a, px, py, pc):
            g, k = groups[a]
            return outs[g].at[N_DEV * k + 4 * px + 2 * py + pc]

        def copy(a, k, block, to, src=None):
            dst = dst_of(a, *block)
            return pltpu.make_async_remote_copy(
                src_ref=dst if src is None else src, dst_ref=dst,
                send_sem=send_sems.at[a, k], recv_sem=recv_sems.at[a, k], device_id=to, device_id_type=MESH)

        mine = [pltpu.make_async_copy(ins[a], dst_of(a, *me), local_sems.at[a]) for a in range(n)]
        for cp in mine:
            cp.start()
        first = []
        for a in range(n):
            first.append(copy(a, 0, me, sib, src=ins[a]))
            first += [copy(a, 1 + j, me, (*chip, c), src=ins[a]) for j, chip in enumerate(chips)]
        for cp in first:
            cp.start()
        passed = []
        for j, chip in enumerate(chips):
            for a in range(n):
                copy(a, 1 + j, (*chip, c), me).wait_recv()
                fwd = copy(a, 4 + j, (*chip, c), sib)
                fwd.start()
                passed.append(fwd)
        for a in range(n):
            copy(a, 0, sib, me).wait_recv()
            for j, chip in enumerate(chips):
                copy(a, 4 + j, (*chip, 1 - c), me).wait_recv()
        for cp in first + passed:
            cp.wait_send()
        for cp in mine:
            cp.wait()

    any_spec = pl.BlockSpec(memory_space=pl.ANY)
    return pl.pallas_call(
        body, name=name,
        in_specs=[any_spec] * n, out_specs=[any_spec] * ng,
        out_shape=[jax.ShapeDtypeStruct((N_DEV * per_group[gi],) + arrs[first_of[gi]].shape, arrs[first_of[gi]].dtype)
                   for gi in range(ng)],
        scratch_shapes=[pltpu.SemaphoreType.DMA((n, 7)), pltpu.SemaphoreType.DMA((n, 7)),
                        pltpu.SemaphoreType.DMA((n,))],
    )(*arrs)


def _pair_exchange(gs, name):
    n = len(gs)

    def body(*refs):
        g_refs, o_refs = refs[:n], refs[n:2 * n]
        send_sems, recv_sems = refs[2 * n:]
        x, y, c = _mesh_pos()
        sib = (x, y, 1 - c)
        copies = []
        for a in range(n):
            for j in range(4):
                copies.append(pltpu.make_async_remote_copy(
                    src_ref=g_refs[a].at[2 * j + (1 - c)], dst_ref=o_refs[a].at[j],
                    send_sem=send_sems.at[a, j], recv_sem=recv_sems.at[a, j], device_id=sib, device_id_type=MESH))
        for cp in copies:
            cp.start()
        for cp in copies:
            cp.wait_recv()
        for cp in copies:
            cp.wait_send()

    any_spec = pl.BlockSpec(memory_space=pl.ANY)
    return pl.pallas_call(
        body, name=name, in_specs=[any_spec] * n, out_specs=[any_spec] * n,
        out_shape=[jax.ShapeDtypeStruct((4,) + g.shape[1:], g.dtype) for g in gs],
        scratch_shapes=[pltpu.SemaphoreType.DMA((n, 4)), pltpu.SemaphoreType.DMA((n, 4))],
    )(*gs)


def _chip_exchange(parts, name):
    n = len(parts)

    def body(*refs):
        p_refs, o_refs = refs[:n], refs[n:2 * n]
        send_sems, recv_sems = refs[2 * n:]
        x, y, c = _mesh_pos()
        chips = [(1 - x, y), (x, 1 - y), (1 - x, 1 - y)]
        copies = []
        for a in range(n):
            for k, (px, py) in enumerate(chips):
                copies.append(pltpu.make_async_remote_copy(
                    src_ref=p_refs[a].at[2 * px + py], dst_ref=o_refs[a].at[k],
                    send_sem=send_sems.at[a, k], recv_sem=recv_sems.at[a, k], device_id=(px, py, c),
                    device_id_type=MESH))
        for cp in copies:
            cp.start()
        for cp in copies:
            cp.wait_recv()
        for cp in copies:
            cp.wait_send()

    any_spec = pl.BlockSpec(memory_space=pl.ANY)
    return pl.pallas_call(
        body, name=name, in_specs=[any_spec] * n, out_specs=[any_spec] * n,
        out_shape=[jax.ShapeDtypeStruct((3,) + p.shape[1:], p.dtype) for p in parts],
        scratch_shapes=[pltpu.SemaphoreType.DMA((n, 3)), pltpu.SemaphoreType.DMA((n, 3))],
    )(*parts)


HBM_SPEC = pl.BlockSpec(memory_space=pltpu.HBM)
SEM_SPEC = pl.BlockSpec(memory_space=pltpu.SEMAPHORE)
ANY_SPEC = pl.BlockSpec(memory_space=pl.ANY)
DATAFLOW_EFFECT = pltpu.SideEffectType.DATAFLOW_SIDE_EFFECTING


def _peers():
    x, y, c = _mesh_pos()
    flip = lambda v, b: 1 - v if b else v
    return [(flip(x, (k >> 2) & 1), flip(y, (k >> 1) & 1), flip(c, k & 1)) for k in range(1, N_DEV)]


def _slot(p):
    return 4 * p[0] + 2 * p[1] + p[2]


def _direct_copy(src_refs, land_refs, sems, a, k, p, land_of, dst_slot, src_slot):
    s = src_slot(a, p)
    return pltpu.make_async_remote_copy(
        src_ref=src_refs[a] if s is None else src_refs[a].at[s], dst_ref=land_refs[land_of[a]].at[dst_slot(a, k)],
        send_sem=sems[0].at[a * (N_DEV - 1) + k], recv_sem=sems[1].at[a * (N_DEV - 1) + k], device_id=p,
        device_id_type=MESH)


def _direct_start(srcs, lands, land_of, dst_slot, src_slot, after, name, collective_id):
    n, nl = len(srcs), len(lands)

    def body(*refs):
        src_refs, land_refs = refs[:n], refs[n:n + nl]
        sems = (refs[n + nl + 1], refs[n + nl + 2])
        token = refs[-1]
        peers = _peers()
        barrier = pltpu.get_barrier_semaphore()
        for p in peers:
            pl.semaphore_signal(barrier, inc=1, device_id=p, device_id_type=MESH)
        pl.semaphore_wait(barrier, N_DEV - 1)
        for a in range(n):
            for k, p in enumerate(peers):
                _direct_copy(src_refs, land_refs, sems, a, k, p, land_of, dst_slot, src_slot).start()
        token[...] = jnp.zeros_like(token)

    hbm = lambda t: pltpu.HBM(t.shape, t.dtype)
    sem_t = pltpu.SemaphoreType.DMA((n * (N_DEV - 1),))
    outs = pl.pallas_call(
        body, name=name,
        out_shape=(sem_t, sem_t, *[hbm(t) for t in srcs], *[hbm(t) for t in lands], jax.ShapeDtypeStruct((8, LANES), F32)),
        in_specs=[HBM_SPEC] * (n + nl) + [ANY_SPEC],
        out_specs=(SEM_SPEC, SEM_SPEC, *([HBM_SPEC] * (n + nl)), pl.BlockSpec(memory_space=pltpu.VMEM)),
        input_output_aliases={i: 2 + i for i in range(n + nl)},
        compiler_params=pltpu.CompilerParams(has_side_effects=DATAFLOW_EFFECT, collective_id=collective_id),
    )(*[pltpu.with_memory_space_constraint(t, pltpu.HBM) for t in srcs],
      *[pltpu.with_memory_space_constraint(t, pltpu.HBM) for t in lands], after)
    return outs[0], outs[1], list(outs[2:2 + n]), list(outs[2 + n:2 + n + nl]), outs[-1]


def _direct_wait(send_sems, recv_sems, srcs, lands, land_of, idxs, dst_slot, src_slot, after, name):
    land_ids = []
    for a in idxs:
        if land_of[a] not in land_ids:
            land_ids.append(land_of[a])
    m, ml = len(idxs), len(land_ids)
    sub_land_of = {j: land_ids.index(land_of[a]) for j, a in enumerate(idxs)}

    def body(*refs):
        src_refs, land_refs = refs[:m], refs[m:m + ml]
        ssem, rsem = refs[m + ml], refs[m + ml + 1]
        for j, a in enumerate(idxs):
            for k, p in enumerate(_peers()):
                s = src_slot(a, p)
                cp = pltpu.make_async_remote_copy(
                    src_ref=src_refs[j] if s is None else src_refs[j].at[s],
                    dst_ref=land_refs[sub_land_of[j]].at[dst_slot(a, k)],
                    send_sem=ssem.at[a * (N_DEV - 1) + k], recv_sem=rsem.at[a * (N_DEV - 1) + k], device_id=p,
                    device_id_type=MESH)
                cp.wait_send()
                cp.wait_recv()

    hbm = lambda t: pltpu.HBM(t.shape, t.dtype)
    sub_s, sub_l = [srcs[a] for a in idxs], [lands[g] for g in land_ids]
    outs = pl.pallas_call(
        body, name=name,
        out_shape=(*[hbm(t) for t in sub_s], *[hbm(t) for t in sub_l]),
        in_specs=[HBM_SPEC] * (m + ml) + [SEM_SPEC, SEM_SPEC, ANY_SPEC],
        out_specs=tuple([HBM_SPEC] * (m + ml)),
        input_output_aliases={i: i for i in range(m + ml)},
        compiler_params=pltpu.CompilerParams(has_side_effects=DATAFLOW_EFFECT),
    )(*sub_s, *sub_l, send_sems, recv_sems, after)
    return list(outs[:m]), list(outs[m:])


def _row_block(R, C):
    best = None
    for d in range(16, R + 1, 16):
        if R % d == 0 and d * C <= 256 * 1024:
            best = d
    return best if best is not None else R


def _pair_add(g, recv, cidx, name):
    _, R, C = g.shape
    tr = _row_block(R, C)

    def body(c_ref, g_ref, r_ref, o_ref):
        del c_ref
        o_ref[...] = (g_ref[...].astype(F32) + r_ref[...].astype(F32)).astype(BF16)

    grid_spec = pltpu.PrefetchScalarGridSpec(
        num_scalar_prefetch=1, grid=(4, R // tr),
        in_specs=[pl.BlockSpec((None, tr, C), lambda j, i, c: (2 * j + c[0], i, 0)),
                  pl.BlockSpec((None, tr, C), lambda j, i, c: (j, i, 0))],
        out_specs=pl.BlockSpec((None, tr, C), lambda j, i, c: (j, i, 0)))
    return pl.pallas_call(
        body, name=name, grid_spec=grid_spec,
        out_shape=jax.ShapeDtypeStruct((4, R, C), BF16),
        compiler_params=_cp(("parallel", "parallel")),
    )(cidx, g, recv)


def _adamw_math(w, g, m, v):
    m = ADAM_B1 * m + (1.0 - ADAM_B1) * g
    v = ADAM_B2 * v + (1.0 - ADAM_B2) * (g * g)
    m_hat = m / (1.0 - ADAM_B1 ** ADAM_STEP)
    v_hat = v / (1.0 - ADAM_B2 ** ADAM_STEP)
    delta = -ADAM_LR * (m_hat / (jnp.sqrt(v_hat) + ADAM_EPS) + ADAM_WD * w)
    return delta, m, v


def _sum_adamw(parts, w, m, v, name, sel=None):
    R, C = w.shape
    tr = _row_block(R, C)
    specs, args = [], []
    for arr, idxs in parts:
        for idx in idxs:
            if idx < 0:
                specs.append(pl.BlockSpec((None, tr, C), lambda i, s: (s[0], i, 0)))
            else:
                specs.append(pl.BlockSpec((None, tr, C), lambda i, s, idx=idx: (idx, i, 0)))
            args.append(arr)
    npart = len(args)
    blk = pl.BlockSpec((tr, C), lambda i, s: (i, 0))

    def body(s_ref, *refs):
        del s_ref
        g = refs[0][...].astype(F32)
        for r in refs[1:npart]:
            g = g + r[...].astype(F32)
        w_ref, m_ref, v_ref, g_out, d_out, m_out, v_out = refs[npart:]
        delta, mm, vv = _adamw_math(w_ref[...], g, m_ref[...], v_ref[...])
        g_out[...] = g
        d_out[...] = delta
        m_out[...] = mm
        v_out[...] = vv

    grid_spec = pltpu.PrefetchScalarGridSpec(
        num_scalar_prefetch=1, grid=(R // tr,),
        in_specs=specs + [blk, blk, blk], out_specs=[blk] * 4)
    if sel is None:
        sel = jnp.zeros((1,), jnp.int32)
    return pl.pallas_call(
        body, name=name, grid_spec=grid_spec,
        out_shape=[jax.ShapeDtypeStruct((R, C), F32)] * 4,
        compiler_params=_cp(("parallel",)),
    )(sel, *args, w, m, v)


def _rows(a, c):
    return a.reshape(-1, c)


def _pad_rows(a, r):
    return jnp.pad(a, ((0, r - a.shape[0]), (0, 0))) if a.shape[0] != r else a


def _gate_tables():
    hp = N_HEADS // 2
    sel_q = np.zeros((hp, 3 * LANES, 2 * LANES), np.float32)
    sel_k = np.zeros((hp, 3 * LANES, 2 * LANES), np.float32)
    const_q = np.zeros((hp, 1, 2 * LANES), np.float32)
    const_k = np.zeros((hp, 1, 2 * LANES), np.float32)
    for p in range(hp):
        for hh in range(2):
            h = 2 * p + hh
            base = hh * LANES + HEAD_DIM
            for piece in range(3):
                sel_q[p, piece * LANES + h, base + piece] = 1.0
                sel_k[p, piece * LANES + h, base + 3 + piece] = -1.0
            const_k[p, 0, base:base + 3] = 1.0
            const_q[p, 0, base + 3:base + 6] = 1.0
    as_bf = lambda t: jnp.asarray(t, BF16)
    return as_bf(sel_q), as_bf(sel_k), jnp.asarray(const_q), jnp.asarray(const_k)


def _pad_heads(w):
    d = w.shape[0]
    w3 = w.reshape(d, N_HEADS, HEAD_DIM)
    return jnp.pad(w3, ((0, 0), (0, 0), (0, LANES - HEAD_DIM))).reshape(d, N_HEADS * LANES)


def kernel(x, mix_norm_g, ffn_norm_g, gm_w_in, gm_ln_g, gm_ln_b, gm_w_s, gm_b_s, gm_w_out, fox_w_qkvf, fox_b_f, fox_w_o, ffn_w_gate, ffn_w_up, ffn_conv_w, ffn_conv_b, ffn_w_down, final_norm_g, loss_target, m_mix_norm_g, m_ffn_norm_g, m_gm_w_in, m_gm_ln_g, m_gm_ln_b, m_gm_w_s, m_gm_b_s, m_gm_w_out, m_fox_w_qkvf, m_fox_b_f, m_fox_w_o, m_ffn_w_gate, m_ffn_w_up, m_ffn_conv_w, m_ffn_conv_b, m_ffn_w_down, m_final_norm_g, v_mix_norm_g, v_ffn_norm_g, v_gm_w_in, v_gm_ln_g, v_gm_ln_b, v_gm_w_s, v_gm_b_s, v_gm_w_out, v_fox_w_qkvf, v_fox_b_f, v_fox_w_o, v_ffn_w_gate, v_ffn_w_up, v_ffn_conv_w, v_ffn_conv_b, v_ffn_w_down, v_final_norm_g):
    T, D = x.shape[1], x.shape[2]
    E = gm_ln_g.shape[1]
    FF = ffn_conv_b.shape[1]
    NQKVF = 3 * D + N_HEADS
    xi, yi, ci = _mesh_pos()
    me = 4 * xi + 2 * yi + ci
    h0 = x.reshape(T, D)
    tgt = loss_target.reshape(T, D)

    nl = ffn_w_gate.shape[0]
    to16 = lambda a: a.astype(BF16)
    n_cw_rows = ffn_conv_w.size // LANES
    cw_rows = _pad_rows(_rows(ffn_conv_w.astype(F32), LANES), 16)
    w_in_g, w_out_g8, cwg = _all_gather([to16(gm_w_in[0]), to16(gm_w_out[0]), cw_rows], "ag_weights")
    w_out_g = w_out_g8.reshape(E, D)
    later, land_of, land_off, lands = [], [], [], []
    for l in range(nl):
        later += [to16(ffn_w_gate[l]), to16(ffn_w_up[l]), to16(ffn_w_down[l])]
        land_of += [2 * l, 2 * l, 2 * l + 1]
        land_off += [0, N_DEV, 0]
        lands += [lax.empty((2 * N_DEV, D, FF // N_DEV), BF16), lax.empty((N_DEV, FF // N_DEV, D), BF16)]
    later += [to16(fox_w_qkvf[0]), to16(fox_w_o[0])]
    land_of += [2 * nl, 2 * nl + 1]
    land_off += [0, 0]
    lands += [lax.empty((N_DEV, D, NQKVF // N_DEV), BF16), lax.empty((N_DEV, D // N_DEV, D), BF16)]
    ag_dst = lambda a, k: land_off[a] + _slot(_mesh_pos())
    ag_src = lambda a, p: None
    ag_send, ag_recv, later, lands, ag_token = _direct_start(later, lands, land_of, ag_dst, ag_src, w_in_g,
                                                             "ag_later_start", collective_id=1)

    def own_blocks(landed, shards, offs):
        for s, o in zip(shards, offs):
            landed = lax.dynamic_update_index_in_dim(landed, s, o + me, 0)
        return landed

    def gather_wait(idxs, after, name):
        return _direct_wait(ag_send, ag_recv, later, lands, land_of, idxs, ag_dst, ag_src, after, name)

    conv_w_full = jnp.transpose(cwg[:, :n_cw_rows].reshape(N_DEV, nl, 3, FF // N_DEV), (1, 2, 0, 3)).reshape(nl, 3, FF)

    ffn_w = {}

    def ffn_weights(l):
        return ffn_w[l]

    def land_ffn(l, shards, gu_land, dn_land):
        ffn_w[l] = (own_blocks(gu_land, shards[:2], [0, N_DEV]), own_blocks(dn_land, shards[2:3], [0]).reshape(FF, D))

    saved = {}

    def ffn_fwd(l, h_in):
        wgul, wdl = ffn_weights(l)
        hn = _rms_fwd(h_in, ffn_norm_g[l:l + 1], f"ffn{l}_norm")
        au = _mm_nn(hn, wgul, f"ffn{l}_gate_up")
        act = _ffn_mid_fwd(au, conv_w_full[l], ffn_conv_b[l:l + 1], f"ffn{l}_mid")
        h_out = _mm_nn(act, wdl, f"ffn{l}_down", res=h_in)
        saved[f"ffn{l}"] = (h_in, hn, au, act)
        return h_out

    bs_col = gm_b_s[0].reshape(GM_GROUPS, CHUNK, 1)
    hn0 = _rms_fwd(h0, mix_norm_g[0:1], "mix0_norm", after=ag_token)
    z = _mm_nn(hn0, w_in_g, "gm_in")
    gu = _sgu_fwd(z, gm_ln_g, gm_ln_b, gm_w_s[0], bs_col, "gm_sgu")
    h1 = _mm_nn(gu, w_out_g, "gm_out", res=h0)
    mine0, land0 = gather_wait([0, 1, 2], h1, "ag_ffn0_wait")
    land_ffn(0, mine0, *land0)
    h2 = ffn_fwd(0, h1)

    mine1, rest = gather_wait(list(range(3, 3 * nl + 2)), h2, "ag_layer1_wait")
    for l in range(1, nl):
        land_ffn(l, mine1[3 * (l - 1):3 * l], rest[2 * (l - 1)], rest[2 * (l - 1) + 1])
    w_qkvf = jnp.transpose(own_blocks(rest[-2], mine1[-2:-1], [0]), (1, 0, 2)).reshape(D, NQKVF)
    w_o_g = own_blocks(rest[-1], mine1[-1:], [0]).reshape(D, D)
    w_q, w_k, w_v = w_qkvf[:, :D], w_qkvf[:, D:2 * D], w_qkvf[:, 2 * D:3 * D]
    w_f = jnp.pad(w_qkvf[:, 3 * D:], ((0, 0), (0, LANES - N_HEADS)))
    bf_row = jnp.pad(fox_b_f, ((0, 0), (0, LANES - N_HEADS)))
    sel_q, sel_k, const_q, const_k = _gate_tables()
    scale = HEAD_DIM ** -0.5
    hn2 = _rms_fwd(h2, mix_norm_g[1:2], "mix1_norm")
    f_logit = _mm_nn(hn2, w_f, "fox_f")
    cp, sneg = _gate_scan(f_logit, bf_row, "fox_scan")
    qp = _qk_proj(hn2, _pad_heads(w_q), cp, sel_q, const_q, scale * LOG2E, "fox_q")
    kp = _qk_proj(hn2, _pad_heads(w_k), cp, sel_k, const_k, 1.0, "fox_k")
    vv = _mm_nn(hn2, w_v, "fox_v", out_dtype=BF16)
    o, o32, lse = _attn_fwd_t(qp, kp, jnp.transpose(vv), "fox_attn")
    h3 = _mm_nn(o, w_o_g, "fox_o", res=h2)
    h4 = ffn_fwd(1, h3)

    dh, dh16, d_final, loss_row = _loss_head(h4, tgt, final_norm_g.reshape(1, D), "loss_head")
    loss = lax.psum(loss_row[0, 0], ("x", "y", "c"))

    rs_dst = lambda a, k: k
    rs_src = lambda a, p: _slot(p)
    me_idx = me.astype(jnp.int32).reshape(1)

    def rs_start(grads, name, cid):
        lands = [lax.empty((N_DEV - 1,) + g.shape[1:], BF16) for g in grads]
        return _direct_start(grads, lands, list(range(len(grads))), rs_dst, rs_src, loss_row, name, collective_id=cid)

    def rs_wait(st, after, name):
        n = len(st[2])
        return _direct_wait(st[0], st[1], st[2], st[3], list(range(n)), list(range(n)), rs_dst, rs_src, after, name)

    def ffn_bwd(l, dh, dh16, after=None):
        wgul, wdl = ffn_weights(l)
        h_in, hn, au, act = saved[f"ffn{l}"]
        dact = _mm_nt([dh16], wdl, f"ffn{l}_dact", out_dtype=BF16, after=after)
        d_wd = _mm_tn(act, dh16, f"ffn{l}_dwd", out_dtype=BF16)
        da, dup, d_cw, d_cb = _ffn_mid_bwd(au, dact, conv_w_full[l], ffn_conv_b[l:l + 1], f"ffn{l}_dmid")
        dhn = _mm_nt([da, dup], wgul, f"ffn{l}_dhn")
        d_wg = _mm_tn(hn, da, f"ffn{l}_dwg", blocked_w=FF // N_DEV, out_dtype=BF16)
        d_wu = _mm_tn(hn, dup, f"ffn{l}_dwu", blocked_w=FF // N_DEV, out_dtype=BF16)
        dh_in, dh_in16, d_norm = _rms_bwd(dhn, h_in, ffn_norm_g[l:l + 1], dh, f"ffn{l}_dnorm")
        big_g = [d_wg, d_wu, d_wd.reshape(N_DEV, FF // N_DEV, D)]
        return dh_in, dh_in16, big_g, dict(cw=d_cw, cb=d_cb, norm=d_norm)

    dh, dh16, big_ffn1, g_ffn1 = ffn_bwd(1, dh, dh16)

    do = _mm_nt([dh16], w_o_g, "fox_do", out_dtype=BF16)
    d_wo = _mm_tn(o, dh16, "fox_dwo", out_dtype=BF16)
    dq, dk, dv, dqe, dke = _attn_bwd(qp, kp, vv, o32, do, lse, scale, "fox_dattn")
    gate_lane = lambda e, r: jnp.pad(jnp.transpose(e[:, r::8, :].reshape(N_HEADS, T)), ((0, 0), (0, LANES - N_HEADS)))
    df, d_bf = _gate_scan_bwd(gate_lane(dqe, 0), gate_lane(dke, 3), sneg, "fox_dscan")
    dhn = _mm_nt([df], w_f, "fox_dhn_f")
    dhn = _mm_nt([dq, dk, dv], w_qkvf[:, :3 * D], "fox_dhn_qkv", add=dhn)
    d_wq = _mm_tn(hn2, dq, "fox_dwq", out_dtype=BF16)
    d_wk = _mm_tn(hn2, dk, "fox_dwk", out_dtype=BF16)
    d_wv = _mm_tn(hn2, dv, "fox_dwv", out_dtype=BF16)
    d_wf = _mm_tn(hn2, df, "fox_dwf", out_dtype=BF16)
    d_wqkvf = jnp.concatenate([d_wq, d_wk, d_wv, d_wf[:, :N_HEADS]], axis=1)
    dh, dh16, d_mix1 = _rms_bwd(dhn, h2, mix_norm_g[1:2], dh, "mix1_dnorm")
    st1 = rs_start([jnp.transpose(d_wqkvf.reshape(D, N_DEV, NQKVF // N_DEV), (1, 0, 2)),
                    d_wo.reshape(N_DEV, D // N_DEV, D)] + big_ffn1, "rs1_start", 2)

    dh, dh16, big_ffn0, g_ffn0 = ffn_bwd(0, dh, dh16, after=st1[4])
    st2 = rs_start(big_ffn0, "rs2_start", 3)

    dgu = _mm_nt([dh16], w_out_g, "gm_dgu", out_dtype=BF16, after=st2[4])
    d_wout = _mm_tn(gu, dh16, "gm_dwout", out_dtype=BF16)
    dz, d_lng, d_lnb, d_ws, d_bs = _sgu_bwd(z, dgu, gm_ln_g, gm_ln_b, gm_w_s[0], bs_col, "gm_dsgu")
    d_win = _mm_tn(hn0, dz, "gm_dwin", blocked_w=2 * E // N_DEV, out_dtype=BF16)
    st3 = rs_start([d_win, d_wout.reshape(N_DEV, E // N_DEV, D)], "rs3_start", 4)
    dhn = _mm_nt([dz], w_in_g, "gm_dhn", after=st3[4])
    dx, _, d_mix0 = _rms_bwd(dhn, h0, mix_norm_g[0:1], dh, "mix0_dnorm")

    own1, land1 = rs_wait(st1, dx, "rs1_wait")
    own2, land2 = rs_wait(st2, land1[0], "rs2_wait")
    cat1 = lambda a, b: jnp.concatenate([a, b], axis=1)
    big_out = {}

    def big_adamw(name, w, m, v, own, landed):
        shard2d = lambda a, c=own.shape[2]: a.reshape(-1, c)
        res = _sum_adamw([(own, [-1]), (landed, list(range(N_DEV - 1)))], shard2d(w), shard2d(m), shard2d(v),
                         f"adamw_{name}", sel=me_idx)
        big_out[name] = [t.reshape(w.shape) for t in res]

    big_adamw("fox_w_qkvf", fox_w_qkvf, m_fox_w_qkvf, v_fox_w_qkvf, own1[0], land1[0])
    big_adamw("fox_w_o", fox_w_o, m_fox_w_o, v_fox_w_o, own1[1], land1[1])
    big_adamw("ffn_w_gate", ffn_w_gate, m_ffn_w_gate, v_ffn_w_gate, cat1(own2[0], own1[2]), cat1(land2[0], land1[2]))
    big_adamw("ffn_w_up", ffn_w_up, m_ffn_w_up, v_ffn_w_up, cat1(own2[1], own1[3]), cat1(land2[1], land1[3]))
    big_adamw("ffn_w_down", ffn_w_down, m_ffn_w_down, v_ffn_w_down, cat1(own2[2], own1[4]), cat1(land2[2], land1[4]))

    small = [("mix_norm_g", mix_norm_g, m_mix_norm_g, v_mix_norm_g, jnp.concatenate([d_mix0, d_mix1], axis=0)),
             ("ffn_norm_g", ffn_norm_g, m_ffn_norm_g, v_ffn_norm_g, jnp.concatenate([g_ffn0["norm"], g_ffn1["norm"]], axis=0)),
             ("gm_ln_g", gm_ln_g, m_gm_ln_g, v_gm_ln_g, d_lng),
             ("gm_ln_b", gm_ln_b, m_gm_ln_b, v_gm_ln_b, d_lnb),
             ("gm_w_s", gm_w_s, m_gm_w_s, v_gm_w_s, d_ws),
             ("gm_b_s", gm_b_s, m_gm_b_s, v_gm_b_s, d_bs),
             ("fox_b_f", fox_b_f, m_fox_b_f, v_fox_b_f, d_bf[:, :N_HEADS]),
             ("ffn_conv_b", ffn_conv_b, m_ffn_conv_b, v_ffn_conv_b, jnp.concatenate([g_ffn0["cb"], g_ffn1["cb"]], axis=0)),
             ("final_norm_g", final_norm_g, m_final_norm_g, v_final_norm_g, d_final)]
    d_cw_full = jnp.stack([g_ffn0["cw"], g_ffn1["cw"]], axis=0)

    def small_rows(a):
        flat = a.astype(F32).reshape(-1)
        n = -(-flat.size // (8 * LANES)) * (8 * LANES)
        return jnp.pad(flat, (0, n - flat.size)).reshape(-1, LANES)

    s_rows = [small_rows(p[1]).shape[0] for p in small]
    s_off = np.concatenate([[0], np.cumsum(s_rows)]).tolist()
    cw_g_rows = small_rows(d_cw_full)
    zeros_cw = jnp.zeros_like(cw_g_rows)
    cat = lambda k: jnp.concatenate([small_rows(p[k]) for p in small] + [zeros_cw], axis=0)
    g_small = jnp.concatenate([small_rows(p[4]) for p in small] + [cw_g_rows], axis=0)
    (gs_all,) = _all_gather([g_small], "ag_small_grads")
    small_out = _sum_adamw([(gs_all, list(range(N_DEV)))], cat(1), cat(2), cat(3), "adamw_small")
    gs = small_out[0]

    g_cw_full = gs[s_off[-1]:].reshape(-1)[:d_cw_full.size].reshape(d_cw_full.shape)
    g_cw = lax.dynamic_slice_in_dim(g_cw_full, me * (FF // N_DEV), FF // N_DEV, axis=2)
    cw2 = lambda a: _pad_rows(_rows(a.astype(F32), LANES), 16)
    cw_out = _sum_adamw([(cw2(g_cw)[None], [0])], cw2(ffn_conv_w), cw2(m_ffn_conv_w), cw2(v_ffn_conv_w), "adamw_conv_w")

    own3, land3 = rs_wait(st3, cw_out[0], "rs3_wait")
    big_adamw("gm_w_in", gm_w_in, m_gm_w_in, v_gm_w_in, own3[0], land3[0])
    big_adamw("gm_w_out", gm_w_out, m_gm_w_out, v_gm_w_out, own3[1], land3[1])

    names = ["mix_norm_g", "ffn_norm_g", "gm_w_in", "gm_ln_g", "gm_ln_b", "gm_w_s", "gm_b_s", "gm_w_out", "fox_w_qkvf",
             "fox_b_f", "fox_w_o", "ffn_w_gate", "ffn_w_up", "ffn_conv_w", "ffn_conv_b", "ffn_w_down", "final_norm_g"]
    small_idx = {p[0]: k for k, p in enumerate(small)}

    def pick(kind, name):
        if name in big_out:
            return big_out[name][kind]
        if name == "ffn_conv_w":
            return cw_out[kind][:n_cw_rows].reshape(ffn_conv_w.shape)
        k = small_idx[name]
        shp = small[k][1].shape
        return small_out[kind][s_off[k]:s_off[k + 1]].reshape(-1)[:int(np.prod(shp))].reshape(shp)

    outs = [loss, dx.reshape(x.shape)]
    for kind in range(4):
        outs += [pick(kind, n) for n in names]
    return tuple(outs)
```

```python
import functools
import math

import numpy as np
import jax
import jax.numpy as jnp
from jax import lax
from jax.experimental import pallas as pl
from jax.experimental.pallas import tpu as pltpu

F32 = jnp.float32
BF16 = jnp.bfloat16
MESH = pl.DeviceIdType.MESH

N_HEADS = 16
HEAD_DIM = 64
CHUNK = 128
GM_GROUPS = 8
RMS_EPS = 1e-6
LN_EPS = 1e-5
ADAM_LR = 0.001
ADAM_B1 = 0.9
ADAM_B2 = 0.999
ADAM_EPS = 1e-08
ADAM_WD = 0.01
ADAM_STEP = 10
N_DEV = 8

LANES = 128
VMEM_BYTES_V7X = 64 * 1024 * 1024
VMEM_LIMIT = 56 * 1024 * 1024

TM = 512
TM_MM = 1024
TT = 1024
TQ = 512
TF = 512
KV_UNROLL_BWD = 2
KV_UNROLL = 2
MM_BLOCK_BYTES = 8 * 1024 * 1024
NEG = -1e30
LOG2E = math.log2(math.e)
LN2 = math.log(2.0)


def _cp(sem=None, vmem=VMEM_LIMIT):
    return pltpu.CompilerParams(dimension_semantics=sem, vmem_limit_bytes=vmem)


def _gelu(x):
    c = math.sqrt(2.0 / math.pi)
    return x * (0.5 * (1.0 + jnp.tanh(c * (x + 0.044715 * (x * x * x)))))


def _gelu_grad(x):
    c = math.sqrt(2.0 / math.pi)
    t = jnp.tanh(c * (x + 0.044715 * (x * x * x)))
    return 0.5 * (1.0 + t) + x * (0.5 * (1.0 - t * t)) * (c * (1.0 + 3.0 * 0.044715 * (x * x)))


def _sigmoid(x):
    return 1.0 / (1.0 + jnp.exp(-x))


def _dot_nt(a, b):
    return lax.dot_general(a, b, (((1,), (1,)), ((), ())), preferred_element_type=F32)


def _dot_tn(a, b):
    return lax.dot_general(a, b, (((0,), (0,)), ((), ())), preferred_element_type=F32)


def _rms_fwd(h, g, name, after=None):
    T, D = h.shape
    tm = min(TM, T)

    def body(h_ref, g_ref, *rest):
        o_ref = rest[-1]
        x = h_ref[...]
        r = lax.rsqrt(jnp.mean(x * x, axis=-1, keepdims=True) + RMS_EPS)
        o_ref[...] = ((x * r) * g_ref[...]).astype(BF16)

    in_specs = [pl.BlockSpec((tm, D), lambda i: (i, 0)), pl.BlockSpec((1, D), lambda i: (0, 0))]
    args = [h, g]
    if after is not None:
        in_specs.append(pl.BlockSpec(memory_space=pl.ANY))
        args.append(after)
    return pl.pallas_call(
        body, name=name, grid=(T // tm,),
        in_specs=in_specs,
        out_specs=pl.BlockSpec((tm, D), lambda i: (i, 0)),
        out_shape=jax.ShapeDtypeStruct((T, D), BF16),
        compiler_params=_cp(("parallel",)),
    )(*args)


def _rms_bwd(dhn, h, g, dres, name):
    T, D = h.shape
    tm = min(TM, T)

    def body(d_ref, h_ref, g_ref, r_ref, o_ref, ob_ref, dg_ref):
        x = h_ref[...]
        d = d_ref[...]
        r = lax.rsqrt(jnp.mean(x * x, axis=-1, keepdims=True) + RMS_EPS)
        dyg = d * g_ref[...]
        dot = jnp.mean(dyg * x, axis=-1, keepdims=True)
        dh = r_ref[...] + (r * dyg - x * ((r * r * r) * dot))
        o_ref[...] = dh
        ob_ref[...] = dh.astype(BF16)
        part = jnp.sum(d * (x * r), axis=0, keepdims=True)

        @pl.when(pl.program_id(0) == 0)
        def _():
            dg_ref[...] = part

        @pl.when(pl.program_id(0) != 0)
        def _():
            dg_ref[...] += part

    blk = pl.BlockSpec((tm, D), lambda i: (i, 0))
    row = pl.BlockSpec((1, D), lambda i: (0, 0))
    return pl.pallas_call(
        body, name=name, grid=(T // tm,),
        in_specs=[blk, blk, row, blk],
        out_specs=[blk, blk, row],
        out_shape=[jax.ShapeDtypeStruct((T, D), F32), jax.ShapeDtypeStruct((T, D), BF16),
                   jax.ShapeDtypeStruct((1, D), F32)],
        compiler_params=_cp(("arbitrary",)),
    )(dhn, h, g, dres)


def _loss_head(h, tgt, g, name):
    T, D = h.shape
    tm = min(TM, T)

    def body(h_ref, t_ref, g_ref, o_ref, ob_ref, dg_ref, l_ref):
        x = h_ref[...]
        gg = g_ref[...]
        r = lax.rsqrt(jnp.mean(x * x, axis=-1, keepdims=True) + RMS_EPS)
        xr = x * r
        e = xr * gg - t_ref[...]
        lpart = 0.5 * jnp.sum(jnp.mean(e * e, axis=-1, keepdims=True), axis=0, keepdims=True)
        dy = e * (1.0 / D)
        dyg = dy * gg
        dot = jnp.mean(dyg * x, axis=-1, keepdims=True)
        dh = r * dyg - x * ((r * r * r) * dot)
        o_ref[...] = dh
        ob_ref[...] = dh.astype(BF16)
        part = jnp.sum(dy * xr, axis=0, keepdims=True)
        lrow = jnp.broadcast_to(lpart, (1, LANES))

        @pl.when(pl.program_id(0) == 0)
        def _():
            dg_ref[...] = part
            l_ref[...] = lrow

        @pl.when(pl.program_id(0) != 0)
        def _():
            dg_ref[...] += part
            l_ref[...] += lrow

    blk = pl.BlockSpec((tm, D), lambda i: (i, 0))
    row = pl.BlockSpec((1, D), lambda i: (0, 0))
    return pl.pallas_call(
        body, name=name, grid=(T // tm,),
        in_specs=[blk, blk, row],
        out_specs=[blk, blk, row, pl.BlockSpec((1, LANES), lambda i: (0, 0))],
        out_shape=[jax.ShapeDtypeStruct((T, D), F32), jax.ShapeDtypeStruct((T, D), BF16),
                   jax.ShapeDtypeStruct((1, D), F32), jax.ShapeDtypeStruct((1, LANES), F32)],
        compiler_params=_cp(("arbitrary",)),
    )(h, tgt, g)


def _mm_nn(a, b, name, out_dtype=F32, res=None):
    M, K = a.shape
    b3 = b if b.ndim == 3 else b[None]
    nb, _, w = b3.shape
    N = nb * w
    tm = min(TM_MM, M, max(256, MM_BLOCK_BYTES // (4 * N)))
    o_spec = pl.BlockSpec((tm, N), lambda i: (i, 0))
    in_specs = [pl.BlockSpec((tm, K), lambda i: (i, 0)), pl.BlockSpec((nb, K, w), lambda i: (0, 0, 0))]
    args = [a, b3]
    if res is not None:
        in_specs.append(o_spec)
        args.append(res)

    def body(*refs):
        a_ref, b_ref = refs[0], refs[1]
        o_ref = refs[-1]
        av = a_ref[...]
        for j in range(nb):
            cols = slice(j * w, (j + 1) * w)
            acc = jnp.dot(av, b_ref[j], preferred_element_type=F32)
            if res is not None:
                acc = refs[2][:, cols] + acc
            o_ref[:, cols] = acc.astype(out_dtype)

    return pl.pallas_call(
        body, name=name, grid=(M // tm,),
        in_specs=in_specs, out_specs=o_spec,
        out_shape=jax.ShapeDtypeStruct((M, N), out_dtype),
        compiler_params=_cp(("parallel",)),
    )(*args)


def _mm_nt(a_list, b, name, out_dtype=F32, add=None, after=None):
    M, kw = a_list[0].shape
    tm = min(TM, M)
    na = len(a_list)
    blocked = b.ndim == 3
    N = b.shape[1] if blocked else b.shape[0]
    b_spec = pl.BlockSpec(b.shape, lambda i: (0,) * b.ndim)
    o_spec = pl.BlockSpec((tm, N), lambda i: (i, 0))
    in_specs = [pl.BlockSpec((tm, kw), lambda i: (i, 0)) for _ in a_list] + [b_spec]
    args = list(a_list) + [b]
    if add is not None:
        in_specs.append(o_spec)
        args.append(add)
    if after is not None:
        in_specs.append(pl.BlockSpec(memory_space=pl.ANY))
        args.append(after)

    def body(*refs):
        a_refs = refs[:na]
        b_ref = refs[na]
        o_ref = refs[-1]
        acc = refs[na + 1][...] if add is not None else None
        for s, a_ref in enumerate(a_refs):
            if blocked:
                w = b_ref.shape[2]
                per = kw // w
                parts = [_dot_nt(a_ref[:, jj * w:(jj + 1) * w], b_ref[s * per + jj]) for jj in range(per)]
            else:
                parts = [_dot_nt(a_ref[...], b_ref[:, s * kw:(s + 1) * kw])]
            for part in parts:
                acc = part if acc is None else acc + part
        o_ref[...] = acc.astype(out_dtype)

    return pl.pallas_call(
        body, name=name, grid=(M // tm,),
        in_specs=in_specs, out_specs=o_spec,
        out_shape=jax.ShapeDtypeStruct((M, N), out_dtype),
        compiler_params=_cp(("parallel",)),
    )(*args)


def _mm_tn(x, y, name, blocked_w=None, out_dtype=F32):
    T, Kx = x.shape
    N = y.shape[1]
    tt = min(TT, T)
    nt = T // tt
    tkx = min(Kx, max(LANES, MM_BLOCK_BYTES // (4 * N)))
    if blocked_w is not None:
        blk_shape, full_shape = (N // blocked_w, tkx, blocked_w), (N // blocked_w, Kx, blocked_w)
        o_spec = pl.BlockSpec(blk_shape, lambda i, t: (0, i, 0))
    else:
        blk_shape, full_shape = (tkx, N), (Kx, N)
        o_spec = pl.BlockSpec(blk_shape, lambda i, t: (i, 0))

    def body(x_ref, y_ref, o_ref, acc_ref):
        part = _dot_tn(x_ref[...], y_ref[...])
        t = pl.program_id(1)
        if blocked_w is None:
            pieces = [(slice(None), part)]
        else:
            pieces = [(j, part[:, j * blocked_w:(j + 1) * blocked_w]) for j in range(N // blocked_w)]

        @pl.when(t == 0)
        def _():
            for idx, pj in pieces:
                acc_ref[idx] = pj

        @pl.when(t != 0)
        def _():
            for idx, pj in pieces:
                acc_ref[idx] += pj

        @pl.when(t == nt - 1)
        def _():
            o_ref[...] = acc_ref[...].astype(out_dtype)

    return pl.pallas_call(
        body, name=name, grid=(Kx // tkx, nt),
        in_specs=[pl.BlockSpec((tt, tkx), lambda i, t: (t, i)),
                  pl.BlockSpec((tt, N), lambda i, t: (t, 0))],
        out_specs=o_spec, out_shape=jax.ShapeDtypeStruct(full_shape, out_dtype),
        scratch_shapes=[pltpu.VMEM(blk_shape, F32)],
        compiler_params=_cp(("parallel", "arbitrary")),
    )(x, y)


def _sgu_pieces(z, lng, lnb, wc, bs_ref):
    E = z.shape[1] // 2
    gd = E // GM_GROUPS
    zu, zv = z[:, :E], z[:, E:]
    u = _gelu(zu)
    v = _gelu(zv)
    mu = jnp.mean(v, axis=-1, keepdims=True)
    xc = v - mu
    rs = lax.rsqrt(jnp.mean(xc * xc, axis=-1, keepdims=True) + LN_EPS)
    xhat = xc * rs
    vln = xhat * lng + lnb
    s = []
    for g in range(GM_GROUPS):
        vg = vln[:, g * gd:(g + 1) * gd].astype(BF16)
        s.append(jnp.dot(wc[g], vg, preferred_element_type=F32) + bs_ref[g])
    return zu, zv, u, xhat, rs, vln, s


def _causal_ws(ws_ref):
    t = lax.broadcasted_iota(jnp.int32, (CHUNK, CHUNK), 0)
    s = lax.broadcasted_iota(jnp.int32, (CHUNK, CHUNK), 1)
    tri = t >= s
    return [jnp.where(tri, ws_ref[g], 0.0).astype(BF16) for g in range(GM_GROUPS)], tri


def _sgu_fwd(z, lng, lnb, ws, bs, name):
    T, E2 = z.shape
    E = E2 // 2
    gd = E // GM_GROUPS
    tm = min(2 * CHUNK, T)

    def body(z_ref, lng_ref, lnb_ref, ws_ref, bs_ref, o_ref):
        wc, _ = _causal_ws(ws_ref)
        for c in range(tm // CHUNK):
            rows = slice(c * CHUNK, (c + 1) * CHUNK)
            _, _, u, _, _, _, s = _sgu_pieces(z_ref[rows, :], lng_ref[...], lnb_ref[...], wc, bs_ref)
            for g in range(GM_GROUPS):
                cols = slice(g * gd, (g + 1) * gd)
                o_ref[rows, cols] = (u[:, cols] * s[g]).astype(BF16)

    full = lambda shape: pl.BlockSpec(shape, lambda i: (0,) * len(shape))
    return pl.pallas_call(
        body, name=name, grid=(T // tm,),
        in_specs=[pl.BlockSpec((tm, E2), lambda i: (i, 0)), full((1, E)), full((1, E)),
                  full((GM_GROUPS, CHUNK, CHUNK)), full((GM_GROUPS, CHUNK, 1))],
        out_specs=pl.BlockSpec((tm, E), lambda i: (i, 0)),
        out_shape=jax.ShapeDtypeStruct((T, E), BF16),
        compiler_params=_cp(("parallel",)),
    )(z, lng, lnb, ws, bs)


def _sgu_bwd(z, dg, lng, lnb, ws, bs, name):
    T, E2 = z.shape
    E = E2 // 2
    gd = E // GM_GROUPS
    tm = min(2 * CHUNK, T)
    nsteps = T // tm

    def body(z_ref, dg_ref, lng_ref, lnb_ref, ws_ref, bs_ref, dz_ref, dlng_ref, dlnb_ref, dws_ref, dbs_ref):
        i = pl.program_id(0)

        @pl.when(i == 0)
        def _():
            dlng_ref[...] = jnp.zeros_like(dlng_ref)
            dlnb_ref[...] = jnp.zeros_like(dlnb_ref)
            dws_ref[...] = jnp.zeros_like(dws_ref)
            dbs_ref[...] = jnp.zeros_like(dbs_ref)

        wc, tri = _causal_ws(ws_ref)
        lng_v = lng_ref[...]
        for c in range(tm // CHUNK):
            rows = slice(c * CHUNK, (c + 1) * CHUNK)
            zu, zv, u, xhat, rs, vln, s = _sgu_pieces(z_ref[rows, :], lng_v, lnb_ref[...], wc, bs_ref)
            dgc = dg_ref[rows, :].astype(F32)
            du, dvln = [], []
            for g in range(GM_GROUPS):
                cols = slice(g * gd, (g + 1) * gd)
                dgg = dgc[:, cols]
                du.append(dgg * s[g])
                ds = dgg * u[:, cols]
                dsb = ds.astype(BF16)
                dws_ref[g] += _dot_nt(dsb, vln[:, cols].astype(BF16))
                dbs_ref[g] += jnp.sum(ds, axis=-1, keepdims=True)
                dvln.append(_dot_tn(wc[g], dsb))
            du = jnp.concatenate(du, axis=1)
            dvln = jnp.concatenate(dvln, axis=1)
            dlng_ref[...] += jnp.sum(dvln * xhat, axis=0, keepdims=True)
            dlnb_ref[...] += jnp.sum(dvln, axis=0, keepdims=True)
            dxh = dvln * lng_v
            m1 = jnp.mean(dxh, axis=-1, keepdims=True)
            m2 = jnp.mean(dxh * xhat, axis=-1, keepdims=True)
            dv = rs * (dxh - m1 - xhat * m2)
            dz_ref[rows, :E] = (du * _gelu_grad(zu)).astype(BF16)
            dz_ref[rows, E:] = (dv * _gelu_grad(zv)).astype(BF16)

        @pl.when(i == nsteps - 1)
        def _():
            for g in range(GM_GROUPS):
                dws_ref[g] = jnp.where(tri, dws_ref[g], 0.0)

    full = lambda shape: pl.BlockSpec(shape, lambda i: (0,) * len(shape))
    return pl.pallas_call(
        body, name=name, grid=(nsteps,),
        in_specs=[pl.BlockSpec((tm, E2), lambda i: (i, 0)), pl.BlockSpec((tm, E), lambda i: (i, 0)),
                  full((1, E)), full((1, E)), full((GM_GROUPS, CHUNK, CHUNK)), full((GM_GROUPS, CHUNK, 1))],
        out_specs=[pl.BlockSpec((tm, E2), lambda i: (i, 0)), full((1, E)), full((1, E)),
                   full((GM_GROUPS, CHUNK, CHUNK)), full((GM_GROUPS, CHUNK, 1))],
        out_shape=[jax.ShapeDtypeStruct((T, E2), BF16), jax.ShapeDtypeStruct((1, E), F32),
                   jax.ShapeDtypeStruct((1, E), F32), jax.ShapeDtypeStruct((GM_GROUPS, CHUNK, CHUNK), F32),
                   jax.ShapeDtypeStruct((GM_GROUPS, CHUNK, 1), F32)],
        compiler_params=_cp(("arbitrary",)),
    )(z, dg, lng, lnb, ws, bs)


HALO = 16


def _conv_taps(a_ext, w_ref, b_ref):
    n = a_ext.shape[0]
    am1 = pltpu.roll(a_ext, 1, 0)
    am2 = pltpu.roll(a_ext, 2, 0)
    del n
    return ((b_ref[...] + am2 * w_ref[0:1, :]) + am1 * w_ref[1:2, :]) + a_ext * w_ref[2:3, :], am1, am2


def _ffn_mid_fwd(au, cw, cb, name):
    T, F = au.shape[0], au.shape[1] // 2
    tm, tf = min(TM, T), min(TF, F)
    hb = tm // HALO
    nf = F // tf

    def body(a_ref, ap_ref, u_ref, w_ref, b_ref, o_ref):
        i = pl.program_id(1)
        prev = jnp.where(i == 0, 0.0, ap_ref[...])
        ext = jnp.concatenate([prev, a_ref[...]], axis=0)
        conv, _, _ = _conv_taps(ext, w_ref, b_ref)
        conv = conv[HALO:, :]
        o_ref[...] = ((conv * _sigmoid(conv)) * u_ref[...]).astype(BF16)

    main = pl.BlockSpec((tm, tf), lambda f, i: (i, f))
    return pl.pallas_call(
        body, name=name, grid=(nf, T // tm),
        in_specs=[main, pl.BlockSpec((HALO, tf), lambda f, i: (jnp.maximum(i * hb - 1, 0), f)),
                  pl.BlockSpec((tm, tf), lambda f, i: (i, nf + f)),
                  pl.BlockSpec((3, tf), lambda f, i: (0, f)), pl.BlockSpec((1, tf), lambda f, i: (0, f))],
        out_specs=main, out_shape=jax.ShapeDtypeStruct((T, F), BF16),
        compiler_params=_cp(("parallel", "parallel")),
    )(au, au, au, cw, cb)


def _ffn_mid_bwd(au, dact, cw, cb, name):
    T, F = au.shape[0], au.shape[1] // 2
    tm, tf = min(TM, T), min(TF, F)
    hb = tm // HALO
    nt = T // tm
    nf = F // tf
    last_h = T // HALO - 1

    def body(a_ref, ap_ref, an_ref, u_ref, un_ref, d_ref, dn_ref, w_ref, b_ref, da_ref, du_ref, dcw_ref, dcb_ref):
        i = pl.program_id(1)
        prev = jnp.where(i == 0, 0.0, ap_ref[...])
        a_main = a_ref[...]
        a_ext = jnp.concatenate([prev, a_main, an_ref[...]], axis=0)
        conv, am1, am2 = _conv_taps(a_ext, w_ref, b_ref)
        conv = conv[HALO:, :]
        sig = _sigmoid(conv)
        u_ext = jnp.concatenate([u_ref[...], un_ref[...]], axis=0)
        d_ext = jnp.concatenate([d_ref[...], dn_ref[...]], axis=0).astype(F32)
        n = tm + HALO
        row = lax.broadcasted_iota(jnp.int32, (n, 1), 0)
        live = jnp.logical_or(row < tm, i < nt - 1)
        dconv = jnp.where(live, d_ext * u_ext * (sig * (1.0 + conv * (1.0 - sig))), 0.0)
        du_ref[...] = (d_ext[:tm, :] * (conv[:tm, :] * sig[:tm, :])).astype(BF16)
        dp1 = pltpu.roll(dconv, n - 1, 0)[:tm, :]
        dp2 = pltpu.roll(dconv, n - 2, 0)[:tm, :]
        dc = dconv[:tm, :]
        da_ref[...] = ((dc * w_ref[2:3, :] + dp1 * w_ref[1:2, :]) + dp2 * w_ref[0:1, :]).astype(BF16)
        g2 = jnp.sum(dc * a_main, axis=0, keepdims=True)
        g1 = jnp.sum(dc * am1[HALO:HALO + tm, :], axis=0, keepdims=True)
        g0 = jnp.sum(dc * am2[HALO:HALO + tm, :], axis=0, keepdims=True)
        gb = jnp.sum(dc, axis=0, keepdims=True)

        @pl.when(i == 0)
        def _():
            dcw_ref[...] = jnp.zeros_like(dcw_ref)
            dcb_ref[...] = jnp.zeros_like(dcb_ref)

        dcw_ref[0:1, :] += g0
        dcw_ref[1:2, :] += g1
        dcw_ref[2:3, :] += g2
        dcb_ref[...] += gb

    main = pl.BlockSpec((tm, tf), lambda f, i: (i, f))
    prev = pl.BlockSpec((HALO, tf), lambda f, i: (jnp.maximum(i * hb - 1, 0), f))
    nxt = pl.BlockSpec((HALO, tf), lambda f, i: (jnp.minimum((i + 1) * hb, last_h), f))
    main_u = pl.BlockSpec((tm, tf), lambda f, i: (i, nf + f))
    nxt_u = pl.BlockSpec((HALO, tf), lambda f, i: (jnp.minimum((i + 1) * hb, last_h), nf + f))
    return pl.pallas_call(
        body, name=name, grid=(nf, nt),
        in_specs=[main, prev, nxt, main_u, nxt_u, main, nxt,
                  pl.BlockSpec((3, tf), lambda f, i: (0, f)), pl.BlockSpec((1, tf), lambda f, i: (0, f))],
        out_specs=[main, main, pl.BlockSpec((3, tf), lambda f, i: (0, f)), pl.BlockSpec((1, tf), lambda f, i: (0, f))],
        out_shape=[jax.ShapeDtypeStruct((T, F), BF16), jax.ShapeDtypeStruct((T, F), BF16),
                   jax.ShapeDtypeStruct((3, F), F32), jax.ShapeDtypeStruct((1, F), F32)],
        compiler_params=_cp(("parallel", "arbitrary")),
    )(au, au, au, au, au, dact, dact, cw, cb)


def _split3(x):
    hi = x.astype(BF16)
    r1 = x - hi.astype(F32)
    mid = r1.astype(BF16)
    lo = (r1 - mid.astype(F32)).astype(BF16)
    return hi, mid, lo


def _tri_ones(n, upper):
    r = lax.broadcasted_iota(jnp.int32, (n, n), 0)
    c = lax.broadcasted_iota(jnp.int32, (n, n), 1)
    return jnp.where((r <= c) if upper else (r >= c), 1.0, 0.0).astype(BF16)


def _gate_scan(f, bf, name):
    T = f.shape[0]
    tm = min(256, T)

    def body(f_ref, b_ref, cp_ref, sn_ref, carry_ref):
        i = pl.program_id(0)

        @pl.when(i == 0)
        def _():
            carry_ref[...] = jnp.zeros_like(carry_ref)

        x = f_ref[...] + b_ref[...]
        e = jnp.exp(-jnp.abs(x))
        logf = jnp.minimum(x, 0.0) - jnp.log(1.0 + e)
        sn_ref[...] = jnp.where(x >= 0.0, e / (1.0 + e), 1.0 / (1.0 + e))
        tri = _tri_ones(tm, upper=False)
        c = carry_ref[...]
        for piece in _split3(logf):
            c = c + jnp.dot(tri, piece, preferred_element_type=F32)
        carry_ref[...] += jnp.sum(logf, axis=0, keepdims=True)
        hi, mid, lo = _split3(c * LOG2E)
        cp_ref[:, 0:LANES] = hi
        cp_ref[:, LANES:2 * LANES] = mid
        cp_ref[:, 2 * LANES:3 * LANES] = lo

    return pl.pallas_call(
        body, name=name, grid=(T // tm,),
        in_specs=[pl.BlockSpec((tm, LANES), lambda i: (i, 0)), pl.BlockSpec((1, LANES), lambda i: (0, 0))],
        out_specs=[pl.BlockSpec((tm, 3 * LANES), lambda i: (i, 0)), pl.BlockSpec((tm, LANES), lambda i: (i, 0))],
        out_shape=[jax.ShapeDtypeStruct((T, 3 * LANES), BF16), jax.ShapeDtypeStruct((T, LANES), F32)],
        scratch_shapes=[pltpu.VMEM((1, LANES), F32)],
        compiler_params=_cp(("arbitrary",)),
    )(f, bf)


def _gate_scan_bwd(dcq, dck, sneg, name):
    T = dcq.shape[0]
    tm = min(256, T)
    n = T // tm

    def body(dcq_ref, dck_ref, sn_ref, df_ref, db_ref, carry_ref):
        i = pl.program_id(0)

        @pl.when(i == 0)
        def _():
            carry_ref[...] = jnp.zeros_like(carry_ref)
            db_ref[...] = jnp.zeros_like(db_ref)

        tri = _tri_ones(tm, upper=True)
        dcb = dcq_ref[...] - dck_ref[...]
        acc = carry_ref[...]
        for piece in _split3(dcb):
            acc = acc + jnp.dot(tri, piece, preferred_element_type=F32)
        carry_ref[...] += jnp.sum(dcb, axis=0, keepdims=True)
        df = acc * sn_ref[...]
        df_ref[...] = df.astype(BF16)
        db_ref[...] += jnp.sum(df, axis=0, keepdims=True)

    rev = pl.BlockSpec((tm, LANES), lambda i: (n - 1 - i, 0))
    return pl.pallas_call(
        body, name=name, grid=(n,),
        in_specs=[rev, rev, rev],
        out_specs=[rev, pl.BlockSpec((1, LANES), lambda i: (0, 0))],
        out_shape=[jax.ShapeDtypeStruct((T, LANES), BF16), jax.ShapeDtypeStruct((1, LANES), F32)],
        scratch_shapes=[pltpu.VMEM((1, LANES), F32)],
        compiler_params=_cp(("arbitrary",)),
    )(dcq, dck, sneg)


def _qk_proj(hn, w_pads, cp, sels, consts, scales, name):
    T, D = hn.shape
    H = w_pads[0].shape[1] // LANES
    tm = min(TM, T)

    def body(a_ref, cp_ref, wq_ref, wk_ref, sq_ref, sk_ref, cq_ref, ck_ref, qo_ref, ko_ref):
        a = a_ref[...]
        cpv = cp_ref[...]
        for w_ref, sel_ref, c_ref, o_ref, scale in ((wq_ref, sq_ref, cq_ref, qo_ref, scales[0]),
                                                    (wk_ref, sk_ref, ck_ref, ko_ref, scales[1])):
            for p in range(H // 2):
                acc = jnp.dot(a, w_ref[:, p * 2 * LANES:(p + 1) * 2 * LANES], preferred_element_type=F32)
                if scale != 1.0:
                    acc = acc * scale
                acc = acc + jnp.dot(cpv, sel_ref[p], preferred_element_type=F32) + c_ref[p]
                o_ref[2 * p] = acc[:, :LANES].astype(BF16)
                o_ref[2 * p + 1] = acc[:, LANES:].astype(BF16)

    whole = lambda t: pl.BlockSpec(t.shape, lambda i: (0,) * t.ndim)
    out = jax.ShapeDtypeStruct((H, T, LANES), BF16)
    o_spec = pl.BlockSpec((H, tm, LANES), lambda i: (0, i, 0))
    return pl.pallas_call(
        body, name=name, grid=(T // tm,),
        in_specs=[pl.BlockSpec((tm, D), lambda i: (i, 0)), pl.BlockSpec((tm, 3 * LANES), lambda i: (i, 0)),
                  whole(w_pads[0]), whole(w_pads[1]), whole(sels[0]), whole(sels[1]), whole(consts[0]), whole(consts[1])],
        out_specs=[o_spec, o_spec], out_shape=[out, out],
        compiler_params=_cp(("parallel",)),
    )(hn, cp, w_pads[0], w_pads[1], sels[0], sels[1], consts[0], consts[1])


def _lane_lo():
    return lax.broadcasted_iota(jnp.int32, (1, LANES), 1) < HEAD_DIM


def _attn_fwd(qp, kp, v, name):
    H, T, _ = qp.shape
    tq = min(TQ, T)
    nrep = tq // LANES
    n_parts = 4 if tq % 512 == 0 else 1
    rows = tq // n_parts

    def body(q_ref, k_ref, v_ref, o_ref, o32_ref, lse_ref, m_sc, acc_sc):
        i = pl.program_id(1)
        m_sc[...] = jnp.full(m_sc.shape, NEG, F32)
        acc_sc[...] = jnp.zeros_like(acc_sc)
        ones_col = jnp.where(lax.broadcasted_iota(jnp.int32, (tq, LANES), 1) == 0, 1.0, 0.0).astype(BF16)

        def step(j, masked):
            off = pl.multiple_of(j * tq, tq)
            vaug = jnp.concatenate([v_ref[pl.ds(off, tq), :], ones_col], axis=1)
            chains = [(h, rp) for h in range(2) for rp in range(n_parts)]
            s_all = [_dot_nt(q_ref[h, rp * rows:(rp + 1) * rows, :], k_ref[h, pl.ds(off, tq), :]) for h, rp in chains]
            for (h, rp), s in zip(chains, s_all):
                rsl = slice(rp * rows, (rp + 1) * rows)
                tiles = [s[:, c * LANES:(c + 1) * LANES] for c in range(nrep)]
                if masked:
                    r = lax.broadcasted_iota(jnp.int32, (rows, LANES), 0) + rp * rows
                    cc = lax.broadcasted_iota(jnp.int32, (rows, LANES), 1)
                    tiles = [jnp.where(r >= cc + c * LANES, t, NEG) for c, t in enumerate(tiles)]
                mt = tiles[0]
                for t in tiles[1:]:
                    mt = jnp.maximum(mt, t)
                m_prev = m_sc[h, rsl, :]
                m_new = jnp.maximum(m_prev, jnp.max(mt, axis=-1, keepdims=True))
                alpha = jnp.exp2(m_prev - m_new)
                p16 = jnp.concatenate([jnp.exp2(t - m_new).astype(BF16) for t in tiles], axis=1)
                pv = jnp.dot(p16, vaug, preferred_element_type=F32)
                acc_sc[h, rsl, :] = jnp.concatenate([alpha, alpha], axis=1) * acc_sc[h, rsl, :] + pv
                m_sc[h, rsl, :] = m_new

        def loop_body(j, carry):
            step(j, False)
            return carry

        lax.fori_loop(0, i, loop_body, 0)
        step(i, True)
        lo = _lane_lo()
        acc0, acc1 = acc_sc[0], acc_sc[1]
        l0 = jnp.sum(acc0[:, LANES:], axis=-1, keepdims=True)
        l1 = jnp.sum(acc1[:, LANES:], axis=-1, keepdims=True)
        o = jnp.where(lo, acc0[:, :LANES] / l0, acc1[:, :LANES] / l1)
        o_ref[...] = o.astype(BF16)
        o32_ref[...] = o
        lse_ref[...] = jnp.where(lo, m_sc[0] + jnp.log(l0) * LOG2E, m_sc[1] + jnp.log(l1) * LOG2E)

    oblk = pl.BlockSpec((tq, LANES), lambda p, i: (i, p))
    return pl.pallas_call(
        body, name=name, grid=(H // 2, T // tq),
        in_specs=[pl.BlockSpec((2, tq, LANES), lambda p, i: (p, i, 0)),
                  pl.BlockSpec((2, T, LANES), lambda p, i: (p, 0, 0)),
                  pl.BlockSpec((T, LANES), lambda p, i: (0, p))],
        out_specs=[oblk, oblk, pl.BlockSpec((None, tq, LANES), lambda p, i: (p, i, 0))],
        out_shape=[jax.ShapeDtypeStruct((T, H * HEAD_DIM), BF16), jax.ShapeDtypeStruct((T, H * HEAD_DIM), F32),
                   jax.ShapeDtypeStruct((H // 2, T, LANES), F32)],
        scratch_shapes=[pltpu.VMEM((2, tq, LANES), F32), pltpu.VMEM((2, tq, 2 * LANES), F32)],
        compiler_params=_cp(("parallel", "arbitrary")),
    )(qp, kp, v)


def _attn_fwd_t(qp, kp, vt, name):
    H, T, _ = qp.shape
    tq = min(TQ, T)
    hd = HEAD_DIM
    ext = hd + 16

    def body(q_ref, k_ref, vt_ref, o_ref, o32_ref, lse_ref, m_sc, acc_sc):
        i = pl.program_id(1)
        m_sc[...] = jnp.full(m_sc.shape, NEG, F32)
        acc_sc[...] = jnp.zeros_like(acc_sc)
        q_t = [jnp.transpose(q_ref[h].astype(F32)).astype(BF16) for h in range(2)]
        ones_rows = jnp.where(lax.broadcasted_iota(jnp.int32, (16, tq), 0) == 0, 1.0, 0.0).astype(BF16)

        def steps(blocks):
            offs = [pl.multiple_of(j * tq, tq) for j, _ in blocks]
            s_all = [[jnp.dot(k_ref[h, pl.ds(off, tq), :], q_t[h], preferred_element_type=F32) for h in range(2)]
                     for off in offs]
            for (j, masked), off, s_blk in zip(blocks, offs, s_all):
                for h in range(2):
                    s = s_blk[h]
                    if masked:
                        kr = lax.broadcasted_iota(jnp.int32, (tq, tq), 0)
                        qc = lax.broadcasted_iota(jnp.int32, (tq, tq), 1)
                        s = jnp.where(qc >= kr, s, NEG)
                    m_prev = m_sc[h]
                    m_new = jnp.maximum(m_prev, jnp.max(s, axis=0, keepdims=True))
                    alpha = jnp.exp2(m_prev - m_new)
                    p16 = jnp.exp2(s - m_new).astype(BF16)
                    v_aug = jnp.concatenate([vt_ref[h * hd:(h + 1) * hd, pl.ds(off, tq)], ones_rows], axis=0)
                    pv = jnp.dot(v_aug, p16, preferred_element_type=F32)
                    acc_sc[h] = alpha * acc_sc[h] + pv
                    m_sc[h] = m_new

        def group_body(t, carry):
            steps([(KV_UNROLL * t + u, False) for u in range(KV_UNROLL)])
            return carry

        lax.fori_loop(0, i // KV_UNROLL, group_body, 0)
        for rem in range(KV_UNROLL):

            @pl.when(i % KV_UNROLL == rem)
            def _(rem=rem):
                steps([(i - rem + u, u == rem) for u in range(rem + 1)])

        o_t, lse_t = [], []
        for h in range(2):
            acc = acc_sc[h]
            l = acc[hd:hd + 1, :]
            o_t.append(acc[:hd, :] / l)
            lse_t.append(jnp.broadcast_to(m_sc[h] + jnp.log(l) * LOG2E, (hd, tq)))
        o = jnp.transpose(jnp.concatenate(o_t, axis=0))
        o_ref[...] = o.astype(BF16)
        o32_ref[...] = o
        lse_ref[...] = jnp.transpose(jnp.concatenate(lse_t, axis=0))

    oblk = pl.BlockSpec((tq, LANES), lambda p, i: (i, p))
    return pl.pallas_call(
        body, name=name, grid=(H // 2, T // tq),
        in_specs=[pl.BlockSpec((2, tq, LANES), lambda p, i: (p, i, 0)),
                  pl.BlockSpec((2, T, LANES), lambda p, i: (p, 0, 0)),
                  pl.BlockSpec((2 * hd, T), lambda p, i: (p, 0))],
        out_specs=[oblk, oblk, pl.BlockSpec((None, tq, LANES), lambda p, i: (p, i, 0))],
        out_shape=[jax.ShapeDtypeStruct((T, H * HEAD_DIM), BF16), jax.ShapeDtypeStruct((T, H * HEAD_DIM), F32),
                   jax.ShapeDtypeStruct((H // 2, T, LANES), F32)],
        scratch_shapes=[pltpu.VMEM((2, 1, tq), F32), pltpu.VMEM((2, ext, tq), F32)],
        compiler_params=_cp(("parallel", "arbitrary")),
    )(qp, kp, vt)


def _attn_bwd(qp, kp, v, o, do, lse, scale, name):
    H, T, _ = qp.shape
    tq = min(TQ, T)
    nq = T // tq
    nrep = tq // LANES

    def body(q_ref, k_ref, v_ref, o_ref, do_ref, lse_ref, dq_ref, dk_ref, dv_ref, dqe_ref, dke_ref, dk_sc, dv_sc, dq_sc):
        i = pl.program_id(1)

        @pl.when(i == 0)
        def _():
            dk_sc[...] = jnp.zeros_like(dk_sc)
            dv_sc[...] = jnp.zeros_like(dv_sc)

        dq_sc[...] = jnp.zeros_like(dq_sc)

        lo = _lane_lo()
        dob = do_ref[...]
        dof = dob.astype(F32)
        prod = dof * o_ref[...].astype(F32)
        lse2 = lse_ref[...]
        lse2_sw = pltpu.roll(lse2, HEAD_DIM, 1)
        zero = jnp.zeros_like(dob)
        do_h = [jnp.where(lo, dob, zero), jnp.where(lo, zero, dob)]
        rep = lambda col: jnp.broadcast_to(col, (tq, LANES))
        delta = [rep(jnp.sum(jnp.where(lo, prod, 0.0), axis=-1, keepdims=True)),
                 rep(jnp.sum(jnp.where(lo, 0.0, prod), axis=-1, keepdims=True))]
        lse_h = [jnp.where(lo, lse2, lse2_sw), jnp.where(lo, lse2_sw, lse2)]
        qs = [q_ref[0], q_ref[1]]
        tr16 = lambda a: jnp.transpose(a.astype(F32)).astype(BF16)
        q_t = [tr16(qs[0]), tr16(qs[1])]
        do_t = [tr16(do_h[0]), tr16(do_h[1])]

        def steps(blocks):
            offs = [pl.multiple_of(j * tq, tq) for j, _ in blocks]
            vblks = [v_ref[pl.ds(off, tq), :] for off in offs]
            kblks = [[k_ref[h, pl.ds(off, tq), :] for h in range(2)] for off in offs]
            s_all = [[_dot_nt(qs[h], kb[h]) for h in range(2)] for kb in kblks]
            dp_all = [[_dot_nt(do_h[h], vb) for h in range(2)] for vb in vblks]
            for b, ((j, masked), off) in enumerate(zip(blocks, offs)):
                dv_add = None
                for h in range(2):
                    kblk, s, dp = kblks[b][h], s_all[b][h], dp_all[b][h]
                    p16, ds16 = [], []
                    for c in range(nrep):
                        cols = slice(c * LANES, (c + 1) * LANES)
                        p = jnp.exp2(s[:, cols] - lse_h[h])
                        if masked:
                            r = lax.broadcasted_iota(jnp.int32, (tq, LANES), 0)
                            cc = lax.broadcasted_iota(jnp.int32, (tq, LANES), 1)
                            p = jnp.where(r >= cc + c * LANES, p, 0.0)
                        p16.append(p.astype(BF16))
                        ds16.append((p * (dp[:, cols] - delta[h])).astype(BF16))
                    p16 = jnp.concatenate(p16, axis=1)
                    dsb = jnp.concatenate(ds16, axis=1)
                    dq_sc[h] += jnp.dot(dsb, kblk, preferred_element_type=F32)
                    dk_sc[h, :, pl.ds(off, tq)] += jnp.dot(q_t[h], dsb, preferred_element_type=F32)
                    pv = jnp.dot(do_t[h], p16, preferred_element_type=F32)
                    dv_add = pv if dv_add is None else dv_add + pv
                dv_sc[:, pl.ds(off, tq)] += dv_add

        def group_body(t, carry):
            steps([(KV_UNROLL_BWD * t + u, False) for u in range(KV_UNROLL_BWD)])
            return carry

        lax.fori_loop(0, i // KV_UNROLL_BWD, group_body, 0)
        for rem in range(KV_UNROLL_BWD):

            @pl.when(i % KV_UNROLL_BWD == rem)
            def _(rem=rem):
                steps([(i - rem + u, u == rem) for u in range(rem + 1)])

        dq0, dq1 = dq_sc[0], dq_sc[1]
        dq_ref[...] = (jnp.where(lo, dq0, pltpu.roll(dq1, HEAD_DIM, 1)) * scale).astype(BF16)
        dqe_ref[0:8, :] = jnp.transpose(dq0)[HEAD_DIM:HEAD_DIM + 8, :]
        dqe_ref[8:16, :] = jnp.transpose(dq1)[HEAD_DIM:HEAD_DIM + 8, :]

        @pl.when(i == nq - 1)
        def _():
            for h in range(2):
                dke_ref[8 * h:8 * h + 8, :] = dk_sc[h, HEAD_DIM:HEAD_DIM + 8, :]
            for cb in range(nq):
                tok = slice(cb * tq, (cb + 1) * tq)
                dk0 = jnp.transpose(dk_sc[0, :, tok])
                dk1 = jnp.transpose(dk_sc[1, :, tok])
                dk_ref[tok, :] = (jnp.where(lo, dk0, pltpu.roll(dk1, HEAD_DIM, 1)) * LN2).astype(BF16)
                dv_ref[tok, :] = jnp.transpose(dv_sc[:, tok]).astype(BF16)

    qblk = pl.BlockSpec((tq, LANES), lambda p, i: (i, p))
    pair = pl.BlockSpec((T, LANES), lambda p, i: (0, p))
    tok16 = jax.ShapeDtypeStruct((T, H * HEAD_DIM), BF16)
    gate32 = jax.ShapeDtypeStruct((H // 2, 16, T), F32)
    return pl.pallas_call(
        body, name=name, grid=(H // 2, nq),
        in_specs=[pl.BlockSpec((2, tq, LANES), lambda p, i: (p, i, 0)),
                  pl.BlockSpec((2, T, LANES), lambda p, i: (p, 0, 0)),
                  pair, qblk, qblk,
                  pl.BlockSpec((None, tq, LANES), lambda p, i: (p, i, 0))],
        out_specs=[qblk, pair, pair, pl.BlockSpec((None, 16, tq), lambda p, i: (p, 0, i)),
                   pl.BlockSpec((None, 16, T), lambda p, i: (p, 0, 0))],
        out_shape=[tok16, tok16, tok16, gate32, gate32],
        scratch_shapes=[pltpu.VMEM((2, LANES, T), F32), pltpu.VMEM((LANES, T), F32),
                        pltpu.VMEM((2, tq, LANES), F32)],
        compiler_params=_cp(("parallel", "arbitrary")),
    )(qp, kp, v, o, do, lse)


def _mesh_pos():
    return lax.axis_index("x"), lax.axis_index("y"), lax.axis_index("c")


def _all_gather(arrs, name, groups=None):
    n = len(arrs)
    if groups is None:
        groups = [(a, 0) for a in range(n)]
    ng = 1 + max(g for g, _ in groups)
    per_group = [sum(1 for g, _ in groups if g == gi) for gi in range(ng)]
    first_of = [next(a for a in range(n) if groups[a][0] == gi) for gi in range(ng)]

    def body(*refs):
        ins, outs = refs[:n], refs[n:n + ng]
        send_sems, recv_sems, local_sems = refs[n + ng:]
        x, y, c = _mesh_pos()
        me, sib = (x, y, c), (x, y, 1 - c)
        chips = [(1 - x, y), (x, 1 - y), (1 - x, 1 - y)]

        def dst_of(a, px, py, pc):
            g, k = groups[a]
            return outs[g].at[N_DEV * k + 4 * px + 2 * py + pc]

        def copy(a, k, block, to, src=None):
            dst = dst_of(a, *block)
            return pltpu.make_async_remote_copy(
                src_ref=dst if src is None else src, dst_ref=dst,
                send_sem=send_sems.at[a, k], recv_sem=recv_sems.at[a, k], device_id=to, device_id_type=MESH)

        mine = [pltpu.make_async_copy(ins[a], dst_of(a, *me), local_sems.at[a]) for a in range(n)]
        for cp in mine:
            cp.start()
        first = []
        for a in range(n):
            first.append(copy(a, 0, me, sib, src=ins[a]))
            first += [copy(a, 1 + j, me, (*chip, c), src=ins[a]) for j, chip in enumerate(chips)]
        for cp in first:
            cp.start()
        passed = []
        for j, chip in enumerate(chips):
            for a in range(n):
                copy(a, 1 + j, (*chip, c), me).wait_recv()
                fwd = copy(a, 4 + j, (*chip, c), sib)
                fwd.start()
                passed.append(fwd)
        for a in range(n):
            copy(a, 0, sib, me).wait_recv()
            for j, chip in enumerate(chips):
                copy(a, 4 + j, (*chip, 1 - c), me).wait_recv()
        for cp in first + passed:
            cp.wait_send()
        for cp in mine:
            cp.wait()

    any_spec = pl.BlockSpec(memory_space=pl.ANY)
    return pl.pallas_call(
        body, name=name,
        in_specs=[any_spec] * n, out_specs=[any_spec] * ng,
        out_shape=[jax.ShapeDtypeStruct((N_DEV * per_group[gi],) + arrs[first_of[gi]].shape, arrs[first_of[gi]].dtype)
                   for gi in range(ng)],
        scratch_shapes=[pltpu.SemaphoreType.DMA((n, 7)), pltpu.SemaphoreType.DMA((n, 7)),
                        pltpu.SemaphoreType.DMA((n,))],
    )(*arrs)


def _pair_exchange(gs, name):
    n = len(gs)

    def body(*refs):
        g_refs, o_refs = refs[:n], refs[n:2 * n]
        send_sems, recv_sems = refs[2 * n:]
        x, y, c = _mesh_pos()
        sib = (x, y, 1 - c)
        copies = []
        for a in range(n):
            for j in range(4):
                copies.append(pltpu.make_async_remote_copy(
                    src_ref=g_refs[a].at[2 * j + (1 - c)], dst_ref=o_refs[a].at[j],
                    send_sem=send_sems.at[a, j], recv_sem=recv_sems.at[a, j], device_id=sib, device_id_type=MESH))
        for cp in copies:
            cp.start()
        for cp in copies:
            cp.wait_recv()
        for cp in copies:
            cp.wait_send()

    any_spec = pl.BlockSpec(memory_space=pl.ANY)
    return pl.pallas_call(
        body, name=name, in_specs=[any_spec] * n, out_specs=[any_spec] * n,
        out_shape=[jax.ShapeDtypeStruct((4,) + g.shape[1:], g.dtype) for g in gs],
        scratch_shapes=[pltpu.SemaphoreType.DMA((n, 4)), pltpu.SemaphoreType.DMA((n, 4))],
    )(*gs)


def _chip_exchange(parts, name):
    n = len(parts)

    def body(*refs):
        p_refs, o_refs = refs[:n], refs[n:2 * n]
        send_sems, recv_sems = refs[2 * n:]
        x, y, c = _mesh_pos()
        chips = [(1 - x, y), (x, 1 - y), (1 - x, 1 - y)]
        copies = []
        for a in range(n):
            for k, (px, py) in enumerate(chips):
                copies.append(pltpu.make_async_remote_copy(
                    src_ref=p_refs[a].at[2 * px + py], dst_ref=o_refs[a].at[k],
                    send_sem=send_sems.at[a, k], recv_sem=recv_sems.at[a, k], device_id=(px, py, c),
                    device_id_type=MESH))
        for cp in copies:
            cp.start()
        for cp in copies:
            cp.wait_recv()
        for cp in copies:
            cp.wait_send()

    any_spec = pl.BlockSpec(memory_space=pl.ANY)
    return pl.pallas_call(
        body, name=name, in_specs=[any_spec] * n, out_specs=[any_spec] * n,
        out_shape=[jax.ShapeDtypeStruct((3,) + p.shape[1:], p.dtype) for p in parts],
        scratch_shapes=[pltpu.SemaphoreType.DMA((n, 3)), pltpu.SemaphoreType.DMA((n, 3))],
    )(*parts)


HBM_SPEC = pl.BlockSpec(memory_space=pltpu.HBM)
SEM_SPEC = pl.BlockSpec(memory_space=pltpu.SEMAPHORE)
ANY_SPEC = pl.BlockSpec(memory_space=pl.ANY)
DATAFLOW_EFFECT = pltpu.SideEffectType.DATAFLOW_SIDE_EFFECTING


def _peers():
    x, y, c = _mesh_pos()
    flip = lambda v, b: 1 - v if b else v
    return [(flip(x, (k >> 2) & 1), flip(y, (k >> 1) & 1), flip(c, k & 1)) for k in range(1, N_DEV)]


def _slot(p):
    return 4 * p[0] + 2 * p[1] + p[2]


def _direct_copy(src_refs, land_refs, sems, a, k, p, land_of, dst_slot, src_slot):
    s = src_slot(a, p)
    return pltpu.make_async_remote_copy(
        src_ref=src_refs[a] if s is None else src_refs[a].at[s], dst_ref=land_refs[land_of[a]].at[dst_slot(a, k)],
        send_sem=sems[0].at[a * (N_DEV - 1) + k], recv_sem=sems[1].at[a * (N_DEV - 1) + k], device_id=p,
        device_id_type=MESH)


def _direct_start(srcs, lands, land_of, dst_slot, src_slot, after, name, collective_id):
    n, nl = len(srcs), len(lands)

    def body(*refs):
        src_refs, land_refs = refs[:n], refs[n:n + nl]
        sems = (refs[n + nl + 1], refs[n + nl + 2])
        token = refs[-1]
        peers = _peers()
        barrier = pltpu.get_barrier_semaphore()
        for p in peers:
            pl.semaphore_signal(barrier, inc=1, device_id=p, device_id_type=MESH)
        pl.semaphore_wait(barrier, N_DEV - 1)
        for a in range(n):
            for k, p in enumerate(peers):
                _direct_copy(src_refs, land_refs, sems, a, k, p, land_of, dst_slot, src_slot).start()
        token[...] = jnp.zeros_like(token)

    hbm = lambda t: pltpu.HBM(t.shape, t.dtype)
    sem_t = pltpu.SemaphoreType.DMA((n * (N_DEV - 1),))
    outs = pl.pallas_call(
        body, name=name,
        out_shape=(sem_t, sem_t, *[hbm(t) for t in srcs], *[hbm(t) for t in lands], jax.ShapeDtypeStruct((8, LANES), F32)),
        in_specs=[HBM_SPEC] * (n + nl) + [ANY_SPEC],
        out_specs=(SEM_SPEC, SEM_SPEC, *([HBM_SPEC] * (n + nl)), pl.BlockSpec(memory_space=pltpu.VMEM)),
        input_output_aliases={i: 2 + i for i in range(n + nl)},
        compiler_params=pltpu.CompilerParams(has_side_effects=DATAFLOW_EFFECT, collective_id=collective_id),
    )(*[pltpu.with_memory_space_constraint(t, pltpu.HBM) for t in srcs],
      *[pltpu.with_memory_space_constraint(t, pltpu.HBM) for t in lands], after)
    return outs[0], outs[1], list(outs[2:2 + n]), list(outs[2 + n:2 + n + nl]), outs[-1]


def _direct_wait(send_sems, recv_sems, srcs, lands, land_of, idxs, dst_slot, src_slot, after, name):
    land_ids = []
    for a in idxs:
        if land_of[a] not in land_ids:
            land_ids.append(land_of[a])
    m, ml = len(idxs), len(land_ids)
    sub_land_of = {j: land_ids.index(land_of[a]) for j, a in enumerate(idxs)}

    def body(*refs):
        src_refs, land_refs = refs[:m], refs[m:m + ml]
        ssem, rsem = refs[m + ml], refs[m + ml + 1]
        for j, a in enumerate(idxs):
            for k, p in enumerate(_peers()):
                s = src_slot(a, p)
                cp = pltpu.make_async_remote_copy(
                    src_ref=src_refs[j] if s is None else src_refs[j].at[s],
                    dst_ref=land_refs[sub_land_of[j]].at[dst_slot(a, k)],
                    send_sem=ssem.at[a * (N_DEV - 1) + k], recv_sem=rsem.at[a * (N_DEV - 1) + k], device_id=p,
                    device_id_type=MESH)
                cp.wait_send()
                cp.wait_recv()

    hbm = lambda t: pltpu.HBM(t.shape, t.dtype)
    sub_s, sub_l = [srcs[a] for a in idxs], [lands[g] for g in land_ids]
    outs = pl.pallas_call(
        body, name=name,
        out_shape=(*[hbm(t) for t in sub_s], *[hbm(t) for t in sub_l]),
        in_specs=[HBM_SPEC] * (m + ml) + [SEM_SPEC, SEM_SPEC, ANY_SPEC],
        out_specs=tuple([HBM_SPEC] * (m + ml)),
        input_output_aliases={i: i for i in range(m + ml)},
        compiler_params=pltpu.CompilerParams(has_side_effects=DATAFLOW_EFFECT),
    )(*sub_s, *sub_l, send_sems, recv_sems, after)
    return list(outs[:m]), list(outs[m:])


def _row_block(R, C):
    best = None
    for d in range(16, R + 1, 16):
        if R % d == 0 and d * C <= 256 * 1024:
            best = d
    return best if best is not None else R


def _pair_add(g, recv, cidx, name):
    _, R, C = g.shape
    tr = _row_block(R, C)

    def body(c_ref, g_ref, r_ref, o_ref):
        del c_ref
        o_ref[...] = (g_ref[...].astype(F32) + r_ref[...].astype(F32)).astype(BF16)

    grid_spec = pltpu.PrefetchScalarGridSpec(
        num_scalar_prefetch=1, grid=(4, R // tr),
        in_specs=[pl.BlockSpec((None, tr, C), lambda j, i, c: (2 * j + c[0], i, 0)),
                  pl.BlockSpec((None, tr, C), lambda j, i, c: (j, i, 0))],
        out_specs=pl.BlockSpec((None, tr, C), lambda j, i, c: (j, i, 0)))
    return pl.pallas_call(
        body, name=name, grid_spec=grid_spec,
        out_shape=jax.ShapeDtypeStruct((4, R, C), BF16),
        compiler_params=_cp(("parallel", "parallel")),
    )(cidx, g, recv)


def _adamw_math(w, g, m, v):
    m = ADAM_B1 * m + (1.0 - ADAM_B1) * g
    v = ADAM_B2 * v + (1.0 - ADAM_B2) * (g * g)
    m_hat = m / (1.0 - ADAM_B1 ** ADAM_STEP)
    v_hat = v / (1.0 - ADAM_B2 ** ADAM_STEP)
    delta = -ADAM_LR * (m_hat / (jnp.sqrt(v_hat) + ADAM_EPS) + ADAM_WD * w)
    return delta, m, v


def _sum_adamw(parts, w, m, v, name, sel=None):
    R, C = w.shape
    tr = _row_block(R, C)
    specs, args = [], []
    for arr, idxs in parts:
        for idx in idxs:
            if idx < 0:
                specs.append(pl.BlockSpec((None, tr, C), lambda i, s: (s[0], i, 0)))
            else:
                specs.append(pl.BlockSpec((None, tr, C), lambda i, s, idx=idx: (idx, i, 0)))
            args.append(arr)
    npart = len(args)
    blk = pl.BlockSpec((tr, C), lambda i, s: (i, 0))

    def body(s_ref, *refs):
        del s_ref
        g = refs[0][...].astype(F32)
        for r in refs[1:npart]:
            g = g + r[...].astype(F32)
        w_ref, m_ref, v_ref, g_out, d_out, m_out, v_out = refs[npart:]
        delta, mm, vv = _adamw_math(w_ref[...], g, m_ref[...], v_ref[...])
        g_out[...] = g
        d_out[...] = delta
        m_out[...] = mm
        v_out[...] = vv

    grid_spec = pltpu.PrefetchScalarGridSpec(
        num_scalar_prefetch=1, grid=(R // tr,),
        in_specs=specs + [blk, blk, blk], out_specs=[blk] * 4)
    if sel is None:
        sel = jnp.zeros((1,), jnp.int32)
    return pl.pallas_call(
        body, name=name, grid_spec=grid_spec,
        out_shape=[jax.ShapeDtypeStruct((R, C), F32)] * 4,
        compiler_params=_cp(("parallel",)),
    )(sel, *args, w, m, v)


def _rows(a, c):
    return a.reshape(-1, c)


def _pad_rows(a, r):
    return jnp.pad(a, ((0, r - a.shape[0]), (0, 0))) if a.shape[0] != r else a


def _gate_tables():
    hp = N_HEADS // 2
    sel_q = np.zeros((hp, 3 * LANES, 2 * LANES), np.float32)
    sel_k = np.zeros((hp, 3 * LANES, 2 * LANES), np.float32)
    const_q = np.zeros((hp, 1, 2 * LANES), np.float32)
    const_k = np.zeros((hp, 1, 2 * LANES), np.float32)
    for p in range(hp):
        for hh in range(2):
            h = 2 * p + hh
            base = hh * LANES + HEAD_DIM
            for piece in range(3):
                sel_q[p, piece * LANES + h, base + piece] = 1.0
                sel_k[p, piece * LANES + h, base + 3 + piece] = -1.0
            const_k[p, 0, base:base + 3] = 1.0
            const_q[p, 0, base + 3:base + 6] = 1.0
    as_bf = lambda t: jnp.asarray(t, BF16)
    return as_bf(sel_q), as_bf(sel_k), jnp.asarray(const_q), jnp.asarray(const_k)


def _pad_heads(w):
    d = w.shape[0]
    w3 = w.reshape(d, N_HEADS, HEAD_DIM)
    return jnp.pad(w3, ((0, 0), (0, 0), (0, LANES - HEAD_DIM))).reshape(d, N_HEADS * LANES)


def kernel(x, mix_norm_g, ffn_norm_g, gm_w_in, gm_ln_g, gm_ln_b, gm_w_s, gm_b_s, gm_w_out, fox_w_qkvf, fox_b_f, fox_w_o, ffn_w_gate, ffn_w_up, ffn_conv_w, ffn_conv_b, ffn_w_down, final_norm_g, loss_target, m_mix_norm_g, m_ffn_norm_g, m_gm_w_in, m_gm_ln_g, m_gm_ln_b, m_gm_w_s, m_gm_b_s, m_gm_w_out, m_fox_w_qkvf, m_fox_b_f, m_fox_w_o, m_ffn_w_gate, m_ffn_w_up, m_ffn_conv_w, m_ffn_conv_b, m_ffn_w_down, m_final_norm_g, v_mix_norm_g, v_ffn_norm_g, v_gm_w_in, v_gm_ln_g, v_gm_ln_b, v_gm_w_s, v_gm_b_s, v_gm_w_out, v_fox_w_qkvf, v_fox_b_f, v_fox_w_o, v_ffn_w_gate, v_ffn_w_up, v_ffn_conv_w, v_ffn_conv_b, v_ffn_w_down, v_final_norm_g):
    T, D = x.shape[1], x.shape[2]
    E = gm_ln_g.shape[1]
    FF = ffn_conv_b.shape[1]
    NQKVF = 3 * D + N_HEADS
    xi, yi, ci = _mesh_pos()
    me = 4 * xi + 2 * yi + ci
    h0 = x.reshape(T, D)
    tgt = loss_target.reshape(T, D)

    nl = ffn_w_gate.shape[0]
    to16 = lambda a: a.astype(BF16)
    n_cw_rows = ffn_conv_w.size // LANES
    cw_rows = _pad_rows(_rows(ffn_conv_w.astype(F32), LANES), 16)
    w_in_g, w_out_g8, cwg = _all_gather([to16(gm_w_in[0]), to16(gm_w_out[0]), cw_rows], "ag_weights")
    w_out_g = w_out_g8.reshape(E, D)
    later, land_of, land_off, lands = [], [], [], []
    for l in range(nl):
        later += [to16(ffn_w_gate[l]), to16(ffn_w_up[l]), to16(ffn_w_down[l])]
        land_of += [2 * l, 2 * l, 2 * l + 1]
        land_off += [0, N_DEV, 0]
        lands += [lax.empty((2 * N_DEV, D, FF // N_DEV), BF16), lax.empty((N_DEV, FF // N_DEV, D), BF16)]
    later += [to16(fox_w_qkvf[0]), to16(fox_w_o[0])]
    land_of += [2 * nl, 2 * nl + 1]
    land_off += [0, 0]
    lands += [lax.empty((N_DEV, D, NQKVF // N_DEV), BF16), lax.empty((N_DEV, D // N_DEV, D), BF16)]
    ag_dst = lambda a, k: land_off[a] + _slot(_mesh_pos())
    ag_src = lambda a, p: None
    ag_send, ag_recv, later, lands, ag_token = _direct_start(later, lands, land_of, ag_dst, ag_src, w_in_g,
                                                             "ag_later_start", collective_id=1)

    def own_blocks(landed, shards, offs):
        for s, o in zip(shards, offs):
            landed = lax.dynamic_update_index_in_dim(landed, s, o + me, 0)
        return landed

    def gather_wait(idxs, after, name):
        return _direct_wait(ag_send, ag_recv, later, lands, land_of, idxs, ag_dst, ag_src, after, name)

    conv_w_full = jnp.transpose(cwg[:, :n_cw_rows].reshape(N_DEV, nl, 3, FF // N_DEV), (1, 2, 0, 3)).reshape(nl, 3, FF)

    ffn_w = {}

    def ffn_weights(l):
        return ffn_w[l]

    def land_ffn(l, shards, gu_land, dn_land):
        ffn_w[l] = (own_blocks(gu_land, shards[:2], [0, N_DEV]), own_blocks(dn_land, shards[2:3], [0]).reshape(FF, D))

    saved = {}

    def ffn_fwd(l, h_in):
        wgul, wdl = ffn_weights(l)
        hn = _rms_fwd(h_in, ffn_norm_g[l:l + 1], f"ffn{l}_norm")
        au = _mm_nn(hn, wgul, f"ffn{l}_gate_up")
        act = _ffn_mid_fwd(au, conv_w_full[l], ffn_conv_b[l:l + 1], f"ffn{l}_mid")
        h_out = _mm_nn(act, wdl, f"ffn{l}_down", res=h_in)
        saved[f"ffn{l}"] = (h_in, hn, au, act)
        return h_out

    bs_col = gm_b_s[0].reshape(GM_GROUPS, CHUNK, 1)
    hn0 = _rms_fwd(h0, mix_norm_g[0:1], "mix0_norm", after=ag_token)
    z = _mm_nn(hn0, w_in_g, "gm_in")
    gu = _sgu_fwd(z, gm_ln_g, gm_ln_b, gm_w_s[0], bs_col, "gm_sgu")
    h1 = _mm_nn(gu, w_out_g, "gm_out", res=h0)
    mine0, land0 = gather_wait([0, 1, 2], h1, "ag_ffn0_wait")
    land_ffn(0, mine0, *land0)
    h2 = ffn_fwd(0, h1)

    mine1, rest = gather_wait(list(range(3, 3 * nl + 2)), h2, "ag_layer1_wait")
    for l in range(1, nl):
        land_ffn(l, mine1[3 * (l - 1):3 * l], rest[2 * (l - 1)], rest[2 * (l - 1) + 1])
    w_qkvf = jnp.transpose(own_blocks(rest[-2], mine1[-2:-1], [0]), (1, 0, 2)).reshape(D, NQKVF)
    w_o_g = own_blocks(rest[-1], mine1[-1:], [0]).reshape(D, D)
    w_q, w_k, w_v = w_qkvf[:, :D], w_qkvf[:, D:2 * D], w_qkvf[:, 2 * D:3 * D]
    w_f = jnp.pad(w_qkvf[:, 3 * D:], ((0, 0), (0, LANES - N_HEADS)))
    bf_row = jnp.pad(fox_b_f, ((0, 0), (0, LANES - N_HEADS)))
    sel_q, sel_k, const_q, const_k = _gate_tables()
    scale = HEAD_DIM ** -0.5
    hn2 = _rms_fwd(h2, mix_norm_g[1:2], "mix1_norm")
    f_logit = _mm_nn(hn2, w_f, "fox_f")
    cp, sneg = _gate_scan(f_logit, bf_row, "fox_scan")
    qp, kp = _qk_proj(hn2, (_pad_heads(w_q), _pad_heads(w_k)), cp, (sel_q, sel_k), (const_q, const_k),
                      (scale * LOG2E, 1.0), "fox_qk")
    vv = _mm_nn(hn2, w_v, "fox_v", out_dtype=BF16)
    o, o32, lse = _attn_fwd_t(qp, kp, jnp.transpose(vv), "fox_attn")
    h3 = _mm_nn(o, w_o_g, "fox_o", res=h2)
    h4 = ffn_fwd(1, h3)

    dh, dh16, d_final, loss_row = _loss_head(h4, tgt, final_norm_g.reshape(1, D), "loss_head")
    loss = lax.psum(loss_row[0, 0], ("x", "y", "c"))

    rs_dst = lambda a, k: k
    rs_src = lambda a, p: _slot(p)
    me_idx = me.astype(jnp.int32).reshape(1)

    def rs_start(grads, name, cid):
        lands = [lax.empty((N_DEV - 1,) + g.shape[1:], BF16) for g in grads]
        return _direct_start(grads, lands, list(range(len(grads))), rs_dst, rs_src, loss_row, name, collective_id=cid)

    def rs_wait(st, after, name):
        n = len(st[2])
        return _direct_wait(st[0], st[1], st[2], st[3], list(range(n)), list(range(n)), rs_dst, rs_src, after, name)

    def ffn_bwd(l, dh, dh16, after=None):
        wgul, wdl = ffn_weights(l)
        h_in, hn, au, act = saved[f"ffn{l}"]
        dact = _mm_nt([dh16], wdl, f"ffn{l}_dact", out_dtype=BF16, after=after)
        d_wd = _mm_tn(act, dh16, f"ffn{l}_dwd", out_dtype=BF16)
        da, dup, d_cw, d_cb = _ffn_mid_bwd(au, dact, conv_w_full[l], ffn_conv_b[l:l + 1], f"ffn{l}_dmid")
        dhn = _mm_nt([da, dup], wgul, f"ffn{l}_dhn")
        d_wg = _mm_tn(hn, da, f"ffn{l}_dwg", blocked_w=FF // N_DEV, out_dtype=BF16)
        d_wu = _mm_tn(hn, dup, f"ffn{l}_dwu", blocked_w=FF // N_DEV, out_dtype=BF16)
        dh_in, dh_in16, d_norm = _rms_bwd(dhn, h_in, ffn_norm_g[l:l + 1], dh, f"ffn{l}_dnorm")
        big_g = [d_wg, d_wu, d_wd.reshape(N_DEV, FF // N_DEV, D)]
        return dh_in, dh_in16, big_g, dict(cw=d_cw, cb=d_cb, norm=d_norm)

    dh, dh16, big_ffn1, g_ffn1 = ffn_bwd(1, dh, dh16)

    do = _mm_nt([dh16], w_o_g, "fox_do", out_dtype=BF16)
    d_wo = _mm_tn(o, dh16, "fox_dwo", out_dtype=BF16)
    dq, dk, dv, dqe, dke = _attn_bwd(qp, kp, vv, o32, do, lse, scale, "fox_dattn")
    gate_lane = lambda e, r: jnp.pad(jnp.transpose(e[:, r::8, :].reshape(N_HEADS, T)), ((0, 0), (0, LANES - N_HEADS)))
    df, d_bf = _gate_scan_bwd(gate_lane(dqe, 0), gate_lane(dke, 3), sneg, "fox_dscan")
    dhn = _mm_nt([df], w_f, "fox_dhn_f")
    dhn = _mm_nt([dq, dk, dv], w_qkvf[:, :3 * D], "fox_dhn_qkv", add=dhn)
    d_wq = _mm_tn(hn2, dq, "fox_dwq", out_dtype=BF16)
    d_wk = _mm_tn(hn2, dk, "fox_dwk", out_dtype=BF16)
    d_wv = _mm_tn(hn2, dv, "fox_dwv", out_dtype=BF16)
    d_wf = _mm_tn(hn2, df, "fox_dwf", out_dtype=BF16)
    d_wqkvf = jnp.concatenate([d_wq, d_wk, d_wv, d_wf[:, :N_HEADS]], axis=1)
    dh, dh16, d_mix1 = _rms_bwd(dhn, h2, mix_norm_g[1:2], dh, "mix1_dnorm")
    st1 = rs_start([jnp.transpose(d_wqkvf.reshape(D, N_DEV, NQKVF // N_DEV), (1, 0, 2)),
                    d_wo.reshape(N_DEV, D // N_DEV, D)] + big_ffn1, "rs1_start", 2)

    dh, dh16, big_ffn0, g_ffn0 = ffn_bwd(0, dh, dh16, after=st1[4])
    st2 = rs_start(big_ffn0, "rs2_start", 3)

    dgu = _mm_nt([dh16], w_out_g, "gm_dgu", out_dtype=BF16, after=st2[4])
    d_wout = _mm_tn(gu, dh16, "gm_dwout", out_dtype=BF16)
    dz, d_lng, d_lnb, d_ws, d_bs = _sgu_bwd(z, dgu, gm_ln_g, gm_ln_b, gm_w_s[0], bs_col, "gm_dsgu")
    d_win = _mm_tn(hn0, dz, "gm_dwin", blocked_w=2 * E // N_DEV, out_dtype=BF16)
    st3 = rs_start([d_win, d_wout.reshape(N_DEV, E // N_DEV, D)], "rs3_start", 4)
    dhn = _mm_nt([dz], w_in_g, "gm_dhn", after=st3[4])
    dx, _, d_mix0 = _rms_bwd(dhn, h0, mix_norm_g[0:1], dh, "mix0_dnorm")

    own1, land1 = rs_wait(st1, dx, "rs1_wait")
    own2, land2 = rs_wait(st2, land1[0], "rs2_wait")
    cat1 = lambda a, b: jnp.concatenate([a, b], axis=1)
    big_out = {}

    def big_adamw(name, w, m, v, own, landed):
        shard2d = lambda a, c=own.shape[2]: a.reshape(-1, c)
        res = _sum_adamw([(own, [-1]), (landed, list(range(N_DEV - 1)))], shard2d(w), shard2d(m), shard2d(v),
                         f"adamw_{name}", sel=me_idx)
        big_out[name] = [t.reshape(w.shape) for t in res]

    big_adamw("fox_w_qkvf", fox_w_qkvf, m_fox_w_qkvf, v_fox_w_qkvf, own1[0], land1[0])
    big_adamw("fox_w_o", fox_w_o, m_fox_w_o, v_fox_w_o, own1[1], land1[1])
    big_adamw("ffn_w_gate", ffn_w_gate, m_ffn_w_gate, v_ffn_w_gate, cat1(own2[0], own1[2]), cat1(land2[0], land1[2]))
    big_adamw("ffn_w_up", ffn_w_up, m_ffn_w_up, v_ffn_w_up, cat1(own2[1], own1[3]), cat1(land2[1], land1[3]))
    big_adamw("ffn_w_down", ffn_w_down, m_ffn_w_down, v_ffn_w_down, cat1(own2[2], own1[4]), cat1(land2[2], land1[4]))

    small = [("mix_norm_g", mix_norm_g, m_mix_norm_g, v_mix_norm_g, jnp.concatenate([d_mix0, d_mix1], axis=0)),
             ("ffn_norm_g", ffn_norm_g, m_ffn_norm_g, v_ffn_norm_g, jnp.concatenate([g_ffn0["norm"], g_ffn1["norm"]], axis=0)),
             ("gm_ln_g", gm_ln_g, m_gm_ln_g, v_gm_ln_g, d_lng),
             ("gm_ln_b", gm_ln_b, m_gm_ln_b, v_gm_ln_b, d_lnb),
             ("gm_w_s", gm_w_s, m_gm_w_s, v_gm_w_s, d_ws),
             ("gm_b_s", gm_b_s, m_gm_b_s, v_gm_b_s, d_bs),
             ("fox_b_f", fox_b_f, m_fox_b_f, v_fox_b_f, d_bf[:, :N_HEADS]),
             ("ffn_conv_b", ffn_conv_b, m_ffn_conv_b, v_ffn_conv_b, jnp.concatenate([g_ffn0["cb"], g_ffn1["cb"]], axis=0)),
             ("final_norm_g", final_norm_g, m_final_norm_g, v_final_norm_g, d_final)]
    d_cw_full = jnp.stack([g_ffn0["cw"], g_ffn1["cw"]], axis=0)

    def small_rows(a):
        flat = a.astype(F32).reshape(-1)
        n = -(-flat.size // (8 * LANES)) * (8 * LANES)
        return jnp.pad(flat, (0, n - flat.size)).reshape(-1, LANES)

    s_rows = [small_rows(p[1]).shape[0] for p in small]
    s_off = np.concatenate([[0], np.cumsum(s_rows)]).tolist()
    cw_g_rows = small_rows(d_cw_full)
    zeros_cw = jnp.zeros_like(cw_g_rows)
    cat = lambda k: jnp.concatenate([small_rows(p[k]) for p in small] + [zeros_cw], axis=0)
    g_small = jnp.concatenate([small_rows(p[4]) for p in small] + [cw_g_rows], axis=0)
    (gs_all,) = _all_gather([g_small], "ag_small_grads")
    small_out = _sum_adamw([(gs_all, list(range(N_DEV)))], cat(1), cat(2), cat(3), "adamw_small")
    gs = small_out[0]

    g_cw_full = gs[s_off[-1]:].reshape(-1)[:d_cw_full.size].reshape(d_cw_full.shape)
    g_cw = lax.dynamic_slice_in_dim(g_cw_full, me * (FF // N_DEV), FF // N_DEV, axis=2)
    cw2 = lambda a: _pad_rows(_rows(a.astype(F32), LANES), 16)
    cw_out = _sum_adamw([(cw2(g_cw)[None], [0])], cw2(ffn_conv_w), cw2(m_ffn_conv_w), cw2(v_ffn_conv_w), "adamw_conv_w")

    own3, land3 = rs_wait(st3, cw_out[0], "rs3_wait")
    big_adamw("gm_w_in", gm_w_in, m_gm_w_in, v_gm_w_in, own3[0], land3[0])
    big_adamw("gm_w_out", gm_w_out, m_gm_w_out, v_gm_w_out, own3[1], land3[1])

    names = ["mix_norm_g", "ffn_norm_g", "gm_w_in", "gm_ln_g", "gm_ln_b", "gm_w_s", "gm_b_s", "gm_w_out", "fox_w_qkvf",
             "fox_b_f", "fox_w_o", "ffn_w_gate", "ffn_w_up", "ffn_conv_w", "ffn_conv_b", "ffn_w_down", "final_norm_g"]
    small_idx = {p[0]: k for k, p in enumerate(small)}

    def pick(kind, name):
        if name in big_out:
            return big_out[name][kind]
        if name == "ffn_conv_w":
            return cw_out[kind][:n_cw_rows].reshape(ffn_conv_w.shape)
        k = small_idx[name]
        shp = small[k][1].shape
        return small_out[kind][s_off[k]:s_off[k + 1]].reshape(-1)[:int(np.prod(shp))].reshape(shp)

    outs = [loss, dx.reshape(x.shape)]
    for kind in range(4):
        outs += [pick(kind, n) for n in names]
    return tuple(outs)
```

```python
import functools
import math

import numpy as np
import jax
import jax.numpy as jnp
from jax import lax
from jax.experimental import pallas as pl
from jax.experimental.pallas import tpu as pltpu

F32 = jnp.float32
BF16 = jnp.bfloat16
MESH = pl.DeviceIdType.MESH

N_HEADS = 16
HEAD_DIM = 64
CHUNK = 128
GM_GROUPS = 8
RMS_EPS = 1e-6
LN_EPS = 1e-5
ADAM_LR = 0.001
ADAM_B1 = 0.9
ADAM_B2 = 0.999
ADAM_EPS = 1e-08
ADAM_WD = 0.01
ADAM_STEP = 10
N_DEV = 8

LANES = 128
VMEM_BYTES_V7X = 64 * 1024 * 1024
VMEM_LIMIT = 56 * 1024 * 1024

TM = 512
TM_MM = 1024
TT = 1024
TQ = 512
TF = 512
KV_UNROLL_BWD = 2
KV_UNROLL = 2
MM_BLOCK_BYTES = 8 * 1024 * 1024
NEG = -1e30
LOG2E = math.log2(math.e)
LN2 = math.log(2.0)


def _cp(sem=None, vmem=VMEM_LIMIT):
    return pltpu.CompilerParams(dimension_semantics=sem, vmem_limit_bytes=vmem)


def _gelu(x):
    c = math.sqrt(2.0 / math.pi)
    return x * (0.5 * (1.0 + jnp.tanh(c * (x + 0.044715 * (x * x * x)))))


def _gelu_grad(x):
    c = math.sqrt(2.0 / math.pi)
    t = jnp.tanh(c * (x + 0.044715 * (x * x * x)))
    return 0.5 * (1.0 + t) + x * (0.5 * (1.0 - t * t)) * (c * (1.0 + 3.0 * 0.044715 * (x * x)))


def _sigmoid(x):
    return 1.0 / (1.0 + jnp.exp(-x))


def _dot_nt(a, b):
    return lax.dot_general(a, b, (((1,), (1,)), ((), ())), preferred_element_type=F32)


def _dot_tn(a, b):
    return lax.dot_general(a, b, (((0,), (0,)), ((), ())), preferred_element_type=F32)


def _rms_fwd(h, g, name, after=None):
    T, D = h.shape
    tm = min(TM, T)

    def body(h_ref, g_ref, *rest):
        o_ref = rest[-1]
        x = h_ref[...]
        r = lax.rsqrt(jnp.mean(x * x, axis=-1, keepdims=True) + RMS_EPS)
        o_ref[...] = ((x * r) * g_ref[...]).astype(BF16)

    in_specs = [pl.BlockSpec((tm, D), lambda i: (i, 0)), pl.BlockSpec((1, D), lambda i: (0, 0))]
    args = [h, g]
    if after is not None:
        in_specs.append(pl.BlockSpec(memory_space=pl.ANY))
        args.append(after)
    return pl.pallas_call(
        body, name=name, grid=(T // tm,),
        in_specs=in_specs,
        out_specs=pl.BlockSpec((tm, D), lambda i: (i, 0)),
        out_shape=jax.ShapeDtypeStruct((T, D), BF16),
        compiler_params=_cp(("parallel",)),
    )(*args)


def _rms_bwd(dhn, h, g, dres, name):
    T, D = h.shape
    tm = min(TM, T)

    def body(d_ref, h_ref, g_ref, r_ref, o_ref, ob_ref, dg_ref):
        x = h_ref[...]
        d = d_ref[...]
        r = lax.rsqrt(jnp.mean(x * x, axis=-1, keepdims=True) + RMS_EPS)
        dyg = d * g_ref[...]
        dot = jnp.mean(dyg * x, axis=-1, keepdims=True)
        dh = r_ref[...] + (r * dyg - x * ((r * r * r) * dot))
        o_ref[...] = dh
        ob_ref[...] = dh.astype(BF16)
        part = jnp.sum(d * (x * r), axis=0, keepdims=True)

        @pl.when(pl.program_id(0) == 0)
        def _():
            dg_ref[...] = part

        @pl.when(pl.program_id(0) != 0)
        def _():
            dg_ref[...] += part

    blk = pl.BlockSpec((tm, D), lambda i: (i, 0))
    row = pl.BlockSpec((1, D), lambda i: (0, 0))
    return pl.pallas_call(
        body, name=name, grid=(T // tm,),
        in_specs=[blk, blk, row, blk],
        out_specs=[blk, blk, row],
        out_shape=[jax.ShapeDtypeStruct((T, D), F32), jax.ShapeDtypeStruct((T, D), BF16),
                   jax.ShapeDtypeStruct((1, D), F32)],
        compiler_params=_cp(("arbitrary",)),
    )(dhn, h, g, dres)


def _loss_head(h, tgt, g, name):
    T, D = h.shape
    tm = min(TM, T)

    def body(h_ref, t_ref, g_ref, o_ref, ob_ref, dg_ref, l_ref):
        x = h_ref[...]
        gg = g_ref[...]
        r = lax.rsqrt(jnp.mean(x * x, axis=-1, keepdims=True) + RMS_EPS)
        xr = x * r
        e = xr * gg - t_ref[...]
        lpart = 0.5 * jnp.sum(jnp.mean(e * e, axis=-1, keepdims=True), axis=0, keepdims=True)
        dy = e * (1.0 / D)
        dyg = dy * gg
        dot = jnp.mean(dyg * x, axis=-1, keepdims=True)
        dh = r * dyg - x * ((r * r * r) * dot)
        o_ref[...] = dh
        ob_ref[...] = dh.astype(BF16)
        part = jnp.sum(dy * xr, axis=0, keepdims=True)
        lrow = jnp.broadcast_to(lpart, (1, LANES))

        @pl.when(pl.program_id(0) == 0)
        def _():
            dg_ref[...] = part
            l_ref[...] = lrow

        @pl.when(pl.program_id(0) != 0)
        def _():
            dg_ref[...] += part
            l_ref[...] += lrow

    blk = pl.BlockSpec((tm, D), lambda i: (i, 0))
    row = pl.BlockSpec((1, D), lambda i: (0, 0))
    return pl.pallas_call(
        body, name=name, grid=(T // tm,),
        in_specs=[blk, blk, row],
        out_specs=[blk, blk, row, pl.BlockSpec((1, LANES), lambda i: (0, 0))],
        out_shape=[jax.ShapeDtypeStruct((T, D), F32), jax.ShapeDtypeStruct((T, D), BF16),
                   jax.ShapeDtypeStruct((1, D), F32), jax.ShapeDtypeStruct((1, LANES), F32)],
        compiler_params=_cp(("arbitrary",)),
    )(h, tgt, g)


def _mm_nn(a, b, name, out_dtype=F32, res=None, norm_g=None):
    M, K = a.shape
    b3 = b if b.ndim == 3 else b[None]
    nb, _, w = b3.shape
    N = nb * w
    tm = min(TM_MM, M, max(256, MM_BLOCK_BYTES // (4 * N)))
    o_spec = pl.BlockSpec((tm, N), lambda i: (i, 0))
    in_specs = [pl.BlockSpec((tm, K), lambda i: (i, 0)), pl.BlockSpec((nb, K, w), lambda i: (0, 0, 0))]
    args = [a, b3]
    if res is not None:
        in_specs.append(o_spec)
        args.append(res)
    if norm_g is not None:
        in_specs.append(pl.BlockSpec((1, N), lambda i: (0, 0)))
        args.append(norm_g)
    n_out = 2 if norm_g is not None else 1

    def body(*refs):
        a_ref, b_ref = refs[0], refs[1]
        o_ref = refs[-n_out]
        av = a_ref[...]
        for j in range(nb):
            cols = slice(j * w, (j + 1) * w)
            acc = jnp.dot(av, b_ref[j], preferred_element_type=F32)
            if res is not None:
                acc = refs[2][:, cols] + acc
            o_ref[:, cols] = acc.astype(out_dtype)
        if norm_g is not None:
            x = o_ref[...]
            r = lax.rsqrt(jnp.mean(x * x, axis=-1, keepdims=True) + RMS_EPS)
            refs[-1][...] = ((x * r) * refs[3][...]).astype(BF16)

    out_shape = jax.ShapeDtypeStruct((M, N), out_dtype)
    if norm_g is None:
        out_specs, out_shapes = o_spec, out_shape
    else:
        out_specs, out_shapes = [o_spec, o_spec], [out_shape, jax.ShapeDtypeStruct((M, N), BF16)]
    return pl.pallas_call(
        body, name=name, grid=(M // tm,),
        in_specs=in_specs, out_specs=out_specs, out_shape=out_shapes,
        compiler_params=_cp(("parallel",)),
    )(*args)


def _mm_nt(a_list, b, name, out_dtype=F32, add=None, after=None, norm_bwd=None):
    M, kw = a_list[0].shape
    tm = min(TM, M)
    na = len(a_list)
    blocked = b.ndim == 3
    N = b.shape[1] if blocked else b.shape[0]
    b_spec = pl.BlockSpec(b.shape, lambda i: (0,) * b.ndim)
    o_spec = pl.BlockSpec((tm, N), lambda i: (i, 0))
    row_spec = pl.BlockSpec((1, N), lambda i: (0, 0))
    in_specs = [pl.BlockSpec((tm, kw), lambda i: (i, 0)) for _ in a_list] + [b_spec]
    args = list(a_list) + [b]
    if add is not None:
        in_specs.append(o_spec)
        args.append(add)
    n_in = len(args)
    if norm_bwd is not None:
        in_specs += [o_spec, row_spec, o_spec]
        args += list(norm_bwd)
    if after is not None:
        in_specs.append(pl.BlockSpec(memory_space=pl.ANY))
        args.append(after)
    n_args = len(args)

    def body(*refs):
        a_refs = refs[:na]
        b_ref = refs[na]
        acc = refs[na + 1][...] if add is not None else None
        for s, a_ref in enumerate(a_refs):
            if blocked:
                w = b_ref.shape[2]
                per = kw // w
                parts = [_dot_nt(a_ref[:, jj * w:(jj + 1) * w], b_ref[s * per + jj]) for jj in range(per)]
            else:
                parts = [_dot_nt(a_ref[...], b_ref[:, s * kw:(s + 1) * kw])]
            for part in parts:
                acc = part if acc is None else acc + part
        if norm_bwd is None:
            refs[n_args][...] = acc.astype(out_dtype)
            return
        h_ref, g_ref, r_ref = refs[n_in:n_in + 3]
        o_ref, ob_ref, dg_ref = refs[n_args:n_args + 3]
        x = h_ref[...]
        r = lax.rsqrt(jnp.mean(x * x, axis=-1, keepdims=True) + RMS_EPS)
        dyg = acc * g_ref[...]
        dot = jnp.mean(dyg * x, axis=-1, keepdims=True)
        dh = r_ref[...] + (r * dyg - x * ((r * r * r) * dot))
        o_ref[...] = dh
        ob_ref[...] = dh.astype(BF16)
        part_g = jnp.sum(acc * (x * r), axis=0, keepdims=True)

        @pl.when(pl.program_id(0) == 0)
        def _():
            dg_ref[...] = part_g

        @pl.when(pl.program_id(0) != 0)
        def _():
            dg_ref[...] += part_g

    if norm_bwd is None:
        out_specs, out_shapes, sem = o_spec, jax.ShapeDtypeStruct((M, N), out_dtype), ("parallel",)
    else:
        out_specs = [o_spec, o_spec, row_spec]
        out_shapes = [jax.ShapeDtypeStruct((M, N), F32), jax.ShapeDtypeStruct((M, N), BF16),
                      jax.ShapeDtypeStruct((1, N), F32)]
        sem = ("arbitrary",)
    return pl.pallas_call(
        body, name=name, grid=(M // tm,),
        in_specs=in_specs, out_specs=out_specs, out_shape=out_shapes,
        compiler_params=_cp(sem),
    )(*args)


def _mm_tn(x, y, name, blocked_w=None, out_dtype=F32):
    T, Kx = x.shape
    N = y.shape[1]
    tt = min(TT, T)
    nt = T // tt
    tkx = min(Kx, max(LANES, MM_BLOCK_BYTES // (4 * N)))
    if blocked_w is not None:
        blk_shape, full_shape = (N // blocked_w, tkx, blocked_w), (N // blocked_w, Kx, blocked_w)
        o_spec = pl.BlockSpec(blk_shape, lambda i, t: (0, i, 0))
    else:
        blk_shape, full_shape = (tkx, N), (Kx, N)
        o_spec = pl.BlockSpec(blk_shape, lambda i, t: (i, 0))

    def body(x_ref, y_ref, o_ref, acc_ref):
        part = _dot_tn(x_ref[...], y_ref[...])
        t = pl.program_id(1)
        if blocked_w is None:
            pieces = [(slice(None), part)]
        else:
            pieces = [(j, part[:, j * blocked_w:(j + 1) * blocked_w]) for j in range(N // blocked_w)]

        @pl.when(t == 0)
        def _():
            for idx, pj in pieces:
                acc_ref[idx] = pj

        @pl.when(t != 0)
        def _():
            for idx, pj in pieces:
                acc_ref[idx] += pj

        @pl.when(t == nt - 1)
        def _():
            o_ref[...] = acc_ref[...].astype(out_dtype)

    return pl.pallas_call(
        body, name=name, grid=(Kx // tkx, nt),
        in_specs=[pl.BlockSpec((tt, tkx), lambda i, t: (t, i)),
                  pl.BlockSpec((tt, N), lambda i, t: (t, 0))],
        out_specs=o_spec, out_shape=jax.ShapeDtypeStruct(full_shape, out_dtype),
        scratch_shapes=[pltpu.VMEM(blk_shape, F32)],
        compiler_params=_cp(("parallel", "arbitrary")),
    )(x, y)


def _sgu_pieces(z, lng, lnb, wc, bs_ref):
    E = z.shape[1] // 2
    gd = E // GM_GROUPS
    zu, zv = z[:, :E], z[:, E:]
    u = _gelu(zu)
    v = _gelu(zv)
    mu = jnp.mean(v, axis=-1, keepdims=True)
    xc = v - mu
    rs = lax.rsqrt(jnp.mean(xc * xc, axis=-1, keepdims=True) + LN_EPS)
    xhat = xc * rs
    vln = xhat * lng + lnb
    s = []
    for g in range(GM_GROUPS):
        vg = vln[:, g * gd:(g + 1) * gd].astype(BF16)
        s.append(jnp.dot(wc[g], vg, preferred_element_type=F32) + bs_ref[g])
    return zu, zv, u, xhat, rs, vln, s


def _causal_ws(ws_ref):
    t = lax.broadcasted_iota(jnp.int32, (CHUNK, CHUNK), 0)
    s = lax.broadcasted_iota(jnp.int32, (CHUNK, CHUNK), 1)
    tri = t >= s
    return [jnp.where(tri, ws_ref[g], 0.0).astype(BF16) for g in range(GM_GROUPS)], tri


def _sgu_fwd(z, lng, lnb, ws, bs, name):
    T, E2 = z.shape
    E = E2 // 2
    gd = E // GM_GROUPS
    tm = min(2 * CHUNK, T)

    def body(z_ref, lng_ref, lnb_ref, ws_ref, bs_ref, o_ref):
        wc, _ = _causal_ws(ws_ref)
        for c in range(tm // CHUNK):
            rows = slice(c * CHUNK, (c + 1) * CHUNK)
            _, _, u, _, _, _, s = _sgu_pieces(z_ref[rows, :], lng_ref[...], lnb_ref[...], wc, bs_ref)
            for g in range(GM_GROUPS):
                cols = slice(g * gd, (g + 1) * gd)
                o_ref[rows, cols] = (u[:, cols] * s[g]).astype(BF16)

    full = lambda shape: pl.BlockSpec(shape, lambda i: (0,) * len(shape))
    return pl.pallas_call(
        body, name=name, grid=(T // tm,),
        in_specs=[pl.BlockSpec((tm, E2), lambda i: (i, 0)), full((1, E)), full((1, E)),
                  full((GM_GROUPS, CHUNK, CHUNK)), full((GM_GROUPS, CHUNK, 1))],
        out_specs=pl.BlockSpec((tm, E), lambda i: (i, 0)),
        out_shape=jax.ShapeDtypeStruct((T, E), BF16),
        compiler_params=_cp(("parallel",)),
    )(z, lng, lnb, ws, bs)


def _sgu_bwd(z, dg, lng, lnb, ws, bs, name):
    T, E2 = z.shape
    E = E2 // 2
    gd = E // GM_GROUPS
    tm = min(2 * CHUNK, T)
    nsteps = T // tm

    def body(z_ref, dg_ref, lng_ref, lnb_ref, ws_ref, bs_ref, dz_ref, dlng_ref, dlnb_ref, dws_ref, dbs_ref):
        i = pl.program_id(0)

        @pl.when(i == 0)
        def _():
            dlng_ref[...] = jnp.zeros_like(dlng_ref)
            dlnb_ref[...] = jnp.zeros_like(dlnb_ref)
            dws_ref[...] = jnp.zeros_like(dws_ref)
            dbs_ref[...] = jnp.zeros_like(dbs_ref)

        wc, tri = _causal_ws(ws_ref)
        lng_v = lng_ref[...]
        for c in range(tm // CHUNK):
            rows = slice(c * CHUNK, (c + 1) * CHUNK)
            zu, zv, u, xhat, rs, vln, s = _sgu_pieces(z_ref[rows, :], lng_v, lnb_ref[...], wc, bs_ref)
            dgc = dg_ref[rows, :].astype(F32)
            du, dvln = [], []
            for g in range(GM_GROUPS):
                cols = slice(g * gd, (g + 1) * gd)
                dgg = dgc[:, cols]
                du.append(dgg * s[g])
                ds = dgg * u[:, cols]
                dsb = ds.astype(BF16)
                dws_ref[g] += _dot_nt(dsb, vln[:, cols].astype(BF16))
                dbs_ref[g] += jnp.sum(ds, axis=-1, keepdims=True)
                dvln.append(_dot_tn(wc[g], dsb))
            du = jnp.concatenate(du, axis=1)
            dvln = jnp.concatenate(dvln, axis=1)
            dlng_ref[...] += jnp.sum(dvln * xhat, axis=0, keepdims=True)
            dlnb_ref[...] += jnp.sum(dvln, axis=0, keepdims=True)
            dxh = dvln * lng_v
            m1 = jnp.mean(dxh, axis=-1, keepdims=True)
            m2 = jnp.mean(dxh * xhat, axis=-1, keepdims=True)
            dv = rs * (dxh - m1 - xhat * m2)
            dz_ref[rows, :E] = (du * _gelu_grad(zu)).astype(BF16)
            dz_ref[rows, E:] = (dv * _gelu_grad(zv)).astype(BF16)

        @pl.when(i == nsteps - 1)
        def _():
            for g in range(GM_GROUPS):
                dws_ref[g] = jnp.where(tri, dws_ref[g], 0.0)

    full = lambda shape: pl.BlockSpec(shape, lambda i: (0,) * len(shape))
    return pl.pallas_call(
        body, name=name, grid=(nsteps,),
        in_specs=[pl.BlockSpec((tm, E2), lambda i: (i, 0)), pl.BlockSpec((tm, E), lambda i: (i, 0)),
                  full((1, E)), full((1, E)), full((GM_GROUPS, CHUNK, CHUNK)), full((GM_GROUPS, CHUNK, 1))],
        out_specs=[pl.BlockSpec((tm, E2), lambda i: (i, 0)), full((1, E)), full((1, E)),
                   full((GM_GROUPS, CHUNK, CHUNK)), full((GM_GROUPS, CHUNK, 1))],
        out_shape=[jax.ShapeDtypeStruct((T, E2), BF16), jax.ShapeDtypeStruct((1, E), F32),
                   jax.ShapeDtypeStruct((1, E), F32), jax.ShapeDtypeStruct((GM_GROUPS, CHUNK, CHUNK), F32),
                   jax.ShapeDtypeStruct((GM_GROUPS, CHUNK, 1), F32)],
        compiler_params=_cp(("arbitrary",)),
    )(z, dg, lng, lnb, ws, bs)


HALO = 16


def _conv_taps(a_ext, w_ref, b_ref):
    n = a_ext.shape[0]
    am1 = pltpu.roll(a_ext, 1, 0)
    am2 = pltpu.roll(a_ext, 2, 0)
    del n
    return ((b_ref[...] + am2 * w_ref[0:1, :]) + am1 * w_ref[1:2, :]) + a_ext * w_ref[2:3, :], am1, am2


def _ffn_mid_fwd(au, cw, cb, name):
    T, F = au.shape[0], au.shape[1] // 2
    tm, tf = min(TM, T), min(TF, F)
    hb = tm // HALO
    nf = F // tf

    def body(a_ref, ap_ref, u_ref, w_ref, b_ref, o_ref):
        i = pl.program_id(1)
        prev = jnp.where(i == 0, 0.0, ap_ref[...])
        ext = jnp.concatenate([prev, a_ref[...]], axis=0)
        conv, _, _ = _conv_taps(ext, w_ref, b_ref)
        conv = conv[HALO:, :]
        o_ref[...] = ((conv * _sigmoid(conv)) * u_ref[...]).astype(BF16)

    main = pl.BlockSpec((tm, tf), lambda f, i: (i, f))
    return pl.pallas_call(
        body, name=name, grid=(nf, T // tm),
        in_specs=[main, pl.BlockSpec((HALO, tf), lambda f, i: (jnp.maximum(i * hb - 1, 0), f)),
                  pl.BlockSpec((tm, tf), lambda f, i: (i, nf + f)),
                  pl.BlockSpec((3, tf), lambda f, i: (0, f)), pl.BlockSpec((1, tf), lambda f, i: (0, f))],
        out_specs=main, out_shape=jax.ShapeDtypeStruct((T, F), BF16),
        compiler_params=_cp(("parallel", "parallel")),
    )(au, au, au, cw, cb)


def _ffn_mid_bwd(au, dact, cw, cb, name):
    T, F = au.shape[0], au.shape[1] // 2
    tm, tf = min(TM, T), min(TF, F)
    hb = tm // HALO
    nt = T // tm
    nf = F // tf
    last_h = T // HALO - 1

    def body(a_ref, ap_ref, an_ref, u_ref, un_ref, d_ref, dn_ref, w_ref, b_ref, da_ref, du_ref, dcw_ref, dcb_ref):
        i = pl.program_id(1)
        prev = jnp.where(i == 0, 0.0, ap_ref[...])
        a_main = a_ref[...]
        a_ext = jnp.concatenate([prev, a_main, an_ref[...]], axis=0)
        conv, am1, am2 = _conv_taps(a_ext, w_ref, b_ref)
        conv = conv[HALO:, :]
        sig = _sigmoid(conv)
        u_ext = jnp.concatenate([u_ref[...], un_ref[...]], axis=0)
        d_ext = jnp.concatenate([d_ref[...], dn_ref[...]], axis=0).astype(F32)
        n = tm + HALO
        row = lax.broadcasted_iota(jnp.int32, (n, 1), 0)
        live = jnp.logical_or(row < tm, i < nt - 1)
        dconv = jnp.where(live, d_ext * u_ext * (sig * (1.0 + conv * (1.0 - sig))), 0.0)
        du_ref[...] = (d_ext[:tm, :] * (conv[:tm, :] * sig[:tm, :])).astype(BF16)
        dp1 = pltpu.roll(dconv, n - 1, 0)[:tm, :]
        dp2 = pltpu.roll(dconv, n - 2, 0)[:tm, :]
        dc = dconv[:tm, :]
        da_ref[...] = ((dc * w_ref[2:3, :] + dp1 * w_ref[1:2, :]) + dp2 * w_ref[0:1, :]).astype(BF16)
        g2 = jnp.sum(dc * a_main, axis=0, keepdims=True)
        g1 = jnp.sum(dc * am1[HALO:HALO + tm, :], axis=0, keepdims=True)
        g0 = jnp.sum(dc * am2[HALO:HALO + tm, :], axis=0, keepdims=True)
        gb = jnp.sum(dc, axis=0, keepdims=True)

        @pl.when(i == 0)
        def _():
            dcw_ref[...] = jnp.zeros_like(dcw_ref)
            dcb_ref[...] = jnp.zeros_like(dcb_ref)

        dcw_ref[0:1, :] += g0
        dcw_ref[1:2, :] += g1
        dcw_ref[2:3, :] += g2
        dcb_ref[...] += gb

    main = pl.BlockSpec((tm, tf), lambda f, i: (i, f))
    prev = pl.BlockSpec((HALO, tf), lambda f, i: (jnp.maximum(i * hb - 1, 0), f))
    nxt = pl.BlockSpec((HALO, tf), lambda f, i: (jnp.minimum((i + 1) * hb, last_h), f))
    main_u = pl.BlockSpec((tm, tf), lambda f, i: (i, nf + f))
    nxt_u = pl.BlockSpec((HALO, tf), lambda f, i: (jnp.minimum((i + 1) * hb, last_h), nf + f))
    return pl.pallas_call(
        body, name=name, grid=(nf, nt),
        in_specs=[main, prev, nxt, main_u, nxt_u, main, nxt,
                  pl.BlockSpec((3, tf), lambda f, i: (0, f)), pl.BlockSpec((1, tf), lambda f, i: (0, f))],
        out_specs=[main, main, pl.BlockSpec((3, tf), lambda f, i: (0, f)), pl.BlockSpec((1, tf), lambda f, i: (0, f))],
        out_shape=[jax.ShapeDtypeStruct((T, F), BF16), jax.ShapeDtypeStruct((T, F), BF16),
                   jax.ShapeDtypeStruct((3, F), F32), jax.ShapeDtypeStruct((1, F), F32)],
        compiler_params=_cp(("parallel", "arbitrary")),
    )(au, au, au, au, au, dact, dact, cw, cb)


def _split3(x):
    hi = x.astype(BF16)
    r1 = x - hi.astype(F32)
    mid = r1.astype(BF16)
    lo = (r1 - mid.astype(F32)).astype(BF16)
    return hi, mid, lo


def _tri_ones(n, upper):
    r = lax.broadcasted_iota(jnp.int32, (n, n), 0)
    c = lax.broadcasted_iota(jnp.int32, (n, n), 1)
    return jnp.where((r <= c) if upper else (r >= c), 1.0, 0.0).astype(BF16)


def _gate_scan(f, bf, name):
    T = f.shape[0]
    tm = min(256, T)

    def body(f_ref, b_ref, cp_ref, sn_ref, carry_ref):
        i = pl.program_id(0)

        @pl.when(i == 0)
        def _():
            carry_ref[...] = jnp.zeros_like(carry_ref)

        x = f_ref[...] + b_ref[...]
        e = jnp.exp(-jnp.abs(x))
        logf = jnp.minimum(x, 0.0) - jnp.log(1.0 + e)
        sn_ref[...] = jnp.where(x >= 0.0, e / (1.0 + e), 1.0 / (1.0 + e))
        tri = _tri_ones(tm, upper=False)
        c = carry_ref[...]
        for piece in _split3(logf):
            c = c + jnp.dot(tri, piece, preferred_element_type=F32)
        carry_ref[...] += jnp.sum(logf, axis=0, keepdims=True)
        hi, mid, lo = _split3(c * LOG2E)
        cp_ref[:, 0:LANES] = hi
        cp_ref[:, LANES:2 * LANES] = mid
        cp_ref[:, 2 * LANES:3 * LANES] = lo

    return pl.pallas_call(
        body, name=name, grid=(T // tm,),
        in_specs=[pl.BlockSpec((tm, LANES), lambda i: (i, 0)), pl.BlockSpec((1, LANES), lambda i: (0, 0))],
        out_specs=[pl.BlockSpec((tm, 3 * LANES), lambda i: (i, 0)), pl.BlockSpec((tm, LANES), lambda i: (i, 0))],
        out_shape=[jax.ShapeDtypeStruct((T, 3 * LANES), BF16), jax.ShapeDtypeStruct((T, LANES), F32)],
        scratch_shapes=[pltpu.VMEM((1, LANES), F32)],
        compiler_params=_cp(("arbitrary",)),
    )(f, bf)


def _gate_scan_bwd(dcq, dck, sneg, name):
    T = dcq.shape[0]
    tm = min(256, T)
    n = T // tm

    def body(dcq_ref, dck_ref, sn_ref, df_ref, db_ref, carry_ref):
        i = pl.program_id(0)

        @pl.when(i == 0)
        def _():
            carry_ref[...] = jnp.zeros_like(carry_ref)
            db_ref[...] = jnp.zeros_like(db_ref)

        tri = _tri_ones(tm, upper=True)
        dcb = dcq_ref[...] - dck_ref[...]
        acc = carry_ref[...]
        for piece in _split3(dcb):
            acc = acc + jnp.dot(tri, piece, preferred_element_type=F32)
        carry_ref[...] += jnp.sum(dcb, axis=0, keepdims=True)
        df = acc * sn_ref[...]
        df_ref[...] = df.astype(BF16)
        db_ref[...] += jnp.sum(df, axis=0, keepdims=True)

    rev = pl.BlockSpec((tm, LANES), lambda i: (n - 1 - i, 0))
    return pl.pallas_call(
        body, name=name, grid=(n,),
        in_specs=[rev, rev, rev],
        out_specs=[rev, pl.BlockSpec((1, LANES), lambda i: (0, 0))],
        out_shape=[jax.ShapeDtypeStruct((T, LANES), BF16), jax.ShapeDtypeStruct((1, LANES), F32)],
        scratch_shapes=[pltpu.VMEM((1, LANES), F32)],
        compiler_params=_cp(("arbitrary",)),
    )(dcq, dck, sneg)


def _qk_proj(hn, w_pads, cp, sels, consts, scales, name):
    T, D = hn.shape
    H = w_pads[0].shape[1] // LANES
    tm = min(TM, T)

    def body(a_ref, cp_ref, wq_ref, wk_ref, sq_ref, sk_ref, cq_ref, ck_ref, qo_ref, ko_ref):
        a = a_ref[...]
        cpv = cp_ref[...]
        for w_ref, sel_ref, c_ref, o_ref, scale in ((wq_ref, sq_ref, cq_ref, qo_ref, scales[0]),
                                                    (wk_ref, sk_ref, ck_ref, ko_ref, scales[1])):
            for p in range(H // 2):
                acc = jnp.dot(a, w_ref[:, p * 2 * LANES:(p + 1) * 2 * LANES], preferred_element_type=F32)
                if scale != 1.0:
                    acc = acc * scale
                acc = acc + jnp.dot(cpv, sel_ref[p], preferred_element_type=F32) + c_ref[p]
                o_ref[2 * p] = acc[:, :LANES].astype(BF16)
                o_ref[2 * p + 1] = acc[:, LANES:].astype(BF16)

    whole = lambda t: pl.BlockSpec(t.shape, lambda i: (0,) * t.ndim)
    out = jax.ShapeDtypeStruct((H, T, LANES), BF16)
    o_spec = pl.BlockSpec((H, tm, LANES), lambda i: (0, i, 0))
    return pl.pallas_call(
        body, name=name, grid=(T // tm,),
        in_specs=[pl.BlockSpec((tm, D), lambda i: (i, 0)), pl.BlockSpec((tm, 3 * LANES), lambda i: (i, 0)),
                  whole(w_pads[0]), whole(w_pads[1]), whole(sels[0]), whole(sels[1]), whole(consts[0]), whole(consts[1])],
        out_specs=[o_spec, o_spec], out_shape=[out, out],
        compiler_params=_cp(("parallel",)),
    )(hn, cp, w_pads[0], w_pads[1], sels[0], sels[1], consts[0], consts[1])


def _lane_lo():
    return lax.broadcasted_iota(jnp.int32, (1, LANES), 1) < HEAD_DIM


def _attn_fwd(qp, kp, v, name):
    H, T, _ = qp.shape
    tq = min(TQ, T)
    nrep = tq // LANES
    n_parts = 4 if tq % 512 == 0 else 1
    rows = tq // n_parts

    def body(q_ref, k_ref, v_ref, o_ref, o32_ref, lse_ref, m_sc, acc_sc):
        i = pl.program_id(1)
        m_sc[...] = jnp.full(m_sc.shape, NEG, F32)
        acc_sc[...] = jnp.zeros_like(acc_sc)
        ones_col = jnp.where(lax.broadcasted_iota(jnp.int32, (tq, LANES), 1) == 0, 1.0, 0.0).astype(BF16)

        def step(j, masked):
            off = pl.multiple_of(j * tq, tq)
            vaug = jnp.concatenate([v_ref[pl.ds(off, tq), :], ones_col], axis=1)
            chains = [(h, rp) for h in range(2) for rp in range(n_parts)]
            s_all = [_dot_nt(q_ref[h, rp * rows:(rp + 1) * rows, :], k_ref[h, pl.ds(off, tq), :]) for h, rp in chains]
            for (h, rp), s in zip(chains, s_all):
                rsl = slice(rp * rows, (rp + 1) * rows)
                tiles = [s[:, c * LANES:(c + 1) * LANES] for c in range(nrep)]
                if masked:
                    r = lax.broadcasted_iota(jnp.int32, (rows, LANES), 0) + rp * rows
                    cc = lax.broadcasted_iota(jnp.int32, (rows, LANES), 1)
                    tiles = [jnp.where(r >= cc + c * LANES, t, NEG) for c, t in enumerate(tiles)]
                mt = tiles[0]
                for t in tiles[1:]:
                    mt = jnp.maximum(mt, t)
                m_prev = m_sc[h, rsl, :]
                m_new = jnp.maximum(m_prev, jnp.max(mt, axis=-1, keepdims=True))
                alpha = jnp.exp2(m_prev - m_new)
                p16 = jnp.concatenate([jnp.exp2(t - m_new).astype(BF16) for t in tiles], axis=1)
                pv = jnp.dot(p16, vaug, preferred_element_type=F32)
                acc_sc[h, rsl, :] = jnp.concatenate([alpha, alpha], axis=1) * acc_sc[h, rsl, :] + pv
                m_sc[h, rsl, :] = m_new

        def loop_body(j, carry):
            step(j, False)
            return carry

        lax.fori_loop(0, i, loop_body, 0)
        step(i, True)
        lo = _lane_lo()
        acc0, acc1 = acc_sc[0], acc_sc[1]
        l0 = jnp.sum(acc0[:, LANES:], axis=-1, keepdims=True)
        l1 = jnp.sum(acc1[:, LANES:], axis=-1, keepdims=True)
        o = jnp.where(lo, acc0[:, :LANES] / l0, acc1[:, :LANES] / l1)
        o_ref[...] = o.astype(BF16)
        o32_ref[...] = o
        lse_ref[...] = jnp.where(lo, m_sc[0] + jnp.log(l0) * LOG2E, m_sc[1] + jnp.log(l1) * LOG2E)

    oblk = pl.BlockSpec((tq, LANES), lambda p, i: (i, p))
    return pl.pallas_call(
        body, name=name, grid=(H // 2, T // tq),
        in_specs=[pl.BlockSpec((2, tq, LANES), lambda p, i: (p, i, 0)),
                  pl.BlockSpec((2, T, LANES), lambda p, i: (p, 0, 0)),
                  pl.BlockSpec((T, LANES), lambda p, i: (0, p))],
        out_specs=[oblk, oblk, pl.BlockSpec((None, tq, LANES), lambda p, i: (p, i, 0))],
        out_shape=[jax.ShapeDtypeStruct((T, H * HEAD_DIM), BF16), jax.ShapeDtypeStruct((T, H * HEAD_DIM), F32),
                   jax.ShapeDtypeStruct((H // 2, T, LANES), F32)],
        scratch_shapes=[pltpu.VMEM((2, tq, LANES), F32), pltpu.VMEM((2, tq, 2 * LANES), F32)],
        compiler_params=_cp(("parallel", "arbitrary")),
    )(qp, kp, v)


def _attn_fwd_t(qp, kp, vt, name):
    H, T, _ = qp.shape
    tq = min(TQ, T)
    hd = HEAD_DIM
    ext = hd + 16

    def body(q_ref, k_ref, vt_ref, o_ref, o32_ref, lse_ref, m_sc, acc_sc):
        i = pl.program_id(1)
        m_sc[...] = jnp.full(m_sc.shape, NEG, F32)
        acc_sc[...] = jnp.zeros_like(acc_sc)
        q_t = [jnp.transpose(q_ref[h].astype(F32)).astype(BF16) for h in range(2)]
        ones_rows = jnp.where(lax.broadcasted_iota(jnp.int32, (16, tq), 0) == 0, 1.0, 0.0).astype(BF16)

        def steps(blocks):
            offs = [pl.multiple_of(j * tq, tq) for j, _ in blocks]
            s_all = [[jnp.dot(k_ref[h, pl.ds(off, tq), :], q_t[h], preferred_element_type=F32) for h in range(2)]
                     for off in offs]
            for (j, masked), off, s_blk in zip(blocks, offs, s_all):
                for h in range(2):
                    s = s_blk[h]
                    if masked:
                        kr = lax.broadcasted_iota(jnp.int32, (tq, tq), 0)
                        qc = lax.broadcasted_iota(jnp.int32, (tq, tq), 1)
                        s = jnp.where(qc >= kr, s, NEG)
                    m_prev = m_sc[h]
                    m_new = jnp.maximum(m_prev, jnp.max(s, axis=0, keepdims=True))
                    alpha = jnp.exp2(m_prev - m_new)
                    p16 = jnp.exp2(s - m_new).astype(BF16)
                    v_aug = jnp.concatenate([vt_ref[h * hd:(h + 1) * hd, pl.ds(off, tq)], ones_rows], axis=0)
                    pv = jnp.dot(v_aug, p16, preferred_element_type=F32)
                    acc_sc[h] = alpha * acc_sc[h] + pv
                    m_sc[h] = m_new

        def group_body(t, carry):
            steps([(KV_UNROLL * t + u, False) for u in range(KV_UNROLL)])
            return carry

        lax.fori_loop(0, i // KV_UNROLL, group_body, 0)
        for rem in range(KV_UNROLL):

            @pl.when(i % KV_UNROLL == rem)
            def _(rem=rem):
                steps([(i - rem + u, u == rem) for u in range(rem + 1)])

        o_t, lse_t = [], []
        for h in range(2):
            acc = acc_sc[h]
            l = acc[hd:hd + 1, :]
            o_t.append(acc[:hd, :] / l)
            lse_t.append(jnp.broadcast_to(m_sc[h] + jnp.log(l) * LOG2E, (hd, tq)))
        o = jnp.transpose(jnp.concatenate(o_t, axis=0))
        o_ref[...] = o.astype(BF16)
        o32_ref[...] = o
        lse_ref[...] = jnp.transpose(jnp.concatenate(lse_t, axis=0))

    oblk = pl.BlockSpec((tq, LANES), lambda p, i: (i, p))
    return pl.pallas_call(
        body, name=name, grid=(H // 2, T // tq),
        in_specs=[pl.BlockSpec((2, tq, LANES), lambda p, i: (p, i, 0)),
                  pl.BlockSpec((2, T, LANES), lambda p, i: (p, 0, 0)),
                  pl.BlockSpec((2 * hd, T), lambda p, i: (p, 0))],
        out_specs=[oblk, oblk, pl.BlockSpec((None, tq, LANES), lambda p, i: (p, i, 0))],
        out_shape=[jax.ShapeDtypeStruct((T, H * HEAD_DIM), BF16), jax.ShapeDtypeStruct((T, H * HEAD_DIM), F32),
                   jax.ShapeDtypeStruct((H // 2, T, LANES), F32)],
        scratch_shapes=[pltpu.VMEM((2, 1, tq), F32), pltpu.VMEM((2, ext, tq), F32)],
        compiler_params=_cp(("parallel", "arbitrary")),
    )(qp, kp, vt)


def _attn_bwd(qp, kp, v, o, do, lse, scale, name):
    H, T, _ = qp.shape
    tq = min(TQ, T)
    nq = T // tq
    nrep = tq // LANES

    def body(q_ref, k_ref, v_ref, o_ref, do_ref, lse_ref, dq_ref, dk_ref, dv_ref, dqe_ref, dke_ref, dk_sc, dv_sc, dq_sc):
        i = pl.program_id(1)

        @pl.when(i == 0)
        def _():
            dk_sc[...] = jnp.zeros_like(dk_sc)
            dv_sc[...] = jnp.zeros_like(dv_sc)

        dq_sc[...] = jnp.zeros_like(dq_sc)

        lo = _lane_lo()
        dob = do_ref[...]
        dof = dob.astype(F32)
        prod = dof * o_ref[...].astype(F32)
        lse2 = lse_ref[...]
        lse2_sw = pltpu.roll(lse2, HEAD_DIM, 1)
        zero = jnp.zeros_like(dob)
        do_h = [jnp.where(lo, dob, zero), jnp.where(lo, zero, dob)]
        rep = lambda col: jnp.broadcast_to(col, (tq, LANES))
        delta = [rep(jnp.sum(jnp.where(lo, prod, 0.0), axis=-1, keepdims=True)),
                 rep(jnp.sum(jnp.where(lo, 0.0, prod), axis=-1, keepdims=True))]
        lse_h = [jnp.where(lo, lse2, lse2_sw), jnp.where(lo, lse2_sw, lse2)]
        qs = [q_ref[0], q_ref[1]]
        tr16 = lambda a: jnp.transpose(a.astype(F32)).astype(BF16)
        q_t = [tr16(qs[0]), tr16(qs[1])]
        do_t = [tr16(do_h[0]), tr16(do_h[1])]

        def steps(blocks):
            offs = [pl.multiple_of(j * tq, tq) for j, _ in blocks]
            vblks = [v_ref[pl.ds(off, tq), :] for off in offs]
            kblks = [[k_ref[h, pl.ds(off, tq), :] for h in range(2)] for off in offs]
            s_all = [[_dot_nt(qs[h], kb[h]) for h in range(2)] for kb in kblks]
            dp_all = [[_dot_nt(do_h[h], vb) for h in range(2)] for vb in vblks]
            for b, ((j, masked), off) in enumerate(zip(blocks, offs)):
                dv_add = None
                for h in range(2):
                    kblk, s, dp = kblks[b][h], s_all[b][h], dp_all[b][h]
                    p16, ds16 = [], []
                    for c in range(nrep):
                        cols = slice(c * LANES, (c + 1) * LANES)
                        p = jnp.exp2(s[:, cols] - lse_h[h])
                        if masked:
                            r = lax.broadcasted_iota(jnp.int32, (tq, LANES), 0)
                            cc = lax.broadcasted_iota(jnp.int32, (tq, LANES), 1)
                            p = jnp.where(r >= cc + c * LANES, p, 0.0)
                        p16.append(p.astype(BF16))
                        ds16.append((p * (dp[:, cols] - delta[h])).astype(BF16))
                    p16 = jnp.concatenate(p16, axis=1)
                    dsb = jnp.concatenate(ds16, axis=1)
                    dq_sc[h] += jnp.dot(dsb, kblk, preferred_element_type=F32)
                    dk_sc[h, :, pl.ds(off, tq)] += jnp.dot(q_t[h], dsb, preferred_element_type=F32)
                    pv = jnp.dot(do_t[h], p16, preferred_element_type=F32)
                    dv_add = pv if dv_add is None else dv_add + pv
                dv_sc[:, pl.ds(off, tq)] += dv_add

        def group_body(t, carry):
            steps([(KV_UNROLL_BWD * t + u, False) for u in range(KV_UNROLL_BWD)])
            return carry

        lax.fori_loop(0, i // KV_UNROLL_BWD, group_body, 0)
        for rem in range(KV_UNROLL_BWD):

            @pl.when(i % KV_UNROLL_BWD == rem)
            def _(rem=rem):
                steps([(i - rem + u, u == rem) for u in range(rem + 1)])

        dq0, dq1 = dq_sc[0], dq_sc[1]
        dq_ref[...] = (jnp.where(lo, dq0, pltpu.roll(dq1, HEAD_DIM, 1)) * scale).astype(BF16)
        dqe_ref[0:8, :] = jnp.transpose(dq0)[HEAD_DIM:HEAD_DIM + 8, :]
        dqe_ref[8:16, :] = jnp.transpose(dq1)[HEAD_DIM:HEAD_DIM + 8, :]

        @pl.when(i == nq - 1)
        def _():
            for h in range(2):
                dke_ref[8 * h:8 * h + 8, :] = dk_sc[h, HEAD_DIM:HEAD_DIM + 8, :]
            for cb in range(nq):
                tok = slice(cb * tq, (cb + 1) * tq)
                dk0 = jnp.transpose(dk_sc[0, :, tok])
                dk1 = jnp.transpose(dk_sc[1, :, tok])
                dk_ref[tok, :] = (jnp.where(lo, dk0, pltpu.roll(dk1, HEAD_DIM, 1)) * LN2).astype(BF16)
                dv_ref[tok, :] = jnp.transpose(dv_sc[:, tok]).astype(BF16)

    qblk = pl.BlockSpec((tq, LANES), lambda p, i: (i, p))
    pair = pl.BlockSpec((T, LANES), lambda p, i: (0, p))
    tok16 = jax.ShapeDtypeStruct((T, H * HEAD_DIM), BF16)
    gate32 = jax.ShapeDtypeStruct((H // 2, 16, T), F32)
    return pl.pallas_call(
        body, name=name, grid=(H // 2, nq),
        in_specs=[pl.BlockSpec((2, tq, LANES), lambda p, i: (p, i, 0)),
                  pl.BlockSpec((2, T, LANES), lambda p, i: (p, 0, 0)),
                  pair, qblk, qblk,
                  pl.BlockSpec((None, tq, LANES), lambda p, i: (p, i, 0))],
        out_specs=[qblk, pair, pair, pl.BlockSpec((None, 16, tq), lambda p, i: (p, 0, i)),
                   pl.BlockSpec((None, 16, T), lambda p, i: (p, 0, 0))],
        out_shape=[tok16, tok16, tok16, gate32, gate32],
        scratch_shapes=[pltpu.VMEM((2, LANES, T), F32), pltpu.VMEM((LANES, T), F32),
                        pltpu.VMEM((2, tq, LANES), F32)],
        compiler_params=_cp(("parallel", "arbitrary")),
    )(qp, kp, v, o, do, lse)


def _mesh_pos():
    return lax.axis_index("x"), lax.axis_index("y"), lax.axis_index("c")


def _all_gather(arrs, name, groups=None):
    n = len(arrs)
    if groups is None:
        groups = [(a, 0) for a in range(n)]
    ng = 1 + max(g for g, _ in groups)
    per_group = [sum(1 for g, _ in groups if g == gi) for gi in range(ng)]
    first_of = [next(a for a in range(n) if groups[a][0] == gi) for gi in range(ng)]

    def body(*refs):
        ins, outs = refs[:n], refs[n:n + ng]
        send_sems, recv_sems, local_sems = refs[n + ng:]
        x, y, c = _mesh_pos()
        me, sib = (x, y, c), (x, y, 1 - c)
        chips = [(1 - x, y), (x, 1 - y), (1 - x, 1 - y)]

        def dst_of(a, px, py, pc):
            g, k = groups[a]
            return outs[g].at[N_DEV * k + 4 * px + 2 * py + pc]

        def copy(a, k, block, to, src=None):
            dst = dst_of(a, *block)
            return pltpu.make_async_remote_copy(
                src_ref=dst if src is None else src, dst_ref=dst,
                send_sem=send_sems.at[a, k], recv_sem=recv_sems.at[a, k], device_id=to, device_id_type=MESH)

        mine = [pltpu.make_async_copy(ins[a], dst_of(a, *me), local_sems.at[a]) for a in range(n)]
        for cp in mine:
            cp.start()
        first = []
        for a in range(n):
            first.append(copy(a, 0, me, sib, src=ins[a]))
            first += [copy(a, 1 + j, me, (*chip, c), src=ins[a]) for j, chip in enumerate(chips)]
        for cp in first:
            cp.start()
        passed = []
        for j, chip in enumerate(chips):
            for a in range(n):
                copy(a, 1 + j, (*chip, c), me).wait_recv()
                fwd = copy(a, 4 + j, (*chip, c), sib)
                fwd.start()
                passed.append(fwd)
        for a in range(n):
            copy(a, 0, sib, me).wait_recv()
            for j, chip in enumerate(chips):
                copy(a, 4 + j, (*chip, 1 - c), me).wait_recv()
        for cp in first + passed:
            cp.wait_send()
        for cp in mine:
            cp.wait()

    any_spec = pl.BlockSpec(memory_space=pl.ANY)
    return pl.pallas_call(
        body, name=name,
        in_specs=[any_spec] * n, out_specs=[any_spec] * ng,
        out_shape=[jax.ShapeDtypeStruct((N_DEV * per_group[gi],) + arrs[first_of[gi]].shape, arrs[first_of[gi]].dtype)
                   for gi in range(ng)],
        scratch_shapes=[pltpu.SemaphoreType.DMA((n, 7)), pltpu.SemaphoreType.DMA((n, 7)),
                        pltpu.SemaphoreType.DMA((n,))],
    )(*arrs)


def _pair_exchange(gs, name):
    n = len(gs)

    def body(*refs):
        g_refs, o_refs = refs[:n], refs[n:2 * n]
        send_sems, recv_sems = refs[2 * n:]
        x, y, c = _mesh_pos()
        sib = (x, y, 1 - c)
        copies = []
        for a in range(n):
            for j in range(4):
                copies.append(pltpu.make_async_remote_copy(
                    src_ref=g_refs[a].at[2 * j + (1 - c)], dst_ref=o_refs[a].at[j],
                    send_sem=send_sems.at[a, j], recv_sem=recv_sems.at[a, j], device_id=sib, device_id_type=MESH))
        for cp in copies:
            cp.start()
        for cp in copies:
            cp.wait_recv()
        for cp in copies:
            cp.wait_send()

    any_spec = pl.BlockSpec(memory_space=pl.ANY)
    return pl.pallas_call(
        body, name=name, in_specs=[any_spec] * n, out_specs=[any_spec] * n,
        out_shape=[jax.ShapeDtypeStruct((4,) + g.shape[1:], g.dtype) for g in gs],
        scratch_shapes=[pltpu.SemaphoreType.DMA((n, 4)), pltpu.SemaphoreType.DMA((n, 4))],
    )(*gs)


def _chip_exchange(parts, name):
    n = len(parts)

    def body(*refs):
        p_refs, o_refs = refs[:n], refs[n:2 * n]
        send_sems, recv_sems = refs[2 * n:]
        x, y, c = _mesh_pos()
        chips = [(1 - x, y), (x, 1 - y), (1 - x, 1 - y)]
        copies = []
        for a in range(n):
            for k, (px, py) in enumerate(chips):
                copies.append(pltpu.make_async_remote_copy(
                    src_ref=p_refs[a].at[2 * px + py], dst_ref=o_refs[a].at[k],
                    send_sem=send_sems.at[a, k], recv_sem=recv_sems.at[a, k], device_id=(px, py, c),
                    device_id_type=MESH))
        for cp in copies:
            cp.start()
        for cp in copies:
            cp.wait_recv()
        for cp in copies:
            cp.wait_send()

    any_spec = pl.BlockSpec(memory_space=pl.ANY)
    return pl.pallas_call(
        body, name=name, in_specs=[any_spec] * n, out_specs=[any_spec] * n,
        out_shape=[jax.ShapeDtypeStruct((3,) + p.shape[1:], p.dtype) for p in parts],
        scratch_shapes=[pltpu.SemaphoreType.DMA((n, 3)), pltpu.SemaphoreType.DMA((n, 3))],
    )(*parts)


HBM_SPEC = pl.BlockSpec(memory_space=pltpu.HBM)
SEM_SPEC = pl.BlockSpec(memory_space=pltpu.SEMAPHORE)
ANY_SPEC = pl.BlockSpec(memory_space=pl.ANY)
DATAFLOW_EFFECT = pltpu.SideEffectType.DATAFLOW_SIDE_EFFECTING


def _peers():
    x, y, c = _mesh_pos()
    flip = lambda v, b: 1 - v if b else v
    return [(flip(x, (k >> 2) & 1), flip(y, (k >> 1) & 1), flip(c, k & 1)) for k in range(1, N_DEV)]


def _slot(p):
    return 4 * p[0] + 2 * p[1] + p[2]


def _direct_copy(src_refs, land_refs, sems, a, k, p, land_of, dst_slot, src_slot):
    s = src_slot(a, p)
    return pltpu.make_async_remote_copy(
        src_ref=src_refs[a] if s is None else src_refs[a].at[s], dst_ref=land_refs[land_of[a]].at[dst_slot(a, k)],
        send_sem=sems[0].at[a * (N_DEV - 1) + k], recv_sem=sems[1].at[a * (N_DEV - 1) + k], device_id=p,
        device_id_type=MESH)


def _direct_start(srcs, lands, land_of, dst_slot, src_slot, after, name, collective_id):
    n, nl = len(srcs), len(lands)

    def body(*refs):
        src_refs, land_refs = refs[:n], refs[n:n + nl]
        sems = (refs[n + nl + 1], refs[n + nl + 2])
        token = refs[-1]
        peers = _peers()
        barrier = pltpu.get_barrier_semaphore()
        for p in peers:
            pl.semaphore_signal(barrier, inc=1, device_id=p, device_id_type=MESH)
        pl.semaphore_wait(barrier, N_DEV - 1)
        for a in range(n):
            for k, p in enumerate(peers):
                _direct_copy(src_refs, land_refs, sems, a, k, p, land_of, dst_slot, src_slot).start()
        token[...] = jnp.zeros_like(token)

    hbm = lambda t: pltpu.HBM(t.shape, t.dtype)
    sem_t = pltpu.SemaphoreType.DMA((n * (N_DEV - 1),))
    outs = pl.pallas_call(
        body, name=name,
        out_shape=(sem_t, sem_t, *[hbm(t) for t in srcs], *[hbm(t) for t in lands], jax.ShapeDtypeStruct((8, LANES), F32)),
        in_specs=[HBM_SPEC] * (n + nl) + [ANY_SPEC],
        out_specs=(SEM_SPEC, SEM_SPEC, *([HBM_SPEC] * (n + nl)), pl.BlockSpec(memory_space=pltpu.VMEM)),
        input_output_aliases={i: 2 + i for i in range(n + nl)},
        compiler_params=pltpu.CompilerParams(has_side_effects=DATAFLOW_EFFECT, collective_id=collective_id),
    )(*[pltpu.with_memory_space_constraint(t, pltpu.HBM) for t in srcs],
      *[pltpu.with_memory_space_constraint(t, pltpu.HBM) for t in lands], after)
    return outs[0], outs[1], list(outs[2:2 + n]), list(outs[2 + n:2 + n + nl]), outs[-1]


def _direct_wait(send_sems, recv_sems, srcs, lands, land_of, idxs, dst_slot, src_slot, after, name):
    land_ids = []
    for a in idxs:
        if land_of[a] not in land_ids:
            land_ids.append(land_of[a])
    m, ml = len(idxs), len(land_ids)
    sub_land_of = {j: land_ids.index(land_of[a]) for j, a in enumerate(idxs)}

    def body(*refs):
        src_refs, land_refs = refs[:m], refs[m:m + ml]
        ssem, rsem = refs[m + ml], refs[m + ml + 1]
        for j, a in enumerate(idxs):
            for k, p in enumerate(_peers()):
                s = src_slot(a, p)
                cp = pltpu.make_async_remote_copy(
                    src_ref=src_refs[j] if s is None else src_refs[j].at[s],
                    dst_ref=land_refs[sub_land_of[j]].at[dst_slot(a, k)],
                    send_sem=ssem.at[a * (N_DEV - 1) + k], recv_sem=rsem.at[a * (N_DEV - 1) + k], device_id=p,
                    device_id_type=MESH)
                cp.wait_send()
                cp.wait_recv()

    hbm = lambda t: pltpu.HBM(t.shape, t.dtype)
    sub_s, sub_l = [srcs[a] for a in idxs], [lands[g] for g in land_ids]
    outs = pl.pallas_call(
        body, name=name,
        out_shape=(*[hbm(t) for t in sub_s], *[hbm(t) for t in sub_l]),
        in_specs=[HBM_SPEC] * (m + ml) + [SEM_SPEC, SEM_SPEC, ANY_SPEC],
        out_specs=tuple([HBM_SPEC] * (m + ml)),
        input_output_aliases={i: i for i in range(m + ml)},
        compiler_params=pltpu.CompilerParams(has_side_effects=DATAFLOW_EFFECT),
    )(*sub_s, *sub_l, send_sems, recv_sems, after)
    return list(outs[:m]), list(outs[m:])


def _row_block(R, C):
    best = None
    for d in range(16, R + 1, 16):
        if R % d == 0 and d * C <= 256 * 1024:
            best = d
    return best if best is not None else R


def _pair_add(g, recv, cidx, name):
    _, R, C = g.shape
    tr = _row_block(R, C)

    def body(c_ref, g_ref, r_ref, o_ref):
        del c_ref
        o_ref[...] = (g_ref[...].astype(F32) + r_ref[...].astype(F32)).astype(BF16)

    grid_spec = pltpu.PrefetchScalarGridSpec(
        num_scalar_prefetch=1, grid=(4, R // tr),
        in_specs=[pl.BlockSpec((None, tr, C), lambda j, i, c: (2 * j + c[0], i, 0)),
                  pl.BlockSpec((None, tr, C), lambda j, i, c: (j, i, 0))],
        out_specs=pl.BlockSpec((None, tr, C), lambda j, i, c: (j, i, 0)))
    return pl.pallas_call(
        body, name=name, grid_spec=grid_spec,
        out_shape=jax.ShapeDtypeStruct((4, R, C), BF16),
        compiler_params=_cp(("parallel", "parallel")),
    )(cidx, g, recv)


def _adamw_math(w, g, m, v):
    m = ADAM_B1 * m + (1.0 - ADAM_B1) * g
    v = ADAM_B2 * v + (1.0 - ADAM_B2) * (g * g)
    m_hat = m / (1.0 - ADAM_B1 ** ADAM_STEP)
    v_hat = v / (1.0 - ADAM_B2 ** ADAM_STEP)
    delta = -ADAM_LR * (m_hat / (jnp.sqrt(v_hat) + ADAM_EPS) + ADAM_WD * w)
    return delta, m, v


def _sum_adamw(parts, w, m, v, name, sel=None):
    R, C = w.shape
    tr = _row_block(R, C)
    specs, args = [], []
    for arr, idxs in parts:
        for idx in idxs:
            if idx < 0:
                specs.append(pl.BlockSpec((None, tr, C), lambda i, s: (s[0], i, 0)))
            else:
                specs.append(pl.BlockSpec((None, tr, C), lambda i, s, idx=idx: (idx, i, 0)))
            args.append(arr)
    npart = len(args)
    blk = pl.BlockSpec((tr, C), lambda i, s: (i, 0))

    def body(s_ref, *refs):
        del s_ref
        g = refs[0][...].astype(F32)
        for r in refs[1:npart]:
            g = g + r[...].astype(F32)
        w_ref, m_ref, v_ref, g_out, d_out, m_out, v_out = refs[npart:]
        delta, mm, vv = _adamw_math(w_ref[...], g, m_ref[...], v_ref[...])
        g_out[...] = g
        d_out[...] = delta
        m_out[...] = mm
        v_out[...] = vv

    grid_spec = pltpu.PrefetchScalarGridSpec(
        num_scalar_prefetch=1, grid=(R // tr,),
        in_specs=specs + [blk, blk, blk], out_specs=[blk] * 4)
    if sel is None:
        sel = jnp.zeros((1,), jnp.int32)
    return pl.pallas_call(
        body, name=name, grid_spec=grid_spec,
        out_shape=[jax.ShapeDtypeStruct((R, C), F32)] * 4,
        compiler_params=_cp(("parallel",)),
    )(sel, *args, w, m, v)


def _rows(a, c):
    return a.reshape(-1, c)


def _pad_rows(a, r):
    return jnp.pad(a, ((0, r - a.shape[0]), (0, 0))) if a.shape[0] != r else a


def _gate_tables():
    hp = N_HEADS // 2
    sel_q = np.zeros((hp, 3 * LANES, 2 * LANES), np.float32)
    sel_k = np.zeros((hp, 3 * LANES, 2 * LANES), np.float32)
    const_q = np.zeros((hp, 1, 2 * LANES), np.float32)
    const_k = np.zeros((hp, 1, 2 * LANES), np.float32)
    for p in range(hp):
        for hh in range(2):
            h = 2 * p + hh
            base = hh * LANES + HEAD_DIM
            for piece in range(3):
                sel_q[p, piece * LANES + h, base + piece] = 1.0
                sel_k[p, piece * LANES + h, base + 3 + piece] = -1.0
            const_k[p, 0, base:base + 3] = 1.0
            const_q[p, 0, base + 3:base + 6] = 1.0
    as_bf = lambda t: jnp.asarray(t, BF16)
    return as_bf(sel_q), as_bf(sel_k), jnp.asarray(const_q), jnp.asarray(const_k)


def _pad_heads(w):
    d = w.shape[0]
    w3 = w.reshape(d, N_HEADS, HEAD_DIM)
    return jnp.pad(w3, ((0, 0), (0, 0), (0, LANES - HEAD_DIM))).reshape(d, N_HEADS * LANES)


def kernel(x, mix_norm_g, ffn_norm_g, gm_w_in, gm_ln_g, gm_ln_b, gm_w_s, gm_b_s, gm_w_out, fox_w_qkvf, fox_b_f, fox_w_o, ffn_w_gate, ffn_w_up, ffn_conv_w, ffn_conv_b, ffn_w_down, final_norm_g, loss_target, m_mix_norm_g, m_ffn_norm_g, m_gm_w_in, m_gm_ln_g, m_gm_ln_b, m_gm_w_s, m_gm_b_s, m_gm_w_out, m_fox_w_qkvf, m_fox_b_f, m_fox_w_o, m_ffn_w_gate, m_ffn_w_up, m_ffn_conv_w, m_ffn_conv_b, m_ffn_w_down, m_final_norm_g, v_mix_norm_g, v_ffn_norm_g, v_gm_w_in, v_gm_ln_g, v_gm_ln_b, v_gm_w_s, v_gm_b_s, v_gm_w_out, v_fox_w_qkvf, v_fox_b_f, v_fox_w_o, v_ffn_w_gate, v_ffn_w_up, v_ffn_conv_w, v_ffn_conv_b, v_ffn_w_down, v_final_norm_g):
    T, D = x.shape[1], x.shape[2]
    E = gm_ln_g.shape[1]
    FF = ffn_conv_b.shape[1]
    NQKVF = 3 * D + N_HEADS
    xi, yi, ci = _mesh_pos()
    me = 4 * xi + 2 * yi + ci
    h0 = x.reshape(T, D)
    tgt = loss_target.reshape(T, D)

    nl = ffn_w_gate.shape[0]
    to16 = lambda a: a.astype(BF16)
    n_cw_rows = ffn_conv_w.size // LANES
    cw_rows = _pad_rows(_rows(ffn_conv_w.astype(F32), LANES), 16)
    w_in_g, w_out_g8, cwg = _all_gather([to16(gm_w_in[0]), to16(gm_w_out[0]), cw_rows], "ag_weights")
    w_out_g = w_out_g8.reshape(E, D)
    later, land_of, land_off, lands = [], [], [], []
    for l in range(nl):
        later += [to16(ffn_w_gate[l]), to16(ffn_w_up[l]), to16(ffn_w_down[l])]
        land_of += [2 * l, 2 * l, 2 * l + 1]
        land_off += [0, N_DEV, 0]
        lands += [lax.empty((2 * N_DEV, D, FF // N_DEV), BF16), lax.empty((N_DEV, FF // N_DEV, D), BF16)]
    later += [to16(fox_w_qkvf[0]), to16(fox_w_o[0])]
    land_of += [2 * nl, 2 * nl + 1]
    land_off += [0, 0]
    lands += [lax.empty((N_DEV, D, NQKVF // N_DEV), BF16), lax.empty((N_DEV, D // N_DEV, D), BF16)]
    ag_dst = lambda a, k: land_off[a] + _slot(_mesh_pos())
    ag_src = lambda a, p: None
    ag_send, ag_recv, later, lands, ag_token = _direct_start(later, lands, land_of, ag_dst, ag_src, w_in_g,
                                                             "ag_later_start", collective_id=1)

    def own_blocks(landed, shards, offs):
        for s, o in zip(shards, offs):
            landed = lax.dynamic_update_index_in_dim(landed, s, o + me, 0)
        return landed

    def gather_wait(idxs, after, name):
        return _direct_wait(ag_send, ag_recv, later, lands, land_of, idxs, ag_dst, ag_src, after, name)

    conv_w_full = jnp.transpose(cwg[:, :n_cw_rows].reshape(N_DEV, nl, 3, FF // N_DEV), (1, 2, 0, 3)).reshape(nl, 3, FF)

    ffn_w = {}

    def ffn_weights(l):
        return ffn_w[l]

    def land_ffn(l, shards, gu_land, dn_land):
        ffn_w[l] = (own_blocks(gu_land, shards[:2], [0, N_DEV]), own_blocks(dn_land, shards[2:3], [0]).reshape(FF, D))

    saved = {}

    def ffn_fwd(l, h_in, hn, next_g):
        wgul, wdl = ffn_weights(l)
        au = _mm_nn(hn, wgul, f"ffn{l}_gate_up")
        act = _ffn_mid_fwd(au, conv_w_full[l], ffn_conv_b[l:l + 1], f"ffn{l}_mid")
        saved[f"ffn{l}"] = (h_in, hn, au, act)
        if next_g is None:
            return _mm_nn(act, wdl, f"ffn{l}_down", res=h_in), None
        return _mm_nn(act, wdl, f"ffn{l}_down", res=h_in, norm_g=next_g)

    bs_col = gm_b_s[0].reshape(GM_GROUPS, CHUNK, 1)
    hn0 = _rms_fwd(h0, mix_norm_g[0:1], "mix0_norm", after=ag_token)
    z = _mm_nn(hn0, w_in_g, "gm_in")
    gu = _sgu_fwd(z, gm_ln_g, gm_ln_b, gm_w_s[0], bs_col, "gm_sgu")
    h1, hn_f0 = _mm_nn(gu, w_out_g, "gm_out", res=h0, norm_g=ffn_norm_g[0:1])
    mine0, land0 = gather_wait([0, 1, 2], h1, "ag_ffn0_wait")
    land_ffn(0, mine0, *land0)
    h2, hn2 = ffn_fwd(0, h1, hn_f0, mix_norm_g[1:2])

    mine1, rest = gather_wait(list(range(3, 3 * nl + 2)), h2, "ag_layer1_wait")
    for l in range(1, nl):
        land_ffn(l, mine1[3 * (l - 1):3 * l], rest[2 * (l - 1)], rest[2 * (l - 1) + 1])
    w_qkvf = jnp.transpose(own_blocks(rest[-2], mine1[-2:-1], [0]), (1, 0, 2)).reshape(D, NQKVF)
    w_o_g = own_blocks(rest[-1], mine1[-1:], [0]).reshape(D, D)
    w_q, w_k, w_v = w_qkvf[:, :D], w_qkvf[:, D:2 * D], w_qkvf[:, 2 * D:3 * D]
    w_f = jnp.pad(w_qkvf[:, 3 * D:], ((0, 0), (0, LANES - N_HEADS)))
    bf_row = jnp.pad(fox_b_f, ((0, 0), (0, LANES - N_HEADS)))
    sel_q, sel_k, const_q, const_k = _gate_tables()
    scale = HEAD_DIM ** -0.5
    f_logit = _mm_nn(hn2, w_f, "fox_f")
    cp, sneg = _gate_scan(f_logit, bf_row, "fox_scan")
    qp, kp = _qk_proj(hn2, (_pad_heads(w_q), _pad_heads(w_k)), cp, (sel_q, sel_k), (const_q, const_k),
                      (scale * LOG2E, 1.0), "fox_qk")
    vv = _mm_nn(hn2, w_v, "fox_v", out_dtype=BF16)
    o, o32, lse = _attn_fwd_t(qp, kp, jnp.transpose(vv), "fox_attn")
    h3, hn_f1 = _mm_nn(o, w_o_g, "fox_o", res=h2, norm_g=ffn_norm_g[1:2])
    h4, _ = ffn_fwd(1, h3, hn_f1, None)

    dh, dh16, d_final, loss_row = _loss_head(h4, tgt, final_norm_g.reshape(1, D), "loss_head")
    loss = lax.psum(loss_row[0, 0], ("x", "y", "c"))

    rs_dst = lambda a, k: k
    rs_src = lambda a, p: _slot(p)
    me_idx = me.astype(jnp.int32).reshape(1)

    def rs_start(grads, name, cid):
        lands = [lax.empty((N_DEV - 1,) + g.shape[1:], BF16) for g in grads]
        return _direct_start(grads, lands, list(range(len(grads))), rs_dst, rs_src, loss_row, name, collective_id=cid)

    def rs_wait(st, after, name):
        n = len(st[2])
        return _direct_wait(st[0], st[1], st[2], st[3], list(range(n)), list(range(n)), rs_dst, rs_src, after, name)

    def ffn_bwd(l, dh, dh16, after=None):
        wgul, wdl = ffn_weights(l)
        h_in, hn, au, act = saved[f"ffn{l}"]
        dact = _mm_nt([dh16], wdl, f"ffn{l}_dact", out_dtype=BF16, after=after)
        d_wd = _mm_tn(act, dh16, f"ffn{l}_dwd", out_dtype=BF16)
        da, dup, d_cw, d_cb = _ffn_mid_bwd(au, dact, conv_w_full[l], ffn_conv_b[l:l + 1], f"ffn{l}_dmid")
        dh_in, dh_in16, d_norm = _mm_nt([da, dup], wgul, f"ffn{l}_dhn", norm_bwd=(h_in, ffn_norm_g[l:l + 1], dh))
        d_wg = _mm_tn(hn, da, f"ffn{l}_dwg", blocked_w=FF // N_DEV, out_dtype=BF16)
        d_wu = _mm_tn(hn, dup, f"ffn{l}_dwu", blocked_w=FF // N_DEV, out_dtype=BF16)
        big_g = [d_wg, d_wu, d_wd.reshape(N_DEV, FF // N_DEV, D)]
        return dh_in, dh_in16, big_g, dict(cw=d_cw, cb=d_cb, norm=d_norm)

    dh, dh16, big_ffn1, g_ffn1 = ffn_bwd(1, dh, dh16)

    do = _mm_nt([dh16], w_o_g, "fox_do", out_dtype=BF16)
    d_wo = _mm_tn(o, dh16, "fox_dwo", out_dtype=BF16)
    dq, dk, dv, dqe, dke = _attn_bwd(qp, kp, vv, o32, do, lse, scale, "fox_dattn")
    gate_lane = lambda e, r: jnp.pad(jnp.transpose(e[:, r::8, :].reshape(N_HEADS, T)), ((0, 0), (0, LANES - N_HEADS)))
    df, d_bf = _gate_scan_bwd(gate_lane(dqe, 0), gate_lane(dke, 3), sneg, "fox_dscan")
    dhn = _mm_nt([df], w_f, "fox_dhn_f")
    dh_mix1 = _mm_nt([dq, dk, dv], w_qkvf[:, :3 * D], "fox_dhn_qkv", add=dhn, norm_bwd=(h2, mix_norm_g[1:2], dh))
    d_wq = _mm_tn(hn2, dq, "fox_dwq", out_dtype=BF16)
    d_wk = _mm_tn(hn2, dk, "fox_dwk", out_dtype=BF16)
    d_wv = _mm_tn(hn2, dv, "fox_dwv", out_dtype=BF16)
    d_wf = _mm_tn(hn2, df, "fox_dwf", out_dtype=BF16)
    d_wqkvf = jnp.concatenate([d_wq, d_wk, d_wv, d_wf[:, :N_HEADS]], axis=1)
    dh, dh16, d_mix1 = dh_mix1
    st1 = rs_start([jnp.transpose(d_wqkvf.reshape(D, N_DEV, NQKVF // N_DEV), (1, 0, 2)),
                    d_wo.reshape(N_DEV, D // N_DEV, D)] + big_ffn1, "rs1_start", 2)

    dh, dh16, big_ffn0, g_ffn0 = ffn_bwd(0, dh, dh16, after=st1[4])
    st2 = rs_start(big_ffn0, "rs2_start", 3)

    dgu = _mm_nt([dh16], w_out_g, "gm_dgu", out_dtype=BF16, after=st2[4])
    d_wout = _mm_tn(gu, dh16, "gm_dwout", out_dtype=BF16)
    dz, d_lng, d_lnb, d_ws, d_bs = _sgu_bwd(z, dgu, gm_ln_g, gm_ln_b, gm_w_s[0], bs_col, "gm_dsgu")
    d_win = _mm_tn(hn0, dz, "gm_dwin", blocked_w=2 * E // N_DEV, out_dtype=BF16)
    st3 = rs_start([d_win, d_wout.reshape(N_DEV, E // N_DEV, D)], "rs3_start", 4)
    dx, _, d_mix0 = _mm_nt([dz], w_in_g, "gm_dhn", after=st3[4], norm_bwd=(h0, mix_norm_g[0:1], dh))

    own1, land1 = rs_wait(st1, dx, "rs1_wait")
    own2, land2 = rs_wait(st2, land1[0], "rs2_wait")
    cat1 = lambda a, b: jnp.concatenate([a, b], axis=1)
    big_out = {}

    def big_adamw(name, w, m, v, own, landed):
        shard2d = lambda a, c=own.shape[2]: a.reshape(-1, c)
        res = _sum_adamw([(own, [-1]), (landed, list(range(N_DEV - 1)))], shard2d(w), shard2d(m), shard2d(v),
                         f"adamw_{name}", sel=me_idx)
        big_out[name] = [t.reshape(w.shape) for t in res]

    big_adamw("fox_w_qkvf", fox_w_qkvf, m_fox_w_qkvf, v_fox_w_qkvf, own1[0], land1[0])
    big_adamw("fox_w_o", fox_w_o, m_fox_w_o, v_fox_w_o, own1[1], land1[1])
    big_adamw("ffn_w_gate", ffn_w_gate, m_ffn_w_gate, v_ffn_w_gate, cat1(own2[0], own1[2]), cat1(land2[0], land1[2]))
    big_adamw("ffn_w_up", ffn_w_up, m_ffn_w_up, v_ffn_w_up, cat1(own2[1], own1[3]), cat1(land2[1], land1[3]))
    big_adamw("ffn_w_down", ffn_w_down, m_ffn_w_down, v_ffn_w_down, cat1(own2[2], own1[4]), cat1(land2[2], land1[4]))

    small = [("mix_norm_g", mix_norm_g, m_mix_norm_g, v_mix_norm_g, jnp.concatenate([d_mix0, d_mix1], axis=0)),
             ("ffn_norm_g", ffn_norm_g, m_ffn_norm_g, v_ffn_norm_g, jnp.concatenate([g_ffn0["norm"], g_ffn1["norm"]], axis=0)),
             ("gm_ln_g", gm_ln_g, m_gm_ln_g, v_gm_ln_g, d_lng),
             ("gm_ln_b", gm_ln_b, m_gm_ln_b, v_gm_ln_b, d_lnb),
             ("gm_w_s", gm_w_s, m_gm_w_s, v_gm_w_s, d_ws),
             ("gm_b_s", gm_b_s, m_gm_b_s, v_gm_b_s, d_bs),
             ("fox_b_f", fox_b_f, m_fox_b_f, v_fox_b_f, d_bf[:, :N_HEADS]),
             ("ffn_conv_b", ffn_conv_b, m_ffn_conv_b, v_ffn_conv_b, jnp.concatenate([g_ffn0["cb"], g_ffn1["cb"]], axis=0)),
             ("final_norm_g", final_norm_g, m_final_norm_g, v_final_norm_g, d_final)]
    d_cw_full = jnp.stack([g_ffn0["cw"], g_ffn1["cw"]], axis=0)

    def small_rows(a):
        flat = a.astype(F32).reshape(-1)
        n = -(-flat.size // (8 * LANES)) * (8 * LANES)
        return jnp.pad(flat, (0, n - flat.size)).reshape(-1, LANES)

    s_rows = [small_rows(p[1]).shape[0] for p in small]
    s_off = np.concatenate([[0], np.cumsum(s_rows)]).tolist()
    cw_g_rows = small_rows(d_cw_full)
    zeros_cw = jnp.zeros_like(cw_g_rows)
    cat = lambda k: jnp.concatenate([small_rows(p[k]) for p in small] + [zeros_cw], axis=0)
    g_small = jnp.concatenate([small_rows(p[4]) for p in small] + [cw_g_rows], axis=0)
    (gs_all,) = _all_gather([g_small], "ag_small_grads")
    small_out = _sum_adamw([(gs_all, list(range(N_DEV)))], cat(1), cat(2), cat(3), "adamw_small")
    gs = small_out[0]

    g_cw_full = gs[s_off[-1]:].reshape(-1)[:d_cw_full.size].reshape(d_cw_full.shape)
    g_cw = lax.dynamic_slice_in_dim(g_cw_full, me * (FF // N_DEV), FF // N_DEV, axis=2)
    cw2 = lambda a: _pad_rows(_rows(a.astype(F32), LANES), 16)
    cw_out = _sum_adamw([(cw2(g_cw)[None], [0])], cw2(ffn_conv_w), cw2(m_ffn_conv_w), cw2(v_ffn_conv_w), "adamw_conv_w")

    own3, land3 = rs_wait(st3, cw_out[0], "rs3_wait")
    big_adamw("gm_w_in", gm_w_in, m_gm_w_in, v_gm_w_in, own3[0], land3[0])
    big_adamw("gm_w_out", gm_w_out, m_gm_w_out, v_gm_w_out, own3[1], land3[1])

    names = ["mix_norm_g", "ffn_norm_g", "gm_w_in", "gm_ln_g", "gm_ln_b", "gm_w_s", "gm_b_s", "gm_w_out", "fox_w_qkvf",
             "fox_b_f", "fox_w_o", "ffn_w_gate", "ffn_w_up", "ffn_conv_w", "ffn_conv_b", "ffn_w_down", "final_norm_g"]
    small_idx = {p[0]: k for k, p in enumerate(small)}

    def pick(kind, name):
        if name in big_out:
            return big_out[name][kind]
        if name == "ffn_conv_w":
            return cw_out[kind][:n_cw_rows].reshape(ffn_conv_w.shape)
        k = small_idx[name]
        shp = small[k][1].shape
        return small_out[kind][s_off[k]:s_off[k + 1]].reshape(-1)[:int(np.prod(shp))].reshape(shp)

    outs = [loss, dx.reshape(x.shape)]
    for kind in range(4):
        outs += [pick(kind, n) for n in names]
    return tuple(outs)
```

```python
import functools
import math

import numpy as np
import jax
import jax.numpy as jnp
from jax import lax
from jax.experimental import pallas as pl
from jax.experimental.pallas import tpu as pltpu

F32 = jnp.float32
BF16 = jnp.bfloat16
MESH = pl.DeviceIdType.MESH

N_HEADS = 16
HEAD_DIM = 64
CHUNK = 128
GM_GROUPS = 8
RMS_EPS = 1e-6
LN_EPS = 1e-5
ADAM_LR = 0.001
ADAM_B1 = 0.9
ADAM_B2 = 0.999
ADAM_EPS = 1e-08
ADAM_WD = 0.01
ADAM_STEP = 10
N_DEV = 8

LANES = 128
VMEM_BYTES_V7X = 64 * 1024 * 1024
VMEM_LIMIT = 56 * 1024 * 1024

TM = 512
TM_MM = 1024
TT = 1024
TQ = 512
TF = 512
KV_UNROLL_BWD = 2
KV_UNROLL = 2
MM_BLOCK_BYTES = 8 * 1024 * 1024
NEG = -1e30
LOG2E = math.log2(math.e)
LN2 = math.log(2.0)


def _cp(sem=None, vmem=VMEM_LIMIT):
    return pltpu.CompilerParams(dimension_semantics=sem, vmem_limit_bytes=vmem)


def _gelu(x):
    c = math.sqrt(2.0 / math.pi)
    return x * (0.5 * (1.0 + jnp.tanh(c * (x + 0.044715 * (x * x * x)))))


def _gelu_grad(x):
    c = math.sqrt(2.0 / math.pi)
    t = jnp.tanh(c * (x + 0.044715 * (x * x * x)))
    return 0.5 * (1.0 + t) + x * (0.5 * (1.0 - t * t)) * (c * (1.0 + 3.0 * 0.044715 * (x * x)))


def _sigmoid(x):
    return 1.0 / (1.0 + jnp.exp(-x))


def _dot_nt(a, b):
    return lax.dot_general(a, b, (((1,), (1,)), ((), ())), preferred_element_type=F32)


def _dot_tn(a, b):
    return lax.dot_general(a, b, (((0,), (0,)), ((), ())), preferred_element_type=F32)


def _rms_fwd(h, g, name, after=None):
    T, D = h.shape
    tm = min(TM, T)

    def body(h_ref, g_ref, *rest):
        o_ref = rest[-1]
        x = h_ref[...]
        r = lax.rsqrt(jnp.mean(x * x, axis=-1, keepdims=True) + RMS_EPS)
        o_ref[...] = ((x * r) * g_ref[...]).astype(BF16)

    in_specs = [pl.BlockSpec((tm, D), lambda i: (i, 0)), pl.BlockSpec((1, D), lambda i: (0, 0))]
    args = [h, g]
    if after is not None:
        in_specs.append(pl.BlockSpec(memory_space=pl.ANY))
        args.append(after)
    return pl.pallas_call(
        body, name=name, grid=(T // tm,),
        in_specs=in_specs,
        out_specs=pl.BlockSpec((tm, D), lambda i: (i, 0)),
        out_shape=jax.ShapeDtypeStruct((T, D), BF16),
        compiler_params=_cp(("parallel",)),
    )(*args)


def _rms_bwd(dhn, h, g, dres, name):
    T, D = h.shape
    tm = min(TM, T)

    def body(d_ref, h_ref, g_ref, r_ref, o_ref, ob_ref, dg_ref):
        x = h_ref[...]
        d = d_ref[...]
        r = lax.rsqrt(jnp.mean(x * x, axis=-1, keepdims=True) + RMS_EPS)
        dyg = d * g_ref[...]
        dot = jnp.mean(dyg * x, axis=-1, keepdims=True)
        dh = r_ref[...] + (r * dyg - x * ((r * r * r) * dot))
        o_ref[...] = dh
        ob_ref[...] = dh.astype(BF16)
        part = jnp.sum(d * (x * r), axis=0, keepdims=True)

        @pl.when(pl.program_id(0) == 0)
        def _():
            dg_ref[...] = part

        @pl.when(pl.program_id(0) != 0)
        def _():
            dg_ref[...] += part

    blk = pl.BlockSpec((tm, D), lambda i: (i, 0))
    row = pl.BlockSpec((1, D), lambda i: (0, 0))
    return pl.pallas_call(
        body, name=name, grid=(T // tm,),
        in_specs=[blk, blk, row, blk],
        out_specs=[blk, blk, row],
        out_shape=[jax.ShapeDtypeStruct((T, D), F32), jax.ShapeDtypeStruct((T, D), BF16),
                   jax.ShapeDtypeStruct((1, D), F32)],
        compiler_params=_cp(("arbitrary",)),
    )(dhn, h, g, dres)


def _loss_head(h, tgt, g, name):
    T, D = h.shape
    tm = min(TM, T)

    def body(h_ref, t_ref, g_ref, o_ref, ob_ref, dg_ref, l_ref):
        x = h_ref[...]
        gg = g_ref[...]
        r = lax.rsqrt(jnp.mean(x * x, axis=-1, keepdims=True) + RMS_EPS)
        xr = x * r
        e = xr * gg - t_ref[...]
        lpart = 0.5 * jnp.sum(jnp.mean(e * e, axis=-1, keepdims=True), axis=0, keepdims=True)
        dy = e * (1.0 / D)
        dyg = dy * gg
        dot = jnp.mean(dyg * x, axis=-1, keepdims=True)
        dh = r * dyg - x * ((r * r * r) * dot)
        o_ref[...] = dh
        ob_ref[...] = dh.astype(BF16)
        part = jnp.sum(dy * xr, axis=0, keepdims=True)
        lrow = jnp.broadcast_to(lpart, (1, LANES))

        @pl.when(pl.program_id(0) == 0)
        def _():
            dg_ref[...] = part
            l_ref[...] = lrow

        @pl.when(pl.program_id(0) != 0)
        def _():
            dg_ref[...] += part
            l_ref[...] += lrow

    blk = pl.BlockSpec((tm, D), lambda i: (i, 0))
    row = pl.BlockSpec((1, D), lambda i: (0, 0))
    return pl.pallas_call(
        body, name=name, grid=(T // tm,),
        in_specs=[blk, blk, row],
        out_specs=[blk, blk, row, pl.BlockSpec((1, LANES), lambda i: (0, 0))],
        out_shape=[jax.ShapeDtypeStruct((T, D), F32), jax.ShapeDtypeStruct((T, D), BF16),
                   jax.ShapeDtypeStruct((1, D), F32), jax.ShapeDtypeStruct((1, LANES), F32)],
        compiler_params=_cp(("arbitrary",)),
    )(h, tgt, g)


def _mm_nn(a, b, name, out_dtype=F32, res=None, norm_g=None):
    M, K = a.shape
    b3 = b if b.ndim == 3 else b[None]
    nb, _, w = b3.shape
    N = nb * w
    tm = min(TM_MM, M, max(256, MM_BLOCK_BYTES // (4 * N)))
    o_spec = pl.BlockSpec((tm, N), lambda i: (i, 0))
    in_specs = [pl.BlockSpec((tm, K), lambda i: (i, 0)), pl.BlockSpec((nb, K, w), lambda i: (0, 0, 0))]
    args = [a, b3]
    if res is not None:
        in_specs.append(o_spec)
        args.append(res)
    if norm_g is not None:
        in_specs.append(pl.BlockSpec((1, N), lambda i: (0, 0)))
        args.append(norm_g)
    n_out = 2 if norm_g is not None else 1

    def body(*refs):
        a_ref, b_ref = refs[0], refs[1]
        o_ref = refs[-n_out]
        av = a_ref[...]
        for j in range(nb):
            cols = slice(j * w, (j + 1) * w)
            acc = jnp.dot(av, b_ref[j], preferred_element_type=F32)
            if res is not None:
                acc = refs[2][:, cols] + acc
            o_ref[:, cols] = acc.astype(out_dtype)
        if norm_g is not None:
            x = o_ref[...]
            r = lax.rsqrt(jnp.mean(x * x, axis=-1, keepdims=True) + RMS_EPS)
            refs[-1][...] = ((x * r) * refs[3][...]).astype(BF16)

    out_shape = jax.ShapeDtypeStruct((M, N), out_dtype)
    if norm_g is None:
        out_specs, out_shapes = o_spec, out_shape
    else:
        out_specs, out_shapes = [o_spec, o_spec], [out_shape, jax.ShapeDtypeStruct((M, N), BF16)]
    return pl.pallas_call(
        body, name=name, grid=(M // tm,),
        in_specs=in_specs, out_specs=out_specs, out_shape=out_shapes,
        compiler_params=_cp(("parallel",)),
    )(*args)


def _mm_nt(a_list, b, name, out_dtype=F32, add=None, after=None, norm_bwd=None):
    M, kw = a_list[0].shape
    tm = min(TM, M)
    na = len(a_list)
    blocked = b.ndim == 3
    N = b.shape[1] if blocked else b.shape[0]
    b_spec = pl.BlockSpec(b.shape, lambda i: (0,) * b.ndim)
    o_spec = pl.BlockSpec((tm, N), lambda i: (i, 0))
    row_spec = pl.BlockSpec((1, N), lambda i: (0, 0))
    in_specs = [pl.BlockSpec((tm, kw), lambda i: (i, 0)) for _ in a_list] + [b_spec]
    args = list(a_list) + [b]
    if add is not None:
        in_specs.append(o_spec)
        args.append(add)
    n_in = len(args)
    if norm_bwd is not None:
        in_specs += [o_spec, row_spec, o_spec]
        args += list(norm_bwd)
    if after is not None:
        in_specs.append(pl.BlockSpec(memory_space=pl.ANY))
        args.append(after)
    n_args = len(args)

    def body(*refs):
        a_refs = refs[:na]
        b_ref = refs[na]
        acc = refs[na + 1][...] if add is not None else None
        for s, a_ref in enumerate(a_refs):
            if blocked:
                w = b_ref.shape[2]
                per = kw // w
                parts = [_dot_nt(a_ref[:, jj * w:(jj + 1) * w], b_ref[s * per + jj]) for jj in range(per)]
            else:
                parts = [_dot_nt(a_ref[...], b_ref[:, s * kw:(s + 1) * kw])]
            for part in parts:
                acc = part if acc is None else acc + part
        if norm_bwd is None:
            refs[n_args][...] = acc.astype(out_dtype)
            return
        h_ref, g_ref, r_ref = refs[n_in:n_in + 3]
        o_ref, ob_ref, dg_ref = refs[n_args:n_args + 3]
        x = h_ref[...]
        r = lax.rsqrt(jnp.mean(x * x, axis=-1, keepdims=True) + RMS_EPS)
        dyg = acc * g_ref[...]
        dot = jnp.mean(dyg * x, axis=-1, keepdims=True)
        dh = r_ref[...] + (r * dyg - x * ((r * r * r) * dot))
        o_ref[...] = dh
        ob_ref[...] = dh.astype(BF16)
        part_g = jnp.sum(acc * (x * r), axis=0, keepdims=True)

        @pl.when(pl.program_id(0) == 0)
        def _():
            dg_ref[...] = part_g

        @pl.when(pl.program_id(0) != 0)
        def _():
            dg_ref[...] += part_g

    if norm_bwd is None:
        out_specs, out_shapes, sem = o_spec, jax.ShapeDtypeStruct((M, N), out_dtype), ("parallel",)
    else:
        out_specs = [o_spec, o_spec, row_spec]
        out_shapes = [jax.ShapeDtypeStruct((M, N), F32), jax.ShapeDtypeStruct((M, N), BF16),
                      jax.ShapeDtypeStruct((1, N), F32)]
        sem = ("arbitrary",)
    return pl.pallas_call(
        body, name=name, grid=(M // tm,),
        in_specs=in_specs, out_specs=out_specs, out_shape=out_shapes,
        compiler_params=_cp(sem),
    )(*args)


def _mm_tn(x, y, name, blocked_w=None, out_dtype=F32):
    T, Kx = x.shape
    N = y.shape[1]
    tt = min(TT, T)
    nt = T // tt
    tkx = min(Kx, max(LANES, MM_BLOCK_BYTES // (4 * N)))
    if blocked_w is not None:
        blk_shape, full_shape = (N // blocked_w, tkx, blocked_w), (N // blocked_w, Kx, blocked_w)
        o_spec = pl.BlockSpec(blk_shape, lambda i, t: (0, i, 0))
    else:
        blk_shape, full_shape = (tkx, N), (Kx, N)
        o_spec = pl.BlockSpec(blk_shape, lambda i, t: (i, 0))

    def body(x_ref, y_ref, o_ref, acc_ref):
        part = _dot_tn(x_ref[...], y_ref[...])
        t = pl.program_id(1)
        if blocked_w is None:
            pieces = [(slice(None), part)]
        else:
            pieces = [(j, part[:, j * blocked_w:(j + 1) * blocked_w]) for j in range(N // blocked_w)]

        @pl.when(t == 0)
        def _():
            for idx, pj in pieces:
                acc_ref[idx] = pj

        @pl.when(t != 0)
        def _():
            for idx, pj in pieces:
                acc_ref[idx] += pj

        @pl.when(t == nt - 1)
        def _():
            o_ref[...] = acc_ref[...].astype(out_dtype)

    return pl.pallas_call(
        body, name=name, grid=(Kx // tkx, nt),
        in_specs=[pl.BlockSpec((tt, tkx), lambda i, t: (t, i)),
                  pl.BlockSpec((tt, N), lambda i, t: (t, 0))],
        out_specs=o_spec, out_shape=jax.ShapeDtypeStruct(full_shape, out_dtype),
        scratch_shapes=[pltpu.VMEM(blk_shape, F32)],
        compiler_params=_cp(("parallel", "arbitrary")),
    )(x, y)


def _sgu_pieces(z, lng, lnb, wc, bs_ref):
    E = z.shape[1] // 2
    gd = E // GM_GROUPS
    zu, zv = z[:, :E], z[:, E:]
    u = _gelu(zu)
    v = _gelu(zv)
    mu = jnp.mean(v, axis=-1, keepdims=True)
    xc = v - mu
    rs = lax.rsqrt(jnp.mean(xc * xc, axis=-1, keepdims=True) + LN_EPS)
    xhat = xc * rs
    vln = xhat * lng + lnb
    s = []
    for g in range(GM_GROUPS):
        vg = vln[:, g * gd:(g + 1) * gd].astype(BF16)
        s.append(jnp.dot(wc[g], vg, preferred_element_type=F32) + bs_ref[g])
    return zu, zv, u, xhat, rs, vln, s


def _causal_ws(ws_ref):
    t = lax.broadcasted_iota(jnp.int32, (CHUNK, CHUNK), 0)
    s = lax.broadcasted_iota(jnp.int32, (CHUNK, CHUNK), 1)
    tri = t >= s
    return [jnp.where(tri, ws_ref[g], 0.0).astype(BF16) for g in range(GM_GROUPS)], tri


def _sgu_fwd(z, lng, lnb, ws, bs, name):
    T, E2 = z.shape
    E = E2 // 2
    gd = E // GM_GROUPS
    tm = min(2 * CHUNK, T)

    def body(z_ref, lng_ref, lnb_ref, ws_ref, bs_ref, o_ref):
        wc, _ = _causal_ws(ws_ref)
        for c in range(tm // CHUNK):
            rows = slice(c * CHUNK, (c + 1) * CHUNK)
            _, _, u, _, _, _, s = _sgu_pieces(z_ref[rows, :], lng_ref[...], lnb_ref[...], wc, bs_ref)
            for g in range(GM_GROUPS):
                cols = slice(g * gd, (g + 1) * gd)
                o_ref[rows, cols] = (u[:, cols] * s[g]).astype(BF16)

    full = lambda shape: pl.BlockSpec(shape, lambda i: (0,) * len(shape))
    return pl.pallas_call(
        body, name=name, grid=(T // tm,),
        in_specs=[pl.BlockSpec((tm, E2), lambda i: (i, 0)), full((1, E)), full((1, E)),
                  full((GM_GROUPS, CHUNK, CHUNK)), full((GM_GROUPS, CHUNK, 1))],
        out_specs=pl.BlockSpec((tm, E), lambda i: (i, 0)),
        out_shape=jax.ShapeDtypeStruct((T, E), BF16),
        compiler_params=_cp(("parallel",)),
    )(z, lng, lnb, ws, bs)


def _gm_in_fused(hn, w_in, lng, lnb, ws, bs, name):
    T, D = hn.shape
    nb, _, w = w_in.shape
    E2 = nb * w
    E = E2 // 2
    gd = E // GM_GROUPS
    tm = min(TM, T)

    def body(a_ref, w_ref, lng_ref, lnb_ref, ws_ref, bs_ref, z_ref, o_ref):
        av = a_ref[...]
        for j in range(nb):
            z_ref[:, j * w:(j + 1) * w] = jnp.dot(av, w_ref[j], preferred_element_type=F32)
        wc, _ = _causal_ws(ws_ref)
        for c in range(tm // CHUNK):
            rows = slice(c * CHUNK, (c + 1) * CHUNK)
            _, _, u, _, _, _, s = _sgu_pieces(z_ref[rows, :], lng_ref[...], lnb_ref[...], wc, bs_ref)
            for g in range(GM_GROUPS):
                cols = slice(g * gd, (g + 1) * gd)
                o_ref[rows, cols] = (u[:, cols] * s[g]).astype(BF16)

    full = lambda shape: pl.BlockSpec(shape, lambda i: (0,) * len(shape))
    return pl.pallas_call(
        body, name=name, grid=(T // tm,),
        in_specs=[pl.BlockSpec((tm, D), lambda i: (i, 0)), full((nb, D, w)), full((1, E)), full((1, E)),
                  full((GM_GROUPS, CHUNK, CHUNK)), full((GM_GROUPS, CHUNK, 1))],
        out_specs=[pl.BlockSpec((tm, E2), lambda i: (i, 0)), pl.BlockSpec((tm, E), lambda i: (i, 0))],
        out_shape=[jax.ShapeDtypeStruct((T, E2), F32), jax.ShapeDtypeStruct((T, E), BF16)],
        compiler_params=_cp(("parallel",)),
    )(hn, w_in, lng, lnb, ws, bs)


def _sgu_bwd(z, dg, lng, lnb, ws, bs, name):
    T, E2 = z.shape
    E = E2 // 2
    gd = E // GM_GROUPS
    tm = min(2 * CHUNK, T)
    nsteps = T // tm

    def body(z_ref, dg_ref, lng_ref, lnb_ref, ws_ref, bs_ref, dz_ref, dlng_ref, dlnb_ref, dws_ref, dbs_ref):
        i = pl.program_id(0)

        @pl.when(i == 0)
        def _():
            dlng_ref[...] = jnp.zeros_like(dlng_ref)
            dlnb_ref[...] = jnp.zeros_like(dlnb_ref)
            dws_ref[...] = jnp.zeros_like(dws_ref)
            dbs_ref[...] = jnp.zeros_like(dbs_ref)

        wc, tri = _causal_ws(ws_ref)
        lng_v = lng_ref[...]
        for c in range(tm // CHUNK):
            rows = slice(c * CHUNK, (c + 1) * CHUNK)
            zu, zv, u, xhat, rs, vln, s = _sgu_pieces(z_ref[rows, :], lng_v, lnb_ref[...], wc, bs_ref)
            dgc = dg_ref[rows, :].astype(F32)
            du, dvln = [], []
            for g in range(GM_GROUPS):
                cols = slice(g * gd, (g + 1) * gd)
                dgg = dgc[:, cols]
                du.append(dgg * s[g])
                ds = dgg * u[:, cols]
                dsb = ds.astype(BF16)
                dws_ref[g] += _dot_nt(dsb, vln[:, cols].astype(BF16))
                dbs_ref[g] += jnp.sum(ds, axis=-1, keepdims=True)
                dvln.append(_dot_tn(wc[g], dsb))
            du = jnp.concatenate(du, axis=1)
            dvln = jnp.concatenate(dvln, axis=1)
            dlng_ref[...] += jnp.sum(dvln * xhat, axis=0, keepdims=True)
            dlnb_ref[...] += jnp.sum(dvln, axis=0, keepdims=True)
            dxh = dvln * lng_v
            m1 = jnp.mean(dxh, axis=-1, keepdims=True)
            m2 = jnp.mean(dxh * xhat, axis=-1, keepdims=True)
            dv = rs * (dxh - m1 - xhat * m2)
            dz_ref[rows, :E] = (du * _gelu_grad(zu)).astype(BF16)
            dz_ref[rows, E:] = (dv * _gelu_grad(zv)).astype(BF16)

        @pl.when(i == nsteps - 1)
        def _():
            for g in range(GM_GROUPS):
                dws_ref[g] = jnp.where(tri, dws_ref[g], 0.0)

    full = lambda shape: pl.BlockSpec(shape, lambda i: (0,) * len(shape))
    return pl.pallas_call(
        body, name=name, grid=(nsteps,),
        in_specs=[pl.BlockSpec((tm, E2), lambda i: (i, 0)), pl.BlockSpec((tm, E), lambda i: (i, 0)),
                  full((1, E)), full((1, E)), full((GM_GROUPS, CHUNK, CHUNK)), full((GM_GROUPS, CHUNK, 1))],
        out_specs=[pl.BlockSpec((tm, E2), lambda i: (i, 0)), full((1, E)), full((1, E)),
                   full((GM_GROUPS, CHUNK, CHUNK)), full((GM_GROUPS, CHUNK, 1))],
        out_shape=[jax.ShapeDtypeStruct((T, E2), BF16), jax.ShapeDtypeStruct((1, E), F32),
                   jax.ShapeDtypeStruct((1, E), F32), jax.ShapeDtypeStruct((GM_GROUPS, CHUNK, CHUNK), F32),
                   jax.ShapeDtypeStruct((GM_GROUPS, CHUNK, 1), F32)],
        compiler_params=_cp(("arbitrary",)),
    )(z, dg, lng, lnb, ws, bs)


HALO = 16


def _conv_taps(a_ext, w_ref, b_ref):
    n = a_ext.shape[0]
    am1 = pltpu.roll(a_ext, 1, 0)
    am2 = pltpu.roll(a_ext, 2, 0)
    del n
    return ((b_ref[...] + am2 * w_ref[0:1, :]) + am1 * w_ref[1:2, :]) + a_ext * w_ref[2:3, :], am1, am2


def _ffn_up_fused(hn, wgu, cw, cb, name):
    T, D = hn.shape
    nb2, _, w = wgu.shape
    nb = nb2 // 2
    F = nb * w
    tm = min(TM, T)

    def body(a_ref, w_ref, cw_ref, cb_ref, au_ref, act_ref, halo_ref):
        @pl.when(pl.program_id(0) == 0)
        def _():
            halo_ref[...] = jnp.zeros_like(halo_ref)

        av = a_ref[...]
        for j in range(nb):
            cols = slice(j * w, (j + 1) * w)
            g = jnp.dot(av, w_ref[j], preferred_element_type=F32)
            u = jnp.dot(av, w_ref[nb + j], preferred_element_type=F32)
            au_ref[:, cols] = g
            au_ref[:, F + j * w:F + (j + 1) * w] = u
            ext = jnp.concatenate([halo_ref[:, cols], g], axis=0)
            am1 = pltpu.roll(ext, 1, 0)
            am2 = pltpu.roll(ext, 2, 0)
            conv = ((cb_ref[:, cols] + am2 * cw_ref[0:1, cols]) + am1 * cw_ref[1:2, cols]) + ext * cw_ref[2:3, cols]
            conv = conv[HALO:, :]
            act_ref[:, cols] = ((conv * _sigmoid(conv)) * u).astype(BF16)
            halo_ref[:, cols] = g[tm - HALO:, :]

    return pl.pallas_call(
        body, name=name, grid=(T // tm,),
        in_specs=[pl.BlockSpec((tm, D), lambda i: (i, 0)), pl.BlockSpec((nb2, D, w), lambda i: (0, 0, 0)),
                  pl.BlockSpec((3, F), lambda i: (0, 0)), pl.BlockSpec((1, F), lambda i: (0, 0))],
        out_specs=[pl.BlockSpec((tm, 2 * F), lambda i: (i, 0)), pl.BlockSpec((tm, F), lambda i: (i, 0))],
        out_shape=[jax.ShapeDtypeStruct((T, 2 * F), F32), jax.ShapeDtypeStruct((T, F), BF16)],
        scratch_shapes=[pltpu.VMEM((HALO, F), F32)],
        compiler_params=_cp(("arbitrary",)),
    )(hn, wgu, cw, cb)


def _ffn_mid_fwd(au, cw, cb, name):
    T, F = au.shape[0], au.shape[1] // 2
    tm, tf = min(TM, T), min(TF, F)
    hb = tm // HALO
    nf = F // tf

    def body(a_ref, ap_ref, u_ref, w_ref, b_ref, o_ref):
        i = pl.program_id(1)
        prev = jnp.where(i == 0, 0.0, ap_ref[...])
        ext = jnp.concatenate([prev, a_ref[...]], axis=0)
        conv, _, _ = _conv_taps(ext, w_ref, b_ref)
        conv = conv[HALO:, :]
        o_ref[...] = ((conv * _sigmoid(conv)) * u_ref[...]).astype(BF16)

    main = pl.BlockSpec((tm, tf), lambda f, i: (i, f))
    return pl.pallas_call(
        body, name=name, grid=(nf, T // tm),
        in_specs=[main, pl.BlockSpec((HALO, tf), lambda f, i: (jnp.maximum(i * hb - 1, 0), f)),
                  pl.BlockSpec((tm, tf), lambda f, i: (i, nf + f)),
                  pl.BlockSpec((3, tf), lambda f, i: (0, f)), pl.BlockSpec((1, tf), lambda f, i: (0, f))],
        out_specs=main, out_shape=jax.ShapeDtypeStruct((T, F), BF16),
        compiler_params=_cp(("parallel", "parallel")),
    )(au, au, au, cw, cb)


def _ffn_mid_bwd(au, dact, cw, cb, name):
    T, F = au.shape[0], au.shape[1] // 2
    tm, tf = min(TM, T), min(TF, F)
    hb = tm // HALO
    nt = T // tm
    nf = F // tf
    last_h = T // HALO - 1

    def body(a_ref, ap_ref, an_ref, u_ref, un_ref, d_ref, dn_ref, w_ref, b_ref, da_ref, du_ref, dcw_ref, dcb_ref):
        i = pl.program_id(1)
        prev = jnp.where(i == 0, 0.0, ap_ref[...])
        a_main = a_ref[...]
        a_ext = jnp.concatenate([prev, a_main, an_ref[...]], axis=0)
        conv, am1, am2 = _conv_taps(a_ext, w_ref, b_ref)
        conv = conv[HALO:, :]
        sig = _sigmoid(conv)
        u_ext = jnp.concatenate([u_ref[...], un_ref[...]], axis=0)
        d_ext = jnp.concatenate([d_ref[...], dn_ref[...]], axis=0).astype(F32)
        n = tm + HALO
        row = lax.broadcasted_iota(jnp.int32, (n, 1), 0)
        live = jnp.logical_or(row < tm, i < nt - 1)
        dconv = jnp.where(live, d_ext * u_ext * (sig * (1.0 + conv * (1.0 - sig))), 0.0)
        du_ref[...] = (d_ext[:tm, :] * (conv[:tm, :] * sig[:tm, :])).astype(BF16)
        dp1 = pltpu.roll(dconv, n - 1, 0)[:tm, :]
        dp2 = pltpu.roll(dconv, n - 2, 0)[:tm, :]
        dc = dconv[:tm, :]
        da_ref[...] = ((dc * w_ref[2:3, :] + dp1 * w_ref[1:2, :]) + dp2 * w_ref[0:1, :]).astype(BF16)
        g2 = jnp.sum(dc * a_main, axis=0, keepdims=True)
        g1 = jnp.sum(dc * am1[HALO:HALO + tm, :], axis=0, keepdims=True)
        g0 = jnp.sum(dc * am2[HALO:HALO + tm, :], axis=0, keepdims=True)
        gb = jnp.sum(dc, axis=0, keepdims=True)

        @pl.when(i == 0)
        def _():
            dcw_ref[...] = jnp.zeros_like(dcw_ref)
            dcb_ref[...] = jnp.zeros_like(dcb_ref)

        dcw_ref[0:1, :] += g0
        dcw_ref[1:2, :] += g1
        dcw_ref[2:3, :] += g2
        dcb_ref[...] += gb

    main = pl.BlockSpec((tm, tf), lambda f, i: (i, f))
    prev = pl.BlockSpec((HALO, tf), lambda f, i: (jnp.maximum(i * hb - 1, 0), f))
    nxt = pl.BlockSpec((HALO, tf), lambda f, i: (jnp.minimum((i + 1) * hb, last_h), f))
    main_u = pl.BlockSpec((tm, tf), lambda f, i: (i, nf + f))
    nxt_u = pl.BlockSpec((HALO, tf), lambda f, i: (jnp.minimum((i + 1) * hb, last_h), nf + f))
    return pl.pallas_call(
        body, name=name, grid=(nf, nt),
        in_specs=[main, prev, nxt, main_u, nxt_u, main, nxt,
                  pl.BlockSpec((3, tf), lambda f, i: (0, f)), pl.BlockSpec((1, tf), lambda f, i: (0, f))],
        out_specs=[main, main, pl.BlockSpec((3, tf), lambda f, i: (0, f)), pl.BlockSpec((1, tf), lambda f, i: (0, f))],
        out_shape=[jax.ShapeDtypeStruct((T, F), BF16), jax.ShapeDtypeStruct((T, F), BF16),
                   jax.ShapeDtypeStruct((3, F), F32), jax.ShapeDtypeStruct((1, F), F32)],
        compiler_params=_cp(("parallel", "arbitrary")),
    )(au, au, au, au, au, dact, dact, cw, cb)


def _split3(x):
    hi = x.astype(BF16)
    r1 = x - hi.astype(F32)
    mid = r1.astype(BF16)
    lo = (r1 - mid.astype(F32)).astype(BF16)
    return hi, mid, lo


def _tri_ones(n, upper):
    r = lax.broadcasted_iota(jnp.int32, (n, n), 0)
    c = lax.broadcasted_iota(jnp.int32, (n, n), 1)
    return jnp.where((r <= c) if upper else (r >= c), 1.0, 0.0).astype(BF16)


def _gate_scan(f, bf, name):
    T = f.shape[0]
    tm = min(256, T)

    def body(f_ref, b_ref, cp_ref, sn_ref, carry_ref):
        i = pl.program_id(0)

        @pl.when(i == 0)
        def _():
            carry_ref[...] = jnp.zeros_like(carry_ref)

        x = f_ref[...] + b_ref[...]
        e = jnp.exp(-jnp.abs(x))
        logf = jnp.minimum(x, 0.0) - jnp.log(1.0 + e)
        sn_ref[...] = jnp.where(x >= 0.0, e / (1.0 + e), 1.0 / (1.0 + e))
        tri = _tri_ones(tm, upper=False)
        c = carry_ref[...]
        for piece in _split3(logf):
            c = c + jnp.dot(tri, piece, preferred_element_type=F32)
        carry_ref[...] += jnp.sum(logf, axis=0, keepdims=True)
        hi, mid, lo = _split3(c * LOG2E)
        cp_ref[:, 0:LANES] = hi
        cp_ref[:, LANES:2 * LANES] = mid
        cp_ref[:, 2 * LANES:3 * LANES] = lo

    return pl.pallas_call(
        body, name=name, grid=(T // tm,),
        in_specs=[pl.BlockSpec((tm, LANES), lambda i: (i, 0)), pl.BlockSpec((1, LANES), lambda i: (0, 0))],
        out_specs=[pl.BlockSpec((tm, 3 * LANES), lambda i: (i, 0)), pl.BlockSpec((tm, LANES), lambda i: (i, 0))],
        out_shape=[jax.ShapeDtypeStruct((T, 3 * LANES), BF16), jax.ShapeDtypeStruct((T, LANES), F32)],
        scratch_shapes=[pltpu.VMEM((1, LANES), F32)],
        compiler_params=_cp(("arbitrary",)),
    )(f, bf)


def _gate_scan_bwd(dcq, dck, sneg, name):
    T = dcq.shape[0]
    tm = min(256, T)
    n = T // tm

    def body(dcq_ref, dck_ref, sn_ref, df_ref, db_ref, carry_ref):
        i = pl.program_id(0)

        @pl.when(i == 0)
        def _():
            carry_ref[...] = jnp.zeros_like(carry_ref)
            db_ref[...] = jnp.zeros_like(db_ref)

        tri = _tri_ones(tm, upper=True)
        dcb = dcq_ref[...] - dck_ref[...]
        acc = carry_ref[...]
        for piece in _split3(dcb):
            acc = acc + jnp.dot(tri, piece, preferred_element_type=F32)
        carry_ref[...] += jnp.sum(dcb, axis=0, keepdims=True)
        df = acc * sn_ref[...]
        df_ref[...] = df.astype(BF16)
        db_ref[...] += jnp.sum(df, axis=0, keepdims=True)

    rev = pl.BlockSpec((tm, LANES), lambda i: (n - 1 - i, 0))
    return pl.pallas_call(
        body, name=name, grid=(n,),
        in_specs=[rev, rev, rev],
        out_specs=[rev, pl.BlockSpec((1, LANES), lambda i: (0, 0))],
        out_shape=[jax.ShapeDtypeStruct((T, LANES), BF16), jax.ShapeDtypeStruct((1, LANES), F32)],
        scratch_shapes=[pltpu.VMEM((1, LANES), F32)],
        compiler_params=_cp(("arbitrary",)),
    )(dcq, dck, sneg)


def _qk_proj(hn, w_pads, cp, sels, consts, scales, name):
    T, D = hn.shape
    H = w_pads[0].shape[1] // LANES
    tm = min(TM, T)

    def body(a_ref, cp_ref, wq_ref, wk_ref, sq_ref, sk_ref, cq_ref, ck_ref, qo_ref, ko_ref):
        a = a_ref[...]
        cpv = cp_ref[...]
        for w_ref, sel_ref, c_ref, o_ref, scale in ((wq_ref, sq_ref, cq_ref, qo_ref, scales[0]),
                                                    (wk_ref, sk_ref, ck_ref, ko_ref, scales[1])):
            for p in range(H // 2):
                acc = jnp.dot(a, w_ref[:, p * 2 * LANES:(p + 1) * 2 * LANES], preferred_element_type=F32)
                if scale != 1.0:
                    acc = acc * scale
                acc = acc + jnp.dot(cpv, sel_ref[p], preferred_element_type=F32) + c_ref[p]
                o_ref[2 * p] = acc[:, :LANES].astype(BF16)
                o_ref[2 * p + 1] = acc[:, LANES:].astype(BF16)

    whole = lambda t: pl.BlockSpec(t.shape, lambda i: (0,) * t.ndim)
    out = jax.ShapeDtypeStruct((H, T, LANES), BF16)
    o_spec = pl.BlockSpec((H, tm, LANES), lambda i: (0, i, 0))
    return pl.pallas_call(
        body, name=name, grid=(T // tm,),
        in_specs=[pl.BlockSpec((tm, D), lambda i: (i, 0)), pl.BlockSpec((tm, 3 * LANES), lambda i: (i, 0)),
                  whole(w_pads[0]), whole(w_pads[1]), whole(sels[0]), whole(sels[1]), whole(consts[0]), whole(consts[1])],
        out_specs=[o_spec, o_spec], out_shape=[out, out],
        compiler_params=_cp(("parallel",)),
    )(hn, cp, w_pads[0], w_pads[1], sels[0], sels[1], consts[0], consts[1])


def _lane_lo():
    return lax.broadcasted_iota(jnp.int32, (1, LANES), 1) < HEAD_DIM


def _attn_fwd(qp, kp, v, name):
    H, T, _ = qp.shape
    tq = min(TQ, T)
    nrep = tq // LANES
    n_parts = 4 if tq % 512 == 0 else 1
    rows = tq // n_parts

    def body(q_ref, k_ref, v_ref, o_ref, o32_ref, lse_ref, m_sc, acc_sc):
        i = pl.program_id(1)
        m_sc[...] = jnp.full(m_sc.shape, NEG, F32)
        acc_sc[...] = jnp.zeros_like(acc_sc)
        ones_col = jnp.where(lax.broadcasted_iota(jnp.int32, (tq, LANES), 1) == 0, 1.0, 0.0).astype(BF16)

        def step(j, masked):
            off = pl.multiple_of(j * tq, tq)
            vaug = jnp.concatenate([v_ref[pl.ds(off, tq), :], ones_col], axis=1)
            chains = [(h, rp) for h in range(2) for rp in range(n_parts)]
            s_all = [_dot_nt(q_ref[h, rp * rows:(rp + 1) * rows, :], k_ref[h, pl.ds(off, tq), :]) for h, rp in chains]
            for (h, rp), s in zip(chains, s_all):
                rsl = slice(rp * rows, (rp + 1) * rows)
                tiles = [s[:, c * LANES:(c + 1) * LANES] for c in range(nrep)]
                if masked:
                    r = lax.broadcasted_iota(jnp.int32, (rows, LANES), 0) + rp * rows
                    cc = lax.broadcasted_iota(jnp.int32, (rows, LANES), 1)
                    tiles = [jnp.where(r >= cc + c * LANES, t, NEG) for c, t in enumerate(tiles)]
                mt = tiles[0]
                for t in tiles[1:]:
                    mt = jnp.maximum(mt, t)
                m_prev = m_sc[h, rsl, :]
                m_new = jnp.maximum(m_prev, jnp.max(mt, axis=-1, keepdims=True))
                alpha = jnp.exp2(m_prev - m_new)
                p16 = jnp.concatenate([jnp.exp2(t - m_new).astype(BF16) for t in tiles], axis=1)
                pv = jnp.dot(p16, vaug, preferred_element_type=F32)
                acc_sc[h, rsl, :] = jnp.concatenate([alpha, alpha], axis=1) * acc_sc[h, rsl, :] + pv
                m_sc[h, rsl, :] = m_new

        def loop_body(j, carry):
            step(j, False)
            return carry

        lax.fori_loop(0, i, loop_body, 0)
        step(i, True)
        lo = _lane_lo()
        acc0, acc1 = acc_sc[0], acc_sc[1]
        l0 = jnp.sum(acc0[:, LANES:], axis=-1, keepdims=True)
        l1 = jnp.sum(acc1[:, LANES:], axis=-1, keepdims=True)
        o = jnp.where(lo, acc0[:, :LANES] / l0, acc1[:, :LANES] / l1)
        o_ref[...] = o.astype(BF16)
        o32_ref[...] = o
        lse_ref[...] = jnp.where(lo, m_sc[0] + jnp.log(l0) * LOG2E, m_sc[1] + jnp.log(l1) * LOG2E)

    oblk = pl.BlockSpec((tq, LANES), lambda p, i: (i, p))
    return pl.pallas_call(
        body, name=name, grid=(H // 2, T // tq),
        in_specs=[pl.BlockSpec((2, tq, LANES), lambda p, i: (p, i, 0)),
                  pl.BlockSpec((2, T, LANES), lambda p, i: (p, 0, 0)),
                  pl.BlockSpec((T, LANES), lambda p, i: (0, p))],
        out_specs=[oblk, oblk, pl.BlockSpec((None, tq, LANES), lambda p, i: (p, i, 0))],
        out_shape=[jax.ShapeDtypeStruct((T, H * HEAD_DIM), BF16), jax.ShapeDtypeStruct((T, H * HEAD_DIM), F32),
                   jax.ShapeDtypeStruct((H // 2, T, LANES), F32)],
        scratch_shapes=[pltpu.VMEM((2, tq, LANES), F32), pltpu.VMEM((2, tq, 2 * LANES), F32)],
        compiler_params=_cp(("parallel", "arbitrary")),
    )(qp, kp, v)


def _attn_fwd_t(qp, kp, vt, name):
    H, T, _ = qp.shape
    tq = min(TQ, T)
    hd = HEAD_DIM
    ext = hd + 16

    def body(q_ref, k_ref, vt_ref, o_ref, o32_ref, lse_ref, m_sc, acc_sc):
        i = pl.program_id(1)
        m_sc[...] = jnp.full(m_sc.shape, NEG, F32)
        acc_sc[...] = jnp.zeros_like(acc_sc)
        q_t = [jnp.transpose(q_ref[h].astype(F32)).astype(BF16) for h in range(2)]
        ones_rows = jnp.where(lax.broadcasted_iota(jnp.int32, (16, tq), 0) == 0, 1.0, 0.0).astype(BF16)

        def steps(blocks):
            offs = [pl.multiple_of(j * tq, tq) for j, _ in blocks]
            s_all = [[jnp.dot(k_ref[h, pl.ds(off, tq), :], q_t[h], preferred_element_type=F32) for h in range(2)]
                     for off in offs]
            for (j, masked), off, s_blk in zip(blocks, offs, s_all):
                for h in range(2):
                    s = s_blk[h]
                    if masked:
                        kr = lax.broadcasted_iota(jnp.int32, (tq, tq), 0)
                        qc = lax.broadcasted_iota(jnp.int32, (tq, tq), 1)
                        s = jnp.where(qc >= kr, s, NEG)
                    m_prev = m_sc[h]
                    m_new = jnp.maximum(m_prev, jnp.max(s, axis=0, keepdims=True))
                    alpha = jnp.exp2(m_prev - m_new)
                    p16 = jnp.exp2(s - m_new).astype(BF16)
                    v_aug = jnp.concatenate([vt_ref[h * hd:(h + 1) * hd, pl.ds(off, tq)], ones_rows], axis=0)
                    pv = jnp.dot(v_aug, p16, preferred_element_type=F32)
                    acc_sc[h] = alpha * acc_sc[h] + pv
                    m_sc[h] = m_new

        def group_body(t, carry):
            steps([(KV_UNROLL * t + u, False) for u in range(KV_UNROLL)])
            return carry

        lax.fori_loop(0, i // KV_UNROLL, group_body, 0)
        for rem in range(KV_UNROLL):

            @pl.when(i % KV_UNROLL == rem)
            def _(rem=rem):
                steps([(i - rem + u, u == rem) for u in range(rem + 1)])

        o_t, lse_t = [], []
        for h in range(2):
            acc = acc_sc[h]
            l = acc[hd:hd + 1, :]
            o_t.append(acc[:hd, :] / l)
            lse_t.append(jnp.broadcast_to(m_sc[h] + jnp.log(l) * LOG2E, (hd, tq)))
        o = jnp.transpose(jnp.concatenate(o_t, axis=0))
        o_ref[...] = o.astype(BF16)
        o32_ref[...] = o
        lse_ref[...] = jnp.transpose(jnp.concatenate(lse_t, axis=0))

    oblk = pl.BlockSpec((tq, LANES), lambda p, i: (i, p))
    return pl.pallas_call(
        body, name=name, grid=(H // 2, T // tq),
        in_specs=[pl.BlockSpec((2, tq, LANES), lambda p, i: (p, i, 0)),
                  pl.BlockSpec((2, T, LANES), lambda p, i: (p, 0, 0)),
                  pl.BlockSpec((2 * hd, T), lambda p, i: (p, 0))],
        out_specs=[oblk, oblk, pl.BlockSpec((None, tq, LANES), lambda p, i: (p, i, 0))],
        out_shape=[jax.ShapeDtypeStruct((T, H * HEAD_DIM), BF16), jax.ShapeDtypeStruct((T, H * HEAD_DIM), F32),
                   jax.ShapeDtypeStruct((H // 2, T, LANES), F32)],
        scratch_shapes=[pltpu.VMEM((2, 1, tq), F32), pltpu.VMEM((2, ext, tq), F32)],
        compiler_params=_cp(("parallel", "arbitrary")),
    )(qp, kp, vt)


def _attn_bwd(qp, kp, v, o, do, lse, scale, name):
    H, T, _ = qp.shape
    tq = min(TQ, T)
    nq = T // tq
    nrep = tq // LANES

    def body(q_ref, k_ref, v_ref, o_ref, do_ref, lse_ref, dq_ref, dk_ref, dv_ref, dqe_ref, dke_ref, dk_sc, dv_sc, dq_sc):
        i = pl.program_id(1)

        @pl.when(i == 0)
        def _():
            dk_sc[...] = jnp.zeros_like(dk_sc)
            dv_sc[...] = jnp.zeros_like(dv_sc)

        dq_sc[...] = jnp.zeros_like(dq_sc)

        lo = _lane_lo()
        dob = do_ref[...]
        dof = dob.astype(F32)
        prod = dof * o_ref[...].astype(F32)
        lse2 = lse_ref[...]
        lse2_sw = pltpu.roll(lse2, HEAD_DIM, 1)
        zero = jnp.zeros_like(dob)
        do_h = [jnp.where(lo, dob, zero), jnp.where(lo, zero, dob)]
        rep = lambda col: jnp.broadcast_to(col, (tq, LANES))
        delta = [rep(jnp.sum(jnp.where(lo, prod, 0.0), axis=-1, keepdims=True)),
                 rep(jnp.sum(jnp.where(lo, 0.0, prod), axis=-1, keepdims=True))]
        lse_h = [jnp.where(lo, lse2, lse2_sw), jnp.where(lo, lse2_sw, lse2)]
        qs = [q_ref[0], q_ref[1]]
        tr16 = lambda a: jnp.transpose(a.astype(F32)).astype(BF16)
        q_t = [tr16(qs[0]), tr16(qs[1])]
        do_t = [tr16(do_h[0]), tr16(do_h[1])]

        def steps(blocks):
            offs = [pl.multiple_of(j * tq, tq) for j, _ in blocks]
            vblks = [v_ref[pl.ds(off, tq), :] for off in offs]
            kblks = [[k_ref[h, pl.ds(off, tq), :] for h in range(2)] for off in offs]
            s_all = [[_dot_nt(qs[h], kb[h]) for h in range(2)] for kb in kblks]
            dp_all = [[_dot_nt(do_h[h], vb) for h in range(2)] for vb in vblks]
            for b, ((j, masked), off) in enumerate(zip(blocks, offs)):
                dv_add = None
                for h in range(2):
                    kblk, s, dp = kblks[b][h], s_all[b][h], dp_all[b][h]
                    p16, ds16 = [], []
                    for c in range(nrep):
                        cols = slice(c * LANES, (c + 1) * LANES)
                        p = jnp.exp2(s[:, cols] - lse_h[h])
                        if masked:
                            r = lax.broadcasted_iota(jnp.int32, (tq, LANES), 0)
                            cc = lax.broadcasted_iota(jnp.int32, (tq, LANES), 1)
                            p = jnp.where(r >= cc + c * LANES, p, 0.0)
                        p16.append(p.astype(BF16))
                        ds16.append((p * (dp[:, cols] - delta[h])).astype(BF16))
                    p16 = jnp.concatenate(p16, axis=1)
                    dsb = jnp.concatenate(ds16, axis=1)
                    dq_sc[h] += jnp.dot(dsb, kblk, preferred_element_type=F32)
                    dk_sc[h, :, pl.ds(off, tq)] += jnp.dot(q_t[h], dsb, preferred_element_type=F32)
                    pv = jnp.dot(do_t[h], p16, preferred_element_type=F32)
                    dv_add = pv if dv_add is None else dv_add + pv
                dv_sc[:, pl.ds(off, tq)] += dv_add

        def group_body(t, carry):
            steps([(KV_UNROLL_BWD * t + u, False) for u in range(KV_UNROLL_BWD)])
            return carry

        lax.fori_loop(0, i // KV_UNROLL_BWD, group_body, 0)
        for rem in range(KV_UNROLL_BWD):

            @pl.when(i % KV_UNROLL_BWD == rem)
            def _(rem=rem):
                steps([(i - rem + u, u == rem) for u in range(rem + 1)])

        dq0, dq1 = dq_sc[0], dq_sc[1]
        dq_ref[...] = (jnp.where(lo, dq0, pltpu.roll(dq1, HEAD_DIM, 1)) * scale).astype(BF16)
        dqe_ref[0:8, :] = jnp.transpose(dq0)[HEAD_DIM:HEAD_DIM + 8, :]
        dqe_ref[8:16, :] = jnp.transpose(dq1)[HEAD_DIM:HEAD_DIM + 8, :]

        @pl.when(i == nq - 1)
        def _():
            for h in range(2):
                dke_ref[8 * h:8 * h + 8, :] = dk_sc[h, HEAD_DIM:HEAD_DIM + 8, :]
            for cb in range(nq):
                tok = slice(cb * tq, (cb + 1) * tq)
                dk0 = jnp.transpose(dk_sc[0, :, tok])
                dk1 = jnp.transpose(dk_sc[1, :, tok])
                dk_ref[tok, :] = (jnp.where(lo, dk0, pltpu.roll(dk1, HEAD_DIM, 1)) * LN2).astype(BF16)
                dv_ref[tok, :] = jnp.transpose(dv_sc[:, tok]).astype(BF16)

    qblk = pl.BlockSpec((tq, LANES), lambda p, i: (i, p))
    pair = pl.BlockSpec((T, LANES), lambda p, i: (0, p))
    tok16 = jax.ShapeDtypeStruct((T, H * HEAD_DIM), BF16)
    gate32 = jax.ShapeDtypeStruct((H // 2, 16, T), F32)
    return pl.pallas_call(
        body, name=name, grid=(H // 2, nq),
        in_specs=[pl.BlockSpec((2, tq, LANES), lambda p, i: (p, i, 0)),
                  pl.BlockSpec((2, T, LANES), lambda p, i: (p, 0, 0)),
                  pair, qblk, qblk,
                  pl.BlockSpec((None, tq, LANES), lambda p, i: (p, i, 0))],
        out_specs=[qblk, pair, pair, pl.BlockSpec((None, 16, tq), lambda p, i: (p, 0, i)),
                   pl.BlockSpec((None, 16, T), lambda p, i: (p, 0, 0))],
        out_shape=[tok16, tok16, tok16, gate32, gate32],
        scratch_shapes=[pltpu.VMEM((2, LANES, T), F32), pltpu.VMEM((LANES, T), F32),
                        pltpu.VMEM((2, tq, LANES), F32)],
        compiler_params=_cp(("parallel", "arbitrary")),
    )(qp, kp, v, o, do, lse)


def _mesh_pos():
    return lax.axis_index("x"), lax.axis_index("y"), lax.axis_index("c")


def _all_gather(arrs, name, groups=None):
    n = len(arrs)
    if groups is None:
        groups = [(a, 0) for a in range(n)]
    ng = 1 + max(g for g, _ in groups)
    per_group = [sum(1 for g, _ in groups if g == gi) for gi in range(ng)]
    first_of = [next(a for a in range(n) if groups[a][0] == gi) for gi in range(ng)]

    def body(*refs):
        ins, outs = refs[:n], refs[n:n + ng]
        send_sems, recv_sems, local_sems = refs[n + ng:]
        x, y, c = _mesh_pos()
        me, sib = (x, y, c), (x, y, 1 - c)
        chips = [(1 - x, y), (x, 1 - y), (1 - x, 1 - y)]

        def dst_of(a, px, py, pc):
            g, k = groups[a]
            return outs[g].at[N_DEV * k + 4 * px + 2 * py + pc]

        def copy(a, k, block, to, src=None):
            dst = dst_of(a, *block)
            return pltpu.make_async_remote_copy(
                src_ref=dst if src is None else src, dst_ref=dst,
                send_sem=send_sems.at[a, k], recv_sem=recv_sems.at[a, k], device_id=to, device_id_type=MESH)

        mine = [pltpu.make_async_copy(ins[a], dst_of(a, *me), local_sems.at[a]) for a in range(n)]
        for cp in mine:
            cp.start()
        first = []
        for a in range(n):
            first.append(copy(a, 0, me, sib, src=ins[a]))
            first += [copy(a, 1 + j, me, (*chip, c), src=ins[a]) for j, chip in enumerate(chips)]
        for cp in first:
            cp.start()
        passed = []
        for j, chip in enumerate(chips):
            for a in range(n):
                copy(a, 1 + j, (*chip, c), me).wait_recv()
                fwd = copy(a, 4 + j, (*chip, c), sib)
                fwd.start()
                passed.append(fwd)
        for a in range(n):
            copy(a, 0, sib, me).wait_recv()
            for j, chip in enumerate(chips):
                copy(a, 4 + j, (*chip, 1 - c), me).wait_recv()
        for cp in first + passed:
            cp.wait_send()
        for cp in mine:
            cp.wait()

    any_spec = pl.BlockSpec(memory_space=pl.ANY)
    return pl.pallas_call(
        body, name=name,
        in_specs=[any_spec] * n, out_specs=[any_spec] * ng,
        out_shape=[jax.ShapeDtypeStruct((N_DEV * per_group[gi],) + arrs[first_of[gi]].shape, arrs[first_of[gi]].dtype)
                   for gi in range(ng)],
        scratch_shapes=[pltpu.SemaphoreType.DMA((n, 7)), pltpu.SemaphoreType.DMA((n, 7)),
                        pltpu.SemaphoreType.DMA((n,))],
    )(*arrs)


def _pair_exchange(gs, name):
    n = len(gs)

    def body(*refs):
        g_refs, o_refs = refs[:n], refs[n:2 * n]
        send_sems, recv_sems = refs[2 * n:]
        x, y, c = _mesh_pos()
        sib = (x, y, 1 - c)
        copies = []
        for a in range(n):
            for j in range(4):
                copies.append(pltpu.make_async_remote_copy(
                    src_ref=g_refs[a].at[2 * j + (1 - c)], dst_ref=o_refs[a].at[j],
                    send_sem=send_sems.at[a, j], recv_sem=recv_sems.at[a, j], device_id=sib, device_id_type=MESH))
        for cp in copies:
            cp.start()
        for cp in copies:
            cp.wait_recv()
        for cp in copies:
            cp.wait_send()

    any_spec = pl.BlockSpec(memory_space=pl.ANY)
    return pl.pallas_call(
        body, name=name, in_specs=[any_spec] * n, out_specs=[any_spec] * n,
        out_shape=[jax.ShapeDtypeStruct((4,) + g.shape[1:], g.dtype) for g in gs],
        scratch_shapes=[pltpu.SemaphoreType.DMA((n, 4)), pltpu.SemaphoreType.DMA((n, 4))],
    )(*gs)


def _chip_exchange(parts, name):
    n = len(parts)

    def body(*refs):
        p_refs, o_refs = refs[:n], refs[n:2 * n]
        send_sems, recv_sems = refs[2 * n:]
        x, y, c = _mesh_pos()
        chips = [(1 - x, y), (x, 1 - y), (1 - x, 1 - y)]
        copies = []
        for a in range(n):
            for k, (px, py) in enumerate(chips):
                copies.append(pltpu.make_async_remote_copy(
                    src_ref=p_refs[a].at[2 * px + py], dst_ref=o_refs[a].at[k],
                    send_sem=send_sems.at[a, k], recv_sem=recv_sems.at[a, k], device_id=(px, py, c),
                    device_id_type=MESH))
        for cp in copies:
            cp.start()
        for cp in copies:
            cp.wait_recv()
        for cp in copies:
            cp.wait_send()

    any_spec = pl.BlockSpec(memory_space=pl.ANY)
    return pl.pallas_call(
        body, name=name, in_specs=[any_spec] * n, out_specs=[any_spec] * n,
        out_shape=[jax.ShapeDtypeStruct((3,) + p.shape[1:], p.dtype) for p in parts],
        scratch_shapes=[pltpu.SemaphoreType.DMA((n, 3)), pltpu.SemaphoreType.DMA((n, 3))],
    )(*parts)


HBM_SPEC = pl.BlockSpec(memory_space=pltpu.HBM)
SEM_SPEC = pl.BlockSpec(memory_space=pltpu.SEMAPHORE)
ANY_SPEC = pl.BlockSpec(memory_space=pl.ANY)
DATAFLOW_EFFECT = pltpu.SideEffectType.DATAFLOW_SIDE_EFFECTING


def _peers():
    x, y, c = _mesh_pos()
    flip = lambda v, b: 1 - v if b else v
    return [(flip(x, (k >> 2) & 1), flip(y, (k >> 1) & 1), flip(c, k & 1)) for k in range(1, N_DEV)]


def _slot(p):
    return 4 * p[0] + 2 * p[1] + p[2]


def _direct_copy(src_refs, land_refs, sems, a, k, p, land_of, dst_slot, src_slot):
    s = src_slot(a, p)
    return pltpu.make_async_remote_copy(
        src_ref=src_refs[a] if s is None else src_refs[a].at[s], dst_ref=land_refs[land_of[a]].at[dst_slot(a, k)],
        send_sem=sems[0].at[a * (N_DEV - 1) + k], recv_sem=sems[1].at[a * (N_DEV - 1) + k], device_id=p,
        device_id_type=MESH)


def _direct_start(srcs, lands, land_of, dst_slot, src_slot, after, name, collective_id):
    n, nl = len(srcs), len(lands)

    def body(*refs):
        src_refs, land_refs = refs[:n], refs[n:n + nl]
        sems = (refs[n + nl + 1], refs[n + nl + 2])
        token = refs[-1]
        peers = _peers()
        barrier = pltpu.get_barrier_semaphore()
        for p in peers:
            pl.semaphore_signal(barrier, inc=1, device_id=p, device_id_type=MESH)
        pl.semaphore_wait(barrier, N_DEV - 1)
        for a in range(n):
            for k, p in enumerate(peers):
                _direct_copy(src_refs, land_refs, sems, a, k, p, land_of, dst_slot, src_slot).start()
        token[...] = jnp.zeros_like(token)

    hbm = lambda t: pltpu.HBM(t.shape, t.dtype)
    sem_t = pltpu.SemaphoreType.DMA((n * (N_DEV - 1),))
    outs = pl.pallas_call(
        body, name=name,
        out_shape=(sem_t, sem_t, *[hbm(t) for t in srcs], *[hbm(t) for t in lands], jax.ShapeDtypeStruct((8, LANES), F32)),
        in_specs=[HBM_SPEC] * (n + nl) + [ANY_SPEC],
        out_specs=(SEM_SPEC, SEM_SPEC, *([HBM_SPEC] * (n + nl)), pl.BlockSpec(memory_space=pltpu.VMEM)),
        input_output_aliases={i: 2 + i for i in range(n + nl)},
        compiler_params=pltpu.CompilerParams(has_side_effects=DATAFLOW_EFFECT, collective_id=collective_id),
    )(*[pltpu.with_memory_space_constraint(t, pltpu.HBM) for t in srcs],
      *[pltpu.with_memory_space_constraint(t, pltpu.HBM) for t in lands], after)
    return outs[0], outs[1], list(outs[2:2 + n]), list(outs[2 + n:2 + n + nl]), outs[-1]


def _direct_wait(send_sems, recv_sems, srcs, lands, land_of, idxs, dst_slot, src_slot, after, name):
    land_ids = []
    for a in idxs:
        if land_of[a] not in land_ids:
            land_ids.append(land_of[a])
    m, ml = len(idxs), len(land_ids)
    sub_land_of = {j: land_ids.index(land_of[a]) for j, a in enumerate(idxs)}

    def body(*refs):
        src_refs, land_refs = refs[:m], refs[m:m + ml]
        ssem, rsem = refs[m + ml], refs[m + ml + 1]
        for j, a in enumerate(idxs):
            for k, p in enumerate(_peers()):
                s = src_slot(a, p)
                cp = pltpu.make_async_remote_copy(
                    src_ref=src_refs[j] if s is None else src_refs[j].at[s],
                    dst_ref=land_refs[sub_land_of[j]].at[dst_slot(a, k)],
                    send_sem=ssem.at[a * (N_DEV - 1) + k], recv_sem=rsem.at[a * (N_DEV - 1) + k], device_id=p,
                    device_id_type=MESH)
                cp.wait_send()
                cp.wait_recv()

    hbm = lambda t: pltpu.HBM(t.shape, t.dtype)
    sub_s, sub_l = [srcs[a] for a in idxs], [lands[g] for g in land_ids]
    outs = pl.pallas_call(
        body, name=name,
        out_shape=(*[hbm(t) for t in sub_s], *[hbm(t) for t in sub_l]),
        in_specs=[HBM_SPEC] * (m + ml) + [SEM_SPEC, SEM_SPEC, ANY_SPEC],
        out_specs=tuple([HBM_SPEC] * (m + ml)),
        input_output_aliases={i: i for i in range(m + ml)},
        compiler_params=pltpu.CompilerParams(has_side_effects=DATAFLOW_EFFECT),
    )(*sub_s, *sub_l, send_sems, recv_sems, after)
    return list(outs[:m]), list(outs[m:])


def _row_block(R, C):
    best = None
    for d in range(16, R + 1, 16):
        if R % d == 0 and d * C <= 256 * 1024:
            best = d
    return best if best is not None else R


def _pair_add(g, recv, cidx, name):
    _, R, C = g.shape
    tr = _row_block(R, C)

    def body(c_ref, g_ref, r_ref, o_ref):
        del c_ref
        o_ref[...] = (g_ref[...].astype(F32) + r_ref[...].astype(F32)).astype(BF16)

    grid_spec = pltpu.PrefetchScalarGridSpec(
        num_scalar_prefetch=1, grid=(4, R // tr),
        in_specs=[pl.BlockSpec((None, tr, C), lambda j, i, c: (2 * j + c[0], i, 0)),
                  pl.BlockSpec((None, tr, C), lambda j, i, c: (j, i, 0))],
        out_specs=pl.BlockSpec((None, tr, C), lambda j, i, c: (j, i, 0)))
    return pl.pallas_call(
        body, name=name, grid_spec=grid_spec,
        out_shape=jax.ShapeDtypeStruct((4, R, C), BF16),
        compiler_params=_cp(("parallel", "parallel")),
    )(cidx, g, recv)


def _adamw_math(w, g, m, v):
    m = ADAM_B1 * m + (1.0 - ADAM_B1) * g
    v = ADAM_B2 * v + (1.0 - ADAM_B2) * (g * g)
    m_hat = m / (1.0 - ADAM_B1 ** ADAM_STEP)
    v_hat = v / (1.0 - ADAM_B2 ** ADAM_STEP)
    delta = -ADAM_LR * (m_hat / (jnp.sqrt(v_hat) + ADAM_EPS) + ADAM_WD * w)
    return delta, m, v


def _sum_adamw(parts, w, m, v, name, sel=None):
    R, C = w.shape
    tr = _row_block(R, C)
    specs, args = [], []
    for arr, idxs in parts:
        for idx in idxs:
            if idx < 0:
                specs.append(pl.BlockSpec((None, tr, C), lambda i, s: (s[0], i, 0)))
            else:
                specs.append(pl.BlockSpec((None, tr, C), lambda i, s, idx=idx: (idx, i, 0)))
            args.append(arr)
    npart = len(args)
    blk = pl.BlockSpec((tr, C), lambda i, s: (i, 0))

    def body(s_ref, *refs):
        del s_ref
        g = refs[0][...].astype(F32)
        for r in refs[1:npart]:
            g = g + r[...].astype(F32)
        w_ref, m_ref, v_ref, g_out, d_out, m_out, v_out = refs[npart:]
        delta, mm, vv = _adamw_math(w_ref[...], g, m_ref[...], v_ref[...])
        g_out[...] = g
        d_out[...] = delta
        m_out[...] = mm
        v_out[...] = vv

    grid_spec = pltpu.PrefetchScalarGridSpec(
        num_scalar_prefetch=1, grid=(R // tr,),
        in_specs=specs + [blk, blk, blk], out_specs=[blk] * 4)
    if sel is None:
        sel = jnp.zeros((1,), jnp.int32)
    return pl.pallas_call(
        body, name=name, grid_spec=grid_spec,
        out_shape=[jax.ShapeDtypeStruct((R, C), F32)] * 4,
        compiler_params=_cp(("parallel",)),
    )(sel, *args, w, m, v)


def _rows(a, c):
    return a.reshape(-1, c)


def _pad_rows(a, r):
    return jnp.pad(a, ((0, r - a.shape[0]), (0, 0))) if a.shape[0] != r else a


def _gate_tables():
    hp = N_HEADS // 2
    sel_q = np.zeros((hp, 3 * LANES, 2 * LANES), np.float32)
    sel_k = np.zeros((hp, 3 * LANES, 2 * LANES), np.float32)
    const_q = np.zeros((hp, 1, 2 * LANES), np.float32)
    const_k = np.zeros((hp, 1, 2 * LANES), np.float32)
    for p in range(hp):
        for hh in range(2):
            h = 2 * p + hh
            base = hh * LANES + HEAD_DIM
            for piece in range(3):
                sel_q[p, piece * LANES + h, base + piece] = 1.0
                sel_k[p, piece * LANES + h, base + 3 + piece] = -1.0
            const_k[p, 0, base:base + 3] = 1.0
            const_q[p, 0, base + 3:base + 6] = 1.0
    as_bf = lambda t: jnp.asarray(t, BF16)
    return as_bf(sel_q), as_bf(sel_k), jnp.asarray(const_q), jnp.asarray(const_k)


def _pad_heads(w):
    d = w.shape[0]
    w3 = w.reshape(d, N_HEADS, HEAD_DIM)
    return jnp.pad(w3, ((0, 0), (0, 0), (0, LANES - HEAD_DIM))).reshape(d, N_HEADS * LANES)


def kernel(x, mix_norm_g, ffn_norm_g, gm_w_in, gm_ln_g, gm_ln_b, gm_w_s, gm_b_s, gm_w_out, fox_w_qkvf, fox_b_f, fox_w_o, ffn_w_gate, ffn_w_up, ffn_conv_w, ffn_conv_b, ffn_w_down, final_norm_g, loss_target, m_mix_norm_g, m_ffn_norm_g, m_gm_w_in, m_gm_ln_g, m_gm_ln_b, m_gm_w_s, m_gm_b_s, m_gm_w_out, m_fox_w_qkvf, m_fox_b_f, m_fox_w_o, m_ffn_w_gate, m_ffn_w_up, m_ffn_conv_w, m_ffn_conv_b, m_ffn_w_down, m_final_norm_g, v_mix_norm_g, v_ffn_norm_g, v_gm_w_in, v_gm_ln_g, v_gm_ln_b, v_gm_w_s, v_gm_b_s, v_gm_w_out, v_fox_w_qkvf, v_fox_b_f, v_fox_w_o, v_ffn_w_gate, v_ffn_w_up, v_ffn_conv_w, v_ffn_conv_b, v_ffn_w_down, v_final_norm_g):
    T, D = x.shape[1], x.shape[2]
    E = gm_ln_g.shape[1]
    FF = ffn_conv_b.shape[1]
    NQKVF = 3 * D + N_HEADS
    xi, yi, ci = _mesh_pos()
    me = 4 * xi + 2 * yi + ci
    h0 = x.reshape(T, D)
    tgt = loss_target.reshape(T, D)

    nl = ffn_w_gate.shape[0]
    to16 = lambda a: a.astype(BF16)
    n_cw_rows = ffn_conv_w.size // LANES
    cw_rows = _pad_rows(_rows(ffn_conv_w.astype(F32), LANES), 16)
    w_in_g, w_out_g8, cwg = _all_gather([to16(gm_w_in[0]), to16(gm_w_out[0]), cw_rows], "ag_weights")
    w_out_g = w_out_g8.reshape(E, D)
    later, land_of, land_off, lands = [], [], [], []
    for l in range(nl):
        later += [to16(ffn_w_gate[l]), to16(ffn_w_up[l]), to16(ffn_w_down[l])]
        land_of += [2 * l, 2 * l, 2 * l + 1]
        land_off += [0, N_DEV, 0]
        lands += [lax.empty((2 * N_DEV, D, FF // N_DEV), BF16), lax.empty((N_DEV, FF // N_DEV, D), BF16)]
    later += [to16(fox_w_qkvf[0]), to16(fox_w_o[0])]
    land_of += [2 * nl, 2 * nl + 1]
    land_off += [0, 0]
    lands += [lax.empty((N_DEV, D, NQKVF // N_DEV), BF16), lax.empty((N_DEV, D // N_DEV, D), BF16)]
    ag_dst = lambda a, k: land_off[a] + _slot(_mesh_pos())
    ag_src = lambda a, p: None
    ag_send, ag_recv, later, lands, ag_token = _direct_start(later, lands, land_of, ag_dst, ag_src, w_in_g,
                                                             "ag_later_start", collective_id=1)

    def own_blocks(landed, shards, offs):
        for s, o in zip(shards, offs):
            landed = lax.dynamic_update_index_in_dim(landed, s, o + me, 0)
        return landed

    def gather_wait(idxs, after, name):
        return _direct_wait(ag_send, ag_recv, later, lands, land_of, idxs, ag_dst, ag_src, after, name)

    conv_w_full = jnp.transpose(cwg[:, :n_cw_rows].reshape(N_DEV, nl, 3, FF // N_DEV), (1, 2, 0, 3)).reshape(nl, 3, FF)

    ffn_w = {}

    def ffn_weights(l):
        return ffn_w[l]

    def land_ffn(l, shards, gu_land, dn_land):
        ffn_w[l] = (own_blocks(gu_land, shards[:2], [0, N_DEV]), own_blocks(dn_land, shards[2:3], [0]).reshape(FF, D))

    saved = {}

    def ffn_fwd(l, h_in, hn, next_g):
        wgul, wdl = ffn_weights(l)
        au, act = _ffn_up_fused(hn, wgul, conv_w_full[l], ffn_conv_b[l:l + 1], f"ffn{l}_up")
        saved[f"ffn{l}"] = (h_in, hn, au, act)
        if next_g is None:
            return _mm_nn(act, wdl, f"ffn{l}_down", res=h_in), None
        return _mm_nn(act, wdl, f"ffn{l}_down", res=h_in, norm_g=next_g)

    bs_col = gm_b_s[0].reshape(GM_GROUPS, CHUNK, 1)
    hn0 = _rms_fwd(h0, mix_norm_g[0:1], "mix0_norm", after=ag_token)
    z, gu = _gm_in_fused(hn0, w_in_g, gm_ln_g, gm_ln_b, gm_w_s[0], bs_col, "gm_in")
    h1, hn_f0 = _mm_nn(gu, w_out_g, "gm_out", res=h0, norm_g=ffn_norm_g[0:1])
    mine0, land0 = gather_wait([0, 1, 2], h1, "ag_ffn0_wait")
    land_ffn(0, mine0, *land0)
    h2, hn2 = ffn_fwd(0, h1, hn_f0, mix_norm_g[1:2])

    mine1, rest = gather_wait(list(range(3, 3 * nl + 2)), h2, "ag_layer1_wait")
    for l in range(1, nl):
        land_ffn(l, mine1[3 * (l - 1):3 * l], rest[2 * (l - 1)], rest[2 * (l - 1) + 1])
    w_qkvf = jnp.transpose(own_blocks(rest[-2], mine1[-2:-1], [0]), (1, 0, 2)).reshape(D, NQKVF)
    w_o_g = own_blocks(rest[-1], mine1[-1:], [0]).reshape(D, D)
    w_q, w_k, w_v = w_qkvf[:, :D], w_qkvf[:, D:2 * D], w_qkvf[:, 2 * D:3 * D]
    w_f = jnp.pad(w_qkvf[:, 3 * D:], ((0, 0), (0, LANES - N_HEADS)))
    bf_row = jnp.pad(fox_b_f, ((0, 0), (0, LANES - N_HEADS)))
    sel_q, sel_k, const_q, const_k = _gate_tables()
    scale = HEAD_DIM ** -0.5
    f_logit = _mm_nn(hn2, w_f, "fox_f")
    cp, sneg = _gate_scan(f_logit, bf_row, "fox_scan")
    qp, kp = _qk_proj(hn2, (_pad_heads(w_q), _pad_heads(w_k)), cp, (sel_q, sel_k), (const_q, const_k),
                      (scale * LOG2E, 1.0), "fox_qk")
    vv = _mm_nn(hn2, w_v, "fox_v", out_dtype=BF16)
    o, o32, lse = _attn_fwd_t(qp, kp, jnp.transpose(vv), "fox_attn")
    h3, hn_f1 = _mm_nn(o, w_o_g, "fox_o", res=h2, norm_g=ffn_norm_g[1:2])
    h4, _ = ffn_fwd(1, h3, hn_f1, None)

    dh, dh16, d_final, loss_row = _loss_head(h4, tgt, final_norm_g.reshape(1, D), "loss_head")
    loss = lax.psum(loss_row[0, 0], ("x", "y", "c"))

    rs_dst = lambda a, k: k
    rs_src = lambda a, p: _slot(p)
    me_idx = me.astype(jnp.int32).reshape(1)

    def rs_start(grads, name, cid):
        lands = [lax.empty((N_DEV - 1,) + g.shape[1:], BF16) for g in grads]
        return _direct_start(grads, lands, list(range(len(grads))), rs_dst, rs_src, loss_row, name, collective_id=cid)

    def rs_wait(st, after, name):
        n = len(st[2])
        return _direct_wait(st[0], st[1], st[2], st[3], list(range(n)), list(range(n)), rs_dst, rs_src, after, name)

    def ffn_bwd(l, dh, dh16, after=None):
        wgul, wdl = ffn_weights(l)
        h_in, hn, au, act = saved[f"ffn{l}"]
        dact = _mm_nt([dh16], wdl, f"ffn{l}_dact", out_dtype=BF16, after=after)
        d_wd = _mm_tn(act, dh16, f"ffn{l}_dwd", out_dtype=BF16)
        da, dup, d_cw, d_cb = _ffn_mid_bwd(au, dact, conv_w_full[l], ffn_conv_b[l:l + 1], f"ffn{l}_dmid")
        dh_in, dh_in16, d_norm = _mm_nt([da, dup], wgul, f"ffn{l}_dhn", norm_bwd=(h_in, ffn_norm_g[l:l + 1], dh))
        d_wg = _mm_tn(hn, da, f"ffn{l}_dwg", blocked_w=FF // N_DEV, out_dtype=BF16)
        d_wu = _mm_tn(hn, dup, f"ffn{l}_dwu", blocked_w=FF // N_DEV, out_dtype=BF16)
        big_g = [d_wg, d_wu, d_wd.reshape(N_DEV, FF // N_DEV, D)]
        return dh_in, dh_in16, big_g, dict(cw=d_cw, cb=d_cb, norm=d_norm)

    dh, dh16, big_ffn1, g_ffn1 = ffn_bwd(1, dh, dh16)

    do = _mm_nt([dh16], w_o_g, "fox_do", out_dtype=BF16)
    d_wo = _mm_tn(o, dh16, "fox_dwo", out_dtype=BF16)
    dq, dk, dv, dqe, dke = _attn_bwd(qp, kp, vv, o32, do, lse, scale, "fox_dattn")
    gate_lane = lambda e, r: jnp.pad(jnp.transpose(e[:, r::8, :].reshape(N_HEADS, T)), ((0, 0), (0, LANES - N_HEADS)))
    df, d_bf = _gate_scan_bwd(gate_lane(dqe, 0), gate_lane(dke, 3), sneg, "fox_dscan")
    dhn = _mm_nt([df], w_f, "fox_dhn_f")
    dh_mix1 = _mm_nt([dq, dk, dv], w_qkvf[:, :3 * D], "fox_dhn_qkv", add=dhn, norm_bwd=(h2, mix_norm_g[1:2], dh))
    d_wq = _mm_tn(hn2, dq, "fox_dwq", out_dtype=BF16)
    d_wk = _mm_tn(hn2, dk, "fox_dwk", out_dtype=BF16)
    d_wv = _mm_tn(hn2, dv, "fox_dwv", out_dtype=BF16)
    d_wf = _mm_tn(hn2, df, "fox_dwf", out_dtype=BF16)
    d_wqkvf = jnp.concatenate([d_wq, d_wk, d_wv, d_wf[:, :N_HEADS]], axis=1)
    dh, dh16, d_mix1 = dh_mix1
    st1 = rs_start([jnp.transpose(d_wqkvf.reshape(D, N_DEV, NQKVF // N_DEV), (1, 0, 2)),
                    d_wo.reshape(N_DEV, D // N_DEV, D)] + big_ffn1, "rs1_start", 2)

    dh, dh16, big_ffn0, g_ffn0 = ffn_bwd(0, dh, dh16, after=st1[4])
    st2 = rs_start(big_ffn0, "rs2_start", 3)

    dgu = _mm_nt([dh16], w_out_g, "gm_dgu", out_dtype=BF16, after=st2[4])
    d_wout = _mm_tn(gu, dh16, "gm_dwout", out_dtype=BF16)
    dz, d_lng, d_lnb, d_ws, d_bs = _sgu_bwd(z, dgu, gm_ln_g, gm_ln_b, gm_w_s[0], bs_col, "gm_dsgu")
    d_win = _mm_tn(hn0, dz, "gm_dwin", blocked_w=2 * E // N_DEV, out_dtype=BF16)
    st3 = rs_start([d_win, d_wout.reshape(N_DEV, E // N_DEV, D)], "rs3_start", 4)
    dx, _, d_mix0 = _mm_nt([dz], w_in_g, "gm_dhn", after=st3[4], norm_bwd=(h0, mix_norm_g[0:1], dh))

    own1, land1 = rs_wait(st1, dx, "rs1_wait")
    own2, land2 = rs_wait(st2, land1[0], "rs2_wait")
    cat1 = lambda a, b: jnp.concatenate([a, b], axis=1)
    big_out = {}

    def big_adamw(name, w, m, v, own, landed):
        shard2d = lambda a, c=own.shape[2]: a.reshape(-1, c)
        res = _sum_adamw([(own, [-1]), (landed, list(range(N_DEV - 1)))], shard2d(w), shard2d(m), shard2d(v),
                         f"adamw_{name}", sel=me_idx)
        big_out[name] = [t.reshape(w.shape) for t in res]

    big_adamw("fox_w_qkvf", fox_w_qkvf, m_fox_w_qkvf, v_fox_w_qkvf, own1[0], land1[0])
    big_adamw("fox_w_o", fox_w_o, m_fox_w_o, v_fox_w_o, own1[1], land1[1])
    big_adamw("ffn_w_gate", ffn_w_gate, m_ffn_w_gate, v_ffn_w_gate, cat1(own2[0], own1[2]), cat1(land2[0], land1[2]))
    big_adamw("ffn_w_up", ffn_w_up, m_ffn_w_up, v_ffn_w_up, cat1(own2[1], own1[3]), cat1(land2[1], land1[3]))
    big_adamw("ffn_w_down", ffn_w_down, m_ffn_w_down, v_ffn_w_down, cat1(own2[2], own1[4]), cat1(land2[2], land1[4]))

    small = [("mix_norm_g", mix_norm_g, m_mix_norm_g, v_mix_norm_g, jnp.concatenate([d_mix0, d_mix1], axis=0)),
             ("ffn_norm_g", ffn_norm_g, m_ffn_norm_g, v_ffn_norm_g, jnp.concatenate([g_ffn0["norm"], g_ffn1["norm"]], axis=0)),
             ("gm_ln_g", gm_ln_g, m_gm_ln_g, v_gm_ln_g, d_lng),
             ("gm_ln_b", gm_ln_b, m_gm_ln_b, v_gm_ln_b, d_lnb),
             ("gm_w_s", gm_w_s, m_gm_w_s, v_gm_w_s, d_ws),
             ("gm_b_s", gm_b_s, m_gm_b_s, v_gm_b_s, d_bs),
             ("fox_b_f", fox_b_f, m_fox_b_f, v_fox_b_f, d_bf[:, :N_HEADS]),
             ("ffn_conv_b", ffn_conv_b, m_ffn_conv_b, v_ffn_conv_b, jnp.concatenate([g_ffn0["cb"], g_ffn1["cb"]], axis=0)),
             ("final_norm_g", final_norm_g, m_final_norm_g, v_final_norm_g, d_final)]
    d_cw_full = jnp.stack([g_ffn0["cw"], g_ffn1["cw"]], axis=0)

    def small_rows(a):
        flat = a.astype(F32).reshape(-1)
        n = -(-flat.size // (8 * LANES)) * (8 * LANES)
        return jnp.pad(flat, (0, n - flat.size)).reshape(-1, LANES)

    s_rows = [small_rows(p[1]).shape[0] for p in small]
    s_off = np.concatenate([[0], np.cumsum(s_rows)]).tolist()
    cw_g_rows = small_rows(d_cw_full)
    zeros_cw = jnp.zeros_like(cw_g_rows)
    cat = lambda k: jnp.concatenate([small_rows(p[k]) for p in small] + [zeros_cw], axis=0)
    g_small = jnp.concatenate([small_rows(p[4]) for p in small] + [cw_g_rows], axis=0)
    (gs_all,) = _all_gather([g_small], "ag_small_grads")
    small_out = _sum_adamw([(gs_all, list(range(N_DEV)))], cat(1), cat(2), cat(3), "adamw_small")
    gs = small_out[0]

    g_cw_full = gs[s_off[-1]:].reshape(-1)[:d_cw_full.size].reshape(d_cw_full.shape)
    g_cw = lax.dynamic_slice_in_dim(g_cw_full, me * (FF // N_DEV), FF // N_DEV, axis=2)
    cw2 = lambda a: _pad_rows(_rows(a.astype(F32), LANES), 16)
    cw_out = _sum_adamw([(cw2(g_cw)[None], [0])], cw2(ffn_conv_w), cw2(m_ffn_conv_w), cw2(v_ffn_conv_w), "adamw_conv_w")

    own3, land3 = rs_wait(st3, cw_out[0], "rs3_wait")
    big_adamw("gm_w_in", gm_w_in, m_gm_w_in, v_gm_w_in, own3[0], land3[0])
    big_adamw("gm_w_out", gm_w_out, m_gm_w_out, v_gm_w_out, own3[1], land3[1])

    names = ["mix_norm_g", "ffn_norm_g", "gm_w_in", "gm_ln_g", "gm_ln_b", "gm_w_s", "gm_b_s", "gm_w_out", "fox_w_qkvf",
             "fox_b_f", "fox_w_o", "ffn_w_gate", "ffn_w_up", "ffn_conv_w", "ffn_conv_b", "ffn_w_down", "final_norm_g"]
    small_idx = {p[0]: k for k, p in enumerate(small)}

    def pick(kind, name):
        if name in big_out:
            return big_out[name][kind]
        if name == "ffn_conv_w":
            return cw_out[kind][:n_cw_rows].reshape(ffn_conv_w.shape)
        k = small_idx[name]
        shp = small[k][1].shape
        return small_out[kind][s_off[k]:s_off[k + 1]].reshape(-1)[:int(np.prod(shp))].reshape(shp)

    outs = [loss, dx.reshape(x.shape)]
    for kind in range(4):
        outs += [pick(kind, n) for n in names]
    return tuple(outs)
```

```python
import math

import numpy as np
import jax
import jax.numpy as jnp
from jax import lax
from jax.experimental import pallas as pl
from jax.experimental.pallas import tpu as pltpu

F32 = jnp.float32
BF16 = jnp.bfloat16
MESH = pl.DeviceIdType.MESH

N_HEADS = 16
HEAD_DIM = 64
CHUNK = 128
GM_GROUPS = 8
RMS_EPS = 1e-6
LN_EPS = 1e-5
ADAM_LR = 0.001
ADAM_B1 = 0.9
ADAM_B2 = 0.999
ADAM_EPS = 1e-08
ADAM_WD = 0.01
ADAM_STEP = 10
N_DEV = 8

LANES = 128
VMEM_BYTES_V7X = 64 * 1024 * 1024
VMEM_LIMIT = 56 * 1024 * 1024

TM = 512
TM_MM = 1024
TT = 1024
TQ = 512
TF = 512
KV_UNROLL_BWD = 2
KV_UNROLL = 2
MM_BLOCK_BYTES = 8 * 1024 * 1024
NEG = -1e30
LOG2E = math.log2(math.e)
LN2 = math.log(2.0)


def _cp(sem=None, vmem=VMEM_LIMIT):
    return pltpu.CompilerParams(dimension_semantics=sem, vmem_limit_bytes=vmem)


def _gelu(x):
    c = math.sqrt(2.0 / math.pi)
    return x * (0.5 * (1.0 + jnp.tanh(c * (x + 0.044715 * (x * x * x)))))


def _gelu_grad(x):
    c = math.sqrt(2.0 / math.pi)
    t = jnp.tanh(c * (x + 0.044715 * (x * x * x)))
    return 0.5 * (1.0 + t) + x * (0.5 * (1.0 - t * t)) * (c * (1.0 + 3.0 * 0.044715 * (x * x)))


def _sigmoid(x):
    return 1.0 / (1.0 + jnp.exp(-x))


def _dot_nt(a, b):
    return lax.dot_general(a, b, (((1,), (1,)), ((), ())), preferred_element_type=F32)


def _dot_tn(a, b):
    return lax.dot_general(a, b, (((0,), (0,)), ((), ())), preferred_element_type=F32)


def _rms_fwd(h, g, name, after=None):
    T, D = h.shape
    tm = min(TM, T)

    def body(h_ref, g_ref, *rest):
        o_ref = rest[-1]
        x = h_ref[...]
        r = lax.rsqrt(jnp.mean(x * x, axis=-1, keepdims=True) + RMS_EPS)
        o_ref[...] = ((x * r) * g_ref[...]).astype(BF16)

    in_specs = [pl.BlockSpec((tm, D), lambda i: (i, 0)), pl.BlockSpec((1, D), lambda i: (0, 0))]
    args = [h, g]
    if after is not None:
        in_specs.append(pl.BlockSpec(memory_space=pl.ANY))
        args.append(after)
    return pl.pallas_call(
        body, name=name, grid=(T // tm,),
        in_specs=in_specs,
        out_specs=pl.BlockSpec((tm, D), lambda i: (i, 0)),
        out_shape=jax.ShapeDtypeStruct((T, D), BF16),
        compiler_params=_cp(("parallel",)),
    )(*args)


def _loss_head(h, tgt, g, name):
    T, D = h.shape
    tm = min(TM, T)

    def body(h_ref, t_ref, g_ref, o_ref, ob_ref, dg_ref, l_ref):
        x = h_ref[...]
        gg = g_ref[...]
        r = lax.rsqrt(jnp.mean(x * x, axis=-1, keepdims=True) + RMS_EPS)
        xr = x * r
        e = xr * gg - t_ref[...]
        lpart = 0.5 * jnp.sum(jnp.mean(e * e, axis=-1, keepdims=True), axis=0, keepdims=True)
        dy = e * (1.0 / D)
        dyg = dy * gg
        dot = jnp.mean(dyg * x, axis=-1, keepdims=True)
        dh = r * dyg - x * ((r * r * r) * dot)
        o_ref[...] = dh
        ob_ref[...] = dh.astype(BF16)
        part = jnp.sum(dy * xr, axis=0, keepdims=True)
        lrow = jnp.broadcast_to(lpart, (1, LANES))

        @pl.when(pl.program_id(0) == 0)
        def _():
            dg_ref[...] = part
            l_ref[...] = lrow

        @pl.when(pl.program_id(0) != 0)
        def _():
            dg_ref[...] += part
            l_ref[...] += lrow

    blk = pl.BlockSpec((tm, D), lambda i: (i, 0))
    row = pl.BlockSpec((1, D), lambda i: (0, 0))
    return pl.pallas_call(
        body, name=name, grid=(T // tm,),
        in_specs=[blk, blk, row],
        out_specs=[blk, blk, row, pl.BlockSpec((1, LANES), lambda i: (0, 0))],
        out_shape=[jax.ShapeDtypeStruct((T, D), F32), jax.ShapeDtypeStruct((T, D), BF16),
                   jax.ShapeDtypeStruct((1, D), F32), jax.ShapeDtypeStruct((1, LANES), F32)],
        compiler_params=_cp(("arbitrary",)),
    )(h, tgt, g)


def _mm_nn(a, b, name, out_dtype=F32, res=None, norm_g=None):
    M, K = a.shape
    b3 = b if b.ndim == 3 else b[None]
    nb, _, w = b3.shape
    N = nb * w
    tm = min(TM_MM, M, max(256, MM_BLOCK_BYTES // (4 * N)))
    o_spec = pl.BlockSpec((tm, N), lambda i: (i, 0))
    in_specs = [pl.BlockSpec((tm, K), lambda i: (i, 0)), pl.BlockSpec((nb, K, w), lambda i: (0, 0, 0))]
    args = [a, b3]
    if res is not None:
        in_specs.append(o_spec)
        args.append(res)
    if norm_g is not None:
        in_specs.append(pl.BlockSpec((1, N), lambda i: (0, 0)))
        args.append(norm_g)
    n_out = 2 if norm_g is not None else 1

    def body(*refs):
        a_ref, b_ref = refs[0], refs[1]
        o_ref = refs[-n_out]
        av = a_ref[...]
        for j in range(nb):
            cols = slice(j * w, (j + 1) * w)
            acc = jnp.dot(av, b_ref[j], preferred_element_type=F32)
            if res is not None:
                acc = refs[2][:, cols] + acc
            o_ref[:, cols] = acc.astype(out_dtype)
        if norm_g is not None:
            x = o_ref[...]
            r = lax.rsqrt(jnp.mean(x * x, axis=-1, keepdims=True) + RMS_EPS)
            refs[-1][...] = ((x * r) * refs[3][...]).astype(BF16)

    out_shape = jax.ShapeDtypeStruct((M, N), out_dtype)
    if norm_g is None:
        out_specs, out_shapes = o_spec, out_shape
    else:
        out_specs, out_shapes = [o_spec, o_spec], [out_shape, jax.ShapeDtypeStruct((M, N), BF16)]
    return pl.pallas_call(
        body, name=name, grid=(M // tm,),
        in_specs=in_specs, out_specs=out_specs, out_shape=out_shapes,
        compiler_params=_cp(("parallel",)),
    )(*args)


def _mm_nt(a_list, b, name, out_dtype=F32, add=None, after=None, norm_bwd=None):
    M, kw = a_list[0].shape
    tm = min(TM, M)
    na = len(a_list)
    blocked = b.ndim == 3
    N = b.shape[1] if blocked else b.shape[0]
    b_spec = pl.BlockSpec(b.shape, lambda i: (0,) * b.ndim)
    o_spec = pl.BlockSpec((tm, N), lambda i: (i, 0))
    row_spec = pl.BlockSpec((1, N), lambda i: (0, 0))
    in_specs = [pl.BlockSpec((tm, kw), lambda i: (i, 0)) for _ in a_list] + [b_spec]
    args = list(a_list) + [b]
    if add is not None:
        in_specs.append(o_spec)
        args.append(add)
    n_in = len(args)
    if norm_bwd is not None:
        in_specs += [o_spec, row_spec, o_spec]
        args += list(norm_bwd)
    if after is not None:
        in_specs.append(pl.BlockSpec(memory_space=pl.ANY))
        args.append(after)
    n_args = len(args)

    def body(*refs):
        a_refs = refs[:na]
        b_ref = refs[na]
        acc = refs[na + 1][...] if add is not None else None
        for s, a_ref in enumerate(a_refs):
            if blocked:
                w = b_ref.shape[2]
                per = kw // w
                parts = [_dot_nt(a_ref[:, jj * w:(jj + 1) * w], b_ref[s * per + jj]) for jj in range(per)]
            else:
                parts = [_dot_nt(a_ref[...], b_ref[:, s * kw:(s + 1) * kw])]
            for part in parts:
                acc = part if acc is None else acc + part
        if norm_bwd is None:
            refs[n_args][...] = acc.astype(out_dtype)
            return
        h_ref, g_ref, r_ref = refs[n_in:n_in + 3]
        o_ref, ob_ref, dg_ref = refs[n_args:n_args + 3]
        x = h_ref[...]
        r = lax.rsqrt(jnp.mean(x * x, axis=-1, keepdims=True) + RMS_EPS)
        dyg = acc * g_ref[...]
        dot = jnp.mean(dyg * x, axis=-1, keepdims=True)
        dh = r_ref[...] + (r * dyg - x * ((r * r * r) * dot))
        o_ref[...] = dh
        ob_ref[...] = dh.astype(BF16)
        part_g = jnp.sum(acc * (x * r), axis=0, keepdims=True)

        @pl.when(pl.program_id(0) == 0)
        def _():
            dg_ref[...] = part_g

        @pl.when(pl.program_id(0) != 0)
        def _():
            dg_ref[...] += part_g

    if norm_bwd is None:
        out_specs, out_shapes, sem = o_spec, jax.ShapeDtypeStruct((M, N), out_dtype), ("parallel",)
    else:
        out_specs = [o_spec, o_spec, row_spec]
        out_shapes = [jax.ShapeDtypeStruct((M, N), F32), jax.ShapeDtypeStruct((M, N), BF16),
                      jax.ShapeDtypeStruct((1, N), F32)]
        sem = ("arbitrary",)
    return pl.pallas_call(
        body, name=name, grid=(M // tm,),
        in_specs=in_specs, out_specs=out_specs, out_shape=out_shapes,
        compiler_params=_cp(sem),
    )(*args)


def _mm_tn(x, y, name, blocked_w=None, out_dtype=F32):
    T, Kx = x.shape
    N = y.shape[1]
    tt = min(TT, T)
    nt = T // tt
    tkx = min(Kx, max(LANES, MM_BLOCK_BYTES // (4 * N)))
    if blocked_w is not None:
        blk_shape, full_shape = (N // blocked_w, tkx, blocked_w), (N // blocked_w, Kx, blocked_w)
        o_spec = pl.BlockSpec(blk_shape, lambda i, t: (0, i, 0))
    else:
        blk_shape, full_shape = (tkx, N), (Kx, N)
        o_spec = pl.BlockSpec(blk_shape, lambda i, t: (i, 0))

    def body(x_ref, y_ref, o_ref, acc_ref):
        part = _dot_tn(x_ref[...], y_ref[...])
        t = pl.program_id(1)
        if blocked_w is None:
            pieces = [(slice(None), part)]
        else:
            pieces = [(j, part[:, j * blocked_w:(j + 1) * blocked_w]) for j in range(N // blocked_w)]

        @pl.when(t == 0)
        def _():
            for idx, pj in pieces:
                acc_ref[idx] = pj

        @pl.when(t != 0)
        def _():
            for idx, pj in pieces:
                acc_ref[idx] += pj

        @pl.when(t == nt - 1)
        def _():
            o_ref[...] = acc_ref[...].astype(out_dtype)

    return pl.pallas_call(
        body, name=name, grid=(Kx // tkx, nt),
        in_specs=[pl.BlockSpec((tt, tkx), lambda i, t: (t, i)),
                  pl.BlockSpec((tt, N), lambda i, t: (t, 0))],
        out_specs=o_spec, out_shape=jax.ShapeDtypeStruct(full_shape, out_dtype),
        scratch_shapes=[pltpu.VMEM(blk_shape, F32)],
        compiler_params=_cp(("parallel", "arbitrary")),
    )(x, y)


def _sgu_pieces(z, lng, lnb, wc, bs_ref):
    E = z.shape[1] // 2
    gd = E // GM_GROUPS
    zu, zv = z[:, :E], z[:, E:]
    u = _gelu(zu)
    v = _gelu(zv)
    mu = jnp.mean(v, axis=-1, keepdims=True)
    xc = v - mu
    rs = lax.rsqrt(jnp.mean(xc * xc, axis=-1, keepdims=True) + LN_EPS)
    xhat = xc * rs
    vln = xhat * lng + lnb
    s = []
    for g in range(GM_GROUPS):
        vg = vln[:, g * gd:(g + 1) * gd].astype(BF16)
        s.append(jnp.dot(wc[g], vg, preferred_element_type=F32) + bs_ref[g])
    return zu, zv, u, xhat, rs, vln, s


def _causal_ws(ws_ref):
    t = lax.broadcasted_iota(jnp.int32, (CHUNK, CHUNK), 0)
    s = lax.broadcasted_iota(jnp.int32, (CHUNK, CHUNK), 1)
    tri = t >= s
    return [jnp.where(tri, ws_ref[g], 0.0).astype(BF16) for g in range(GM_GROUPS)], tri


def _gm_in_fused(hn, w_in, lng, lnb, ws, bs, name):
    T, D = hn.shape
    nb, _, w = w_in.shape
    E2 = nb * w
    E = E2 // 2
    gd = E // GM_GROUPS
    tm = min(TM, T)

    def body(a_ref, w_ref, lng_ref, lnb_ref, ws_ref, bs_ref, z_ref, o_ref):
        av = a_ref[...]
        for j in range(nb):
            z_ref[:, j * w:(j + 1) * w] = jnp.dot(av, w_ref[j], preferred_element_type=F32)
        wc, _ = _causal_ws(ws_ref)
        for c in range(tm // CHUNK):
            rows = slice(c * CHUNK, (c + 1) * CHUNK)
            _, _, u, _, _, _, s = _sgu_pieces(z_ref[rows, :], lng_ref[...], lnb_ref[...], wc, bs_ref)
            for g in range(GM_GROUPS):
                cols = slice(g * gd, (g + 1) * gd)
                o_ref[rows, cols] = (u[:, cols] * s[g]).astype(BF16)

    full = lambda shape: pl.BlockSpec(shape, lambda i: (0,) * len(shape))
    return pl.pallas_call(
        body, name=name, grid=(T // tm,),
        in_specs=[pl.BlockSpec((tm, D), lambda i: (i, 0)), full((nb, D, w)), full((1, E)), full((1, E)),
                  full((GM_GROUPS, CHUNK, CHUNK)), full((GM_GROUPS, CHUNK, 1))],
        out_specs=[pl.BlockSpec((tm, E2), lambda i: (i, 0)), pl.BlockSpec((tm, E), lambda i: (i, 0))],
        out_shape=[jax.ShapeDtypeStruct((T, E2), F32), jax.ShapeDtypeStruct((T, E), BF16)],
        compiler_params=_cp(("parallel",)),
    )(hn, w_in, lng, lnb, ws, bs)


def _sgu_bwd(z, dh16, w_out, lng, lnb, ws, bs, name, after=None):
    T, E2 = z.shape
    D = dh16.shape[1]
    E = E2 // 2
    gd = E // GM_GROUPS
    tm = min(2 * CHUNK, T)
    nsteps = T // tm

    def body(z_ref, dh_ref, wo_ref, lng_ref, lnb_ref, ws_ref, bs_ref, *rest):
        dz_ref, dlng_ref, dlnb_ref, dws_ref, dbs_ref, dg_ref = rest[-6:]
        i = pl.program_id(0)

        @pl.when(i == 0)
        def _():
            dlng_ref[...] = jnp.zeros_like(dlng_ref)
            dlnb_ref[...] = jnp.zeros_like(dlnb_ref)
            dws_ref[...] = jnp.zeros_like(dws_ref)
            dbs_ref[...] = jnp.zeros_like(dbs_ref)

        dg_ref[...] = _dot_nt(dh_ref[...], wo_ref[...]).astype(BF16)
        wc, tri = _causal_ws(ws_ref)
        lng_v = lng_ref[...]
        for c in range(tm // CHUNK):
            rows = slice(c * CHUNK, (c + 1) * CHUNK)
            zu, zv, u, xhat, rs, vln, s = _sgu_pieces(z_ref[rows, :], lng_v, lnb_ref[...], wc, bs_ref)
            dgc = dg_ref[rows, :].astype(F32)
            du, dvln = [], []
            for g in range(GM_GROUPS):
                cols = slice(g * gd, (g + 1) * gd)
                dgg = dgc[:, cols]
                du.append(dgg * s[g])
                ds = dgg * u[:, cols]
                dsb = ds.astype(BF16)
                dws_ref[g] += _dot_nt(dsb, vln[:, cols].astype(BF16))
                dbs_ref[g] += jnp.sum(ds, axis=-1, keepdims=True)
                dvln.append(_dot_tn(wc[g], dsb))
            du = jnp.concatenate(du, axis=1)
            dvln = jnp.concatenate(dvln, axis=1)
            dlng_ref[...] += jnp.sum(dvln * xhat, axis=0, keepdims=True)
            dlnb_ref[...] += jnp.sum(dvln, axis=0, keepdims=True)
            dxh = dvln * lng_v
            m1 = jnp.mean(dxh, axis=-1, keepdims=True)
            m2 = jnp.mean(dxh * xhat, axis=-1, keepdims=True)
            dv = rs * (dxh - m1 - xhat * m2)
            dz_ref[rows, :E] = (du * _gelu_grad(zu)).astype(BF16)
            dz_ref[rows, E:] = (dv * _gelu_grad(zv)).astype(BF16)

        @pl.when(i == nsteps - 1)
        def _():
            for g in range(GM_GROUPS):
                dws_ref[g] = jnp.where(tri, dws_ref[g], 0.0)

    full = lambda shape: pl.BlockSpec(shape, lambda i: (0,) * len(shape))
    in_specs = [pl.BlockSpec((tm, E2), lambda i: (i, 0)), pl.BlockSpec((tm, D), lambda i: (i, 0)), full((E, D)),
                full((1, E)), full((1, E)), full((GM_GROUPS, CHUNK, CHUNK)), full((GM_GROUPS, CHUNK, 1))]
    args = [z, dh16, w_out, lng, lnb, ws, bs]
    if after is not None:
        in_specs.append(pl.BlockSpec(memory_space=pl.ANY))
        args.append(after)
    return pl.pallas_call(
        body, name=name, grid=(nsteps,),
        in_specs=in_specs,
        out_specs=[pl.BlockSpec((tm, E2), lambda i: (i, 0)), full((1, E)), full((1, E)),
                   full((GM_GROUPS, CHUNK, CHUNK)), full((GM_GROUPS, CHUNK, 1))],
        out_shape=[jax.ShapeDtypeStruct((T, E2), BF16), jax.ShapeDtypeStruct((1, E), F32),
                   jax.ShapeDtypeStruct((1, E), F32), jax.ShapeDtypeStruct((GM_GROUPS, CHUNK, CHUNK), F32),
                   jax.ShapeDtypeStruct((GM_GROUPS, CHUNK, 1), F32)],
        scratch_shapes=[pltpu.VMEM((tm, E), BF16)],
        compiler_params=_cp(("arbitrary",)),
    )(*args)


HALO = 16


def _conv_taps(a_ext, w_ref, b_ref):
    n = a_ext.shape[0]
    am1 = pltpu.roll(a_ext, 1, 0)
    am2 = pltpu.roll(a_ext, 2, 0)
    del n
    return ((b_ref[...] + am2 * w_ref[0:1, :]) + am1 * w_ref[1:2, :]) + a_ext * w_ref[2:3, :], am1, am2


def _ffn_up_fused(hn, wgu, cw, cb, name):
    T, D = hn.shape
    nb2, _, w = wgu.shape
    nb = nb2 // 2
    F = nb * w
    tm = min(TM, T)

    def body(a_ref, w_ref, cw_ref, cb_ref, au_ref, act_ref, halo_ref):
        @pl.when(pl.program_id(0) == 0)
        def _():
            halo_ref[...] = jnp.zeros_like(halo_ref)

        av = a_ref[...]
        for j in range(nb):
            cols = slice(j * w, (j + 1) * w)
            g = jnp.dot(av, w_ref[j], preferred_element_type=F32)
            u = jnp.dot(av, w_ref[nb + j], preferred_element_type=F32)
            au_ref[:, cols] = g
            au_ref[:, F + j * w:F + (j + 1) * w] = u
            ext = jnp.concatenate([halo_ref[:, cols], g], axis=0)
            am1 = pltpu.roll(ext, 1, 0)
            am2 = pltpu.roll(ext, 2, 0)
            conv = ((cb_ref[:, cols] + am2 * cw_ref[0:1, cols]) + am1 * cw_ref[1:2, cols]) + ext * cw_ref[2:3, cols]
            conv = conv[HALO:, :]
            act_ref[:, cols] = ((conv * _sigmoid(conv)) * u).astype(BF16)
            halo_ref[:, cols] = g[tm - HALO:, :]

    return pl.pallas_call(
        body, name=name, grid=(T // tm,),
        in_specs=[pl.BlockSpec((tm, D), lambda i: (i, 0)), pl.BlockSpec((nb2, D, w), lambda i: (0, 0, 0)),
                  pl.BlockSpec((3, F), lambda i: (0, 0)), pl.BlockSpec((1, F), lambda i: (0, 0))],
        out_specs=[pl.BlockSpec((tm, 2 * F), lambda i: (i, 0)), pl.BlockSpec((tm, F), lambda i: (i, 0))],
        out_shape=[jax.ShapeDtypeStruct((T, 2 * F), F32), jax.ShapeDtypeStruct((T, F), BF16)],
        scratch_shapes=[pltpu.VMEM((HALO, F), F32)],
        compiler_params=_cp(("arbitrary",)),
    )(hn, wgu, cw, cb)


def _ffn_mid_bwd(au, dh16, wd, cw, cb, name, after=None):
    T, F = au.shape[0], au.shape[1] // 2
    D = dh16.shape[1]
    tm, tf = min(TM, T), min(TF, F)
    hb = tm // HALO
    nt = T // tm
    nf = F // tf
    last_h = T // HALO - 1

    def body(a_ref, ap_ref, an_ref, u_ref, un_ref, dh_ref, dhn_ref, wd_ref, w_ref, b_ref, *rest):
        da_ref, du_ref, dcw_ref, dcb_ref = rest[-4:]
        i = pl.program_id(1)
        prev = jnp.where(i == 0, 0.0, ap_ref[...])
        a_main = a_ref[...]
        a_ext = jnp.concatenate([prev, a_main, an_ref[...]], axis=0)
        conv, am1, am2 = _conv_taps(a_ext, w_ref, b_ref)
        conv = conv[HALO:, :]
        sig = _sigmoid(conv)
        u_ext = jnp.concatenate([u_ref[...], un_ref[...]], axis=0)
        wd_f = wd_ref[...]
        d_ext = jnp.concatenate([_dot_nt(dh_ref[...], wd_f), _dot_nt(dhn_ref[...], wd_f)], axis=0)
        d_ext = d_ext.astype(BF16).astype(F32)
        n = tm + HALO
        row = lax.broadcasted_iota(jnp.int32, (n, 1), 0)
        live = jnp.logical_or(row < tm, i < nt - 1)
        dconv = jnp.where(live, d_ext * u_ext * (sig * (1.0 + conv * (1.0 - sig))), 0.0)
        du_ref[...] = (d_ext[:tm, :] * (conv[:tm, :] * sig[:tm, :])).astype(BF16)
        dp1 = pltpu.roll(dconv, n - 1, 0)[:tm, :]
        dp2 = pltpu.roll(dconv, n - 2, 0)[:tm, :]
        dc = dconv[:tm, :]
        da_ref[...] = ((dc * w_ref[2:3, :] + dp1 * w_ref[1:2, :]) + dp2 * w_ref[0:1, :]).astype(BF16)
        g2 = jnp.sum(dc * a_main, axis=0, keepdims=True)
        g1 = jnp.sum(dc * am1[HALO:HALO + tm, :], axis=0, keepdims=True)
        g0 = jnp.sum(dc * am2[HALO:HALO + tm, :], axis=0, keepdims=True)
        gb = jnp.sum(dc, axis=0, keepdims=True)

        @pl.when(i == 0)
        def _():
            dcw_ref[...] = jnp.zeros_like(dcw_ref)
            dcb_ref[...] = jnp.zeros_like(dcb_ref)

        dcw_ref[0:1, :] += g0
        dcw_ref[1:2, :] += g1
        dcw_ref[2:3, :] += g2
        dcb_ref[...] += gb

    main = pl.BlockSpec((tm, tf), lambda f, i: (i, f))
    prev = pl.BlockSpec((HALO, tf), lambda f, i: (jnp.maximum(i * hb - 1, 0), f))
    nxt = pl.BlockSpec((HALO, tf), lambda f, i: (jnp.minimum((i + 1) * hb, last_h), f))
    main_u = pl.BlockSpec((tm, tf), lambda f, i: (i, nf + f))
    nxt_u = pl.BlockSpec((HALO, tf), lambda f, i: (jnp.minimum((i + 1) * hb, last_h), nf + f))
    in_specs = [main, prev, nxt, main_u, nxt_u,
                pl.BlockSpec((tm, D), lambda f, i: (i, 0)),
                pl.BlockSpec((HALO, D), lambda f, i: (jnp.minimum((i + 1) * hb, last_h), 0)),
                pl.BlockSpec((tf, D), lambda f, i: (f, 0)),
                pl.BlockSpec((3, tf), lambda f, i: (0, f)), pl.BlockSpec((1, tf), lambda f, i: (0, f))]
    args = [au, au, au, au, au, dh16, dh16, wd, cw, cb]
    if after is not None:
        in_specs.append(pl.BlockSpec(memory_space=pl.ANY))
        args.append(after)
    return pl.pallas_call(
        body, name=name, grid=(nf, nt),
        in_specs=in_specs,
        out_specs=[main, main, pl.BlockSpec((3, tf), lambda f, i: (0, f)), pl.BlockSpec((1, tf), lambda f, i: (0, f))],
        out_shape=[jax.ShapeDtypeStruct((T, F), BF16), jax.ShapeDtypeStruct((T, F), BF16),
                   jax.ShapeDtypeStruct((3, F), F32), jax.ShapeDtypeStruct((1, F), F32)],
        compiler_params=_cp(("parallel", "arbitrary")),
    )(*args)


def _split3(x):
    hi = x.astype(BF16)
    r1 = x - hi.astype(F32)
    mid = r1.astype(BF16)
    lo = (r1 - mid.astype(F32)).astype(BF16)
    return hi, mid, lo


def _tri_ones(n, upper):
    r = lax.broadcasted_iota(jnp.int32, (n, n), 0)
    c = lax.broadcasted_iota(jnp.int32, (n, n), 1)
    return jnp.where((r <= c) if upper else (r >= c), 1.0, 0.0).astype(BF16)


def _gate_scan(f, bf, name):
    T = f.shape[0]
    tm = min(256, T)

    def body(f_ref, b_ref, cp_ref, sn_ref, carry_ref):
        i = pl.program_id(0)

        @pl.when(i == 0)
        def _():
            carry_ref[...] = jnp.zeros_like(carry_ref)

        x = f_ref[...] + b_ref[...]
        e = jnp.exp(-jnp.abs(x))
        logf = jnp.minimum(x, 0.0) - jnp.log(1.0 + e)
        sn_ref[...] = jnp.where(x >= 0.0, e / (1.0 + e), 1.0 / (1.0 + e))
        tri = _tri_ones(tm, upper=False)
        c = carry_ref[...]
        for piece in _split3(logf):
            c = c + jnp.dot(tri, piece, preferred_element_type=F32)
        carry_ref[...] += jnp.sum(logf, axis=0, keepdims=True)
        hi, mid, lo = _split3(c * LOG2E)
        cp_ref[:, 0:LANES] = hi
        cp_ref[:, LANES:2 * LANES] = mid
        cp_ref[:, 2 * LANES:3 * LANES] = lo

    return pl.pallas_call(
        body, name=name, grid=(T // tm,),
        in_specs=[pl.BlockSpec((tm, LANES), lambda i: (i, 0)), pl.BlockSpec((1, LANES), lambda i: (0, 0))],
        out_specs=[pl.BlockSpec((tm, 3 * LANES), lambda i: (i, 0)), pl.BlockSpec((tm, LANES), lambda i: (i, 0))],
        out_shape=[jax.ShapeDtypeStruct((T, 3 * LANES), BF16), jax.ShapeDtypeStruct((T, LANES), F32)],
        scratch_shapes=[pltpu.VMEM((1, LANES), F32)],
        compiler_params=_cp(("arbitrary",)),
    )(f, bf)


def _gate_scan_bwd(dcq, dck, sneg, name):
    T = dcq.shape[0]
    tm = min(256, T)
    n = T // tm

    def body(dcq_ref, dck_ref, sn_ref, df_ref, db_ref, carry_ref):
        i = pl.program_id(0)

        @pl.when(i == 0)
        def _():
            carry_ref[...] = jnp.zeros_like(carry_ref)
            db_ref[...] = jnp.zeros_like(db_ref)

        tri = _tri_ones(tm, upper=True)
        dcb = dcq_ref[...] - dck_ref[...]
        acc = carry_ref[...]
        for piece in _split3(dcb):
            acc = acc + jnp.dot(tri, piece, preferred_element_type=F32)
        carry_ref[...] += jnp.sum(dcb, axis=0, keepdims=True)
        df = acc * sn_ref[...]
        df_ref[...] = df.astype(BF16)
        db_ref[...] += jnp.sum(df, axis=0, keepdims=True)

    rev = pl.BlockSpec((tm, LANES), lambda i: (n - 1 - i, 0))
    return pl.pallas_call(
        body, name=name, grid=(n,),
        in_specs=[rev, rev, rev],
        out_specs=[rev, pl.BlockSpec((1, LANES), lambda i: (0, 0))],
        out_shape=[jax.ShapeDtypeStruct((T, LANES), BF16), jax.ShapeDtypeStruct((1, LANES), F32)],
        scratch_shapes=[pltpu.VMEM((1, LANES), F32)],
        compiler_params=_cp(("arbitrary",)),
    )(dcq, dck, sneg)


def _qk_proj(hn, w_pads, cp, sels, consts, scales, name):
    T, D = hn.shape
    H = w_pads[0].shape[1] // LANES
    tm = min(TM, T)

    def body(a_ref, cp_ref, wq_ref, wk_ref, sq_ref, sk_ref, cq_ref, ck_ref, qo_ref, ko_ref):
        a = a_ref[...]
        cpv = cp_ref[...]
        for w_ref, sel_ref, c_ref, o_ref, scale in ((wq_ref, sq_ref, cq_ref, qo_ref, scales[0]),
                                                    (wk_ref, sk_ref, ck_ref, ko_ref, scales[1])):
            for p in range(H // 2):
                acc = jnp.dot(a, w_ref[:, p * 2 * LANES:(p + 1) * 2 * LANES], preferred_element_type=F32)
                if scale != 1.0:
                    acc = acc * scale
                acc = acc + jnp.dot(cpv, sel_ref[p], preferred_element_type=F32) + c_ref[p]
                o_ref[2 * p] = acc[:, :LANES].astype(BF16)
                o_ref[2 * p + 1] = acc[:, LANES:].astype(BF16)

    whole = lambda t: pl.BlockSpec(t.shape, lambda i: (0,) * t.ndim)
    out = jax.ShapeDtypeStruct((H, T, LANES), BF16)
    o_spec = pl.BlockSpec((H, tm, LANES), lambda i: (0, i, 0))
    return pl.pallas_call(
        body, name=name, grid=(T // tm,),
        in_specs=[pl.BlockSpec((tm, D), lambda i: (i, 0)), pl.BlockSpec((tm, 3 * LANES), lambda i: (i, 0)),
                  whole(w_pads[0]), whole(w_pads[1]), whole(sels[0]), whole(sels[1]), whole(consts[0]), whole(consts[1])],
        out_specs=[o_spec, o_spec], out_shape=[out, out],
        compiler_params=_cp(("parallel",)),
    )(hn, cp, w_pads[0], w_pads[1], sels[0], sels[1], consts[0], consts[1])


def _lane_lo():
    return lax.broadcasted_iota(jnp.int32, (1, LANES), 1) < HEAD_DIM


def _attn_fwd_t(qp, kp, vt, name):
    H, T, _ = qp.shape
    tq = min(TQ, T)
    hd = HEAD_DIM
    ext = hd + 16

    def body(q_ref, k_ref, vt_ref, o_ref, o32_ref, lse_ref, m_sc, acc_sc):
        i = pl.program_id(1)
        m_sc[...] = jnp.full(m_sc.shape, NEG, F32)
        acc_sc[...] = jnp.zeros_like(acc_sc)
        q_t = [jnp.transpose(q_ref[h].astype(F32)).astype(BF16) for h in range(2)]
        ones_rows = jnp.where(lax.broadcasted_iota(jnp.int32, (16, tq), 0) == 0, 1.0, 0.0).astype(BF16)

        def steps(blocks):
            offs = [pl.multiple_of(j * tq, tq) for j, _ in blocks]
            s_all = [[jnp.dot(k_ref[h, pl.ds(off, tq), :], q_t[h], preferred_element_type=F32) for h in range(2)]
                     for off in offs]
            for (j, masked), off, s_blk in zip(blocks, offs, s_all):
                for h in range(2):
                    s = s_blk[h]
                    if masked:
                        kr = lax.broadcasted_iota(jnp.int32, (tq, tq), 0)
                        qc = lax.broadcasted_iota(jnp.int32, (tq, tq), 1)
                        s = jnp.where(qc >= kr, s, NEG)
                    m_prev = m_sc[h]
                    m_new = jnp.maximum(m_prev, jnp.max(s, axis=0, keepdims=True))
                    alpha = jnp.exp2(m_prev - m_new)
                    p16 = jnp.exp2(s - m_new).astype(BF16)
                    v_aug = jnp.concatenate([vt_ref[h * hd:(h + 1) * hd, pl.ds(off, tq)], ones_rows], axis=0)
                    pv = jnp.dot(v_aug, p16, preferred_element_type=F32)
                    acc_sc[h] = alpha * acc_sc[h] + pv
                    m_sc[h] = m_new

        def group_body(t, carry):
            steps([(KV_UNROLL * t + u, False) for u in range(KV_UNROLL)])
            return carry

        lax.fori_loop(0, i // KV_UNROLL, group_body, 0)
        for rem in range(KV_UNROLL):

            @pl.when(i % KV_UNROLL == rem)
            def _(rem=rem):
                steps([(i - rem + u, u == rem) for u in range(rem + 1)])

        o_t, lse_t = [], []
        for h in range(2):
            acc = acc_sc[h]
            l = acc[hd:hd + 1, :]
            o_t.append(acc[:hd, :] / l)
            lse_t.append(jnp.broadcast_to(m_sc[h] + jnp.log(l) * LOG2E, (hd, tq)))
        o = jnp.transpose(jnp.concatenate(o_t, axis=0))
        o_ref[...] = o.astype(BF16)
        o32_ref[...] = o
        lse_ref[...] = jnp.transpose(jnp.concatenate(lse_t, axis=0))

    oblk = pl.BlockSpec((tq, LANES), lambda p, i: (i, p))
    return pl.pallas_call(
        body, name=name, grid=(H // 2, T // tq),
        in_specs=[pl.BlockSpec((2, tq, LANES), lambda p, i: (p, i, 0)),
                  pl.BlockSpec((2, T, LANES), lambda p, i: (p, 0, 0)),
                  pl.BlockSpec((2 * hd, T), lambda p, i: (p, 0))],
        out_specs=[oblk, oblk, pl.BlockSpec((None, tq, LANES), lambda p, i: (p, i, 0))],
        out_shape=[jax.ShapeDtypeStruct((T, H * HEAD_DIM), BF16), jax.ShapeDtypeStruct((T, H * HEAD_DIM), F32),
                   jax.ShapeDtypeStruct((H // 2, T, LANES), F32)],
        scratch_shapes=[pltpu.VMEM((2, 1, tq), F32), pltpu.VMEM((2, ext, tq), F32)],
        compiler_params=_cp(("parallel", "arbitrary")),
    )(qp, kp, vt)


def _attn_bwd(qp, kp, v, o, do, lse, scale, name):
    H, T, _ = qp.shape
    tq = min(TQ, T)
    nq = T // tq
    nrep = tq // LANES

    def body(q_ref, k_ref, v_ref, o_ref, do_ref, lse_ref, dq_ref, dk_ref, dv_ref, dqe_ref, dke_ref, dk_sc, dv_sc, dq_sc):
        i = pl.program_id(1)

        @pl.when(i == 0)
        def _():
            dk_sc[...] = jnp.zeros_like(dk_sc)
            dv_sc[...] = jnp.zeros_like(dv_sc)

        dq_sc[...] = jnp.zeros_like(dq_sc)

        lo = _lane_lo()
        dob = do_ref[...]
        dof = dob.astype(F32)
        prod = dof * o_ref[...].astype(F32)
        lse2 = lse_ref[...]
        lse2_sw = pltpu.roll(lse2, HEAD_DIM, 1)
        zero = jnp.zeros_like(dob)
        do_h = [jnp.where(lo, dob, zero), jnp.where(lo, zero, dob)]
        rep = lambda col: jnp.broadcast_to(col, (tq, LANES))
        delta = [rep(jnp.sum(jnp.where(lo, prod, 0.0), axis=-1, keepdims=True)),
                 rep(jnp.sum(jnp.where(lo, 0.0, prod), axis=-1, keepdims=True))]
        lse_h = [jnp.where(lo, lse2, lse2_sw), jnp.where(lo, lse2_sw, lse2)]
        qs = [q_ref[0], q_ref[1]]
        tr16 = lambda a: jnp.transpose(a.astype(F32)).astype(BF16)
        q_t = [tr16(qs[0]), tr16(qs[1])]
        do_t = [tr16(do_h[0]), tr16(do_h[1])]

        def steps(blocks):
            offs = [pl.multiple_of(j * tq, tq) for j, _ in blocks]
            vblks = [v_ref[pl.ds(off, tq), :] for off in offs]
            kblks = [[k_ref[h, pl.ds(off, tq), :] for h in range(2)] for off in offs]
            s_all = [[_dot_nt(qs[h], kb[h]) for h in range(2)] for kb in kblks]
            dp_all = [[_dot_nt(do_h[h], vb) for h in range(2)] for vb in vblks]
            for b, ((j, masked), off) in enumerate(zip(blocks, offs)):
                dv_add = None
                for h in range(2):
                    kblk, s, dp = kblks[b][h], s_all[b][h], dp_all[b][h]
                    p16, ds16 = [], []
                    for c in range(nrep):
                        cols = slice(c * LANES, (c + 1) * LANES)
                        p = jnp.exp2(s[:, cols] - lse_h[h])
                        if masked:
                            r = lax.broadcasted_iota(jnp.int32, (tq, LANES), 0)
                            cc = lax.broadcasted_iota(jnp.int32, (tq, LANES), 1)
                            p = jnp.where(r >= cc + c * LANES, p, 0.0)
                        p16.append(p.astype(BF16))
                        ds16.append((p * (dp[:, cols] - delta[h])).astype(BF16))
                    p16 = jnp.concatenate(p16, axis=1)
                    dsb = jnp.concatenate(ds16, axis=1)
                    dq_sc[h] += jnp.dot(dsb, kblk, preferred_element_type=F32)
                    dk_sc[h, :, pl.ds(off, tq)] += jnp.dot(q_t[h], dsb, preferred_element_type=F32)
                    pv = jnp.dot(do_t[h], p16, preferred_element_type=F32)
                    dv_add = pv if dv_add is None else dv_add + pv
                dv_sc[:, pl.ds(off, tq)] += dv_add

        def group_body(t, carry):
            steps([(KV_UNROLL_BWD * t + u, False) for u in range(KV_UNROLL_BWD)])
            return carry

        lax.fori_loop(0, i // KV_UNROLL_BWD, group_body, 0)
        for rem in range(KV_UNROLL_BWD):

            @pl.when(i % KV_UNROLL_BWD == rem)
            def _(rem=rem):
                steps([(i - rem + u, u == rem) for u in range(rem + 1)])

        dq0, dq1 = dq_sc[0], dq_sc[1]
        dq_ref[...] = (jnp.where(lo, dq0, pltpu.roll(dq1, HEAD_DIM, 1)) * scale).astype(BF16)
        dqe_ref[0:8, :] = jnp.transpose(dq0)[HEAD_DIM:HEAD_DIM + 8, :]
        dqe_ref[8:16, :] = jnp.transpose(dq1)[HEAD_DIM:HEAD_DIM + 8, :]

        @pl.when(i == nq - 1)
        def _():
            for h in range(2):
                dke_ref[8 * h:8 * h + 8, :] = dk_sc[h, HEAD_DIM:HEAD_DIM + 8, :]
            for cb in range(nq):
                tok = slice(cb * tq, (cb + 1) * tq)
                dk0 = jnp.transpose(dk_sc[0, :, tok])
                dk1 = jnp.transpose(dk_sc[1, :, tok])
                dk_ref[tok, :] = (jnp.where(lo, dk0, pltpu.roll(dk1, HEAD_DIM, 1)) * LN2).astype(BF16)
                dv_ref[tok, :] = jnp.transpose(dv_sc[:, tok]).astype(BF16)

    qblk = pl.BlockSpec((tq, LANES), lambda p, i: (i, p))
    pair = pl.BlockSpec((T, LANES), lambda p, i: (0, p))
    tok16 = jax.ShapeDtypeStruct((T, H * HEAD_DIM), BF16)
    gate32 = jax.ShapeDtypeStruct((H // 2, 16, T), F32)
    return pl.pallas_call(
        body, name=name, grid=(H // 2, nq),
        in_specs=[pl.BlockSpec((2, tq, LANES), lambda p, i: (p, i, 0)),
                  pl.BlockSpec((2, T, LANES), lambda p, i: (p, 0, 0)),
                  pair, qblk, qblk,
                  pl.BlockSpec((None, tq, LANES), lambda p, i: (p, i, 0))],
        out_specs=[qblk, pair, pair, pl.BlockSpec((None, 16, tq), lambda p, i: (p, 0, i)),
                   pl.BlockSpec((None, 16, T), lambda p, i: (p, 0, 0))],
        out_shape=[tok16, tok16, tok16, gate32, gate32],
        scratch_shapes=[pltpu.VMEM((2, LANES, T), F32), pltpu.VMEM((LANES, T), F32),
                        pltpu.VMEM((2, tq, LANES), F32)],
        compiler_params=_cp(("parallel", "arbitrary")),
    )(qp, kp, v, o, do, lse)


def _mesh_pos():
    return lax.axis_index("x"), lax.axis_index("y"), lax.axis_index("c")


def _all_gather(arrs, name, groups=None):
    n = len(arrs)
    if groups is None:
        groups = [(a, 0) for a in range(n)]
    ng = 1 + max(g for g, _ in groups)
    per_group = [sum(1 for g, _ in groups if g == gi) for gi in range(ng)]
    first_of = [next(a for a in range(n) if groups[a][0] == gi) for gi in range(ng)]

    def body(*refs):
        ins, outs = refs[:n], refs[n:n + ng]
        send_sems, recv_sems, local_sems = refs[n + ng:]
        x, y, c = _mesh_pos()
        me, sib = (x, y, c), (x, y, 1 - c)
        chips = [(1 - x, y), (x, 1 - y), (1 - x, 1 - y)]

        def dst_of(a, px, py, pc):
            g, k = groups[a]
            return outs[g].at[N_DEV * k + 4 * px + 2 * py + pc]

        def copy(a, k, block, to, src=None):
            dst = dst_of(a, *block)
            return pltpu.make_async_remote_copy(
                src_ref=dst if src is None else src, dst_ref=dst,
                send_sem=send_sems.at[a, k], recv_sem=recv_sems.at[a, k], device_id=to, device_id_type=MESH)

        mine = [pltpu.make_async_copy(ins[a], dst_of(a, *me), local_sems.at[a]) for a in range(n)]
        for cp in mine:
            cp.start()
        first = []
        for a in range(n):
            first.append(copy(a, 0, me, sib, src=ins[a]))
            first += [copy(a, 1 + j, me, (*chip, c), src=ins[a]) for j, chip in enumerate(chips)]
        for cp in first:
            cp.start()
        passed = []
        for j, chip in enumerate(chips):
            for a in range(n):
                copy(a, 1 + j, (*chip, c), me).wait_recv()
                fwd = copy(a, 4 + j, (*chip, c), sib)
                fwd.start()
                passed.append(fwd)
        for a in range(n):
            copy(a, 0, sib, me).wait_recv()
            for j, chip in enumerate(chips):
                copy(a, 4 + j, (*chip, 1 - c), me).wait_recv()
        for cp in first + passed:
            cp.wait_send()
        for cp in mine:
            cp.wait()

    any_spec = pl.BlockSpec(memory_space=pl.ANY)
    return pl.pallas_call(
        body, name=name,
        in_specs=[any_spec] * n, out_specs=[any_spec] * ng,
        out_shape=[jax.ShapeDtypeStruct((N_DEV * per_group[gi],) + arrs[first_of[gi]].shape, arrs[first_of[gi]].dtype)
                   for gi in range(ng)],
        scratch_shapes=[pltpu.SemaphoreType.DMA((n, 7)), pltpu.SemaphoreType.DMA((n, 7)),
                        pltpu.SemaphoreType.DMA((n,))],
    )(*arrs)


HBM_SPEC = pl.BlockSpec(memory_space=pltpu.HBM)
SEM_SPEC = pl.BlockSpec(memory_space=pltpu.SEMAPHORE)
ANY_SPEC = pl.BlockSpec(memory_space=pl.ANY)
DATAFLOW_EFFECT = pltpu.SideEffectType.DATAFLOW_SIDE_EFFECTING


def _peers():
    x, y, c = _mesh_pos()
    flip = lambda v, b: 1 - v if b else v
    return [(flip(x, (k >> 2) & 1), flip(y, (k >> 1) & 1), flip(c, k & 1)) for k in range(1, N_DEV)]


def _slot(p):
    return 4 * p[0] + 2 * p[1] + p[2]


def _direct_copy(src_refs, land_refs, sems, a, k, p, land_of, dst_slot, src_slot):
    s = src_slot(a, p)
    return pltpu.make_async_remote_copy(
        src_ref=src_refs[a] if s is None else src_refs[a].at[s], dst_ref=land_refs[land_of[a]].at[dst_slot(a, k)],
        send_sem=sems[0].at[a * (N_DEV - 1) + k], recv_sem=sems[1].at[a * (N_DEV - 1) + k], device_id=p,
        device_id_type=MESH)


def _direct_start(srcs, lands, land_of, dst_slot, src_slot, after, name, collective_id):
    n, nl = len(srcs), len(lands)

    def body(*refs):
        src_refs, land_refs = refs[:n], refs[n:n + nl]
        sems = (refs[n + nl + 1], refs[n + nl + 2])
        token = refs[-1]
        peers = _peers()
        barrier = pltpu.get_barrier_semaphore()
        for p in peers:
            pl.semaphore_signal(barrier, inc=1, device_id=p, device_id_type=MESH)
        pl.semaphore_wait(barrier, N_DEV - 1)
        for a in range(n):
            for k, p in enumerate(peers):
                _direct_copy(src_refs, land_refs, sems, a, k, p, land_of, dst_slot, src_slot).start()
        token[...] = jnp.zeros_like(token)

    hbm = lambda t: pltpu.HBM(t.shape, t.dtype)
    sem_t = pltpu.SemaphoreType.DMA((n * (N_DEV - 1),))
    outs = pl.pallas_call(
        body, name=name,
        out_shape=(sem_t, sem_t, *[hbm(t) for t in srcs], *[hbm(t) for t in lands], jax.ShapeDtypeStruct((8, LANES), F32)),
        in_specs=[HBM_SPEC] * (n + nl) + [ANY_SPEC],
        out_specs=(SEM_SPEC, SEM_SPEC, *([HBM_SPEC] * (n + nl)), pl.BlockSpec(memory_space=pltpu.VMEM)),
        input_output_aliases={i: 2 + i for i in range(n + nl)},
        compiler_params=pltpu.CompilerParams(has_side_effects=DATAFLOW_EFFECT, collective_id=collective_id),
    )(*[pltpu.with_memory_space_constraint(t, pltpu.HBM) for t in srcs],
      *[pltpu.with_memory_space_constraint(t, pltpu.HBM) for t in lands], after)
    return outs[0], outs[1], list(outs[2:2 + n]), list(outs[2 + n:2 + n + nl]), outs[-1]


def _direct_wait(send_sems, recv_sems, srcs, lands, land_of, idxs, dst_slot, src_slot, after, name):
    land_ids = []
    for a in idxs:
        if land_of[a] not in land_ids:
            land_ids.append(land_of[a])
    m, ml = len(idxs), len(land_ids)
    sub_land_of = {j: land_ids.index(land_of[a]) for j, a in enumerate(idxs)}

    def body(*refs):
        src_refs, land_refs = refs[:m], refs[m:m + ml]
        ssem, rsem = refs[m + ml], refs[m + ml + 1]
        for j, a in enumerate(idxs):
            for k, p in enumerate(_peers()):
                s = src_slot(a, p)
                cp = pltpu.make_async_remote_copy(
                    src_ref=src_refs[j] if s is None else src_refs[j].at[s],
                    dst_ref=land_refs[sub_land_of[j]].at[dst_slot(a, k)],
                    send_sem=ssem.at[a * (N_DEV - 1) + k], recv_sem=rsem.at[a * (N_DEV - 1) + k], device_id=p,
                    device_id_type=MESH)
                cp.wait_send()
                cp.wait_recv()

    hbm = lambda t: pltpu.HBM(t.shape, t.dtype)
    sub_s, sub_l = [srcs[a] for a in idxs], [lands[g] for g in land_ids]
    outs = pl.pallas_call(
        body, name=name,
        out_shape=(*[hbm(t) for t in sub_s], *[hbm(t) for t in sub_l]),
        in_specs=[HBM_SPEC] * (m + ml) + [SEM_SPEC, SEM_SPEC, ANY_SPEC],
        out_specs=tuple([HBM_SPEC] * (m + ml)),
        input_output_aliases={i: i for i in range(m + ml)},
        compiler_params=pltpu.CompilerParams(has_side_effects=DATAFLOW_EFFECT),
    )(*sub_s, *sub_l, send_sems, recv_sems, after)
    return list(outs[:m]), list(outs[m:])


def _row_block(R, C):
    best = None
    for d in range(16, R + 1, 16):
        if R % d == 0 and d * C <= 256 * 1024:
            best = d
    return best if best is not None else R


def _adamw_math(w, g, m, v):
    m = ADAM_B1 * m + (1.0 - ADAM_B1) * g
    v = ADAM_B2 * v + (1.0 - ADAM_B2) * (g * g)
    m_hat = m / (1.0 - ADAM_B1 ** ADAM_STEP)
    v_hat = v / (1.0 - ADAM_B2 ** ADAM_STEP)
    delta = -ADAM_LR * (m_hat / (jnp.sqrt(v_hat) + ADAM_EPS) + ADAM_WD * w)
    return delta, m, v


def _sum_adamw(parts, w, m, v, name, sel=None):
    R, C = w.shape
    tr = _row_block(R, C)
    specs, args = [], []
    for arr, idxs in parts:
        for idx in idxs:
            if idx < 0:
                specs.append(pl.BlockSpec((None, tr, C), lambda i, s: (s[0], i, 0)))
            else:
                specs.append(pl.BlockSpec((None, tr, C), lambda i, s, idx=idx: (idx, i, 0)))
            args.append(arr)
    npart = len(args)
    blk = pl.BlockSpec((tr, C), lambda i, s: (i, 0))

    def body(s_ref, *refs):
        del s_ref
        g = refs[0][...].astype(F32)
        for r in refs[1:npart]:
            g = g + r[...].astype(F32)
        w_ref, m_ref, v_ref, g_out, d_out, m_out, v_out = refs[npart:]
        delta, mm, vv = _adamw_math(w_ref[...], g, m_ref[...], v_ref[...])
        g_out[...] = g
        d_out[...] = delta
        m_out[...] = mm
        v_out[...] = vv

    grid_spec = pltpu.PrefetchScalarGridSpec(
        num_scalar_prefetch=1, grid=(R // tr,),
        in_specs=specs + [blk, blk, blk], out_specs=[blk] * 4)
    if sel is None:
        sel = jnp.zeros((1,), jnp.int32)
    return pl.pallas_call(
        body, name=name, grid_spec=grid_spec,
        out_shape=[jax.ShapeDtypeStruct((R, C), F32)] * 4,
        compiler_params=_cp(("parallel",)),
    )(sel, *args, w, m, v)


def _rows(a, c):
    return a.reshape(-1, c)


def _pad_rows(a, r):
    return jnp.pad(a, ((0, r - a.shape[0]), (0, 0))) if a.shape[0] != r else a


def _gate_tables():
    hp = N_HEADS // 2
    sel_q = np.zeros((hp, 3 * LANES, 2 * LANES), np.float32)
    sel_k = np.zeros((hp, 3 * LANES, 2 * LANES), np.float32)
    const_q = np.zeros((hp, 1, 2 * LANES), np.float32)
    const_k = np.zeros((hp, 1, 2 * LANES), np.float32)
    for p in range(hp):
        for hh in range(2):
            h = 2 * p + hh
            base = hh * LANES + HEAD_DIM
            for piece in range(3):
                sel_q[p, piece * LANES + h, base + piece] = 1.0
                sel_k[p, piece * LANES + h, base + 3 + piece] = -1.0
            const_k[p, 0, base:base + 3] = 1.0
            const_q[p, 0, base + 3:base + 6] = 1.0
    as_bf = lambda t: jnp.asarray(t, BF16)
    return as_bf(sel_q), as_bf(sel_k), jnp.asarray(const_q), jnp.asarray(const_k)


def _pad_heads(w):
    d = w.shape[0]
    w3 = w.reshape(d, N_HEADS, HEAD_DIM)
    return jnp.pad(w3, ((0, 0), (0, 0), (0, LANES - HEAD_DIM))).reshape(d, N_HEADS * LANES)


def kernel(x, mix_norm_g, ffn_norm_g, gm_w_in, gm_ln_g, gm_ln_b, gm_w_s, gm_b_s, gm_w_out, fox_w_qkvf, fox_b_f, fox_w_o, ffn_w_gate, ffn_w_up, ffn_conv_w, ffn_conv_b, ffn_w_down, final_norm_g, loss_target, m_mix_norm_g, m_ffn_norm_g, m_gm_w_in, m_gm_ln_g, m_gm_ln_b, m_gm_w_s, m_gm_b_s, m_gm_w_out, m_fox_w_qkvf, m_fox_b_f, m_fox_w_o, m_ffn_w_gate, m_ffn_w_up, m_ffn_conv_w, m_ffn_conv_b, m_ffn_w_down, m_final_norm_g, v_mix_norm_g, v_ffn_norm_g, v_gm_w_in, v_gm_ln_g, v_gm_ln_b, v_gm_w_s, v_gm_b_s, v_gm_w_out, v_fox_w_qkvf, v_fox_b_f, v_fox_w_o, v_ffn_w_gate, v_ffn_w_up, v_ffn_conv_w, v_ffn_conv_b, v_ffn_w_down, v_final_norm_g):
    T, D = x.shape[1], x.shape[2]
    E = gm_ln_g.shape[1]
    FF = ffn_conv_b.shape[1]
    NQKVF = 3 * D + N_HEADS
    xi, yi, ci = _mesh_pos()
    me = 4 * xi + 2 * yi + ci
    h0 = x.reshape(T, D)
    tgt = loss_target.reshape(T, D)

    nl = ffn_w_gate.shape[0]
    to16 = lambda a: a.astype(BF16)
    n_cw_rows = ffn_conv_w.size // LANES
    cw_rows = _pad_rows(_rows(ffn_conv_w.astype(F32), LANES), 16)
    w_in_g, w_out_g8, cwg = _all_gather([to16(gm_w_in[0]), to16(gm_w_out[0]), cw_rows], "ag_weights")
    w_out_g = w_out_g8.reshape(E, D)
    later, land_of, land_off, lands = [], [], [], []
    for l in range(nl):
        later += [to16(ffn_w_gate[l]), to16(ffn_w_up[l]), to16(ffn_w_down[l])]
        land_of += [2 * l, 2 * l, 2 * l + 1]
        land_off += [0, N_DEV, 0]
        lands += [lax.empty((2 * N_DEV, D, FF // N_DEV), BF16), lax.empty((N_DEV, FF // N_DEV, D), BF16)]
    later += [to16(fox_w_qkvf[0]), to16(fox_w_o[0])]
    land_of += [2 * nl, 2 * nl + 1]
    land_off += [0, 0]
    lands += [lax.empty((N_DEV, D, NQKVF // N_DEV), BF16), lax.empty((N_DEV, D // N_DEV, D), BF16)]
    ag_dst = lambda a, k: land_off[a] + _slot(_mesh_pos())
    ag_src = lambda a, p: None
    ag_send, ag_recv, later, lands, ag_token = _direct_start(later, lands, land_of, ag_dst, ag_src, w_in_g,
                                                             "ag_later_start", collective_id=1)

    def own_blocks(landed, shards, offs):
        for s, o in zip(shards, offs):
            landed = lax.dynamic_update_index_in_dim(landed, s, o + me, 0)
        return landed

    def gather_wait(idxs, after, name):
        return _direct_wait(ag_send, ag_recv, later, lands, land_of, idxs, ag_dst, ag_src, after, name)

    conv_w_full = jnp.transpose(cwg[:, :n_cw_rows].reshape(N_DEV, nl, 3, FF // N_DEV), (1, 2, 0, 3)).reshape(nl, 3, FF)

    ffn_w = {}

    def ffn_weights(l):
        return ffn_w[l]

    def land_ffn(l, shards, gu_land, dn_land):
        ffn_w[l] = (own_blocks(gu_land, shards[:2], [0, N_DEV]), own_blocks(dn_land, shards[2:3], [0]).reshape(FF, D))

    saved = {}

    def ffn_fwd(l, h_in, hn, next_g):
        wgul, wdl = ffn_weights(l)
        au, act = _ffn_up_fused(hn, wgul, conv_w_full[l], ffn_conv_b[l:l + 1], f"ffn{l}_up")
        saved[f"ffn{l}"] = (h_in, hn, au, act)
        if next_g is None:
            return _mm_nn(act, wdl, f"ffn{l}_down", res=h_in), None
        return _mm_nn(act, wdl, f"ffn{l}_down", res=h_in, norm_g=next_g)

    bs_col = gm_b_s[0].reshape(GM_GROUPS, CHUNK, 1)
    hn0 = _rms_fwd(h0, mix_norm_g[0:1], "mix0_norm", after=ag_token)
    z, gu = _gm_in_fused(hn0, w_in_g, gm_ln_g, gm_ln_b, gm_w_s[0], bs_col, "gm_in")
    h1, hn_f0 = _mm_nn(gu, w_out_g, "gm_out", res=h0, norm_g=ffn_norm_g[0:1])
    mine0, land0 = gather_wait([0, 1, 2], h1, "ag_ffn0_wait")
    land_ffn(0, mine0, *land0)
    h2, hn2 = ffn_fwd(0, h1, hn_f0, mix_norm_g[1:2])

    mine1, rest = gather_wait(list(range(3, 3 * nl + 2)), h2, "ag_layer1_wait")
    for l in range(1, nl):
        land_ffn(l, mine1[3 * (l - 1):3 * l], rest[2 * (l - 1)], rest[2 * (l - 1) + 1])
    w_qkvf = jnp.transpose(own_blocks(rest[-2], mine1[-2:-1], [0]), (1, 0, 2)).reshape(D, NQKVF)
    w_o_g = own_blocks(rest[-1], mine1[-1:], [0]).reshape(D, D)
    w_q, w_k, w_v = w_qkvf[:, :D], w_qkvf[:, D:2 * D], w_qkvf[:, 2 * D:3 * D]
    w_f = jnp.pad(w_qkvf[:, 3 * D:], ((0, 0), (0, LANES - N_HEADS)))
    bf_row = jnp.pad(fox_b_f, ((0, 0), (0, LANES - N_HEADS)))
    sel_q, sel_k, const_q, const_k = _gate_tables()
    scale = HEAD_DIM ** -0.5
    f_logit = _mm_nn(hn2, w_f, "fox_f")
    cp, sneg = _gate_scan(f_logit, bf_row, "fox_scan")
    qp, kp = _qk_proj(hn2, (_pad_heads(w_q), _pad_heads(w_k)), cp, (sel_q, sel_k), (const_q, const_k),
                      (scale * LOG2E, 1.0), "fox_qk")
    vv = _mm_nn(hn2, w_v, "fox_v", out_dtype=BF16)
    o, o32, lse = _attn_fwd_t(qp, kp, jnp.transpose(vv), "fox_attn")
    h3, hn_f1 = _mm_nn(o, w_o_g, "fox_o", res=h2, norm_g=ffn_norm_g[1:2])
    h4, _ = ffn_fwd(1, h3, hn_f1, None)

    dh, dh16, d_final, loss_row = _loss_head(h4, tgt, final_norm_g.reshape(1, D), "loss_head")
    loss = lax.psum(loss_row[0, 0], ("x", "y", "c"))

    rs_dst = lambda a, k: k
    rs_src = lambda a, p: _slot(p)
    me_idx = me.astype(jnp.int32).reshape(1)

    def rs_start(grads, name, cid):
        lands = [lax.empty((N_DEV - 1,) + g.shape[1:], BF16) for g in grads]
        return _direct_start(grads, lands, list(range(len(grads))), rs_dst, rs_src, loss_row, name, collective_id=cid)

    def rs_wait(st, after, name):
        n = len(st[2])
        return _direct_wait(st[0], st[1], st[2], st[3], list(range(n)), list(range(n)), rs_dst, rs_src, after, name)

    def ffn_bwd(l, dh, dh16, after=None):
        wgul, wdl = ffn_weights(l)
        h_in, hn, au, act = saved[f"ffn{l}"]
        da, dup, d_cw, d_cb = _ffn_mid_bwd(au, dh16, wdl, conv_w_full[l], ffn_conv_b[l:l + 1], f"ffn{l}_dmid", after=after)
        d_wd = _mm_tn(act, dh16, f"ffn{l}_dwd", out_dtype=BF16)
        dh_in, dh_in16, d_norm = _mm_nt([da, dup], wgul, f"ffn{l}_dhn", norm_bwd=(h_in, ffn_norm_g[l:l + 1], dh))
        d_wg = _mm_tn(hn, da, f"ffn{l}_dwg", blocked_w=FF // N_DEV, out_dtype=BF16)
        d_wu = _mm_tn(hn, dup, f"ffn{l}_dwu", blocked_w=FF // N_DEV, out_dtype=BF16)
        big_g = [d_wg, d_wu, d_wd.reshape(N_DEV, FF // N_DEV, D)]
        return dh_in, dh_in16, big_g, dict(cw=d_cw, cb=d_cb, norm=d_norm)

    dh, dh16, big_ffn1, g_ffn1 = ffn_bwd(1, dh, dh16)

    do = _mm_nt([dh16], w_o_g, "fox_do", out_dtype=BF16)
    d_wo = _mm_tn(o, dh16, "fox_dwo", out_dtype=BF16)
    dq, dk, dv, dqe, dke = _attn_bwd(qp, kp, vv, o32, do, lse, scale, "fox_dattn")
    gate_lane = lambda e, r: jnp.pad(jnp.transpose(e[:, r::8, :].reshape(N_HEADS, T)), ((0, 0), (0, LANES - N_HEADS)))
    df, d_bf = _gate_scan_bwd(gate_lane(dqe, 0), gate_lane(dke, 3), sneg, "fox_dscan")
    dhn = _mm_nt([df], w_f, "fox_dhn_f")
    dh_mix1 = _mm_nt([dq, dk, dv], w_qkvf[:, :3 * D], "fox_dhn_qkv", add=dhn, norm_bwd=(h2, mix_norm_g[1:2], dh))
    d_wq = _mm_tn(hn2, dq, "fox_dwq", out_dtype=BF16)
    d_wk = _mm_tn(hn2, dk, "fox_dwk", out_dtype=BF16)
    d_wv = _mm_tn(hn2, dv, "fox_dwv", out_dtype=BF16)
    d_wf = _mm_tn(hn2, df, "fox_dwf", out_dtype=BF16)
    d_wqkvf = jnp.concatenate([d_wq, d_wk, d_wv, d_wf[:, :N_HEADS]], axis=1)
    dh, dh16, d_mix1 = dh_mix1
    st1 = rs_start([jnp.transpose(d_wqkvf.reshape(D, N_DEV, NQKVF // N_DEV), (1, 0, 2)),
                    d_wo.reshape(N_DEV, D // N_DEV, D)] + big_ffn1, "rs1_start", 2)

    dh, dh16, big_ffn0, g_ffn0 = ffn_bwd(0, dh, dh16, after=st1[4])
    st2 = rs_start(big_ffn0, "rs2_start", 3)

    dz, d_lng, d_lnb, d_ws, d_bs = _sgu_bwd(z, dh16, w_out_g, gm_ln_g, gm_ln_b, gm_w_s[0], bs_col, "gm_dsgu", after=st2[4])
    d_wout = _mm_tn(gu, dh16, "gm_dwout", out_dtype=BF16)
    d_win = _mm_tn(hn0, dz, "gm_dwin", blocked_w=2 * E // N_DEV, out_dtype=BF16)
    st3 = rs_start([d_win, d_wout.reshape(N_DEV, E // N_DEV, D)], "rs3_start", 4)
    dx, _, d_mix0 = _mm_nt([dz], w_in_g, "gm_dhn", after=st3[4], norm_bwd=(h0, mix_norm_g[0:1], dh))

    own1, land1 = rs_wait(st1, dx, "rs1_wait")
    own2, land2 = rs_wait(st2, land1[0], "rs2_wait")
    cat1 = lambda a, b: jnp.concatenate([a, b], axis=1)
    big_out = {}

    def big_adamw(name, w, m, v, own, landed):
        shard2d = lambda a, c=own.shape[2]: a.reshape(-1, c)
        res = _sum_adamw([(own, [-1]), (landed, list(range(N_DEV - 1)))], shard2d(w), shard2d(m), shard2d(v),
                         f"adamw_{name}", sel=me_idx)
        big_out[name] = [t.reshape(w.shape) for t in res]

    big_adamw("fox_w_qkvf", fox_w_qkvf, m_fox_w_qkvf, v_fox_w_qkvf, own1[0], land1[0])
    big_adamw("fox_w_o", fox_w_o, m_fox_w_o, v_fox_w_o, own1[1], land1[1])
    big_adamw("ffn_w_gate", ffn_w_gate, m_ffn_w_gate, v_ffn_w_gate, cat1(own2[0], own1[2]), cat1(land2[0], land1[2]))
    big_adamw("ffn_w_up", ffn_w_up, m_ffn_w_up, v_ffn_w_up, cat1(own2[1], own1[3]), cat1(land2[1], land1[3]))
    big_adamw("ffn_w_down", ffn_w_down, m_ffn_w_down, v_ffn_w_down, cat1(own2[2], own1[4]), cat1(land2[2], land1[4]))

    small = [("mix_norm_g", mix_norm_g, m_mix_norm_g, v_mix_norm_g, jnp.concatenate([d_mix0, d_mix1], axis=0)),
             ("ffn_norm_g", ffn_norm_g, m_ffn_norm_g, v_ffn_norm_g, jnp.concatenate([g_ffn0["norm"], g_ffn1["norm"]], axis=0)),
             ("gm_ln_g", gm_ln_g, m_gm_ln_g, v_gm_ln_g, d_lng),
             ("gm_ln_b", gm_ln_b, m_gm_ln_b, v_gm_ln_b, d_lnb),
             ("gm_w_s", gm_w_s, m_gm_w_s, v_gm_w_s, d_ws),
             ("gm_b_s", gm_b_s, m_gm_b_s, v_gm_b_s, d_bs),
             ("fox_b_f", fox_b_f, m_fox_b_f, v_fox_b_f, d_bf[:, :N_HEADS]),
             ("ffn_conv_b", ffn_conv_b, m_ffn_conv_b, v_ffn_conv_b, jnp.concatenate([g_ffn0["cb"], g_ffn1["cb"]], axis=0)),
             ("final_norm_g", final_norm_g, m_final_norm_g, v_final_norm_g, d_final)]
    d_cw_full = jnp.stack([g_ffn0["cw"], g_ffn1["cw"]], axis=0)

    def small_rows(a):
        flat = a.astype(F32).reshape(-1)
        n = -(-flat.size // (8 * LANES)) * (8 * LANES)
        return jnp.pad(flat, (0, n - flat.size)).reshape(-1, LANES)

    s_rows = [small_rows(p[1]).shape[0] for p in small]
    s_off = np.concatenate([[0], np.cumsum(s_rows)]).tolist()
    cw_g_rows = small_rows(d_cw_full)
    zeros_cw = jnp.zeros_like(cw_g_rows)
    cat = lambda k: jnp.concatenate([small_rows(p[k]) for p in small] + [zeros_cw], axis=0)
    g_small = jnp.concatenate([small_rows(p[4]) for p in small] + [cw_g_rows], axis=0)
    (gs_all,) = _all_gather([g_small], "ag_small_grads")
    small_out = _sum_adamw([(gs_all, list(range(N_DEV)))], cat(1), cat(2), cat(3), "adamw_small")
    gs = small_out[0]

    g_cw_full = gs[s_off[-1]:].reshape(-1)[:d_cw_full.size].reshape(d_cw_full.shape)
    g_cw = lax.dynamic_slice_in_dim(g_cw_full, me * (FF // N_DEV), FF // N_DEV, axis=2)
    cw2 = lambda a: _pad_rows(_rows(a.astype(F32), LANES), 16)
    cw_out = _sum_adamw([(cw2(g_cw)[None], [0])], cw2(ffn_conv_w), cw2(m_ffn_conv_w), cw2(v_ffn_conv_w), "adamw_conv_w")

    own3, land3 = rs_wait(st3, cw_out[0], "rs3_wait")
    big_adamw("gm_w_in", gm_w_in, m_gm_w_in, v_gm_w_in, own3[0], land3[0])
    big_adamw("gm_w_out", gm_w_out, m_gm_w_out, v_gm_w_out, own3[1], land3[1])

    names = ["mix_norm_g", "ffn_norm_g", "gm_w_in", "gm_ln_g", "gm_ln_b", "gm_w_s", "gm_b_s", "gm_w_out", "fox_w_qkvf",
             "fox_b_f", "fox_w_o", "ffn_w_gate", "ffn_w_up", "ffn_conv_w", "ffn_conv_b", "ffn_w_down", "final_norm_g"]
    small_idx = {p[0]: k for k, p in enumerate(small)}

    def pick(kind, name):
        if name in big_out:
            return big_out[name][kind]
        if name == "ffn_conv_w":
            return cw_out[kind][:n_cw_rows].reshape(ffn_conv_w.shape)
        k = small_idx[name]
        shp = small[k][1].shape
        return small_out[kind][s_off[k]:s_off[k + 1]].reshape(-1)[:int(np.prod(shp))].reshape(shp)

    outs = [loss, dx.reshape(x.shape)]
    for kind in range(4):
        outs += [pick(kind, n) for n in names]
    return tuple(outs)
```

```python
import math

import numpy as np
import jax
import jax.numpy as jnp
from jax import lax
from jax.experimental import pallas as pl
from jax.experimental.pallas import tpu as pltpu

F32 = jnp.float32
BF16 = jnp.bfloat16
MESH = pl.DeviceIdType.MESH

N_HEADS = 16
HEAD_DIM = 64
CHUNK = 128
GM_GROUPS = 8
RMS_EPS = 1e-6
LN_EPS = 1e-5
ADAM_LR = 0.001
ADAM_B1 = 0.9
ADAM_B2 = 0.999
ADAM_EPS = 1e-08
ADAM_WD = 0.01
ADAM_STEP = 10
N_DEV = 8

LANES = 128
VMEM_BYTES_V7X = 64 * 1024 * 1024
VMEM_LIMIT = 56 * 1024 * 1024

TM = 512
TM_MM = 1024
TT = 1024
TQ = 512
TF = 512
KV_UNROLL_BWD = 2
KV_UNROLL = 2
TN_ROWS = 512
MM_BLOCK_BYTES = 8 * 1024 * 1024
NEG = -1e30
LOG2E = math.log2(math.e)
LN2 = math.log(2.0)


def _cp(sem=None, vmem=VMEM_LIMIT):
    return pltpu.CompilerParams(dimension_semantics=sem, vmem_limit_bytes=vmem)


def _gelu(x):
    c = math.sqrt(2.0 / math.pi)
    return x * (0.5 * (1.0 + jnp.tanh(c * (x + 0.044715 * (x * x * x)))))


def _gelu_grad(x):
    c = math.sqrt(2.0 / math.pi)
    t = jnp.tanh(c * (x + 0.044715 * (x * x * x)))
    return 0.5 * (1.0 + t) + x * (0.5 * (1.0 - t * t)) * (c * (1.0 + 3.0 * 0.044715 * (x * x)))


def _sigmoid(x):
    return 1.0 / (1.0 + jnp.exp(-x))


def _dot_nt(a, b):
    return lax.dot_general(a, b, (((1,), (1,)), ((), ())), preferred_element_type=F32)


def _dot_tn(a, b):
    return lax.dot_general(a, b, (((0,), (0,)), ((), ())), preferred_element_type=F32)


def _rms_fwd(h, g, name, after=None):
    T, D = h.shape
    tm = min(TM, T)

    def body(h_ref, g_ref, *rest):
        o_ref = rest[-1]
        x = h_ref[...]
        r = lax.rsqrt(jnp.mean(x * x, axis=-1, keepdims=True) + RMS_EPS)
        o_ref[...] = ((x * r) * g_ref[...]).astype(BF16)

    in_specs = [pl.BlockSpec((tm, D), lambda i: (i, 0)), pl.BlockSpec((1, D), lambda i: (0, 0))]
    args = [h, g]
    if after is not None:
        in_specs.append(pl.BlockSpec(memory_space=pl.ANY))
        args.append(after)
    return pl.pallas_call(
        body, name=name, grid=(T // tm,),
        in_specs=in_specs,
        out_specs=pl.BlockSpec((tm, D), lambda i: (i, 0)),
        out_shape=jax.ShapeDtypeStruct((T, D), BF16),
        compiler_params=_cp(("parallel",)),
    )(*args)


def _loss_head(h, tgt, g, name):
    T, D = h.shape
    tm = min(TM, T)

    def body(h_ref, t_ref, g_ref, o_ref, ob_ref, dg_ref, l_ref):
        x = h_ref[...]
        gg = g_ref[...]
        r = lax.rsqrt(jnp.mean(x * x, axis=-1, keepdims=True) + RMS_EPS)
        xr = x * r
        e = xr * gg - t_ref[...]
        lpart = 0.5 * jnp.sum(jnp.mean(e * e, axis=-1, keepdims=True), axis=0, keepdims=True)
        dy = e * (1.0 / D)
        dyg = dy * gg
        dot = jnp.mean(dyg * x, axis=-1, keepdims=True)
        dh = r * dyg - x * ((r * r * r) * dot)
        o_ref[...] = dh
        ob_ref[...] = dh.astype(BF16)
        part = jnp.sum(dy * xr, axis=0, keepdims=True)
        lrow = jnp.broadcast_to(lpart, (1, LANES))

        @pl.when(pl.program_id(0) == 0)
        def _():
            dg_ref[...] = part
            l_ref[...] = lrow

        @pl.when(pl.program_id(0) != 0)
        def _():
            dg_ref[...] += part
            l_ref[...] += lrow

    blk = pl.BlockSpec((tm, D), lambda i: (i, 0))
    row = pl.BlockSpec((1, D), lambda i: (0, 0))
    return pl.pallas_call(
        body, name=name, grid=(T // tm,),
        in_specs=[blk, blk, row],
        out_specs=[blk, blk, row, pl.BlockSpec((1, LANES), lambda i: (0, 0))],
        out_shape=[jax.ShapeDtypeStruct((T, D), F32), jax.ShapeDtypeStruct((T, D), BF16),
                   jax.ShapeDtypeStruct((1, D), F32), jax.ShapeDtypeStruct((1, LANES), F32)],
        compiler_params=_cp(("arbitrary",)),
    )(h, tgt, g)


def _mm_nn(a, b, name, out_dtype=F32, res=None, norm_g=None):
    M, K = a.shape
    b3 = b if b.ndim == 3 else b[None]
    nb, _, w = b3.shape
    N = nb * w
    tm = min(TM_MM, M, max(256, MM_BLOCK_BYTES // (4 * N)))
    o_spec = pl.BlockSpec((tm, N), lambda i: (i, 0))
    in_specs = [pl.BlockSpec((tm, K), lambda i: (i, 0)), pl.BlockSpec((nb, K, w), lambda i: (0, 0, 0))]
    args = [a, b3]
    if res is not None:
        in_specs.append(o_spec)
        args.append(res)
    if norm_g is not None:
        in_specs.append(pl.BlockSpec((1, N), lambda i: (0, 0)))
        args.append(norm_g)
    n_out = 2 if norm_g is not None else 1

    def body(*refs):
        a_ref, b_ref = refs[0], refs[1]
        o_ref = refs[-n_out]
        av = a_ref[...]
        for j in range(nb):
            cols = slice(j * w, (j + 1) * w)
            acc = jnp.dot(av, b_ref[j], preferred_element_type=F32)
            if res is not None:
                acc = refs[2][:, cols] + acc
            o_ref[:, cols] = acc.astype(out_dtype)
        if norm_g is not None:
            x = o_ref[...]
            r = lax.rsqrt(jnp.mean(x * x, axis=-1, keepdims=True) + RMS_EPS)
            refs[-1][...] = ((x * r) * refs[3][...]).astype(BF16)

    out_shape = jax.ShapeDtypeStruct((M, N), out_dtype)
    if norm_g is None:
        out_specs, out_shapes = o_spec, out_shape
    else:
        out_specs, out_shapes = [o_spec, o_spec], [out_shape, jax.ShapeDtypeStruct((M, N), BF16)]
    return pl.pallas_call(
        body, name=name, grid=(M // tm,),
        in_specs=in_specs, out_specs=out_specs, out_shape=out_shapes,
        compiler_params=_cp(("parallel",)),
    )(*args)


def _mm_nt(a_list, b, name, out_dtype=F32, add=None, after=None, norm_bwd=None):
    M, kw = a_list[0].shape
    tm = min(TM, M)
    na = len(a_list)
    blocked = b.ndim == 3
    N = b.shape[1] if blocked else b.shape[0]
    b_spec = pl.BlockSpec(b.shape, lambda i: (0,) * b.ndim)
    o_spec = pl.BlockSpec((tm, N), lambda i: (i, 0))
    row_spec = pl.BlockSpec((1, N), lambda i: (0, 0))
    in_specs = [pl.BlockSpec((tm, kw), lambda i: (i, 0)) for _ in a_list] + [b_spec]
    args = list(a_list) + [b]
    if add is not None:
        in_specs.append(o_spec)
        args.append(add)
    n_in = len(args)
    if norm_bwd is not None:
        in_specs += [o_spec, row_spec, o_spec]
        args += list(norm_bwd)
    if after is not None:
        in_specs.append(pl.BlockSpec(memory_space=pl.ANY))
        args.append(after)
    n_args = len(args)

    def body(*refs):
        a_refs = refs[:na]
        b_ref = refs[na]
        acc = refs[na + 1][...] if add is not None else None
        for s, a_ref in enumerate(a_refs):
            if blocked:
                w = b_ref.shape[2]
                per = kw // w
                parts = [_dot_nt(a_ref[:, jj * w:(jj + 1) * w], b_ref[s * per + jj]) for jj in range(per)]
            else:
                parts = [_dot_nt(a_ref[...], b_ref[:, s * kw:(s + 1) * kw])]
            for part in parts:
                acc = part if acc is None else acc + part
        if norm_bwd is None:
            refs[n_args][...] = acc.astype(out_dtype)
            return
        h_ref, g_ref, r_ref = refs[n_in:n_in + 3]
        o_ref, ob_ref, dg_ref = refs[n_args:n_args + 3]
        x = h_ref[...]
        r = lax.rsqrt(jnp.mean(x * x, axis=-1, keepdims=True) + RMS_EPS)
        dyg = acc * g_ref[...]
        dot = jnp.mean(dyg * x, axis=-1, keepdims=True)
        dh = r_ref[...] + (r * dyg - x * ((r * r * r) * dot))
        o_ref[...] = dh
        ob_ref[...] = dh.astype(BF16)
        part_g = jnp.sum(acc * (x * r), axis=0, keepdims=True)

        @pl.when(pl.program_id(0) == 0)
        def _():
            dg_ref[...] = part_g

        @pl.when(pl.program_id(0) != 0)
        def _():
            dg_ref[...] += part_g

    if norm_bwd is None:
        out_specs, out_shapes, sem = o_spec, jax.ShapeDtypeStruct((M, N), out_dtype), ("parallel",)
    else:
        out_specs = [o_spec, o_spec, row_spec]
        out_shapes = [jax.ShapeDtypeStruct((M, N), F32), jax.ShapeDtypeStruct((M, N), BF16),
                      jax.ShapeDtypeStruct((1, N), F32)]
        sem = ("arbitrary",)
    return pl.pallas_call(
        body, name=name, grid=(M // tm,),
        in_specs=in_specs, out_specs=out_specs, out_shape=out_shapes,
        compiler_params=_cp(sem),
    )(*args)


def _mm_tn(x, y, name, blocked_w=None, out_dtype=F32):
    T, Kx = x.shape
    N = y.shape[1]
    tt = min(TT, T)
    nt = T // tt
    tkx = min(Kx, max(LANES, MM_BLOCK_BYTES // (4 * N)))
    if blocked_w is not None:
        blk_shape, full_shape = (N // blocked_w, tkx, blocked_w), (N // blocked_w, Kx, blocked_w)
        o_spec = pl.BlockSpec(blk_shape, lambda i, t: (0, i, 0))
    else:
        blk_shape, full_shape = (tkx, N), (Kx, N)
        o_spec = pl.BlockSpec(blk_shape, lambda i, t: (i, 0))

    rk = min(tkx, TN_ROWS)

    def body(x_ref, y_ref, o_ref, acc_ref):
        t = pl.program_id(1)

        @pl.when(t == 0)
        def _():
            acc_ref[...] = jnp.zeros_like(acc_ref)

        for r in range(tkx // rk):
            rows = slice(r * rk, (r + 1) * rk)
            part = _dot_tn(x_ref[:, rows], y_ref[...])
            if blocked_w is None:
                acc_ref[rows, :] += part
            else:
                for j in range(N // blocked_w):
                    acc_ref[j, rows, :] += part[:, j * blocked_w:(j + 1) * blocked_w]

        @pl.when(t == nt - 1)
        def _():
            o_ref[...] = acc_ref[...].astype(out_dtype)

    return pl.pallas_call(
        body, name=name, grid=(Kx // tkx, nt),
        in_specs=[pl.BlockSpec((tt, tkx), lambda i, t: (t, i)),
                  pl.BlockSpec((tt, N), lambda i, t: (t, 0))],
        out_specs=o_spec, out_shape=jax.ShapeDtypeStruct(full_shape, out_dtype),
        scratch_shapes=[pltpu.VMEM(blk_shape, F32)],
        compiler_params=_cp(("parallel", "arbitrary")),
    )(x, y)


def _sgu_pieces(z, lng, lnb, wc, bs_ref):
    E = z.shape[1] // 2
    gd = E // GM_GROUPS
    zu, zv = z[:, :E], z[:, E:]
    u = _gelu(zu)
    v = _gelu(zv)
    mu = jnp.mean(v, axis=-1, keepdims=True)
    xc = v - mu
    rs = lax.rsqrt(jnp.mean(xc * xc, axis=-1, keepdims=True) + LN_EPS)
    xhat = xc * rs
    vln = xhat * lng + lnb
    s = []
    for g in range(GM_GROUPS):
        vg = vln[:, g * gd:(g + 1) * gd].astype(BF16)
        s.append(jnp.dot(wc[g], vg, preferred_element_type=F32) + bs_ref[g])
    return zu, zv, u, xhat, rs, vln, s


def _causal_ws(ws_ref):
    t = lax.broadcasted_iota(jnp.int32, (CHUNK, CHUNK), 0)
    s = lax.broadcasted_iota(jnp.int32, (CHUNK, CHUNK), 1)
    tri = t >= s
    return [jnp.where(tri, ws_ref[g], 0.0).astype(BF16) for g in range(GM_GROUPS)], tri


def _gm_in_fused(hn, w_in, lng, lnb, ws, bs, name):
    T, D = hn.shape
    nb, _, w = w_in.shape
    E2 = nb * w
    E = E2 // 2
    gd = E // GM_GROUPS
    tm = min(TM, T)

    def body(a_ref, w_ref, lng_ref, lnb_ref, ws_ref, bs_ref, z_ref, o_ref):
        av = a_ref[...]
        for j in range(nb):
            z_ref[:, j * w:(j + 1) * w] = jnp.dot(av, w_ref[j], preferred_element_type=F32)
        wc, _ = _causal_ws(ws_ref)
        for c in range(tm // CHUNK):
            rows = slice(c * CHUNK, (c + 1) * CHUNK)
            _, _, u, _, _, _, s = _sgu_pieces(z_ref[rows, :], lng_ref[...], lnb_ref[...], wc, bs_ref)
            for g in range(GM_GROUPS):
                cols = slice(g * gd, (g + 1) * gd)
                o_ref[rows, cols] = (u[:, cols] * s[g]).astype(BF16)

    full = lambda shape: pl.BlockSpec(shape, lambda i: (0,) * len(shape))
    return pl.pallas_call(
        body, name=name, grid=(T // tm,),
        in_specs=[pl.BlockSpec((tm, D), lambda i: (i, 0)), full((nb, D, w)), full((1, E)), full((1, E)),
                  full((GM_GROUPS, CHUNK, CHUNK)), full((GM_GROUPS, CHUNK, 1))],
        out_specs=[pl.BlockSpec((tm, E2), lambda i: (i, 0)), pl.BlockSpec((tm, E), lambda i: (i, 0))],
        out_shape=[jax.ShapeDtypeStruct((T, E2), F32), jax.ShapeDtypeStruct((T, E), BF16)],
        compiler_params=_cp(("parallel",)),
    )(hn, w_in, lng, lnb, ws, bs)


def _sgu_bwd(z, dh16, w_out, lng, lnb, ws, bs, name, after=None):
    T, E2 = z.shape
    D = dh16.shape[1]
    E = E2 // 2
    gd = E // GM_GROUPS
    tm = min(2 * CHUNK, T)
    nsteps = T // tm

    def body(z_ref, dh_ref, wo_ref, lng_ref, lnb_ref, ws_ref, bs_ref, *rest):
        dz_ref, dlng_ref, dlnb_ref, dws_ref, dbs_ref, dg_ref = rest[-6:]
        i = pl.program_id(0)

        @pl.when(i == 0)
        def _():
            dlng_ref[...] = jnp.zeros_like(dlng_ref)
            dlnb_ref[...] = jnp.zeros_like(dlnb_ref)
            dws_ref[...] = jnp.zeros_like(dws_ref)
            dbs_ref[...] = jnp.zeros_like(dbs_ref)

        dg_ref[...] = _dot_nt(dh_ref[...], wo_ref[...]).astype(BF16)
        wc, tri = _causal_ws(ws_ref)
        lng_v = lng_ref[...]
        for c in range(tm // CHUNK):
            rows = slice(c * CHUNK, (c + 1) * CHUNK)
            zu, zv, u, xhat, rs, vln, s = _sgu_pieces(z_ref[rows, :], lng_v, lnb_ref[...], wc, bs_ref)
            dgc = dg_ref[rows, :].astype(F32)
            du, dvln = [], []
            for g in range(GM_GROUPS):
                cols = slice(g * gd, (g + 1) * gd)
                dgg = dgc[:, cols]
                du.append(dgg * s[g])
                ds = dgg * u[:, cols]
                dsb = ds.astype(BF16)
                dws_ref[g] += _dot_nt(dsb, vln[:, cols].astype(BF16))
                dbs_ref[g] += jnp.sum(ds, axis=-1, keepdims=True)
                dvln.append(_dot_tn(wc[g], dsb))
            du = jnp.concatenate(du, axis=1)
            dvln = jnp.concatenate(dvln, axis=1)
            dlng_ref[...] += jnp.sum(dvln * xhat, axis=0, keepdims=True)
            dlnb_ref[...] += jnp.sum(dvln, axis=0, keepdims=True)
            dxh = dvln * lng_v
            m1 = jnp.mean(dxh, axis=-1, keepdims=True)
            m2 = jnp.mean(dxh * xhat, axis=-1, keepdims=True)
            dv = rs * (dxh - m1 - xhat * m2)
            dz_ref[rows, :E] = (du * _gelu_grad(zu)).astype(BF16)
            dz_ref[rows, E:] = (dv * _gelu_grad(zv)).astype(BF16)

        @pl.when(i == nsteps - 1)
        def _():
            for g in range(GM_GROUPS):
                dws_ref[g] = jnp.where(tri, dws_ref[g], 0.0)

    full = lambda shape: pl.BlockSpec(shape, lambda i: (0,) * len(shape))
    in_specs = [pl.BlockSpec((tm, E2), lambda i: (i, 0)), pl.BlockSpec((tm, D), lambda i: (i, 0)), full((E, D)),
                full((1, E)), full((1, E)), full((GM_GROUPS, CHUNK, CHUNK)), full((GM_GROUPS, CHUNK, 1))]
    args = [z, dh16, w_out, lng, lnb, ws, bs]
    if after is not None:
        in_specs.append(pl.BlockSpec(memory_space=pl.ANY))
        args.append(after)
    return pl.pallas_call(
        body, name=name, grid=(nsteps,),
        in_specs=in_specs,
        out_specs=[pl.BlockSpec((tm, E2), lambda i: (i, 0)), full((1, E)), full((1, E)),
                   full((GM_GROUPS, CHUNK, CHUNK)), full((GM_GROUPS, CHUNK, 1))],
        out_shape=[jax.ShapeDtypeStruct((T, E2), BF16), jax.ShapeDtypeStruct((1, E), F32),
                   jax.ShapeDtypeStruct((1, E), F32), jax.ShapeDtypeStruct((GM_GROUPS, CHUNK, CHUNK), F32),
                   jax.ShapeDtypeStruct((GM_GROUPS, CHUNK, 1), F32)],
        scratch_shapes=[pltpu.VMEM((tm, E), BF16)],
        compiler_params=_cp(("arbitrary",)),
    )(*args)


HALO = 16


def _conv_taps(a_ext, w_ref, b_ref):
    n = a_ext.shape[0]
    am1 = pltpu.roll(a_ext, 1, 0)
    am2 = pltpu.roll(a_ext, 2, 0)
    del n
    return ((b_ref[...] + am2 * w_ref[0:1, :]) + am1 * w_ref[1:2, :]) + a_ext * w_ref[2:3, :], am1, am2


def _ffn_up_fused(hn, wgu, cw, cb, name):
    T, D = hn.shape
    nb2, _, w = wgu.shape
    nb = nb2 // 2
    F = nb * w
    tm = min(TM, T)

    def body(a_ref, w_ref, cw_ref, cb_ref, au_ref, act_ref, halo_ref):
        @pl.when(pl.program_id(0) == 0)
        def _():
            halo_ref[...] = jnp.zeros_like(halo_ref)

        av = a_ref[...]
        for j in range(nb):
            cols = slice(j * w, (j + 1) * w)
            g = jnp.dot(av, w_ref[j], preferred_element_type=F32)
            u = jnp.dot(av, w_ref[nb + j], preferred_element_type=F32)
            au_ref[:, cols] = g
            au_ref[:, F + j * w:F + (j + 1) * w] = u
            ext = jnp.concatenate([halo_ref[:, cols], g], axis=0)
            am1 = pltpu.roll(ext, 1, 0)
            am2 = pltpu.roll(ext, 2, 0)
            conv = ((cb_ref[:, cols] + am2 * cw_ref[0:1, cols]) + am1 * cw_ref[1:2, cols]) + ext * cw_ref[2:3, cols]
            conv = conv[HALO:, :]
            act_ref[:, cols] = ((conv * _sigmoid(conv)) * u).astype(BF16)
            halo_ref[:, cols] = g[tm - HALO:, :]

    return pl.pallas_call(
        body, name=name, grid=(T // tm,),
        in_specs=[pl.BlockSpec((tm, D), lambda i: (i, 0)), pl.BlockSpec((nb2, D, w), lambda i: (0, 0, 0)),
                  pl.BlockSpec((3, F), lambda i: (0, 0)), pl.BlockSpec((1, F), lambda i: (0, 0))],
        out_specs=[pl.BlockSpec((tm, 2 * F), lambda i: (i, 0)), pl.BlockSpec((tm, F), lambda i: (i, 0))],
        out_shape=[jax.ShapeDtypeStruct((T, 2 * F), F32), jax.ShapeDtypeStruct((T, F), BF16)],
        scratch_shapes=[pltpu.VMEM((HALO, F), F32)],
        compiler_params=_cp(("arbitrary",)),
    )(hn, wgu, cw, cb)


def _ffn_mid_bwd(au, dh16, wd, cw, cb, name, after=None):
    T, F = au.shape[0], au.shape[1] // 2
    D = dh16.shape[1]
    tm, tf = min(TM, T), min(TF, F)
    hb = tm // HALO
    nt = T // tm
    nf = F // tf
    last_h = T // HALO - 1

    def body(a_ref, ap_ref, an_ref, u_ref, un_ref, dh_ref, dhn_ref, wd_ref, w_ref, b_ref, *rest):
        da_ref, du_ref, dcw_ref, dcb_ref = rest[-4:]
        i = pl.program_id(1)
        prev = jnp.where(i == 0, 0.0, ap_ref[...])
        a_main = a_ref[...]
        a_ext = jnp.concatenate([prev, a_main, an_ref[...]], axis=0)
        conv, am1, am2 = _conv_taps(a_ext, w_ref, b_ref)
        conv = conv[HALO:, :]
        sig = _sigmoid(conv)
        u_ext = jnp.concatenate([u_ref[...], un_ref[...]], axis=0)
        wd_f = wd_ref[...]
        d_ext = jnp.concatenate([_dot_nt(dh_ref[...], wd_f), _dot_nt(dhn_ref[...], wd_f)], axis=0)
        d_ext = d_ext.astype(BF16).astype(F32)
        n = tm + HALO
        row = lax.broadcasted_iota(jnp.int32, (n, 1), 0)
        live = jnp.logical_or(row < tm, i < nt - 1)
        dconv = jnp.where(live, d_ext * u_ext * (sig * (1.0 + conv * (1.0 - sig))), 0.0)
        du_ref[...] = (d_ext[:tm, :] * (conv[:tm, :] * sig[:tm, :])).astype(BF16)
        dp1 = pltpu.roll(dconv, n - 1, 0)[:tm, :]
        dp2 = pltpu.roll(dconv, n - 2, 0)[:tm, :]
        dc = dconv[:tm, :]
        da_ref[...] = ((dc * w_ref[2:3, :] + dp1 * w_ref[1:2, :]) + dp2 * w_ref[0:1, :]).astype(BF16)
        g2 = jnp.sum(dc * a_main, axis=0, keepdims=True)
        g1 = jnp.sum(dc * am1[HALO:HALO + tm, :], axis=0, keepdims=True)
        g0 = jnp.sum(dc * am2[HALO:HALO + tm, :], axis=0, keepdims=True)
        gb = jnp.sum(dc, axis=0, keepdims=True)

        @pl.when(i == 0)
        def _():
            dcw_ref[...] = jnp.zeros_like(dcw_ref)
            dcb_ref[...] = jnp.zeros_like(dcb_ref)

        dcw_ref[0:1, :] += g0
        dcw_ref[1:2, :] += g1
        dcw_ref[2:3, :] += g2
        dcb_ref[...] += gb

    main = pl.BlockSpec((tm, tf), lambda f, i: (i, f))
    prev = pl.BlockSpec((HALO, tf), lambda f, i: (jnp.maximum(i * hb - 1, 0), f))
    nxt = pl.BlockSpec((HALO, tf), lambda f, i: (jnp.minimum((i + 1) * hb, last_h), f))
    main_u = pl.BlockSpec((tm, tf), lambda f, i: (i, nf + f))
    nxt_u = pl.BlockSpec((HALO, tf), lambda f, i: (jnp.minimum((i + 1) * hb, last_h), nf + f))
    in_specs = [main, prev, nxt, main_u, nxt_u,
                pl.BlockSpec((tm, D), lambda f, i: (i, 0)),
                pl.BlockSpec((HALO, D), lambda f, i: (jnp.minimum((i + 1) * hb, last_h), 0)),
                pl.BlockSpec((tf, D), lambda f, i: (f, 0)),
                pl.BlockSpec((3, tf), lambda f, i: (0, f)), pl.BlockSpec((1, tf), lambda f, i: (0, f))]
    args = [au, au, au, au, au, dh16, dh16, wd, cw, cb]
    if after is not None:
        in_specs.append(pl.BlockSpec(memory_space=pl.ANY))
        args.append(after)
    return pl.pallas_call(
        body, name=name, grid=(nf, nt),
        in_specs=in_specs,
        out_specs=[main, main, pl.BlockSpec((3, tf), lambda f, i: (0, f)), pl.BlockSpec((1, tf), lambda f, i: (0, f))],
        out_shape=[jax.ShapeDtypeStruct((T, F), BF16), jax.ShapeDtypeStruct((T, F), BF16),
                   jax.ShapeDtypeStruct((3, F), F32), jax.ShapeDtypeStruct((1, F), F32)],
        compiler_params=_cp(("parallel", "arbitrary")),
    )(*args)


def _split3(x):
    hi = x.astype(BF16)
    r1 = x - hi.astype(F32)
    mid = r1.astype(BF16)
    lo = (r1 - mid.astype(F32)).astype(BF16)
    return hi, mid, lo


def _tri_ones(n, upper):
    r = lax.broadcasted_iota(jnp.int32, (n, n), 0)
    c = lax.broadcasted_iota(jnp.int32, (n, n), 1)
    return jnp.where((r <= c) if upper else (r >= c), 1.0, 0.0).astype(BF16)


def _gate_scan(f, bf, name):
    T = f.shape[0]
    tm = min(256, T)

    def body(f_ref, b_ref, cp_ref, sn_ref, carry_ref):
        i = pl.program_id(0)

        @pl.when(i == 0)
        def _():
            carry_ref[...] = jnp.zeros_like(carry_ref)

        x = f_ref[...] + b_ref[...]
        e = jnp.exp(-jnp.abs(x))
        logf = jnp.minimum(x, 0.0) - jnp.log(1.0 + e)
        sn_ref[...] = jnp.where(x >= 0.0, e / (1.0 + e), 1.0 / (1.0 + e))
        tri = _tri_ones(tm, upper=False)
        c = carry_ref[...]
        for piece in _split3(logf):
            c = c + jnp.dot(tri, piece, preferred_element_type=F32)
        carry_ref[...] += jnp.sum(logf, axis=0, keepdims=True)
        hi, mid, lo = _split3(c * LOG2E)
        cp_ref[:, 0:LANES] = hi
        cp_ref[:, LANES:2 * LANES] = mid
        cp_ref[:, 2 * LANES:3 * LANES] = lo

    return pl.pallas_call(
        body, name=name, grid=(T // tm,),
        in_specs=[pl.BlockSpec((tm, LANES), lambda i: (i, 0)), pl.BlockSpec((1, LANES), lambda i: (0, 0))],
        out_specs=[pl.BlockSpec((tm, 3 * LANES), lambda i: (i, 0)), pl.BlockSpec((tm, LANES), lambda i: (i, 0))],
        out_shape=[jax.ShapeDtypeStruct((T, 3 * LANES), BF16), jax.ShapeDtypeStruct((T, LANES), F32)],
        scratch_shapes=[pltpu.VMEM((1, LANES), F32)],
        compiler_params=_cp(("arbitrary",)),
    )(f, bf)


def _gate_scan_bwd(dcq, dck, sneg, name):
    T = dcq.shape[0]
    tm = min(256, T)
    n = T // tm

    def body(dcq_ref, dck_ref, sn_ref, df_ref, db_ref, carry_ref):
        i = pl.program_id(0)

        @pl.when(i == 0)
        def _():
            carry_ref[...] = jnp.zeros_like(carry_ref)
            db_ref[...] = jnp.zeros_like(db_ref)

        tri = _tri_ones(tm, upper=True)
        dcb = dcq_ref[...] - dck_ref[...]
        acc = carry_ref[...]
        for piece in _split3(dcb):
            acc = acc + jnp.dot(tri, piece, preferred_element_type=F32)
        carry_ref[...] += jnp.sum(dcb, axis=0, keepdims=True)
        df = acc * sn_ref[...]
        df_ref[...] = df.astype(BF16)
        db_ref[...] += jnp.sum(df, axis=0, keepdims=True)

    rev = pl.BlockSpec((tm, LANES), lambda i: (n - 1 - i, 0))
    return pl.pallas_call(
        body, name=name, grid=(n,),
        in_specs=[rev, rev, rev],
        out_specs=[rev, pl.BlockSpec((1, LANES), lambda i: (0, 0))],
        out_shape=[jax.ShapeDtypeStruct((T, LANES), BF16), jax.ShapeDtypeStruct((1, LANES), F32)],
        scratch_shapes=[pltpu.VMEM((1, LANES), F32)],
        compiler_params=_cp(("arbitrary",)),
    )(dcq, dck, sneg)


def _qk_proj(hn, w_pads, cp, sels, consts, scales, name):
    T, D = hn.shape
    H = w_pads[0].shape[1] // LANES
    tm = min(TM, T)

    def body(a_ref, cp_ref, wq_ref, wk_ref, sq_ref, sk_ref, cq_ref, ck_ref, qo_ref, ko_ref):
        a = a_ref[...]
        cpv = cp_ref[...]
        for w_ref, sel_ref, c_ref, o_ref, scale in ((wq_ref, sq_ref, cq_ref, qo_ref, scales[0]),
                                                    (wk_ref, sk_ref, ck_ref, ko_ref, scales[1])):
            for p in range(H // 2):
                acc = jnp.dot(a, w_ref[:, p * 2 * LANES:(p + 1) * 2 * LANES], preferred_element_type=F32)
                if scale != 1.0:
                    acc = acc * scale
                acc = acc + jnp.dot(cpv, sel_ref[p], preferred_element_type=F32) + c_ref[p]
                o_ref[2 * p] = acc[:, :LANES].astype(BF16)
                o_ref[2 * p + 1] = acc[:, LANES:].astype(BF16)

    whole = lambda t: pl.BlockSpec(t.shape, lambda i: (0,) * t.ndim)
    out = jax.ShapeDtypeStruct((H, T, LANES), BF16)
    o_spec = pl.BlockSpec((H, tm, LANES), lambda i: (0, i, 0))
    return pl.pallas_call(
        body, name=name, grid=(T // tm,),
        in_specs=[pl.BlockSpec((tm, D), lambda i: (i, 0)), pl.BlockSpec((tm, 3 * LANES), lambda i: (i, 0)),
                  whole(w_pads[0]), whole(w_pads[1]), whole(sels[0]), whole(sels[1]), whole(consts[0]), whole(consts[1])],
        out_specs=[o_spec, o_spec], out_shape=[out, out],
        compiler_params=_cp(("parallel",)),
    )(hn, cp, w_pads[0], w_pads[1], sels[0], sels[1], consts[0], consts[1])


def _lane_lo():
    return lax.broadcasted_iota(jnp.int32, (1, LANES), 1) < HEAD_DIM


def _attn_fwd_t(qp, kp, vt, name):
    H, T, _ = qp.shape
    tq = min(TQ, T)
    hd = HEAD_DIM
    ext = hd + 16

    def body(q_ref, k_ref, vt_ref, o_ref, o32_ref, lse_ref, m_sc, acc_sc):
        i = pl.program_id(1)
        m_sc[...] = jnp.full(m_sc.shape, NEG, F32)
        acc_sc[...] = jnp.zeros_like(acc_sc)
        q_t = [jnp.transpose(q_ref[h].astype(F32)).astype(BF16) for h in range(2)]
        ones_rows = jnp.where(lax.broadcasted_iota(jnp.int32, (16, tq), 0) == 0, 1.0, 0.0).astype(BF16)

        def steps(blocks):
            offs = [pl.multiple_of(j * tq, tq) for j, _ in blocks]
            s_all = [[jnp.dot(k_ref[h, pl.ds(off, tq), :], q_t[h], preferred_element_type=F32) for h in range(2)]
                     for off in offs]
            for (j, masked), off, s_blk in zip(blocks, offs, s_all):
                for h in range(2):
                    s = s_blk[h]
                    if masked:
                        kr = lax.broadcasted_iota(jnp.int32, (tq, tq), 0)
                        qc = lax.broadcasted_iota(jnp.int32, (tq, tq), 1)
                        s = jnp.where(qc >= kr, s, NEG)
                    m_prev = m_sc[h]
                    m_new = jnp.maximum(m_prev, jnp.max(s, axis=0, keepdims=True))
                    alpha = jnp.exp2(m_prev - m_new)
                    p16 = jnp.exp2(s - m_new).astype(BF16)
                    v_aug = jnp.concatenate([vt_ref[h * hd:(h + 1) * hd, pl.ds(off, tq)], ones_rows], axis=0)
                    pv = jnp.dot(v_aug, p16, preferred_element_type=F32)
                    acc_sc[h] = alpha * acc_sc[h] + pv
                    m_sc[h] = m_new

        def group_body(t, carry):
            steps([(KV_UNROLL * t + u, False) for u in range(KV_UNROLL)])
            return carry

        lax.fori_loop(0, i // KV_UNROLL, group_body, 0)
        for rem in range(KV_UNROLL):

            @pl.when(i % KV_UNROLL == rem)
            def _(rem=rem):
                steps([(i - rem + u, u == rem) for u in range(rem + 1)])

        o_t, lse_t = [], []
        for h in range(2):
            acc = acc_sc[h]
            l = acc[hd:hd + 1, :]
            o_t.append(acc[:hd, :] / l)
            lse_t.append(jnp.broadcast_to(m_sc[h] + jnp.log(l) * LOG2E, (hd, tq)))
        o = jnp.transpose(jnp.concatenate(o_t, axis=0))
        o_ref[...] = o.astype(BF16)
        o32_ref[...] = o
        lse_ref[...] = jnp.transpose(jnp.concatenate(lse_t, axis=0))

    oblk = pl.BlockSpec((tq, LANES), lambda p, i: (i, p))
    return pl.pallas_call(
        body, name=name, grid=(H // 2, T // tq),
        in_specs=[pl.BlockSpec((2, tq, LANES), lambda p, i: (p, i, 0)),
                  pl.BlockSpec((2, T, LANES), lambda p, i: (p, 0, 0)),
                  pl.BlockSpec((2 * hd, T), lambda p, i: (p, 0))],
        out_specs=[oblk, oblk, pl.BlockSpec((None, tq, LANES), lambda p, i: (p, i, 0))],
        out_shape=[jax.ShapeDtypeStruct((T, H * HEAD_DIM), BF16), jax.ShapeDtypeStruct((T, H * HEAD_DIM), F32),
                   jax.ShapeDtypeStruct((H // 2, T, LANES), F32)],
        scratch_shapes=[pltpu.VMEM((2, 1, tq), F32), pltpu.VMEM((2, ext, tq), F32)],
        compiler_params=_cp(("parallel", "arbitrary")),
    )(qp, kp, vt)


def _attn_bwd(qp, kp, v, o, do, lse, scale, name):
    H, T, _ = qp.shape
    tq = min(TQ, T)
    nq = T // tq
    nrep = tq // LANES

    def body(q_ref, k_ref, v_ref, o_ref, do_ref, lse_ref, dq_ref, dk_ref, dv_ref, dqe_ref, dke_ref, dk_sc, dv_sc, dq_sc):
        i = pl.program_id(1)

        @pl.when(i == 0)
        def _():
            dk_sc[...] = jnp.zeros_like(dk_sc)
            dv_sc[...] = jnp.zeros_like(dv_sc)

        dq_sc[...] = jnp.zeros_like(dq_sc)

        lo = _lane_lo()
        dob = do_ref[...]
        dof = dob.astype(F32)
        prod = dof * o_ref[...].astype(F32)
        lse2 = lse_ref[...]
        lse2_sw = pltpu.roll(lse2, HEAD_DIM, 1)
        zero = jnp.zeros_like(dob)
        do_h = [jnp.where(lo, dob, zero), jnp.where(lo, zero, dob)]
        rep = lambda col: jnp.broadcast_to(col, (tq, LANES))
        delta = [rep(jnp.sum(jnp.where(lo, prod, 0.0), axis=-1, keepdims=True)),
                 rep(jnp.sum(jnp.where(lo, 0.0, prod), axis=-1, keepdims=True))]
        lse_h = [jnp.where(lo, lse2, lse2_sw), jnp.where(lo, lse2_sw, lse2)]
        qs = [q_ref[0], q_ref[1]]
        tr16 = lambda a: jnp.transpose(a.astype(F32)).astype(BF16)
        q_t = [tr16(qs[0]), tr16(qs[1])]
        do_t = [tr16(do_h[0]), tr16(do_h[1])]

        def steps(blocks):
            offs = [pl.multiple_of(j * tq, tq) for j, _ in blocks]
            vblks = [v_ref[pl.ds(off, tq), :] for off in offs]
            kblks = [[k_ref[h, pl.ds(off, tq), :] for h in range(2)] for off in offs]
            s_all = [[_dot_nt(qs[h], kb[h]) for h in range(2)] for kb in kblks]
            dp_all = [[_dot_nt(do_h[h], vb) for h in range(2)] for vb in vblks]
            for b, ((j, masked), off) in enumerate(zip(blocks, offs)):
                dv_add = None
                for h in range(2):
                    kblk, s, dp = kblks[b][h], s_all[b][h], dp_all[b][h]
                    p16, ds16 = [], []
                    for c in range(nrep):
                        cols = slice(c * LANES, (c + 1) * LANES)
                        p = jnp.exp2(s[:, cols] - lse_h[h])
                        if masked:
                            r = lax.broadcasted_iota(jnp.int32, (tq, LANES), 0)
                            cc = lax.broadcasted_iota(jnp.int32, (tq, LANES), 1)
                            p = jnp.where(r >= cc + c * LANES, p, 0.0)
                        p16.append(p.astype(BF16))
                        ds16.append((p * (dp[:, cols] - delta[h])).astype(BF16))
                    p16 = jnp.concatenate(p16, axis=1)
                    dsb = jnp.concatenate(ds16, axis=1)
                    dq_sc[h] += jnp.dot(dsb, kblk, preferred_element_type=F32)
                    dk_sc[h, :, pl.ds(off, tq)] += jnp.dot(q_t[h], dsb, preferred_element_type=F32)
                    pv = jnp.dot(do_t[h], p16, preferred_element_type=F32)
                    dv_add = pv if dv_add is None else dv_add + pv
                dv_sc[:, pl.ds(off, tq)] += dv_add

        def group_body(t, carry):
            steps([(KV_UNROLL_BWD * t + u, False) for u in range(KV_UNROLL_BWD)])
            return carry

        lax.fori_loop(0, i // KV_UNROLL_BWD, group_body, 0)
        for rem in range(KV_UNROLL_BWD):

            @pl.when(i % KV_UNROLL_BWD == rem)
            def _(rem=rem):
                steps([(i - rem + u, u == rem) for u in range(rem + 1)])

        dq0, dq1 = dq_sc[0], dq_sc[1]
        dq_ref[...] = (jnp.where(lo, dq0, pltpu.roll(dq1, HEAD_DIM, 1)) * scale).astype(BF16)
        row8 = lax.broadcasted_iota(jnp.int32, (8, 1), 0)
        pick = lambda blk, r: jnp.sum(jnp.where(row8 == r, blk, 0.0), axis=0, keepdims=True)
        two_rows = lambda a, b: jnp.where(row8 == 0, a, jnp.where(row8 == 1, b, 0.0))
        gate_rows = slice(HEAD_DIM, HEAD_DIM + 8)
        dqe_ref[...] = two_rows(pick(jnp.transpose(dq0)[gate_rows, :], 0), pick(jnp.transpose(dq1)[gate_rows, :], 0))

        @pl.when(i == nq - 1)
        def _():
            dke_ref[...] = two_rows(pick(dk_sc[0, gate_rows, :], 3), pick(dk_sc[1, gate_rows, :], 3))
            for cb in range(nq):
                tok = slice(cb * tq, (cb + 1) * tq)
                dk0 = jnp.transpose(dk_sc[0, :, tok])
                dk1 = jnp.transpose(dk_sc[1, :, tok])
                dk_ref[tok, :] = (jnp.where(lo, dk0, pltpu.roll(dk1, HEAD_DIM, 1)) * LN2).astype(BF16)
                dv_ref[tok, :] = jnp.transpose(dv_sc[:, tok]).astype(BF16)

    qblk = pl.BlockSpec((tq, LANES), lambda p, i: (i, p))
    pair = pl.BlockSpec((T, LANES), lambda p, i: (0, p))
    tok16 = jax.ShapeDtypeStruct((T, H * HEAD_DIM), BF16)
    gate32 = jax.ShapeDtypeStruct((H // 2, 8, T), F32)
    return pl.pallas_call(
        body, name=name, grid=(H // 2, nq),
        in_specs=[pl.BlockSpec((2, tq, LANES), lambda p, i: (p, i, 0)),
                  pl.BlockSpec((2, T, LANES), lambda p, i: (p, 0, 0)),
                  pair, qblk, qblk,
                  pl.BlockSpec((None, tq, LANES), lambda p, i: (p, i, 0))],
        out_specs=[qblk, pair, pair, pl.BlockSpec((None, 8, tq), lambda p, i: (p, 0, i)),
                   pl.BlockSpec((None, 8, T), lambda p, i: (p, 0, 0))],
        out_shape=[tok16, tok16, tok16, gate32, gate32],
        scratch_shapes=[pltpu.VMEM((2, LANES, T), F32), pltpu.VMEM((LANES, T), F32),
                        pltpu.VMEM((2, tq, LANES), F32)],
        compiler_params=_cp(("parallel", "arbitrary")),
    )(qp, kp, v, o, do, lse)


def _mesh_pos():
    return lax.axis_index("x"), lax.axis_index("y"), lax.axis_index("c")


def _all_gather(arrs, name, groups=None):
    n = len(arrs)
    if groups is None:
        groups = [(a, 0) for a in range(n)]
    ng = 1 + max(g for g, _ in groups)
    per_group = [sum(1 for g, _ in groups if g == gi) for gi in range(ng)]
    first_of = [next(a for a in range(n) if groups[a][0] == gi) for gi in range(ng)]

    def body(*refs):
        ins, outs = refs[:n], refs[n:n + ng]
        send_sems, recv_sems, local_sems = refs[n + ng:]
        x, y, c = _mesh_pos()
        me, sib = (x, y, c), (x, y, 1 - c)
        chips = [(1 - x, y), (x, 1 - y), (1 - x, 1 - y)]

        def dst_of(a, px, py, pc):
            g, k = groups[a]
            return outs[g].at[N_DEV * k + 4 * px + 2 * py + pc]

        def copy(a, k, block, to, src=None):
            dst = dst_of(a, *block)
            return pltpu.make_async_remote_copy(
                src_ref=dst if src is None else src, dst_ref=dst,
                send_sem=send_sems.at[a, k], recv_sem=recv_sems.at[a, k], device_id=to, device_id_type=MESH)

        mine = [pltpu.make_async_copy(ins[a], dst_of(a, *me), local_sems.at[a]) for a in range(n)]
        for cp in mine:
            cp.start()
        first = []
        for a in range(n):
            first.append(copy(a, 0, me, sib, src=ins[a]))
            first += [copy(a, 1 + j, me, (*chip, c), src=ins[a]) for j, chip in enumerate(chips)]
        for cp in first:
            cp.start()
        passed = []
        for j, chip in enumerate(chips):
            for a in range(n):
                copy(a, 1 + j, (*chip, c), me).wait_recv()
                fwd = copy(a, 4 + j, (*chip, c), sib)
                fwd.start()
                passed.append(fwd)
        for a in range(n):
            copy(a, 0, sib, me).wait_recv()
            for j, chip in enumerate(chips):
                copy(a, 4 + j, (*chip, 1 - c), me).wait_recv()
        for cp in first + passed:
            cp.wait_send()
        for cp in mine:
            cp.wait()

    any_spec = pl.BlockSpec(memory_space=pl.ANY)
    return pl.pallas_call(
        body, name=name,
        in_specs=[any_spec] * n, out_specs=[any_spec] * ng,
        out_shape=[jax.ShapeDtypeStruct((N_DEV * per_group[gi],) + arrs[first_of[gi]].shape, arrs[first_of[gi]].dtype)
                   for gi in range(ng)],
        scratch_shapes=[pltpu.SemaphoreType.DMA((n, 7)), pltpu.SemaphoreType.DMA((n, 7)),
                        pltpu.SemaphoreType.DMA((n,))],
    )(*arrs)


HBM_SPEC = pl.BlockSpec(memory_space=pltpu.HBM)
SEM_SPEC = pl.BlockSpec(memory_space=pltpu.SEMAPHORE)
ANY_SPEC = pl.BlockSpec(memory_space=pl.ANY)
DATAFLOW_EFFECT = pltpu.SideEffectType.DATAFLOW_SIDE_EFFECTING


def _peers():
    x, y, c = _mesh_pos()
    flip = lambda v, b: 1 - v if b else v
    return [(flip(x, (k >> 2) & 1), flip(y, (k >> 1) & 1), flip(c, k & 1)) for k in range(1, N_DEV)]


def _slot(p):
    return 4 * p[0] + 2 * p[1] + p[2]


def _direct_copy(src_refs, land_refs, sems, a, k, p, land_of, dst_slot, src_slot):
    s = src_slot(a, p)
    return pltpu.make_async_remote_copy(
        src_ref=src_refs[a] if s is None else src_refs[a].at[s], dst_ref=land_refs[land_of[a]].at[dst_slot(a, k)],
        send_sem=sems[0].at[a * (N_DEV - 1) + k], recv_sem=sems[1].at[a * (N_DEV - 1) + k], device_id=p,
        device_id_type=MESH)


def _direct_start(srcs, lands, land_of, dst_slot, src_slot, after, name, collective_id):
    n, nl = len(srcs), len(lands)

    def body(*refs):
        src_refs, land_refs = refs[:n], refs[n:n + nl]
        sems = (refs[n + nl + 1], refs[n + nl + 2])
        token = refs[-1]
        peers = _peers()
        barrier = pltpu.get_barrier_semaphore()
        for p in peers:
            pl.semaphore_signal(barrier, inc=1, device_id=p, device_id_type=MESH)
        pl.semaphore_wait(barrier, N_DEV - 1)
        for a in range(n):
            for k, p in enumerate(peers):
                _direct_copy(src_refs, land_refs, sems, a, k, p, land_of, dst_slot, src_slot).start()
        token[...] = jnp.zeros_like(token)

    hbm = lambda t: pltpu.HBM(t.shape, t.dtype)
    sem_t = pltpu.SemaphoreType.DMA((n * (N_DEV - 1),))
    outs = pl.pallas_call(
        body, name=name,
        out_shape=(sem_t, sem_t, *[hbm(t) for t in srcs], *[hbm(t) for t in lands], jax.ShapeDtypeStruct((8, LANES), F32)),
        in_specs=[HBM_SPEC] * (n + nl) + [ANY_SPEC],
        out_specs=(SEM_SPEC, SEM_SPEC, *([HBM_SPEC] * (n + nl)), pl.BlockSpec(memory_space=pltpu.VMEM)),
        input_output_aliases={i: 2 + i for i in range(n + nl)},
        compiler_params=pltpu.CompilerParams(has_side_effects=DATAFLOW_EFFECT, collective_id=collective_id),
    )(*[pltpu.with_memory_space_constraint(t, pltpu.HBM) for t in srcs],
      *[pltpu.with_memory_space_constraint(t, pltpu.HBM) for t in lands], after)
    return outs[0], outs[1], list(outs[2:2 + n]), list(outs[2 + n:2 + n + nl]), outs[-1]


def _direct_wait(send_sems, recv_sems, srcs, lands, land_of, idxs, dst_slot, src_slot, after, name):
    land_ids = []
    for a in idxs:
        if land_of[a] not in land_ids:
            land_ids.append(land_of[a])
    m, ml = len(idxs), len(land_ids)
    sub_land_of = {j: land_ids.index(land_of[a]) for j, a in enumerate(idxs)}

    def body(*refs):
        src_refs, land_refs = refs[:m], refs[m:m + ml]
        ssem, rsem = refs[m + ml], refs[m + ml + 1]
        for j, a in enumerate(idxs):
            for k, p in enumerate(_peers()):
                s = src_slot(a, p)
                cp = pltpu.make_async_remote_copy(
                    src_ref=src_refs[j] if s is None else src_refs[j].at[s],
                    dst_ref=land_refs[sub_land_of[j]].at[dst_slot(a, k)],
                    send_sem=ssem.at[a * (N_DEV - 1) + k], recv_sem=rsem.at[a * (N_DEV - 1) + k], device_id=p,
                    device_id_type=MESH)
                cp.wait_send()
                cp.wait_recv()

    hbm = lambda t: pltpu.HBM(t.shape, t.dtype)
    sub_s, sub_l = [srcs[a] for a in idxs], [lands[g] for g in land_ids]
    outs = pl.pallas_call(
        body, name=name,
        out_shape=(*[hbm(t) for t in sub_s], *[hbm(t) for t in sub_l]),
        in_specs=[HBM_SPEC] * (m + ml) + [SEM_SPEC, SEM_SPEC, ANY_SPEC],
        out_specs=tuple([HBM_SPEC] * (m + ml)),
        input_output_aliases={i: i for i in range(m + ml)},
        compiler_params=pltpu.CompilerParams(has_side_effects=DATAFLOW_EFFECT),
    )(*sub_s, *sub_l, send_sems, recv_sems, after)
    return list(outs[:m]), list(outs[m:])


def _row_block(R, C):
    best = None
    for d in range(16, R + 1, 16):
        if R % d == 0 and d * C <= 256 * 1024:
            best = d
    return best if best is not None else R


def _adamw_math(w, g, m, v):
    m = ADAM_B1 * m + (1.0 - ADAM_B1) * g
    v = ADAM_B2 * v + (1.0 - ADAM_B2) * (g * g)
    m_hat = m / (1.0 - ADAM_B1 ** ADAM_STEP)
    v_hat = v / (1.0 - ADAM_B2 ** ADAM_STEP)
    delta = -ADAM_LR * (m_hat / (jnp.sqrt(v_hat) + ADAM_EPS) + ADAM_WD * w)
    return delta, m, v


def _sum_adamw(parts, w, m, v, name, sel=None):
    R, C = w.shape
    nseg = max(len(arr) if isinstance(arr, list) else 1 for arr, _ in parts)
    tr = _row_block(R // nseg, C)
    bps = R // nseg // tr
    specs, args = [], []
    for arr, idxs in parts:
        pieces = arr if isinstance(arr, list) else [arr] * nseg
        for idx in idxs:
            for sg in range(nseg if isinstance(arr, list) else 1):
                row = (lambda i, sg=sg: jnp.clip(i - sg * bps, 0, bps - 1)) if isinstance(arr, list) else (lambda i: i)
                if idx < 0:
                    specs.append(pl.BlockSpec((None, tr, C), lambda i, s, row=row: (s[0], row(i), 0)))
                else:
                    specs.append(pl.BlockSpec((None, tr, C), lambda i, s, idx=idx, row=row: (idx, row(i), 0)))
                args.append(pieces[sg])
    seg_counts = [(nseg if isinstance(arr, list) else 1) for arr, idxs in parts for _ in idxs]
    npart = len(args)
    blk = pl.BlockSpec((tr, C), lambda i, s: (i, 0))

    def body(s_ref, *refs):
        del s_ref
        seg = pl.program_id(0) // bps
        g, at = None, 0
        for cnt in seg_counts:
            term = refs[at][...].astype(F32)
            for sg in range(1, cnt):
                term = jnp.where(seg == sg, refs[at + sg][...].astype(F32), term)
            g = term if g is None else g + term
            at += cnt
        w_ref, m_ref, v_ref, g_out, d_out, m_out, v_out = refs[npart:]
        delta, mm, vv = _adamw_math(w_ref[...], g, m_ref[...], v_ref[...])
        g_out[...] = g
        d_out[...] = delta
        m_out[...] = mm
        v_out[...] = vv

    grid_spec = pltpu.PrefetchScalarGridSpec(
        num_scalar_prefetch=1, grid=(R // tr,),
        in_specs=specs + [blk, blk, blk], out_specs=[blk] * 4)
    if sel is None:
        sel = jnp.zeros((1,), jnp.int32)
    return pl.pallas_call(
        body, name=name, grid_spec=grid_spec,
        out_shape=[jax.ShapeDtypeStruct((R, C), F32)] * 4,
        compiler_params=_cp(("parallel",)),
    )(sel, *args, w, m, v)


def _rows(a, c):
    return a.reshape(-1, c)


def _pad_rows(a, r):
    return jnp.pad(a, ((0, r - a.shape[0]), (0, 0))) if a.shape[0] != r else a


def _gate_tables():
    hp = N_HEADS // 2
    sel_q = np.zeros((hp, 3 * LANES, 2 * LANES), np.float32)
    sel_k = np.zeros((hp, 3 * LANES, 2 * LANES), np.float32)
    const_q = np.zeros((hp, 1, 2 * LANES), np.float32)
    const_k = np.zeros((hp, 1, 2 * LANES), np.float32)
    for p in range(hp):
        for hh in range(2):
            h = 2 * p + hh
            base = hh * LANES + HEAD_DIM
            for piece in range(3):
                sel_q[p, piece * LANES + h, base + piece] = 1.0
                sel_k[p, piece * LANES + h, base + 3 + piece] = -1.0
            const_k[p, 0, base:base + 3] = 1.0
            const_q[p, 0, base + 3:base + 6] = 1.0
    as_bf = lambda t: jnp.asarray(t, BF16)
    return as_bf(sel_q), as_bf(sel_k), jnp.asarray(const_q), jnp.asarray(const_k)


def _pad_heads(w):
    d = w.shape[0]
    w3 = w.reshape(d, N_HEADS, HEAD_DIM)
    return jnp.pad(w3, ((0, 0), (0, 0), (0, LANES - HEAD_DIM))).reshape(d, N_HEADS * LANES)


def kernel(x, mix_norm_g, ffn_norm_g, gm_w_in, gm_ln_g, gm_ln_b, gm_w_s, gm_b_s, gm_w_out, fox_w_qkvf, fox_b_f, fox_w_o, ffn_w_gate, ffn_w_up, ffn_conv_w, ffn_conv_b, ffn_w_down, final_norm_g, loss_target, m_mix_norm_g, m_ffn_norm_g, m_gm_w_in, m_gm_ln_g, m_gm_ln_b, m_gm_w_s, m_gm_b_s, m_gm_w_out, m_fox_w_qkvf, m_fox_b_f, m_fox_w_o, m_ffn_w_gate, m_ffn_w_up, m_ffn_conv_w, m_ffn_conv_b, m_ffn_w_down, m_final_norm_g, v_mix_norm_g, v_ffn_norm_g, v_gm_w_in, v_gm_ln_g, v_gm_ln_b, v_gm_w_s, v_gm_b_s, v_gm_w_out, v_fox_w_qkvf, v_fox_b_f, v_fox_w_o, v_ffn_w_gate, v_ffn_w_up, v_ffn_conv_w, v_ffn_conv_b, v_ffn_w_down, v_final_norm_g):
    T, D = x.shape[1], x.shape[2]
    E = gm_ln_g.shape[1]
    FF = ffn_conv_b.shape[1]
    NQKVF = 3 * D + N_HEADS
    xi, yi, ci = _mesh_pos()
    me = 4 * xi + 2 * yi + ci
    h0 = x.reshape(T, D)
    tgt = loss_target.reshape(T, D)

    nl = ffn_w_gate.shape[0]
    to16 = lambda a: a.astype(BF16)
    n_cw_rows = ffn_conv_w.size // LANES
    cw_rows = _pad_rows(_rows(ffn_conv_w.astype(F32), LANES), 16)
    w_in_g, w_out_g8, cwg = _all_gather([to16(gm_w_in[0]), to16(gm_w_out[0]), cw_rows], "ag_weights")
    w_out_g = w_out_g8.reshape(E, D)
    later, land_of, land_off, lands = [], [], [], []
    for l in range(nl):
        later += [to16(ffn_w_gate[l]), to16(ffn_w_up[l]), to16(ffn_w_down[l])]
        land_of += [2 * l, 2 * l, 2 * l + 1]
        land_off += [0, N_DEV, 0]
        lands += [lax.empty((2 * N_DEV, D, FF // N_DEV), BF16), lax.empty((N_DEV, FF // N_DEV, D), BF16)]
    later += [to16(fox_w_qkvf[0]), to16(fox_w_o[0])]
    land_of += [2 * nl, 2 * nl + 1]
    land_off += [0, 0]
    lands += [lax.empty((N_DEV, D, NQKVF // N_DEV), BF16), lax.empty((N_DEV, D // N_DEV, D), BF16)]
    ag_dst = lambda a, k: land_off[a] + _slot(_mesh_pos())
    ag_src = lambda a, p: None
    ag_send, ag_recv, later, lands, ag_token = _direct_start(later, lands, land_of, ag_dst, ag_src, w_in_g,
                                                             "ag_later_start", collective_id=1)

    def own_blocks(landed, shards, offs):
        for s, o in zip(shards, offs):
            landed = lax.dynamic_update_index_in_dim(landed, s, o + me, 0)
        return landed

    def gather_wait(idxs, after, name):
        return _direct_wait(ag_send, ag_recv, later, lands, land_of, idxs, ag_dst, ag_src, after, name)

    conv_w_full = jnp.transpose(cwg[:, :n_cw_rows].reshape(N_DEV, nl, 3, FF // N_DEV), (1, 2, 0, 3)).reshape(nl, 3, FF)

    ffn_w = {}

    def ffn_weights(l):
        return ffn_w[l]

    def land_ffn(l, shards, gu_land, dn_land):
        ffn_w[l] = (own_blocks(gu_land, shards[:2], [0, N_DEV]), own_blocks(dn_land, shards[2:3], [0]).reshape(FF, D))

    saved = {}

    def ffn_fwd(l, h_in, hn, next_g):
        wgul, wdl = ffn_weights(l)
        au, act = _ffn_up_fused(hn, wgul, conv_w_full[l], ffn_conv_b[l:l + 1], f"ffn{l}_up")
        saved[f"ffn{l}"] = (h_in, hn, au, act)
        if next_g is None:
            return _mm_nn(act, wdl, f"ffn{l}_down", res=h_in), None
        return _mm_nn(act, wdl, f"ffn{l}_down", res=h_in, norm_g=next_g)

    bs_col = gm_b_s[0].reshape(GM_GROUPS, CHUNK, 1)
    hn0 = _rms_fwd(h0, mix_norm_g[0:1], "mix0_norm", after=ag_token)
    z, gu = _gm_in_fused(hn0, w_in_g, gm_ln_g, gm_ln_b, gm_w_s[0], bs_col, "gm_in")
    h1, hn_f0 = _mm_nn(gu, w_out_g, "gm_out", res=h0, norm_g=ffn_norm_g[0:1])
    mine0, land0 = gather_wait([0, 1, 2], h1, "ag_ffn0_wait")
    land_ffn(0, mine0, *land0)
    h2, hn2 = ffn_fwd(0, h1, hn_f0, mix_norm_g[1:2])

    mine1, rest = gather_wait(list(range(3, 3 * nl + 2)), h2, "ag_layer1_wait")
    for l in range(1, nl):
        land_ffn(l, mine1[3 * (l - 1):3 * l], rest[2 * (l - 1)], rest[2 * (l - 1) + 1])
    w_qkvf = jnp.transpose(own_blocks(rest[-2], mine1[-2:-1], [0]), (1, 0, 2)).reshape(D, NQKVF)
    w_o_g = own_blocks(rest[-1], mine1[-1:], [0]).reshape(D, D)
    w_q, w_k, w_v = w_qkvf[:, :D], w_qkvf[:, D:2 * D], w_qkvf[:, 2 * D:3 * D]
    w_f = jnp.pad(w_qkvf[:, 3 * D:], ((0, 0), (0, LANES - N_HEADS)))
    bf_row = jnp.pad(fox_b_f, ((0, 0), (0, LANES - N_HEADS)))
    sel_q, sel_k, const_q, const_k = _gate_tables()
    scale = HEAD_DIM ** -0.5
    f_logit = _mm_nn(hn2, w_f, "fox_f")
    cp, sneg = _gate_scan(f_logit, bf_row, "fox_scan")
    qp, kp = _qk_proj(hn2, (_pad_heads(w_q), _pad_heads(w_k)), cp, (sel_q, sel_k), (const_q, const_k),
                      (scale * LOG2E, 1.0), "fox_qk")
    vv = _mm_nn(hn2, w_v, "fox_v", out_dtype=BF16)
    o, o32, lse = _attn_fwd_t(qp, kp, jnp.transpose(vv), "fox_attn")
    h3, hn_f1 = _mm_nn(o, w_o_g, "fox_o", res=h2, norm_g=ffn_norm_g[1:2])
    h4, _ = ffn_fwd(1, h3, hn_f1, None)

    dh, dh16, d_final, loss_row = _loss_head(h4, tgt, final_norm_g.reshape(1, D), "loss_head")
    loss = lax.psum(loss_row[0, 0], ("x", "y", "c"))

    rs_dst = lambda a, k: k
    rs_src = lambda a, p: _slot(p)
    me_idx = me.astype(jnp.int32).reshape(1)

    def rs_start(grads, name, cid):
        lands = [lax.empty((N_DEV - 1,) + g.shape[1:], BF16) for g in grads]
        return _direct_start(grads, lands, list(range(len(grads))), rs_dst, rs_src, loss_row, name, collective_id=cid)

    def rs_wait(st, after, name):
        n = len(st[2])
        return _direct_wait(st[0], st[1], st[2], st[3], list(range(n)), list(range(n)), rs_dst, rs_src, after, name)

    def ffn_bwd(l, dh, dh16, after=None):
        wgul, wdl = ffn_weights(l)
        h_in, hn, au, act = saved[f"ffn{l}"]
        da, dup, d_cw, d_cb = _ffn_mid_bwd(au, dh16, wdl, conv_w_full[l], ffn_conv_b[l:l + 1], f"ffn{l}_dmid", after=after)
        d_wd = _mm_tn(act, dh16, f"ffn{l}_dwd", out_dtype=BF16)
        dh_in, dh_in16, d_norm = _mm_nt([da, dup], wgul, f"ffn{l}_dhn", norm_bwd=(h_in, ffn_norm_g[l:l + 1], dh))
        d_wg = _mm_tn(hn, da, f"ffn{l}_dwg", blocked_w=FF // N_DEV, out_dtype=BF16)
        d_wu = _mm_tn(hn, dup, f"ffn{l}_dwu", blocked_w=FF // N_DEV, out_dtype=BF16)
        big_g = [d_wg, d_wu, d_wd.reshape(N_DEV, FF // N_DEV, D)]
        return dh_in, dh_in16, big_g, dict(cw=d_cw, cb=d_cb, norm=d_norm)

    dh, dh16, big_ffn1, g_ffn1 = ffn_bwd(1, dh, dh16)

    do = _mm_nt([dh16], w_o_g, "fox_do", out_dtype=BF16)
    d_wo = _mm_tn(o, dh16, "fox_dwo", out_dtype=BF16)
    dq, dk, dv, dqe, dke = _attn_bwd(qp, kp, vv, o32, do, lse, scale, "fox_dattn")
    gate_lane = lambda e: jnp.pad(jnp.transpose(e[:, :2, :].reshape(N_HEADS, T)), ((0, 0), (0, LANES - N_HEADS)))
    df, d_bf = _gate_scan_bwd(gate_lane(dqe), gate_lane(dke), sneg, "fox_dscan")
    dhn = _mm_nt([df], w_f, "fox_dhn_f")
    dh_mix1 = _mm_nt([dq, dk, dv], w_qkvf[:, :3 * D], "fox_dhn_qkv", add=dhn, norm_bwd=(h2, mix_norm_g[1:2], dh))
    d_wq = _mm_tn(hn2, dq, "fox_dwq", out_dtype=BF16)
    d_wk = _mm_tn(hn2, dk, "fox_dwk", out_dtype=BF16)
    d_wv = _mm_tn(hn2, dv, "fox_dwv", out_dtype=BF16)
    d_wf = _mm_tn(hn2, df, "fox_dwf", out_dtype=BF16)
    d_wqkvf = jnp.concatenate([d_wq, d_wk, d_wv, d_wf[:, :N_HEADS]], axis=1)
    dh, dh16, d_mix1 = dh_mix1
    st1 = rs_start([jnp.transpose(d_wqkvf.reshape(D, N_DEV, NQKVF // N_DEV), (1, 0, 2)),
                    d_wo.reshape(N_DEV, D // N_DEV, D)] + big_ffn1, "rs1_start", 2)

    dh, dh16, big_ffn0, g_ffn0 = ffn_bwd(0, dh, dh16, after=st1[4])
    st2 = rs_start(big_ffn0, "rs2_start", 3)

    dz, d_lng, d_lnb, d_ws, d_bs = _sgu_bwd(z, dh16, w_out_g, gm_ln_g, gm_ln_b, gm_w_s[0], bs_col, "gm_dsgu", after=st2[4])
    d_wout = _mm_tn(gu, dh16, "gm_dwout", out_dtype=BF16)
    d_win = _mm_tn(hn0, dz, "gm_dwin", blocked_w=2 * E // N_DEV, out_dtype=BF16)
    st3 = rs_start([d_win, d_wout.reshape(N_DEV, E // N_DEV, D)], "rs3_start", 4)
    dx, _, d_mix0 = _mm_nt([dz], w_in_g, "gm_dhn", after=st3[4], norm_bwd=(h0, mix_norm_g[0:1], dh))

    own1, land1 = rs_wait(st1, dx, "rs1_wait")
    own2, land2 = rs_wait(st2, land1[0], "rs2_wait")
    big_out = {}

    def big_adamw(name, w, m, v, own, landed):
        shard2d = lambda a, c=(own[0] if isinstance(own, list) else own).shape[2]: a.reshape(-1, c)
        res = _sum_adamw([(own, [-1]), (landed, list(range(N_DEV - 1)))], shard2d(w), shard2d(m), shard2d(v),
                         f"adamw_{name}", sel=me_idx)
        big_out[name] = [t.reshape(w.shape) for t in res]

    big_adamw("fox_w_qkvf", fox_w_qkvf, m_fox_w_qkvf, v_fox_w_qkvf, own1[0], land1[0])
    big_adamw("fox_w_o", fox_w_o, m_fox_w_o, v_fox_w_o, own1[1], land1[1])
    big_adamw("ffn_w_gate", ffn_w_gate, m_ffn_w_gate, v_ffn_w_gate, [own2[0], own1[2]], [land2[0], land1[2]])
    big_adamw("ffn_w_up", ffn_w_up, m_ffn_w_up, v_ffn_w_up, [own2[1], own1[3]], [land2[1], land1[3]])
    big_adamw("ffn_w_down", ffn_w_down, m_ffn_w_down, v_ffn_w_down, [own2[2], own1[4]], [land2[2], land1[4]])

    small = [("mix_norm_g", mix_norm_g, m_mix_norm_g, v_mix_norm_g, jnp.concatenate([d_mix0, d_mix1], axis=0)),
             ("ffn_norm_g", ffn_norm_g, m_ffn_norm_g, v_ffn_norm_g, jnp.concatenate([g_ffn0["norm"], g_ffn1["norm"]], axis=0)),
             ("gm_ln_g", gm_ln_g, m_gm_ln_g, v_gm_ln_g, d_lng),
             ("gm_ln_b", gm_ln_b, m_gm_ln_b, v_gm_ln_b, d_lnb),
             ("gm_w_s", gm_w_s, m_gm_w_s, v_gm_w_s, d_ws),
             ("gm_b_s", gm_b_s, m_gm_b_s, v_gm_b_s, d_bs),
             ("fox_b_f", fox_b_f, m_fox_b_f, v_fox_b_f, d_bf[:, :N_HEADS]),
             ("ffn_conv_b", ffn_conv_b, m_ffn_conv_b, v_ffn_conv_b, jnp.concatenate([g_ffn0["cb"], g_ffn1["cb"]], axis=0)),
             ("final_norm_g", final_norm_g, m_final_norm_g, v_final_norm_g, d_final)]
    d_cw_full = jnp.stack([g_ffn0["cw"], g_ffn1["cw"]], axis=0)

    def small_rows(a):
        flat = a.astype(F32).reshape(-1)
        n = -(-flat.size // (8 * LANES)) * (8 * LANES)
        return jnp.pad(flat, (0, n - flat.size)).reshape(-1, LANES)

    s_rows = [small_rows(p[1]).shape[0] for p in small]
    s_off = np.concatenate([[0], np.cumsum(s_rows)]).tolist()
    cw_g_rows = small_rows(d_cw_full)
    zeros_cw = jnp.zeros_like(cw_g_rows)
    cat = lambda k: jnp.concatenate([small_rows(p[k]) for p in small] + [zeros_cw], axis=0)
    g_small = jnp.concatenate([small_rows(p[4]) for p in small] + [cw_g_rows], axis=0)
    (gs_all,) = _all_gather([g_small], "ag_small_grads")
    small_out = _sum_adamw([(gs_all, list(range(N_DEV)))], cat(1), cat(2), cat(3), "adamw_small")
    gs = small_out[0]

    g_cw_full = gs[s_off[-1]:].reshape(-1)[:d_cw_full.size].reshape(d_cw_full.shape)
    g_cw = lax.dynamic_slice_in_dim(g_cw_full, me * (FF // N_DEV), FF // N_DEV, axis=2)
    cw2 = lambda a: _pad_rows(_rows(a.astype(F32), LANES), 16)
    cw_out = _sum_adamw([(cw2(g_cw)[None], [0])], cw2(ffn_conv_w), cw2(m_ffn_conv_w), cw2(v_ffn_conv_w), "adamw_conv_w")

    own3, land3 = rs_wait(st3, cw_out[0], "rs3_wait")
    big_adamw("gm_w_in", gm_w_in, m_gm_w_in, v_gm_w_in, own3[0], land3[0])
    big_adamw("gm_w_out", gm_w_out, m_gm_w_out, v_gm_w_out, own3[1], land3[1])

    names = ["mix_norm_g", "ffn_norm_g", "gm_w_in", "gm_ln_g", "gm_ln_b", "gm_w_s", "gm_b_s", "gm_w_out", "fox_w_qkvf",
             "fox_b_f", "fox_w_o", "ffn_w_gate", "ffn_w_up", "ffn_conv_w", "ffn_conv_b", "ffn_w_down", "final_norm_g"]
    small_idx = {p[0]: k for k, p in enumerate(small)}

    def pick(kind, name):
        if name in big_out:
            return big_out[name][kind]
        if name == "ffn_conv_w":
            return cw_out[kind][:n_cw_rows].reshape(ffn_conv_w.shape)
        k = small_idx[name]
        shp = small[k][1].shape
        return small_out[kind][s_off[k]:s_off[k + 1]].reshape(-1)[:int(np.prod(shp))].reshape(shp)

    outs = [loss, dx.reshape(x.shape)]
    for kind in range(4):
        outs += [pick(kind, n) for n in names]
    return tuple(outs)
```

```python
import math

import numpy as np
import jax
import jax.numpy as jnp
from jax import lax
from jax.experimental import pallas as pl
from jax.experimental.pallas import tpu as pltpu

F32 = jnp.float32
BF16 = jnp.bfloat16
MESH = pl.DeviceIdType.MESH

N_HEADS = 16
HEAD_DIM = 64
CHUNK = 128
GM_GROUPS = 8
RMS_EPS = 1e-6
LN_EPS = 1e-5
ADAM_LR = 0.001
ADAM_B1 = 0.9
ADAM_B2 = 0.999
ADAM_EPS = 1e-08
ADAM_WD = 0.01
ADAM_STEP = 10
N_DEV = 8

LANES = 128
VMEM_BYTES_V7X = 64 * 1024 * 1024
VMEM_LIMIT = 56 * 1024 * 1024

TM = 512
TM_MM = 1024
TT = 1024
TQ = 512
TF = 512
KV_UNROLL_BWD = 2
KV_UNROLL = 2
TN_ROWS = 512
MM_BLOCK_BYTES = 8 * 1024 * 1024
NEG = -1e30
LOG2E = math.log2(math.e)
LN2 = math.log(2.0)


def _cp(sem=None, vmem=VMEM_LIMIT):
    return pltpu.CompilerParams(dimension_semantics=sem, vmem_limit_bytes=vmem)


def _gelu(x):
    c = math.sqrt(2.0 / math.pi)
    return x * (0.5 * (1.0 + jnp.tanh(c * (x + 0.044715 * (x * x * x)))))


def _gelu_grad(x):
    c = math.sqrt(2.0 / math.pi)
    t = jnp.tanh(c * (x + 0.044715 * (x * x * x)))
    return 0.5 * (1.0 + t) + x * (0.5 * (1.0 - t * t)) * (c * (1.0 + 3.0 * 0.044715 * (x * x)))


def _sigmoid(x):
    return 1.0 / (1.0 + jnp.exp(-x))


def _dot_nt(a, b):
    return lax.dot_general(a, b, (((1,), (1,)), ((), ())), preferred_element_type=F32)


def _dot_tn(a, b):
    return lax.dot_general(a, b, (((0,), (0,)), ((), ())), preferred_element_type=F32)


def _rms_fwd(h, g, name, after=None):
    T, D = h.shape
    tm = min(TM, T)

    def body(h_ref, g_ref, *rest):
        o_ref = rest[-1]
        x = h_ref[...]
        r = lax.rsqrt(jnp.mean(x * x, axis=-1, keepdims=True) + RMS_EPS)
        o_ref[...] = ((x * r) * g_ref[...]).astype(BF16)

    in_specs = [pl.BlockSpec((tm, D), lambda i: (i, 0)), pl.BlockSpec((1, D), lambda i: (0, 0))]
    args = [h, g]
    if after is not None:
        in_specs.append(pl.BlockSpec(memory_space=pl.ANY))
        args.append(after)
    return pl.pallas_call(
        body, name=name, grid=(T // tm,),
        in_specs=in_specs,
        out_specs=pl.BlockSpec((tm, D), lambda i: (i, 0)),
        out_shape=jax.ShapeDtypeStruct((T, D), BF16),
        compiler_params=_cp(("parallel",)),
    )(*args)


def _loss_head(h, tgt, g, name):
    T, D = h.shape
    tm = min(TM, T)

    def body(h_ref, t_ref, g_ref, o_ref, ob_ref, dg_ref, l_ref):
        x = h_ref[...]
        gg = g_ref[...]
        r = lax.rsqrt(jnp.mean(x * x, axis=-1, keepdims=True) + RMS_EPS)
        xr = x * r
        e = xr * gg - t_ref[...]
        lpart = 0.5 * jnp.sum(jnp.mean(e * e, axis=-1, keepdims=True), axis=0, keepdims=True)
        dy = e * (1.0 / D)
        dyg = dy * gg
        dot = jnp.mean(dyg * x, axis=-1, keepdims=True)
        dh = r * dyg - x * ((r * r * r) * dot)
        o_ref[...] = dh
        ob_ref[...] = dh.astype(BF16)
        part = jnp.sum(dy * xr, axis=0, keepdims=True)
        lrow = jnp.broadcast_to(lpart, (1, LANES))

        @pl.when(pl.program_id(0) == 0)
        def _():
            dg_ref[...] = part
            l_ref[...] = lrow

        @pl.when(pl.program_id(0) != 0)
        def _():
            dg_ref[...] += part
            l_ref[...] += lrow

    blk = pl.BlockSpec((tm, D), lambda i: (i, 0))
    row = pl.BlockSpec((1, D), lambda i: (0, 0))
    return pl.pallas_call(
        body, name=name, grid=(T // tm,),
        in_specs=[blk, blk, row],
        out_specs=[blk, blk, row, pl.BlockSpec((1, LANES), lambda i: (0, 0))],
        out_shape=[jax.ShapeDtypeStruct((T, D), F32), jax.ShapeDtypeStruct((T, D), BF16),
                   jax.ShapeDtypeStruct((1, D), F32), jax.ShapeDtypeStruct((1, LANES), F32)],
        compiler_params=_cp(("arbitrary",)),
    )(h, tgt, g)


def _mm_nn(a, b, name, out_dtype=F32, res=None, norm_g=None):
    M, K = a.shape
    b3 = b if b.ndim == 3 else b[None]
    nb, _, w = b3.shape
    N = nb * w
    tm = min(TM_MM, M, max(256, MM_BLOCK_BYTES // (4 * N)))
    o_spec = pl.BlockSpec((tm, N), lambda i: (i, 0))
    in_specs = [pl.BlockSpec((tm, K), lambda i: (i, 0)), pl.BlockSpec((nb, K, w), lambda i: (0, 0, 0))]
    args = [a, b3]
    if res is not None:
        in_specs.append(o_spec)
        args.append(res)
    if norm_g is not None:
        in_specs.append(pl.BlockSpec((1, N), lambda i: (0, 0)))
        args.append(norm_g)
    n_out = 2 if norm_g is not None else 1

    def body(*refs):
        a_ref, b_ref = refs[0], refs[1]
        o_ref = refs[-n_out]
        av = a_ref[...]
        for j in range(nb):
            cols = slice(j * w, (j + 1) * w)
            acc = jnp.dot(av, b_ref[j], preferred_element_type=F32)
            if res is not None:
                acc = refs[2][:, cols] + acc
            o_ref[:, cols] = acc.astype(out_dtype)
        if norm_g is not None:
            x = o_ref[...]
            r = lax.rsqrt(jnp.mean(x * x, axis=-1, keepdims=True) + RMS_EPS)
            refs[-1][...] = ((x * r) * refs[3][...]).astype(BF16)

    out_shape = jax.ShapeDtypeStruct((M, N), out_dtype)
    if norm_g is None:
        out_specs, out_shapes = o_spec, out_shape
    else:
        out_specs, out_shapes = [o_spec, o_spec], [out_shape, jax.ShapeDtypeStruct((M, N), BF16)]
    return pl.pallas_call(
        body, name=name, grid=(M // tm,),
        in_specs=in_specs, out_specs=out_specs, out_shape=out_shapes,
        compiler_params=_cp(("parallel",)),
    )(*args)


def _mm_nt(a_list, b, name, out_dtype=F32, add=None, after=None, norm_bwd=None):
    M, kw = a_list[0].shape
    tm = min(TM, M)
    na = len(a_list)
    blocked = b.ndim == 3
    N = b.shape[1] if blocked else b.shape[0]
    b_spec = pl.BlockSpec(b.shape, lambda i: (0,) * b.ndim)
    o_spec = pl.BlockSpec((tm, N), lambda i: (i, 0))
    row_spec = pl.BlockSpec((1, N), lambda i: (0, 0))
    in_specs = [pl.BlockSpec((tm, kw), lambda i: (i, 0)) for _ in a_list] + [b_spec]
    args = list(a_list) + [b]
    if add is not None:
        in_specs.append(o_spec)
        args.append(add)
    n_in = len(args)
    if norm_bwd is not None:
        in_specs += [o_spec, row_spec, o_spec]
        args += list(norm_bwd)
    if after is not None:
        in_specs.append(pl.BlockSpec(memory_space=pl.ANY))
        args.append(after)
    n_args = len(args)

    def body(*refs):
        a_refs = refs[:na]
        b_ref = refs[na]
        acc = refs[na + 1][...] if add is not None else None
        for s, a_ref in enumerate(a_refs):
            if blocked:
                w = b_ref.shape[2]
                per = kw // w
                parts = [_dot_nt(a_ref[:, jj * w:(jj + 1) * w], b_ref[s * per + jj]) for jj in range(per)]
            else:
                parts = [_dot_nt(a_ref[...], b_ref[:, s * kw:(s + 1) * kw])]
            for part in parts:
                acc = part if acc is None else acc + part
        if norm_bwd is None:
            refs[n_args][...] = acc.astype(out_dtype)
            return
        h_ref, g_ref, r_ref = refs[n_in:n_in + 3]
        o_ref, ob_ref, dg_ref = refs[n_args:n_args + 3]
        x = h_ref[...]
        r = lax.rsqrt(jnp.mean(x * x, axis=-1, keepdims=True) + RMS_EPS)
        dyg = acc * g_ref[...]
        dot = jnp.mean(dyg * x, axis=-1, keepdims=True)
        dh = r_ref[...] + (r * dyg - x * ((r * r * r) * dot))
        o_ref[...] = dh
        ob_ref[...] = dh.astype(BF16)
        part_g = jnp.sum(acc * (x * r), axis=0, keepdims=True)

        @pl.when(pl.program_id(0) == 0)
        def _():
            dg_ref[...] = part_g

        @pl.when(pl.program_id(0) != 0)
        def _():
            dg_ref[...] += part_g

    if norm_bwd is None:
        out_specs, out_shapes, sem = o_spec, jax.ShapeDtypeStruct((M, N), out_dtype), ("parallel",)
    else:
        out_specs = [o_spec, o_spec, row_spec]
        out_shapes = [jax.ShapeDtypeStruct((M, N), F32), jax.ShapeDtypeStruct((M, N), BF16),
                      jax.ShapeDtypeStruct((1, N), F32)]
        sem = ("arbitrary",)
    return pl.pallas_call(
        body, name=name, grid=(M // tm,),
        in_specs=in_specs, out_specs=out_specs, out_shape=out_shapes,
        compiler_params=_cp(sem),
    )(*args)


def _mm_tn(x, y, name, blocked_w=None, out_dtype=F32):
    T, Kx = x.shape
    N = y.shape[1]
    tt = min(TT, T)
    nt = T // tt
    tkx = min(Kx, max(LANES, MM_BLOCK_BYTES // (4 * N)))
    if blocked_w is not None:
        blk_shape, full_shape = (N // blocked_w, tkx, blocked_w), (N // blocked_w, Kx, blocked_w)
        o_spec = pl.BlockSpec(blk_shape, lambda i, t: (0, i, 0))
    else:
        blk_shape, full_shape = (tkx, N), (Kx, N)
        o_spec = pl.BlockSpec(blk_shape, lambda i, t: (i, 0))

    rk = min(tkx, TN_ROWS)

    def body(x_ref, y_ref, o_ref, acc_ref):
        t = pl.program_id(1)

        @pl.when(t == 0)
        def _():
            acc_ref[...] = jnp.zeros_like(acc_ref)

        for r in range(tkx // rk):
            rows = slice(r * rk, (r + 1) * rk)
            part = _dot_tn(x_ref[:, rows], y_ref[...])
            if blocked_w is None:
                acc_ref[rows, :] += part
            else:
                for j in range(N // blocked_w):
                    acc_ref[j, rows, :] += part[:, j * blocked_w:(j + 1) * blocked_w]

        @pl.when(t == nt - 1)
        def _():
            o_ref[...] = acc_ref[...].astype(out_dtype)

    return pl.pallas_call(
        body, name=name, grid=(Kx // tkx, nt),
        in_specs=[pl.BlockSpec((tt, tkx), lambda i, t: (t, i)),
                  pl.BlockSpec((tt, N), lambda i, t: (t, 0))],
        out_specs=o_spec, out_shape=jax.ShapeDtypeStruct(full_shape, out_dtype),
        scratch_shapes=[pltpu.VMEM(blk_shape, F32)],
        compiler_params=_cp(("parallel", "arbitrary")),
    )(x, y)


def _sgu_pieces(z, lng, lnb, wc, bs_ref):
    E = z.shape[1] // 2
    gd = E // GM_GROUPS
    zu, zv = z[:, :E], z[:, E:]
    u = _gelu(zu)
    v = _gelu(zv)
    mu = jnp.mean(v, axis=-1, keepdims=True)
    xc = v - mu
    rs = lax.rsqrt(jnp.mean(xc * xc, axis=-1, keepdims=True) + LN_EPS)
    xhat = xc * rs
    vln = xhat * lng + lnb
    s = []
    for g in range(GM_GROUPS):
        vg = vln[:, g * gd:(g + 1) * gd].astype(BF16)
        s.append(jnp.dot(wc[g], vg, preferred_element_type=F32) + bs_ref[g])
    return zu, zv, u, xhat, rs, vln, s


def _causal_ws(ws_ref):
    t = lax.broadcasted_iota(jnp.int32, (CHUNK, CHUNK), 0)
    s = lax.broadcasted_iota(jnp.int32, (CHUNK, CHUNK), 1)
    tri = t >= s
    return [jnp.where(tri, ws_ref[g], 0.0).astype(BF16) for g in range(GM_GROUPS)], tri


def _gm_in_fused(hn, w_in, lng, lnb, ws, bs, name):
    T, D = hn.shape
    nb, _, w = w_in.shape
    E2 = nb * w
    E = E2 // 2
    gd = E // GM_GROUPS
    tm = min(TM, T)

    def body(a_ref, w_ref, lng_ref, lnb_ref, ws_ref, bs_ref, z_ref, o_ref):
        av = a_ref[...]
        for j in range(nb):
            z_ref[:, j * w:(j + 1) * w] = jnp.dot(av, w_ref[j], preferred_element_type=F32)
        wc, _ = _causal_ws(ws_ref)
        for c in range(tm // CHUNK):
            rows = slice(c * CHUNK, (c + 1) * CHUNK)
            _, _, u, _, _, _, s = _sgu_pieces(z_ref[rows, :], lng_ref[...], lnb_ref[...], wc, bs_ref)
            for g in range(GM_GROUPS):
                cols = slice(g * gd, (g + 1) * gd)
                o_ref[rows, cols] = (u[:, cols] * s[g]).astype(BF16)

    full = lambda shape: pl.BlockSpec(shape, lambda i: (0,) * len(shape))
    return pl.pallas_call(
        body, name=name, grid=(T // tm,),
        in_specs=[pl.BlockSpec((tm, D), lambda i: (i, 0)), full((nb, D, w)), full((1, E)), full((1, E)),
                  full((GM_GROUPS, CHUNK, CHUNK)), full((GM_GROUPS, CHUNK, 1))],
        out_specs=[pl.BlockSpec((tm, E2), lambda i: (i, 0)), pl.BlockSpec((tm, E), lambda i: (i, 0))],
        out_shape=[jax.ShapeDtypeStruct((T, E2), F32), jax.ShapeDtypeStruct((T, E), BF16)],
        compiler_params=_cp(("parallel",)),
    )(hn, w_in, lng, lnb, ws, bs)


def _sgu_bwd(z, dh16, w_out, lng, lnb, ws, bs, name, after=None):
    T, E2 = z.shape
    D = dh16.shape[1]
    E = E2 // 2
    gd = E // GM_GROUPS
    tm = min(2 * CHUNK, T)
    nsteps = T // tm

    def body(z_ref, dh_ref, wo_ref, lng_ref, lnb_ref, ws_ref, bs_ref, *rest):
        dz_ref, dlng_ref, dlnb_ref, dws_ref, dbs_ref, dg_ref = rest[-6:]
        i = pl.program_id(0)

        @pl.when(i == 0)
        def _():
            dlng_ref[...] = jnp.zeros_like(dlng_ref)
            dlnb_ref[...] = jnp.zeros_like(dlnb_ref)
            dws_ref[...] = jnp.zeros_like(dws_ref)
            dbs_ref[...] = jnp.zeros_like(dbs_ref)

        dg_ref[...] = _dot_nt(dh_ref[...], wo_ref[...]).astype(BF16)
        wc, tri = _causal_ws(ws_ref)
        lng_v = lng_ref[...]
        for c in range(tm // CHUNK):
            rows = slice(c * CHUNK, (c + 1) * CHUNK)
            zu, zv, u, xhat, rs, vln, s = _sgu_pieces(z_ref[rows, :], lng_v, lnb_ref[...], wc, bs_ref)
            dgc = dg_ref[rows, :].astype(F32)
            du, dvln = [], []
            for g in range(GM_GROUPS):
                cols = slice(g * gd, (g + 1) * gd)
                dgg = dgc[:, cols]
                du.append(dgg * s[g])
                ds = dgg * u[:, cols]
                dsb = ds.astype(BF16)
                dws_ref[g] += _dot_nt(dsb, vln[:, cols].astype(BF16))
                dbs_ref[g] += jnp.sum(ds, axis=-1, keepdims=True)
                dvln.append(_dot_tn(wc[g], dsb))
            du = jnp.concatenate(du, axis=1)
            dvln = jnp.concatenate(dvln, axis=1)
            dlng_ref[...] += jnp.sum(dvln * xhat, axis=0, keepdims=True)
            dlnb_ref[...] += jnp.sum(dvln, axis=0, keepdims=True)
            dxh = dvln * lng_v
            m1 = jnp.mean(dxh, axis=-1, keepdims=True)
            m2 = jnp.mean(dxh * xhat, axis=-1, keepdims=True)
            dv = rs * (dxh - m1 - xhat * m2)
            dz_ref[rows, :E] = (du * _gelu_grad(zu)).astype(BF16)
            dz_ref[rows, E:] = (dv * _gelu_grad(zv)).astype(BF16)

        @pl.when(i == nsteps - 1)
        def _():
            for g in range(GM_GROUPS):
                dws_ref[g] = jnp.where(tri, dws_ref[g], 0.0)

    full = lambda shape: pl.BlockSpec(shape, lambda i: (0,) * len(shape))
    in_specs = [pl.BlockSpec((tm, E2), lambda i: (i, 0)), pl.BlockSpec((tm, D), lambda i: (i, 0)), full((E, D)),
                full((1, E)), full((1, E)), full((GM_GROUPS, CHUNK, CHUNK)), full((GM_GROUPS, CHUNK, 1))]
    args = [z, dh16, w_out, lng, lnb, ws, bs]
    if after is not None:
        in_specs.append(pl.BlockSpec(memory_space=pl.ANY))
        args.append(after)
    return pl.pallas_call(
        body, name=name, grid=(nsteps,),
        in_specs=in_specs,
        out_specs=[pl.BlockSpec((tm, E2), lambda i: (i, 0)), full((1, E)), full((1, E)),
                   full((GM_GROUPS, CHUNK, CHUNK)), full((GM_GROUPS, CHUNK, 1))],
        out_shape=[jax.ShapeDtypeStruct((T, E2), BF16), jax.ShapeDtypeStruct((1, E), F32),
                   jax.ShapeDtypeStruct((1, E), F32), jax.ShapeDtypeStruct((GM_GROUPS, CHUNK, CHUNK), F32),
                   jax.ShapeDtypeStruct((GM_GROUPS, CHUNK, 1), F32)],
        scratch_shapes=[pltpu.VMEM((tm, E), BF16)],
        compiler_params=_cp(("arbitrary",)),
    )(*args)


HALO = 16


def _conv_taps(a_ext, w_ref, b_ref):
    n = a_ext.shape[0]
    am1 = pltpu.roll(a_ext, 1, 0)
    am2 = pltpu.roll(a_ext, 2, 0)
    del n
    return ((b_ref[...] + am2 * w_ref[0:1, :]) + am1 * w_ref[1:2, :]) + a_ext * w_ref[2:3, :], am1, am2


def _ffn_up_fused(hn, wgu, cw, cb, name):
    T, D = hn.shape
    nb2, _, w = wgu.shape
    nb = nb2 // 2
    F = nb * w
    tm = min(TM, T)

    def body(a_ref, w_ref, cw_ref, cb_ref, au_ref, act_ref, halo_ref):
        @pl.when(pl.program_id(0) == 0)
        def _():
            halo_ref[...] = jnp.zeros_like(halo_ref)

        av = a_ref[...]
        for j in range(nb):
            cols = slice(j * w, (j + 1) * w)
            g = jnp.dot(av, w_ref[j], preferred_element_type=F32)
            u = jnp.dot(av, w_ref[nb + j], preferred_element_type=F32)
            au_ref[:, cols] = g
            au_ref[:, F + j * w:F + (j + 1) * w] = u
            ext = jnp.concatenate([halo_ref[:, cols], g], axis=0)
            am1 = pltpu.roll(ext, 1, 0)
            am2 = pltpu.roll(ext, 2, 0)
            conv = ((cb_ref[:, cols] + am2 * cw_ref[0:1, cols]) + am1 * cw_ref[1:2, cols]) + ext * cw_ref[2:3, cols]
            conv = conv[HALO:, :]
            act_ref[:, cols] = ((conv * _sigmoid(conv)) * u).astype(BF16)
            halo_ref[:, cols] = g[tm - HALO:, :]

    return pl.pallas_call(
        body, name=name, grid=(T // tm,),
        in_specs=[pl.BlockSpec((tm, D), lambda i: (i, 0)), pl.BlockSpec((nb2, D, w), lambda i: (0, 0, 0)),
                  pl.BlockSpec((3, F), lambda i: (0, 0)), pl.BlockSpec((1, F), lambda i: (0, 0))],
        out_specs=[pl.BlockSpec((tm, 2 * F), lambda i: (i, 0)), pl.BlockSpec((tm, F), lambda i: (i, 0))],
        out_shape=[jax.ShapeDtypeStruct((T, 2 * F), F32), jax.ShapeDtypeStruct((T, F), BF16)],
        scratch_shapes=[pltpu.VMEM((HALO, F), F32)],
        compiler_params=_cp(("arbitrary",)),
    )(hn, wgu, cw, cb)


def _ffn_mid_bwd(au, dh16, wd, cw, cb, name, after=None):
    T, F = au.shape[0], au.shape[1] // 2
    D = dh16.shape[1]
    tm, tf = min(TM, T), min(TF, F)
    hb = tm // HALO
    nt = T // tm
    nf = F // tf
    last_h = T // HALO - 1

    def body(a_ref, ap_ref, an_ref, u_ref, un_ref, dh_ref, dhn_ref, wd_ref, w_ref, b_ref, *rest):
        da_ref, du_ref, dcw_ref, dcb_ref = rest[-4:]
        i = pl.program_id(1)
        prev = jnp.where(i == 0, 0.0, ap_ref[...])
        a_main = a_ref[...]
        a_ext = jnp.concatenate([prev, a_main, an_ref[...]], axis=0)
        conv, am1, am2 = _conv_taps(a_ext, w_ref, b_ref)
        conv = conv[HALO:, :]
        sig = _sigmoid(conv)
        u_ext = jnp.concatenate([u_ref[...], un_ref[...]], axis=0)
        wd_f = wd_ref[...]
        d_ext = jnp.concatenate([_dot_nt(dh_ref[...], wd_f), _dot_nt(dhn_ref[...], wd_f)], axis=0)
        d_ext = d_ext.astype(BF16).astype(F32)
        n = tm + HALO
        row = lax.broadcasted_iota(jnp.int32, (n, 1), 0)
        live = jnp.logical_or(row < tm, i < nt - 1)
        dconv = jnp.where(live, d_ext * u_ext * (sig * (1.0 + conv * (1.0 - sig))), 0.0)
        du_ref[...] = (d_ext[:tm, :] * (conv[:tm, :] * sig[:tm, :])).astype(BF16)
        dp1 = pltpu.roll(dconv, n - 1, 0)[:tm, :]
        dp2 = pltpu.roll(dconv, n - 2, 0)[:tm, :]
        dc = dconv[:tm, :]
        da_ref[...] = ((dc * w_ref[2:3, :] + dp1 * w_ref[1:2, :]) + dp2 * w_ref[0:1, :]).astype(BF16)
        g2 = jnp.sum(dc * a_main, axis=0, keepdims=True)
        g1 = jnp.sum(dc * am1[HALO:HALO + tm, :], axis=0, keepdims=True)
        g0 = jnp.sum(dc * am2[HALO:HALO + tm, :], axis=0, keepdims=True)
        gb = jnp.sum(dc, axis=0, keepdims=True)

        @pl.when(i == 0)
        def _():
            dcw_ref[...] = jnp.zeros_like(dcw_ref)
            dcb_ref[...] = jnp.zeros_like(dcb_ref)

        dcw_ref[0:1, :] += g0
        dcw_ref[1:2, :] += g1
        dcw_ref[2:3, :] += g2
        dcb_ref[...] += gb

    main = pl.BlockSpec((tm, tf), lambda f, i: (i, f))
    prev = pl.BlockSpec((HALO, tf), lambda f, i: (jnp.maximum(i * hb - 1, 0), f))
    nxt = pl.BlockSpec((HALO, tf), lambda f, i: (jnp.minimum((i + 1) * hb, last_h), f))
    main_u = pl.BlockSpec((tm, tf), lambda f, i: (i, nf + f))
    nxt_u = pl.BlockSpec((HALO, tf), lambda f, i: (jnp.minimum((i + 1) * hb, last_h), nf + f))
    in_specs = [main, prev, nxt, main_u, nxt_u,
                pl.BlockSpec((tm, D), lambda f, i: (i, 0)),
                pl.BlockSpec((HALO, D), lambda f, i: (jnp.minimum((i + 1) * hb, last_h), 0)),
                pl.BlockSpec((tf, D), lambda f, i: (f, 0)),
                pl.BlockSpec((3, tf), lambda f, i: (0, f)), pl.BlockSpec((1, tf), lambda f, i: (0, f))]
    args = [au, au, au, au, au, dh16, dh16, wd, cw, cb]
    if after is not None:
        in_specs.append(pl.BlockSpec(memory_space=pl.ANY))
        args.append(after)
    return pl.pallas_call(
        body, name=name, grid=(nf, nt),
        in_specs=in_specs,
        out_specs=[main, main, pl.BlockSpec((3, tf), lambda f, i: (0, f)), pl.BlockSpec((1, tf), lambda f, i: (0, f))],
        out_shape=[jax.ShapeDtypeStruct((T, F), BF16), jax.ShapeDtypeStruct((T, F), BF16),
                   jax.ShapeDtypeStruct((3, F), F32), jax.ShapeDtypeStruct((1, F), F32)],
        compiler_params=_cp(("parallel", "arbitrary")),
    )(*args)


def _split3(x):
    hi = x.astype(BF16)
    r1 = x - hi.astype(F32)
    mid = r1.astype(BF16)
    lo = (r1 - mid.astype(F32)).astype(BF16)
    return hi, mid, lo


def _tri_ones(n, upper):
    r = lax.broadcasted_iota(jnp.int32, (n, n), 0)
    c = lax.broadcasted_iota(jnp.int32, (n, n), 1)
    return jnp.where((r <= c) if upper else (r >= c), 1.0, 0.0).astype(BF16)


def _gate_scan(f, bf, name):
    T = f.shape[0]
    tm = min(256, T)

    def body(f_ref, b_ref, cp_ref, sn_ref, carry_ref):
        i = pl.program_id(0)

        @pl.when(i == 0)
        def _():
            carry_ref[...] = jnp.zeros_like(carry_ref)

        x = f_ref[...] + b_ref[...]
        e = jnp.exp(-jnp.abs(x))
        logf = jnp.minimum(x, 0.0) - jnp.log(1.0 + e)
        sn_ref[...] = jnp.where(x >= 0.0, e / (1.0 + e), 1.0 / (1.0 + e))
        tri = _tri_ones(tm, upper=False)
        c = carry_ref[...]
        for piece in _split3(logf):
            c = c + jnp.dot(tri, piece, preferred_element_type=F32)
        carry_ref[...] += jnp.sum(logf, axis=0, keepdims=True)
        hi, mid, lo = _split3(c * LOG2E)
        cp_ref[:, 0:LANES] = hi
        cp_ref[:, LANES:2 * LANES] = mid
        cp_ref[:, 2 * LANES:3 * LANES] = lo

    return pl.pallas_call(
        body, name=name, grid=(T // tm,),
        in_specs=[pl.BlockSpec((tm, LANES), lambda i: (i, 0)), pl.BlockSpec((1, LANES), lambda i: (0, 0))],
        out_specs=[pl.BlockSpec((tm, 3 * LANES), lambda i: (i, 0)), pl.BlockSpec((tm, LANES), lambda i: (i, 0))],
        out_shape=[jax.ShapeDtypeStruct((T, 3 * LANES), BF16), jax.ShapeDtypeStruct((T, LANES), F32)],
        scratch_shapes=[pltpu.VMEM((1, LANES), F32)],
        compiler_params=_cp(("arbitrary",)),
    )(f, bf)


def _gate_scan_bwd(dcq, dck, sneg, name):
    T = dcq.shape[0]
    tm = min(256, T)
    n = T // tm

    def body(dcq_ref, dck_ref, sn_ref, df_ref, db_ref, carry_ref):
        i = pl.program_id(0)

        @pl.when(i == 0)
        def _():
            carry_ref[...] = jnp.zeros_like(carry_ref)
            db_ref[...] = jnp.zeros_like(db_ref)

        tri = _tri_ones(tm, upper=True)
        dcb = dcq_ref[...] - dck_ref[...]
        acc = carry_ref[...]
        for piece in _split3(dcb):
            acc = acc + jnp.dot(tri, piece, preferred_element_type=F32)
        carry_ref[...] += jnp.sum(dcb, axis=0, keepdims=True)
        df = acc * sn_ref[...]
        df_ref[...] = df.astype(BF16)
        db_ref[...] += jnp.sum(df, axis=0, keepdims=True)

    rev = pl.BlockSpec((tm, LANES), lambda i: (n - 1 - i, 0))
    return pl.pallas_call(
        body, name=name, grid=(n,),
        in_specs=[rev, rev, rev],
        out_specs=[rev, pl.BlockSpec((1, LANES), lambda i: (0, 0))],
        out_shape=[jax.ShapeDtypeStruct((T, LANES), BF16), jax.ShapeDtypeStruct((1, LANES), F32)],
        scratch_shapes=[pltpu.VMEM((1, LANES), F32)],
        compiler_params=_cp(("arbitrary",)),
    )(dcq, dck, sneg)


def _qk_proj(hn, w_pads, cp, sels, consts, scales, name):
    T, D = hn.shape
    H = w_pads[0].shape[1] // LANES
    tm = min(TM, T)

    def body(a_ref, cp_ref, wq_ref, wk_ref, sq_ref, sk_ref, cq_ref, ck_ref, qo_ref, ko_ref):
        a = a_ref[...]
        cpv = cp_ref[...]
        for w_ref, sel_ref, c_ref, o_ref, scale in ((wq_ref, sq_ref, cq_ref, qo_ref, scales[0]),
                                                    (wk_ref, sk_ref, ck_ref, ko_ref, scales[1])):
            for p in range(H // 2):
                acc = jnp.dot(a, w_ref[:, p * 2 * LANES:(p + 1) * 2 * LANES], preferred_element_type=F32)
                if scale != 1.0:
                    acc = acc * scale
                acc = acc + jnp.dot(cpv, sel_ref[p], preferred_element_type=F32) + c_ref[p]
                o_ref[2 * p] = acc[:, :LANES].astype(BF16)
                o_ref[2 * p + 1] = acc[:, LANES:].astype(BF16)

    whole = lambda t: pl.BlockSpec(t.shape, lambda i: (0,) * t.ndim)
    out = jax.ShapeDtypeStruct((H, T, LANES), BF16)
    o_spec = pl.BlockSpec((H, tm, LANES), lambda i: (0, i, 0))
    return pl.pallas_call(
        body, name=name, grid=(T // tm,),
        in_specs=[pl.BlockSpec((tm, D), lambda i: (i, 0)), pl.BlockSpec((tm, 3 * LANES), lambda i: (i, 0)),
                  whole(w_pads[0]), whole(w_pads[1]), whole(sels[0]), whole(sels[1]), whole(consts[0]), whole(consts[1])],
        out_specs=[o_spec, o_spec], out_shape=[out, out],
        compiler_params=_cp(("parallel",)),
    )(hn, cp, w_pads[0], w_pads[1], sels[0], sels[1], consts[0], consts[1])


def _lane_lo():
    return lax.broadcasted_iota(jnp.int32, (1, LANES), 1) < HEAD_DIM


def _attn_fwd_t(qp, kp, vt, name):
    H, T, _ = qp.shape
    tq = min(TQ, T)
    hd = HEAD_DIM
    ext = hd + 16

    def body(q_ref, k_ref, vt_ref, o_ref, o32_ref, lse_ref, m_sc, acc_sc):
        i = pl.program_id(1)
        m_sc[...] = jnp.full(m_sc.shape, NEG, F32)
        acc_sc[...] = jnp.zeros_like(acc_sc)
        q_t = [jnp.transpose(q_ref[h].astype(F32)).astype(BF16) for h in range(2)]
        ones_rows = jnp.where(lax.broadcasted_iota(jnp.int32, (16, tq), 0) == 0, 1.0, 0.0).astype(BF16)

        def steps(blocks):
            offs = [pl.multiple_of(j * tq, tq) for j, _ in blocks]
            s_all = [[jnp.dot(k_ref[h, pl.ds(off, tq), :], q_t[h], preferred_element_type=F32) for h in range(2)]
                     for off in offs]
            for (j, masked), off, s_blk in zip(blocks, offs, s_all):
                for h in range(2):
                    s = s_blk[h]
                    if masked:
                        kr = lax.broadcasted_iota(jnp.int32, (tq, tq), 0)
                        qc = lax.broadcasted_iota(jnp.int32, (tq, tq), 1)
                        s = jnp.where(qc >= kr, s, NEG)
                    m_prev = m_sc[h]
                    m_new = jnp.maximum(m_prev, jnp.max(s, axis=0, keepdims=True))
                    alpha = jnp.exp2(m_prev - m_new)
                    p16 = jnp.exp2(s - m_new).astype(BF16)
                    v_aug = jnp.concatenate([vt_ref[h * hd:(h + 1) * hd, pl.ds(off, tq)], ones_rows], axis=0)
                    pv = jnp.dot(v_aug, p16, preferred_element_type=F32)
                    acc_sc[h] = alpha * acc_sc[h] + pv
                    m_sc[h] = m_new

        def group_body(t, carry):
            steps([(KV_UNROLL * t + u, False) for u in range(KV_UNROLL)])
            return carry

        lax.fori_loop(0, i // KV_UNROLL, group_body, 0)
        for rem in range(KV_UNROLL):

            @pl.when(i % KV_UNROLL == rem)
            def _(rem=rem):
                steps([(i - rem + u, u == rem) for u in range(rem + 1)])

        o_t, lse_t = [], []
        for h in range(2):
            acc = acc_sc[h]
            l = acc[hd:hd + 1, :]
            o_t.append(acc[:hd, :] / l)
            lse_t.append(jnp.broadcast_to(m_sc[h] + jnp.log(l) * LOG2E, (hd, tq)))
        o = jnp.transpose(jnp.concatenate(o_t, axis=0))
        o_ref[...] = o.astype(BF16)
        o32_ref[...] = o
        lse_ref[...] = jnp.transpose(jnp.concatenate(lse_t, axis=0))

    oblk = pl.BlockSpec((tq, LANES), lambda p, i: (i, p))
    return pl.pallas_call(
        body, name=name, grid=(H // 2, T // tq),
        in_specs=[pl.BlockSpec((2, tq, LANES), lambda p, i: (p, i, 0)),
                  pl.BlockSpec((2, T, LANES), lambda p, i: (p, 0, 0)),
                  pl.BlockSpec((2 * hd, T), lambda p, i: (p, 0))],
        out_specs=[oblk, oblk, pl.BlockSpec((None, tq, LANES), lambda p, i: (p, i, 0))],
        out_shape=[jax.ShapeDtypeStruct((T, H * HEAD_DIM), BF16), jax.ShapeDtypeStruct((T, H * HEAD_DIM), F32),
                   jax.ShapeDtypeStruct((H // 2, T, LANES), F32)],
        scratch_shapes=[pltpu.VMEM((2, 1, tq), F32), pltpu.VMEM((2, ext, tq), F32)],
        compiler_params=_cp(("parallel", "arbitrary")),
    )(qp, kp, vt)


def _attn_bwd(qp, kp, v, o, do, lse, scale, name):
    H, T, _ = qp.shape
    tq = min(TQ, T)
    nq = T // tq
    nrep = tq // LANES

    def body(q_ref, k_ref, v_ref, o_ref, do_ref, lse_ref, dq_ref, dk_ref, dv_ref, dqe_ref, dke_ref, dk_sc, dv_sc, dq_sc):
        i = pl.program_id(1)

        @pl.when(i == 0)
        def _():
            dk_sc[...] = jnp.zeros_like(dk_sc)
            dv_sc[...] = jnp.zeros_like(dv_sc)

        dq_sc[...] = jnp.zeros_like(dq_sc)

        lo = _lane_lo()
        dob = do_ref[...]
        dof = dob.astype(F32)
        prod = dof * o_ref[...].astype(F32)
        lse2 = lse_ref[...]
        lse2_sw = pltpu.roll(lse2, HEAD_DIM, 1)
        zero = jnp.zeros_like(dob)
        do_h = [jnp.where(lo, dob, zero), jnp.where(lo, zero, dob)]
        rep = lambda col: jnp.broadcast_to(col, (tq, LANES))
        delta = [rep(jnp.sum(jnp.where(lo, prod, 0.0), axis=-1, keepdims=True)),
                 rep(jnp.sum(jnp.where(lo, 0.0, prod), axis=-1, keepdims=True))]
        lse_h = [jnp.where(lo, lse2, lse2_sw), jnp.where(lo, lse2_sw, lse2)]
        qs = [q_ref[0], q_ref[1]]
        tr16 = lambda a: jnp.transpose(a.astype(F32)).astype(BF16)
        q_t = [tr16(qs[0]), tr16(qs[1])]
        do_t = [tr16(do_h[0]), tr16(do_h[1])]

        def steps(blocks):
            offs = [pl.multiple_of(j * tq, tq) for j, _ in blocks]
            vblks = [v_ref[pl.ds(off, tq), :] for off in offs]
            kblks = [[k_ref[h, pl.ds(off, tq), :] for h in range(2)] for off in offs]
            s_all = [[_dot_nt(qs[h], kb[h]) for h in range(2)] for kb in kblks]
            dp_all = [[_dot_nt(do_h[h], vb) for h in range(2)] for vb in vblks]
            for b, ((j, masked), off) in enumerate(zip(blocks, offs)):
                dv_add = None
                for h in range(2):
                    kblk, s, dp = kblks[b][h], s_all[b][h], dp_all[b][h]
                    p16, ds16 = [], []
                    for c in range(nrep):
                        cols = slice(c * LANES, (c + 1) * LANES)
                        p = jnp.exp2(s[:, cols] - lse_h[h])
                        if masked:
                            r = lax.broadcasted_iota(jnp.int32, (tq, LANES), 0)
                            cc = lax.broadcasted_iota(jnp.int32, (tq, LANES), 1)
                            p = jnp.where(r >= cc + c * LANES, p, 0.0)
                        p16.append(p.astype(BF16))
                        ds16.append((p * (dp[:, cols] - delta[h])).astype(BF16))
                    p16 = jnp.concatenate(p16, axis=1)
                    dsb = jnp.concatenate(ds16, axis=1)
                    dq_sc[h] += jnp.dot(dsb, kblk, preferred_element_type=F32)
                    dk_sc[h, :, pl.ds(off, tq)] += jnp.dot(q_t[h], dsb, preferred_element_type=F32)
                    pv = jnp.dot(do_t[h], p16, preferred_element_type=F32)
                    dv_add = pv if dv_add is None else dv_add + pv
                dv_sc[:, pl.ds(off, tq)] += dv_add

        def group_body(t, carry):
            steps([(KV_UNROLL_BWD * t + u, False) for u in range(KV_UNROLL_BWD)])
            return carry

        lax.fori_loop(0, i // KV_UNROLL_BWD, group_body, 0)
        for rem in range(KV_UNROLL_BWD):

            @pl.when(i % KV_UNROLL_BWD == rem)
            def _(rem=rem):
                steps([(i - rem + u, u == rem) for u in range(rem + 1)])

        dq0, dq1 = dq_sc[0], dq_sc[1]
        dq_ref[...] = (jnp.where(lo, dq0, pltpu.roll(dq1, HEAD_DIM, 1)) * scale).astype(BF16)
        row8 = lax.broadcasted_iota(jnp.int32, (8, 1), 0)
        pick = lambda blk, r: jnp.sum(jnp.where(row8 == r, blk, 0.0), axis=0, keepdims=True)
        two_rows = lambda a, b: jnp.where(row8 == 0, a, jnp.where(row8 == 1, b, 0.0))
        gate_rows = slice(HEAD_DIM, HEAD_DIM + 8)
        dqe_ref[...] = two_rows(pick(jnp.transpose(dq0)[gate_rows, :], 0), pick(jnp.transpose(dq1)[gate_rows, :], 0))

        @pl.when(i == nq - 1)
        def _():
            dke_ref[...] = two_rows(pick(dk_sc[0, gate_rows, :], 3), pick(dk_sc[1, gate_rows, :], 3))
            for cb in range(nq):
                tok = slice(cb * tq, (cb + 1) * tq)
                dk0 = jnp.transpose(dk_sc[0, :, tok])
                dk1 = jnp.transpose(dk_sc[1, :, tok])
                dk_ref[tok, :] = (jnp.where(lo, dk0, pltpu.roll(dk1, HEAD_DIM, 1)) * LN2).astype(BF16)
                dv_ref[tok, :] = jnp.transpose(dv_sc[:, tok]).astype(BF16)

    qblk = pl.BlockSpec((tq, LANES), lambda p, i: (i, p))
    pair = pl.BlockSpec((T, LANES), lambda p, i: (0, p))
    tok16 = jax.ShapeDtypeStruct((T, H * HEAD_DIM), BF16)
    gate32 = jax.ShapeDtypeStruct((H // 2, 8, T), F32)
    return pl.pallas_call(
        body, name=name, grid=(H // 2, nq),
        in_specs=[pl.BlockSpec((2, tq, LANES), lambda p, i: (p, i, 0)),
                  pl.BlockSpec((2, T, LANES), lambda p, i: (p, 0, 0)),
                  pair, qblk, qblk,
                  pl.BlockSpec((None, tq, LANES), lambda p, i: (p, i, 0))],
        out_specs=[qblk, pair, pair, pl.BlockSpec((None, 8, tq), lambda p, i: (p, 0, i)),
                   pl.BlockSpec((None, 8, T), lambda p, i: (p, 0, 0))],
        out_shape=[tok16, tok16, tok16, gate32, gate32],
        scratch_shapes=[pltpu.VMEM((2, LANES, T), F32), pltpu.VMEM((LANES, T), F32),
                        pltpu.VMEM((2, tq, LANES), F32)],
        compiler_params=_cp(("parallel", "arbitrary")),
    )(qp, kp, v, o, do, lse)


def _mesh_pos():
    return lax.axis_index("x"), lax.axis_index("y"), lax.axis_index("c")


def _all_gather(arrs, name, groups=None):
    n = len(arrs)
    if groups is None:
        groups = [(a, 0) for a in range(n)]
    ng = 1 + max(g for g, _ in groups)
    per_group = [sum(1 for g, _ in groups if g == gi) for gi in range(ng)]
    first_of = [next(a for a in range(n) if groups[a][0] == gi) for gi in range(ng)]

    def body(*refs):
        ins, outs = refs[:n], refs[n:n + ng]
        send_sems, recv_sems, local_sems = refs[n + ng:]
        x, y, c = _mesh_pos()
        me, sib = (x, y, c), (x, y, 1 - c)
        chips = [(1 - x, y), (x, 1 - y), (1 - x, 1 - y)]

        def dst_of(a, px, py, pc):
            g, k = groups[a]
            return outs[g].at[N_DEV * k + 4 * px + 2 * py + pc]

        def copy(a, k, block, to, src=None):
            dst = dst_of(a, *block)
            return pltpu.make_async_remote_copy(
                src_ref=dst if src is None else src, dst_ref=dst,
                send_sem=send_sems.at[a, k], recv_sem=recv_sems.at[a, k], device_id=to, device_id_type=MESH)

        mine = [pltpu.make_async_copy(ins[a], dst_of(a, *me), local_sems.at[a]) for a in range(n)]
        for cp in mine:
            cp.start()
        first = []
        for a in range(n):
            first.append(copy(a, 0, me, sib, src=ins[a]))
            first += [copy(a, 1 + j, me, (*chip, c), src=ins[a]) for j, chip in enumerate(chips)]
        for cp in first:
            cp.start()
        passed = []
        for j, chip in enumerate(chips):
            for a in range(n):
                copy(a, 1 + j, (*chip, c), me).wait_recv()
                fwd = copy(a, 4 + j, (*chip, c), sib)
                fwd.start()
                passed.append(fwd)
        for a in range(n):
            copy(a, 0, sib, me).wait_recv()
            for j, chip in enumerate(chips):
                copy(a, 4 + j, (*chip, 1 - c), me).wait_recv()
        for cp in first + passed:
            cp.wait_send()
        for cp in mine:
            cp.wait()

    any_spec = pl.BlockSpec(memory_space=pl.ANY)
    return pl.pallas_call(
        body, name=name,
        in_specs=[any_spec] * n, out_specs=[any_spec] * ng,
        out_shape=[jax.ShapeDtypeStruct((N_DEV * per_group[gi],) + arrs[first_of[gi]].shape, arrs[first_of[gi]].dtype)
                   for gi in range(ng)],
        scratch_shapes=[pltpu.SemaphoreType.DMA((n, 7)), pltpu.SemaphoreType.DMA((n, 7)),
                        pltpu.SemaphoreType.DMA((n,))],
    )(*arrs)


HBM_SPEC = pl.BlockSpec(memory_space=pltpu.HBM)
SEM_SPEC = pl.BlockSpec(memory_space=pltpu.SEMAPHORE)
ANY_SPEC = pl.BlockSpec(memory_space=pl.ANY)
DATAFLOW_EFFECT = pltpu.SideEffectType.DATAFLOW_SIDE_EFFECTING


def _peers():
    x, y, c = _mesh_pos()
    flip = lambda v, b: 1 - v if b else v
    return [(flip(x, (k >> 2) & 1), flip(y, (k >> 1) & 1), flip(c, k & 1)) for k in range(1, N_DEV)]


def _slot(p):
    return 4 * p[0] + 2 * p[1] + p[2]


def _direct_copy(src_refs, land_refs, sems, a, k, p, land_of, dst_slot, src_slot):
    s = src_slot(a, p)
    return pltpu.make_async_remote_copy(
        src_ref=src_refs[a] if s is None else src_refs[a].at[s], dst_ref=land_refs[land_of[a]].at[dst_slot(a, k)],
        send_sem=sems[0].at[a * (N_DEV - 1) + k], recv_sem=sems[1].at[a * (N_DEV - 1) + k], device_id=p,
        device_id_type=MESH)


def _direct_start(srcs, lands, land_of, dst_slot, src_slot, after, name, collective_id):
    n, nl = len(srcs), len(lands)

    def body(*refs):
        src_refs, land_refs = refs[:n], refs[n:n + nl]
        sems = (refs[n + nl + 1], refs[n + nl + 2])
        token = refs[-1]
        peers = _peers()
        barrier = pltpu.get_barrier_semaphore()
        for p in peers:
            pl.semaphore_signal(barrier, inc=1, device_id=p, device_id_type=MESH)
        pl.semaphore_wait(barrier, N_DEV - 1)
        for a in range(n):
            for k, p in enumerate(peers):
                _direct_copy(src_refs, land_refs, sems, a, k, p, land_of, dst_slot, src_slot).start()
        token[...] = jnp.zeros_like(token)

    hbm = lambda t: pltpu.HBM(t.shape, t.dtype)
    sem_t = pltpu.SemaphoreType.DMA((n * (N_DEV - 1),))
    outs = pl.pallas_call(
        body, name=name,
        out_shape=(sem_t, sem_t, *[hbm(t) for t in srcs], *[hbm(t) for t in lands], jax.ShapeDtypeStruct((8, LANES), F32)),
        in_specs=[HBM_SPEC] * (n + nl) + [ANY_SPEC],
        out_specs=(SEM_SPEC, SEM_SPEC, *([HBM_SPEC] * (n + nl)), pl.BlockSpec(memory_space=pltpu.VMEM)),
        input_output_aliases={i: 2 + i for i in range(n + nl)},
        compiler_params=pltpu.CompilerParams(has_side_effects=DATAFLOW_EFFECT, collective_id=collective_id),
    )(*[pltpu.with_memory_space_constraint(t, pltpu.HBM) for t in srcs],
      *[pltpu.with_memory_space_constraint(t, pltpu.HBM) for t in lands], after)
    return outs[0], outs[1], list(outs[2:2 + n]), list(outs[2 + n:2 + n + nl]), outs[-1]


def _direct_wait(send_sems, recv_sems, srcs, lands, land_of, idxs, dst_slot, src_slot, after, name):
    land_ids = []
    for a in idxs:
        if land_of[a] not in land_ids:
            land_ids.append(land_of[a])
    m, ml = len(idxs), len(land_ids)
    sub_land_of = {j: land_ids.index(land_of[a]) for j, a in enumerate(idxs)}

    def body(*refs):
        src_refs, land_refs = refs[:m], refs[m:m + ml]
        ssem, rsem = refs[m + ml], refs[m + ml + 1]
        for j, a in enumerate(idxs):
            for k, p in enumerate(_peers()):
                s = src_slot(a, p)
                cp = pltpu.make_async_remote_copy(
                    src_ref=src_refs[j] if s is None else src_refs[j].at[s],
                    dst_ref=land_refs[sub_land_of[j]].at[dst_slot(a, k)],
                    send_sem=ssem.at[a * (N_DEV - 1) + k], recv_sem=rsem.at[a * (N_DEV - 1) + k], device_id=p,
                    device_id_type=MESH)
                cp.wait_send()
                cp.wait_recv()

    hbm = lambda t: pltpu.HBM(t.shape, t.dtype)
    sub_s, sub_l = [srcs[a] for a in idxs], [lands[g] for g in land_ids]
    outs = pl.pallas_call(
        body, name=name,
        out_shape=(*[hbm(t) for t in sub_s], *[hbm(t) for t in sub_l]),
        in_specs=[HBM_SPEC] * (m + ml) + [SEM_SPEC, SEM_SPEC, ANY_SPEC],
        out_specs=tuple([HBM_SPEC] * (m + ml)),
        input_output_aliases={i: i for i in range(m + ml)},
        compiler_params=pltpu.CompilerParams(has_side_effects=DATAFLOW_EFFECT),
    )(*sub_s, *sub_l, send_sems, recv_sems, after)
    return list(outs[:m]), list(outs[m:])


def _row_block(R, C):
    best = None
    for d in range(16, R + 1, 16):
        if R % d == 0 and d * C <= 256 * 1024:
            best = d
    return best if best is not None else R


def _adamw_math(w, g, m, v):
    m = ADAM_B1 * m + (1.0 - ADAM_B1) * g
    v = ADAM_B2 * v + (1.0 - ADAM_B2) * (g * g)
    m_hat = m / (1.0 - ADAM_B1 ** ADAM_STEP)
    v_hat = v / (1.0 - ADAM_B2 ** ADAM_STEP)
    delta = -ADAM_LR * (m_hat / (jnp.sqrt(v_hat) + ADAM_EPS) + ADAM_WD * w)
    return delta, m, v


def _sum_adamw(parts, w, m, v, name, sel=None):
    R, C = w.shape
    nseg = max(len(arr) if isinstance(arr, list) else 1 for arr, _ in parts)
    tr = _row_block(R // nseg, C)
    bps = R // nseg // tr
    specs, args = [], []
    for arr, idxs in parts:
        pieces = arr if isinstance(arr, list) else [arr] * nseg
        for idx in idxs:
            for sg in range(nseg if isinstance(arr, list) else 1):
                row = (lambda i, sg=sg: jnp.clip(i - sg * bps, 0, bps - 1)) if isinstance(arr, list) else (lambda i: i)
                if idx < 0:
                    specs.append(pl.BlockSpec((None, tr, C), lambda i, s, row=row: (s[0], row(i), 0)))
                else:
                    specs.append(pl.BlockSpec((None, tr, C), lambda i, s, idx=idx, row=row: (idx, row(i), 0)))
                args.append(pieces[sg])
    seg_counts = [(nseg if isinstance(arr, list) else 1) for arr, idxs in parts for _ in idxs]
    npart = len(args)
    blk = pl.BlockSpec((tr, C), lambda i, s: (i, 0))

    def body(s_ref, *refs):
        del s_ref
        seg = pl.program_id(0) // bps
        g, at = None, 0
        for cnt in seg_counts:
            term = refs[at][...].astype(F32)
            for sg in range(1, cnt):
                term = jnp.where(seg == sg, refs[at + sg][...].astype(F32), term)
            g = term if g is None else g + term
            at += cnt
        w_ref, m_ref, v_ref, g_out, d_out, m_out, v_out = refs[npart:]
        delta, mm, vv = _adamw_math(w_ref[...], g, m_ref[...], v_ref[...])
        g_out[...] = g
        d_out[...] = delta
        m_out[...] = mm
        v_out[...] = vv

    grid_spec = pltpu.PrefetchScalarGridSpec(
        num_scalar_prefetch=1, grid=(R // tr,),
        in_specs=specs + [blk, blk, blk], out_specs=[blk] * 4)
    if sel is None:
        sel = jnp.zeros((1,), jnp.int32)
    return pl.pallas_call(
        body, name=name, grid_spec=grid_spec,
        out_shape=[jax.ShapeDtypeStruct((R, C), F32)] * 4,
        compiler_params=_cp(("parallel",)),
    )(sel, *args, w, m, v)


def _rows(a, c):
    return a.reshape(-1, c)


def _pad_rows(a, r):
    return jnp.pad(a, ((0, r - a.shape[0]), (0, 0))) if a.shape[0] != r else a


def _gate_tables():
    hp = N_HEADS // 2
    sel_q = np.zeros((hp, 3 * LANES, 2 * LANES), np.float32)
    sel_k = np.zeros((hp, 3 * LANES, 2 * LANES), np.float32)
    const_q = np.zeros((hp, 1, 2 * LANES), np.float32)
    const_k = np.zeros((hp, 1, 2 * LANES), np.float32)
    for p in range(hp):
        for hh in range(2):
            h = 2 * p + hh
            base = hh * LANES + HEAD_DIM
            for piece in range(3):
                sel_q[p, piece * LANES + h, base + piece] = 1.0
                sel_k[p, piece * LANES + h, base + 3 + piece] = -1.0
            const_k[p, 0, base:base + 3] = 1.0
            const_q[p, 0, base + 3:base + 6] = 1.0
    as_bf = lambda t: jnp.asarray(t, BF16)
    return as_bf(sel_q), as_bf(sel_k), jnp.asarray(const_q), jnp.asarray(const_k)


def _pad_heads(w):
    d = w.shape[0]
    w3 = w.reshape(d, N_HEADS, HEAD_DIM)
    return jnp.pad(w3, ((0, 0), (0, 0), (0, LANES - HEAD_DIM))).reshape(d, N_HEADS * LANES)


def kernel(x, mix_norm_g, ffn_norm_g, gm_w_in, gm_ln_g, gm_ln_b, gm_w_s, gm_b_s, gm_w_out, fox_w_qkvf, fox_b_f, fox_w_o, ffn_w_gate, ffn_w_up, ffn_conv_w, ffn_conv_b, ffn_w_down, final_norm_g, loss_target, m_mix_norm_g, m_ffn_norm_g, m_gm_w_in, m_gm_ln_g, m_gm_ln_b, m_gm_w_s, m_gm_b_s, m_gm_w_out, m_fox_w_qkvf, m_fox_b_f, m_fox_w_o, m_ffn_w_gate, m_ffn_w_up, m_ffn_conv_w, m_ffn_conv_b, m_ffn_w_down, m_final_norm_g, v_mix_norm_g, v_ffn_norm_g, v_gm_w_in, v_gm_ln_g, v_gm_ln_b, v_gm_w_s, v_gm_b_s, v_gm_w_out, v_fox_w_qkvf, v_fox_b_f, v_fox_w_o, v_ffn_w_gate, v_ffn_w_up, v_ffn_conv_w, v_ffn_conv_b, v_ffn_w_down, v_final_norm_g):
    T, D = x.shape[1], x.shape[2]
    E = gm_ln_g.shape[1]
    FF = ffn_conv_b.shape[1]
    NQKVF = 3 * D + N_HEADS
    xi, yi, ci = _mesh_pos()
    me = 4 * xi + 2 * yi + ci
    h0 = x.reshape(T, D)
    tgt = loss_target.reshape(T, D)

    nl = ffn_w_gate.shape[0]
    to16 = lambda a: a.astype(BF16)
    n_cw_rows = ffn_conv_w.size // LANES
    cw_rows = _pad_rows(_rows(ffn_conv_w.astype(F32), LANES), 16)
    (w_in_g,) = _all_gather([to16(gm_w_in[0])], "ag_weights")
    later, land_of, land_off = [to16(gm_w_out[0]), cw_rows], [0, 1], [0, 0]
    lands = [lax.empty((N_DEV, E // N_DEV, D), BF16), lax.empty((N_DEV,) + cw_rows.shape, F32)]
    for l in range(nl):
        later += [to16(ffn_w_gate[l]), to16(ffn_w_up[l]), to16(ffn_w_down[l])]
        land_of += [2 + 2 * l, 2 + 2 * l, 3 + 2 * l]
        land_off += [0, N_DEV, 0]
        lands += [lax.empty((2 * N_DEV, D, FF // N_DEV), BF16), lax.empty((N_DEV, FF // N_DEV, D), BF16)]
    later += [to16(fox_w_qkvf[0]), to16(fox_w_o[0])]
    land_of += [2 + 2 * nl, 3 + 2 * nl]
    land_off += [0, 0]
    lands += [lax.empty((N_DEV, D, NQKVF // N_DEV), BF16), lax.empty((N_DEV, D // N_DEV, D), BF16)]
    ag_dst = lambda a, k: land_off[a] + _slot(_mesh_pos())
    ag_src = lambda a, p: None
    ag_send, ag_recv, later, lands, ag_token = _direct_start(later, lands, land_of, ag_dst, ag_src, w_in_g,
                                                             "ag_later_start", collective_id=1)

    def own_blocks(landed, shards, offs):
        for s, o in zip(shards, offs):
            landed = lax.dynamic_update_index_in_dim(landed, s, o + me, 0)
        return landed

    def gather_wait(idxs, after, name):
        return _direct_wait(ag_send, ag_recv, later, lands, land_of, idxs, ag_dst, ag_src, after, name)

    ffn_w = {}

    def ffn_weights(l):
        return ffn_w[l]

    def land_ffn(l, shards, gu_land, dn_land):
        ffn_w[l] = (own_blocks(gu_land, shards[:2], [0, N_DEV]), own_blocks(dn_land, shards[2:3], [0]).reshape(FF, D))

    saved = {}

    def ffn_fwd(l, h_in, hn, next_g):
        wgul, wdl = ffn_weights(l)
        au, act = _ffn_up_fused(hn, wgul, conv_w_full[l], ffn_conv_b[l:l + 1], f"ffn{l}_up")
        saved[f"ffn{l}"] = (h_in, hn, au, act)
        if next_g is None:
            return _mm_nn(act, wdl, f"ffn{l}_down", res=h_in), None
        return _mm_nn(act, wdl, f"ffn{l}_down", res=h_in, norm_g=next_g)

    bs_col = gm_b_s[0].reshape(GM_GROUPS, CHUNK, 1)
    hn0 = _rms_fwd(h0, mix_norm_g[0:1], "mix0_norm", after=ag_token)
    z, gu = _gm_in_fused(hn0, w_in_g, gm_ln_g, gm_ln_b, gm_w_s[0], bs_col, "gm_in")
    mine_o, land_o = gather_wait([0, 1], z, "ag_wout_wait")
    w_out_g = own_blocks(land_o[0], mine_o[0:1], [0]).reshape(E, D)
    cwg = own_blocks(land_o[1], mine_o[1:2], [0])
    conv_w_full = jnp.transpose(cwg[:, :n_cw_rows].reshape(N_DEV, nl, 3, FF // N_DEV), (1, 2, 0, 3)).reshape(nl, 3, FF)
    h1, hn_f0 = _mm_nn(gu, w_out_g, "gm_out", res=h0, norm_g=ffn_norm_g[0:1])
    mine0, land0 = gather_wait([2, 3, 4], h1, "ag_ffn0_wait")
    land_ffn(0, mine0, *land0)
    h2, hn2 = ffn_fwd(0, h1, hn_f0, mix_norm_g[1:2])

    mine1, rest = gather_wait(list(range(5, 3 * nl + 4)), h2, "ag_layer1_wait")
    for l in range(1, nl):
        land_ffn(l, mine1[3 * (l - 1):3 * l], rest[2 * (l - 1)], rest[2 * (l - 1) + 1])
    w_qkvf = jnp.transpose(own_blocks(rest[-2], mine1[-2:-1], [0]), (1, 0, 2)).reshape(D, NQKVF)
    w_o_g = own_blocks(rest[-1], mine1[-1:], [0]).reshape(D, D)
    w_q, w_k, w_v = w_qkvf[:, :D], w_qkvf[:, D:2 * D], w_qkvf[:, 2 * D:3 * D]
    w_f = jnp.pad(w_qkvf[:, 3 * D:], ((0, 0), (0, LANES - N_HEADS)))
    bf_row = jnp.pad(fox_b_f, ((0, 0), (0, LANES - N_HEADS)))
    sel_q, sel_k, const_q, const_k = _gate_tables()
    scale = HEAD_DIM ** -0.5
    f_logit = _mm_nn(hn2, w_f, "fox_f")
    cp, sneg = _gate_scan(f_logit, bf_row, "fox_scan")
    qp, kp = _qk_proj(hn2, (_pad_heads(w_q), _pad_heads(w_k)), cp, (sel_q, sel_k), (const_q, const_k),
                      (scale * LOG2E, 1.0), "fox_qk")
    vv = _mm_nn(hn2, w_v, "fox_v", out_dtype=BF16)
    o, o32, lse = _attn_fwd_t(qp, kp, jnp.transpose(vv), "fox_attn")
    h3, hn_f1 = _mm_nn(o, w_o_g, "fox_o", res=h2, norm_g=ffn_norm_g[1:2])
    h4, _ = ffn_fwd(1, h3, hn_f1, None)

    dh, dh16, d_final, loss_row = _loss_head(h4, tgt, final_norm_g.reshape(1, D), "loss_head")
    loss = lax.psum(loss_row[0, 0], ("x", "y", "c"))

    rs_dst = lambda a, k: k
    rs_src = lambda a, p: _slot(p)
    me_idx = me.astype(jnp.int32).reshape(1)

    def rs_start(grads, name, cid):
        lands = [lax.empty((N_DEV - 1,) + g.shape[1:], BF16) for g in grads]
        return _direct_start(grads, lands, list(range(len(grads))), rs_dst, rs_src, loss_row, name, collective_id=cid)

    def rs_wait(st, after, name):
        n = len(st[2])
        return _direct_wait(st[0], st[1], st[2], st[3], list(range(n)), list(range(n)), rs_dst, rs_src, after, name)

    def ffn_bwd(l, dh, dh16, after=None):
        wgul, wdl = ffn_weights(l)
        h_in, hn, au, act = saved[f"ffn{l}"]
        da, dup, d_cw, d_cb = _ffn_mid_bwd(au, dh16, wdl, conv_w_full[l], ffn_conv_b[l:l + 1], f"ffn{l}_dmid", after=after)
        d_wd = _mm_tn(act, dh16, f"ffn{l}_dwd", out_dtype=BF16)
        dh_in, dh_in16, d_norm = _mm_nt([da, dup], wgul, f"ffn{l}_dhn", norm_bwd=(h_in, ffn_norm_g[l:l + 1], dh))
        d_wg = _mm_tn(hn, da, f"ffn{l}_dwg", blocked_w=FF // N_DEV, out_dtype=BF16)
        d_wu = _mm_tn(hn, dup, f"ffn{l}_dwu", blocked_w=FF // N_DEV, out_dtype=BF16)
        big_g = [d_wg, d_wu, d_wd.reshape(N_DEV, FF // N_DEV, D)]
        return dh_in, dh_in16, big_g, dict(cw=d_cw, cb=d_cb, norm=d_norm)

    dh, dh16, big_ffn1, g_ffn1 = ffn_bwd(1, dh, dh16)

    do = _mm_nt([dh16], w_o_g, "fox_do", out_dtype=BF16)
    d_wo = _mm_tn(o, dh16, "fox_dwo", out_dtype=BF16)
    dq, dk, dv, dqe, dke = _attn_bwd(qp, kp, vv, o32, do, lse, scale, "fox_dattn")
    gate_lane = lambda e: jnp.pad(jnp.transpose(e[:, :2, :].reshape(N_HEADS, T)), ((0, 0), (0, LANES - N_HEADS)))
    df, d_bf = _gate_scan_bwd(gate_lane(dqe), gate_lane(dke), sneg, "fox_dscan")
    dhn = _mm_nt([df], w_f, "fox_dhn_f")
    dh_mix1 = _mm_nt([dq, dk, dv], w_qkvf[:, :3 * D], "fox_dhn_qkv", add=dhn, norm_bwd=(h2, mix_norm_g[1:2], dh))
    d_wq = _mm_tn(hn2, dq, "fox_dwq", out_dtype=BF16)
    d_wk = _mm_tn(hn2, dk, "fox_dwk", out_dtype=BF16)
    d_wv = _mm_tn(hn2, dv, "fox_dwv", out_dtype=BF16)
    d_wf = _mm_tn(hn2, df, "fox_dwf", out_dtype=BF16)
    d_wqkvf = jnp.concatenate([d_wq, d_wk, d_wv, d_wf[:, :N_HEADS]], axis=1)
    dh, dh16, d_mix1 = dh_mix1
    st1 = rs_start([jnp.transpose(d_wqkvf.reshape(D, N_DEV, NQKVF // N_DEV), (1, 0, 2)),
                    d_wo.reshape(N_DEV, D // N_DEV, D)] + big_ffn1, "rs1_start", 2)

    dh, dh16, big_ffn0, g_ffn0 = ffn_bwd(0, dh, dh16, after=st1[4])
    st2 = rs_start(big_ffn0, "rs2_start", 3)

    dz, d_lng, d_lnb, d_ws, d_bs = _sgu_bwd(z, dh16, w_out_g, gm_ln_g, gm_ln_b, gm_w_s[0], bs_col, "gm_dsgu", after=st2[4])
    d_wout = _mm_tn(gu, dh16, "gm_dwout", out_dtype=BF16)
    d_win = _mm_tn(hn0, dz, "gm_dwin", blocked_w=2 * E // N_DEV, out_dtype=BF16)
    st3 = rs_start([d_win, d_wout.reshape(N_DEV, E // N_DEV, D)], "rs3_start", 4)
    dx, _, d_mix0 = _mm_nt([dz], w_in_g, "gm_dhn", after=st3[4], norm_bwd=(h0, mix_norm_g[0:1], dh))

    small = [("mix_norm_g", mix_norm_g, m_mix_norm_g, v_mix_norm_g, jnp.concatenate([d_mix0, d_mix1], axis=0)),
             ("ffn_norm_g", ffn_norm_g, m_ffn_norm_g, v_ffn_norm_g, jnp.concatenate([g_ffn0["norm"], g_ffn1["norm"]], axis=0)),
             ("gm_ln_g", gm_ln_g, m_gm_ln_g, v_gm_ln_g, d_lng),
             ("gm_ln_b", gm_ln_b, m_gm_ln_b, v_gm_ln_b, d_lnb),
             ("gm_w_s", gm_w_s, m_gm_w_s, v_gm_w_s, d_ws),
             ("gm_b_s", gm_b_s, m_gm_b_s, v_gm_b_s, d_bs),
             ("fox_b_f", fox_b_f, m_fox_b_f, v_fox_b_f, d_bf[:, :N_HEADS]),
             ("ffn_conv_b", ffn_conv_b, m_ffn_conv_b, v_ffn_conv_b, jnp.concatenate([g_ffn0["cb"], g_ffn1["cb"]], axis=0)),
             ("final_norm_g", final_norm_g, m_final_norm_g, v_final_norm_g, d_final)]
    d_cw_full = jnp.stack([g_ffn0["cw"], g_ffn1["cw"]], axis=0)

    def small_rows(a):
        flat = a.astype(F32).reshape(-1)
        n = -(-flat.size // (8 * LANES)) * (8 * LANES)
        return jnp.pad(flat, (0, n - flat.size)).reshape(-1, LANES)

    s_rows = [small_rows(p[1]).shape[0] for p in small]
    s_off = np.concatenate([[0], np.cumsum(s_rows)]).tolist()
    cw_g_rows = small_rows(d_cw_full)
    g_small = jnp.concatenate([small_rows(p[4]) for p in small] + [cw_g_rows], axis=0)
    sg_dst = lambda a, k: _slot(_mesh_pos())
    sg_src = lambda a, p: None
    sg = _direct_start([g_small], [lax.empty((N_DEV,) + g_small.shape, F32)], [0], sg_dst, sg_src, dx,
                       "sg_start", collective_id=5)

    own1, land1 = rs_wait(st1, sg[4], "rs1_wait")
    own2, land2 = rs_wait(st2, land1[0], "rs2_wait")
    big_out = {}

    def big_adamw(name, w, m, v, own, landed):
        shard2d = lambda a, c=(own[0] if isinstance(own, list) else own).shape[2]: a.reshape(-1, c)
        res = _sum_adamw([(own, [-1]), (landed, list(range(N_DEV - 1)))], shard2d(w), shard2d(m), shard2d(v),
                         f"adamw_{name}", sel=me_idx)
        big_out[name] = [t.reshape(w.shape) for t in res]

    big_adamw("fox_w_qkvf", fox_w_qkvf, m_fox_w_qkvf, v_fox_w_qkvf, own1[0], land1[0])
    big_adamw("fox_w_o", fox_w_o, m_fox_w_o, v_fox_w_o, own1[1], land1[1])
    big_adamw("ffn_w_gate", ffn_w_gate, m_ffn_w_gate, v_ffn_w_gate, [own2[0], own1[2]], [land2[0], land1[2]])
    big_adamw("ffn_w_up", ffn_w_up, m_ffn_w_up, v_ffn_w_up, [own2[1], own1[3]], [land2[1], land1[3]])
    big_adamw("ffn_w_down", ffn_w_down, m_ffn_w_down, v_ffn_w_down, [own2[2], own1[4]], [land2[2], land1[4]])

    (g_small_own,), (g_small_land,) = _direct_wait(sg[0], sg[1], sg[2], sg[3], [0], [0], sg_dst, sg_src,
                                                   big_out["ffn_w_down"][0], "sg_wait")
    gs_all = own_blocks(g_small_land, [g_small_own], [0])
    zeros_cw = jnp.zeros_like(cw_g_rows)
    cat = lambda k: jnp.concatenate([small_rows(p[k]) for p in small] + [zeros_cw], axis=0)
    small_out = _sum_adamw([(gs_all, list(range(N_DEV)))], cat(1), cat(2), cat(3), "adamw_small")
    gs = small_out[0]

    g_cw_full = gs[s_off[-1]:].reshape(-1)[:d_cw_full.size].reshape(d_cw_full.shape)
    g_cw = lax.dynamic_slice_in_dim(g_cw_full, me * (FF // N_DEV), FF // N_DEV, axis=2)
    cw2 = lambda a: _pad_rows(_rows(a.astype(F32), LANES), 16)
    cw_out = _sum_adamw([(cw2(g_cw)[None], [0])], cw2(ffn_conv_w), cw2(m_ffn_conv_w), cw2(v_ffn_conv_w), "adamw_conv_w")

    own3, land3 = rs_wait(st3, cw_out[0], "rs3_wait")
    big_adamw("gm_w_in", gm_w_in, m_gm_w_in, v_gm_w_in, own3[0], land3[0])
    big_adamw("gm_w_out", gm_w_out, m_gm_w_out, v_gm_w_out, own3[1], land3[1])

    names = ["mix_norm_g", "ffn_norm_g", "gm_w_in", "gm_ln_g", "gm_ln_b", "gm_w_s", "gm_b_s", "gm_w_out", "fox_w_qkvf",
             "fox_b_f", "fox_w_o", "ffn_w_gate", "ffn_w_up", "ffn_conv_w", "ffn_conv_b", "ffn_w_down", "final_norm_g"]
    small_idx = {p[0]: k for k, p in enumerate(small)}

    def pick(kind, name):
        if name in big_out:
            return big_out[name][kind]
        if name == "ffn_conv_w":
            return cw_out[kind][:n_cw_rows].reshape(ffn_conv_w.shape)
        k = small_idx[name]
        shp = small[k][1].shape
        return small_out[kind][s_off[k]:s_off[k + 1]].reshape(-1)[:int(np.prod(shp))].reshape(shp)

    outs = [loss, dx.reshape(x.shape)]
    for kind in range(4):
        outs += [pick(kind, n) for n in names]
    return tuple(outs)
```

```python
import math

import numpy as np
import jax
import jax.numpy as jnp
from jax import lax
from jax.experimental import pallas as pl
from jax.experimental.pallas import tpu as pltpu

F32 = jnp.float32
BF16 = jnp.bfloat16
MESH = pl.DeviceIdType.MESH

N_HEADS = 16
HEAD_DIM = 64
CHUNK = 128
GM_GROUPS = 8
RMS_EPS = 1e-6
LN_EPS = 1e-5
ADAM_LR = 0.001
ADAM_B1 = 0.9
ADAM_B2 = 0.999
ADAM_EPS = 1e-08
ADAM_WD = 0.01
ADAM_STEP = 10
N_DEV = 8

LANES = 128
VMEM_BYTES_V7X = 64 * 1024 * 1024
VMEM_LIMIT = 56 * 1024 * 1024

TM = 512
TM_MM = 1024
TT = 1024
TQ = 512
TF = 512
KV_UNROLL_BWD = 2
KV_UNROLL = 2
TN_ROWS = 512
MM_BLOCK_BYTES = 8 * 1024 * 1024
NEG = -1e30
LOG2E = math.log2(math.e)
LN2 = math.log(2.0)


def _cp(sem=None, vmem=VMEM_LIMIT):
    return pltpu.CompilerParams(dimension_semantics=sem, vmem_limit_bytes=vmem)


def _gelu(x):
    c = math.sqrt(2.0 / math.pi)
    return x * (0.5 * (1.0 + jnp.tanh(c * (x + 0.044715 * (x * x * x)))))


def _gelu_grad(x):
    c = math.sqrt(2.0 / math.pi)
    t = jnp.tanh(c * (x + 0.044715 * (x * x * x)))
    return 0.5 * (1.0 + t) + x * (0.5 * (1.0 - t * t)) * (c * (1.0 + 3.0 * 0.044715 * (x * x)))


def _sigmoid(x):
    return 1.0 / (1.0 + jnp.exp(-x))


def _dot_nt(a, b):
    return lax.dot_general(a, b, (((1,), (1,)), ((), ())), preferred_element_type=F32)


def _dot_tn(a, b):
    return lax.dot_general(a, b, (((0,), (0,)), ((), ())), preferred_element_type=F32)


def _rms_fwd(h, g, name, after=None):
    T, D = h.shape
    tm = min(TM, T)

    def body(h_ref, g_ref, *rest):
        o_ref = rest[-1]
        x = h_ref[...]
        r = lax.rsqrt(jnp.mean(x * x, axis=-1, keepdims=True) + RMS_EPS)
        o_ref[...] = ((x * r) * g_ref[...]).astype(BF16)

    in_specs = [pl.BlockSpec((tm, D), lambda i: (i, 0)), pl.BlockSpec((1, D), lambda i: (0, 0))]
    args = [h, g]
    if after is not None:
        in_specs.append(pl.BlockSpec(memory_space=pl.ANY))
        args.append(after)
    return pl.pallas_call(
        body, name=name, grid=(T // tm,),
        in_specs=in_specs,
        out_specs=pl.BlockSpec((tm, D), lambda i: (i, 0)),
        out_shape=jax.ShapeDtypeStruct((T, D), BF16),
        compiler_params=_cp(("parallel",)),
    )(*args)


def _loss_head(h, tgt, g, name):
    T, D = h.shape
    tm = min(TM, T)

    def body(h_ref, t_ref, g_ref, o_ref, ob_ref, dg_ref, l_ref):
        x = h_ref[...]
        gg = g_ref[...]
        r = lax.rsqrt(jnp.mean(x * x, axis=-1, keepdims=True) + RMS_EPS)
        xr = x * r
        e = xr * gg - t_ref[...]
        lpart = 0.5 * jnp.sum(jnp.mean(e * e, axis=-1, keepdims=True), axis=0, keepdims=True)
        dy = e * (1.0 / D)
        dyg = dy * gg
        dot = jnp.mean(dyg * x, axis=-1, keepdims=True)
        dh = r * dyg - x * ((r * r * r) * dot)
        o_ref[...] = dh
        ob_ref[...] = dh.astype(BF16)
        part = jnp.sum(dy * xr, axis=0, keepdims=True)
        lrow = jnp.broadcast_to(lpart, (1, LANES))

        @pl.when(pl.program_id(0) == 0)
        def _():
            dg_ref[...] = part
            l_ref[...] = lrow

        @pl.when(pl.program_id(0) != 0)
        def _():
            dg_ref[...] += part
            l_ref[...] += lrow

    blk = pl.BlockSpec((tm, D), lambda i: (i, 0))
    row = pl.BlockSpec((1, D), lambda i: (0, 0))
    return pl.pallas_call(
        body, name=name, grid=(T // tm,),
        in_specs=[blk, blk, row],
        out_specs=[blk, blk, row, pl.BlockSpec((1, LANES), lambda i: (0, 0))],
        out_shape=[jax.ShapeDtypeStruct((T, D), F32), jax.ShapeDtypeStruct((T, D), BF16),
                   jax.ShapeDtypeStruct((1, D), F32), jax.ShapeDtypeStruct((1, LANES), F32)],
        compiler_params=_cp(("arbitrary",)),
    )(h, tgt, g)


def _mm_nn(a, b, name, out_dtype=F32, res=None, norm_g=None):
    M, K = a.shape
    b3 = b if b.ndim == 3 else b[None]
    nb, _, w = b3.shape
    N = nb * w
    tm = min(TM_MM, M, max(256, MM_BLOCK_BYTES // (4 * N)))
    o_spec = pl.BlockSpec((tm, N), lambda i: (i, 0))
    in_specs = [pl.BlockSpec((tm, K), lambda i: (i, 0)), pl.BlockSpec((nb, K, w), lambda i: (0, 0, 0))]
    args = [a, b3]
    if res is not None:
        in_specs.append(o_spec)
        args.append(res)
    if norm_g is not None:
        in_specs.append(pl.BlockSpec((1, N), lambda i: (0, 0)))
        args.append(norm_g)
    n_out = 2 if norm_g is not None else 1

    def body(*refs):
        a_ref, b_ref = refs[0], refs[1]
        o_ref = refs[-n_out]
        av = a_ref[...]
        for j in range(nb):
            cols = slice(j * w, (j + 1) * w)
            acc = jnp.dot(av, b_ref[j], preferred_element_type=F32)
            if res is not None:
                acc = refs[2][:, cols] + acc
            o_ref[:, cols] = acc.astype(out_dtype)
        if norm_g is not None:
            x = o_ref[...]
            r = lax.rsqrt(jnp.mean(x * x, axis=-1, keepdims=True) + RMS_EPS)
            refs[-1][...] = ((x * r) * refs[3][...]).astype(BF16)

    out_shape = jax.ShapeDtypeStruct((M, N), out_dtype)
    if norm_g is None:
        out_specs, out_shapes = o_spec, out_shape
    else:
        out_specs, out_shapes = [o_spec, o_spec], [out_shape, jax.ShapeDtypeStruct((M, N), BF16)]
    return pl.pallas_call(
        body, name=name, grid=(M // tm,),
        in_specs=in_specs, out_specs=out_specs, out_shape=out_shapes,
        compiler_params=_cp(("parallel",)),
    )(*args)


def _mm_nt(a_list, b, name, out_dtype=F32, add=None, after=None, norm_bwd=None):
    M, kw = a_list[0].shape
    tm = min(TM, M)
    na = len(a_list)
    blocked = b.ndim == 3
    N = b.shape[1] if blocked else b.shape[0]
    b_spec = pl.BlockSpec(b.shape, lambda i: (0,) * b.ndim)
    o_spec = pl.BlockSpec((tm, N), lambda i: (i, 0))
    row_spec = pl.BlockSpec((1, N), lambda i: (0, 0))
    in_specs = [pl.BlockSpec((tm, kw), lambda i: (i, 0)) for _ in a_list] + [b_spec]
    args = list(a_list) + [b]
    if add is not None:
        in_specs.append(o_spec)
        args.append(add)
    n_in = len(args)
    if norm_bwd is not None:
        in_specs += [o_spec, row_spec, o_spec]
        args += list(norm_bwd)
    if after is not None:
        in_specs.append(pl.BlockSpec(memory_space=pl.ANY))
        args.append(after)
    n_args = len(args)

    def body(*refs):
        a_refs = refs[:na]
        b_ref = refs[na]
        acc = refs[na + 1][...] if add is not None else None
        for s, a_ref in enumerate(a_refs):
            if blocked:
                w = b_ref.shape[2]
                per = kw // w
                parts = [_dot_nt(a_ref[:, jj * w:(jj + 1) * w], b_ref[s * per + jj]) for jj in range(per)]
            else:
                parts = [_dot_nt(a_ref[...], b_ref[:, s * kw:(s + 1) * kw])]
            for part in parts:
                acc = part if acc is None else acc + part
        if norm_bwd is None:
            refs[n_args][...] = acc.astype(out_dtype)
            return
        h_ref, g_ref, r_ref = refs[n_in:n_in + 3]
        o_ref, ob_ref, dg_ref = refs[n_args:n_args + 3]
        x = h_ref[...]
        r = lax.rsqrt(jnp.mean(x * x, axis=-1, keepdims=True) + RMS_EPS)
        dyg = acc * g_ref[...]
        dot = jnp.mean(dyg * x, axis=-1, keepdims=True)
        dh = r_ref[...] + (r * dyg - x * ((r * r * r) * dot))
        o_ref[...] = dh
        ob_ref[...] = dh.astype(BF16)
        part_g = jnp.sum(acc * (x * r), axis=0, keepdims=True)

        @pl.when(pl.program_id(0) == 0)
        def _():
            dg_ref[...] = part_g

        @pl.when(pl.program_id(0) != 0)
        def _():
            dg_ref[...] += part_g

    if norm_bwd is None:
        out_specs, out_shapes, sem = o_spec, jax.ShapeDtypeStruct((M, N), out_dtype), ("parallel",)
    else:
        out_specs = [o_spec, o_spec, row_spec]
        out_shapes = [jax.ShapeDtypeStruct((M, N), F32), jax.ShapeDtypeStruct((M, N), BF16),
                      jax.ShapeDtypeStruct((1, N), F32)]
        sem = ("arbitrary",)
    return pl.pallas_call(
        body, name=name, grid=(M // tm,),
        in_specs=in_specs, out_specs=out_specs, out_shape=out_shapes,
        compiler_params=_cp(sem),
    )(*args)


def _mm_tn(x, y, name, blocked_w=None, out_dtype=F32):
    T, Kx = x.shape
    N = y.shape[1]
    tt = min(TT, T)
    nt = T // tt
    tkx = min(Kx, max(LANES, MM_BLOCK_BYTES // (4 * N)))
    if blocked_w is not None:
        blk_shape, full_shape = (N // blocked_w, tkx, blocked_w), (N // blocked_w, Kx, blocked_w)
        o_spec = pl.BlockSpec(blk_shape, lambda i, t: (0, i, 0))
    else:
        blk_shape, full_shape = (tkx, N), (Kx, N)
        o_spec = pl.BlockSpec(blk_shape, lambda i, t: (i, 0))

    rk = min(tkx, TN_ROWS)

    def body(x_ref, y_ref, o_ref, acc_ref):
        t = pl.program_id(1)

        @pl.when(t == 0)
        def _():
            acc_ref[...] = jnp.zeros_like(acc_ref)

        for r in range(tkx // rk):
            rows = slice(r * rk, (r + 1) * rk)
            part = _dot_tn(x_ref[:, rows], y_ref[...])
            if blocked_w is None:
                acc_ref[rows, :] += part
            else:
                for j in range(N // blocked_w):
                    acc_ref[j, rows, :] += part[:, j * blocked_w:(j + 1) * blocked_w]

        @pl.when(t == nt - 1)
        def _():
            o_ref[...] = acc_ref[...].astype(out_dtype)

    return pl.pallas_call(
        body, name=name, grid=(Kx // tkx, nt),
        in_specs=[pl.BlockSpec((tt, tkx), lambda i, t: (t, i)),
                  pl.BlockSpec((tt, N), lambda i, t: (t, 0))],
        out_specs=o_spec, out_shape=jax.ShapeDtypeStruct(full_shape, out_dtype),
        scratch_shapes=[pltpu.VMEM(blk_shape, F32)],
        compiler_params=_cp(("parallel", "arbitrary")),
    )(x, y)


def _sgu_pieces(z, lng, lnb, wc, bs_ref):
    E = z.shape[1] // 2
    gd = E // GM_GROUPS
    zu, zv = z[:, :E], z[:, E:]
    u = _gelu(zu)
    v = _gelu(zv)
    mu = jnp.mean(v, axis=-1, keepdims=True)
    xc = v - mu
    rs = lax.rsqrt(jnp.mean(xc * xc, axis=-1, keepdims=True) + LN_EPS)
    xhat = xc * rs
    vln = xhat * lng + lnb
    s = []
    for g in range(GM_GROUPS):
        vg = vln[:, g * gd:(g + 1) * gd].astype(BF16)
        s.append(jnp.dot(wc[g], vg, preferred_element_type=F32) + bs_ref[g])
    return zu, zv, u, xhat, rs, vln, s


def _causal_ws(ws_ref):
    t = lax.broadcasted_iota(jnp.int32, (CHUNK, CHUNK), 0)
    s = lax.broadcasted_iota(jnp.int32, (CHUNK, CHUNK), 1)
    tri = t >= s
    return [jnp.where(tri, ws_ref[g], 0.0).astype(BF16) for g in range(GM_GROUPS)], tri


def _gm_in_fused(hn, w_in, lng, lnb, ws, bs, name):
    T, D = hn.shape
    nb, _, w = w_in.shape
    E2 = nb * w
    E = E2 // 2
    gd = E // GM_GROUPS
    tm = min(TM, T)

    def body(a_ref, w_ref, lng_ref, lnb_ref, ws_ref, bs_ref, z_ref, o_ref):
        av = a_ref[...]
        for j in range(nb):
            z_ref[:, j * w:(j + 1) * w] = jnp.dot(av, w_ref[j], preferred_element_type=F32)
        wc, _ = _causal_ws(ws_ref)
        for c in range(tm // CHUNK):
            rows = slice(c * CHUNK, (c + 1) * CHUNK)
            _, _, u, _, _, _, s = _sgu_pieces(z_ref[rows, :], lng_ref[...], lnb_ref[...], wc, bs_ref)
            for g in range(GM_GROUPS):
                cols = slice(g * gd, (g + 1) * gd)
                o_ref[rows, cols] = (u[:, cols] * s[g]).astype(BF16)

    full = lambda shape: pl.BlockSpec(shape, lambda i: (0,) * len(shape))
    return pl.pallas_call(
        body, name=name, grid=(T // tm,),
        in_specs=[pl.BlockSpec((tm, D), lambda i: (i, 0)), full((nb, D, w)), full((1, E)), full((1, E)),
                  full((GM_GROUPS, CHUNK, CHUNK)), full((GM_GROUPS, CHUNK, 1))],
        out_specs=[pl.BlockSpec((tm, E2), lambda i: (i, 0)), pl.BlockSpec((tm, E), lambda i: (i, 0))],
        out_shape=[jax.ShapeDtypeStruct((T, E2), F32), jax.ShapeDtypeStruct((T, E), BF16)],
        compiler_params=_cp(("parallel",)),
    )(hn, w_in, lng, lnb, ws, bs)


def _sgu_bwd(z, dh16, w_out, lng, lnb, ws, bs, name, after=None):
    T, E2 = z.shape
    D = dh16.shape[1]
    E = E2 // 2
    gd = E // GM_GROUPS
    tm = min(2 * CHUNK, T)
    nsteps = T // tm

    def body(z_ref, dh_ref, wo_ref, lng_ref, lnb_ref, ws_ref, bs_ref, *rest):
        dz_ref, dlng_ref, dlnb_ref, dws_ref, dbs_ref, dg_ref = rest[-6:]
        i = pl.program_id(0)

        @pl.when(i == 0)
        def _():
            dlng_ref[...] = jnp.zeros_like(dlng_ref)
            dlnb_ref[...] = jnp.zeros_like(dlnb_ref)
            dws_ref[...] = jnp.zeros_like(dws_ref)
            dbs_ref[...] = jnp.zeros_like(dbs_ref)

        dg_ref[...] = _dot_nt(dh_ref[...], wo_ref[...]).astype(BF16)
        wc, tri = _causal_ws(ws_ref)
        lng_v = lng_ref[...]
        for c in range(tm // CHUNK):
            rows = slice(c * CHUNK, (c + 1) * CHUNK)
            zu, zv, u, xhat, rs, vln, s = _sgu_pieces(z_ref[rows, :], lng_v, lnb_ref[...], wc, bs_ref)
            dgc = dg_ref[rows, :].astype(F32)
            du, dvln = [], []
            for g in range(GM_GROUPS):
                cols = slice(g * gd, (g + 1) * gd)
                dgg = dgc[:, cols]
                du.append(dgg * s[g])
                ds = dgg * u[:, cols]
                dsb = ds.astype(BF16)
                dws_ref[g] += _dot_nt(dsb, vln[:, cols].astype(BF16))
                dbs_ref[g] += jnp.sum(ds, axis=-1, keepdims=True)
                dvln.append(_dot_tn(wc[g], dsb))
            du = jnp.concatenate(du, axis=1)
            dvln = jnp.concatenate(dvln, axis=1)
            dlng_ref[...] += jnp.sum(dvln * xhat, axis=0, keepdims=True)
            dlnb_ref[...] += jnp.sum(dvln, axis=0, keepdims=True)
            dxh = dvln * lng_v
            m1 = jnp.mean(dxh, axis=-1, keepdims=True)
            m2 = jnp.mean(dxh * xhat, axis=-1, keepdims=True)
            dv = rs * (dxh - m1 - xhat * m2)
            dz_ref[rows, :E] = (du * _gelu_grad(zu)).astype(BF16)
            dz_ref[rows, E:] = (dv * _gelu_grad(zv)).astype(BF16)

        @pl.when(i == nsteps - 1)
        def _():
            for g in range(GM_GROUPS):
                dws_ref[g] = jnp.where(tri, dws_ref[g], 0.0)

    full = lambda shape: pl.BlockSpec(shape, lambda i: (0,) * len(shape))
    in_specs = [pl.BlockSpec((tm, E2), lambda i: (i, 0)), pl.BlockSpec((tm, D), lambda i: (i, 0)), full((E, D)),
                full((1, E)), full((1, E)), full((GM_GROUPS, CHUNK, CHUNK)), full((GM_GROUPS, CHUNK, 1))]
    args = [z, dh16, w_out, lng, lnb, ws, bs]
    if after is not None:
        in_specs.append(pl.BlockSpec(memory_space=pl.ANY))
        args.append(after)
    return pl.pallas_call(
        body, name=name, grid=(nsteps,),
        in_specs=in_specs,
        out_specs=[pl.BlockSpec((tm, E2), lambda i: (i, 0)), full((1, E)), full((1, E)),
                   full((GM_GROUPS, CHUNK, CHUNK)), full((GM_GROUPS, CHUNK, 1))],
        out_shape=[jax.ShapeDtypeStruct((T, E2), BF16), jax.ShapeDtypeStruct((1, E), F32),
                   jax.ShapeDtypeStruct((1, E), F32), jax.ShapeDtypeStruct((GM_GROUPS, CHUNK, CHUNK), F32),
                   jax.ShapeDtypeStruct((GM_GROUPS, CHUNK, 1), F32)],
        scratch_shapes=[pltpu.VMEM((tm, E), BF16)],
        compiler_params=_cp(("arbitrary",)),
    )(*args)


HALO = 16


def _conv_taps(a_ext, w_ref, b_ref):
    n = a_ext.shape[0]
    am1 = pltpu.roll(a_ext, 1, 0)
    am2 = pltpu.roll(a_ext, 2, 0)
    del n
    return ((b_ref[...] + am2 * w_ref[0:1, :]) + am1 * w_ref[1:2, :]) + a_ext * w_ref[2:3, :], am1, am2


def _ffn_up_fused(hn, wgu, cw, cb, name):
    T, D = hn.shape
    nb2, _, w = wgu.shape
    nb = nb2 // 2
    F = nb * w
    tm = min(TM, T)

    def body(a_ref, w_ref, cw_ref, cb_ref, au_ref, act_ref, halo_ref):
        @pl.when(pl.program_id(0) == 0)
        def _():
            halo_ref[...] = jnp.zeros_like(halo_ref)

        av = a_ref[...]
        for j in range(nb):
            cols = slice(j * w, (j + 1) * w)
            g = jnp.dot(av, w_ref[j], preferred_element_type=F32)
            u = jnp.dot(av, w_ref[nb + j], preferred_element_type=F32)
            au_ref[:, cols] = g
            au_ref[:, F + j * w:F + (j + 1) * w] = u
            ext = jnp.concatenate([halo_ref[:, cols], g], axis=0)
            am1 = pltpu.roll(ext, 1, 0)
            am2 = pltpu.roll(ext, 2, 0)
            conv = ((cb_ref[:, cols] + am2 * cw_ref[0:1, cols]) + am1 * cw_ref[1:2, cols]) + ext * cw_ref[2:3, cols]
            conv = conv[HALO:, :]
            act_ref[:, cols] = ((conv * _sigmoid(conv)) * u).astype(BF16)
            halo_ref[:, cols] = g[tm - HALO:, :]

    return pl.pallas_call(
        body, name=name, grid=(T // tm,),
        in_specs=[pl.BlockSpec((tm, D), lambda i: (i, 0)), pl.BlockSpec((nb2, D, w), lambda i: (0, 0, 0)),
                  pl.BlockSpec((3, F), lambda i: (0, 0)), pl.BlockSpec((1, F), lambda i: (0, 0))],
        out_specs=[pl.BlockSpec((tm, 2 * F), lambda i: (i, 0)), pl.BlockSpec((tm, F), lambda i: (i, 0))],
        out_shape=[jax.ShapeDtypeStruct((T, 2 * F), F32), jax.ShapeDtypeStruct((T, F), BF16)],
        scratch_shapes=[pltpu.VMEM((HALO, F), F32)],
        compiler_params=_cp(("arbitrary",)),
    )(hn, wgu, cw, cb)


def _ffn_mid_bwd(au, dh16, wd, cw, cb, name, after=None):
    T, F = au.shape[0], au.shape[1] // 2
    D = dh16.shape[1]
    tm, tf = min(TM, T), min(TF, F)
    hb = tm // HALO
    nt = T // tm
    nf = F // tf
    last_h = T // HALO - 1

    def body(a_ref, ap_ref, an_ref, u_ref, un_ref, dh_ref, dhn_ref, wd_ref, w_ref, b_ref, *rest):
        da_ref, du_ref, dcw_ref, dcb_ref = rest[-4:]
        i = pl.program_id(1)
        prev = jnp.where(i == 0, 0.0, ap_ref[...])
        a_main = a_ref[...]
        a_ext = jnp.concatenate([prev, a_main, an_ref[...]], axis=0)
        conv, am1, am2 = _conv_taps(a_ext, w_ref, b_ref)
        conv = conv[HALO:, :]
        sig = _sigmoid(conv)
        u_ext = jnp.concatenate([u_ref[...], un_ref[...]], axis=0)
        wd_f = wd_ref[...]
        d_ext = jnp.concatenate([_dot_nt(dh_ref[...], wd_f), _dot_nt(dhn_ref[...], wd_f)], axis=0)
        d_ext = d_ext.astype(BF16).astype(F32)
        n = tm + HALO
        row = lax.broadcasted_iota(jnp.int32, (n, 1), 0)
        live = jnp.logical_or(row < tm, i < nt - 1)
        dconv = jnp.where(live, d_ext * u_ext * (sig * (1.0 + conv * (1.0 - sig))), 0.0)
        du_ref[...] = (d_ext[:tm, :] * (conv[:tm, :] * sig[:tm, :])).astype(BF16)
        dp1 = pltpu.roll(dconv, n - 1, 0)[:tm, :]
        dp2 = pltpu.roll(dconv, n - 2, 0)[:tm, :]
        dc = dconv[:tm, :]
        da_ref[...] = ((dc * w_ref[2:3, :] + dp1 * w_ref[1:2, :]) + dp2 * w_ref[0:1, :]).astype(BF16)
        g2 = jnp.sum(dc * a_main, axis=0, keepdims=True)
        g1 = jnp.sum(dc * am1[HALO:HALO + tm, :], axis=0, keepdims=True)
        g0 = jnp.sum(dc * am2[HALO:HALO + tm, :], axis=0, keepdims=True)
        gb = jnp.sum(dc, axis=0, keepdims=True)

        @pl.when(i == 0)
        def _():
            dcw_ref[...] = jnp.zeros_like(dcw_ref)
            dcb_ref[...] = jnp.zeros_like(dcb_ref)

        dcw_ref[0:1, :] += g0
        dcw_ref[1:2, :] += g1
        dcw_ref[2:3, :] += g2
        dcb_ref[...] += gb

    main = pl.BlockSpec((tm, tf), lambda f, i: (i, f))
    prev = pl.BlockSpec((HALO, tf), lambda f, i: (jnp.maximum(i * hb - 1, 0), f))
    nxt = pl.BlockSpec((HALO, tf), lambda f, i: (jnp.minimum((i + 1) * hb, last_h), f))
    main_u = pl.BlockSpec((tm, tf), lambda f, i: (i, nf + f))
    nxt_u = pl.BlockSpec((HALO, tf), lambda f, i: (jnp.minimum((i + 1) * hb, last_h), nf + f))
    in_specs = [main, prev, nxt, main_u, nxt_u,
                pl.BlockSpec((tm, D), lambda f, i: (i, 0)),
                pl.BlockSpec((HALO, D), lambda f, i: (jnp.minimum((i + 1) * hb, last_h), 0)),
                pl.BlockSpec((tf, D), lambda f, i: (f, 0)),
                pl.BlockSpec((3, tf), lambda f, i: (0, f)), pl.BlockSpec((1, tf), lambda f, i: (0, f))]
    args = [au, au, au, au, au, dh16, dh16, wd, cw, cb]
    if after is not None:
        in_specs.append(pl.BlockSpec(memory_space=pl.ANY))
        args.append(after)
    return pl.pallas_call(
        body, name=name, grid=(nf, nt),
        in_specs=in_specs,
        out_specs=[main, main, pl.BlockSpec((3, tf), lambda f, i: (0, f)), pl.BlockSpec((1, tf), lambda f, i: (0, f))],
        out_shape=[jax.ShapeDtypeStruct((T, F), BF16), jax.ShapeDtypeStruct((T, F), BF16),
                   jax.ShapeDtypeStruct((3, F), F32), jax.ShapeDtypeStruct((1, F), F32)],
        compiler_params=_cp(("parallel", "arbitrary")),
    )(*args)


def _split3(x):
    hi = x.astype(BF16)
    r1 = x - hi.astype(F32)
    mid = r1.astype(BF16)
    lo = (r1 - mid.astype(F32)).astype(BF16)
    return hi, mid, lo


def _tri_ones(n, upper):
    r = lax.broadcasted_iota(jnp.int32, (n, n), 0)
    c = lax.broadcasted_iota(jnp.int32, (n, n), 1)
    return jnp.where((r <= c) if upper else (r >= c), 1.0, 0.0).astype(BF16)


def _gate_scan(f, bf, name):
    T = f.shape[0]
    tm = min(256, T)

    def body(f_ref, b_ref, cp_ref, sn_ref, carry_ref):
        i = pl.program_id(0)

        @pl.when(i == 0)
        def _():
            carry_ref[...] = jnp.zeros_like(carry_ref)

        x = f_ref[...] + b_ref[...]
        e = jnp.exp(-jnp.abs(x))
        logf = jnp.minimum(x, 0.0) - jnp.log(1.0 + e)
        sn_ref[...] = jnp.where(x >= 0.0, e / (1.0 + e), 1.0 / (1.0 + e))
        tri = _tri_ones(tm, upper=False)
        c = carry_ref[...]
        for piece in _split3(logf):
            c = c + jnp.dot(tri, piece, preferred_element_type=F32)
        carry_ref[...] += jnp.sum(logf, axis=0, keepdims=True)
        hi, mid, lo = _split3(c * LOG2E)
        cp_ref[:, 0:LANES] = hi
        cp_ref[:, LANES:2 * LANES] = mid
        cp_ref[:, 2 * LANES:3 * LANES] = lo

    return pl.pallas_call(
        body, name=name, grid=(T // tm,),
        in_specs=[pl.BlockSpec((tm, LANES), lambda i: (i, 0)), pl.BlockSpec((1, LANES), lambda i: (0, 0))],
        out_specs=[pl.BlockSpec((tm, 3 * LANES), lambda i: (i, 0)), pl.BlockSpec((tm, LANES), lambda i: (i, 0))],
        out_shape=[jax.ShapeDtypeStruct((T, 3 * LANES), BF16), jax.ShapeDtypeStruct((T, LANES), F32)],
        scratch_shapes=[pltpu.VMEM((1, LANES), F32)],
        compiler_params=_cp(("arbitrary",)),
    )(f, bf)


def _gate_scan_bwd(dcq, dck, sneg, name):
    T = dcq.shape[0]
    tm = min(256, T)
    n = T // tm

    def body(dcq_ref, dck_ref, sn_ref, df_ref, db_ref, carry_ref):
        i = pl.program_id(0)

        @pl.when(i == 0)
        def _():
            carry_ref[...] = jnp.zeros_like(carry_ref)
            db_ref[...] = jnp.zeros_like(db_ref)

        tri = _tri_ones(tm, upper=True)
        dcb = dcq_ref[...] - dck_ref[...]
        acc = carry_ref[...]
        for piece in _split3(dcb):
            acc = acc + jnp.dot(tri, piece, preferred_element_type=F32)
        carry_ref[...] += jnp.sum(dcb, axis=0, keepdims=True)
        df = acc * sn_ref[...]
        df_ref[...] = df.astype(BF16)
        db_ref[...] += jnp.sum(df, axis=0, keepdims=True)

    rev = pl.BlockSpec((tm, LANES), lambda i: (n - 1 - i, 0))
    return pl.pallas_call(
        body, name=name, grid=(n,),
        in_specs=[rev, rev, rev],
        out_specs=[rev, pl.BlockSpec((1, LANES), lambda i: (0, 0))],
        out_shape=[jax.ShapeDtypeStruct((T, LANES), BF16), jax.ShapeDtypeStruct((1, LANES), F32)],
        scratch_shapes=[pltpu.VMEM((1, LANES), F32)],
        compiler_params=_cp(("arbitrary",)),
    )(dcq, dck, sneg)


def _qk_proj(hn, w_pads, cp, sels, consts, scales, name):
    T, D = hn.shape
    H = w_pads[0].shape[1] // LANES
    tm = min(TM, T)

    def body(a_ref, cp_ref, wq_ref, wk_ref, sq_ref, sk_ref, cq_ref, ck_ref, qo_ref, ko_ref):
        a = a_ref[...]
        cpv = cp_ref[...]
        for w_ref, sel_ref, c_ref, o_ref, scale in ((wq_ref, sq_ref, cq_ref, qo_ref, scales[0]),
                                                    (wk_ref, sk_ref, ck_ref, ko_ref, scales[1])):
            for p in range(H // 2):
                acc = jnp.dot(a, w_ref[:, p * 2 * LANES:(p + 1) * 2 * LANES], preferred_element_type=F32)
                if scale != 1.0:
                    acc = acc * scale
                acc = acc + jnp.dot(cpv, sel_ref[p], preferred_element_type=F32) + c_ref[p]
                o_ref[2 * p] = acc[:, :LANES].astype(BF16)
                o_ref[2 * p + 1] = acc[:, LANES:].astype(BF16)

    whole = lambda t: pl.BlockSpec(t.shape, lambda i: (0,) * t.ndim)
    out = jax.ShapeDtypeStruct((H, T, LANES), BF16)
    o_spec = pl.BlockSpec((H, tm, LANES), lambda i: (0, i, 0))
    return pl.pallas_call(
        body, name=name, grid=(T // tm,),
        in_specs=[pl.BlockSpec((tm, D), lambda i: (i, 0)), pl.BlockSpec((tm, 3 * LANES), lambda i: (i, 0)),
                  whole(w_pads[0]), whole(w_pads[1]), whole(sels[0]), whole(sels[1]), whole(consts[0]), whole(consts[1])],
        out_specs=[o_spec, o_spec], out_shape=[out, out],
        compiler_params=_cp(("parallel",)),
    )(hn, cp, w_pads[0], w_pads[1], sels[0], sels[1], consts[0], consts[1])


def _lane_lo():
    return lax.broadcasted_iota(jnp.int32, (1, LANES), 1) < HEAD_DIM


def _attn_fwd_t(qp, kp, vt, name):
    H, T, _ = qp.shape
    tq = min(TQ, T)
    hd = HEAD_DIM
    ext = hd + 16

    def body(q_ref, k_ref, vt_ref, o_ref, o32_ref, lse_ref, m_sc, acc_sc):
        i = pl.program_id(1)
        m_sc[...] = jnp.full(m_sc.shape, NEG, F32)
        acc_sc[...] = jnp.zeros_like(acc_sc)
        q_t = [jnp.transpose(q_ref[h].astype(F32)).astype(BF16) for h in range(2)]
        ones_rows = jnp.where(lax.broadcasted_iota(jnp.int32, (16, tq), 0) == 0, 1.0, 0.0).astype(BF16)

        def steps(blocks):
            offs = [pl.multiple_of(j * tq, tq) for j, _ in blocks]
            s_all = [[jnp.dot(k_ref[h, pl.ds(off, tq), :], q_t[h], preferred_element_type=F32) for h in range(2)]
                     for off in offs]
            for (j, masked), off, s_blk in zip(blocks, offs, s_all):
                for h in range(2):
                    s = s_blk[h]
                    if masked:
                        kr = lax.broadcasted_iota(jnp.int32, (tq, tq), 0)
                        qc = lax.broadcasted_iota(jnp.int32, (tq, tq), 1)
                        s = jnp.where(qc >= kr, s, NEG)
                    m_prev = m_sc[h]
                    m_new = jnp.maximum(m_prev, jnp.max(s, axis=0, keepdims=True))
                    alpha = jnp.exp2(m_prev - m_new)
                    p16 = jnp.exp2(s - m_new).astype(BF16)
                    v_aug = jnp.concatenate([vt_ref[h * hd:(h + 1) * hd, pl.ds(off, tq)], ones_rows], axis=0)
                    pv = jnp.dot(v_aug, p16, preferred_element_type=F32)
                    acc_sc[h] = alpha * acc_sc[h] + pv
                    m_sc[h] = m_new

        def group_body(t, carry):
            steps([(KV_UNROLL * t + u, False) for u in range(KV_UNROLL)])
            return carry

        lax.fori_loop(0, i // KV_UNROLL, group_body, 0)
        for rem in range(KV_UNROLL):

            @pl.when(i % KV_UNROLL == rem)
            def _(rem=rem):
                steps([(i - rem + u, u == rem) for u in range(rem + 1)])

        o_t, lse_t = [], []
        for h in range(2):
            acc = acc_sc[h]
            l = acc[hd:hd + 1, :]
            o_t.append(acc[:hd, :] / l)
            lse_t.append(jnp.broadcast_to(m_sc[h] + jnp.log(l) * LOG2E, (hd, tq)))
        o = jnp.transpose(jnp.concatenate(o_t, axis=0))
        o_ref[...] = o.astype(BF16)
        o32_ref[...] = o
        lse_ref[...] = jnp.transpose(jnp.concatenate(lse_t, axis=0))

    oblk = pl.BlockSpec((tq, LANES), lambda p, i: (i, p))
    return pl.pallas_call(
        body, name=name, grid=(H // 2, T // tq),
        in_specs=[pl.BlockSpec((2, tq, LANES), lambda p, i: (p, i, 0)),
                  pl.BlockSpec((2, T, LANES), lambda p, i: (p, 0, 0)),
                  pl.BlockSpec((2 * hd, T), lambda p, i: (p, 0))],
        out_specs=[oblk, oblk, pl.BlockSpec((None, tq, LANES), lambda p, i: (p, i, 0))],
        out_shape=[jax.ShapeDtypeStruct((T, H * HEAD_DIM), BF16), jax.ShapeDtypeStruct((T, H * HEAD_DIM), F32),
                   jax.ShapeDtypeStruct((H // 2, T, LANES), F32)],
        scratch_shapes=[pltpu.VMEM((2, 1, tq), F32), pltpu.VMEM((2, ext, tq), F32)],
        compiler_params=_cp(("parallel", "arbitrary")),
    )(qp, kp, vt)


def _attn_bwd(qp, kp, v, o, do, lse, scale, name):
    H, T, _ = qp.shape
    tq = min(TQ, T)
    nq = T // tq
    nrep = tq // LANES

    def body(q_ref, k_ref, v_ref, o_ref, do_ref, lse_ref, dq_ref, dk_ref, dv_ref, dqe_ref, dke_ref, dk_sc, dv_sc, dq_sc):
        i = pl.program_id(1)

        @pl.when(i == 0)
        def _():
            dk_sc[...] = jnp.zeros_like(dk_sc)
            dv_sc[...] = jnp.zeros_like(dv_sc)

        dq_sc[...] = jnp.zeros_like(dq_sc)

        lo = _lane_lo()
        dob = do_ref[...]
        dof = dob.astype(F32)
        prod = dof * o_ref[...].astype(F32)
        lse2 = lse_ref[...]
        lse2_sw = pltpu.roll(lse2, HEAD_DIM, 1)
        zero = jnp.zeros_like(dob)
        do_h = [jnp.where(lo, dob, zero), jnp.where(lo, zero, dob)]
        rep = lambda col: jnp.broadcast_to(col, (tq, LANES))
        delta = [rep(jnp.sum(jnp.where(lo, prod, 0.0), axis=-1, keepdims=True)),
                 rep(jnp.sum(jnp.where(lo, 0.0, prod), axis=-1, keepdims=True))]
        lse_h = [jnp.where(lo, lse2, lse2_sw), jnp.where(lo, lse2_sw, lse2)]
        qs = [q_ref[0], q_ref[1]]
        tr16 = lambda a: jnp.transpose(a.astype(F32)).astype(BF16)
        q_t = [tr16(qs[0]), tr16(qs[1])]
        do_t = [tr16(do_h[0]), tr16(do_h[1])]

        def steps(blocks):
            offs = [pl.multiple_of(j * tq, tq) for j, _ in blocks]
            vblks = [v_ref[pl.ds(off, tq), :] for off in offs]
            kblks = [[k_ref[h, pl.ds(off, tq), :] for h in range(2)] for off in offs]
            s_all = [[_dot_nt(qs[h], kb[h]) for h in range(2)] for kb in kblks]
            dp_all = [[_dot_nt(do_h[h], vb) for h in range(2)] for vb in vblks]
            for b, ((j, masked), off) in enumerate(zip(blocks, offs)):
                dv_add = None
                for h in range(2):
                    kblk, s, dp = kblks[b][h], s_all[b][h], dp_all[b][h]
                    p16, ds16 = [], []
                    for c in range(nrep):
                        cols = slice(c * LANES, (c + 1) * LANES)
                        p = jnp.exp2(s[:, cols] - lse_h[h])
                        if masked:
                            r = lax.broadcasted_iota(jnp.int32, (tq, LANES), 0)
                            cc = lax.broadcasted_iota(jnp.int32, (tq, LANES), 1)
                            p = jnp.where(r >= cc + c * LANES, p, 0.0)
                        p16.append(p.astype(BF16))
                        ds16.append((p * (dp[:, cols] - delta[h])).astype(BF16))
                    p16 = jnp.concatenate(p16, axis=1)
                    dsb = jnp.concatenate(ds16, axis=1)
                    dq_sc[h] += jnp.dot(dsb, kblk, preferred_element_type=F32)
                    dk_sc[h, :, pl.ds(off, tq)] += jnp.dot(q_t[h], dsb, preferred_element_type=F32)
                    pv = jnp.dot(do_t[h], p16, preferred_element_type=F32)
                    dv_add = pv if dv_add is None else dv_add + pv
                dv_sc[:, pl.ds(off, tq)] += dv_add

        def group_body(t, carry):
            steps([(KV_UNROLL_BWD * t + u, False) for u in range(KV_UNROLL_BWD)])
            return carry

        lax.fori_loop(0, i // KV_UNROLL_BWD, group_body, 0)
        for rem in range(KV_UNROLL_BWD):

            @pl.when(i % KV_UNROLL_BWD == rem)
            def _(rem=rem):
                steps([(i - rem + u, u == rem) for u in range(rem + 1)])

        dq0, dq1 = dq_sc[0], dq_sc[1]
        dq_ref[...] = (jnp.where(lo, dq0, pltpu.roll(dq1, HEAD_DIM, 1)) * scale).astype(BF16)
        row8 = lax.broadcasted_iota(jnp.int32, (8, 1), 0)
        pick = lambda blk, r: jnp.sum(jnp.where(row8 == r, blk, 0.0), axis=0, keepdims=True)
        two_rows = lambda a, b: jnp.where(row8 == 0, a, jnp.where(row8 == 1, b, 0.0))
        gate_rows = slice(HEAD_DIM, HEAD_DIM + 8)
        dqe_ref[...] = two_rows(pick(jnp.transpose(dq0)[gate_rows, :], 0), pick(jnp.transpose(dq1)[gate_rows, :], 0))

        @pl.when(i == nq - 1)
        def _():
            dke_ref[...] = two_rows(pick(dk_sc[0, gate_rows, :], 3), pick(dk_sc[1, gate_rows, :], 3))
            for cb in range(nq):
                tok = slice(cb * tq, (cb + 1) * tq)
                dk0 = jnp.transpose(dk_sc[0, :, tok])
                dk1 = jnp.transpose(dk_sc[1, :, tok])
                dk_ref[tok, :] = (jnp.where(lo, dk0, pltpu.roll(dk1, HEAD_DIM, 1)) * LN2).astype(BF16)
                dv_ref[tok, :] = jnp.transpose(dv_sc[:, tok]).astype(BF16)

    qblk = pl.BlockSpec((tq, LANES), lambda p, i: (i, p))
    pair = pl.BlockSpec((T, LANES), lambda p, i: (0, p))
    tok16 = jax.ShapeDtypeStruct((T, H * HEAD_DIM), BF16)
    gate32 = jax.ShapeDtypeStruct((H // 2, 8, T), F32)
    return pl.pallas_call(
        body, name=name, grid=(H // 2, nq),
        in_specs=[pl.BlockSpec((2, tq, LANES), lambda p, i: (p, i, 0)),
                  pl.BlockSpec((2, T, LANES), lambda p, i: (p, 0, 0)),
                  pair, qblk, qblk,
                  pl.BlockSpec((None, tq, LANES), lambda p, i: (p, i, 0))],
        out_specs=[qblk, pair, pair, pl.BlockSpec((None, 8, tq), lambda p, i: (p, 0, i)),
                   pl.BlockSpec((None, 8, T), lambda p, i: (p, 0, 0))],
        out_shape=[tok16, tok16, tok16, gate32, gate32],
        scratch_shapes=[pltpu.VMEM((2, LANES, T), F32), pltpu.VMEM((LANES, T), F32),
                        pltpu.VMEM((2, tq, LANES), F32)],
        compiler_params=_cp(("parallel", "arbitrary")),
    )(qp, kp, v, o, do, lse)


def _mesh_pos():
    return lax.axis_index("x"), lax.axis_index("y"), lax.axis_index("c")


def _all_gather(arrs, name, groups=None):
    n = len(arrs)
    if groups is None:
        groups = [(a, 0) for a in range(n)]
    ng = 1 + max(g for g, _ in groups)
    per_group = [sum(1 for g, _ in groups if g == gi) for gi in range(ng)]
    first_of = [next(a for a in range(n) if groups[a][0] == gi) for gi in range(ng)]

    def body(*refs):
        ins, outs = refs[:n], refs[n:n + ng]
        send_sems, recv_sems, local_sems = refs[n + ng:]
        x, y, c = _mesh_pos()
        me, sib = (x, y, c), (x, y, 1 - c)
        chips = [(1 - x, y), (x, 1 - y), (1 - x, 1 - y)]

        def dst_of(a, px, py, pc):
            g, k = groups[a]
            return outs[g].at[N_DEV * k + 4 * px + 2 * py + pc]

        def copy(a, k, block, to, src=None):
            dst = dst_of(a, *block)
            return pltpu.make_async_remote_copy(
                src_ref=dst if src is None else src, dst_ref=dst,
                send_sem=send_sems.at[a, k], recv_sem=recv_sems.at[a, k], device_id=to, device_id_type=MESH)

        mine = [pltpu.make_async_copy(ins[a], dst_of(a, *me), local_sems.at[a]) for a in range(n)]
        for cp in mine:
            cp.start()
        first = []
        for a in range(n):
            first.append(copy(a, 0, me, sib, src=ins[a]))
            first += [copy(a, 1 + j, me, (*chip, c), src=ins[a]) for j, chip in enumerate(chips)]
        for cp in first:
            cp.start()
        passed = []
        for j, chip in enumerate(chips):
            for a in range(n):
                copy(a, 1 + j, (*chip, c), me).wait_recv()
                fwd = copy(a, 4 + j, (*chip, c), sib)
                fwd.start()
                passed.append(fwd)
        for a in range(n):
            copy(a, 0, sib, me).wait_recv()
            for j, chip in enumerate(chips):
                copy(a, 4 + j, (*chip, 1 - c), me).wait_recv()
        for cp in first + passed:
            cp.wait_send()
        for cp in mine:
            cp.wait()

    any_spec = pl.BlockSpec(memory_space=pl.ANY)
    return pl.pallas_call(
        body, name=name,
        in_specs=[any_spec] * n, out_specs=[any_spec] * ng,
        out_shape=[jax.ShapeDtypeStruct((N_DEV * per_group[gi],) + arrs[first_of[gi]].shape, arrs[first_of[gi]].dtype)
                   for gi in range(ng)],
        scratch_shapes=[pltpu.SemaphoreType.DMA((n, 7)), pltpu.SemaphoreType.DMA((n, 7)),
                        pltpu.SemaphoreType.DMA((n,))],
    )(*arrs)


HBM_SPEC = pl.BlockSpec(memory_space=pltpu.HBM)
SEM_SPEC = pl.BlockSpec(memory_space=pltpu.SEMAPHORE)
ANY_SPEC = pl.BlockSpec(memory_space=pl.ANY)
DATAFLOW_EFFECT = pltpu.SideEffectType.DATAFLOW_SIDE_EFFECTING


def _peers():
    x, y, c = _mesh_pos()
    flip = lambda v, b: 1 - v if b else v
    return [(flip(x, (k >> 2) & 1), flip(y, (k >> 1) & 1), flip(c, k & 1)) for k in range(1, N_DEV)]


def _slot(p):
    return 4 * p[0] + 2 * p[1] + p[2]


def _direct_copy(src_refs, land_refs, sems, a, k, p, land_of, dst_slot, src_slot):
    s = src_slot(a, p)
    return pltpu.make_async_remote_copy(
        src_ref=src_refs[a] if s is None else src_refs[a].at[s], dst_ref=land_refs[land_of[a]].at[dst_slot(a, k)],
        send_sem=sems[0].at[a * (N_DEV - 1) + k], recv_sem=sems[1].at[a * (N_DEV - 1) + k], device_id=p,
        device_id_type=MESH)


def _direct_start(srcs, lands, land_of, dst_slot, src_slot, after, name, collective_id):
    n, nl = len(srcs), len(lands)

    def body(*refs):
        src_refs, land_refs = refs[:n], refs[n:n + nl]
        sems = (refs[n + nl + 1], refs[n + nl + 2])
        token = refs[-1]
        peers = _peers()
        barrier = pltpu.get_barrier_semaphore()
        for p in peers:
            pl.semaphore_signal(barrier, inc=1, device_id=p, device_id_type=MESH)
        pl.semaphore_wait(barrier, N_DEV - 1)
        for a in range(n):
            for k, p in enumerate(peers):
                _direct_copy(src_refs, land_refs, sems, a, k, p, land_of, dst_slot, src_slot).start()
        token[...] = jnp.zeros_like(token)

    hbm = lambda t: pltpu.HBM(t.shape, t.dtype)
    sem_t = pltpu.SemaphoreType.DMA((n * (N_DEV - 1),))
    outs = pl.pallas_call(
        body, name=name,
        out_shape=(sem_t, sem_t, *[hbm(t) for t in srcs], *[hbm(t) for t in lands], jax.ShapeDtypeStruct((8, LANES), F32)),
        in_specs=[HBM_SPEC] * (n + nl) + [ANY_SPEC],
        out_specs=(SEM_SPEC, SEM_SPEC, *([HBM_SPEC] * (n + nl)), pl.BlockSpec(memory_space=pltpu.VMEM)),
        input_output_aliases={i: 2 + i for i in range(n + nl)},
        compiler_params=pltpu.CompilerParams(has_side_effects=DATAFLOW_EFFECT, collective_id=collective_id),
    )(*[pltpu.with_memory_space_constraint(t, pltpu.HBM) for t in srcs],
      *[pltpu.with_memory_space_constraint(t, pltpu.HBM) for t in lands], after)
    return outs[0], outs[1], list(outs[2:2 + n]), list(outs[2 + n:2 + n + nl]), outs[-1]


def _direct_wait(send_sems, recv_sems, srcs, lands, land_of, idxs, dst_slot, src_slot, after, name):
    land_ids = []
    for a in idxs:
        if land_of[a] not in land_ids:
            land_ids.append(land_of[a])
    m, ml = len(idxs), len(land_ids)
    sub_land_of = {j: land_ids.index(land_of[a]) for j, a in enumerate(idxs)}

    def body(*refs):
        src_refs, land_refs = refs[:m], refs[m:m + ml]
        ssem, rsem = refs[m + ml], refs[m + ml + 1]
        for j, a in enumerate(idxs):
            for k, p in enumerate(_peers()):
                s = src_slot(a, p)
                cp = pltpu.make_async_remote_copy(
                    src_ref=src_refs[j] if s is None else src_refs[j].at[s],
                    dst_ref=land_refs[sub_land_of[j]].at[dst_slot(a, k)],
                    send_sem=ssem.at[a * (N_DEV - 1) + k], recv_sem=rsem.at[a * (N_DEV - 1) + k], device_id=p,
                    device_id_type=MESH)
                cp.wait_send()
                cp.wait_recv()

    hbm = lambda t: pltpu.HBM(t.shape, t.dtype)
    sub_s, sub_l = [srcs[a] for a in idxs], [lands[g] for g in land_ids]
    outs = pl.pallas_call(
        body, name=name,
        out_shape=(*[hbm(t) for t in sub_s], *[hbm(t) for t in sub_l]),
        in_specs=[HBM_SPEC] * (m + ml) + [SEM_SPEC, SEM_SPEC, ANY_SPEC],
        out_specs=tuple([HBM_SPEC] * (m + ml)),
        input_output_aliases={i: i for i in range(m + ml)},
        compiler_params=pltpu.CompilerParams(has_side_effects=DATAFLOW_EFFECT),
    )(*sub_s, *sub_l, send_sems, recv_sems, after)
    return list(outs[:m]), list(outs[m:])


def _row_block(R, C):
    best = None
    for d in range(16, R + 1, 16):
        if R % d == 0 and d * C <= 256 * 1024:
            best = d
    return best if best is not None else R


def _adamw_math(w, g, m, v):
    m = ADAM_B1 * m + (1.0 - ADAM_B1) * g
    v = ADAM_B2 * v + (1.0 - ADAM_B2) * (g * g)
    m_hat = m / (1.0 - ADAM_B1 ** ADAM_STEP)
    v_hat = v / (1.0 - ADAM_B2 ** ADAM_STEP)
    delta = -ADAM_LR * (m_hat / (jnp.sqrt(v_hat) + ADAM_EPS) + ADAM_WD * w)
    return delta, m, v


def _sum_adamw(parts, w, m, v, name, sel=None):
    R, C = w.shape
    nseg = max(len(arr) if isinstance(arr, list) else 1 for arr, _ in parts)
    tr = _row_block(R // nseg, C)
    bps = R // nseg // tr
    specs, args = [], []
    for arr, idxs in parts:
        pieces = arr if isinstance(arr, list) else [arr] * nseg
        for idx in idxs:
            for sg in range(nseg if isinstance(arr, list) else 1):
                row = (lambda i, sg=sg: jnp.clip(i - sg * bps, 0, bps - 1)) if isinstance(arr, list) else (lambda i: i)
                if idx < 0:
                    specs.append(pl.BlockSpec((None, tr, C), lambda i, s, row=row: (s[0], row(i), 0)))
                else:
                    specs.append(pl.BlockSpec((None, tr, C), lambda i, s, idx=idx, row=row: (idx, row(i), 0)))
                args.append(pieces[sg])
    seg_counts = [(nseg if isinstance(arr, list) else 1) for arr, idxs in parts for _ in idxs]
    npart = len(args)
    blk = pl.BlockSpec((tr, C), lambda i, s: (i, 0))

    def body(s_ref, *refs):
        del s_ref
        seg = pl.program_id(0) // bps
        g, at = None, 0
        for cnt in seg_counts:
            term = refs[at][...].astype(F32)
            for sg in range(1, cnt):
                term = jnp.where(seg == sg, refs[at + sg][...].astype(F32), term)
            g = term if g is None else g + term
            at += cnt
        w_ref, m_ref, v_ref, g_out, d_out, m_out, v_out = refs[npart:]
        delta, mm, vv = _adamw_math(w_ref[...], g, m_ref[...], v_ref[...])
        g_out[...] = g
        d_out[...] = delta
        m_out[...] = mm
        v_out[...] = vv

    grid_spec = pltpu.PrefetchScalarGridSpec(
        num_scalar_prefetch=1, grid=(R // tr,),
        in_specs=specs + [blk, blk, blk], out_specs=[blk] * 4)
    if sel is None:
        sel = jnp.zeros((1,), jnp.int32)
    return pl.pallas_call(
        body, name=name, grid_spec=grid_spec,
        out_shape=[jax.ShapeDtypeStruct((R, C), F32)] * 4,
        compiler_params=_cp(("parallel",)),
    )(sel, *args, w, m, v)


def _rows(a, c):
    return a.reshape(-1, c)


def _pad_rows(a, r):
    return jnp.pad(a, ((0, r - a.shape[0]), (0, 0))) if a.shape[0] != r else a


def _gate_tables():
    hp = N_HEADS // 2
    sel_q = np.zeros((hp, 3 * LANES, 2 * LANES), np.float32)
    sel_k = np.zeros((hp, 3 * LANES, 2 * LANES), np.float32)
    const_q = np.zeros((hp, 1, 2 * LANES), np.float32)
    const_k = np.zeros((hp, 1, 2 * LANES), np.float32)
    for p in range(hp):
        for hh in range(2):
            h = 2 * p + hh
            base = hh * LANES + HEAD_DIM
            for piece in range(3):
                sel_q[p, piece * LANES + h, base + piece] = 1.0
                sel_k[p, piece * LANES + h, base + 3 + piece] = -1.0
            const_k[p, 0, base:base + 3] = 1.0
            const_q[p, 0, base + 3:base + 6] = 1.0
    as_bf = lambda t: jnp.asarray(t, BF16)
    return as_bf(sel_q), as_bf(sel_k), jnp.asarray(const_q), jnp.asarray(const_k)


def _pad_heads(w):
    d = w.shape[0]
    w3 = w.reshape(d, N_HEADS, HEAD_DIM)
    return jnp.pad(w3, ((0, 0), (0, 0), (0, LANES - HEAD_DIM))).reshape(d, N_HEADS * LANES)


def kernel(x, mix_norm_g, ffn_norm_g, gm_w_in, gm_ln_g, gm_ln_b, gm_w_s, gm_b_s, gm_w_out, fox_w_qkvf, fox_b_f, fox_w_o, ffn_w_gate, ffn_w_up, ffn_conv_w, ffn_conv_b, ffn_w_down, final_norm_g, loss_target, m_mix_norm_g, m_ffn_norm_g, m_gm_w_in, m_gm_ln_g, m_gm_ln_b, m_gm_w_s, m_gm_b_s, m_gm_w_out, m_fox_w_qkvf, m_fox_b_f, m_fox_w_o, m_ffn_w_gate, m_ffn_w_up, m_ffn_conv_w, m_ffn_conv_b, m_ffn_w_down, m_final_norm_g, v_mix_norm_g, v_ffn_norm_g, v_gm_w_in, v_gm_ln_g, v_gm_ln_b, v_gm_w_s, v_gm_b_s, v_gm_w_out, v_fox_w_qkvf, v_fox_b_f, v_fox_w_o, v_ffn_w_gate, v_ffn_w_up, v_ffn_conv_w, v_ffn_conv_b, v_ffn_w_down, v_final_norm_g):
    T, D = x.shape[1], x.shape[2]
    E = gm_ln_g.shape[1]
    FF = ffn_conv_b.shape[1]
    NQKVF = 3 * D + N_HEADS
    xi, yi, ci = _mesh_pos()
    me = 4 * xi + 2 * yi + ci
    h0 = x.reshape(T, D)
    tgt = loss_target.reshape(T, D)

    nl = ffn_w_gate.shape[0]
    to16 = lambda a: a.astype(BF16)
    n_cw_rows = ffn_conv_w.size // LANES
    cw_rows = _pad_rows(_rows(ffn_conv_w.astype(F32), LANES), 16)
    (w_in_g,) = _all_gather([to16(gm_w_in[0])], "ag_weights")
    later, land_of, land_off, lands, src_ids = [], [], [], [], {}

    def add_sources(key, srcs, new_lands, of, offs):
        src_ids[key] = list(range(len(later), len(later) + len(srcs)))
        land_of.extend(len(lands) + o for o in of)
        land_off.extend(offs)
        later.extend(srcs)
        lands.extend(new_lands)

    def add_ffn(l):
        add_sources(f"ffn{l}", [to16(ffn_w_gate[l]), to16(ffn_w_up[l]), to16(ffn_w_down[l])],
                    [lax.empty((2 * N_DEV, D, FF // N_DEV), BF16), lax.empty((N_DEV, FF // N_DEV, D), BF16)],
                    [0, 0, 1], [0, N_DEV, 0])

    add_sources("gm_out", [to16(gm_w_out[0]), cw_rows],
                [lax.empty((N_DEV, E // N_DEV, D), BF16), lax.empty((N_DEV,) + cw_rows.shape, F32)], [0, 1], [0, 0])
    add_ffn(0)
    add_sources("fox", [to16(fox_w_qkvf[0]), to16(fox_w_o[0])],
                [lax.empty((N_DEV, D, NQKVF // N_DEV), BF16), lax.empty((N_DEV, D // N_DEV, D), BF16)], [0, 1], [0, 0])
    for l in range(1, nl):
        add_ffn(l)
    ag_dst = lambda a, k: land_off[a] + _slot(_mesh_pos())
    ag_src = lambda a, p: None
    ag_send, ag_recv, later, lands, ag_token = _direct_start(later, lands, land_of, ag_dst, ag_src, w_in_g,
                                                             "ag_later_start", collective_id=1)

    def own_blocks(landed, shards, offs):
        for s, o in zip(shards, offs):
            landed = lax.dynamic_update_index_in_dim(landed, s, o + me, 0)
        return landed

    def gather_wait(idxs, after, name):
        return _direct_wait(ag_send, ag_recv, later, lands, land_of, idxs, ag_dst, ag_src, after, name)

    ffn_w = {}

    def ffn_weights(l):
        return ffn_w[l]

    def land_ffn(l, shards, gu_land, dn_land):
        ffn_w[l] = (own_blocks(gu_land, shards[:2], [0, N_DEV]), own_blocks(dn_land, shards[2:3], [0]).reshape(FF, D))

    saved = {}

    def ffn_fwd(l, h_in, hn, next_g):
        wgul, wdl = ffn_weights(l)
        au, act = _ffn_up_fused(hn, wgul, conv_w_full[l], ffn_conv_b[l:l + 1], f"ffn{l}_up")
        saved[f"ffn{l}"] = (h_in, hn, au, act)
        if next_g is None:
            return _mm_nn(act, wdl, f"ffn{l}_down", res=h_in), None
        return _mm_nn(act, wdl, f"ffn{l}_down", res=h_in, norm_g=next_g)

    bs_col = gm_b_s[0].reshape(GM_GROUPS, CHUNK, 1)
    hn0 = _rms_fwd(h0, mix_norm_g[0:1], "mix0_norm", after=ag_token)
    z, gu = _gm_in_fused(hn0, w_in_g, gm_ln_g, gm_ln_b, gm_w_s[0], bs_col, "gm_in")
    mine_o, land_o = gather_wait(src_ids["gm_out"], z, "ag_wout_wait")
    w_out_g = own_blocks(land_o[0], mine_o[0:1], [0]).reshape(E, D)
    cwg = own_blocks(land_o[1], mine_o[1:2], [0])
    conv_w_full = jnp.transpose(cwg[:, :n_cw_rows].reshape(N_DEV, nl, 3, FF // N_DEV), (1, 2, 0, 3)).reshape(nl, 3, FF)
    h1, hn_f0 = _mm_nn(gu, w_out_g, "gm_out", res=h0, norm_g=ffn_norm_g[0:1])
    mine0, land0 = gather_wait(src_ids["ffn0"], h1, "ag_ffn0_wait")
    land_ffn(0, mine0, *land0)
    h2, hn2 = ffn_fwd(0, h1, hn_f0, mix_norm_g[1:2])

    mine_x, land_x = gather_wait(src_ids["fox"], h2, "ag_fox_wait")
    w_qkvf = jnp.transpose(own_blocks(land_x[0], mine_x[0:1], [0]), (1, 0, 2)).reshape(D, NQKVF)
    w_o_g = own_blocks(land_x[1], mine_x[1:2], [0]).reshape(D, D)
    w_q, w_k, w_v = w_qkvf[:, :D], w_qkvf[:, D:2 * D], w_qkvf[:, 2 * D:3 * D]
    w_f = jnp.pad(w_qkvf[:, 3 * D:], ((0, 0), (0, LANES - N_HEADS)))
    bf_row = jnp.pad(fox_b_f, ((0, 0), (0, LANES - N_HEADS)))
    sel_q, sel_k, const_q, const_k = _gate_tables()
    scale = HEAD_DIM ** -0.5
    f_logit = _mm_nn(hn2, w_f, "fox_f")
    cp, sneg = _gate_scan(f_logit, bf_row, "fox_scan")
    qp, kp = _qk_proj(hn2, (_pad_heads(w_q), _pad_heads(w_k)), cp, (sel_q, sel_k), (const_q, const_k),
                      (scale * LOG2E, 1.0), "fox_qk")
    vv = _mm_nn(hn2, w_v, "fox_v", out_dtype=BF16)
    o, o32, lse = _attn_fwd_t(qp, kp, jnp.transpose(vv), "fox_attn")
    h3, hn_f1 = _mm_nn(o, w_o_g, "fox_o", res=h2, norm_g=ffn_norm_g[1:2])
    mine1, land1w = gather_wait([a for l in range(1, nl) for a in src_ids[f"ffn{l}"]], h3, "ag_ffn1_wait")
    for l in range(1, nl):
        land_ffn(l, mine1[3 * (l - 1):3 * l], land1w[2 * (l - 1)], land1w[2 * (l - 1) + 1])
    h4, _ = ffn_fwd(1, h3, hn_f1, None)

    dh, dh16, d_final, loss_row = _loss_head(h4, tgt, final_norm_g.reshape(1, D), "loss_head")
    loss = lax.psum(loss_row[0, 0], ("x", "y", "c"))

    rs_dst = lambda a, k: k
    rs_src = lambda a, p: _slot(p)
    me_idx = me.astype(jnp.int32).reshape(1)

    def rs_start(grads, name, cid):
        lands = [lax.empty((N_DEV - 1,) + g.shape[1:], BF16) for g in grads]
        return _direct_start(grads, lands, list(range(len(grads))), rs_dst, rs_src, loss_row, name, collective_id=cid)

    def rs_wait(st, after, name):
        n = len(st[2])
        return _direct_wait(st[0], st[1], st[2], st[3], list(range(n)), list(range(n)), rs_dst, rs_src, after, name)

    def ffn_bwd(l, dh, dh16, after=None):
        wgul, wdl = ffn_weights(l)
        h_in, hn, au, act = saved[f"ffn{l}"]
        da, dup, d_cw, d_cb = _ffn_mid_bwd(au, dh16, wdl, conv_w_full[l], ffn_conv_b[l:l + 1], f"ffn{l}_dmid", after=after)
        d_wd = _mm_tn(act, dh16, f"ffn{l}_dwd", out_dtype=BF16)
        dh_in, dh_in16, d_norm = _mm_nt([da, dup], wgul, f"ffn{l}_dhn", norm_bwd=(h_in, ffn_norm_g[l:l + 1], dh))
        d_wg = _mm_tn(hn, da, f"ffn{l}_dwg", blocked_w=FF // N_DEV, out_dtype=BF16)
        d_wu = _mm_tn(hn, dup, f"ffn{l}_dwu", blocked_w=FF // N_DEV, out_dtype=BF16)
        big_g = [d_wg, d_wu, d_wd.reshape(N_DEV, FF // N_DEV, D)]
        return dh_in, dh_in16, big_g, dict(cw=d_cw, cb=d_cb, norm=d_norm)

    dh, dh16, big_ffn1, g_ffn1 = ffn_bwd(1, dh, dh16)

    do = _mm_nt([dh16], w_o_g, "fox_do", out_dtype=BF16)
    d_wo = _mm_tn(o, dh16, "fox_dwo", out_dtype=BF16)
    dq, dk, dv, dqe, dke = _attn_bwd(qp, kp, vv, o32, do, lse, scale, "fox_dattn")
    gate_lane = lambda e: jnp.pad(jnp.transpose(e[:, :2, :].reshape(N_HEADS, T)), ((0, 0), (0, LANES - N_HEADS)))
    df, d_bf = _gate_scan_bwd(gate_lane(dqe), gate_lane(dke), sneg, "fox_dscan")
    dhn = _mm_nt([df], w_f, "fox_dhn_f")
    dh_mix1 = _mm_nt([dq, dk, dv], w_qkvf[:, :3 * D], "fox_dhn_qkv", add=dhn, norm_bwd=(h2, mix_norm_g[1:2], dh))
    d_wq = _mm_tn(hn2, dq, "fox_dwq", out_dtype=BF16)
    d_wk = _mm_tn(hn2, dk, "fox_dwk", out_dtype=BF16)
    d_wv = _mm_tn(hn2, dv, "fox_dwv", out_dtype=BF16)
    d_wf = _mm_tn(hn2, df, "fox_dwf", out_dtype=BF16)
    d_wqkvf = jnp.concatenate([d_wq, d_wk, d_wv, d_wf[:, :N_HEADS]], axis=1)
    dh, dh16, d_mix1 = dh_mix1
    st1 = rs_start([jnp.transpose(d_wqkvf.reshape(D, N_DEV, NQKVF // N_DEV), (1, 0, 2)),
                    d_wo.reshape(N_DEV, D // N_DEV, D)] + big_ffn1, "rs1_start", 2)

    dh, dh16, big_ffn0, g_ffn0 = ffn_bwd(0, dh, dh16, after=st1[4])
    d_wout = _mm_tn(gu, dh16, "gm_dwout", out_dtype=BF16)
    st2 = rs_start(big_ffn0 + [d_wout.reshape(N_DEV, E // N_DEV, D)], "rs2_start", 3)
    dz, d_lng, d_lnb, d_ws, d_bs = _sgu_bwd(z, dh16, w_out_g, gm_ln_g, gm_ln_b, gm_w_s[0], bs_col, "gm_dsgu", after=st2[4])
    d_win = _mm_tn(hn0, dz, "gm_dwin", blocked_w=2 * E // N_DEV, out_dtype=BF16)
    st3 = rs_start([d_win], "rs3_start", 4)
    dx, _, d_mix0 = _mm_nt([dz], w_in_g, "gm_dhn", after=st3[4], norm_bwd=(h0, mix_norm_g[0:1], dh))

    small = [("mix_norm_g", mix_norm_g, m_mix_norm_g, v_mix_norm_g, jnp.concatenate([d_mix0, d_mix1], axis=0)),
             ("ffn_norm_g", ffn_norm_g, m_ffn_norm_g, v_ffn_norm_g, jnp.concatenate([g_ffn0["norm"], g_ffn1["norm"]], axis=0)),
             ("gm_ln_g", gm_ln_g, m_gm_ln_g, v_gm_ln_g, d_lng),
             ("gm_ln_b", gm_ln_b, m_gm_ln_b, v_gm_ln_b, d_lnb),
             ("gm_w_s", gm_w_s, m_gm_w_s, v_gm_w_s, d_ws),
             ("gm_b_s", gm_b_s, m_gm_b_s, v_gm_b_s, d_bs),
             ("fox_b_f", fox_b_f, m_fox_b_f, v_fox_b_f, d_bf[:, :N_HEADS]),
             ("ffn_conv_b", ffn_conv_b, m_ffn_conv_b, v_ffn_conv_b, jnp.concatenate([g_ffn0["cb"], g_ffn1["cb"]], axis=0)),
             ("final_norm_g", final_norm_g, m_final_norm_g, v_final_norm_g, d_final)]
    d_cw_full = jnp.stack([g_ffn0["cw"], g_ffn1["cw"]], axis=0)

    def small_rows(a):
        flat = a.astype(F32).reshape(-1)
        n = -(-flat.size // (8 * LANES)) * (8 * LANES)
        return jnp.pad(flat, (0, n - flat.size)).reshape(-1, LANES)

    s_rows = [small_rows(p[1]).shape[0] for p in small]
    s_off = np.concatenate([[0], np.cumsum(s_rows)]).tolist()
    cw_g_rows = small_rows(d_cw_full)
    g_small = jnp.concatenate([small_rows(p[4]) for p in small] + [cw_g_rows], axis=0)

    own1, land1 = rs_wait(st1, dx, "rs1_wait")
    own2, land2 = rs_wait(st2, land1[0], "rs2_wait")
    big_out = {}

    def big_adamw(name, w, m, v, own, landed):
        shard2d = lambda a, c=(own[0] if isinstance(own, list) else own).shape[2]: a.reshape(-1, c)
        res = _sum_adamw([(own, [-1]), (landed, list(range(N_DEV - 1)))], shard2d(w), shard2d(m), shard2d(v),
                         f"adamw_{name}", sel=me_idx)
        big_out[name] = [t.reshape(w.shape) for t in res]

    big_adamw("fox_w_qkvf", fox_w_qkvf, m_fox_w_qkvf, v_fox_w_qkvf, own1[0], land1[0])
    big_adamw("fox_w_o", fox_w_o, m_fox_w_o, v_fox_w_o, own1[1], land1[1])
    big_adamw("ffn_w_gate", ffn_w_gate, m_ffn_w_gate, v_ffn_w_gate, [own2[0], own1[2]], [land2[0], land1[2]])
    big_adamw("ffn_w_up", ffn_w_up, m_ffn_w_up, v_ffn_w_up, [own2[1], own1[3]], [land2[1], land1[3]])
    big_adamw("ffn_w_down", ffn_w_down, m_ffn_w_down, v_ffn_w_down, [own2[2], own1[4]], [land2[2], land1[4]])
    big_adamw("gm_w_out", gm_w_out, m_gm_w_out, v_gm_w_out, own2[3], land2[3])

    (gs_all,) = _all_gather([g_small], "ag_small_grads")
    zeros_cw = jnp.zeros_like(cw_g_rows)
    cat = lambda k: jnp.concatenate([small_rows(p[k]) for p in small] + [zeros_cw], axis=0)
    small_out = _sum_adamw([(gs_all, list(range(N_DEV)))], cat(1), cat(2), cat(3), "adamw_small")
    gs = small_out[0]

    g_cw_full = gs[s_off[-1]:].reshape(-1)[:d_cw_full.size].reshape(d_cw_full.shape)
    g_cw = lax.dynamic_slice_in_dim(g_cw_full, me * (FF // N_DEV), FF // N_DEV, axis=2)
    cw2 = lambda a: _pad_rows(_rows(a.astype(F32), LANES), 16)
    cw_out = _sum_adamw([(cw2(g_cw)[None], [0])], cw2(ffn_conv_w), cw2(m_ffn_conv_w), cw2(v_ffn_conv_w), "adamw_conv_w")

    own3, land3 = rs_wait(st3, cw_out[0], "rs3_wait")
    big_adamw("gm_w_in", gm_w_in, m_gm_w_in, v_gm_w_in, own3[0], land3[0])

    names = ["mix_norm_g", "ffn_norm_g", "gm_w_in", "gm_ln_g", "gm_ln_b", "gm_w_s", "gm_b_s", "gm_w_out", "fox_w_qkvf",
             "fox_b_f", "fox_w_o", "ffn_w_gate", "ffn_w_up", "ffn_conv_w", "ffn_conv_b", "ffn_w_down", "final_norm_g"]
    small_idx = {p[0]: k for k, p in enumerate(small)}

    def pick(kind, name):
        if name in big_out:
            return big_out[name][kind]
        if name == "ffn_conv_w":
            return cw_out[kind][:n_cw_rows].reshape(ffn_conv_w.shape)
        k = small_idx[name]
        shp = small[k][1].shape
        return small_out[kind][s_off[k]:s_off[k + 1]].reshape(-1)[:int(np.prod(shp))].reshape(shp)

    outs = [loss, dx.reshape(x.shape)]
    for kind in range(4):
        outs += [pick(kind, n) for n in names]
    return tuple(outs)
```

```python
import math

import numpy as np
import jax
import jax.numpy as jnp
from jax import lax
from jax.experimental import pallas as pl
from jax.experimental.pallas import tpu as pltpu

F32 = jnp.float32
BF16 = jnp.bfloat16
MESH = pl.DeviceIdType.MESH

N_HEADS = 16
HEAD_DIM = 64
CHUNK = 128
GM_GROUPS = 8
RMS_EPS = 1e-6
LN_EPS = 1e-5
ADAM_LR = 0.001
ADAM_B1 = 0.9
ADAM_B2 = 0.999
ADAM_EPS = 1e-08
ADAM_WD = 0.01
ADAM_STEP = 10
N_DEV = 8

LANES = 128
VMEM_BYTES_V7X = 64 * 1024 * 1024
VMEM_LIMIT = 56 * 1024 * 1024

TM = 512
TM_MM = 1024
TT = 1024
TQ = 512
TF = 512
KV_UNROLL_BWD = 2
KV_UNROLL = 2
TN_ROWS = 512
MM_BLOCK_BYTES = 8 * 1024 * 1024
NEG = -1e30
LOG2E = math.log2(math.e)
LN2 = math.log(2.0)


def _cp(sem=None, vmem=VMEM_LIMIT):
    return pltpu.CompilerParams(dimension_semantics=sem, vmem_limit_bytes=vmem)


def _gelu(x):
    c = math.sqrt(2.0 / math.pi)
    return x * (0.5 * (1.0 + jnp.tanh(c * (x + 0.044715 * (x * x * x)))))


def _gelu_grad(x):
    c = math.sqrt(2.0 / math.pi)
    t = jnp.tanh(c * (x + 0.044715 * (x * x * x)))
    return 0.5 * (1.0 + t) + x * (0.5 * (1.0 - t * t)) * (c * (1.0 + 3.0 * 0.044715 * (x * x)))


def _sigmoid(x):
    return 1.0 / (1.0 + jnp.exp(-x))


def _dot_nt(a, b):
    return lax.dot_general(a, b, (((1,), (1,)), ((), ())), preferred_element_type=F32)


def _dot_tn(a, b):
    return lax.dot_general(a, b, (((0,), (0,)), ((), ())), preferred_element_type=F32)


def _rms_fwd(h, g, name, after=None):
    T, D = h.shape
    tm = min(TM, T)

    def body(h_ref, g_ref, *rest):
        o_ref = rest[-1]
        x = h_ref[...]
        r = lax.rsqrt(jnp.mean(x * x, axis=-1, keepdims=True) + RMS_EPS)
        o_ref[...] = ((x * r) * g_ref[...]).astype(BF16)

    in_specs = [pl.BlockSpec((tm, D), lambda i: (i, 0)), pl.BlockSpec((1, D), lambda i: (0, 0))]
    args = [h, g]
    if after is not None:
        in_specs.append(pl.BlockSpec(memory_space=pl.ANY))
        args.append(after)
    return pl.pallas_call(
        body, name=name, grid=(T // tm,),
        in_specs=in_specs,
        out_specs=pl.BlockSpec((tm, D), lambda i: (i, 0)),
        out_shape=jax.ShapeDtypeStruct((T, D), BF16),
        compiler_params=_cp(("parallel",)),
    )(*args)


def _loss_head(h, tgt, g, name):
    T, D = h.shape
    tm = min(TM, T)

    def body(h_ref, t_ref, g_ref, o_ref, ob_ref, dg_ref, l_ref):
        x = h_ref[...]
        gg = g_ref[...]
        r = lax.rsqrt(jnp.mean(x * x, axis=-1, keepdims=True) + RMS_EPS)
        xr = x * r
        e = xr * gg - t_ref[...]
        lpart = 0.5 * jnp.sum(jnp.mean(e * e, axis=-1, keepdims=True), axis=0, keepdims=True)
        dy = e * (1.0 / D)
        dyg = dy * gg
        dot = jnp.mean(dyg * x, axis=-1, keepdims=True)
        dh = r * dyg - x * ((r * r * r) * dot)
        o_ref[...] = dh
        ob_ref[...] = dh.astype(BF16)
        part = jnp.sum(dy * xr, axis=0, keepdims=True)
        lrow = jnp.broadcast_to(lpart, (1, LANES))

        @pl.when(pl.program_id(0) == 0)
        def _():
            dg_ref[...] = part
            l_ref[...] = lrow

        @pl.when(pl.program_id(0) != 0)
        def _():
            dg_ref[...] += part
            l_ref[...] += lrow

    blk = pl.BlockSpec((tm, D), lambda i: (i, 0))
    row = pl.BlockSpec((1, D), lambda i: (0, 0))
    return pl.pallas_call(
        body, name=name, grid=(T // tm,),
        in_specs=[blk, blk, row],
        out_specs=[blk, blk, row, pl.BlockSpec((1, LANES), lambda i: (0, 0))],
        out_shape=[jax.ShapeDtypeStruct((T, D), F32), jax.ShapeDtypeStruct((T, D), BF16),
                   jax.ShapeDtypeStruct((1, D), F32), jax.ShapeDtypeStruct((1, LANES), F32)],
        compiler_params=_cp(("arbitrary",)),
    )(h, tgt, g)


def _mm_nn(a, b, name, out_dtype=F32, res=None, norm_g=None):
    M, K = a.shape
    b3 = b if b.ndim == 3 else b[None]
    nb, _, w = b3.shape
    N = nb * w
    tm = min(TM_MM, M, max(256, MM_BLOCK_BYTES // (4 * N)))
    o_spec = pl.BlockSpec((tm, N), lambda i: (i, 0))
    in_specs = [pl.BlockSpec((tm, K), lambda i: (i, 0)), pl.BlockSpec((nb, K, w), lambda i: (0, 0, 0))]
    args = [a, b3]
    if res is not None:
        in_specs.append(o_spec)
        args.append(res)
    if norm_g is not None:
        in_specs.append(pl.BlockSpec((1, N), lambda i: (0, 0)))
        args.append(norm_g)
    n_out = 2 if norm_g is not None else 1

    def body(*refs):
        a_ref, b_ref = refs[0], refs[1]
        o_ref = refs[-n_out]
        av = a_ref[...]
        for j in range(nb):
            cols = slice(j * w, (j + 1) * w)
            acc = jnp.dot(av, b_ref[j], preferred_element_type=F32)
            if res is not None:
                acc = refs[2][:, cols] + acc
            o_ref[:, cols] = acc.astype(out_dtype)
        if norm_g is not None:
            x = o_ref[...]
            r = lax.rsqrt(jnp.mean(x * x, axis=-1, keepdims=True) + RMS_EPS)
            refs[-1][...] = ((x * r) * refs[3][...]).astype(BF16)

    out_shape = jax.ShapeDtypeStruct((M, N), out_dtype)
    if norm_g is None:
        out_specs, out_shapes = o_spec, out_shape
    else:
        out_specs, out_shapes = [o_spec, o_spec], [out_shape, jax.ShapeDtypeStruct((M, N), BF16)]
    return pl.pallas_call(
        body, name=name, grid=(M // tm,),
        in_specs=in_specs, out_specs=out_specs, out_shape=out_shapes,
        compiler_params=_cp(("parallel",)),
    )(*args)


def _mm_nt(a_list, b, name, out_dtype=F32, add=None, after=None, norm_bwd=None):
    M, kw = a_list[0].shape
    tm = min(TM, M)
    na = len(a_list)
    blocked = b.ndim == 3
    N = b.shape[1] if blocked else b.shape[0]
    b_spec = pl.BlockSpec(b.shape, lambda i: (0,) * b.ndim)
    o_spec = pl.BlockSpec((tm, N), lambda i: (i, 0))
    row_spec = pl.BlockSpec((1, N), lambda i: (0, 0))
    in_specs = [pl.BlockSpec((tm, kw), lambda i: (i, 0)) for _ in a_list] + [b_spec]
    args = list(a_list) + [b]
    if add is not None:
        in_specs.append(o_spec)
        args.append(add)
    n_in = len(args)
    if norm_bwd is not None:
        in_specs += [o_spec, row_spec, o_spec]
        args += list(norm_bwd)
    if after is not None:
        in_specs.append(pl.BlockSpec(memory_space=pl.ANY))
        args.append(after)
    n_args = len(args)

    def body(*refs):
        a_refs = refs[:na]
        b_ref = refs[na]
        acc = refs[na + 1][...] if add is not None else None
        for s, a_ref in enumerate(a_refs):
            if blocked:
                w = b_ref.shape[2]
                per = kw // w
                parts = [_dot_nt(a_ref[:, jj * w:(jj + 1) * w], b_ref[s * per + jj]) for jj in range(per)]
            else:
                parts = [_dot_nt(a_ref[...], b_ref[:, s * kw:(s + 1) * kw])]
            for part in parts:
                acc = part if acc is None else acc + part
        if norm_bwd is None:
            refs[n_args][...] = acc.astype(out_dtype)
            return
        h_ref, g_ref, r_ref = refs[n_in:n_in + 3]
        o_ref, ob_ref, dg_ref = refs[n_args:n_args + 3]
        x = h_ref[...]
        r = lax.rsqrt(jnp.mean(x * x, axis=-1, keepdims=True) + RMS_EPS)
        dyg = acc * g_ref[...]
        dot = jnp.mean(dyg * x, axis=-1, keepdims=True)
        dh = r_ref[...] + (r * dyg - x * ((r * r * r) * dot))
        o_ref[...] = dh
        ob_ref[...] = dh.astype(BF16)
        part_g = jnp.sum(acc * (x * r), axis=0, keepdims=True)

        @pl.when(pl.program_id(0) == 0)
        def _():
            dg_ref[...] = part_g

        @pl.when(pl.program_id(0) != 0)
        def _():
            dg_ref[...] += part_g

    if norm_bwd is None:
        out_specs, out_shapes, sem = o_spec, jax.ShapeDtypeStruct((M, N), out_dtype), ("parallel",)
    else:
        out_specs = [o_spec, o_spec, row_spec]
        out_shapes = [jax.ShapeDtypeStruct((M, N), F32), jax.ShapeDtypeStruct((M, N), BF16),
                      jax.ShapeDtypeStruct((1, N), F32)]
        sem = ("arbitrary",)
    return pl.pallas_call(
        body, name=name, grid=(M // tm,),
        in_specs=in_specs, out_specs=out_specs, out_shape=out_shapes,
        compiler_params=_cp(sem),
    )(*args)


def _mm_out_t(a, bt, name):
    M, K = a.shape
    N = bt.shape[0]
    tm = min(TM_MM, M)

    def body(a_ref, b_ref, o_ref):
        o_ref[...] = _dot_nt(b_ref[...], a_ref[...]).astype(BF16)

    return pl.pallas_call(
        body, name=name, grid=(M // tm,),
        in_specs=[pl.BlockSpec((tm, K), lambda i: (i, 0)), pl.BlockSpec((N, K), lambda i: (0, 0))],
        out_specs=pl.BlockSpec((N, tm), lambda i: (0, i)),
        out_shape=jax.ShapeDtypeStruct((N, M), BF16),
        compiler_params=_cp(("parallel",)),
    )(a, bt)


def _mm_tn(x, y, name, blocked_w=None, out_dtype=F32):
    T, Kx = x.shape
    N = y.shape[1]
    tt = min(TT, T)
    nt = T // tt
    tkx = min(Kx, max(LANES, MM_BLOCK_BYTES // (4 * N)))
    if blocked_w is not None:
        blk_shape, full_shape = (N // blocked_w, tkx, blocked_w), (N // blocked_w, Kx, blocked_w)
        o_spec = pl.BlockSpec(blk_shape, lambda i, t: (0, i, 0))
    else:
        blk_shape, full_shape = (tkx, N), (Kx, N)
        o_spec = pl.BlockSpec(blk_shape, lambda i, t: (i, 0))

    rk = min(tkx, TN_ROWS)

    def body(x_ref, y_ref, o_ref, acc_ref):
        t = pl.program_id(1)

        @pl.when(t == 0)
        def _():
            acc_ref[...] = jnp.zeros_like(acc_ref)

        for r in range(tkx // rk):
            rows = slice(r * rk, (r + 1) * rk)
            part = _dot_tn(x_ref[:, rows], y_ref[...])
            if blocked_w is None:
                acc_ref[rows, :] += part
            else:
                for j in range(N // blocked_w):
                    acc_ref[j, rows, :] += part[:, j * blocked_w:(j + 1) * blocked_w]

        @pl.when(t == nt - 1)
        def _():
            o_ref[...] = acc_ref[...].astype(out_dtype)

    return pl.pallas_call(
        body, name=name, grid=(Kx // tkx, nt),
        in_specs=[pl.BlockSpec((tt, tkx), lambda i, t: (t, i)),
                  pl.BlockSpec((tt, N), lambda i, t: (t, 0))],
        out_specs=o_spec, out_shape=jax.ShapeDtypeStruct(full_shape, out_dtype),
        scratch_shapes=[pltpu.VMEM(blk_shape, F32)],
        compiler_params=_cp(("parallel", "arbitrary")),
    )(x, y)


def _sgu_pieces(z, lng, lnb, wc, bs_ref):
    E = z.shape[1] // 2
    gd = E // GM_GROUPS
    zu, zv = z[:, :E], z[:, E:]
    u = _gelu(zu)
    v = _gelu(zv)
    mu = jnp.mean(v, axis=-1, keepdims=True)
    xc = v - mu
    rs = lax.rsqrt(jnp.mean(xc * xc, axis=-1, keepdims=True) + LN_EPS)
    xhat = xc * rs
    vln = xhat * lng + lnb
    s = []
    for g in range(GM_GROUPS):
        vg = vln[:, g * gd:(g + 1) * gd].astype(BF16)
        s.append(jnp.dot(wc[g], vg, preferred_element_type=F32) + bs_ref[g])
    return zu, zv, u, xhat, rs, vln, s


def _causal_ws(ws_ref):
    t = lax.broadcasted_iota(jnp.int32, (CHUNK, CHUNK), 0)
    s = lax.broadcasted_iota(jnp.int32, (CHUNK, CHUNK), 1)
    tri = t >= s
    return [jnp.where(tri, ws_ref[g], 0.0).astype(BF16) for g in range(GM_GROUPS)], tri


def _gm_in_fused(hn, w_in, lng, lnb, ws, bs, name):
    T, D = hn.shape
    nb, _, w = w_in.shape
    E2 = nb * w
    E = E2 // 2
    gd = E // GM_GROUPS
    tm = min(TM, T)

    def body(a_ref, w_ref, lng_ref, lnb_ref, ws_ref, bs_ref, z_ref, o_ref):
        av = a_ref[...]
        for j in range(nb):
            z_ref[:, j * w:(j + 1) * w] = jnp.dot(av, w_ref[j], preferred_element_type=F32)
        wc, _ = _causal_ws(ws_ref)
        for c in range(tm // CHUNK):
            rows = slice(c * CHUNK, (c + 1) * CHUNK)
            _, _, u, _, _, _, s = _sgu_pieces(z_ref[rows, :], lng_ref[...], lnb_ref[...], wc, bs_ref)
            for g in range(GM_GROUPS):
                cols = slice(g * gd, (g + 1) * gd)
                o_ref[rows, cols] = (u[:, cols] * s[g]).astype(BF16)

    full = lambda shape: pl.BlockSpec(shape, lambda i: (0,) * len(shape))
    return pl.pallas_call(
        body, name=name, grid=(T // tm,),
        in_specs=[pl.BlockSpec((tm, D), lambda i: (i, 0)), full((nb, D, w)), full((1, E)), full((1, E)),
                  full((GM_GROUPS, CHUNK, CHUNK)), full((GM_GROUPS, CHUNK, 1))],
        out_specs=[pl.BlockSpec((tm, E2), lambda i: (i, 0)), pl.BlockSpec((tm, E), lambda i: (i, 0))],
        out_shape=[jax.ShapeDtypeStruct((T, E2), F32), jax.ShapeDtypeStruct((T, E), BF16)],
        compiler_params=_cp(("parallel",)),
    )(hn, w_in, lng, lnb, ws, bs)


def _sgu_bwd(z, dh16, w_out, lng, lnb, ws, bs, name, after=None):
    T, E2 = z.shape
    D = dh16.shape[1]
    E = E2 // 2
    gd = E // GM_GROUPS
    tm = min(2 * CHUNK, T)
    nsteps = T // tm

    def body(z_ref, dh_ref, wo_ref, lng_ref, lnb_ref, ws_ref, bs_ref, *rest):
        dz_ref, dlng_ref, dlnb_ref, dws_ref, dbs_ref, dg_ref = rest[-6:]
        i = pl.program_id(0)

        @pl.when(i == 0)
        def _():
            dlng_ref[...] = jnp.zeros_like(dlng_ref)
            dlnb_ref[...] = jnp.zeros_like(dlnb_ref)
            dws_ref[...] = jnp.zeros_like(dws_ref)
            dbs_ref[...] = jnp.zeros_like(dbs_ref)

        dg_ref[...] = _dot_nt(dh_ref[...], wo_ref[...]).astype(BF16)
        wc, tri = _causal_ws(ws_ref)
        lng_v = lng_ref[...]
        for c in range(tm // CHUNK):
            rows = slice(c * CHUNK, (c + 1) * CHUNK)
            zu, zv, u, xhat, rs, vln, s = _sgu_pieces(z_ref[rows, :], lng_v, lnb_ref[...], wc, bs_ref)
            dgc = dg_ref[rows, :].astype(F32)
            du, dvln = [], []
            for g in range(GM_GROUPS):
                cols = slice(g * gd, (g + 1) * gd)
                dgg = dgc[:, cols]
                du.append(dgg * s[g])
                ds = dgg * u[:, cols]
                dsb = ds.astype(BF16)
                dws_ref[g] += _dot_nt(dsb, vln[:, cols].astype(BF16))
                dbs_ref[g] += jnp.sum(ds, axis=-1, keepdims=True)
                dvln.append(_dot_tn(wc[g], dsb))
            du = jnp.concatenate(du, axis=1)
            dvln = jnp.concatenate(dvln, axis=1)
            dlng_ref[...] += jnp.sum(dvln * xhat, axis=0, keepdims=True)
            dlnb_ref[...] += jnp.sum(dvln, axis=0, keepdims=True)
            dxh = dvln * lng_v
            m1 = jnp.mean(dxh, axis=-1, keepdims=True)
            m2 = jnp.mean(dxh * xhat, axis=-1, keepdims=True)
            dv = rs * (dxh - m1 - xhat * m2)
            dz_ref[rows, :E] = (du * _gelu_grad(zu)).astype(BF16)
            dz_ref[rows, E:] = (dv * _gelu_grad(zv)).astype(BF16)

        @pl.when(i == nsteps - 1)
        def _():
            for g in range(GM_GROUPS):
                dws_ref[g] = jnp.where(tri, dws_ref[g], 0.0)

    full = lambda shape: pl.BlockSpec(shape, lambda i: (0,) * len(shape))
    in_specs = [pl.BlockSpec((tm, E2), lambda i: (i, 0)), pl.BlockSpec((tm, D), lambda i: (i, 0)), full((E, D)),
                full((1, E)), full((1, E)), full((GM_GROUPS, CHUNK, CHUNK)), full((GM_GROUPS, CHUNK, 1))]
    args = [z, dh16, w_out, lng, lnb, ws, bs]
    if after is not None:
        in_specs.append(pl.BlockSpec(memory_space=pl.ANY))
        args.append(after)
    return pl.pallas_call(
        body, name=name, grid=(nsteps,),
        in_specs=in_specs,
        out_specs=[pl.BlockSpec((tm, E2), lambda i: (i, 0)), full((1, E)), full((1, E)),
                   full((GM_GROUPS, CHUNK, CHUNK)), full((GM_GROUPS, CHUNK, 1))],
        out_shape=[jax.ShapeDtypeStruct((T, E2), BF16), jax.ShapeDtypeStruct((1, E), F32),
                   jax.ShapeDtypeStruct((1, E), F32), jax.ShapeDtypeStruct((GM_GROUPS, CHUNK, CHUNK), F32),
                   jax.ShapeDtypeStruct((GM_GROUPS, CHUNK, 1), F32)],
        scratch_shapes=[pltpu.VMEM((tm, E), BF16)],
        compiler_params=_cp(("arbitrary",)),
    )(*args)


HALO = 16


def _conv_taps(a_ext, w_ref, b_ref):
    n = a_ext.shape[0]
    am1 = pltpu.roll(a_ext, 1, 0)
    am2 = pltpu.roll(a_ext, 2, 0)
    del n
    return ((b_ref[...] + am2 * w_ref[0:1, :]) + am1 * w_ref[1:2, :]) + a_ext * w_ref[2:3, :], am1, am2


def _ffn_up_fused(hn, wgu, cw, cb, name):
    T, D = hn.shape
    nb2, _, w = wgu.shape
    nb = nb2 // 2
    F = nb * w
    tm = min(TM, T)

    def body(a_ref, w_ref, cw_ref, cb_ref, au_ref, act_ref, halo_ref):
        @pl.when(pl.program_id(0) == 0)
        def _():
            halo_ref[...] = jnp.zeros_like(halo_ref)

        av = a_ref[...]
        for j in range(nb):
            cols = slice(j * w, (j + 1) * w)
            g = jnp.dot(av, w_ref[j], preferred_element_type=F32)
            u = jnp.dot(av, w_ref[nb + j], preferred_element_type=F32)
            au_ref[:, cols] = g
            au_ref[:, F + j * w:F + (j + 1) * w] = u
            ext = jnp.concatenate([halo_ref[:, cols], g], axis=0)
            am1 = pltpu.roll(ext, 1, 0)
            am2 = pltpu.roll(ext, 2, 0)
            conv = ((cb_ref[:, cols] + am2 * cw_ref[0:1, cols]) + am1 * cw_ref[1:2, cols]) + ext * cw_ref[2:3, cols]
            conv = conv[HALO:, :]
            act_ref[:, cols] = ((conv * _sigmoid(conv)) * u).astype(BF16)
            halo_ref[:, cols] = g[tm - HALO:, :]

    return pl.pallas_call(
        body, name=name, grid=(T // tm,),
        in_specs=[pl.BlockSpec((tm, D), lambda i: (i, 0)), pl.BlockSpec((nb2, D, w), lambda i: (0, 0, 0)),
                  pl.BlockSpec((3, F), lambda i: (0, 0)), pl.BlockSpec((1, F), lambda i: (0, 0))],
        out_specs=[pl.BlockSpec((tm, 2 * F), lambda i: (i, 0)), pl.BlockSpec((tm, F), lambda i: (i, 0))],
        out_shape=[jax.ShapeDtypeStruct((T, 2 * F), F32), jax.ShapeDtypeStruct((T, F), BF16)],
        scratch_shapes=[pltpu.VMEM((HALO, F), F32)],
        compiler_params=_cp(("arbitrary",)),
    )(hn, wgu, cw, cb)


def _ffn_mid_bwd(au, dh16, wd, cw, cb, name, after=None):
    T, F = au.shape[0], au.shape[1] // 2
    D = dh16.shape[1]
    tm, tf = min(TM, T), min(TF, F)
    hb = tm // HALO
    nt = T // tm
    nf = F // tf
    last_h = T // HALO - 1

    def body(a_ref, ap_ref, an_ref, u_ref, un_ref, dh_ref, dhn_ref, wd_ref, w_ref, b_ref, *rest):
        da_ref, du_ref, dcw_ref, dcb_ref = rest[-4:]
        i = pl.program_id(1)
        prev = jnp.where(i == 0, 0.0, ap_ref[...])
        a_main = a_ref[...]
        a_ext = jnp.concatenate([prev, a_main, an_ref[...]], axis=0)
        conv, am1, am2 = _conv_taps(a_ext, w_ref, b_ref)
        conv = conv[HALO:, :]
        sig = _sigmoid(conv)
        u_ext = jnp.concatenate([u_ref[...], un_ref[...]], axis=0)
        wd_f = wd_ref[...]
        d_ext = jnp.concatenate([_dot_nt(dh_ref[...], wd_f), _dot_nt(dhn_ref[...], wd_f)], axis=0)
        d_ext = d_ext.astype(BF16).astype(F32)
        n = tm + HALO
        row = lax.broadcasted_iota(jnp.int32, (n, 1), 0)
        live = jnp.logical_or(row < tm, i < nt - 1)
        dconv = jnp.where(live, d_ext * u_ext * (sig * (1.0 + conv * (1.0 - sig))), 0.0)
        du_ref[...] = (d_ext[:tm, :] * (conv[:tm, :] * sig[:tm, :])).astype(BF16)
        dp1 = pltpu.roll(dconv, n - 1, 0)[:tm, :]
        dp2 = pltpu.roll(dconv, n - 2, 0)[:tm, :]
        dc = dconv[:tm, :]
        da_ref[...] = ((dc * w_ref[2:3, :] + dp1 * w_ref[1:2, :]) + dp2 * w_ref[0:1, :]).astype(BF16)
        g2 = jnp.sum(dc * a_main, axis=0, keepdims=True)
        g1 = jnp.sum(dc * am1[HALO:HALO + tm, :], axis=0, keepdims=True)
        g0 = jnp.sum(dc * am2[HALO:HALO + tm, :], axis=0, keepdims=True)
        gb = jnp.sum(dc, axis=0, keepdims=True)

        @pl.when(i == 0)
        def _():
            dcw_ref[...] = jnp.zeros_like(dcw_ref)
            dcb_ref[...] = jnp.zeros_like(dcb_ref)

        dcw_ref[0:1, :] += g0
        dcw_ref[1:2, :] += g1
        dcw_ref[2:3, :] += g2
        dcb_ref[...] += gb

    main = pl.BlockSpec((tm, tf), lambda f, i: (i, f))
    prev = pl.BlockSpec((HALO, tf), lambda f, i: (jnp.maximum(i * hb - 1, 0), f))
    nxt = pl.BlockSpec((HALO, tf), lambda f, i: (jnp.minimum((i + 1) * hb, last_h), f))
    main_u = pl.BlockSpec((tm, tf), lambda f, i: (i, nf + f))
    nxt_u = pl.BlockSpec((HALO, tf), lambda f, i: (jnp.minimum((i + 1) * hb, last_h), nf + f))
    in_specs = [main, prev, nxt, main_u, nxt_u,
                pl.BlockSpec((tm, D), lambda f, i: (i, 0)),
                pl.BlockSpec((HALO, D), lambda f, i: (jnp.minimum((i + 1) * hb, last_h), 0)),
                pl.BlockSpec((tf, D), lambda f, i: (f, 0)),
                pl.BlockSpec((3, tf), lambda f, i: (0, f)), pl.BlockSpec((1, tf), lambda f, i: (0, f))]
    args = [au, au, au, au, au, dh16, dh16, wd, cw, cb]
    if after is not None:
        in_specs.append(pl.BlockSpec(memory_space=pl.ANY))
        args.append(after)
    return pl.pallas_call(
        body, name=name, grid=(nf, nt),
        in_specs=in_specs,
        out_specs=[main, main, pl.BlockSpec((3, tf), lambda f, i: (0, f)), pl.BlockSpec((1, tf), lambda f, i: (0, f))],
        out_shape=[jax.ShapeDtypeStruct((T, F), BF16), jax.ShapeDtypeStruct((T, F), BF16),
                   jax.ShapeDtypeStruct((3, F), F32), jax.ShapeDtypeStruct((1, F), F32)],
        compiler_params=_cp(("parallel", "arbitrary")),
    )(*args)


def _split3(x):
    hi = x.astype(BF16)
    r1 = x - hi.astype(F32)
    mid = r1.astype(BF16)
    lo = (r1 - mid.astype(F32)).astype(BF16)
    return hi, mid, lo


def _tri_ones(n, upper):
    r = lax.broadcasted_iota(jnp.int32, (n, n), 0)
    c = lax.broadcasted_iota(jnp.int32, (n, n), 1)
    return jnp.where((r <= c) if upper else (r >= c), 1.0, 0.0).astype(BF16)


def _gate_scan(f, bf, name):
    T = f.shape[0]
    tm = min(256, T)

    def body(f_ref, b_ref, cp_ref, sn_ref, carry_ref):
        i = pl.program_id(0)

        @pl.when(i == 0)
        def _():
            carry_ref[...] = jnp.zeros_like(carry_ref)

        x = f_ref[...] + b_ref[...]
        e = jnp.exp(-jnp.abs(x))
        logf = jnp.minimum(x, 0.0) - jnp.log(1.0 + e)
        sn_ref[...] = jnp.where(x >= 0.0, e / (1.0 + e), 1.0 / (1.0 + e))
        tri = _tri_ones(tm, upper=False)
        c = carry_ref[...]
        for piece in _split3(logf):
            c = c + jnp.dot(tri, piece, preferred_element_type=F32)
        carry_ref[...] += jnp.sum(logf, axis=0, keepdims=True)
        hi, mid, lo = _split3(c * LOG2E)
        cp_ref[:, 0:LANES] = hi
        cp_ref[:, LANES:2 * LANES] = mid
        cp_ref[:, 2 * LANES:3 * LANES] = lo

    return pl.pallas_call(
        body, name=name, grid=(T // tm,),
        in_specs=[pl.BlockSpec((tm, LANES), lambda i: (i, 0)), pl.BlockSpec((1, LANES), lambda i: (0, 0))],
        out_specs=[pl.BlockSpec((tm, 3 * LANES), lambda i: (i, 0)), pl.BlockSpec((tm, LANES), lambda i: (i, 0))],
        out_shape=[jax.ShapeDtypeStruct((T, 3 * LANES), BF16), jax.ShapeDtypeStruct((T, LANES), F32)],
        scratch_shapes=[pltpu.VMEM((1, LANES), F32)],
        compiler_params=_cp(("arbitrary",)),
    )(f, bf)


def _gate_scan_bwd(dcq, dck, sneg, name):
    T = dcq.shape[0]
    tm = min(256, T)
    n = T // tm

    def body(dcq_ref, dck_ref, sn_ref, df_ref, db_ref, carry_ref):
        i = pl.program_id(0)

        @pl.when(i == 0)
        def _():
            carry_ref[...] = jnp.zeros_like(carry_ref)
            db_ref[...] = jnp.zeros_like(db_ref)

        tri = _tri_ones(tm, upper=True)
        dcb = dcq_ref[...] - dck_ref[...]
        acc = carry_ref[...]
        for piece in _split3(dcb):
            acc = acc + jnp.dot(tri, piece, preferred_element_type=F32)
        carry_ref[...] += jnp.sum(dcb, axis=0, keepdims=True)
        df = acc * sn_ref[...]
        df_ref[...] = df.astype(BF16)
        db_ref[...] += jnp.sum(df, axis=0, keepdims=True)

    rev = pl.BlockSpec((tm, LANES), lambda i: (n - 1 - i, 0))
    return pl.pallas_call(
        body, name=name, grid=(n,),
        in_specs=[rev, rev, rev],
        out_specs=[rev, pl.BlockSpec((1, LANES), lambda i: (0, 0))],
        out_shape=[jax.ShapeDtypeStruct((T, LANES), BF16), jax.ShapeDtypeStruct((1, LANES), F32)],
        scratch_shapes=[pltpu.VMEM((1, LANES), F32)],
        compiler_params=_cp(("arbitrary",)),
    )(dcq, dck, sneg)


def _qk_proj(hn, w_pads, cp, sels, consts, scales, name):
    T, D = hn.shape
    H = w_pads[0].shape[1] // LANES
    tm = min(TM, T)

    def body(a_ref, cp_ref, wq_ref, wk_ref, sq_ref, sk_ref, cq_ref, ck_ref, qo_ref, ko_ref):
        a = a_ref[...]
        cpv = cp_ref[...]
        for w_ref, sel_ref, c_ref, o_ref, scale in ((wq_ref, sq_ref, cq_ref, qo_ref, scales[0]),
                                                    (wk_ref, sk_ref, ck_ref, ko_ref, scales[1])):
            for p in range(H // 2):
                acc = jnp.dot(a, w_ref[:, p * 2 * LANES:(p + 1) * 2 * LANES], preferred_element_type=F32)
                if scale != 1.0:
                    acc = acc * scale
                acc = acc + jnp.dot(cpv, sel_ref[p], preferred_element_type=F32) + c_ref[p]
                o_ref[2 * p] = acc[:, :LANES].astype(BF16)
                o_ref[2 * p + 1] = acc[:, LANES:].astype(BF16)

    whole = lambda t: pl.BlockSpec(t.shape, lambda i: (0,) * t.ndim)
    out = jax.ShapeDtypeStruct((H, T, LANES), BF16)
    o_spec = pl.BlockSpec((H, tm, LANES), lambda i: (0, i, 0))
    return pl.pallas_call(
        body, name=name, grid=(T // tm,),
        in_specs=[pl.BlockSpec((tm, D), lambda i: (i, 0)), pl.BlockSpec((tm, 3 * LANES), lambda i: (i, 0)),
                  whole(w_pads[0]), whole(w_pads[1]), whole(sels[0]), whole(sels[1]), whole(consts[0]), whole(consts[1])],
        out_specs=[o_spec, o_spec], out_shape=[out, out],
        compiler_params=_cp(("parallel",)),
    )(hn, cp, w_pads[0], w_pads[1], sels[0], sels[1], consts[0], consts[1])


def _lane_lo():
    return lax.broadcasted_iota(jnp.int32, (1, LANES), 1) < HEAD_DIM


def _attn_fwd_t(qp, kp, vt, name):
    H, T, _ = qp.shape
    tq = min(TQ, T)
    hd = HEAD_DIM
    ext = hd + 16

    def body(q_ref, k_ref, vt_ref, o_ref, o32_ref, lse_ref, m_sc, acc_sc):
        i = pl.program_id(1)
        m_sc[...] = jnp.full(m_sc.shape, NEG, F32)
        acc_sc[...] = jnp.zeros_like(acc_sc)
        q_t = [jnp.transpose(q_ref[h].astype(F32)).astype(BF16) for h in range(2)]
        ones_rows = jnp.where(lax.broadcasted_iota(jnp.int32, (16, tq), 0) == 0, 1.0, 0.0).astype(BF16)

        def steps(blocks):
            offs = [pl.multiple_of(j * tq, tq) for j, _ in blocks]
            s_all = [[jnp.dot(k_ref[h, pl.ds(off, tq), :], q_t[h], preferred_element_type=F32) for h in range(2)]
                     for off in offs]
            for (j, masked), off, s_blk in zip(blocks, offs, s_all):
                for h in range(2):
                    s = s_blk[h]
                    if masked:
                        kr = lax.broadcasted_iota(jnp.int32, (tq, tq), 0)
                        qc = lax.broadcasted_iota(jnp.int32, (tq, tq), 1)
                        s = jnp.where(qc >= kr, s, NEG)
                    m_prev = m_sc[h]
                    m_new = jnp.maximum(m_prev, jnp.max(s, axis=0, keepdims=True))
                    alpha = jnp.exp2(m_prev - m_new)
                    p16 = jnp.exp2(s - m_new).astype(BF16)
                    v_aug = jnp.concatenate([vt_ref[h * hd:(h + 1) * hd, pl.ds(off, tq)], ones_rows], axis=0)
                    pv = jnp.dot(v_aug, p16, preferred_element_type=F32)
                    acc_sc[h] = alpha * acc_sc[h] + pv
                    m_sc[h] = m_new

        def group_body(t, carry):
            steps([(KV_UNROLL * t + u, False) for u in range(KV_UNROLL)])
            return carry

        lax.fori_loop(0, i // KV_UNROLL, group_body, 0)
        for rem in range(KV_UNROLL):

            @pl.when(i % KV_UNROLL == rem)
            def _(rem=rem):
                steps([(i - rem + u, u == rem) for u in range(rem + 1)])

        o_t, lse_t = [], []
        for h in range(2):
            acc = acc_sc[h]
            l = acc[hd:hd + 1, :]
            o_t.append(acc[:hd, :] / l)
            lse_t.append(jnp.broadcast_to(m_sc[h] + jnp.log(l) * LOG2E, (hd, tq)))
        o = jnp.transpose(jnp.concatenate(o_t, axis=0))
        o_ref[...] = o.astype(BF16)
        o32_ref[...] = o
        lse_ref[...] = jnp.transpose(jnp.concatenate(lse_t, axis=0))

    oblk = pl.BlockSpec((tq, LANES), lambda p, i: (i, p))
    return pl.pallas_call(
        body, name=name, grid=(H // 2, T // tq),
        in_specs=[pl.BlockSpec((2, tq, LANES), lambda p, i: (p, i, 0)),
                  pl.BlockSpec((2, T, LANES), lambda p, i: (p, 0, 0)),
                  pl.BlockSpec((2 * hd, T), lambda p, i: (p, 0))],
        out_specs=[oblk, oblk, pl.BlockSpec((None, tq, LANES), lambda p, i: (p, i, 0))],
        out_shape=[jax.ShapeDtypeStruct((T, H * HEAD_DIM), BF16), jax.ShapeDtypeStruct((T, H * HEAD_DIM), F32),
                   jax.ShapeDtypeStruct((H // 2, T, LANES), F32)],
        scratch_shapes=[pltpu.VMEM((2, 1, tq), F32), pltpu.VMEM((2, ext, tq), F32)],
        compiler_params=_cp(("parallel", "arbitrary")),
    )(qp, kp, vt)


def _attn_bwd(qp, kp, vt, o, do, lse, scale, name):
    H, T, _ = qp.shape
    tq = min(TQ, T)
    nq = T // tq
    nrep = tq // LANES

    def body(q_ref, k_ref, vt_ref, o_ref, do_ref, lse_ref, dq_ref, dk_ref, dv_ref, dqe_ref, dke_ref, dk_sc, dv_sc, dq_sc):
        i = pl.program_id(1)

        @pl.when(i == 0)
        def _():
            dk_sc[...] = jnp.zeros_like(dk_sc)
            dv_sc[...] = jnp.zeros_like(dv_sc)

        dq_sc[...] = jnp.zeros_like(dq_sc)

        lo = _lane_lo()
        dob = do_ref[...]
        dof = dob.astype(F32)
        prod = dof * o_ref[...].astype(F32)
        lse2 = lse_ref[...]
        lse2_sw = pltpu.roll(lse2, HEAD_DIM, 1)
        zero = jnp.zeros_like(dob)
        do_h = [jnp.where(lo, dob, zero), jnp.where(lo, zero, dob)]
        rep = lambda col: jnp.broadcast_to(col, (tq, LANES))
        delta = [rep(jnp.sum(jnp.where(lo, prod, 0.0), axis=-1, keepdims=True)),
                 rep(jnp.sum(jnp.where(lo, 0.0, prod), axis=-1, keepdims=True))]
        lse_h = [jnp.where(lo, lse2, lse2_sw), jnp.where(lo, lse2_sw, lse2)]
        qs = [q_ref[0], q_ref[1]]
        tr16 = lambda a: jnp.transpose(a.astype(F32)).astype(BF16)
        q_t = [tr16(qs[0]), tr16(qs[1])]
        do_t = [tr16(do_h[0]), tr16(do_h[1])]

        def steps(blocks):
            offs = [pl.multiple_of(j * tq, tq) for j, _ in blocks]
            vblks = [vt_ref[:, pl.ds(off, tq)] for off in offs]
            kblks = [[k_ref[h, pl.ds(off, tq), :] for h in range(2)] for off in offs]
            s_all = [[_dot_nt(qs[h], kb[h]) for h in range(2)] for kb in kblks]
            dp_all = [[jnp.dot(do_h[h], vb, preferred_element_type=F32) for h in range(2)] for vb in vblks]
            for b, ((j, masked), off) in enumerate(zip(blocks, offs)):
                dv_add = None
                for h in range(2):
                    kblk, s, dp = kblks[b][h], s_all[b][h], dp_all[b][h]
                    p16, ds16 = [], []
                    for c in range(nrep):
                        cols = slice(c * LANES, (c + 1) * LANES)
                        p = jnp.exp2(s[:, cols] - lse_h[h])
                        if masked:
                            r = lax.broadcasted_iota(jnp.int32, (tq, LANES), 0)
                            cc = lax.broadcasted_iota(jnp.int32, (tq, LANES), 1)
                            p = jnp.where(r >= cc + c * LANES, p, 0.0)
                        p16.append(p.astype(BF16))
                        ds16.append((p * (dp[:, cols] - delta[h])).astype(BF16))
                    p16 = jnp.concatenate(p16, axis=1)
                    dsb = jnp.concatenate(ds16, axis=1)
                    dq_sc[h] += jnp.dot(dsb, kblk, preferred_element_type=F32)
                    dk_sc[h, :, pl.ds(off, tq)] += jnp.dot(q_t[h], dsb, preferred_element_type=F32)
                    pv = jnp.dot(do_t[h], p16, preferred_element_type=F32)
                    dv_add = pv if dv_add is None else dv_add + pv
                dv_sc[:, pl.ds(off, tq)] += dv_add

        def group_body(t, carry):
            steps([(KV_UNROLL_BWD * t + u, False) for u in range(KV_UNROLL_BWD)])
            return carry

        lax.fori_loop(0, i // KV_UNROLL_BWD, group_body, 0)
        for rem in range(KV_UNROLL_BWD):

            @pl.when(i % KV_UNROLL_BWD == rem)
            def _(rem=rem):
                steps([(i - rem + u, u == rem) for u in range(rem + 1)])

        dq0, dq1 = dq_sc[0], dq_sc[1]
        dq_ref[...] = (jnp.where(lo, dq0, pltpu.roll(dq1, HEAD_DIM, 1)) * scale).astype(BF16)
        row8 = lax.broadcasted_iota(jnp.int32, (8, 1), 0)
        pick = lambda blk, r: jnp.sum(jnp.where(row8 == r, blk, 0.0), axis=0, keepdims=True)
        two_rows = lambda a, b: jnp.where(row8 == 0, a, jnp.where(row8 == 1, b, 0.0))
        gate_rows = slice(HEAD_DIM, HEAD_DIM + 8)
        dqe_ref[...] = two_rows(pick(jnp.transpose(dq0)[gate_rows, :], 0), pick(jnp.transpose(dq1)[gate_rows, :], 0))

        @pl.when(i == nq - 1)
        def _():
            dke_ref[...] = two_rows(pick(dk_sc[0, gate_rows, :], 3), pick(dk_sc[1, gate_rows, :], 3))
            for cb in range(nq):
                tok = slice(cb * tq, (cb + 1) * tq)
                dk0 = jnp.transpose(dk_sc[0, :, tok])
                dk1 = jnp.transpose(dk_sc[1, :, tok])
                dk_ref[tok, :] = (jnp.where(lo, dk0, pltpu.roll(dk1, HEAD_DIM, 1)) * LN2).astype(BF16)
                dv_ref[tok, :] = jnp.transpose(dv_sc[:, tok]).astype(BF16)

    qblk = pl.BlockSpec((tq, LANES), lambda p, i: (i, p))
    pair = pl.BlockSpec((T, LANES), lambda p, i: (0, p))
    tok16 = jax.ShapeDtypeStruct((T, H * HEAD_DIM), BF16)
    gate32 = jax.ShapeDtypeStruct((H // 2, 8, T), F32)
    return pl.pallas_call(
        body, name=name, grid=(H // 2, nq),
        in_specs=[pl.BlockSpec((2, tq, LANES), lambda p, i: (p, i, 0)),
                  pl.BlockSpec((2, T, LANES), lambda p, i: (p, 0, 0)),
                  pl.BlockSpec((LANES, T), lambda p, i: (p, 0)), qblk, qblk,
                  pl.BlockSpec((None, tq, LANES), lambda p, i: (p, i, 0))],
        out_specs=[qblk, pair, pair, pl.BlockSpec((None, 8, tq), lambda p, i: (p, 0, i)),
                   pl.BlockSpec((None, 8, T), lambda p, i: (p, 0, 0))],
        out_shape=[tok16, tok16, tok16, gate32, gate32],
        scratch_shapes=[pltpu.VMEM((2, LANES, T), F32), pltpu.VMEM((LANES, T), F32),
                        pltpu.VMEM((2, tq, LANES), F32)],
        compiler_params=_cp(("parallel", "arbitrary")),
    )(qp, kp, vt, o, do, lse)


def _mesh_pos():
    return lax.axis_index("x"), lax.axis_index("y"), lax.axis_index("c")


def _all_gather(arrs, name, groups=None):
    n = len(arrs)
    if groups is None:
        groups = [(a, 0) for a in range(n)]
    ng = 1 + max(g for g, _ in groups)
    per_group = [sum(1 for g, _ in groups if g == gi) for gi in range(ng)]
    first_of = [next(a for a in range(n) if groups[a][0] == gi) for gi in range(ng)]

    def body(*refs):
        ins, outs = refs[:n], refs[n:n + ng]
        send_sems, recv_sems, local_sems = refs[n + ng:]
        x, y, c = _mesh_pos()
        me, sib = (x, y, c), (x, y, 1 - c)
        chips = [(1 - x, y), (x, 1 - y), (1 - x, 1 - y)]

        def dst_of(a, px, py, pc):
            g, k = groups[a]
            return outs[g].at[N_DEV * k + 4 * px + 2 * py + pc]

        def copy(a, k, block, to, src=None):
            dst = dst_of(a, *block)
            return pltpu.make_async_remote_copy(
                src_ref=dst if src is None else src, dst_ref=dst,
                send_sem=send_sems.at[a, k], recv_sem=recv_sems.at[a, k], device_id=to, device_id_type=MESH)

        mine = [pltpu.make_async_copy(ins[a], dst_of(a, *me), local_sems.at[a]) for a in range(n)]
        for cp in mine:
            cp.start()
        first = []
        for a in range(n):
            first.append(copy(a, 0, me, sib, src=ins[a]))
            first += [copy(a, 1 + j, me, (*chip, c), src=ins[a]) for j, chip in enumerate(chips)]
        for cp in first:
            cp.start()
        passed = []
        for j, chip in enumerate(chips):
            for a in range(n):
                copy(a, 1 + j, (*chip, c), me).wait_recv()
                fwd = copy(a, 4 + j, (*chip, c), sib)
                fwd.start()
                passed.append(fwd)
        for a in range(n):
            copy(a, 0, sib, me).wait_recv()
            for j, chip in enumerate(chips):
                copy(a, 4 + j, (*chip, 1 - c), me).wait_recv()
        for cp in first + passed:
            cp.wait_send()
        for cp in mine:
            cp.wait()

    any_spec = pl.BlockSpec(memory_space=pl.ANY)
    return pl.pallas_call(
        body, name=name,
        in_specs=[any_spec] * n, out_specs=[any_spec] * ng,
        out_shape=[jax.ShapeDtypeStruct((N_DEV * per_group[gi],) + arrs[first_of[gi]].shape, arrs[first_of[gi]].dtype)
                   for gi in range(ng)],
        scratch_shapes=[pltpu.SemaphoreType.DMA((n, 7)), pltpu.SemaphoreType.DMA((n, 7)),
                        pltpu.SemaphoreType.DMA((n,))],
    )(*arrs)


HBM_SPEC = pl.BlockSpec(memory_space=pltpu.HBM)
SEM_SPEC = pl.BlockSpec(memory_space=pltpu.SEMAPHORE)
ANY_SPEC = pl.BlockSpec(memory_space=pl.ANY)
DATAFLOW_EFFECT = pltpu.SideEffectType.DATAFLOW_SIDE_EFFECTING


def _peers():
    x, y, c = _mesh_pos()
    flip = lambda v, b: 1 - v if b else v
    return [(flip(x, (k >> 2) & 1), flip(y, (k >> 1) & 1), flip(c, k & 1)) for k in range(1, N_DEV)]


def _slot(p):
    return 4 * p[0] + 2 * p[1] + p[2]


def _direct_copy(src_refs, land_refs, sems, a, k, p, land_of, dst_slot, src_slot):
    s = src_slot(a, p)
    return pltpu.make_async_remote_copy(
        src_ref=src_refs[a] if s is None else src_refs[a].at[s], dst_ref=land_refs[land_of[a]].at[dst_slot(a, k)],
        send_sem=sems[0].at[a * (N_DEV - 1) + k], recv_sem=sems[1].at[a * (N_DEV - 1) + k], device_id=p,
        device_id_type=MESH)


def _direct_start(srcs, lands, land_of, dst_slot, src_slot, after, name, collective_id):
    n, nl = len(srcs), len(lands)

    def body(*refs):
        src_refs, land_refs = refs[:n], refs[n:n + nl]
        sems = (refs[n + nl + 1], refs[n + nl + 2])
        token = refs[-1]
        peers = _peers()
        barrier = pltpu.get_barrier_semaphore()
        for p in peers:
            pl.semaphore_signal(barrier, inc=1, device_id=p, device_id_type=MESH)
        pl.semaphore_wait(barrier, N_DEV - 1)
        for a in range(n):
            for k, p in enumerate(peers):
                _direct_copy(src_refs, land_refs, sems, a, k, p, land_of, dst_slot, src_slot).start()
        token[...] = jnp.zeros_like(token)

    hbm = lambda t: pltpu.HBM(t.shape, t.dtype)
    sem_t = pltpu.SemaphoreType.DMA((n * (N_DEV - 1),))
    outs = pl.pallas_call(
        body, name=name,
        out_shape=(sem_t, sem_t, *[hbm(t) for t in srcs], *[hbm(t) for t in lands], jax.ShapeDtypeStruct((8, LANES), F32)),
        in_specs=[HBM_SPEC] * (n + nl) + [ANY_SPEC],
        out_specs=(SEM_SPEC, SEM_SPEC, *([HBM_SPEC] * (n + nl)), pl.BlockSpec(memory_space=pltpu.VMEM)),
        input_output_aliases={i: 2 + i for i in range(n + nl)},
        compiler_params=pltpu.CompilerParams(has_side_effects=DATAFLOW_EFFECT, collective_id=collective_id),
    )(*[pltpu.with_memory_space_constraint(t, pltpu.HBM) for t in srcs],
      *[pltpu.with_memory_space_constraint(t, pltpu.HBM) for t in lands], after)
    return outs[0], outs[1], list(outs[2:2 + n]), list(outs[2 + n:2 + n + nl]), outs[-1]


def _direct_wait(send_sems, recv_sems, srcs, lands, land_of, idxs, dst_slot, src_slot, after, name):
    land_ids = []
    for a in idxs:
        if land_of[a] not in land_ids:
            land_ids.append(land_of[a])
    m, ml = len(idxs), len(land_ids)
    sub_land_of = {j: land_ids.index(land_of[a]) for j, a in enumerate(idxs)}

    def body(*refs):
        src_refs, land_refs = refs[:m], refs[m:m + ml]
        ssem, rsem = refs[m + ml], refs[m + ml + 1]
        for j, a in enumerate(idxs):
            for k, p in enumerate(_peers()):
                s = src_slot(a, p)
                cp = pltpu.make_async_remote_copy(
                    src_ref=src_refs[j] if s is None else src_refs[j].at[s],
                    dst_ref=land_refs[sub_land_of[j]].at[dst_slot(a, k)],
                    send_sem=ssem.at[a * (N_DEV - 1) + k], recv_sem=rsem.at[a * (N_DEV - 1) + k], device_id=p,
                    device_id_type=MESH)
                cp.wait_send()
                cp.wait_recv()

    hbm = lambda t: pltpu.HBM(t.shape, t.dtype)
    sub_s, sub_l = [srcs[a] for a in idxs], [lands[g] for g in land_ids]
    outs = pl.pallas_call(
        body, name=name,
        out_shape=(*[hbm(t) for t in sub_s], *[hbm(t) for t in sub_l]),
        in_specs=[HBM_SPEC] * (m + ml) + [SEM_SPEC, SEM_SPEC, ANY_SPEC],
        out_specs=tuple([HBM_SPEC] * (m + ml)),
        input_output_aliases={i: i for i in range(m + ml)},
        compiler_params=pltpu.CompilerParams(has_side_effects=DATAFLOW_EFFECT),
    )(*sub_s, *sub_l, send_sems, recv_sems, after)
    return list(outs[:m]), list(outs[m:])


def _row_block(R, C):
    best = None
    for d in range(16, R + 1, 16):
        if R % d == 0 and d * C <= 256 * 1024:
            best = d
    return best if best is not None else R


def _adamw_math(w, g, m, v):
    m = ADAM_B1 * m + (1.0 - ADAM_B1) * g
    v = ADAM_B2 * v + (1.0 - ADAM_B2) * (g * g)
    m_hat = m / (1.0 - ADAM_B1 ** ADAM_STEP)
    v_hat = v / (1.0 - ADAM_B2 ** ADAM_STEP)
    delta = -ADAM_LR * (m_hat / (jnp.sqrt(v_hat) + ADAM_EPS) + ADAM_WD * w)
    return delta, m, v


def _sum_adamw(parts, w, m, v, name, sel=None):
    R, C = w.shape
    nseg = max(len(arr) if isinstance(arr, list) else 1 for arr, _ in parts)
    tr = _row_block(R // nseg, C)
    bps = R // nseg // tr
    specs, args = [], []
    for arr, idxs in parts:
        pieces = arr if isinstance(arr, list) else [arr] * nseg
        for idx in idxs:
            for sg in range(nseg if isinstance(arr, list) else 1):
                row = (lambda i, sg=sg: jnp.clip(i - sg * bps, 0, bps - 1)) if isinstance(arr, list) else (lambda i: i)
                if idx < 0:
                    specs.append(pl.BlockSpec((None, tr, C), lambda i, s, row=row: (s[0], row(i), 0)))
                else:
                    specs.append(pl.BlockSpec((None, tr, C), lambda i, s, idx=idx, row=row: (idx, row(i), 0)))
                args.append(pieces[sg])
    seg_counts = [(nseg if isinstance(arr, list) else 1) for arr, idxs in parts for _ in idxs]
    npart = len(args)
    blk = pl.BlockSpec((tr, C), lambda i, s: (i, 0))

    def body(s_ref, *refs):
        del s_ref
        seg = pl.program_id(0) // bps
        g, at = None, 0
        for cnt in seg_counts:
            term = refs[at][...].astype(F32)
            for sg in range(1, cnt):
                term = jnp.where(seg == sg, refs[at + sg][...].astype(F32), term)
            g = term if g is None else g + term
            at += cnt
        w_ref, m_ref, v_ref, g_out, d_out, m_out, v_out = refs[npart:]
        delta, mm, vv = _adamw_math(w_ref[...], g, m_ref[...], v_ref[...])
        g_out[...] = g
        d_out[...] = delta
        m_out[...] = mm
        v_out[...] = vv

    grid_spec = pltpu.PrefetchScalarGridSpec(
        num_scalar_prefetch=1, grid=(R // tr,),
        in_specs=specs + [blk, blk, blk], out_specs=[blk] * 4)
    if sel is None:
        sel = jnp.zeros((1,), jnp.int32)
    return pl.pallas_call(
        body, name=name, grid_spec=grid_spec,
        out_shape=[jax.ShapeDtypeStruct((R, C), F32)] * 4,
        compiler_params=_cp(("parallel",)),
    )(sel, *args, w, m, v)


def _rows(a, c):
    return a.reshape(-1, c)


def _pad_rows(a, r):
    return jnp.pad(a, ((0, r - a.shape[0]), (0, 0))) if a.shape[0] != r else a


def _gate_tables():
    hp = N_HEADS // 2
    sel_q = np.zeros((hp, 3 * LANES, 2 * LANES), np.float32)
    sel_k = np.zeros((hp, 3 * LANES, 2 * LANES), np.float32)
    const_q = np.zeros((hp, 1, 2 * LANES), np.float32)
    const_k = np.zeros((hp, 1, 2 * LANES), np.float32)
    for p in range(hp):
        for hh in range(2):
            h = 2 * p + hh
            base = hh * LANES + HEAD_DIM
            for piece in range(3):
                sel_q[p, piece * LANES + h, base + piece] = 1.0
                sel_k[p, piece * LANES + h, base + 3 + piece] = -1.0
            const_k[p, 0, base:base + 3] = 1.0
            const_q[p, 0, base + 3:base + 6] = 1.0
    as_bf = lambda t: jnp.asarray(t, BF16)
    return as_bf(sel_q), as_bf(sel_k), jnp.asarray(const_q), jnp.asarray(const_k)


def _pad_heads(w):
    d = w.shape[0]
    w3 = w.reshape(d, N_HEADS, HEAD_DIM)
    return jnp.pad(w3, ((0, 0), (0, 0), (0, LANES - HEAD_DIM))).reshape(d, N_HEADS * LANES)


def kernel(x, mix_norm_g, ffn_norm_g, gm_w_in, gm_ln_g, gm_ln_b, gm_w_s, gm_b_s, gm_w_out, fox_w_qkvf, fox_b_f, fox_w_o, ffn_w_gate, ffn_w_up, ffn_conv_w, ffn_conv_b, ffn_w_down, final_norm_g, loss_target, m_mix_norm_g, m_ffn_norm_g, m_gm_w_in, m_gm_ln_g, m_gm_ln_b, m_gm_w_s, m_gm_b_s, m_gm_w_out, m_fox_w_qkvf, m_fox_b_f, m_fox_w_o, m_ffn_w_gate, m_ffn_w_up, m_ffn_conv_w, m_ffn_conv_b, m_ffn_w_down, m_final_norm_g, v_mix_norm_g, v_ffn_norm_g, v_gm_w_in, v_gm_ln_g, v_gm_ln_b, v_gm_w_s, v_gm_b_s, v_gm_w_out, v_fox_w_qkvf, v_fox_b_f, v_fox_w_o, v_ffn_w_gate, v_ffn_w_up, v_ffn_conv_w, v_ffn_conv_b, v_ffn_w_down, v_final_norm_g):
    T, D = x.shape[1], x.shape[2]
    E = gm_ln_g.shape[1]
    FF = ffn_conv_b.shape[1]
    NQKVF = 3 * D + N_HEADS
    xi, yi, ci = _mesh_pos()
    me = 4 * xi + 2 * yi + ci
    h0 = x.reshape(T, D)
    tgt = loss_target.reshape(T, D)

    nl = ffn_w_gate.shape[0]
    to16 = lambda a: a.astype(BF16)
    n_cw_rows = ffn_conv_w.size // LANES
    cw_rows = _pad_rows(_rows(ffn_conv_w.astype(F32), LANES), 16)
    (w_in_g,) = _all_gather([to16(gm_w_in[0])], "ag_weights")
    later, land_of, land_off, lands, src_ids = [], [], [], [], {}

    def add_sources(key, srcs, new_lands, of, offs):
        src_ids[key] = list(range(len(later), len(later) + len(srcs)))
        land_of.extend(len(lands) + o for o in of)
        land_off.extend(offs)
        later.extend(srcs)
        lands.extend(new_lands)

    def add_ffn(l):
        add_sources(f"ffn{l}", [to16(ffn_w_gate[l]), to16(ffn_w_up[l]), to16(ffn_w_down[l])],
                    [lax.empty((2 * N_DEV, D, FF // N_DEV), BF16), lax.empty((N_DEV, FF // N_DEV, D), BF16)],
                    [0, 0, 1], [0, N_DEV, 0])

    add_sources("gm_out", [to16(gm_w_out[0]), cw_rows],
                [lax.empty((N_DEV, E // N_DEV, D), BF16), lax.empty((N_DEV,) + cw_rows.shape, F32)], [0, 1], [0, 0])
    add_ffn(0)
    add_sources("fox", [to16(fox_w_qkvf[0]), to16(fox_w_o[0])],
                [lax.empty((N_DEV, D, NQKVF // N_DEV), BF16), lax.empty((N_DEV, D // N_DEV, D), BF16)], [0, 1], [0, 0])
    for l in range(1, nl):
        add_ffn(l)
    ag_dst = lambda a, k: land_off[a] + _slot(_mesh_pos())
    ag_src = lambda a, p: None
    ag_send, ag_recv, later, lands, ag_token = _direct_start(later, lands, land_of, ag_dst, ag_src, w_in_g,
                                                             "ag_later_start", collective_id=1)

    def own_blocks(landed, shards, offs):
        for s, o in zip(shards, offs):
            landed = lax.dynamic_update_index_in_dim(landed, s, o + me, 0)
        return landed

    def gather_wait(idxs, after, name):
        return _direct_wait(ag_send, ag_recv, later, lands, land_of, idxs, ag_dst, ag_src, after, name)

    ffn_w = {}

    def ffn_weights(l):
        return ffn_w[l]

    def land_ffn(l, shards, gu_land, dn_land):
        ffn_w[l] = (own_blocks(gu_land, shards[:2], [0, N_DEV]), own_blocks(dn_land, shards[2:3], [0]).reshape(FF, D))

    saved = {}

    def ffn_fwd(l, h_in, hn, next_g):
        wgul, wdl = ffn_weights(l)
        au, act = _ffn_up_fused(hn, wgul, conv_w_full[l], ffn_conv_b[l:l + 1], f"ffn{l}_up")
        saved[f"ffn{l}"] = (h_in, hn, au, act)
        if next_g is None:
            return _mm_nn(act, wdl, f"ffn{l}_down", res=h_in), None
        return _mm_nn(act, wdl, f"ffn{l}_down", res=h_in, norm_g=next_g)

    bs_col = gm_b_s[0].reshape(GM_GROUPS, CHUNK, 1)
    hn0 = _rms_fwd(h0, mix_norm_g[0:1], "mix0_norm", after=ag_token)
    z, gu = _gm_in_fused(hn0, w_in_g, gm_ln_g, gm_ln_b, gm_w_s[0], bs_col, "gm_in")
    mine_o, land_o = gather_wait(src_ids["gm_out"], z, "ag_wout_wait")
    w_out_g = own_blocks(land_o[0], mine_o[0:1], [0]).reshape(E, D)
    cwg = own_blocks(land_o[1], mine_o[1:2], [0])
    conv_w_full = jnp.transpose(cwg[:, :n_cw_rows].reshape(N_DEV, nl, 3, FF // N_DEV), (1, 2, 0, 3)).reshape(nl, 3, FF)
    h1, hn_f0 = _mm_nn(gu, w_out_g, "gm_out", res=h0, norm_g=ffn_norm_g[0:1])
    mine0, land0 = gather_wait(src_ids["ffn0"], h1, "ag_ffn0_wait")
    land_ffn(0, mine0, *land0)
    h2, hn2 = ffn_fwd(0, h1, hn_f0, mix_norm_g[1:2])

    mine_x, land_x = gather_wait(src_ids["fox"], h2, "ag_fox_wait")
    w_qkvf = jnp.transpose(own_blocks(land_x[0], mine_x[0:1], [0]), (1, 0, 2)).reshape(D, NQKVF)
    w_o_g = own_blocks(land_x[1], mine_x[1:2], [0]).reshape(D, D)
    w_q, w_k, w_v = w_qkvf[:, :D], w_qkvf[:, D:2 * D], w_qkvf[:, 2 * D:3 * D]
    w_f = jnp.pad(w_qkvf[:, 3 * D:], ((0, 0), (0, LANES - N_HEADS)))
    bf_row = jnp.pad(fox_b_f, ((0, 0), (0, LANES - N_HEADS)))
    sel_q, sel_k, const_q, const_k = _gate_tables()
    scale = HEAD_DIM ** -0.5
    f_logit = _mm_nn(hn2, w_f, "fox_f")
    cp, sneg = _gate_scan(f_logit, bf_row, "fox_scan")
    qp, kp = _qk_proj(hn2, (_pad_heads(w_q), _pad_heads(w_k)), cp, (sel_q, sel_k), (const_q, const_k),
                      (scale * LOG2E, 1.0), "fox_qk")
    vt = _mm_out_t(hn2, jnp.transpose(w_v), "fox_v")
    o, o32, lse = _attn_fwd_t(qp, kp, vt, "fox_attn")
    h3, hn_f1 = _mm_nn(o, w_o_g, "fox_o", res=h2, norm_g=ffn_norm_g[1:2])
    mine1, land1w = gather_wait([a for l in range(1, nl) for a in src_ids[f"ffn{l}"]], h3, "ag_ffn1_wait")
    for l in range(1, nl):
        land_ffn(l, mine1[3 * (l - 1):3 * l], land1w[2 * (l - 1)], land1w[2 * (l - 1) + 1])
    h4, _ = ffn_fwd(1, h3, hn_f1, None)

    dh, dh16, d_final, loss_row = _loss_head(h4, tgt, final_norm_g.reshape(1, D), "loss_head")
    loss = lax.psum(loss_row[0, 0], ("x", "y", "c"))

    rs_dst = lambda a, k: k
    rs_src = lambda a, p: _slot(p)
    me_idx = me.astype(jnp.int32).reshape(1)

    def rs_start(grads, name, cid):
        lands = [lax.empty((N_DEV - 1,) + g.shape[1:], BF16) for g in grads]
        return _direct_start(grads, lands, list(range(len(grads))), rs_dst, rs_src, loss_row, name, collective_id=cid)

    def rs_wait(st, after, name):
        n = len(st[2])
        return _direct_wait(st[0], st[1], st[2], st[3], list(range(n)), list(range(n)), rs_dst, rs_src, after, name)

    def ffn_bwd(l, dh, dh16, after=None):
        wgul, wdl = ffn_weights(l)
        h_in, hn, au, act = saved[f"ffn{l}"]
        da, dup, d_cw, d_cb = _ffn_mid_bwd(au, dh16, wdl, conv_w_full[l], ffn_conv_b[l:l + 1], f"ffn{l}_dmid", after=after)
        d_wd = _mm_tn(act, dh16, f"ffn{l}_dwd", out_dtype=BF16)
        dh_in, dh_in16, d_norm = _mm_nt([da, dup], wgul, f"ffn{l}_dhn", norm_bwd=(h_in, ffn_norm_g[l:l + 1], dh))
        d_wg = _mm_tn(hn, da, f"ffn{l}_dwg", blocked_w=FF // N_DEV, out_dtype=BF16)
        d_wu = _mm_tn(hn, dup, f"ffn{l}_dwu", blocked_w=FF // N_DEV, out_dtype=BF16)
        big_g = [d_wg, d_wu, d_wd.reshape(N_DEV, FF // N_DEV, D)]
        return dh_in, dh_in16, big_g, dict(cw=d_cw, cb=d_cb, norm=d_norm)

    dh, dh16, big_ffn1, g_ffn1 = ffn_bwd(1, dh, dh16)

    do = _mm_nt([dh16], w_o_g, "fox_do", out_dtype=BF16)
    d_wo = _mm_tn(o, dh16, "fox_dwo", out_dtype=BF16)
    dq, dk, dv, dqe, dke = _attn_bwd(qp, kp, vt, o32, do, lse, scale, "fox_dattn")
    gate_lane = lambda e: jnp.pad(jnp.transpose(e[:, :2, :].reshape(N_HEADS, T)), ((0, 0), (0, LANES - N_HEADS)))
    df, d_bf = _gate_scan_bwd(gate_lane(dqe), gate_lane(dke), sneg, "fox_dscan")
    dhn = _mm_nt([df], w_f, "fox_dhn_f")
    dh_mix1 = _mm_nt([dq, dk, dv], w_qkvf[:, :3 * D], "fox_dhn_qkv", add=dhn, norm_bwd=(h2, mix_norm_g[1:2], dh))
    d_wq = _mm_tn(hn2, dq, "fox_dwq", out_dtype=BF16)
    d_wk = _mm_tn(hn2, dk, "fox_dwk", out_dtype=BF16)
    d_wv = _mm_tn(hn2, dv, "fox_dwv", out_dtype=BF16)
    d_wf = _mm_tn(hn2, df, "fox_dwf", out_dtype=BF16)
    d_wqkvf = jnp.concatenate([d_wq, d_wk, d_wv, d_wf[:, :N_HEADS]], axis=1)
    dh, dh16, d_mix1 = dh_mix1
    st1 = rs_start([jnp.transpose(d_wqkvf.reshape(D, N_DEV, NQKVF // N_DEV), (1, 0, 2)),
                    d_wo.reshape(N_DEV, D // N_DEV, D)] + big_ffn1, "rs1_start", 2)

    dh, dh16, big_ffn0, g_ffn0 = ffn_bwd(0, dh, dh16, after=st1[4])
    d_wout = _mm_tn(gu, dh16, "gm_dwout", out_dtype=BF16)
    st2 = rs_start(big_ffn0 + [d_wout.reshape(N_DEV, E // N_DEV, D)], "rs2_start", 3)
    dz, d_lng, d_lnb, d_ws, d_bs = _sgu_bwd(z, dh16, w_out_g, gm_ln_g, gm_ln_b, gm_w_s[0], bs_col, "gm_dsgu", after=st2[4])
    d_win = _mm_tn(hn0, dz, "gm_dwin", blocked_w=2 * E // N_DEV, out_dtype=BF16)
    st3 = rs_start([d_win], "rs3_start", 4)
    dx, _, d_mix0 = _mm_nt([dz], w_in_g, "gm_dhn", after=st3[4], norm_bwd=(h0, mix_norm_g[0:1], dh))

    small = [("mix_norm_g", mix_norm_g, m_mix_norm_g, v_mix_norm_g, jnp.concatenate([d_mix0, d_mix1], axis=0)),
             ("ffn_norm_g", ffn_norm_g, m_ffn_norm_g, v_ffn_norm_g, jnp.concatenate([g_ffn0["norm"], g_ffn1["norm"]], axis=0)),
             ("gm_ln_g", gm_ln_g, m_gm_ln_g, v_gm_ln_g, d_lng),
             ("gm_ln_b", gm_ln_b, m_gm_ln_b, v_gm_ln_b, d_lnb),
             ("gm_w_s", gm_w_s, m_gm_w_s, v_gm_w_s, d_ws),
             ("gm_b_s", gm_b_s, m_gm_b_s, v_gm_b_s, d_bs),
             ("fox_b_f", fox_b_f, m_fox_b_f, v_fox_b_f, d_bf[:, :N_HEADS]),
             ("ffn_conv_b", ffn_conv_b, m_ffn_conv_b, v_ffn_conv_b, jnp.concatenate([g_ffn0["cb"], g_ffn1["cb"]], axis=0)),
             ("final_norm_g", final_norm_g, m_final_norm_g, v_final_norm_g, d_final)]
    d_cw_full = jnp.stack([g_ffn0["cw"], g_ffn1["cw"]], axis=0)

    def small_rows(a):
        flat = a.astype(F32).reshape(-1)
        n = -(-flat.size // (8 * LANES)) * (8 * LANES)
        return jnp.pad(flat, (0, n - flat.size)).reshape(-1, LANES)

    s_rows = [small_rows(p[1]).shape[0] for p in small]
    s_off = np.concatenate([[0], np.cumsum(s_rows)]).tolist()
    cw_g_rows = small_rows(d_cw_full)
    g_small = jnp.concatenate([small_rows(p[4]) for p in small] + [cw_g_rows], axis=0)

    own1, land1 = rs_wait(st1, dx, "rs1_wait")
    own2, land2 = rs_wait(st2, land1[0], "rs2_wait")
    big_out = {}

    def big_adamw(name, w, m, v, own, landed):
        shard2d = lambda a, c=(own[0] if isinstance(own, list) else own).shape[2]: a.reshape(-1, c)
        res = _sum_adamw([(own, [-1]), (landed, list(range(N_DEV - 1)))], shard2d(w), shard2d(m), shard2d(v),
                         f"adamw_{name}", sel=me_idx)
        big_out[name] = [t.reshape(w.shape) for t in res]

    big_adamw("fox_w_qkvf", fox_w_qkvf, m_fox_w_qkvf, v_fox_w_qkvf, own1[0], land1[0])
    big_adamw("fox_w_o", fox_w_o, m_fox_w_o, v_fox_w_o, own1[1], land1[1])
    big_adamw("ffn_w_gate", ffn_w_gate, m_ffn_w_gate, v_ffn_w_gate, [own2[0], own1[2]], [land2[0], land1[2]])
    big_adamw("ffn_w_up", ffn_w_up, m_ffn_w_up, v_ffn_w_up, [own2[1], own1[3]], [land2[1], land1[3]])
    big_adamw("ffn_w_down", ffn_w_down, m_ffn_w_down, v_ffn_w_down, [own2[2], own1[4]], [land2[2], land1[4]])
    big_adamw("gm_w_out", gm_w_out, m_gm_w_out, v_gm_w_out, own2[3], land2[3])

    (gs_all,) = _all_gather([g_small], "ag_small_grads")
    zeros_cw = jnp.zeros_like(cw_g_rows)
    cat = lambda k: jnp.concatenate([small_rows(p[k]) for p in small] + [zeros_cw], axis=0)
    small_out = _sum_adamw([(gs_all, list(range(N_DEV)))], cat(1), cat(2), cat(3), "adamw_small")
    gs = small_out[0]

    g_cw_full = gs[s_off[-1]:].reshape(-1)[:d_cw_full.size].reshape(d_cw_full.shape)
    g_cw = lax.dynamic_slice_in_dim(g_cw_full, me * (FF // N_DEV), FF // N_DEV, axis=2)
    cw2 = lambda a: _pad_rows(_rows(a.astype(F32), LANES), 16)
    cw_out = _sum_adamw([(cw2(g_cw)[None], [0])], cw2(ffn_conv_w), cw2(m_ffn_conv_w), cw2(v_ffn_conv_w), "adamw_conv_w")

    own3, land3 = rs_wait(st3, cw_out[0], "rs3_wait")
    big_adamw("gm_w_in", gm_w_in, m_gm_w_in, v_gm_w_in, own3[0], land3[0])

    names = ["mix_norm_g", "ffn_norm_g", "gm_w_in", "gm_ln_g", "gm_ln_b", "gm_w_s", "gm_b_s", "gm_w_out", "fox_w_qkvf",
             "fox_b_f", "fox_w_o", "ffn_w_gate", "ffn_w_up", "ffn_conv_w", "ffn_conv_b", "ffn_w_down", "final_norm_g"]
    small_idx = {p[0]: k for k, p in enumerate(small)}

    def pick(kind, name):
        if name in big_out:
            return big_out[name][kind]
        if name == "ffn_conv_w":
            return cw_out[kind][:n_cw_rows].reshape(ffn_conv_w.shape)
        k = small_idx[name]
        shp = small[k][1].shape
        return small_out[kind][s_off[k]:s_off[k + 1]].reshape(-1)[:int(np.prod(shp))].reshape(shp)

    outs = [loss, dx.reshape(x.shape)]
    for kind in range(4):
        outs += [pick(kind, n) for n in names]
    return tuple(outs)
```

```python
import math

import numpy as np
import jax
import jax.numpy as jnp
from jax import lax
from jax.experimental import pallas as pl
from jax.experimental.pallas import tpu as pltpu

F32 = jnp.float32
BF16 = jnp.bfloat16
MESH = pl.DeviceIdType.MESH

N_HEADS = 16
HEAD_DIM = 64
CHUNK = 128
GM_GROUPS = 8
RMS_EPS = 1e-6
LN_EPS = 1e-5
ADAM_LR = 0.001
ADAM_B1 = 0.9
ADAM_B2 = 0.999
ADAM_EPS = 1e-08
ADAM_WD = 0.01
ADAM_STEP = 10
N_DEV = 8

LANES = 128
VMEM_BYTES_V7X = 64 * 1024 * 1024
VMEM_LIMIT = 56 * 1024 * 1024

TM = 512
TM_MM = 1024
TT = 1024
TQ = 512
TF = 1024
KV_UNROLL_BWD = 2
KV_UNROLL = 2
TN_ROWS = 512
MM_BLOCK_BYTES = 8 * 1024 * 1024
NEG = -1e30
LOG2E = math.log2(math.e)
LN2 = math.log(2.0)


def _cp(sem=None, vmem=VMEM_LIMIT):
    return pltpu.CompilerParams(dimension_semantics=sem, vmem_limit_bytes=vmem)


def _gelu(x):
    c = math.sqrt(2.0 / math.pi)
    return x * (0.5 * (1.0 + jnp.tanh(c * (x + 0.044715 * (x * x * x)))))


def _gelu_grad(x):
    c = math.sqrt(2.0 / math.pi)
    t = jnp.tanh(c * (x + 0.044715 * (x * x * x)))
    return 0.5 * (1.0 + t) + x * (0.5 * (1.0 - t * t)) * (c * (1.0 + 3.0 * 0.044715 * (x * x)))


def _sigmoid(x):
    return 1.0 / (1.0 + jnp.exp(-x))


def _dot_nt(a, b):
    return lax.dot_general(a, b, (((1,), (1,)), ((), ())), preferred_element_type=F32)


def _dot_tn(a, b):
    return lax.dot_general(a, b, (((0,), (0,)), ((), ())), preferred_element_type=F32)


def _rms_fwd(h, g, name, after=None):
    T, D = h.shape
    tm = min(TM, T)

    def body(h_ref, g_ref, *rest):
        o_ref = rest[-1]
        x = h_ref[...]
        r = lax.rsqrt(jnp.mean(x * x, axis=-1, keepdims=True) + RMS_EPS)
        o_ref[...] = ((x * r) * g_ref[...]).astype(BF16)

    in_specs = [pl.BlockSpec((tm, D), lambda i: (i, 0)), pl.BlockSpec((1, D), lambda i: (0, 0))]
    args = [h, g]
    if after is not None:
        in_specs.append(pl.BlockSpec(memory_space=pl.ANY))
        args.append(after)
    return pl.pallas_call(
        body, name=name, grid=(T // tm,),
        in_specs=in_specs,
        out_specs=pl.BlockSpec((tm, D), lambda i: (i, 0)),
        out_shape=jax.ShapeDtypeStruct((T, D), BF16),
        compiler_params=_cp(("parallel",)),
    )(*args)


def _loss_head(h, tgt, g, name):
    T, D = h.shape
    tm = min(TM, T)

    def body(h_ref, t_ref, g_ref, o_ref, ob_ref, dg_ref, l_ref):
        x = h_ref[...]
        gg = g_ref[...]
        r = lax.rsqrt(jnp.mean(x * x, axis=-1, keepdims=True) + RMS_EPS)
        xr = x * r
        e = xr * gg - t_ref[...]
        lpart = 0.5 * jnp.sum(jnp.mean(e * e, axis=-1, keepdims=True), axis=0, keepdims=True)
        dy = e * (1.0 / D)
        dyg = dy * gg
        dot = jnp.mean(dyg * x, axis=-1, keepdims=True)
        dh = r * dyg - x * ((r * r * r) * dot)
        o_ref[...] = dh
        ob_ref[...] = dh.astype(BF16)
        part = jnp.sum(dy * xr, axis=0, keepdims=True)
        lrow = jnp.broadcast_to(lpart, (1, LANES))

        @pl.when(pl.program_id(0) == 0)
        def _():
            dg_ref[...] = part
            l_ref[...] = lrow

        @pl.when(pl.program_id(0) != 0)
        def _():
            dg_ref[...] += part
            l_ref[...] += lrow

    blk = pl.BlockSpec((tm, D), lambda i: (i, 0))
    row = pl.BlockSpec((1, D), lambda i: (0, 0))
    return pl.pallas_call(
        body, name=name, grid=(T // tm,),
        in_specs=[blk, blk, row],
        out_specs=[blk, blk, row, pl.BlockSpec((1, LANES), lambda i: (0, 0))],
        out_shape=[jax.ShapeDtypeStruct((T, D), F32), jax.ShapeDtypeStruct((T, D), BF16),
                   jax.ShapeDtypeStruct((1, D), F32), jax.ShapeDtypeStruct((1, LANES), F32)],
        compiler_params=_cp(("arbitrary",)),
    )(h, tgt, g)


def _mm_nn(a, b, name, out_dtype=F32, res=None, norm_g=None):
    M, K = a.shape
    b3 = b if b.ndim == 3 else b[None]
    nb, _, w = b3.shape
    N = nb * w
    tm = min(TM_MM, M, max(256, MM_BLOCK_BYTES // (4 * N)))
    o_spec = pl.BlockSpec((tm, N), lambda i: (i, 0))
    in_specs = [pl.BlockSpec((tm, K), lambda i: (i, 0)), pl.BlockSpec((nb, K, w), lambda i: (0, 0, 0))]
    args = [a, b3]
    if res is not None:
        in_specs.append(o_spec)
        args.append(res)
    if norm_g is not None:
        in_specs.append(pl.BlockSpec((1, N), lambda i: (0, 0)))
        args.append(norm_g)
    n_out = 2 if norm_g is not None else 1

    def body(*refs):
        a_ref, b_ref = refs[0], refs[1]
        o_ref = refs[-n_out]
        av = a_ref[...]
        for j in range(nb):
            cols = slice(j * w, (j + 1) * w)
            acc = jnp.dot(av, b_ref[j], preferred_element_type=F32)
            if res is not None:
                acc = refs[2][:, cols] + acc
            o_ref[:, cols] = acc.astype(out_dtype)
        if norm_g is not None:
            x = o_ref[...]
            r = lax.rsqrt(jnp.mean(x * x, axis=-1, keepdims=True) + RMS_EPS)
            refs[-1][...] = ((x * r) * refs[3][...]).astype(BF16)

    out_shape = jax.ShapeDtypeStruct((M, N), out_dtype)
    if norm_g is None:
        out_specs, out_shapes = o_spec, out_shape
    else:
        out_specs, out_shapes = [o_spec, o_spec], [out_shape, jax.ShapeDtypeStruct((M, N), BF16)]
    return pl.pallas_call(
        body, name=name, grid=(M // tm,),
        in_specs=in_specs, out_specs=out_specs, out_shape=out_shapes,
        compiler_params=_cp(("parallel",)),
    )(*args)


def _mm_nt(a_list, b, name, out_dtype=F32, add=None, after=None, norm_bwd=None):
    M, kw = a_list[0].shape
    tm = min(TM, M)
    na = len(a_list)
    blocked = b.ndim == 3
    N = b.shape[1] if blocked else b.shape[0]
    b_spec = pl.BlockSpec(b.shape, lambda i: (0,) * b.ndim)
    o_spec = pl.BlockSpec((tm, N), lambda i: (i, 0))
    row_spec = pl.BlockSpec((1, N), lambda i: (0, 0))
    in_specs = [pl.BlockSpec((tm, kw), lambda i: (i, 0)) for _ in a_list] + [b_spec]
    args = list(a_list) + [b]
    if add is not None:
        in_specs.append(o_spec)
        args.append(add)
    n_in = len(args)
    if norm_bwd is not None:
        in_specs += [o_spec, row_spec, o_spec]
        args += list(norm_bwd)
    if after is not None:
        in_specs.append(pl.BlockSpec(memory_space=pl.ANY))
        args.append(after)
    n_args = len(args)

    def body(*refs):
        a_refs = refs[:na]
        b_ref = refs[na]
        acc = refs[na + 1][...] if add is not None else None
        for s, a_ref in enumerate(a_refs):
            if blocked:
                w = b_ref.shape[2]
                per = kw // w
                parts = [_dot_nt(a_ref[:, jj * w:(jj + 1) * w], b_ref[s * per + jj]) for jj in range(per)]
            else:
                parts = [_dot_nt(a_ref[...], b_ref[:, s * kw:(s + 1) * kw])]
            for part in parts:
                acc = part if acc is None else acc + part
        if norm_bwd is None:
            refs[n_args][...] = acc.astype(out_dtype)
            return
        h_ref, g_ref, r_ref = refs[n_in:n_in + 3]
        o_ref, ob_ref, dg_ref = refs[n_args:n_args + 3]
        x = h_ref[...]
        r = lax.rsqrt(jnp.mean(x * x, axis=-1, keepdims=True) + RMS_EPS)
        dyg = acc * g_ref[...]
        dot = jnp.mean(dyg * x, axis=-1, keepdims=True)
        dh = r_ref[...] + (r * dyg - x * ((r * r * r) * dot))
        o_ref[...] = dh
        ob_ref[...] = dh.astype(BF16)
        part_g = jnp.sum(acc * (x * r), axis=0, keepdims=True)

        @pl.when(pl.program_id(0) == 0)
        def _():
            dg_ref[...] = part_g

        @pl.when(pl.program_id(0) != 0)
        def _():
            dg_ref[...] += part_g

    if norm_bwd is None:
        out_specs, out_shapes, sem = o_spec, jax.ShapeDtypeStruct((M, N), out_dtype), ("parallel",)
    else:
        out_specs = [o_spec, o_spec, row_spec]
        out_shapes = [jax.ShapeDtypeStruct((M, N), F32), jax.ShapeDtypeStruct((M, N), BF16),
                      jax.ShapeDtypeStruct((1, N), F32)]
        sem = ("arbitrary",)
    return pl.pallas_call(
        body, name=name, grid=(M // tm,),
        in_specs=in_specs, out_specs=out_specs, out_shape=out_shapes,
        compiler_params=_cp(sem),
    )(*args)


def _mm_out_t(a, bt, name):
    M, K = a.shape
    N = bt.shape[0]
    tm = min(TM_MM, M)

    def body(a_ref, b_ref, o_ref):
        o_ref[...] = _dot_nt(b_ref[...], a_ref[...]).astype(BF16)

    return pl.pallas_call(
        body, name=name, grid=(M // tm,),
        in_specs=[pl.BlockSpec((tm, K), lambda i: (i, 0)), pl.BlockSpec((N, K), lambda i: (0, 0))],
        out_specs=pl.BlockSpec((N, tm), lambda i: (0, i)),
        out_shape=jax.ShapeDtypeStruct((N, M), BF16),
        compiler_params=_cp(("parallel",)),
    )(a, bt)


def _mm_tn(x, y, name, blocked_w=None, out_dtype=F32):
    T, Kx = x.shape
    N = y.shape[1]
    tt = min(TT, T)
    nt = T // tt
    tkx = min(Kx, max(LANES, MM_BLOCK_BYTES // (4 * N)))
    if blocked_w is not None:
        blk_shape, full_shape = (N // blocked_w, tkx, blocked_w), (N // blocked_w, Kx, blocked_w)
        o_spec = pl.BlockSpec(blk_shape, lambda i, t: (0, i, 0))
    else:
        blk_shape, full_shape = (tkx, N), (Kx, N)
        o_spec = pl.BlockSpec(blk_shape, lambda i, t: (i, 0))

    rk = min(tkx, TN_ROWS)

    def body(x_ref, y_ref, o_ref, acc_ref):
        t = pl.program_id(1)

        @pl.when(t == 0)
        def _():
            acc_ref[...] = jnp.zeros_like(acc_ref)

        for r in range(tkx // rk):
            rows = slice(r * rk, (r + 1) * rk)
            part = _dot_tn(x_ref[:, rows], y_ref[...])
            if blocked_w is None:
                acc_ref[rows, :] += part
            else:
                for j in range(N // blocked_w):
                    acc_ref[j, rows, :] += part[:, j * blocked_w:(j + 1) * blocked_w]

        @pl.when(t == nt - 1)
        def _():
            o_ref[...] = acc_ref[...].astype(out_dtype)

    return pl.pallas_call(
        body, name=name, grid=(Kx // tkx, nt),
        in_specs=[pl.BlockSpec((tt, tkx), lambda i, t: (t, i)),
                  pl.BlockSpec((tt, N), lambda i, t: (t, 0))],
        out_specs=o_spec, out_shape=jax.ShapeDtypeStruct(full_shape, out_dtype),
        scratch_shapes=[pltpu.VMEM(blk_shape, F32)],
        compiler_params=_cp(("parallel", "arbitrary")),
    )(x, y)


def _sgu_pieces(z, lng, lnb, wc, bs_ref):
    E = z.shape[1] // 2
    gd = E // GM_GROUPS
    zu, zv = z[:, :E], z[:, E:]
    u = _gelu(zu)
    v = _gelu(zv)
    mu = jnp.mean(v, axis=-1, keepdims=True)
    xc = v - mu
    rs = lax.rsqrt(jnp.mean(xc * xc, axis=-1, keepdims=True) + LN_EPS)
    xhat = xc * rs
    vln = xhat * lng + lnb
    s = []
    for g in range(GM_GROUPS):
        vg = vln[:, g * gd:(g + 1) * gd].astype(BF16)
        s.append(jnp.dot(wc[g], vg, preferred_element_type=F32) + bs_ref[g])
    return zu, zv, u, xhat, rs, vln, s


def _causal_ws(ws_ref):
    t = lax.broadcasted_iota(jnp.int32, (CHUNK, CHUNK), 0)
    s = lax.broadcasted_iota(jnp.int32, (CHUNK, CHUNK), 1)
    tri = t >= s
    return [jnp.where(tri, ws_ref[g], 0.0).astype(BF16) for g in range(GM_GROUPS)], tri


def _gm_in_fused(hn, w_in, lng, lnb, ws, bs, name):
    T, D = hn.shape
    nb, _, w = w_in.shape
    E2 = nb * w
    E = E2 // 2
    gd = E // GM_GROUPS
    tm = min(TM, T)

    def body(a_ref, w_ref, lng_ref, lnb_ref, ws_ref, bs_ref, z_ref, o_ref):
        av = a_ref[...]
        for j in range(nb):
            z_ref[:, j * w:(j + 1) * w] = jnp.dot(av, w_ref[j], preferred_element_type=F32)
        wc, _ = _causal_ws(ws_ref)
        for c in range(tm // CHUNK):
            rows = slice(c * CHUNK, (c + 1) * CHUNK)
            _, _, u, _, _, _, s = _sgu_pieces(z_ref[rows, :], lng_ref[...], lnb_ref[...], wc, bs_ref)
            for g in range(GM_GROUPS):
                cols = slice(g * gd, (g + 1) * gd)
                o_ref[rows, cols] = (u[:, cols] * s[g]).astype(BF16)

    full = lambda shape: pl.BlockSpec(shape, lambda i: (0,) * len(shape))
    return pl.pallas_call(
        body, name=name, grid=(T // tm,),
        in_specs=[pl.BlockSpec((tm, D), lambda i: (i, 0)), full((nb, D, w)), full((1, E)), full((1, E)),
                  full((GM_GROUPS, CHUNK, CHUNK)), full((GM_GROUPS, CHUNK, 1))],
        out_specs=[pl.BlockSpec((tm, E2), lambda i: (i, 0)), pl.BlockSpec((tm, E), lambda i: (i, 0))],
        out_shape=[jax.ShapeDtypeStruct((T, E2), F32), jax.ShapeDtypeStruct((T, E), BF16)],
        compiler_params=_cp(("parallel",)),
    )(hn, w_in, lng, lnb, ws, bs)


def _sgu_bwd(z, dh16, w_out, lng, lnb, ws, bs, name, after=None):
    T, E2 = z.shape
    D = dh16.shape[1]
    E = E2 // 2
    gd = E // GM_GROUPS
    tm = min(2 * CHUNK, T)
    nsteps = T // tm

    def body(z_ref, dh_ref, wo_ref, lng_ref, lnb_ref, ws_ref, bs_ref, *rest):
        dz_ref, dlng_ref, dlnb_ref, dws_ref, dbs_ref, dg_ref = rest[-6:]
        i = pl.program_id(0)

        @pl.when(i == 0)
        def _():
            dlng_ref[...] = jnp.zeros_like(dlng_ref)
            dlnb_ref[...] = jnp.zeros_like(dlnb_ref)
            dws_ref[...] = jnp.zeros_like(dws_ref)
            dbs_ref[...] = jnp.zeros_like(dbs_ref)

        dg_ref[...] = _dot_nt(dh_ref[...], wo_ref[...]).astype(BF16)
        wc, tri = _causal_ws(ws_ref)
        lng_v = lng_ref[...]
        for c in range(tm // CHUNK):
            rows = slice(c * CHUNK, (c + 1) * CHUNK)
            zu, zv, u, xhat, rs, vln, s = _sgu_pieces(z_ref[rows, :], lng_v, lnb_ref[...], wc, bs_ref)
            dgc = dg_ref[rows, :].astype(F32)
            du, dvln = [], []
            for g in range(GM_GROUPS):
                cols = slice(g * gd, (g + 1) * gd)
                dgg = dgc[:, cols]
                du.append(dgg * s[g])
                ds = dgg * u[:, cols]
                dsb = ds.astype(BF16)
                dws_ref[g] += _dot_nt(dsb, vln[:, cols].astype(BF16))
                dbs_ref[g] += jnp.sum(ds, axis=-1, keepdims=True)
                dvln.append(_dot_tn(wc[g], dsb))
            du = jnp.concatenate(du, axis=1)
            dvln = jnp.concatenate(dvln, axis=1)
            dlng_ref[...] += jnp.sum(dvln * xhat, axis=0, keepdims=True)
            dlnb_ref[...] += jnp.sum(dvln, axis=0, keepdims=True)
            dxh = dvln * lng_v
            m1 = jnp.mean(dxh, axis=-1, keepdims=True)
            m2 = jnp.mean(dxh * xhat, axis=-1, keepdims=True)
            dv = rs * (dxh - m1 - xhat * m2)
            dz_ref[rows, :E] = (du * _gelu_grad(zu)).astype(BF16)
            dz_ref[rows, E:] = (dv * _gelu_grad(zv)).astype(BF16)

        @pl.when(i == nsteps - 1)
        def _():
            for g in range(GM_GROUPS):
                dws_ref[g] = jnp.where(tri, dws_ref[g], 0.0)

    full = lambda shape: pl.BlockSpec(shape, lambda i: (0,) * len(shape))
    in_specs = [pl.BlockSpec((tm, E2), lambda i: (i, 0)), pl.BlockSpec((tm, D), lambda i: (i, 0)), full((E, D)),
                full((1, E)), full((1, E)), full((GM_GROUPS, CHUNK, CHUNK)), full((GM_GROUPS, CHUNK, 1))]
    args = [z, dh16, w_out, lng, lnb, ws, bs]
    if after is not None:
        in_specs.append(pl.BlockSpec(memory_space=pl.ANY))
        args.append(after)
    return pl.pallas_call(
        body, name=name, grid=(nsteps,),
        in_specs=in_specs,
        out_specs=[pl.BlockSpec((tm, E2), lambda i: (i, 0)), full((1, E)), full((1, E)),
                   full((GM_GROUPS, CHUNK, CHUNK)), full((GM_GROUPS, CHUNK, 1))],
        out_shape=[jax.ShapeDtypeStruct((T, E2), BF16), jax.ShapeDtypeStruct((1, E), F32),
                   jax.ShapeDtypeStruct((1, E), F32), jax.ShapeDtypeStruct((GM_GROUPS, CHUNK, CHUNK), F32),
                   jax.ShapeDtypeStruct((GM_GROUPS, CHUNK, 1), F32)],
        scratch_shapes=[pltpu.VMEM((tm, E), BF16)],
        compiler_params=_cp(("arbitrary",)),
    )(*args)


HALO = 16


def _conv_taps(a_ext, w_ref, b_ref):
    n = a_ext.shape[0]
    am1 = pltpu.roll(a_ext, 1, 0)
    am2 = pltpu.roll(a_ext, 2, 0)
    del n
    return ((b_ref[...] + am2 * w_ref[0:1, :]) + am1 * w_ref[1:2, :]) + a_ext * w_ref[2:3, :], am1, am2


def _ffn_up_fused(hn, wgu, cw, cb, name):
    T, D = hn.shape
    nb2, _, w = wgu.shape
    nb = nb2 // 2
    F = nb * w
    tm = min(TM, T)

    def body(a_ref, w_ref, cw_ref, cb_ref, au_ref, act_ref, halo_ref):
        @pl.when(pl.program_id(0) == 0)
        def _():
            halo_ref[...] = jnp.zeros_like(halo_ref)

        av = a_ref[...]
        for j in range(nb):
            cols = slice(j * w, (j + 1) * w)
            g = jnp.dot(av, w_ref[j], preferred_element_type=F32)
            u = jnp.dot(av, w_ref[nb + j], preferred_element_type=F32)
            au_ref[:, cols] = g
            au_ref[:, F + j * w:F + (j + 1) * w] = u
            ext = jnp.concatenate([halo_ref[:, cols], g], axis=0)
            am1 = pltpu.roll(ext, 1, 0)
            am2 = pltpu.roll(ext, 2, 0)
            conv = ((cb_ref[:, cols] + am2 * cw_ref[0:1, cols]) + am1 * cw_ref[1:2, cols]) + ext * cw_ref[2:3, cols]
            conv = conv[HALO:, :]
            act_ref[:, cols] = ((conv * _sigmoid(conv)) * u).astype(BF16)
            halo_ref[:, cols] = g[tm - HALO:, :]

    return pl.pallas_call(
        body, name=name, grid=(T // tm,),
        in_specs=[pl.BlockSpec((tm, D), lambda i: (i, 0)), pl.BlockSpec((nb2, D, w), lambda i: (0, 0, 0)),
                  pl.BlockSpec((3, F), lambda i: (0, 0)), pl.BlockSpec((1, F), lambda i: (0, 0))],
        out_specs=[pl.BlockSpec((tm, 2 * F), lambda i: (i, 0)), pl.BlockSpec((tm, F), lambda i: (i, 0))],
        out_shape=[jax.ShapeDtypeStruct((T, 2 * F), F32), jax.ShapeDtypeStruct((T, F), BF16)],
        scratch_shapes=[pltpu.VMEM((HALO, F), F32)],
        compiler_params=_cp(("arbitrary",)),
    )(hn, wgu, cw, cb)


def _ffn_mid_bwd(au, dh16, wd, cw, cb, name, after=None):
    T, F = au.shape[0], au.shape[1] // 2
    D = dh16.shape[1]
    tm, tf = min(TM, T), min(TF, F)
    hb = tm // HALO
    nt = T // tm
    nf = F // tf
    last_h = T // HALO - 1

    def body(a_ref, ap_ref, an_ref, u_ref, un_ref, dh_ref, dhn_ref, wd_ref, w_ref, b_ref, *rest):
        da_ref, du_ref, dcw_ref, dcb_ref = rest[-4:]
        i = pl.program_id(1)
        prev = jnp.where(i == 0, 0.0, ap_ref[...])
        a_main = a_ref[...]
        a_ext = jnp.concatenate([prev, a_main, an_ref[...]], axis=0)
        conv, am1, am2 = _conv_taps(a_ext, w_ref, b_ref)
        conv = conv[HALO:, :]
        sig = _sigmoid(conv)
        u_ext = jnp.concatenate([u_ref[...], un_ref[...]], axis=0)
        wd_f = wd_ref[...]
        d_ext = jnp.concatenate([_dot_nt(dh_ref[...], wd_f), _dot_nt(dhn_ref[...], wd_f)], axis=0)
        d_ext = d_ext.astype(BF16).astype(F32)
        n = tm + HALO
        row = lax.broadcasted_iota(jnp.int32, (n, 1), 0)
        live = jnp.logical_or(row < tm, i < nt - 1)
        dconv = jnp.where(live, d_ext * u_ext * (sig * (1.0 + conv * (1.0 - sig))), 0.0)
        du_ref[...] = (d_ext[:tm, :] * (conv[:tm, :] * sig[:tm, :])).astype(BF16)
        dp1 = pltpu.roll(dconv, n - 1, 0)[:tm, :]
        dp2 = pltpu.roll(dconv, n - 2, 0)[:tm, :]
        dc = dconv[:tm, :]
        da_ref[...] = ((dc * w_ref[2:3, :] + dp1 * w_ref[1:2, :]) + dp2 * w_ref[0:1, :]).astype(BF16)
        g2 = jnp.sum(dc * a_main, axis=0, keepdims=True)
        g1 = jnp.sum(dc * am1[HALO:HALO + tm, :], axis=0, keepdims=True)
        g0 = jnp.sum(dc * am2[HALO:HALO + tm, :], axis=0, keepdims=True)
        gb = jnp.sum(dc, axis=0, keepdims=True)

        @pl.when(i == 0)
        def _():
            dcw_ref[...] = jnp.zeros_like(dcw_ref)
            dcb_ref[...] = jnp.zeros_like(dcb_ref)

        dcw_ref[0:1, :] += g0
        dcw_ref[1:2, :] += g1
        dcw_ref[2:3, :] += g2
        dcb_ref[...] += gb

    main = pl.BlockSpec((tm, tf), lambda f, i: (i, f))
    prev = pl.BlockSpec((HALO, tf), lambda f, i: (jnp.maximum(i * hb - 1, 0), f))
    nxt = pl.BlockSpec((HALO, tf), lambda f, i: (jnp.minimum((i + 1) * hb, last_h), f))
    main_u = pl.BlockSpec((tm, tf), lambda f, i: (i, nf + f))
    nxt_u = pl.BlockSpec((HALO, tf), lambda f, i: (jnp.minimum((i + 1) * hb, last_h), nf + f))
    in_specs = [main, prev, nxt, main_u, nxt_u,
                pl.BlockSpec((tm, D), lambda f, i: (i, 0)),
                pl.BlockSpec((HALO, D), lambda f, i: (jnp.minimum((i + 1) * hb, last_h), 0)),
                pl.BlockSpec((tf, D), lambda f, i: (f, 0)),
                pl.BlockSpec((3, tf), lambda f, i: (0, f)), pl.BlockSpec((1, tf), lambda f, i: (0, f))]
    args = [au, au, au, au, au, dh16, dh16, wd, cw, cb]
    if after is not None:
        in_specs.append(pl.BlockSpec(memory_space=pl.ANY))
        args.append(after)
    return pl.pallas_call(
        body, name=name, grid=(nf, nt),
        in_specs=in_specs,
        out_specs=[main, main, pl.BlockSpec((3, tf), lambda f, i: (0, f)), pl.BlockSpec((1, tf), lambda f, i: (0, f))],
        out_shape=[jax.ShapeDtypeStruct((T, F), BF16), jax.ShapeDtypeStruct((T, F), BF16),
                   jax.ShapeDtypeStruct((3, F), F32), jax.ShapeDtypeStruct((1, F), F32)],
        compiler_params=_cp(("parallel", "arbitrary")),
    )(*args)


def _split3(x):
    hi = x.astype(BF16)
    r1 = x - hi.astype(F32)
    mid = r1.astype(BF16)
    lo = (r1 - mid.astype(F32)).astype(BF16)
    return hi, mid, lo


def _tri_ones(n, upper):
    r = lax.broadcasted_iota(jnp.int32, (n, n), 0)
    c = lax.broadcasted_iota(jnp.int32, (n, n), 1)
    return jnp.where((r <= c) if upper else (r >= c), 1.0, 0.0).astype(BF16)


def _gate_scan(f, bf, name):
    T = f.shape[0]
    tm = min(256, T)

    def body(f_ref, b_ref, cp_ref, sn_ref, carry_ref):
        i = pl.program_id(0)

        @pl.when(i == 0)
        def _():
            carry_ref[...] = jnp.zeros_like(carry_ref)

        x = f_ref[...] + b_ref[...]
        e = jnp.exp(-jnp.abs(x))
        logf = jnp.minimum(x, 0.0) - jnp.log(1.0 + e)
        sn_ref[...] = jnp.where(x >= 0.0, e / (1.0 + e), 1.0 / (1.0 + e))
        tri = _tri_ones(tm, upper=False)
        c = carry_ref[...]
        for piece in _split3(logf):
            c = c + jnp.dot(tri, piece, preferred_element_type=F32)
        carry_ref[...] += jnp.sum(logf, axis=0, keepdims=True)
        hi, mid, lo = _split3(c * LOG2E)
        cp_ref[:, 0:LANES] = hi
        cp_ref[:, LANES:2 * LANES] = mid
        cp_ref[:, 2 * LANES:3 * LANES] = lo

    return pl.pallas_call(
        body, name=name, grid=(T // tm,),
        in_specs=[pl.BlockSpec((tm, LANES), lambda i: (i, 0)), pl.BlockSpec((1, LANES), lambda i: (0, 0))],
        out_specs=[pl.BlockSpec((tm, 3 * LANES), lambda i: (i, 0)), pl.BlockSpec((tm, LANES), lambda i: (i, 0))],
        out_shape=[jax.ShapeDtypeStruct((T, 3 * LANES), BF16), jax.ShapeDtypeStruct((T, LANES), F32)],
        scratch_shapes=[pltpu.VMEM((1, LANES), F32)],
        compiler_params=_cp(("arbitrary",)),
    )(f, bf)


def _gate_scan_bwd(dcq, dck, sneg, name):
    T = dcq.shape[0]
    tm = min(256, T)
    n = T // tm

    def body(dcq_ref, dck_ref, sn_ref, df_ref, db_ref, carry_ref):
        i = pl.program_id(0)

        @pl.when(i == 0)
        def _():
            carry_ref[...] = jnp.zeros_like(carry_ref)
            db_ref[...] = jnp.zeros_like(db_ref)

        tri = _tri_ones(tm, upper=True)
        dcb = dcq_ref[...] - dck_ref[...]
        acc = carry_ref[...]
        for piece in _split3(dcb):
            acc = acc + jnp.dot(tri, piece, preferred_element_type=F32)
        carry_ref[...] += jnp.sum(dcb, axis=0, keepdims=True)
        df = acc * sn_ref[...]
        df_ref[...] = df.astype(BF16)
        db_ref[...] += jnp.sum(df, axis=0, keepdims=True)

    rev = pl.BlockSpec((tm, LANES), lambda i: (n - 1 - i, 0))
    return pl.pallas_call(
        body, name=name, grid=(n,),
        in_specs=[rev, rev, rev],
        out_specs=[rev, pl.BlockSpec((1, LANES), lambda i: (0, 0))],
        out_shape=[jax.ShapeDtypeStruct((T, LANES), BF16), jax.ShapeDtypeStruct((1, LANES), F32)],
        scratch_shapes=[pltpu.VMEM((1, LANES), F32)],
        compiler_params=_cp(("arbitrary",)),
    )(dcq, dck, sneg)


def _qk_proj(hn, w_pads, cp, sels, consts, scales, name):
    T, D = hn.shape
    H = w_pads[0].shape[1] // LANES
    tm = min(TM, T)

    def body(a_ref, cp_ref, wq_ref, wk_ref, sq_ref, sk_ref, cq_ref, ck_ref, qo_ref, ko_ref):
        a = a_ref[...]
        cpv = cp_ref[...]
        for w_ref, sel_ref, c_ref, o_ref, scale in ((wq_ref, sq_ref, cq_ref, qo_ref, scales[0]),
                                                    (wk_ref, sk_ref, ck_ref, ko_ref, scales[1])):
            for p in range(H // 2):
                acc = jnp.dot(a, w_ref[:, p * 2 * LANES:(p + 1) * 2 * LANES], preferred_element_type=F32)
                if scale != 1.0:
                    acc = acc * scale
                acc = acc + jnp.dot(cpv, sel_ref[p], preferred_element_type=F32) + c_ref[p]
                o_ref[2 * p] = acc[:, :LANES].astype(BF16)
                o_ref[2 * p + 1] = acc[:, LANES:].astype(BF16)

    whole = lambda t: pl.BlockSpec(t.shape, lambda i: (0,) * t.ndim)
    out = jax.ShapeDtypeStruct((H, T, LANES), BF16)
    o_spec = pl.BlockSpec((H, tm, LANES), lambda i: (0, i, 0))
    return pl.pallas_call(
        body, name=name, grid=(T // tm,),
        in_specs=[pl.BlockSpec((tm, D), lambda i: (i, 0)), pl.BlockSpec((tm, 3 * LANES), lambda i: (i, 0)),
                  whole(w_pads[0]), whole(w_pads[1]), whole(sels[0]), whole(sels[1]), whole(consts[0]), whole(consts[1])],
        out_specs=[o_spec, o_spec], out_shape=[out, out],
        compiler_params=_cp(("parallel",)),
    )(hn, cp, w_pads[0], w_pads[1], sels[0], sels[1], consts[0], consts[1])


def _lane_lo():
    return lax.broadcasted_iota(jnp.int32, (1, LANES), 1) < HEAD_DIM


def _attn_fwd_t(qp, kp, vt, name):
    H, T, _ = qp.shape
    tq = min(TQ, T)
    hd = HEAD_DIM
    ext = hd + 16

    def body(q_ref, k_ref, vt_ref, o_ref, o32_ref, lse_ref, m_sc, acc_sc):
        i = pl.program_id(1)
        m_sc[...] = jnp.full(m_sc.shape, NEG, F32)
        acc_sc[...] = jnp.zeros_like(acc_sc)
        q_t = [jnp.transpose(q_ref[h].astype(F32)).astype(BF16) for h in range(2)]
        ones_rows = jnp.where(lax.broadcasted_iota(jnp.int32, (16, tq), 0) == 0, 1.0, 0.0).astype(BF16)

        def steps(blocks):
            offs = [pl.multiple_of(j * tq, tq) for j, _ in blocks]
            s_all = [[jnp.dot(k_ref[h, pl.ds(off, tq), :], q_t[h], preferred_element_type=F32) for h in range(2)]
                     for off in offs]
            for (j, masked), off, s_blk in zip(blocks, offs, s_all):
                for h in range(2):
                    s = s_blk[h]
                    if masked:
                        kr = lax.broadcasted_iota(jnp.int32, (tq, tq), 0)
                        qc = lax.broadcasted_iota(jnp.int32, (tq, tq), 1)
                        s = jnp.where(qc >= kr, s, NEG)
                    m_prev = m_sc[h]
                    m_new = jnp.maximum(m_prev, jnp.max(s, axis=0, keepdims=True))
                    alpha = jnp.exp2(m_prev - m_new)
                    p16 = jnp.exp2(s - m_new).astype(BF16)
                    v_aug = jnp.concatenate([vt_ref[h * hd:(h + 1) * hd, pl.ds(off, tq)], ones_rows], axis=0)
                    pv = jnp.dot(v_aug, p16, preferred_element_type=F32)
                    acc_sc[h] = alpha * acc_sc[h] + pv
                    m_sc[h] = m_new

        def group_body(t, carry):
            steps([(KV_UNROLL * t + u, False) for u in range(KV_UNROLL)])
            return carry

        lax.fori_loop(0, i // KV_UNROLL, group_body, 0)
        for rem in range(KV_UNROLL):

            @pl.when(i % KV_UNROLL == rem)
            def _(rem=rem):
                steps([(i - rem + u, u == rem) for u in range(rem + 1)])

        o_t, lse_t = [], []
        for h in range(2):
            acc = acc_sc[h]
            l = acc[hd:hd + 1, :]
            o_t.append(acc[:hd, :] / l)
            lse_t.append(jnp.broadcast_to(m_sc[h] + jnp.log(l) * LOG2E, (hd, tq)))
        o = jnp.transpose(jnp.concatenate(o_t, axis=0))
        o_ref[...] = o.astype(BF16)
        o32_ref[...] = o
        lse_ref[...] = jnp.transpose(jnp.concatenate(lse_t, axis=0))

    oblk = pl.BlockSpec((tq, LANES), lambda p, i: (i, p))
    return pl.pallas_call(
        body, name=name, grid=(H // 2, T // tq),
        in_specs=[pl.BlockSpec((2, tq, LANES), lambda p, i: (p, i, 0)),
                  pl.BlockSpec((2, T, LANES), lambda p, i: (p, 0, 0)),
                  pl.BlockSpec((2 * hd, T), lambda p, i: (p, 0))],
        out_specs=[oblk, oblk, pl.BlockSpec((None, tq, LANES), lambda p, i: (p, i, 0))],
        out_shape=[jax.ShapeDtypeStruct((T, H * HEAD_DIM), BF16), jax.ShapeDtypeStruct((T, H * HEAD_DIM), F32),
                   jax.ShapeDtypeStruct((H // 2, T, LANES), F32)],
        scratch_shapes=[pltpu.VMEM((2, 1, tq), F32), pltpu.VMEM((2, ext, tq), F32)],
        compiler_params=_cp(("parallel", "arbitrary")),
    )(qp, kp, vt)


def _attn_bwd(qp, kp, vt, o, do, lse, scale, name):
    H, T, _ = qp.shape
    tq = min(TQ, T)
    nq = T // tq
    nrep = tq // LANES

    def body(q_ref, k_ref, vt_ref, o_ref, do_ref, lse_ref, dq_ref, dk_ref, dv_ref, dqe_ref, dke_ref, dk_sc, dv_sc, dq_sc):
        i = pl.program_id(1)

        @pl.when(i == 0)
        def _():
            dk_sc[...] = jnp.zeros_like(dk_sc)
            dv_sc[...] = jnp.zeros_like(dv_sc)

        dq_sc[...] = jnp.zeros_like(dq_sc)

        lo = _lane_lo()
        dob = do_ref[...]
        dof = dob.astype(F32)
        prod = dof * o_ref[...].astype(F32)
        lse2 = lse_ref[...]
        lse2_sw = pltpu.roll(lse2, HEAD_DIM, 1)
        zero = jnp.zeros_like(dob)
        do_h = [jnp.where(lo, dob, zero), jnp.where(lo, zero, dob)]
        rep = lambda col: jnp.broadcast_to(col, (tq, LANES))
        delta = [rep(jnp.sum(jnp.where(lo, prod, 0.0), axis=-1, keepdims=True)),
                 rep(jnp.sum(jnp.where(lo, 0.0, prod), axis=-1, keepdims=True))]
        lse_h = [jnp.where(lo, lse2, lse2_sw), jnp.where(lo, lse2_sw, lse2)]
        qs = [q_ref[0], q_ref[1]]
        tr16 = lambda a: jnp.transpose(a.astype(F32)).astype(BF16)
        q_t = [tr16(qs[0]), tr16(qs[1])]
        do_t = [tr16(do_h[0]), tr16(do_h[1])]

        def steps(blocks):
            offs = [pl.multiple_of(j * tq, tq) for j, _ in blocks]
            vblks = [vt_ref[:, pl.ds(off, tq)] for off in offs]
            kblks = [[k_ref[h, pl.ds(off, tq), :] for h in range(2)] for off in offs]
            s_all = [[_dot_nt(qs[h], kb[h]) for h in range(2)] for kb in kblks]
            dp_all = [[jnp.dot(do_h[h], vb, preferred_element_type=F32) for h in range(2)] for vb in vblks]
            for b, ((j, masked), off) in enumerate(zip(blocks, offs)):
                dv_add = None
                for h in range(2):
                    kblk, s, dp = kblks[b][h], s_all[b][h], dp_all[b][h]
                    p16, ds16 = [], []
                    for c in range(nrep):
                        cols = slice(c * LANES, (c + 1) * LANES)
                        p = jnp.exp2(s[:, cols] - lse_h[h])
                        if masked:
                            r = lax.broadcasted_iota(jnp.int32, (tq, LANES), 0)
                            cc = lax.broadcasted_iota(jnp.int32, (tq, LANES), 1)
                            p = jnp.where(r >= cc + c * LANES, p, 0.0)
                        p16.append(p.astype(BF16))
                        ds16.append((p * (dp[:, cols] - delta[h])).astype(BF16))
                    p16 = jnp.concatenate(p16, axis=1)
                    dsb = jnp.concatenate(ds16, axis=1)
                    dq_sc[h] += jnp.dot(dsb, kblk, preferred_element_type=F32)
                    dk_sc[h, :, pl.ds(off, tq)] += jnp.dot(q_t[h], dsb, preferred_element_type=F32)
                    pv = jnp.dot(do_t[h], p16, preferred_element_type=F32)
                    dv_add = pv if dv_add is None else dv_add + pv
                dv_sc[:, pl.ds(off, tq)] += dv_add

        def group_body(t, carry):
            steps([(KV_UNROLL_BWD * t + u, False) for u in range(KV_UNROLL_BWD)])
            return carry

        lax.fori_loop(0, i // KV_UNROLL_BWD, group_body, 0)
        for rem in range(KV_UNROLL_BWD):

            @pl.when(i % KV_UNROLL_BWD == rem)
            def _(rem=rem):
                steps([(i - rem + u, u == rem) for u in range(rem + 1)])

        dq0, dq1 = dq_sc[0], dq_sc[1]
        dq_ref[...] = (jnp.where(lo, dq0, pltpu.roll(dq1, HEAD_DIM, 1)) * scale).astype(BF16)
        row8 = lax.broadcasted_iota(jnp.int32, (8, 1), 0)
        pick = lambda blk, r: jnp.sum(jnp.where(row8 == r, blk, 0.0), axis=0, keepdims=True)
        two_rows = lambda a, b: jnp.where(row8 == 0, a, jnp.where(row8 == 1, b, 0.0))
        gate_rows = slice(HEAD_DIM, HEAD_DIM + 8)
        dqe_ref[...] = two_rows(pick(jnp.transpose(dq0)[gate_rows, :], 0), pick(jnp.transpose(dq1)[gate_rows, :], 0))

        @pl.when(i == nq - 1)
        def _():
            dke_ref[...] = two_rows(pick(dk_sc[0, gate_rows, :], 3), pick(dk_sc[1, gate_rows, :], 3))
            for cb in range(nq):
                tok = slice(cb * tq, (cb + 1) * tq)
                dk0 = jnp.transpose(dk_sc[0, :, tok])
                dk1 = jnp.transpose(dk_sc[1, :, tok])
                dk_ref[tok, :] = (jnp.where(lo, dk0, pltpu.roll(dk1, HEAD_DIM, 1)) * LN2).astype(BF16)
                dv_ref[tok, :] = jnp.transpose(dv_sc[:, tok]).astype(BF16)

    qblk = pl.BlockSpec((tq, LANES), lambda p, i: (i, p))
    pair = pl.BlockSpec((T, LANES), lambda p, i: (0, p))
    tok16 = jax.ShapeDtypeStruct((T, H * HEAD_DIM), BF16)
    gate32 = jax.ShapeDtypeStruct((H // 2, 8, T), F32)
    return pl.pallas_call(
        body, name=name, grid=(H // 2, nq),
        in_specs=[pl.BlockSpec((2, tq, LANES), lambda p, i: (p, i, 0)),
                  pl.BlockSpec((2, T, LANES), lambda p, i: (p, 0, 0)),
                  pl.BlockSpec((LANES, T), lambda p, i: (p, 0)), qblk, qblk,
                  pl.BlockSpec((None, tq, LANES), lambda p, i: (p, i, 0))],
        out_specs=[qblk, pair, pair, pl.BlockSpec((None, 8, tq), lambda p, i: (p, 0, i)),
                   pl.BlockSpec((None, 8, T), lambda p, i: (p, 0, 0))],
        out_shape=[tok16, tok16, tok16, gate32, gate32],
        scratch_shapes=[pltpu.VMEM((2, LANES, T), F32), pltpu.VMEM((LANES, T), F32),
                        pltpu.VMEM((2, tq, LANES), F32)],
        compiler_params=_cp(("parallel", "arbitrary")),
    )(qp, kp, vt, o, do, lse)


def _mesh_pos():
    return lax.axis_index("x"), lax.axis_index("y"), lax.axis_index("c")


def _all_gather(arrs, name, groups=None):
    n = len(arrs)
    if groups is None:
        groups = [(a, 0) for a in range(n)]
    ng = 1 + max(g for g, _ in groups)
    per_group = [sum(1 for g, _ in groups if g == gi) for gi in range(ng)]
    first_of = [next(a for a in range(n) if groups[a][0] == gi) for gi in range(ng)]

    def body(*refs):
        ins, outs = refs[:n], refs[n:n + ng]
        send_sems, recv_sems, local_sems = refs[n + ng:]
        x, y, c = _mesh_pos()
        me, sib = (x, y, c), (x, y, 1 - c)
        chips = [(1 - x, y), (x, 1 - y), (1 - x, 1 - y)]

        def dst_of(a, px, py, pc):
            g, k = groups[a]
            return outs[g].at[N_DEV * k + 4 * px + 2 * py + pc]

        def copy(a, k, block, to, src=None):
            dst = dst_of(a, *block)
            return pltpu.make_async_remote_copy(
                src_ref=dst if src is None else src, dst_ref=dst,
                send_sem=send_sems.at[a, k], recv_sem=recv_sems.at[a, k], device_id=to, device_id_type=MESH)

        mine = [pltpu.make_async_copy(ins[a], dst_of(a, *me), local_sems.at[a]) for a in range(n)]
        for cp in mine:
            cp.start()
        first = []
        for a in range(n):
            first.append(copy(a, 0, me, sib, src=ins[a]))
            first += [copy(a, 1 + j, me, (*chip, c), src=ins[a]) for j, chip in enumerate(chips)]
        for cp in first:
            cp.start()
        passed = []
        for j, chip in enumerate(chips):
            for a in range(n):
                copy(a, 1 + j, (*chip, c), me).wait_recv()
                fwd = copy(a, 4 + j, (*chip, c), sib)
                fwd.start()
                passed.append(fwd)
        for a in range(n):
            copy(a, 0, sib, me).wait_recv()
            for j, chip in enumerate(chips):
                copy(a, 4 + j, (*chip, 1 - c), me).wait_recv()
        for cp in first + passed:
            cp.wait_send()
        for cp in mine:
            cp.wait()

    any_spec = pl.BlockSpec(memory_space=pl.ANY)
    return pl.pallas_call(
        body, name=name,
        in_specs=[any_spec] * n, out_specs=[any_spec] * ng,
        out_shape=[jax.ShapeDtypeStruct((N_DEV * per_group[gi],) + arrs[first_of[gi]].shape, arrs[first_of[gi]].dtype)
                   for gi in range(ng)],
        scratch_shapes=[pltpu.SemaphoreType.DMA((n, 7)), pltpu.SemaphoreType.DMA((n, 7)),
                        pltpu.SemaphoreType.DMA((n,))],
    )(*arrs)


HBM_SPEC = pl.BlockSpec(memory_space=pltpu.HBM)
SEM_SPEC = pl.BlockSpec(memory_space=pltpu.SEMAPHORE)
ANY_SPEC = pl.BlockSpec(memory_space=pl.ANY)
DATAFLOW_EFFECT = pltpu.SideEffectType.DATAFLOW_SIDE_EFFECTING


def _peers():
    x, y, c = _mesh_pos()
    flip = lambda v, b: 1 - v if b else v
    return [(flip(x, (k >> 2) & 1), flip(y, (k >> 1) & 1), flip(c, k & 1)) for k in range(1, N_DEV)]


def _slot(p):
    return 4 * p[0] + 2 * p[1] + p[2]


def _direct_copy(src_refs, land_refs, sems, a, k, p, land_of, dst_slot, src_slot):
    s = src_slot(a, p)
    return pltpu.make_async_remote_copy(
        src_ref=src_refs[a] if s is None else src_refs[a].at[s], dst_ref=land_refs[land_of[a]].at[dst_slot(a, k)],
        send_sem=sems[0].at[a * (N_DEV - 1) + k], recv_sem=sems[1].at[a * (N_DEV - 1) + k], device_id=p,
        device_id_type=MESH)


def _direct_start(srcs, lands, land_of, dst_slot, src_slot, after, name, collective_id):
    n, nl = len(srcs), len(lands)

    def body(*refs):
        src_refs, land_refs = refs[:n], refs[n:n + nl]
        sems = (refs[n + nl + 1], refs[n + nl + 2])
        token = refs[-1]
        peers = _peers()
        barrier = pltpu.get_barrier_semaphore()
        for p in peers:
            pl.semaphore_signal(barrier, inc=1, device_id=p, device_id_type=MESH)
        pl.semaphore_wait(barrier, N_DEV - 1)
        for a in range(n):
            for k, p in enumerate(peers):
                _direct_copy(src_refs, land_refs, sems, a, k, p, land_of, dst_slot, src_slot).start()
        token[...] = jnp.zeros_like(token)

    hbm = lambda t: pltpu.HBM(t.shape, t.dtype)
    sem_t = pltpu.SemaphoreType.DMA((n * (N_DEV - 1),))
    outs = pl.pallas_call(
        body, name=name,
        out_shape=(sem_t, sem_t, *[hbm(t) for t in srcs], *[hbm(t) for t in lands], jax.ShapeDtypeStruct((8, LANES), F32)),
        in_specs=[HBM_SPEC] * (n + nl) + [ANY_SPEC],
        out_specs=(SEM_SPEC, SEM_SPEC, *([HBM_SPEC] * (n + nl)), pl.BlockSpec(memory_space=pltpu.VMEM)),
        input_output_aliases={i: 2 + i for i in range(n + nl)},
        compiler_params=pltpu.CompilerParams(has_side_effects=DATAFLOW_EFFECT, collective_id=collective_id),
    )(*[pltpu.with_memory_space_constraint(t, pltpu.HBM) for t in srcs],
      *[pltpu.with_memory_space_constraint(t, pltpu.HBM) for t in lands], after)
    return outs[0], outs[1], list(outs[2:2 + n]), list(outs[2 + n:2 + n + nl]), outs[-1]


def _direct_wait(send_sems, recv_sems, srcs, lands, land_of, idxs, dst_slot, src_slot, after, name):
    land_ids = []
    for a in idxs:
        if land_of[a] not in land_ids:
            land_ids.append(land_of[a])
    m, ml = len(idxs), len(land_ids)
    sub_land_of = {j: land_ids.index(land_of[a]) for j, a in enumerate(idxs)}

    def body(*refs):
        src_refs, land_refs = refs[:m], refs[m:m + ml]
        ssem, rsem = refs[m + ml], refs[m + ml + 1]
        for j, a in enumerate(idxs):
            for k, p in enumerate(_peers()):
                s = src_slot(a, p)
                cp = pltpu.make_async_remote_copy(
                    src_ref=src_refs[j] if s is None else src_refs[j].at[s],
                    dst_ref=land_refs[sub_land_of[j]].at[dst_slot(a, k)],
                    send_sem=ssem.at[a * (N_DEV - 1) + k], recv_sem=rsem.at[a * (N_DEV - 1) + k], device_id=p,
                    device_id_type=MESH)
                cp.wait_send()
                cp.wait_recv()

    hbm = lambda t: pltpu.HBM(t.shape, t.dtype)
    sub_s, sub_l = [srcs[a] for a in idxs], [lands[g] for g in land_ids]
    outs = pl.pallas_call(
        body, name=name,
        out_shape=(*[hbm(t) for t in sub_s], *[hbm(t) for t in sub_l]),
        in_specs=[HBM_SPEC] * (m + ml) + [SEM_SPEC, SEM_SPEC, ANY_SPEC],
        out_specs=tuple([HBM_SPEC] * (m + ml)),
        input_output_aliases={i: i for i in range(m + ml)},
        compiler_params=pltpu.CompilerParams(has_side_effects=DATAFLOW_EFFECT),
    )(*sub_s, *sub_l, send_sems, recv_sems, after)
    return list(outs[:m]), list(outs[m:])


def _row_block(R, C):
    best = None
    for d in range(16, R + 1, 16):
        if R % d == 0 and d * C <= 256 * 1024:
            best = d
    return best if best is not None else R


def _adamw_math(w, g, m, v):
    m = ADAM_B1 * m + (1.0 - ADAM_B1) * g
    v = ADAM_B2 * v + (1.0 - ADAM_B2) * (g * g)
    m_hat = m / (1.0 - ADAM_B1 ** ADAM_STEP)
    v_hat = v / (1.0 - ADAM_B2 ** ADAM_STEP)
    delta = -ADAM_LR * (m_hat / (jnp.sqrt(v_hat) + ADAM_EPS) + ADAM_WD * w)
    return delta, m, v


def _sum_adamw(parts, w, m, v, name, sel=None):
    R, C = w.shape
    nseg = max(len(arr) if isinstance(arr, list) else 1 for arr, _ in parts)
    tr = _row_block(R // nseg, C)
    bps = R // nseg // tr
    specs, args = [], []
    for arr, idxs in parts:
        pieces = arr if isinstance(arr, list) else [arr] * nseg
        for idx in idxs:
            for sg in range(nseg if isinstance(arr, list) else 1):
                row = (lambda i, sg=sg: jnp.clip(i - sg * bps, 0, bps - 1)) if isinstance(arr, list) else (lambda i: i)
                if idx < 0:
                    specs.append(pl.BlockSpec((None, tr, C), lambda i, s, row=row: (s[0], row(i), 0)))
                else:
                    specs.append(pl.BlockSpec((None, tr, C), lambda i, s, idx=idx, row=row: (idx, row(i), 0)))
                args.append(pieces[sg])
    seg_counts = [(nseg if isinstance(arr, list) else 1) for arr, idxs in parts for _ in idxs]
    npart = len(args)
    blk = pl.BlockSpec((tr, C), lambda i, s: (i, 0))

    def body(s_ref, *refs):
        del s_ref
        seg = pl.program_id(0) // bps
        g, at = None, 0
        for cnt in seg_counts:
            term = refs[at][...].astype(F32)
            for sg in range(1, cnt):
                term = jnp.where(seg == sg, refs[at + sg][...].astype(F32), term)
            g = term if g is None else g + term
            at += cnt
        w_ref, m_ref, v_ref, g_out, d_out, m_out, v_out = refs[npart:]
        delta, mm, vv = _adamw_math(w_ref[...], g, m_ref[...], v_ref[...])
        g_out[...] = g
        d_out[...] = delta
        m_out[...] = mm
        v_out[...] = vv

    grid_spec = pltpu.PrefetchScalarGridSpec(
        num_scalar_prefetch=1, grid=(R // tr,),
        in_specs=specs + [blk, blk, blk], out_specs=[blk] * 4)
    if sel is None:
        sel = jnp.zeros((1,), jnp.int32)
    return pl.pallas_call(
        body, name=name, grid_spec=grid_spec,
        out_shape=[jax.ShapeDtypeStruct((R, C), F32)] * 4,
        compiler_params=_cp(("parallel",)),
    )(sel, *args, w, m, v)


def _rows(a, c):
    return a.reshape(-1, c)


def _pad_rows(a, r):
    return jnp.pad(a, ((0, r - a.shape[0]), (0, 0))) if a.shape[0] != r else a


def _gate_tables():
    hp = N_HEADS // 2
    sel_q = np.zeros((hp, 3 * LANES, 2 * LANES), np.float32)
    sel_k = np.zeros((hp, 3 * LANES, 2 * LANES), np.float32)
    const_q = np.zeros((hp, 1, 2 * LANES), np.float32)
    const_k = np.zeros((hp, 1, 2 * LANES), np.float32)
    for p in range(hp):
        for hh in range(2):
            h = 2 * p + hh
            base = hh * LANES + HEAD_DIM
            for piece in range(3):
                sel_q[p, piece * LANES + h, base + piece] = 1.0
                sel_k[p, piece * LANES + h, base + 3 + piece] = -1.0
            const_k[p, 0, base:base + 3] = 1.0
            const_q[p, 0, base + 3:base + 6] = 1.0
    as_bf = lambda t: jnp.asarray(t, BF16)
    return as_bf(sel_q), as_bf(sel_k), jnp.asarray(const_q), jnp.asarray(const_k)


def _pad_heads(w):
    d = w.shape[0]
    w3 = w.reshape(d, N_HEADS, HEAD_DIM)
    return jnp.pad(w3, ((0, 0), (0, 0), (0, LANES - HEAD_DIM))).reshape(d, N_HEADS * LANES)


def kernel(x, mix_norm_g, ffn_norm_g, gm_w_in, gm_ln_g, gm_ln_b, gm_w_s, gm_b_s, gm_w_out, fox_w_qkvf, fox_b_f, fox_w_o, ffn_w_gate, ffn_w_up, ffn_conv_w, ffn_conv_b, ffn_w_down, final_norm_g, loss_target, m_mix_norm_g, m_ffn_norm_g, m_gm_w_in, m_gm_ln_g, m_gm_ln_b, m_gm_w_s, m_gm_b_s, m_gm_w_out, m_fox_w_qkvf, m_fox_b_f, m_fox_w_o, m_ffn_w_gate, m_ffn_w_up, m_ffn_conv_w, m_ffn_conv_b, m_ffn_w_down, m_final_norm_g, v_mix_norm_g, v_ffn_norm_g, v_gm_w_in, v_gm_ln_g, v_gm_ln_b, v_gm_w_s, v_gm_b_s, v_gm_w_out, v_fox_w_qkvf, v_fox_b_f, v_fox_w_o, v_ffn_w_gate, v_ffn_w_up, v_ffn_conv_w, v_ffn_conv_b, v_ffn_w_down, v_final_norm_g):
    T, D = x.shape[1], x.shape[2]
    E = gm_ln_g.shape[1]
    FF = ffn_conv_b.shape[1]
    NQKVF = 3 * D + N_HEADS
    xi, yi, ci = _mesh_pos()
    me = 4 * xi + 2 * yi + ci
    h0 = x.reshape(T, D)
    tgt = loss_target.reshape(T, D)

    nl = ffn_w_gate.shape[0]
    to16 = lambda a: a.astype(BF16)
    n_cw_rows = ffn_conv_w.size // LANES
    cw_rows = _pad_rows(_rows(ffn_conv_w.astype(F32), LANES), 16)
    (w_in_g,) = _all_gather([to16(gm_w_in[0])], "ag_weights")
    later, land_of, land_off, lands, src_ids = [], [], [], [], {}

    def add_sources(key, srcs, new_lands, of, offs):
        src_ids[key] = list(range(len(later), len(later) + len(srcs)))
        land_of.extend(len(lands) + o for o in of)
        land_off.extend(offs)
        later.extend(srcs)
        lands.extend(new_lands)

    def add_ffn(l):
        add_sources(f"ffn{l}", [to16(ffn_w_gate[l]), to16(ffn_w_up[l]), to16(ffn_w_down[l])],
                    [lax.empty((2 * N_DEV, D, FF // N_DEV), BF16), lax.empty((N_DEV, FF // N_DEV, D), BF16)],
                    [0, 0, 1], [0, N_DEV, 0])

    add_sources("gm_out", [to16(gm_w_out[0]), cw_rows],
                [lax.empty((N_DEV, E // N_DEV, D), BF16), lax.empty((N_DEV,) + cw_rows.shape, F32)], [0, 1], [0, 0])
    add_ffn(0)
    add_sources("fox", [to16(fox_w_qkvf[0]), to16(fox_w_o[0])],
                [lax.empty((N_DEV, D, NQKVF // N_DEV), BF16), lax.empty((N_DEV, D // N_DEV, D), BF16)], [0, 1], [0, 0])
    for l in range(1, nl):
        add_ffn(l)
    ag_dst = lambda a, k: land_off[a] + _slot(_mesh_pos())
    ag_src = lambda a, p: None
    ag_send, ag_recv, later, lands, ag_token = _direct_start(later, lands, land_of, ag_dst, ag_src, w_in_g,
                                                             "ag_later_start", collective_id=1)

    def own_blocks(landed, shards, offs):
        for s, o in zip(shards, offs):
            landed = lax.dynamic_update_index_in_dim(landed, s, o + me, 0)
        return landed

    def gather_wait(idxs, after, name):
        return _direct_wait(ag_send, ag_recv, later, lands, land_of, idxs, ag_dst, ag_src, after, name)

    ffn_w = {}

    def ffn_weights(l):
        return ffn_w[l]

    def land_ffn(l, shards, gu_land, dn_land):
        ffn_w[l] = (own_blocks(gu_land, shards[:2], [0, N_DEV]), own_blocks(dn_land, shards[2:3], [0]).reshape(FF, D))

    saved = {}

    def ffn_fwd(l, h_in, hn, next_g):
        wgul, wdl = ffn_weights(l)
        au, act = _ffn_up_fused(hn, wgul, conv_w_full[l], ffn_conv_b[l:l + 1], f"ffn{l}_up")
        saved[f"ffn{l}"] = (h_in, hn, au, act)
        if next_g is None:
            return _mm_nn(act, wdl, f"ffn{l}_down", res=h_in), None
        return _mm_nn(act, wdl, f"ffn{l}_down", res=h_in, norm_g=next_g)

    bs_col = gm_b_s[0].reshape(GM_GROUPS, CHUNK, 1)
    hn0 = _rms_fwd(h0, mix_norm_g[0:1], "mix0_norm", after=ag_token)
    z, gu = _gm_in_fused(hn0, w_in_g, gm_ln_g, gm_ln_b, gm_w_s[0], bs_col, "gm_in")
    mine_o, land_o = gather_wait(src_ids["gm_out"], z, "ag_wout_wait")
    w_out_g = own_blocks(land_o[0], mine_o[0:1], [0]).reshape(E, D)
    cwg = own_blocks(land_o[1], mine_o[1:2], [0])
    conv_w_full = jnp.transpose(cwg[:, :n_cw_rows].reshape(N_DEV, nl, 3, FF // N_DEV), (1, 2, 0, 3)).reshape(nl, 3, FF)
    h1, hn_f0 = _mm_nn(gu, w_out_g, "gm_out", res=h0, norm_g=ffn_norm_g[0:1])
    mine0, land0 = gather_wait(src_ids["ffn0"], h1, "ag_ffn0_wait")
    land_ffn(0, mine0, *land0)
    h2, hn2 = ffn_fwd(0, h1, hn_f0, mix_norm_g[1:2])

    mine_x, land_x = gather_wait(src_ids["fox"], h2, "ag_fox_wait")
    w_qkvf = jnp.transpose(own_blocks(land_x[0], mine_x[0:1], [0]), (1, 0, 2)).reshape(D, NQKVF)
    w_o_g = own_blocks(land_x[1], mine_x[1:2], [0]).reshape(D, D)
    w_q, w_k, w_v = w_qkvf[:, :D], w_qkvf[:, D:2 * D], w_qkvf[:, 2 * D:3 * D]
    w_f = jnp.pad(w_qkvf[:, 3 * D:], ((0, 0), (0, LANES - N_HEADS)))
    bf_row = jnp.pad(fox_b_f, ((0, 0), (0, LANES - N_HEADS)))
    sel_q, sel_k, const_q, const_k = _gate_tables()
    scale = HEAD_DIM ** -0.5
    f_logit = _mm_nn(hn2, w_f, "fox_f")
    cp, sneg = _gate_scan(f_logit, bf_row, "fox_scan")
    qp, kp = _qk_proj(hn2, (_pad_heads(w_q), _pad_heads(w_k)), cp, (sel_q, sel_k), (const_q, const_k),
                      (scale * LOG2E, 1.0), "fox_qk")
    vt = _mm_out_t(hn2, jnp.transpose(w_v), "fox_v")
    o, o32, lse = _attn_fwd_t(qp, kp, vt, "fox_attn")
    h3, hn_f1 = _mm_nn(o, w_o_g, "fox_o", res=h2, norm_g=ffn_norm_g[1:2])
    mine1, land1w = gather_wait([a for l in range(1, nl) for a in src_ids[f"ffn{l}"]], h3, "ag_ffn1_wait")
    for l in range(1, nl):
        land_ffn(l, mine1[3 * (l - 1):3 * l], land1w[2 * (l - 1)], land1w[2 * (l - 1) + 1])
    h4, _ = ffn_fwd(1, h3, hn_f1, None)

    dh, dh16, d_final, loss_row = _loss_head(h4, tgt, final_norm_g.reshape(1, D), "loss_head")
    loss = lax.psum(loss_row[0, 0], ("x", "y", "c"))

    rs_dst = lambda a, k: k
    rs_src = lambda a, p: _slot(p)
    me_idx = me.astype(jnp.int32).reshape(1)

    def rs_start(grads, name, cid):
        lands = [lax.empty((N_DEV - 1,) + g.shape[1:], BF16) for g in grads]
        return _direct_start(grads, lands, list(range(len(grads))), rs_dst, rs_src, loss_row, name, collective_id=cid)

    def rs_wait(st, after, name):
        n = len(st[2])
        return _direct_wait(st[0], st[1], st[2], st[3], list(range(n)), list(range(n)), rs_dst, rs_src, after, name)

    def ffn_bwd(l, dh, dh16, after=None):
        wgul, wdl = ffn_weights(l)
        h_in, hn, au, act = saved[f"ffn{l}"]
        da, dup, d_cw, d_cb = _ffn_mid_bwd(au, dh16, wdl, conv_w_full[l], ffn_conv_b[l:l + 1], f"ffn{l}_dmid", after=after)
        d_wd = _mm_tn(act, dh16, f"ffn{l}_dwd", out_dtype=BF16)
        dh_in, dh_in16, d_norm = _mm_nt([da, dup], wgul, f"ffn{l}_dhn", norm_bwd=(h_in, ffn_norm_g[l:l + 1], dh))
        d_wg = _mm_tn(hn, da, f"ffn{l}_dwg", blocked_w=FF // N_DEV, out_dtype=BF16)
        d_wu = _mm_tn(hn, dup, f"ffn{l}_dwu", blocked_w=FF // N_DEV, out_dtype=BF16)
        big_g = [d_wg, d_wu, d_wd.reshape(N_DEV, FF // N_DEV, D)]
        return dh_in, dh_in16, big_g, dict(cw=d_cw, cb=d_cb, norm=d_norm)

    dh, dh16, big_ffn1, g_ffn1 = ffn_bwd(1, dh, dh16)

    do = _mm_nt([dh16], w_o_g, "fox_do", out_dtype=BF16)
    d_wo = _mm_tn(o, dh16, "fox_dwo", out_dtype=BF16)
    dq, dk, dv, dqe, dke = _attn_bwd(qp, kp, vt, o32, do, lse, scale, "fox_dattn")
    gate_lane = lambda e: jnp.pad(jnp.transpose(e[:, :2, :].reshape(N_HEADS, T)), ((0, 0), (0, LANES - N_HEADS)))
    df, d_bf = _gate_scan_bwd(gate_lane(dqe), gate_lane(dke), sneg, "fox_dscan")
    dhn = _mm_nt([df], w_f, "fox_dhn_f")
    dh_mix1 = _mm_nt([dq, dk, dv], w_qkvf[:, :3 * D], "fox_dhn_qkv", add=dhn, norm_bwd=(h2, mix_norm_g[1:2], dh))
    d_wq = _mm_tn(hn2, dq, "fox_dwq", out_dtype=BF16)
    d_wk = _mm_tn(hn2, dk, "fox_dwk", out_dtype=BF16)
    d_wv = _mm_tn(hn2, dv, "fox_dwv", out_dtype=BF16)
    d_wf = _mm_tn(hn2, df, "fox_dwf", out_dtype=BF16)
    d_wqkvf = jnp.concatenate([d_wq, d_wk, d_wv, d_wf[:, :N_HEADS]], axis=1)
    dh, dh16, d_mix1 = dh_mix1
    st1 = rs_start([jnp.transpose(d_wqkvf.reshape(D, N_DEV, NQKVF // N_DEV), (1, 0, 2)),
                    d_wo.reshape(N_DEV, D // N_DEV, D)] + big_ffn1, "rs1_start", 2)

    dh, dh16, big_ffn0, g_ffn0 = ffn_bwd(0, dh, dh16, after=st1[4])
    d_wout = _mm_tn(gu, dh16, "gm_dwout", out_dtype=BF16)
    st2 = rs_start(big_ffn0 + [d_wout.reshape(N_DEV, E // N_DEV, D)], "rs2_start", 3)
    dz, d_lng, d_lnb, d_ws, d_bs = _sgu_bwd(z, dh16, w_out_g, gm_ln_g, gm_ln_b, gm_w_s[0], bs_col, "gm_dsgu", after=st2[4])
    d_win = _mm_tn(hn0, dz, "gm_dwin", blocked_w=2 * E // N_DEV, out_dtype=BF16)
    st3 = rs_start([d_win], "rs3_start", 4)
    dx, _, d_mix0 = _mm_nt([dz], w_in_g, "gm_dhn", after=st3[4], norm_bwd=(h0, mix_norm_g[0:1], dh))

    small = [("mix_norm_g", mix_norm_g, m_mix_norm_g, v_mix_norm_g, jnp.concatenate([d_mix0, d_mix1], axis=0)),
             ("ffn_norm_g", ffn_norm_g, m_ffn_norm_g, v_ffn_norm_g, jnp.concatenate([g_ffn0["norm"], g_ffn1["norm"]], axis=0)),
             ("gm_ln_g", gm_ln_g, m_gm_ln_g, v_gm_ln_g, d_lng),
             ("gm_ln_b", gm_ln_b, m_gm_ln_b, v_gm_ln_b, d_lnb),
             ("gm_w_s", gm_w_s, m_gm_w_s, v_gm_w_s, d_ws),
             ("gm_b_s", gm_b_s, m_gm_b_s, v_gm_b_s, d_bs),
             ("fox_b_f", fox_b_f, m_fox_b_f, v_fox_b_f, d_bf[:, :N_HEADS]),
             ("ffn_conv_b", ffn_conv_b, m_ffn_conv_b, v_ffn_conv_b, jnp.concatenate([g_ffn0["cb"], g_ffn1["cb"]], axis=0)),
             ("final_norm_g", final_norm_g, m_final_norm_g, v_final_norm_g, d_final)]
    d_cw_full = jnp.stack([g_ffn0["cw"], g_ffn1["cw"]], axis=0)

    def small_rows(a):
        flat = a.astype(F32).reshape(-1)
        n = -(-flat.size // (8 * LANES)) * (8 * LANES)
        return jnp.pad(flat, (0, n - flat.size)).reshape(-1, LANES)

    s_rows = [small_rows(p[1]).shape[0] for p in small]
    s_off = np.concatenate([[0], np.cumsum(s_rows)]).tolist()
    cw_g_rows = small_rows(d_cw_full)
    g_small = jnp.concatenate([small_rows(p[4]) for p in small] + [cw_g_rows], axis=0)

    own1, land1 = rs_wait(st1, dx, "rs1_wait")
    own2, land2 = rs_wait(st2, land1[0], "rs2_wait")
    big_out = {}

    def big_adamw(name, w, m, v, own, landed):
        shard2d = lambda a, c=(own[0] if isinstance(own, list) else own).shape[2]: a.reshape(-1, c)
        res = _sum_adamw([(own, [-1]), (landed, list(range(N_DEV - 1)))], shard2d(w), shard2d(m), shard2d(v),
                         f"adamw_{name}", sel=me_idx)
        big_out[name] = [t.reshape(w.shape) for t in res]

    big_adamw("fox_w_qkvf", fox_w_qkvf, m_fox_w_qkvf, v_fox_w_qkvf, own1[0], land1[0])
    big_adamw("fox_w_o", fox_w_o, m_fox_w_o, v_fox_w_o, own1[1], land1[1])
    big_adamw("ffn_w_gate", ffn_w_gate, m_ffn_w_gate, v_ffn_w_gate, [own2[0], own1[2]], [land2[0], land1[2]])
    big_adamw("ffn_w_up", ffn_w_up, m_ffn_w_up, v_ffn_w_up, [own2[1], own1[3]], [land2[1], land1[3]])
    big_adamw("ffn_w_down", ffn_w_down, m_ffn_w_down, v_ffn_w_down, [own2[2], own1[4]], [land2[2], land1[4]])
    big_adamw("gm_w_out", gm_w_out, m_gm_w_out, v_gm_w_out, own2[3], land2[3])

    (gs_all,) = _all_gather([g_small], "ag_small_grads")
    zeros_cw = jnp.zeros_like(cw_g_rows)
    cat = lambda k: jnp.concatenate([small_rows(p[k]) for p in small] + [zeros_cw], axis=0)
    small_out = _sum_adamw([(gs_all, list(range(N_DEV)))], cat(1), cat(2), cat(3), "adamw_small")
    gs = small_out[0]

    g_cw_full = gs[s_off[-1]:].reshape(-1)[:d_cw_full.size].reshape(d_cw_full.shape)
    g_cw = lax.dynamic_slice_in_dim(g_cw_full, me * (FF // N_DEV), FF // N_DEV, axis=2)
    cw2 = lambda a: _pad_rows(_rows(a.astype(F32), LANES), 16)
    cw_out = _sum_adamw([(cw2(g_cw)[None], [0])], cw2(ffn_conv_w), cw2(m_ffn_conv_w), cw2(v_ffn_conv_w), "adamw_conv_w")

    own3, land3 = rs_wait(st3, cw_out[0], "rs3_wait")
    big_adamw("gm_w_in", gm_w_in, m_gm_w_in, v_gm_w_in, own3[0], land3[0])

    names = ["mix_norm_g", "ffn_norm_g", "gm_w_in", "gm_ln_g", "gm_ln_b", "gm_w_s", "gm_b_s", "gm_w_out", "fox_w_qkvf",
             "fox_b_f", "fox_w_o", "ffn_w_gate", "ffn_w_up", "ffn_conv_w", "ffn_conv_b", "ffn_w_down", "final_norm_g"]
    small_idx = {p[0]: k for k, p in enumerate(small)}

    def pick(kind, name):
        if name in big_out:
            return big_out[name][kind]
        if name == "ffn_conv_w":
            return cw_out[kind][:n_cw_rows].reshape(ffn_conv_w.shape)
        k = small_idx[name]
        shp = small[k][1].shape
        return small_out[kind][s_off[k]:s_off[k + 1]].reshape(-1)[:int(np.prod(shp))].reshape(shp)

    outs = [loss, dx.reshape(x.shape)]
    for kind in range(4):
        outs += [pick(kind, n) for n in names]
    return tuple(outs)
```

```python
import math

import numpy as np
import jax
import jax.numpy as jnp
from jax import lax
from jax.experimental import pallas as pl
from jax.experimental.pallas import tpu as pltpu

F32 = jnp.float32
BF16 = jnp.bfloat16
MESH = pl.DeviceIdType.MESH

N_HEADS = 16
HEAD_DIM = 64
CHUNK = 128
GM_GROUPS = 8
RMS_EPS = 1e-6
LN_EPS = 1e-5
ADAM_LR = 0.001
ADAM_B1 = 0.9
ADAM_B2 = 0.999
ADAM_EPS = 1e-08
ADAM_WD = 0.01
ADAM_STEP = 10
N_DEV = 8

LANES = 128
VMEM_BYTES_V7X = 64 * 1024 * 1024
VMEM_LIMIT = 56 * 1024 * 1024

TM = 512
TM_MM = 1024
TT = 1024
TQ = 512
TF = 1024
ATTN_QW = 256
KV_UNROLL_BWD = 2
KV_UNROLL = 2
TN_ROWS = 512
MM_BLOCK_BYTES = 8 * 1024 * 1024
NEG = -1e30
LOG2E = math.log2(math.e)
LN2 = math.log(2.0)


def _cp(sem=None, vmem=VMEM_LIMIT):
    return pltpu.CompilerParams(dimension_semantics=sem, vmem_limit_bytes=vmem)


def _gelu(x):
    c = math.sqrt(2.0 / math.pi)
    return x * (0.5 * (1.0 + jnp.tanh(c * (x + 0.044715 * (x * x * x)))))


def _gelu_grad(x):
    c = math.sqrt(2.0 / math.pi)
    t = jnp.tanh(c * (x + 0.044715 * (x * x * x)))
    return 0.5 * (1.0 + t) + x * (0.5 * (1.0 - t * t)) * (c * (1.0 + 3.0 * 0.044715 * (x * x)))


def _sigmoid(x):
    return 1.0 / (1.0 + jnp.exp(-x))


def _dot_nt(a, b):
    return lax.dot_general(a, b, (((1,), (1,)), ((), ())), preferred_element_type=F32)


def _dot_tn(a, b):
    return lax.dot_general(a, b, (((0,), (0,)), ((), ())), preferred_element_type=F32)


def _rms_fwd(h, g, name, after=None):
    T, D = h.shape
    tm = min(TM, T)

    def body(h_ref, g_ref, *rest):
        o_ref = rest[-1]
        x = h_ref[...]
        r = lax.rsqrt(jnp.mean(x * x, axis=-1, keepdims=True) + RMS_EPS)
        o_ref[...] = ((x * r) * g_ref[...]).astype(BF16)

    in_specs = [pl.BlockSpec((tm, D), lambda i: (i, 0)), pl.BlockSpec((1, D), lambda i: (0, 0))]
    args = [h, g]
    if after is not None:
        in_specs.append(pl.BlockSpec(memory_space=pl.ANY))
        args.append(after)
    return pl.pallas_call(
        body, name=name, grid=(T // tm,),
        in_specs=in_specs,
        out_specs=pl.BlockSpec((tm, D), lambda i: (i, 0)),
        out_shape=jax.ShapeDtypeStruct((T, D), BF16),
        compiler_params=_cp(("parallel",)),
    )(*args)


def _loss_head(h, tgt, g, name):
    T, D = h.shape
    tm = min(TM, T)

    def body(h_ref, t_ref, g_ref, o_ref, ob_ref, dg_ref, l_ref):
        x = h_ref[...]
        gg = g_ref[...]
        r = lax.rsqrt(jnp.mean(x * x, axis=-1, keepdims=True) + RMS_EPS)
        xr = x * r
        e = xr * gg - t_ref[...]
        lpart = 0.5 * jnp.sum(jnp.mean(e * e, axis=-1, keepdims=True), axis=0, keepdims=True)
        dy = e * (1.0 / D)
        dyg = dy * gg
        dot = jnp.mean(dyg * x, axis=-1, keepdims=True)
        dh = r * dyg - x * ((r * r * r) * dot)
        o_ref[...] = dh
        ob_ref[...] = dh.astype(BF16)
        part = jnp.sum(dy * xr, axis=0, keepdims=True)
        lrow = jnp.broadcast_to(lpart, (1, LANES))

        @pl.when(pl.program_id(0) == 0)
        def _():
            dg_ref[...] = part
            l_ref[...] = lrow

        @pl.when(pl.program_id(0) != 0)
        def _():
            dg_ref[...] += part
            l_ref[...] += lrow

    blk = pl.BlockSpec((tm, D), lambda i: (i, 0))
    row = pl.BlockSpec((1, D), lambda i: (0, 0))
    return pl.pallas_call(
        body, name=name, grid=(T // tm,),
        in_specs=[blk, blk, row],
        out_specs=[blk, blk, row, pl.BlockSpec((1, LANES), lambda i: (0, 0))],
        out_shape=[jax.ShapeDtypeStruct((T, D), F32), jax.ShapeDtypeStruct((T, D), BF16),
                   jax.ShapeDtypeStruct((1, D), F32), jax.ShapeDtypeStruct((1, LANES), F32)],
        compiler_params=_cp(("arbitrary",)),
    )(h, tgt, g)


def _mm_nn(a, b, name, out_dtype=F32, res=None, norm_g=None):
    M, K = a.shape
    b3 = b if b.ndim == 3 else b[None]
    nb, _, w = b3.shape
    N = nb * w
    tm = min(TM_MM, M, max(256, MM_BLOCK_BYTES // (4 * N)))
    o_spec = pl.BlockSpec((tm, N), lambda i: (i, 0))
    in_specs = [pl.BlockSpec((tm, K), lambda i: (i, 0)), pl.BlockSpec((nb, K, w), lambda i: (0, 0, 0))]
    args = [a, b3]
    if res is not None:
        in_specs.append(o_spec)
        args.append(res)
    if norm_g is not None:
        in_specs.append(pl.BlockSpec((1, N), lambda i: (0, 0)))
        args.append(norm_g)
    n_out = 2 if norm_g is not None else 1

    def body(*refs):
        a_ref, b_ref = refs[0], refs[1]
        o_ref = refs[-n_out]
        av = a_ref[...]
        for j in range(nb):
            cols = slice(j * w, (j + 1) * w)
            acc = jnp.dot(av, b_ref[j], preferred_element_type=F32)
            if res is not None:
                acc = refs[2][:, cols] + acc
            o_ref[:, cols] = acc.astype(out_dtype)
        if norm_g is not None:
            x = o_ref[...]
            r = lax.rsqrt(jnp.mean(x * x, axis=-1, keepdims=True) + RMS_EPS)
            refs[-1][...] = ((x * r) * refs[3][...]).astype(BF16)

    out_shape = jax.ShapeDtypeStruct((M, N), out_dtype)
    if norm_g is None:
        out_specs, out_shapes = o_spec, out_shape
    else:
        out_specs, out_shapes = [o_spec, o_spec], [out_shape, jax.ShapeDtypeStruct((M, N), BF16)]
    return pl.pallas_call(
        body, name=name, grid=(M // tm,),
        in_specs=in_specs, out_specs=out_specs, out_shape=out_shapes,
        compiler_params=_cp(("parallel",)),
    )(*args)


def _mm_nt(a_list, b, name, out_dtype=F32, add=None, after=None, norm_bwd=None):
    M, kw = a_list[0].shape
    tm = min(TM, M)
    na = len(a_list)
    blocked = b.ndim == 3
    N = b.shape[1] if blocked else b.shape[0]
    b_spec = pl.BlockSpec(b.shape, lambda i: (0,) * b.ndim)
    o_spec = pl.BlockSpec((tm, N), lambda i: (i, 0))
    row_spec = pl.BlockSpec((1, N), lambda i: (0, 0))
    in_specs = [pl.BlockSpec((tm, kw), lambda i: (i, 0)) for _ in a_list] + [b_spec]
    args = list(a_list) + [b]
    if add is not None:
        in_specs.append(o_spec)
        args.append(add)
    n_in = len(args)
    if norm_bwd is not None:
        in_specs += [o_spec, row_spec, o_spec]
        args += list(norm_bwd)
    if after is not None:
        in_specs.append(pl.BlockSpec(memory_space=pl.ANY))
        args.append(after)
    n_args = len(args)

    def body(*refs):
        a_refs = refs[:na]
        b_ref = refs[na]
        acc = refs[na + 1][...] if add is not None else None
        for s, a_ref in enumerate(a_refs):
            if blocked:
                w = b_ref.shape[2]
                per = kw // w
                parts = [_dot_nt(a_ref[:, jj * w:(jj + 1) * w], b_ref[s * per + jj]) for jj in range(per)]
            else:
                parts = [_dot_nt(a_ref[...], b_ref[:, s * kw:(s + 1) * kw])]
            for part in parts:
                acc = part if acc is None else acc + part
        if norm_bwd is None:
            refs[n_args][...] = acc.astype(out_dtype)
            return
        h_ref, g_ref, r_ref = refs[n_in:n_in + 3]
        o_ref, ob_ref, dg_ref = refs[n_args:n_args + 3]
        x = h_ref[...]
        r = lax.rsqrt(jnp.mean(x * x, axis=-1, keepdims=True) + RMS_EPS)
        dyg = acc * g_ref[...]
        dot = jnp.mean(dyg * x, axis=-1, keepdims=True)
        dh = r_ref[...] + (r * dyg - x * ((r * r * r) * dot))
        o_ref[...] = dh
        ob_ref[...] = dh.astype(BF16)
        part_g = jnp.sum(acc * (x * r), axis=0, keepdims=True)

        @pl.when(pl.program_id(0) == 0)
        def _():
            dg_ref[...] = part_g

        @pl.when(pl.program_id(0) != 0)
        def _():
            dg_ref[...] += part_g

    if norm_bwd is None:
        out_specs, out_shapes, sem = o_spec, jax.ShapeDtypeStruct((M, N), out_dtype), ("parallel",)
    else:
        out_specs = [o_spec, o_spec, row_spec]
        out_shapes = [jax.ShapeDtypeStruct((M, N), F32), jax.ShapeDtypeStruct((M, N), BF16),
                      jax.ShapeDtypeStruct((1, N), F32)]
        sem = ("arbitrary",)
    return pl.pallas_call(
        body, name=name, grid=(M // tm,),
        in_specs=in_specs, out_specs=out_specs, out_shape=out_shapes,
        compiler_params=_cp(sem),
    )(*args)


def _mm_out_t(a, bt, name):
    M, K = a.shape
    N = bt.shape[0]
    tm = min(TM_MM, M)

    def body(a_ref, b_ref, o_ref):
        o_ref[...] = _dot_nt(b_ref[...], a_ref[...]).astype(BF16)

    return pl.pallas_call(
        body, name=name, grid=(M // tm,),
        in_specs=[pl.BlockSpec((tm, K), lambda i: (i, 0)), pl.BlockSpec((N, K), lambda i: (0, 0))],
        out_specs=pl.BlockSpec((N, tm), lambda i: (0, i)),
        out_shape=jax.ShapeDtypeStruct((N, M), BF16),
        compiler_params=_cp(("parallel",)),
    )(a, bt)


def _mm_tn(x, y, name, blocked_w=None, out_dtype=F32):
    T, Kx = x.shape
    N = y.shape[1]
    tt = min(TT, T)
    nt = T // tt
    tkx = min(Kx, max(LANES, MM_BLOCK_BYTES // (4 * N)))
    if blocked_w is not None:
        blk_shape, full_shape = (N // blocked_w, tkx, blocked_w), (N // blocked_w, Kx, blocked_w)
        o_spec = pl.BlockSpec(blk_shape, lambda i, t: (0, i, 0))
    else:
        blk_shape, full_shape = (tkx, N), (Kx, N)
        o_spec = pl.BlockSpec(blk_shape, lambda i, t: (i, 0))

    rk = min(tkx, TN_ROWS)

    def body(x_ref, y_ref, o_ref, acc_ref):
        t = pl.program_id(1)

        @pl.when(t == 0)
        def _():
            acc_ref[...] = jnp.zeros_like(acc_ref)

        for r in range(tkx // rk):
            rows = slice(r * rk, (r + 1) * rk)
            part = _dot_tn(x_ref[:, rows], y_ref[...])
            if blocked_w is None:
                acc_ref[rows, :] += part
            else:
                for j in range(N // blocked_w):
                    acc_ref[j, rows, :] += part[:, j * blocked_w:(j + 1) * blocked_w]

        @pl.when(t == nt - 1)
        def _():
            o_ref[...] = acc_ref[...].astype(out_dtype)

    return pl.pallas_call(
        body, name=name, grid=(Kx // tkx, nt),
        in_specs=[pl.BlockSpec((tt, tkx), lambda i, t: (t, i)),
                  pl.BlockSpec((tt, N), lambda i, t: (t, 0))],
        out_specs=o_spec, out_shape=jax.ShapeDtypeStruct(full_shape, out_dtype),
        scratch_shapes=[pltpu.VMEM(blk_shape, F32)],
        compiler_params=_cp(("parallel", "arbitrary")),
    )(x, y)


def _sgu_pieces(z, lng, lnb, wc, bs_ref):
    E = z.shape[1] // 2
    gd = E // GM_GROUPS
    zu, zv = z[:, :E], z[:, E:]
    u = _gelu(zu)
    v = _gelu(zv)
    mu = jnp.mean(v, axis=-1, keepdims=True)
    xc = v - mu
    rs = lax.rsqrt(jnp.mean(xc * xc, axis=-1, keepdims=True) + LN_EPS)
    xhat = xc * rs
    vln = xhat * lng + lnb
    s = []
    for g in range(GM_GROUPS):
        vg = vln[:, g * gd:(g + 1) * gd].astype(BF16)
        s.append(jnp.dot(wc[g], vg, preferred_element_type=F32) + bs_ref[g])
    return zu, zv, u, xhat, rs, vln, s


def _causal_ws(ws_ref):
    t = lax.broadcasted_iota(jnp.int32, (CHUNK, CHUNK), 0)
    s = lax.broadcasted_iota(jnp.int32, (CHUNK, CHUNK), 1)
    tri = t >= s
    return [jnp.where(tri, ws_ref[g], 0.0).astype(BF16) for g in range(GM_GROUPS)], tri


def _gm_in_fused(hn, w_in, lng, lnb, ws, bs, name):
    T, D = hn.shape
    nb, _, w = w_in.shape
    E2 = nb * w
    E = E2 // 2
    gd = E // GM_GROUPS
    tm = min(TM, T)

    def body(a_ref, w_ref, lng_ref, lnb_ref, ws_ref, bs_ref, z_ref, o_ref):
        av = a_ref[...]
        for j in range(nb):
            z_ref[:, j * w:(j + 1) * w] = jnp.dot(av, w_ref[j], preferred_element_type=F32)
        wc, _ = _causal_ws(ws_ref)
        for c in range(tm // CHUNK):
            rows = slice(c * CHUNK, (c + 1) * CHUNK)
            _, _, u, _, _, _, s = _sgu_pieces(z_ref[rows, :], lng_ref[...], lnb_ref[...], wc, bs_ref)
            for g in range(GM_GROUPS):
                cols = slice(g * gd, (g + 1) * gd)
                o_ref[rows, cols] = (u[:, cols] * s[g]).astype(BF16)

    full = lambda shape: pl.BlockSpec(shape, lambda i: (0,) * len(shape))
    return pl.pallas_call(
        body, name=name, grid=(T // tm,),
        in_specs=[pl.BlockSpec((tm, D), lambda i: (i, 0)), full((nb, D, w)), full((1, E)), full((1, E)),
                  full((GM_GROUPS, CHUNK, CHUNK)), full((GM_GROUPS, CHUNK, 1))],
        out_specs=[pl.BlockSpec((tm, E2), lambda i: (i, 0)), pl.BlockSpec((tm, E), lambda i: (i, 0))],
        out_shape=[jax.ShapeDtypeStruct((T, E2), F32), jax.ShapeDtypeStruct((T, E), BF16)],
        compiler_params=_cp(("parallel",)),
    )(hn, w_in, lng, lnb, ws, bs)


def _sgu_bwd(z, dh16, w_out, lng, lnb, ws, bs, name, after=None):
    T, E2 = z.shape
    D = dh16.shape[1]
    E = E2 // 2
    gd = E // GM_GROUPS
    tm = min(2 * CHUNK, T)
    nsteps = T // tm

    def body(z_ref, dh_ref, wo_ref, lng_ref, lnb_ref, ws_ref, bs_ref, *rest):
        dz_ref, dlng_ref, dlnb_ref, dws_ref, dbs_ref, dg_ref = rest[-6:]
        i = pl.program_id(0)

        @pl.when(i == 0)
        def _():
            dlng_ref[...] = jnp.zeros_like(dlng_ref)
            dlnb_ref[...] = jnp.zeros_like(dlnb_ref)
            dws_ref[...] = jnp.zeros_like(dws_ref)
            dbs_ref[...] = jnp.zeros_like(dbs_ref)

        dg_ref[...] = _dot_nt(dh_ref[...], wo_ref[...]).astype(BF16)
        wc, tri = _causal_ws(ws_ref)
        lng_v = lng_ref[...]
        for c in range(tm // CHUNK):
            rows = slice(c * CHUNK, (c + 1) * CHUNK)
            zu, zv, u, xhat, rs, vln, s = _sgu_pieces(z_ref[rows, :], lng_v, lnb_ref[...], wc, bs_ref)
            dgc = dg_ref[rows, :].astype(F32)
            du, dvln = [], []
            for g in range(GM_GROUPS):
                cols = slice(g * gd, (g + 1) * gd)
                dgg = dgc[:, cols]
                du.append(dgg * s[g])
                ds = dgg * u[:, cols]
                dsb = ds.astype(BF16)
                dws_ref[g] += _dot_nt(dsb, vln[:, cols].astype(BF16))
                dbs_ref[g] += jnp.sum(ds, axis=-1, keepdims=True)
                dvln.append(_dot_tn(wc[g], dsb))
            du = jnp.concatenate(du, axis=1)
            dvln = jnp.concatenate(dvln, axis=1)
            dlng_ref[...] += jnp.sum(dvln * xhat, axis=0, keepdims=True)
            dlnb_ref[...] += jnp.sum(dvln, axis=0, keepdims=True)
            dxh = dvln * lng_v
            m1 = jnp.mean(dxh, axis=-1, keepdims=True)
            m2 = jnp.mean(dxh * xhat, axis=-1, keepdims=True)
            dv = rs * (dxh - m1 - xhat * m2)
            dz_ref[rows, :E] = (du * _gelu_grad(zu)).astype(BF16)
            dz_ref[rows, E:] = (dv * _gelu_grad(zv)).astype(BF16)

        @pl.when(i == nsteps - 1)
        def _():
            for g in range(GM_GROUPS):
                dws_ref[g] = jnp.where(tri, dws_ref[g], 0.0)

    full = lambda shape: pl.BlockSpec(shape, lambda i: (0,) * len(shape))
    in_specs = [pl.BlockSpec((tm, E2), lambda i: (i, 0)), pl.BlockSpec((tm, D), lambda i: (i, 0)), full((E, D)),
                full((1, E)), full((1, E)), full((GM_GROUPS, CHUNK, CHUNK)), full((GM_GROUPS, CHUNK, 1))]
    args = [z, dh16, w_out, lng, lnb, ws, bs]
    if after is not None:
        in_specs.append(pl.BlockSpec(memory_space=pl.ANY))
        args.append(after)
    return pl.pallas_call(
        body, name=name, grid=(nsteps,),
        in_specs=in_specs,
        out_specs=[pl.BlockSpec((tm, E2), lambda i: (i, 0)), full((1, E)), full((1, E)),
                   full((GM_GROUPS, CHUNK, CHUNK)), full((GM_GROUPS, CHUNK, 1))],
        out_shape=[jax.ShapeDtypeStruct((T, E2), BF16), jax.ShapeDtypeStruct((1, E), F32),
                   jax.ShapeDtypeStruct((1, E), F32), jax.ShapeDtypeStruct((GM_GROUPS, CHUNK, CHUNK), F32),
                   jax.ShapeDtypeStruct((GM_GROUPS, CHUNK, 1), F32)],
        scratch_shapes=[pltpu.VMEM((tm, E), BF16)],
        compiler_params=_cp(("arbitrary",)),
    )(*args)


HALO = 16


def _conv_taps(a_ext, w_ref, b_ref):
    n = a_ext.shape[0]
    am1 = pltpu.roll(a_ext, 1, 0)
    am2 = pltpu.roll(a_ext, 2, 0)
    del n
    return ((b_ref[...] + am2 * w_ref[0:1, :]) + am1 * w_ref[1:2, :]) + a_ext * w_ref[2:3, :], am1, am2


def _ffn_up_fused(hn, wgu, cw, cb, name):
    T, D = hn.shape
    nb2, _, w = wgu.shape
    nb = nb2 // 2
    F = nb * w
    tm = min(TM, T)

    def body(a_ref, w_ref, cw_ref, cb_ref, au_ref, act_ref, halo_ref):
        @pl.when(pl.program_id(0) == 0)
        def _():
            halo_ref[...] = jnp.zeros_like(halo_ref)

        av = a_ref[...]
        for j in range(nb):
            cols = slice(j * w, (j + 1) * w)
            g = jnp.dot(av, w_ref[j], preferred_element_type=F32)
            u = jnp.dot(av, w_ref[nb + j], preferred_element_type=F32)
            au_ref[:, cols] = g
            au_ref[:, F + j * w:F + (j + 1) * w] = u
            ext = jnp.concatenate([halo_ref[:, cols], g], axis=0)
            am1 = pltpu.roll(ext, 1, 0)
            am2 = pltpu.roll(ext, 2, 0)
            conv = ((cb_ref[:, cols] + am2 * cw_ref[0:1, cols]) + am1 * cw_ref[1:2, cols]) + ext * cw_ref[2:3, cols]
            conv = conv[HALO:, :]
            act_ref[:, cols] = ((conv * _sigmoid(conv)) * u).astype(BF16)
            halo_ref[:, cols] = g[tm - HALO:, :]

    return pl.pallas_call(
        body, name=name, grid=(T // tm,),
        in_specs=[pl.BlockSpec((tm, D), lambda i: (i, 0)), pl.BlockSpec((nb2, D, w), lambda i: (0, 0, 0)),
                  pl.BlockSpec((3, F), lambda i: (0, 0)), pl.BlockSpec((1, F), lambda i: (0, 0))],
        out_specs=[pl.BlockSpec((tm, 2 * F), lambda i: (i, 0)), pl.BlockSpec((tm, F), lambda i: (i, 0))],
        out_shape=[jax.ShapeDtypeStruct((T, 2 * F), F32), jax.ShapeDtypeStruct((T, F), BF16)],
        scratch_shapes=[pltpu.VMEM((HALO, F), F32)],
        compiler_params=_cp(("arbitrary",)),
    )(hn, wgu, cw, cb)


def _ffn_mid_bwd(au, dh16, wd, cw, cb, name, after=None):
    T, F = au.shape[0], au.shape[1] // 2
    D = dh16.shape[1]
    tm, tf = min(TM, T), min(TF, F)
    hb = tm // HALO
    nt = T // tm
    nf = F // tf
    last_h = T // HALO - 1

    def body(a_ref, ap_ref, an_ref, u_ref, un_ref, dh_ref, dhn_ref, wd_ref, w_ref, b_ref, *rest):
        da_ref, du_ref, dcw_ref, dcb_ref = rest[-4:]
        i = pl.program_id(1)
        prev = jnp.where(i == 0, 0.0, ap_ref[...])
        a_main = a_ref[...]
        a_ext = jnp.concatenate([prev, a_main, an_ref[...]], axis=0)
        conv, am1, am2 = _conv_taps(a_ext, w_ref, b_ref)
        conv = conv[HALO:, :]
        sig = _sigmoid(conv)
        u_ext = jnp.concatenate([u_ref[...], un_ref[...]], axis=0)
        wd_f = wd_ref[...]
        d_ext = jnp.concatenate([_dot_nt(dh_ref[...], wd_f), _dot_nt(dhn_ref[...], wd_f)], axis=0)
        d_ext = d_ext.astype(BF16).astype(F32)
        n = tm + HALO
        row = lax.broadcasted_iota(jnp.int32, (n, 1), 0)
        live = jnp.logical_or(row < tm, i < nt - 1)
        dconv = jnp.where(live, d_ext * u_ext * (sig * (1.0 + conv * (1.0 - sig))), 0.0)
        du_ref[...] = (d_ext[:tm, :] * (conv[:tm, :] * sig[:tm, :])).astype(BF16)
        dp1 = pltpu.roll(dconv, n - 1, 0)[:tm, :]
        dp2 = pltpu.roll(dconv, n - 2, 0)[:tm, :]
        dc = dconv[:tm, :]
        da_ref[...] = ((dc * w_ref[2:3, :] + dp1 * w_ref[1:2, :]) + dp2 * w_ref[0:1, :]).astype(BF16)
        g2 = jnp.sum(dc * a_main, axis=0, keepdims=True)
        g1 = jnp.sum(dc * am1[HALO:HALO + tm, :], axis=0, keepdims=True)
        g0 = jnp.sum(dc * am2[HALO:HALO + tm, :], axis=0, keepdims=True)
        gb = jnp.sum(dc, axis=0, keepdims=True)

        @pl.when(i == 0)
        def _():
            dcw_ref[...] = jnp.zeros_like(dcw_ref)
            dcb_ref[...] = jnp.zeros_like(dcb_ref)

        dcw_ref[0:1, :] += g0
        dcw_ref[1:2, :] += g1
        dcw_ref[2:3, :] += g2
        dcb_ref[...] += gb

    main = pl.BlockSpec((tm, tf), lambda f, i: (i, f))
    prev = pl.BlockSpec((HALO, tf), lambda f, i: (jnp.maximum(i * hb - 1, 0), f))
    nxt = pl.BlockSpec((HALO, tf), lambda f, i: (jnp.minimum((i + 1) * hb, last_h), f))
    main_u = pl.BlockSpec((tm, tf), lambda f, i: (i, nf + f))
    nxt_u = pl.BlockSpec((HALO, tf), lambda f, i: (jnp.minimum((i + 1) * hb, last_h), nf + f))
    in_specs = [main, prev, nxt, main_u, nxt_u,
                pl.BlockSpec((tm, D), lambda f, i: (i, 0)),
                pl.BlockSpec((HALO, D), lambda f, i: (jnp.minimum((i + 1) * hb, last_h), 0)),
                pl.BlockSpec((tf, D), lambda f, i: (f, 0)),
                pl.BlockSpec((3, tf), lambda f, i: (0, f)), pl.BlockSpec((1, tf), lambda f, i: (0, f))]
    args = [au, au, au, au, au, dh16, dh16, wd, cw, cb]
    if after is not None:
        in_specs.append(pl.BlockSpec(memory_space=pl.ANY))
        args.append(after)
    return pl.pallas_call(
        body, name=name, grid=(nf, nt),
        in_specs=in_specs,
        out_specs=[main, main, pl.BlockSpec((3, tf), lambda f, i: (0, f)), pl.BlockSpec((1, tf), lambda f, i: (0, f))],
        out_shape=[jax.ShapeDtypeStruct((T, F), BF16), jax.ShapeDtypeStruct((T, F), BF16),
                   jax.ShapeDtypeStruct((3, F), F32), jax.ShapeDtypeStruct((1, F), F32)],
        compiler_params=_cp(("parallel", "arbitrary")),
    )(*args)


def _split3(x):
    hi = x.astype(BF16)
    r1 = x - hi.astype(F32)
    mid = r1.astype(BF16)
    lo = (r1 - mid.astype(F32)).astype(BF16)
    return hi, mid, lo


def _tri_ones(n, upper):
    r = lax.broadcasted_iota(jnp.int32, (n, n), 0)
    c = lax.broadcasted_iota(jnp.int32, (n, n), 1)
    return jnp.where((r <= c) if upper else (r >= c), 1.0, 0.0).astype(BF16)


def _gate_scan(f, bf, name):
    T = f.shape[0]
    tm = min(256, T)

    def body(f_ref, b_ref, cp_ref, sn_ref, carry_ref):
        i = pl.program_id(0)

        @pl.when(i == 0)
        def _():
            carry_ref[...] = jnp.zeros_like(carry_ref)

        x = f_ref[...] + b_ref[...]
        e = jnp.exp(-jnp.abs(x))
        logf = jnp.minimum(x, 0.0) - jnp.log(1.0 + e)
        sn_ref[...] = jnp.where(x >= 0.0, e / (1.0 + e), 1.0 / (1.0 + e))
        tri = _tri_ones(tm, upper=False)
        c = carry_ref[...]
        for piece in _split3(logf):
            c = c + jnp.dot(tri, piece, preferred_element_type=F32)
        carry_ref[...] += jnp.sum(logf, axis=0, keepdims=True)
        hi, mid, lo = _split3(c * LOG2E)
        cp_ref[:, 0:LANES] = hi
        cp_ref[:, LANES:2 * LANES] = mid
        cp_ref[:, 2 * LANES:3 * LANES] = lo

    return pl.pallas_call(
        body, name=name, grid=(T // tm,),
        in_specs=[pl.BlockSpec((tm, LANES), lambda i: (i, 0)), pl.BlockSpec((1, LANES), lambda i: (0, 0))],
        out_specs=[pl.BlockSpec((tm, 3 * LANES), lambda i: (i, 0)), pl.BlockSpec((tm, LANES), lambda i: (i, 0))],
        out_shape=[jax.ShapeDtypeStruct((T, 3 * LANES), BF16), jax.ShapeDtypeStruct((T, LANES), F32)],
        scratch_shapes=[pltpu.VMEM((1, LANES), F32)],
        compiler_params=_cp(("arbitrary",)),
    )(f, bf)


def _gate_scan_bwd(dcq, dck, sneg, name):
    T = dcq.shape[0]
    tm = min(256, T)
    n = T // tm

    def body(dcq_ref, dck_ref, sn_ref, df_ref, db_ref, carry_ref):
        i = pl.program_id(0)

        @pl.when(i == 0)
        def _():
            carry_ref[...] = jnp.zeros_like(carry_ref)
            db_ref[...] = jnp.zeros_like(db_ref)

        tri = _tri_ones(tm, upper=True)
        dcb = dcq_ref[...] - dck_ref[...]
        acc = carry_ref[...]
        for piece in _split3(dcb):
            acc = acc + jnp.dot(tri, piece, preferred_element_type=F32)
        carry_ref[...] += jnp.sum(dcb, axis=0, keepdims=True)
        df = acc * sn_ref[...]
        df_ref[...] = df.astype(BF16)
        db_ref[...] += jnp.sum(df, axis=0, keepdims=True)

    rev = pl.BlockSpec((tm, LANES), lambda i: (n - 1 - i, 0))
    return pl.pallas_call(
        body, name=name, grid=(n,),
        in_specs=[rev, rev, rev],
        out_specs=[rev, pl.BlockSpec((1, LANES), lambda i: (0, 0))],
        out_shape=[jax.ShapeDtypeStruct((T, LANES), BF16), jax.ShapeDtypeStruct((1, LANES), F32)],
        scratch_shapes=[pltpu.VMEM((1, LANES), F32)],
        compiler_params=_cp(("arbitrary",)),
    )(dcq, dck, sneg)


def _qk_proj(hn, w_pads, cp, sels, consts, scales, name):
    T, D = hn.shape
    H = w_pads[0].shape[1] // LANES
    tm = min(TM, T)

    def body(a_ref, cp_ref, wq_ref, wk_ref, sq_ref, sk_ref, cq_ref, ck_ref, qo_ref, ko_ref):
        a = a_ref[...]
        cpv = cp_ref[...]
        for w_ref, sel_ref, c_ref, o_ref, scale in ((wq_ref, sq_ref, cq_ref, qo_ref, scales[0]),
                                                    (wk_ref, sk_ref, ck_ref, ko_ref, scales[1])):
            for p in range(H // 2):
                acc = jnp.dot(a, w_ref[:, p * 2 * LANES:(p + 1) * 2 * LANES], preferred_element_type=F32)
                if scale != 1.0:
                    acc = acc * scale
                acc = acc + jnp.dot(cpv, sel_ref[p], preferred_element_type=F32) + c_ref[p]
                o_ref[2 * p] = acc[:, :LANES].astype(BF16)
                o_ref[2 * p + 1] = acc[:, LANES:].astype(BF16)

    whole = lambda t: pl.BlockSpec(t.shape, lambda i: (0,) * t.ndim)
    out = jax.ShapeDtypeStruct((H, T, LANES), BF16)
    o_spec = pl.BlockSpec((H, tm, LANES), lambda i: (0, i, 0))
    return pl.pallas_call(
        body, name=name, grid=(T // tm,),
        in_specs=[pl.BlockSpec((tm, D), lambda i: (i, 0)), pl.BlockSpec((tm, 3 * LANES), lambda i: (i, 0)),
                  whole(w_pads[0]), whole(w_pads[1]), whole(sels[0]), whole(sels[1]), whole(consts[0]), whole(consts[1])],
        out_specs=[o_spec, o_spec], out_shape=[out, out],
        compiler_params=_cp(("parallel",)),
    )(hn, cp, w_pads[0], w_pads[1], sels[0], sels[1], consts[0], consts[1])


def _lane_lo():
    return lax.broadcasted_iota(jnp.int32, (1, LANES), 1) < HEAD_DIM


def _attn_fwd_t(qp, kp, vt, name):
    H, T, _ = qp.shape
    tq = min(TQ, T)
    qw = min(ATTN_QW, tq)
    hd = HEAD_DIM
    ext = hd + 16

    def body(q_ref, k_ref, vt_ref, o_ref, o32_ref, lse_ref, m_sc, acc_sc):
        i = pl.program_id(1)
        m_sc[...] = jnp.full(m_sc.shape, NEG, F32)
        acc_sc[...] = jnp.zeros_like(acc_sc)
        q_t = [jnp.transpose(q_ref[h].astype(F32)).astype(BF16) for h in range(2)]
        ones_rows = jnp.where(lax.broadcasted_iota(jnp.int32, (16, tq), 0) == 0, 1.0, 0.0).astype(BF16)

        def steps(blocks):
            offs = [pl.multiple_of(j * tq, tq) for j, _ in blocks]
            s_all = [[[jnp.dot(k_ref[h, pl.ds(off, tq), :], q_t[h][:, c * qw:(c + 1) * qw], preferred_element_type=F32)
                       for c in range(tq // qw)] for h in range(2)] for off in offs]
            for (j, masked), off, s_blk in zip(blocks, offs, s_all):
                for h in range(2):
                    v_aug = jnp.concatenate([vt_ref[h * hd:(h + 1) * hd, pl.ds(off, tq)], ones_rows], axis=0)
                    for c in range(tq // qw):
                        cols = slice(c * qw, (c + 1) * qw)
                        s = s_blk[h][c]
                        if masked:
                            kr = lax.broadcasted_iota(jnp.int32, (tq, qw), 0)
                            qc = lax.broadcasted_iota(jnp.int32, (tq, qw), 1) + c * qw
                            s = jnp.where(qc >= kr, s, NEG)
                        m_prev = m_sc[h, :, cols]
                        m_new = jnp.maximum(m_prev, jnp.max(s, axis=0, keepdims=True))
                        alpha = jnp.exp2(m_prev - m_new)
                        p16 = jnp.exp2(s - m_new).astype(BF16)
                        pv = jnp.dot(v_aug, p16, preferred_element_type=F32)
                        acc_sc[h, :, cols] = alpha * acc_sc[h, :, cols] + pv
                        m_sc[h, :, cols] = m_new

        def group_body(t, carry):
            steps([(KV_UNROLL * t + u, False) for u in range(KV_UNROLL)])
            return carry

        lax.fori_loop(0, i // KV_UNROLL, group_body, 0)
        for rem in range(KV_UNROLL):

            @pl.when(i % KV_UNROLL == rem)
            def _(rem=rem):
                steps([(i - rem + u, u == rem) for u in range(rem + 1)])

        o_t, lse_t = [], []
        for h in range(2):
            acc = acc_sc[h]
            l = acc[hd:hd + 1, :]
            o_t.append(acc[:hd, :] / l)
            lse_t.append(jnp.broadcast_to(m_sc[h] + jnp.log(l) * LOG2E, (hd, tq)))
        o = jnp.transpose(jnp.concatenate(o_t, axis=0))
        o_ref[...] = o.astype(BF16)
        o32_ref[...] = o
        lse_ref[...] = jnp.transpose(jnp.concatenate(lse_t, axis=0))

    oblk = pl.BlockSpec((tq, LANES), lambda p, i: (i, p))
    return pl.pallas_call(
        body, name=name, grid=(H // 2, T // tq),
        in_specs=[pl.BlockSpec((2, tq, LANES), lambda p, i: (p, i, 0)),
                  pl.BlockSpec((2, T, LANES), lambda p, i: (p, 0, 0)),
                  pl.BlockSpec((2 * hd, T), lambda p, i: (p, 0))],
        out_specs=[oblk, oblk, pl.BlockSpec((None, tq, LANES), lambda p, i: (p, i, 0))],
        out_shape=[jax.ShapeDtypeStruct((T, H * HEAD_DIM), BF16), jax.ShapeDtypeStruct((T, H * HEAD_DIM), F32),
                   jax.ShapeDtypeStruct((H // 2, T, LANES), F32)],
        scratch_shapes=[pltpu.VMEM((2, 1, tq), F32), pltpu.VMEM((2, ext, tq), F32)],
        compiler_params=_cp(("parallel", "arbitrary")),
    )(qp, kp, vt)


def _attn_bwd(qp, kp, vt, o, do, lse, scale, name):
    H, T, _ = qp.shape
    tq = min(TQ, T)
    nq = T // tq
    nrep = tq // LANES

    def body(q_ref, k_ref, vt_ref, o_ref, do_ref, lse_ref, dq_ref, dk_ref, dv_ref, dqe_ref, dke_ref, dk_sc, dv_sc, dq_sc):
        i = pl.program_id(1)

        @pl.when(i == 0)
        def _():
            dk_sc[...] = jnp.zeros_like(dk_sc)
            dv_sc[...] = jnp.zeros_like(dv_sc)

        dq_sc[...] = jnp.zeros_like(dq_sc)

        lo = _lane_lo()
        dob = do_ref[...]
        dof = dob.astype(F32)
        prod = dof * o_ref[...].astype(F32)
        lse2 = lse_ref[...]
        lse2_sw = pltpu.roll(lse2, HEAD_DIM, 1)
        zero = jnp.zeros_like(dob)
        do_h = [jnp.where(lo, dob, zero), jnp.where(lo, zero, dob)]
        rep = lambda col: jnp.broadcast_to(col, (tq, LANES))
        delta = [rep(jnp.sum(jnp.where(lo, prod, 0.0), axis=-1, keepdims=True)),
                 rep(jnp.sum(jnp.where(lo, 0.0, prod), axis=-1, keepdims=True))]
        lse_h = [jnp.where(lo, lse2, lse2_sw), jnp.where(lo, lse2_sw, lse2)]
        qs = [q_ref[0], q_ref[1]]
        tr16 = lambda a: jnp.transpose(a.astype(F32)).astype(BF16)
        q_t = [tr16(qs[0]), tr16(qs[1])]
        do_t = [tr16(do_h[0]), tr16(do_h[1])]

        def steps(blocks):
            offs = [pl.multiple_of(j * tq, tq) for j, _ in blocks]
            vblks = [vt_ref[:, pl.ds(off, tq)] for off in offs]
            kblks = [[k_ref[h, pl.ds(off, tq), :] for h in range(2)] for off in offs]
            s_all = [[_dot_nt(qs[h], kb[h]) for h in range(2)] for kb in kblks]
            dp_all = [[jnp.dot(do_h[h], vb, preferred_element_type=F32) for h in range(2)] for vb in vblks]
            for b, ((j, masked), off) in enumerate(zip(blocks, offs)):
                dv_add = None
                for h in range(2):
                    kblk, s, dp = kblks[b][h], s_all[b][h], dp_all[b][h]
                    p16, ds16 = [], []
                    for c in range(nrep):
                        cols = slice(c * LANES, (c + 1) * LANES)
                        p = jnp.exp2(s[:, cols] - lse_h[h])
                        if masked:
                            r = lax.broadcasted_iota(jnp.int32, (tq, LANES), 0)
                            cc = lax.broadcasted_iota(jnp.int32, (tq, LANES), 1)
                            p = jnp.where(r >= cc + c * LANES, p, 0.0)
                        p16.append(p.astype(BF16))
                        ds16.append((p * (dp[:, cols] - delta[h])).astype(BF16))
                    p16 = jnp.concatenate(p16, axis=1)
                    dsb = jnp.concatenate(ds16, axis=1)
                    dq_sc[h] += jnp.dot(dsb, kblk, preferred_element_type=F32)
                    dk_sc[h, :, pl.ds(off, tq)] += jnp.dot(q_t[h], dsb, preferred_element_type=F32)
                    pv = jnp.dot(do_t[h], p16, preferred_element_type=F32)
                    dv_add = pv if dv_add is None else dv_add + pv
                dv_sc[:, pl.ds(off, tq)] += dv_add

        def group_body(t, carry):
            steps([(KV_UNROLL_BWD * t + u, False) for u in range(KV_UNROLL_BWD)])
            return carry

        lax.fori_loop(0, i // KV_UNROLL_BWD, group_body, 0)
        for rem in range(KV_UNROLL_BWD):

            @pl.when(i % KV_UNROLL_BWD == rem)
            def _(rem=rem):
                steps([(i - rem + u, u == rem) for u in range(rem + 1)])

        dq0, dq1 = dq_sc[0], dq_sc[1]
        dq_ref[...] = (jnp.where(lo, dq0, pltpu.roll(dq1, HEAD_DIM, 1)) * scale).astype(BF16)
        row8 = lax.broadcasted_iota(jnp.int32, (8, 1), 0)
        pick = lambda blk, r: jnp.sum(jnp.where(row8 == r, blk, 0.0), axis=0, keepdims=True)
        two_rows = lambda a, b: jnp.where(row8 == 0, a, jnp.where(row8 == 1, b, 0.0))
        gate_rows = slice(HEAD_DIM, HEAD_DIM + 8)
        dqe_ref[...] = two_rows(pick(jnp.transpose(dq0)[gate_rows, :], 0), pick(jnp.transpose(dq1)[gate_rows, :], 0))

        @pl.when(i == nq - 1)
        def _():
            dke_ref[...] = two_rows(pick(dk_sc[0, gate_rows, :], 3), pick(dk_sc[1, gate_rows, :], 3))
            for cb in range(nq):
                tok = slice(cb * tq, (cb + 1) * tq)
                dk0 = jnp.transpose(dk_sc[0, :, tok])
                dk1 = jnp.transpose(dk_sc[1, :, tok])
                dk_ref[tok, :] = (jnp.where(lo, dk0, pltpu.roll(dk1, HEAD_DIM, 1)) * LN2).astype(BF16)
                dv_ref[tok, :] = jnp.transpose(dv_sc[:, tok]).astype(BF16)

    qblk = pl.BlockSpec((tq, LANES), lambda p, i: (i, p))
    pair = pl.BlockSpec((T, LANES), lambda p, i: (0, p))
    tok16 = jax.ShapeDtypeStruct((T, H * HEAD_DIM), BF16)
    gate32 = jax.ShapeDtypeStruct((H // 2, 8, T), F32)
    return pl.pallas_call(
        body, name=name, grid=(H // 2, nq),
        in_specs=[pl.BlockSpec((2, tq, LANES), lambda p, i: (p, i, 0)),
                  pl.BlockSpec((2, T, LANES), lambda p, i: (p, 0, 0)),
                  pl.BlockSpec((LANES, T), lambda p, i: (p, 0)), qblk, qblk,
                  pl.BlockSpec((None, tq, LANES), lambda p, i: (p, i, 0))],
        out_specs=[qblk, pair, pair, pl.BlockSpec((None, 8, tq), lambda p, i: (p, 0, i)),
                   pl.BlockSpec((None, 8, T), lambda p, i: (p, 0, 0))],
        out_shape=[tok16, tok16, tok16, gate32, gate32],
        scratch_shapes=[pltpu.VMEM((2, LANES, T), F32), pltpu.VMEM((LANES, T), F32),
                        pltpu.VMEM((2, tq, LANES), F32)],
        compiler_params=_cp(("parallel", "arbitrary")),
    )(qp, kp, vt, o, do, lse)


def _mesh_pos():
    return lax.axis_index("x"), lax.axis_index("y"), lax.axis_index("c")


def _all_gather(arrs, name, groups=None):
    n = len(arrs)
    if groups is None:
        groups = [(a, 0) for a in range(n)]
    ng = 1 + max(g for g, _ in groups)
    per_group = [sum(1 for g, _ in groups if g == gi) for gi in range(ng)]
    first_of = [next(a for a in range(n) if groups[a][0] == gi) for gi in range(ng)]

    def body(*refs):
        ins, outs = refs[:n], refs[n:n + ng]
        send_sems, recv_sems, local_sems = refs[n + ng:]
        x, y, c = _mesh_pos()
        me, sib = (x, y, c), (x, y, 1 - c)
        chips = [(1 - x, y), (x, 1 - y), (1 - x, 1 - y)]

        def dst_of(a, px, py, pc):
            g, k = groups[a]
            return outs[g].at[N_DEV * k + 4 * px + 2 * py + pc]

        def copy(a, k, block, to, src=None):
            dst = dst_of(a, *block)
            return pltpu.make_async_remote_copy(
                src_ref=dst if src is None else src, dst_ref=dst,
                send_sem=send_sems.at[a, k], recv_sem=recv_sems.at[a, k], device_id=to, device_id_type=MESH)

        mine = [pltpu.make_async_copy(ins[a], dst_of(a, *me), local_sems.at[a]) for a in range(n)]
        for cp in mine:
            cp.start()
        first = []
        for a in range(n):
            first.append(copy(a, 0, me, sib, src=ins[a]))
            first += [copy(a, 1 + j, me, (*chip, c), src=ins[a]) for j, chip in enumerate(chips)]
        for cp in first:
            cp.start()
        passed = []
        for j, chip in enumerate(chips):
            for a in range(n):
                copy(a, 1 + j, (*chip, c), me).wait_recv()
                fwd = copy(a, 4 + j, (*chip, c), sib)
                fwd.start()
                passed.append(fwd)
        for a in range(n):
            copy(a, 0, sib, me).wait_recv()
            for j, chip in enumerate(chips):
                copy(a, 4 + j, (*chip, 1 - c), me).wait_recv()
        for cp in first + passed:
            cp.wait_send()
        for cp in mine:
            cp.wait()

    any_spec = pl.BlockSpec(memory_space=pl.ANY)
    return pl.pallas_call(
        body, name=name,
        in_specs=[any_spec] * n, out_specs=[any_spec] * ng,
        out_shape=[jax.ShapeDtypeStruct((N_DEV * per_group[gi],) + arrs[first_of[gi]].shape, arrs[first_of[gi]].dtype)
                   for gi in range(ng)],
        scratch_shapes=[pltpu.SemaphoreType.DMA((n, 7)), pltpu.SemaphoreType.DMA((n, 7)),
                        pltpu.SemaphoreType.DMA((n,))],
    )(*arrs)


HBM_SPEC = pl.BlockSpec(memory_space=pltpu.HBM)
SEM_SPEC = pl.BlockSpec(memory_space=pltpu.SEMAPHORE)
ANY_SPEC = pl.BlockSpec(memory_space=pl.ANY)
DATAFLOW_EFFECT = pltpu.SideEffectType.DATAFLOW_SIDE_EFFECTING


def _peers():
    x, y, c = _mesh_pos()
    flip = lambda v, b: 1 - v if b else v
    return [(flip(x, (k >> 2) & 1), flip(y, (k >> 1) & 1), flip(c, k & 1)) for k in range(1, N_DEV)]


def _slot(p):
    return 4 * p[0] + 2 * p[1] + p[2]


def _direct_copy(src_refs, land_refs, sems, a, k, p, land_of, dst_slot, src_slot):
    s = src_slot(a, p)
    return pltpu.make_async_remote_copy(
        src_ref=src_refs[a] if s is None else src_refs[a].at[s], dst_ref=land_refs[land_of[a]].at[dst_slot(a, k)],
        send_sem=sems[0].at[a * (N_DEV - 1) + k], recv_sem=sems[1].at[a * (N_DEV - 1) + k], device_id=p,
        device_id_type=MESH)


def _direct_start(srcs, lands, land_of, dst_slot, src_slot, after, name, collective_id):
    n, nl = len(srcs), len(lands)

    def body(*refs):
        src_refs, land_refs = refs[:n], refs[n:n + nl]
        sems = (refs[n + nl + 1], refs[n + nl + 2])
        token = refs[-1]
        peers = _peers()
        barrier = pltpu.get_barrier_semaphore()
        for p in peers:
            pl.semaphore_signal(barrier, inc=1, device_id=p, device_id_type=MESH)
        pl.semaphore_wait(barrier, N_DEV - 1)
        for a in range(n):
            for k, p in enumerate(peers):
                _direct_copy(src_refs, land_refs, sems, a, k, p, land_of, dst_slot, src_slot).start()
        token[...] = jnp.zeros_like(token)

    hbm = lambda t: pltpu.HBM(t.shape, t.dtype)
    sem_t = pltpu.SemaphoreType.DMA((n * (N_DEV - 1),))
    outs = pl.pallas_call(
        body, name=name,
        out_shape=(sem_t, sem_t, *[hbm(t) for t in srcs], *[hbm(t) for t in lands], jax.ShapeDtypeStruct((8, LANES), F32)),
        in_specs=[HBM_SPEC] * (n + nl) + [ANY_SPEC],
        out_specs=(SEM_SPEC, SEM_SPEC, *([HBM_SPEC] * (n + nl)), pl.BlockSpec(memory_space=pltpu.VMEM)),
        input_output_aliases={i: 2 + i for i in range(n + nl)},
        compiler_params=pltpu.CompilerParams(has_side_effects=DATAFLOW_EFFECT, collective_id=collective_id),
    )(*[pltpu.with_memory_space_constraint(t, pltpu.HBM) for t in srcs],
      *[pltpu.with_memory_space_constraint(t, pltpu.HBM) for t in lands], after)
    return outs[0], outs[1], list(outs[2:2 + n]), list(outs[2 + n:2 + n + nl]), outs[-1]


def _direct_wait(send_sems, recv_sems, srcs, lands, land_of, idxs, dst_slot, src_slot, after, name):
    land_ids = []
    for a in idxs:
        if land_of[a] not in land_ids:
            land_ids.append(land_of[a])
    m, ml = len(idxs), len(land_ids)
    sub_land_of = {j: land_ids.index(land_of[a]) for j, a in enumerate(idxs)}

    def body(*refs):
        src_refs, land_refs = refs[:m], refs[m:m + ml]
        ssem, rsem = refs[m + ml], refs[m + ml + 1]
        for j, a in enumerate(idxs):
            for k, p in enumerate(_peers()):
                s = src_slot(a, p)
                cp = pltpu.make_async_remote_copy(
                    src_ref=src_refs[j] if s is None else src_refs[j].at[s],
                    dst_ref=land_refs[sub_land_of[j]].at[dst_slot(a, k)],
                    send_sem=ssem.at[a * (N_DEV - 1) + k], recv_sem=rsem.at[a * (N_DEV - 1) + k], device_id=p,
                    device_id_type=MESH)
                cp.wait_send()
                cp.wait_recv()

    hbm = lambda t: pltpu.HBM(t.shape, t.dtype)
    sub_s, sub_l = [srcs[a] for a in idxs], [lands[g] for g in land_ids]
    outs = pl.pallas_call(
        body, name=name,
        out_shape=(*[hbm(t) for t in sub_s], *[hbm(t) for t in sub_l]),
        in_specs=[HBM_SPEC] * (m + ml) + [SEM_SPEC, SEM_SPEC, ANY_SPEC],
        out_specs=tuple([HBM_SPEC] * (m + ml)),
        input_output_aliases={i: i for i in range(m + ml)},
        compiler_params=pltpu.CompilerParams(has_side_effects=DATAFLOW_EFFECT),
    )(*sub_s, *sub_l, send_sems, recv_sems, after)
    return list(outs[:m]), list(outs[m:])


def _row_block(R, C):
    best = None
    for d in range(16, R + 1, 16):
        if R % d == 0 and d * C <= 256 * 1024:
            best = d
    return best if best is not None else R


def _adamw_math(w, g, m, v):
    m = ADAM_B1 * m + (1.0 - ADAM_B1) * g
    v = ADAM_B2 * v + (1.0 - ADAM_B2) * (g * g)
    m_hat = m / (1.0 - ADAM_B1 ** ADAM_STEP)
    v_hat = v / (1.0 - ADAM_B2 ** ADAM_STEP)
    delta = -ADAM_LR * (m_hat / (jnp.sqrt(v_hat) + ADAM_EPS) + ADAM_WD * w)
    return delta, m, v


def _sum_adamw(parts, w, m, v, name, sel=None):
    R, C = w.shape
    nseg = max(len(arr) if isinstance(arr, list) else 1 for arr, _ in parts)
    tr = _row_block(R // nseg, C)
    bps = R // nseg // tr
    specs, args = [], []
    for arr, idxs in parts:
        pieces = arr if isinstance(arr, list) else [arr] * nseg
        for idx in idxs:
            for sg in range(nseg if isinstance(arr, list) else 1):
                row = (lambda i, sg=sg: jnp.clip(i - sg * bps, 0, bps - 1)) if isinstance(arr, list) else (lambda i: i)
                if idx < 0:
                    specs.append(pl.BlockSpec((None, tr, C), lambda i, s, row=row: (s[0], row(i), 0)))
                else:
                    specs.append(pl.BlockSpec((None, tr, C), lambda i, s, idx=idx, row=row: (idx, row(i), 0)))
                args.append(pieces[sg])
    seg_counts = [(nseg if isinstance(arr, list) else 1) for arr, idxs in parts for _ in idxs]
    npart = len(args)
    blk = pl.BlockSpec((tr, C), lambda i, s: (i, 0))

    def body(s_ref, *refs):
        del s_ref
        seg = pl.program_id(0) // bps
        g, at = None, 0
        for cnt in seg_counts:
            term = refs[at][...].astype(F32)
            for sg in range(1, cnt):
                term = jnp.where(seg == sg, refs[at + sg][...].astype(F32), term)
            g = term if g is None else g + term
            at += cnt
        w_ref, m_ref, v_ref, g_out, d_out, m_out, v_out = refs[npart:]
        delta, mm, vv = _adamw_math(w_ref[...], g, m_ref[...], v_ref[...])
        g_out[...] = g
        d_out[...] = delta
        m_out[...] = mm
        v_out[...] = vv

    grid_spec = pltpu.PrefetchScalarGridSpec(
        num_scalar_prefetch=1, grid=(R // tr,),
        in_specs=specs + [blk, blk, blk], out_specs=[blk] * 4)
    if sel is None:
        sel = jnp.zeros((1,), jnp.int32)
    return pl.pallas_call(
        body, name=name, grid_spec=grid_spec,
        out_shape=[jax.ShapeDtypeStruct((R, C), F32)] * 4,
        compiler_params=_cp(("parallel",)),
    )(sel, *args, w, m, v)


def _rows(a, c):
    return a.reshape(-1, c)


def _pad_rows(a, r):
    return jnp.pad(a, ((0, r - a.shape[0]), (0, 0))) if a.shape[0] != r else a


def _gate_tables():
    hp = N_HEADS // 2
    sel_q = np.zeros((hp, 3 * LANES, 2 * LANES), np.float32)
    sel_k = np.zeros((hp, 3 * LANES, 2 * LANES), np.float32)
    const_q = np.zeros((hp, 1, 2 * LANES), np.float32)
    const_k = np.zeros((hp, 1, 2 * LANES), np.float32)
    for p in range(hp):
        for hh in range(2):
            h = 2 * p + hh
            base = hh * LANES + HEAD_DIM
            for piece in range(3):
                sel_q[p, piece * LANES + h, base + piece] = 1.0
                sel_k[p, piece * LANES + h, base + 3 + piece] = -1.0
            const_k[p, 0, base:base + 3] = 1.0
            const_q[p, 0, base + 3:base + 6] = 1.0
    as_bf = lambda t: jnp.asarray(t, BF16)
    return as_bf(sel_q), as_bf(sel_k), jnp.asarray(const_q), jnp.asarray(const_k)


def _pad_heads(w):
    d = w.shape[0]
    w3 = w.reshape(d, N_HEADS, HEAD_DIM)
    return jnp.pad(w3, ((0, 0), (0, 0), (0, LANES - HEAD_DIM))).reshape(d, N_HEADS * LANES)


def kernel(x, mix_norm_g, ffn_norm_g, gm_w_in, gm_ln_g, gm_ln_b, gm_w_s, gm_b_s, gm_w_out, fox_w_qkvf, fox_b_f, fox_w_o, ffn_w_gate, ffn_w_up, ffn_conv_w, ffn_conv_b, ffn_w_down, final_norm_g, loss_target, m_mix_norm_g, m_ffn_norm_g, m_gm_w_in, m_gm_ln_g, m_gm_ln_b, m_gm_w_s, m_gm_b_s, m_gm_w_out, m_fox_w_qkvf, m_fox_b_f, m_fox_w_o, m_ffn_w_gate, m_ffn_w_up, m_ffn_conv_w, m_ffn_conv_b, m_ffn_w_down, m_final_norm_g, v_mix_norm_g, v_ffn_norm_g, v_gm_w_in, v_gm_ln_g, v_gm_ln_b, v_gm_w_s, v_gm_b_s, v_gm_w_out, v_fox_w_qkvf, v_fox_b_f, v_fox_w_o, v_ffn_w_gate, v_ffn_w_up, v_ffn_conv_w, v_ffn_conv_b, v_ffn_w_down, v_final_norm_g):
    T, D = x.shape[1], x.shape[2]
    E = gm_ln_g.shape[1]
    FF = ffn_conv_b.shape[1]
    NQKVF = 3 * D + N_HEADS
    xi, yi, ci = _mesh_pos()
    me = 4 * xi + 2 * yi + ci
    h0 = x.reshape(T, D)
    tgt = loss_target.reshape(T, D)

    nl = ffn_w_gate.shape[0]
    to16 = lambda a: a.astype(BF16)
    n_cw_rows = ffn_conv_w.size // LANES
    cw_rows = _pad_rows(_rows(ffn_conv_w.astype(F32), LANES), 16)
    (w_in_g,) = _all_gather([to16(gm_w_in[0])], "ag_weights")
    later, land_of, land_off, lands, src_ids = [], [], [], [], {}

    def add_sources(key, srcs, new_lands, of, offs):
        src_ids[key] = list(range(len(later), len(later) + len(srcs)))
        land_of.extend(len(lands) + o for o in of)
        land_off.extend(offs)
        later.extend(srcs)
        lands.extend(new_lands)

    def add_ffn(l):
        add_sources(f"ffn{l}", [to16(ffn_w_gate[l]), to16(ffn_w_up[l]), to16(ffn_w_down[l])],
                    [lax.empty((2 * N_DEV, D, FF // N_DEV), BF16), lax.empty((N_DEV, FF // N_DEV, D), BF16)],
                    [0, 0, 1], [0, N_DEV, 0])

    add_sources("gm_out", [to16(gm_w_out[0]), cw_rows],
                [lax.empty((N_DEV, E // N_DEV, D), BF16), lax.empty((N_DEV,) + cw_rows.shape, F32)], [0, 1], [0, 0])
    add_ffn(0)
    add_sources("fox", [to16(fox_w_qkvf[0]), to16(fox_w_o[0])],
                [lax.empty((N_DEV, D, NQKVF // N_DEV), BF16), lax.empty((N_DEV, D // N_DEV, D), BF16)], [0, 1], [0, 0])
    for l in range(1, nl):
        add_ffn(l)
    ag_dst = lambda a, k: land_off[a] + _slot(_mesh_pos())
    ag_src = lambda a, p: None
    ag_send, ag_recv, later, lands, ag_token = _direct_start(later, lands, land_of, ag_dst, ag_src, w_in_g,
                                                             "ag_later_start", collective_id=1)

    def own_blocks(landed, shards, offs):
        for s, o in zip(shards, offs):
            landed = lax.dynamic_update_index_in_dim(landed, s, o + me, 0)
        return landed

    def gather_wait(idxs, after, name):
        return _direct_wait(ag_send, ag_recv, later, lands, land_of, idxs, ag_dst, ag_src, after, name)

    ffn_w = {}

    def ffn_weights(l):
        return ffn_w[l]

    def land_ffn(l, shards, gu_land, dn_land):
        ffn_w[l] = (own_blocks(gu_land, shards[:2], [0, N_DEV]), own_blocks(dn_land, shards[2:3], [0]).reshape(FF, D))

    saved = {}

    def ffn_fwd(l, h_in, hn, next_g):
        wgul, wdl = ffn_weights(l)
        au, act = _ffn_up_fused(hn, wgul, conv_w_full[l], ffn_conv_b[l:l + 1], f"ffn{l}_up")
        saved[f"ffn{l}"] = (h_in, hn, au, act)
        if next_g is None:
            return _mm_nn(act, wdl, f"ffn{l}_down", res=h_in), None
        return _mm_nn(act, wdl, f"ffn{l}_down", res=h_in, norm_g=next_g)

    bs_col = gm_b_s[0].reshape(GM_GROUPS, CHUNK, 1)
    hn0 = _rms_fwd(h0, mix_norm_g[0:1], "mix0_norm", after=ag_token)
    z, gu = _gm_in_fused(hn0, w_in_g, gm_ln_g, gm_ln_b, gm_w_s[0], bs_col, "gm_in")
    mine_o, land_o = gather_wait(src_ids["gm_out"], z, "ag_wout_wait")
    w_out_g = own_blocks(land_o[0], mine_o[0:1], [0]).reshape(E, D)
    cwg = own_blocks(land_o[1], mine_o[1:2], [0])
    conv_w_full = jnp.transpose(cwg[:, :n_cw_rows].reshape(N_DEV, nl, 3, FF // N_DEV), (1, 2, 0, 3)).reshape(nl, 3, FF)
    h1, hn_f0 = _mm_nn(gu, w_out_g, "gm_out", res=h0, norm_g=ffn_norm_g[0:1])
    mine0, land0 = gather_wait(src_ids["ffn0"], h1, "ag_ffn0_wait")
    land_ffn(0, mine0, *land0)
    h2, hn2 = ffn_fwd(0, h1, hn_f0, mix_norm_g[1:2])

    mine_x, land_x = gather_wait(src_ids["fox"], h2, "ag_fox_wait")
    w_qkvf = jnp.transpose(own_blocks(land_x[0], mine_x[0:1], [0]), (1, 0, 2)).reshape(D, NQKVF)
    w_o_g = own_blocks(land_x[1], mine_x[1:2], [0]).reshape(D, D)
    w_q, w_k, w_v = w_qkvf[:, :D], w_qkvf[:, D:2 * D], w_qkvf[:, 2 * D:3 * D]
    w_f = jnp.pad(w_qkvf[:, 3 * D:], ((0, 0), (0, LANES - N_HEADS)))
    bf_row = jnp.pad(fox_b_f, ((0, 0), (0, LANES - N_HEADS)))
    sel_q, sel_k, const_q, const_k = _gate_tables()
    scale = HEAD_DIM ** -0.5
    f_logit = _mm_nn(hn2, w_f, "fox_f")
    cp, sneg = _gate_scan(f_logit, bf_row, "fox_scan")
    qp, kp = _qk_proj(hn2, (_pad_heads(w_q), _pad_heads(w_k)), cp, (sel_q, sel_k), (const_q, const_k),
                      (scale * LOG2E, 1.0), "fox_qk")
    vt = _mm_out_t(hn2, jnp.transpose(w_v), "fox_v")
    o, o32, lse = _attn_fwd_t(qp, kp, vt, "fox_attn")
    h3, hn_f1 = _mm_nn(o, w_o_g, "fox_o", res=h2, norm_g=ffn_norm_g[1:2])
    mine1, land1w = gather_wait([a for l in range(1, nl) for a in src_ids[f"ffn{l}"]], h3, "ag_ffn1_wait")
    for l in range(1, nl):
        land_ffn(l, mine1[3 * (l - 1):3 * l], land1w[2 * (l - 1)], land1w[2 * (l - 1) + 1])
    h4, _ = ffn_fwd(1, h3, hn_f1, None)

    dh, dh16, d_final, loss_row = _loss_head(h4, tgt, final_norm_g.reshape(1, D), "loss_head")
    loss = lax.psum(loss_row[0, 0], ("x", "y", "c"))

    rs_dst = lambda a, k: k
    rs_src = lambda a, p: _slot(p)
    me_idx = me.astype(jnp.int32).reshape(1)

    def rs_start(grads, name, cid):
        lands = [lax.empty((N_DEV - 1,) + g.shape[1:], BF16) for g in grads]
        return _direct_start(grads, lands, list(range(len(grads))), rs_dst, rs_src, loss_row, name, collective_id=cid)

    def rs_wait(st, after, name):
        n = len(st[2])
        return _direct_wait(st[0], st[1], st[2], st[3], list(range(n)), list(range(n)), rs_dst, rs_src, after, name)

    def ffn_bwd(l, dh, dh16, after=None):
        wgul, wdl = ffn_weights(l)
        h_in, hn, au, act = saved[f"ffn{l}"]
        da, dup, d_cw, d_cb = _ffn_mid_bwd(au, dh16, wdl, conv_w_full[l], ffn_conv_b[l:l + 1], f"ffn{l}_dmid", after=after)
        d_wd = _mm_tn(act, dh16, f"ffn{l}_dwd", out_dtype=BF16)
        dh_in, dh_in16, d_norm = _mm_nt([da, dup], wgul, f"ffn{l}_dhn", norm_bwd=(h_in, ffn_norm_g[l:l + 1], dh))
        d_wg = _mm_tn(hn, da, f"ffn{l}_dwg", blocked_w=FF // N_DEV, out_dtype=BF16)
        d_wu = _mm_tn(hn, dup, f"ffn{l}_dwu", blocked_w=FF // N_DEV, out_dtype=BF16)
        big_g = [d_wg, d_wu, d_wd.reshape(N_DEV, FF // N_DEV, D)]
        return dh_in, dh_in16, big_g, dict(cw=d_cw, cb=d_cb, norm=d_norm)

    dh, dh16, big_ffn1, g_ffn1 = ffn_bwd(1, dh, dh16)

    do = _mm_nt([dh16], w_o_g, "fox_do", out_dtype=BF16)
    d_wo = _mm_tn(o, dh16, "fox_dwo", out_dtype=BF16)
    dq, dk, dv, dqe, dke = _attn_bwd(qp, kp, vt, o32, do, lse, scale, "fox_dattn")
    gate_lane = lambda e: jnp.pad(jnp.transpose(e[:, :2, :].reshape(N_HEADS, T)), ((0, 0), (0, LANES - N_HEADS)))
    df, d_bf = _gate_scan_bwd(gate_lane(dqe), gate_lane(dke), sneg, "fox_dscan")
    dhn = _mm_nt([df], w_f, "fox_dhn_f")
    dh_mix1 = _mm_nt([dq, dk, dv], w_qkvf[:, :3 * D], "fox_dhn_qkv", add=dhn, norm_bwd=(h2, mix_norm_g[1:2], dh))
    d_wq = _mm_tn(hn2, dq, "fox_dwq", out_dtype=BF16)
    d_wk = _mm_tn(hn2, dk, "fox_dwk", out_dtype=BF16)
    d_wv = _mm_tn(hn2, dv, "fox_dwv", out_dtype=BF16)
    d_wf = _mm_tn(hn2, df, "fox_dwf", out_dtype=BF16)
    d_wqkvf = jnp.concatenate([d_wq, d_wk, d_wv, d_wf[:, :N_HEADS]], axis=1)
    dh, dh16, d_mix1 = dh_mix1
    st1 = rs_start([jnp.transpose(d_wqkvf.reshape(D, N_DEV, NQKVF // N_DEV), (1, 0, 2)),
                    d_wo.reshape(N_DEV, D // N_DEV, D)] + big_ffn1, "rs1_start", 2)

    dh, dh16, big_ffn0, g_ffn0 = ffn_bwd(0, dh, dh16, after=st1[4])
    d_wout = _mm_tn(gu, dh16, "gm_dwout", out_dtype=BF16)
    st2 = rs_start(big_ffn0 + [d_wout.reshape(N_DEV, E // N_DEV, D)], "rs2_start", 3)
    dz, d_lng, d_lnb, d_ws, d_bs = _sgu_bwd(z, dh16, w_out_g, gm_ln_g, gm_ln_b, gm_w_s[0], bs_col, "gm_dsgu", after=st2[4])
    d_win = _mm_tn(hn0, dz, "gm_dwin", blocked_w=2 * E // N_DEV, out_dtype=BF16)
    st3 = rs_start([d_win], "rs3_start", 4)
    dx, _, d_mix0 = _mm_nt([dz], w_in_g, "gm_dhn", after=st3[4], norm_bwd=(h0, mix_norm_g[0:1], dh))

    small = [("mix_norm_g", mix_norm_g, m_mix_norm_g, v_mix_norm_g, jnp.concatenate([d_mix0, d_mix1], axis=0)),
             ("ffn_norm_g", ffn_norm_g, m_ffn_norm_g, v_ffn_norm_g, jnp.concatenate([g_ffn0["norm"], g_ffn1["norm"]], axis=0)),
             ("gm_ln_g", gm_ln_g, m_gm_ln_g, v_gm_ln_g, d_lng),
             ("gm_ln_b", gm_ln_b, m_gm_ln_b, v_gm_ln_b, d_lnb),
             ("gm_w_s", gm_w_s, m_gm_w_s, v_gm_w_s, d_ws),
             ("gm_b_s", gm_b_s, m_gm_b_s, v_gm_b_s, d_bs),
             ("fox_b_f", fox_b_f, m_fox_b_f, v_fox_b_f, d_bf[:, :N_HEADS]),
             ("ffn_conv_b", ffn_conv_b, m_ffn_conv_b, v_ffn_conv_b, jnp.concatenate([g_ffn0["cb"], g_ffn1["cb"]], axis=0)),
             ("final_norm_g", final_norm_g, m_final_norm_g, v_final_norm_g, d_final)]
    d_cw_full = jnp.stack([g_ffn0["cw"], g_ffn1["cw"]], axis=0)

    def small_rows(a):
        flat = a.astype(F32).reshape(-1)
        n = -(-flat.size // (8 * LANES)) * (8 * LANES)
        return jnp.pad(flat, (0, n - flat.size)).reshape(-1, LANES)

    s_rows = [small_rows(p[1]).shape[0] for p in small]
    s_off = np.concatenate([[0], np.cumsum(s_rows)]).tolist()
    cw_g_rows = small_rows(d_cw_full)
    g_small = jnp.concatenate([small_rows(p[4]) for p in small] + [cw_g_rows], axis=0)

    own1, land1 = rs_wait(st1, dx, "rs1_wait")
    own2, land2 = rs_wait(st2, land1[0], "rs2_wait")
    big_out = {}

    def big_adamw(name, w, m, v, own, landed):
        shard2d = lambda a, c=(own[0] if isinstance(own, list) else own).shape[2]: a.reshape(-1, c)
        res = _sum_adamw([(own, [-1]), (landed, list(range(N_DEV - 1)))], shard2d(w), shard2d(m), shard2d(v),
                         f"adamw_{name}", sel=me_idx)
        big_out[name] = [t.reshape(w.shape) for t in res]

    big_adamw("fox_w_qkvf", fox_w_qkvf, m_fox_w_qkvf, v_fox_w_qkvf, own1[0], land1[0])
    big_adamw("fox_w_o", fox_w_o, m_fox_w_o, v_fox_w_o, own1[1], land1[1])
    big_adamw("ffn_w_gate", ffn_w_gate, m_ffn_w_gate, v_ffn_w_gate, [own2[0], own1[2]], [land2[0], land1[2]])
    big_adamw("ffn_w_up", ffn_w_up, m_ffn_w_up, v_ffn_w_up, [own2[1], own1[3]], [land2[1], land1[3]])
    big_adamw("ffn_w_down", ffn_w_down, m_ffn_w_down, v_ffn_w_down, [own2[2], own1[4]], [land2[2], land1[4]])
    big_adamw("gm_w_out", gm_w_out, m_gm_w_out, v_gm_w_out, own2[3], land2[3])

    (gs_all,) = _all_gather([g_small], "ag_small_grads")
    zeros_cw = jnp.zeros_like(cw_g_rows)
    cat = lambda k: jnp.concatenate([small_rows(p[k]) for p in small] + [zeros_cw], axis=0)
    small_out = _sum_adamw([(gs_all, list(range(N_DEV)))], cat(1), cat(2), cat(3), "adamw_small")
    gs = small_out[0]

    g_cw_full = gs[s_off[-1]:].reshape(-1)[:d_cw_full.size].reshape(d_cw_full.shape)
    g_cw = lax.dynamic_slice_in_dim(g_cw_full, me * (FF // N_DEV), FF // N_DEV, axis=2)
    cw2 = lambda a: _pad_rows(_rows(a.astype(F32), LANES), 16)
    cw_out = _sum_adamw([(cw2(g_cw)[None], [0])], cw2(ffn_conv_w), cw2(m_ffn_conv_w), cw2(v_ffn_conv_w), "adamw_conv_w")

    own3, land3 = rs_wait(st3, cw_out[0], "rs3_wait")
    big_adamw("gm_w_in", gm_w_in, m_gm_w_in, v_gm_w_in, own3[0], land3[0])

    names = ["mix_norm_g", "ffn_norm_g", "gm_w_in", "gm_ln_g", "gm_ln_b", "gm_w_s", "gm_b_s", "gm_w_out", "fox_w_qkvf",
             "fox_b_f", "fox_w_o", "ffn_w_gate", "ffn_w_up", "ffn_conv_w", "ffn_conv_b", "ffn_w_down", "final_norm_g"]
    small_idx = {p[0]: k for k, p in enumerate(small)}

    def pick(kind, name):
        if name in big_out:
            return big_out[name][kind]
        if name == "ffn_conv_w":
            return cw_out[kind][:n_cw_rows].reshape(ffn_conv_w.shape)
        k = small_idx[name]
        shp = small[k][1].shape
        return small_out[kind][s_off[k]:s_off[k + 1]].reshape(-1)[:int(np.prod(shp))].reshape(shp)

    outs = [loss, dx.reshape(x.shape)]
    for kind in range(4):
        outs += [pick(kind, n) for n in names]
    return tuple(outs)
```

```python
import math

import numpy as np
import jax
import jax.numpy as jnp
from jax import lax
from jax.experimental import pallas as pl
from jax.experimental.pallas import tpu as pltpu

F32 = jnp.float32
BF16 = jnp.bfloat16
MESH = pl.DeviceIdType.MESH

N_HEADS = 16
HEAD_DIM = 64
CHUNK = 128
GM_GROUPS = 8
RMS_EPS = 1e-6
LN_EPS = 1e-5
ADAM_LR = 0.001
ADAM_B1 = 0.9
ADAM_B2 = 0.999
ADAM_EPS = 1e-08
ADAM_WD = 0.01
ADAM_STEP = 10
N_DEV = 8

LANES = 128
VMEM_BYTES_V7X = 64 * 1024 * 1024
VMEM_LIMIT = 56 * 1024 * 1024

TM = 512
TM_MM = 1024
TT = 1024
TQ = 512
TF = 1024
KV_UNROLL_BWD = 2
KV_UNROLL = 4
TN_ROWS = 512
MM_BLOCK_BYTES = 8 * 1024 * 1024
NEG = -1e30
LOG2E = math.log2(math.e)
LN2 = math.log(2.0)


def _cp(sem=None, vmem=VMEM_LIMIT):
    return pltpu.CompilerParams(dimension_semantics=sem, vmem_limit_bytes=vmem)


def _gelu(x):
    c = math.sqrt(2.0 / math.pi)
    return x * (0.5 * (1.0 + jnp.tanh(c * (x + 0.044715 * (x * x * x)))))


def _gelu_grad(x):
    c = math.sqrt(2.0 / math.pi)
    t = jnp.tanh(c * (x + 0.044715 * (x * x * x)))
    return 0.5 * (1.0 + t) + x * (0.5 * (1.0 - t * t)) * (c * (1.0 + 3.0 * 0.044715 * (x * x)))


def _sigmoid(x):
    return 1.0 / (1.0 + jnp.exp(-x))


def _dot_nt(a, b):
    return lax.dot_general(a, b, (((1,), (1,)), ((), ())), preferred_element_type=F32)


def _dot_tn(a, b):
    return lax.dot_general(a, b, (((0,), (0,)), ((), ())), preferred_element_type=F32)


def _rms_fwd(h, g, name, after=None):
    T, D = h.shape
    tm = min(TM, T)

    def body(h_ref, g_ref, *rest):
        o_ref = rest[-1]
        x = h_ref[...]
        r = lax.rsqrt(jnp.mean(x * x, axis=-1, keepdims=True) + RMS_EPS)
        o_ref[...] = ((x * r) * g_ref[...]).astype(BF16)

    in_specs = [pl.BlockSpec((tm, D), lambda i: (i, 0)), pl.BlockSpec((1, D), lambda i: (0, 0))]
    args = [h, g]
    if after is not None:
        in_specs.append(pl.BlockSpec(memory_space=pl.ANY))
        args.append(after)
    return pl.pallas_call(
        body, name=name, grid=(T // tm,),
        in_specs=in_specs,
        out_specs=pl.BlockSpec((tm, D), lambda i: (i, 0)),
        out_shape=jax.ShapeDtypeStruct((T, D), BF16),
        compiler_params=_cp(("parallel",)),
    )(*args)


def _loss_head(h, tgt, g, name):
    T, D = h.shape
    tm = min(TM, T)

    def body(h_ref, t_ref, g_ref, o_ref, ob_ref, dg_ref, l_ref):
        x = h_ref[...]
        gg = g_ref[...]
        r = lax.rsqrt(jnp.mean(x * x, axis=-1, keepdims=True) + RMS_EPS)
        xr = x * r
        e = xr * gg - t_ref[...]
        lpart = 0.5 * jnp.sum(jnp.mean(e * e, axis=-1, keepdims=True), axis=0, keepdims=True)
        dy = e * (1.0 / D)
        dyg = dy * gg
        dot = jnp.mean(dyg * x, axis=-1, keepdims=True)
        dh = r * dyg - x * ((r * r * r) * dot)
        o_ref[...] = dh
        ob_ref[...] = dh.astype(BF16)
        part = jnp.sum(dy * xr, axis=0, keepdims=True)
        lrow = jnp.broadcast_to(lpart, (1, LANES))

        @pl.when(pl.program_id(0) == 0)
        def _():
            dg_ref[...] = part
            l_ref[...] = lrow

        @pl.when(pl.program_id(0) != 0)
        def _():
            dg_ref[...] += part
            l_ref[...] += lrow

    blk = pl.BlockSpec((tm, D), lambda i: (i, 0))
    row = pl.BlockSpec((1, D), lambda i: (0, 0))
    return pl.pallas_call(
        body, name=name, grid=(T // tm,),
        in_specs=[blk, blk, row],
        out_specs=[blk, blk, row, pl.BlockSpec((1, LANES), lambda i: (0, 0))],
        out_shape=[jax.ShapeDtypeStruct((T, D), F32), jax.ShapeDtypeStruct((T, D), BF16),
                   jax.ShapeDtypeStruct((1, D), F32), jax.ShapeDtypeStruct((1, LANES), F32)],
        compiler_params=_cp(("arbitrary",)),
    )(h, tgt, g)


def _mm_nn(a, b, name, out_dtype=F32, res=None, norm_g=None):
    M, K = a.shape
    b3 = b if b.ndim == 3 else b[None]
    nb, _, w = b3.shape
    N = nb * w
    tm = min(TM_MM, M, max(256, MM_BLOCK_BYTES // (4 * N)))
    o_spec = pl.BlockSpec((tm, N), lambda i: (i, 0))
    in_specs = [pl.BlockSpec((tm, K), lambda i: (i, 0)), pl.BlockSpec((nb, K, w), lambda i: (0, 0, 0))]
    args = [a, b3]
    if res is not None:
        in_specs.append(o_spec)
        args.append(res)
    if norm_g is not None:
        in_specs.append(pl.BlockSpec((1, N), lambda i: (0, 0)))
        args.append(norm_g)
    n_out = 2 if norm_g is not None else 1

    def body(*refs):
        a_ref, b_ref = refs[0], refs[1]
        o_ref = refs[-n_out]
        av = a_ref[...]
        for j in range(nb):
            cols = slice(j * w, (j + 1) * w)
            acc = jnp.dot(av, b_ref[j], preferred_element_type=F32)
            if res is not None:
                acc = refs[2][:, cols] + acc
            o_ref[:, cols] = acc.astype(out_dtype)
        if norm_g is not None:
            x = o_ref[...]
            r = lax.rsqrt(jnp.mean(x * x, axis=-1, keepdims=True) + RMS_EPS)
            refs[-1][...] = ((x * r) * refs[3][...]).astype(BF16)

    out_shape = jax.ShapeDtypeStruct((M, N), out_dtype)
    if norm_g is None:
        out_specs, out_shapes = o_spec, out_shape
    else:
        out_specs, out_shapes = [o_spec, o_spec], [out_shape, jax.ShapeDtypeStruct((M, N), BF16)]
    return pl.pallas_call(
        body, name=name, grid=(M // tm,),
        in_specs=in_specs, out_specs=out_specs, out_shape=out_shapes,
        compiler_params=_cp(("parallel",)),
    )(*args)


def _mm_nt(a_list, b, name, out_dtype=F32, add=None, after=None, norm_bwd=None):
    M, kw = a_list[0].shape
    tm = min(TM, M)
    na = len(a_list)
    blocked = b.ndim == 3
    N = b.shape[1] if blocked else b.shape[0]
    b_spec = pl.BlockSpec(b.shape, lambda i: (0,) * b.ndim)
    o_spec = pl.BlockSpec((tm, N), lambda i: (i, 0))
    row_spec = pl.BlockSpec((1, N), lambda i: (0, 0))
    in_specs = [pl.BlockSpec((tm, kw), lambda i: (i, 0)) for _ in a_list] + [b_spec]
    args = list(a_list) + [b]
    if add is not None:
        in_specs.append(o_spec)
        args.append(add)
    n_in = len(args)
    if norm_bwd is not None:
        in_specs += [o_spec, row_spec, o_spec]
        args += list(norm_bwd)
    if after is not None:
        in_specs.append(pl.BlockSpec(memory_space=pl.ANY))
        args.append(after)
    n_args = len(args)

    def body(*refs):
        a_refs = refs[:na]
        b_ref = refs[na]
        acc = refs[na + 1][...] if add is not None else None
        for s, a_ref in enumerate(a_refs):
            if blocked:
                w = b_ref.shape[2]
                per = kw // w
                parts = [_dot_nt(a_ref[:, jj * w:(jj + 1) * w], b_ref[s * per + jj]) for jj in range(per)]
            else:
                parts = [_dot_nt(a_ref[...], b_ref[:, s * kw:(s + 1) * kw])]
            for part in parts:
                acc = part if acc is None else acc + part
        if norm_bwd is None:
            refs[n_args][...] = acc.astype(out_dtype)
            return
        h_ref, g_ref, r_ref = refs[n_in:n_in + 3]
        o_ref, ob_ref, dg_ref = refs[n_args:n_args + 3]
        x = h_ref[...]
        r = lax.rsqrt(jnp.mean(x * x, axis=-1, keepdims=True) + RMS_EPS)
        dyg = acc * g_ref[...]
        dot = jnp.mean(dyg * x, axis=-1, keepdims=True)
        dh = r_ref[...] + (r * dyg - x * ((r * r * r) * dot))
        o_ref[...] = dh
        ob_ref[...] = dh.astype(BF16)
        part_g = jnp.sum(acc * (x * r), axis=0, keepdims=True)

        @pl.when(pl.program_id(0) == 0)
        def _():
            dg_ref[...] = part_g

        @pl.when(pl.program_id(0) != 0)
        def _():
            dg_ref[...] += part_g

    if norm_bwd is None:
        out_specs, out_shapes, sem = o_spec, jax.ShapeDtypeStruct((M, N), out_dtype), ("parallel",)
    else:
        out_specs = [o_spec, o_spec, row_spec]
        out_shapes = [jax.ShapeDtypeStruct((M, N), F32), jax.ShapeDtypeStruct((M, N), BF16),
                      jax.ShapeDtypeStruct((1, N), F32)]
        sem = ("arbitrary",)
    return pl.pallas_call(
        body, name=name, grid=(M // tm,),
        in_specs=in_specs, out_specs=out_specs, out_shape=out_shapes,
        compiler_params=_cp(sem),
    )(*args)


def _mm_out_t(a, bt, name):
    M, K = a.shape
    N = bt.shape[0]
    tm = min(TM_MM, M)

    def body(a_ref, b_ref, o_ref):
        o_ref[...] = _dot_nt(b_ref[...], a_ref[...]).astype(BF16)

    return pl.pallas_call(
        body, name=name, grid=(M // tm,),
        in_specs=[pl.BlockSpec((tm, K), lambda i: (i, 0)), pl.BlockSpec((N, K), lambda i: (0, 0))],
        out_specs=pl.BlockSpec((N, tm), lambda i: (0, i)),
        out_shape=jax.ShapeDtypeStruct((N, M), BF16),
        compiler_params=_cp(("parallel",)),
    )(a, bt)


def _mm_tn(x, y, name, blocked_w=None, out_dtype=F32):
    T, Kx = x.shape
    N = y.shape[1]
    tt = min(TT, T)
    nt = T // tt
    tkx = min(Kx, max(LANES, MM_BLOCK_BYTES // (4 * N)))
    if blocked_w is not None:
        blk_shape, full_shape = (N // blocked_w, tkx, blocked_w), (N // blocked_w, Kx, blocked_w)
        o_spec = pl.BlockSpec(blk_shape, lambda i, t: (0, i, 0))
    else:
        blk_shape, full_shape = (tkx, N), (Kx, N)
        o_spec = pl.BlockSpec(blk_shape, lambda i, t: (i, 0))

    rk = min(tkx, TN_ROWS)

    def body(x_ref, y_ref, o_ref, acc_ref):
        t = pl.program_id(1)

        @pl.when(t == 0)
        def _():
            acc_ref[...] = jnp.zeros_like(acc_ref)

        for r in range(tkx // rk):
            rows = slice(r * rk, (r + 1) * rk)
            part = _dot_tn(x_ref[:, rows], y_ref[...])
            if blocked_w is None:
                acc_ref[rows, :] += part
            else:
                for j in range(N // blocked_w):
                    acc_ref[j, rows, :] += part[:, j * blocked_w:(j + 1) * blocked_w]

        @pl.when(t == nt - 1)
        def _():
            o_ref[...] = acc_ref[...].astype(out_dtype)

    return pl.pallas_call(
        body, name=name, grid=(Kx // tkx, nt),
        in_specs=[pl.BlockSpec((tt, tkx), lambda i, t: (t, i)),
                  pl.BlockSpec((tt, N), lambda i, t: (t, 0))],
        out_specs=o_spec, out_shape=jax.ShapeDtypeStruct(full_shape, out_dtype),
        scratch_shapes=[pltpu.VMEM(blk_shape, F32)],
        compiler_params=_cp(("parallel", "arbitrary")),
    )(x, y)


def _sgu_pieces(z, lng, lnb, wc, bs_ref):
    E = z.shape[1] // 2
    gd = E // GM_GROUPS
    zu, zv = z[:, :E], z[:, E:]
    u = _gelu(zu)
    v = _gelu(zv)
    mu = jnp.mean(v, axis=-1, keepdims=True)
    xc = v - mu
    rs = lax.rsqrt(jnp.mean(xc * xc, axis=-1, keepdims=True) + LN_EPS)
    xhat = xc * rs
    vln = xhat * lng + lnb
    s = []
    for g in range(GM_GROUPS):
        vg = vln[:, g * gd:(g + 1) * gd].astype(BF16)
        s.append(jnp.dot(wc[g], vg, preferred_element_type=F32) + bs_ref[g])
    return zu, zv, u, xhat, rs, vln, s


def _causal_ws(ws_ref):
    t = lax.broadcasted_iota(jnp.int32, (CHUNK, CHUNK), 0)
    s = lax.broadcasted_iota(jnp.int32, (CHUNK, CHUNK), 1)
    tri = t >= s
    return [jnp.where(tri, ws_ref[g], 0.0).astype(BF16) for g in range(GM_GROUPS)], tri


def _gm_in_fused(hn, w_in, lng, lnb, ws, bs, name):
    T, D = hn.shape
    nb, _, w = w_in.shape
    E2 = nb * w
    E = E2 // 2
    gd = E // GM_GROUPS
    tm = min(TM, T)

    def body(a_ref, w_ref, lng_ref, lnb_ref, ws_ref, bs_ref, z_ref, o_ref):
        av = a_ref[...]
        for j in range(nb):
            z_ref[:, j * w:(j + 1) * w] = jnp.dot(av, w_ref[j], preferred_element_type=F32)
        wc, _ = _causal_ws(ws_ref)
        for c in range(tm // CHUNK):
            rows = slice(c * CHUNK, (c + 1) * CHUNK)
            _, _, u, _, _, _, s = _sgu_pieces(z_ref[rows, :], lng_ref[...], lnb_ref[...], wc, bs_ref)
            for g in range(GM_GROUPS):
                cols = slice(g * gd, (g + 1) * gd)
                o_ref[rows, cols] = (u[:, cols] * s[g]).astype(BF16)

    full = lambda shape: pl.BlockSpec(shape, lambda i: (0,) * len(shape))
    return pl.pallas_call(
        body, name=name, grid=(T // tm,),
        in_specs=[pl.BlockSpec((tm, D), lambda i: (i, 0)), full((nb, D, w)), full((1, E)), full((1, E)),
                  full((GM_GROUPS, CHUNK, CHUNK)), full((GM_GROUPS, CHUNK, 1))],
        out_specs=[pl.BlockSpec((tm, E2), lambda i: (i, 0)), pl.BlockSpec((tm, E), lambda i: (i, 0))],
        out_shape=[jax.ShapeDtypeStruct((T, E2), F32), jax.ShapeDtypeStruct((T, E), BF16)],
        compiler_params=_cp(("parallel",)),
    )(hn, w_in, lng, lnb, ws, bs)


def _sgu_bwd(z, dh16, w_out, lng, lnb, ws, bs, name, after=None):
    T, E2 = z.shape
    D = dh16.shape[1]
    E = E2 // 2
    gd = E // GM_GROUPS
    tm = min(2 * CHUNK, T)
    nsteps = T // tm

    def body(z_ref, dh_ref, wo_ref, lng_ref, lnb_ref, ws_ref, bs_ref, *rest):
        dz_ref, dlng_ref, dlnb_ref, dws_ref, dbs_ref, dg_ref = rest[-6:]
        i = pl.program_id(0)

        @pl.when(i == 0)
        def _():
            dlng_ref[...] = jnp.zeros_like(dlng_ref)
            dlnb_ref[...] = jnp.zeros_like(dlnb_ref)
            dws_ref[...] = jnp.zeros_like(dws_ref)
            dbs_ref[...] = jnp.zeros_like(dbs_ref)

        dg_ref[...] = _dot_nt(dh_ref[...], wo_ref[...]).astype(BF16)
        wc, tri = _causal_ws(ws_ref)
        lng_v = lng_ref[...]
        for c in range(tm // CHUNK):
            rows = slice(c * CHUNK, (c + 1) * CHUNK)
            zu, zv, u, xhat, rs, vln, s = _sgu_pieces(z_ref[rows, :], lng_v, lnb_ref[...], wc, bs_ref)
            dgc = dg_ref[rows, :].astype(F32)
            du, dvln = [], []
            for g in range(GM_GROUPS):
                cols = slice(g * gd, (g + 1) * gd)
                dgg = dgc[:, cols]
                du.append(dgg * s[g])
                ds = dgg * u[:, cols]
                dsb = ds.astype(BF16)
                dws_ref[g] += _dot_nt(dsb, vln[:, cols].astype(BF16))
                dbs_ref[g] += jnp.sum(ds, axis=-1, keepdims=True)
                dvln.append(_dot_tn(wc[g], dsb))
            du = jnp.concatenate(du, axis=1)
            dvln = jnp.concatenate(dvln, axis=1)
            dlng_ref[...] += jnp.sum(dvln * xhat, axis=0, keepdims=True)
            dlnb_ref[...] += jnp.sum(dvln, axis=0, keepdims=True)
            dxh = dvln * lng_v
            m1 = jnp.mean(dxh, axis=-1, keepdims=True)
            m2 = jnp.mean(dxh * xhat, axis=-1, keepdims=True)
            dv = rs * (dxh - m1 - xhat * m2)
            dz_ref[rows, :E] = (du * _gelu_grad(zu)).astype(BF16)
            dz_ref[rows, E:] = (dv * _gelu_grad(zv)).astype(BF16)

        @pl.when(i == nsteps - 1)
        def _():
            for g in range(GM_GROUPS):
                dws_ref[g] = jnp.where(tri, dws_ref[g], 0.0)

    full = lambda shape: pl.BlockSpec(shape, lambda i: (0,) * len(shape))
    in_specs = [pl.BlockSpec((tm, E2), lambda i: (i, 0)), pl.BlockSpec((tm, D), lambda i: (i, 0)), full((E, D)),
                full((1, E)), full((1, E)), full((GM_GROUPS, CHUNK, CHUNK)), full((GM_GROUPS, CHUNK, 1))]
    args = [z, dh16, w_out, lng, lnb, ws, bs]
    if after is not None:
        in_specs.append(pl.BlockSpec(memory_space=pl.ANY))
        args.append(after)
    return pl.pallas_call(
        body, name=name, grid=(nsteps,),
        in_specs=in_specs,
        out_specs=[pl.BlockSpec((tm, E2), lambda i: (i, 0)), full((1, E)), full((1, E)),
                   full((GM_GROUPS, CHUNK, CHUNK)), full((GM_GROUPS, CHUNK, 1))],
        out_shape=[jax.ShapeDtypeStruct((T, E2), BF16), jax.ShapeDtypeStruct((1, E), F32),
                   jax.ShapeDtypeStruct((1, E), F32), jax.ShapeDtypeStruct((GM_GROUPS, CHUNK, CHUNK), F32),
                   jax.ShapeDtypeStruct((GM_GROUPS, CHUNK, 1), F32)],
        scratch_shapes=[pltpu.VMEM((tm, E), BF16)],
        compiler_params=_cp(("arbitrary",)),
    )(*args)


HALO = 16


def _conv_taps(a_ext, w_ref, b_ref):
    n = a_ext.shape[0]
    am1 = pltpu.roll(a_ext, 1, 0)
    am2 = pltpu.roll(a_ext, 2, 0)
    del n
    return ((b_ref[...] + am2 * w_ref[0:1, :]) + am1 * w_ref[1:2, :]) + a_ext * w_ref[2:3, :], am1, am2


def _ffn_up_fused(hn, wgu, cw, cb, name):
    T, D = hn.shape
    nb2, _, w = wgu.shape
    nb = nb2 // 2
    F = nb * w
    tm = min(TM, T)

    def body(a_ref, w_ref, cw_ref, cb_ref, au_ref, act_ref, halo_ref):
        @pl.when(pl.program_id(0) == 0)
        def _():
            halo_ref[...] = jnp.zeros_like(halo_ref)

        av = a_ref[...]
        for j in range(nb):
            cols = slice(j * w, (j + 1) * w)
            g = jnp.dot(av, w_ref[j], preferred_element_type=F32)
            u = jnp.dot(av, w_ref[nb + j], preferred_element_type=F32)
            au_ref[:, cols] = g
            au_ref[:, F + j * w:F + (j + 1) * w] = u
            ext = jnp.concatenate([halo_ref[:, cols], g], axis=0)
            am1 = pltpu.roll(ext, 1, 0)
            am2 = pltpu.roll(ext, 2, 0)
            conv = ((cb_ref[:, cols] + am2 * cw_ref[0:1, cols]) + am1 * cw_ref[1:2, cols]) + ext * cw_ref[2:3, cols]
            conv = conv[HALO:, :]
            act_ref[:, cols] = ((conv * _sigmoid(conv)) * u).astype(BF16)
            halo_ref[:, cols] = g[tm - HALO:, :]

    return pl.pallas_call(
        body, name=name, grid=(T // tm,),
        in_specs=[pl.BlockSpec((tm, D), lambda i: (i, 0)), pl.BlockSpec((nb2, D, w), lambda i: (0, 0, 0)),
                  pl.BlockSpec((3, F), lambda i: (0, 0)), pl.BlockSpec((1, F), lambda i: (0, 0))],
        out_specs=[pl.BlockSpec((tm, 2 * F), lambda i: (i, 0)), pl.BlockSpec((tm, F), lambda i: (i, 0))],
        out_shape=[jax.ShapeDtypeStruct((T, 2 * F), F32), jax.ShapeDtypeStruct((T, F), BF16)],
        scratch_shapes=[pltpu.VMEM((HALO, F), F32)],
        compiler_params=_cp(("arbitrary",)),
    )(hn, wgu, cw, cb)


def _ffn_mid_bwd(au, dh16, wd, cw, cb, name, after=None):
    T, F = au.shape[0], au.shape[1] // 2
    D = dh16.shape[1]
    tm, tf = min(TM, T), min(TF, F)
    hb = tm // HALO
    nt = T // tm
    nf = F // tf
    last_h = T // HALO - 1

    def body(a_ref, ap_ref, an_ref, u_ref, un_ref, dh_ref, dhn_ref, wd_ref, w_ref, b_ref, *rest):
        da_ref, du_ref, dcw_ref, dcb_ref = rest[-4:]
        i = pl.program_id(1)
        prev = jnp.where(i == 0, 0.0, ap_ref[...])
        a_main = a_ref[...]
        a_ext = jnp.concatenate([prev, a_main, an_ref[...]], axis=0)
        conv, am1, am2 = _conv_taps(a_ext, w_ref, b_ref)
        conv = conv[HALO:, :]
        sig = _sigmoid(conv)
        u_ext = jnp.concatenate([u_ref[...], un_ref[...]], axis=0)
        wd_f = wd_ref[...]
        d_ext = jnp.concatenate([_dot_nt(dh_ref[...], wd_f), _dot_nt(dhn_ref[...], wd_f)], axis=0)
        d_ext = d_ext.astype(BF16).astype(F32)
        n = tm + HALO
        row = lax.broadcasted_iota(jnp.int32, (n, 1), 0)
        live = jnp.logical_or(row < tm, i < nt - 1)
        dconv = jnp.where(live, d_ext * u_ext * (sig * (1.0 + conv * (1.0 - sig))), 0.0)
        du_ref[...] = (d_ext[:tm, :] * (conv[:tm, :] * sig[:tm, :])).astype(BF16)
        dp1 = pltpu.roll(dconv, n - 1, 0)[:tm, :]
        dp2 = pltpu.roll(dconv, n - 2, 0)[:tm, :]
        dc = dconv[:tm, :]
        da_ref[...] = ((dc * w_ref[2:3, :] + dp1 * w_ref[1:2, :]) + dp2 * w_ref[0:1, :]).astype(BF16)
        g2 = jnp.sum(dc * a_main, axis=0, keepdims=True)
        g1 = jnp.sum(dc * am1[HALO:HALO + tm, :], axis=0, keepdims=True)
        g0 = jnp.sum(dc * am2[HALO:HALO + tm, :], axis=0, keepdims=True)
        gb = jnp.sum(dc, axis=0, keepdims=True)

        @pl.when(i == 0)
        def _():
            dcw_ref[...] = jnp.zeros_like(dcw_ref)
            dcb_ref[...] = jnp.zeros_like(dcb_ref)

        dcw_ref[0:1, :] += g0
        dcw_ref[1:2, :] += g1
        dcw_ref[2:3, :] += g2
        dcb_ref[...] += gb

    main = pl.BlockSpec((tm, tf), lambda f, i: (i, f))
    prev = pl.BlockSpec((HALO, tf), lambda f, i: (jnp.maximum(i * hb - 1, 0), f))
    nxt = pl.BlockSpec((HALO, tf), lambda f, i: (jnp.minimum((i + 1) * hb, last_h), f))
    main_u = pl.BlockSpec((tm, tf), lambda f, i: (i, nf + f))
    nxt_u = pl.BlockSpec((HALO, tf), lambda f, i: (jnp.minimum((i + 1) * hb, last_h), nf + f))
    in_specs = [main, prev, nxt, main_u, nxt_u,
                pl.BlockSpec((tm, D), lambda f, i: (i, 0)),
                pl.BlockSpec((HALO, D), lambda f, i: (jnp.minimum((i + 1) * hb, last_h), 0)),
                pl.BlockSpec((tf, D), lambda f, i: (f, 0)),
                pl.BlockSpec((3, tf), lambda f, i: (0, f)), pl.BlockSpec((1, tf), lambda f, i: (0, f))]
    args = [au, au, au, au, au, dh16, dh16, wd, cw, cb]
    if after is not None:
        in_specs.append(pl.BlockSpec(memory_space=pl.ANY))
        args.append(after)
    return pl.pallas_call(
        body, name=name, grid=(nf, nt),
        in_specs=in_specs,
        out_specs=[main, main, pl.BlockSpec((3, tf), lambda f, i: (0, f)), pl.BlockSpec((1, tf), lambda f, i: (0, f))],
        out_shape=[jax.ShapeDtypeStruct((T, F), BF16), jax.ShapeDtypeStruct((T, F), BF16),
                   jax.ShapeDtypeStruct((3, F), F32), jax.ShapeDtypeStruct((1, F), F32)],
        compiler_params=_cp(("parallel", "arbitrary")),
    )(*args)


def _split3(x):
    hi = x.astype(BF16)
    r1 = x - hi.astype(F32)
    mid = r1.astype(BF16)
    lo = (r1 - mid.astype(F32)).astype(BF16)
    return hi, mid, lo


def _tri_ones(n, upper):
    r = lax.broadcasted_iota(jnp.int32, (n, n), 0)
    c = lax.broadcasted_iota(jnp.int32, (n, n), 1)
    return jnp.where((r <= c) if upper else (r >= c), 1.0, 0.0).astype(BF16)


def _gate_scan(f, bf, name):
    T = f.shape[0]
    tm = min(256, T)

    def body(f_ref, b_ref, cp_ref, sn_ref, carry_ref):
        i = pl.program_id(0)

        @pl.when(i == 0)
        def _():
            carry_ref[...] = jnp.zeros_like(carry_ref)

        x = f_ref[...] + b_ref[...]
        e = jnp.exp(-jnp.abs(x))
        logf = jnp.minimum(x, 0.0) - jnp.log(1.0 + e)
        sn_ref[...] = jnp.where(x >= 0.0, e / (1.0 + e), 1.0 / (1.0 + e))
        tri = _tri_ones(tm, upper=False)
        c = carry_ref[...]
        for piece in _split3(logf):
            c = c + jnp.dot(tri, piece, preferred_element_type=F32)
        carry_ref[...] += jnp.sum(logf, axis=0, keepdims=True)
        hi, mid, lo = _split3(c * LOG2E)
        cp_ref[:, 0:LANES] = hi
        cp_ref[:, LANES:2 * LANES] = mid
        cp_ref[:, 2 * LANES:3 * LANES] = lo

    return pl.pallas_call(
        body, name=name, grid=(T // tm,),
        in_specs=[pl.BlockSpec((tm, LANES), lambda i: (i, 0)), pl.BlockSpec((1, LANES), lambda i: (0, 0))],
        out_specs=[pl.BlockSpec((tm, 3 * LANES), lambda i: (i, 0)), pl.BlockSpec((tm, LANES), lambda i: (i, 0))],
        out_shape=[jax.ShapeDtypeStruct((T, 3 * LANES), BF16), jax.ShapeDtypeStruct((T, LANES), F32)],
        scratch_shapes=[pltpu.VMEM((1, LANES), F32)],
        compiler_params=_cp(("arbitrary",)),
    )(f, bf)


def _gate_scan_bwd(dcq, dck, sneg, name):
    T = dcq.shape[0]
    tm = min(256, T)
    n = T // tm

    def body(dcq_ref, dck_ref, sn_ref, df_ref, db_ref, carry_ref):
        i = pl.program_id(0)

        @pl.when(i == 0)
        def _():
            carry_ref[...] = jnp.zeros_like(carry_ref)
            db_ref[...] = jnp.zeros_like(db_ref)

        tri = _tri_ones(tm, upper=True)
        dcb = dcq_ref[...] - dck_ref[...]
        acc = carry_ref[...]
        for piece in _split3(dcb):
            acc = acc + jnp.dot(tri, piece, preferred_element_type=F32)
        carry_ref[...] += jnp.sum(dcb, axis=0, keepdims=True)
        df = acc * sn_ref[...]
        df_ref[...] = df.astype(BF16)
        db_ref[...] += jnp.sum(df, axis=0, keepdims=True)

    rev = pl.BlockSpec((tm, LANES), lambda i: (n - 1 - i, 0))
    return pl.pallas_call(
        body, name=name, grid=(n,),
        in_specs=[rev, rev, rev],
        out_specs=[rev, pl.BlockSpec((1, LANES), lambda i: (0, 0))],
        out_shape=[jax.ShapeDtypeStruct((T, LANES), BF16), jax.ShapeDtypeStruct((1, LANES), F32)],
        scratch_shapes=[pltpu.VMEM((1, LANES), F32)],
        compiler_params=_cp(("arbitrary",)),
    )(dcq, dck, sneg)


def _qk_proj(hn, w_pads, cp, sels, consts, scales, name):
    T, D = hn.shape
    H = w_pads[0].shape[1] // LANES
    tm = min(TM, T)

    def body(a_ref, cp_ref, wq_ref, wk_ref, sq_ref, sk_ref, cq_ref, ck_ref, qo_ref, ko_ref):
        a = a_ref[...]
        cpv = cp_ref[...]
        for w_ref, sel_ref, c_ref, o_ref, scale in ((wq_ref, sq_ref, cq_ref, qo_ref, scales[0]),
                                                    (wk_ref, sk_ref, ck_ref, ko_ref, scales[1])):
            for p in range(H // 2):
                acc = jnp.dot(a, w_ref[:, p * 2 * LANES:(p + 1) * 2 * LANES], preferred_element_type=F32)
                if scale != 1.0:
                    acc = acc * scale
                acc = acc + jnp.dot(cpv, sel_ref[p], preferred_element_type=F32) + c_ref[p]
                o_ref[2 * p] = acc[:, :LANES].astype(BF16)
                o_ref[2 * p + 1] = acc[:, LANES:].astype(BF16)

    whole = lambda t: pl.BlockSpec(t.shape, lambda i: (0,) * t.ndim)
    out = jax.ShapeDtypeStruct((H, T, LANES), BF16)
    o_spec = pl.BlockSpec((H, tm, LANES), lambda i: (0, i, 0))
    return pl.pallas_call(
        body, name=name, grid=(T // tm,),
        in_specs=[pl.BlockSpec((tm, D), lambda i: (i, 0)), pl.BlockSpec((tm, 3 * LANES), lambda i: (i, 0)),
                  whole(w_pads[0]), whole(w_pads[1]), whole(sels[0]), whole(sels[1]), whole(consts[0]), whole(consts[1])],
        out_specs=[o_spec, o_spec], out_shape=[out, out],
        compiler_params=_cp(("parallel",)),
    )(hn, cp, w_pads[0], w_pads[1], sels[0], sels[1], consts[0], consts[1])


def _lane_lo():
    return lax.broadcasted_iota(jnp.int32, (1, LANES), 1) < HEAD_DIM


def _attn_fwd_t(qp, kp, vt, name):
    H, T, _ = qp.shape
    tq = min(TQ, T)
    hd = HEAD_DIM
    ext = hd + 16

    def body(q_ref, k_ref, vt_ref, o_ref, o32_ref, lse_ref, m_sc, acc_sc):
        i = pl.program_id(1)
        m_sc[...] = jnp.full(m_sc.shape, NEG, F32)
        acc_sc[...] = jnp.zeros_like(acc_sc)
        q_t = [jnp.transpose(q_ref[h].astype(F32)).astype(BF16) for h in range(2)]
        ones_rows = jnp.where(lax.broadcasted_iota(jnp.int32, (16, tq), 0) == 0, 1.0, 0.0).astype(BF16)

        def steps(blocks):
            offs = [pl.multiple_of(j * tq, tq) for j, _ in blocks]
            s_all = [[jnp.dot(k_ref[h, pl.ds(off, tq), :], q_t[h], preferred_element_type=F32) for h in range(2)]
                     for off in offs]
            for (j, masked), off, s_blk in zip(blocks, offs, s_all):
                for h in range(2):
                    s = s_blk[h]
                    if masked:
                        kr = lax.broadcasted_iota(jnp.int32, (tq, tq), 0)
                        qc = lax.broadcasted_iota(jnp.int32, (tq, tq), 1)
                        s = jnp.where(qc >= kr, s, NEG)
                    m_prev = m_sc[h]
                    m_new = jnp.maximum(m_prev, jnp.max(s, axis=0, keepdims=True))
                    alpha = jnp.exp2(m_prev - m_new)
                    p16 = jnp.exp2(s - m_new).astype(BF16)
                    v_aug = jnp.concatenate([vt_ref[h * hd:(h + 1) * hd, pl.ds(off, tq)], ones_rows], axis=0)
                    pv = jnp.dot(v_aug, p16, preferred_element_type=F32)
                    acc_sc[h] = alpha * acc_sc[h] + pv
                    m_sc[h] = m_new

        def group_body(t, carry):
            steps([(KV_UNROLL * t + u, False) for u in range(KV_UNROLL)])
            return carry

        lax.fori_loop(0, i // KV_UNROLL, group_body, 0)
        for rem in range(KV_UNROLL):

            @pl.when(i % KV_UNROLL == rem)
            def _(rem=rem):
                steps([(i - rem + u, u == rem) for u in range(rem + 1)])

        o_t, lse_t = [], []
        for h in range(2):
            acc = acc_sc[h]
            l = acc[hd:hd + 1, :]
            o_t.append(acc[:hd, :] / l)
            lse_t.append(jnp.broadcast_to(m_sc[h] + jnp.log(l) * LOG2E, (hd, tq)))
        o = jnp.transpose(jnp.concatenate(o_t, axis=0))
        o_ref[...] = o.astype(BF16)
        o32_ref[...] = o
        lse_ref[...] = jnp.transpose(jnp.concatenate(lse_t, axis=0))

    oblk = pl.BlockSpec((tq, LANES), lambda p, i: (i, p))
    return pl.pallas_call(
        body, name=name, grid=(H // 2, T // tq),
        in_specs=[pl.BlockSpec((2, tq, LANES), lambda p, i: (p, i, 0)),
                  pl.BlockSpec((2, T, LANES), lambda p, i: (p, 0, 0)),
                  pl.BlockSpec((2 * hd, T), lambda p, i: (p, 0))],
        out_specs=[oblk, oblk, pl.BlockSpec((None, tq, LANES), lambda p, i: (p, i, 0))],
        out_shape=[jax.ShapeDtypeStruct((T, H * HEAD_DIM), BF16), jax.ShapeDtypeStruct((T, H * HEAD_DIM), F32),
                   jax.ShapeDtypeStruct((H // 2, T, LANES), F32)],
        scratch_shapes=[pltpu.VMEM((2, 1, tq), F32), pltpu.VMEM((2, ext, tq), F32)],
        compiler_params=_cp(("parallel", "arbitrary")),
    )(qp, kp, vt)


def _attn_bwd(qp, kp, vt, o, do, lse, scale, name):
    H, T, _ = qp.shape
    tq = min(TQ, T)
    nq = T // tq
    nrep = tq // LANES

    def body(q_ref, k_ref, vt_ref, o_ref, do_ref, lse_ref, dq_ref, dk_ref, dv_ref, dqe_ref, dke_ref, dk_sc, dv_sc, dq_sc):
        i = pl.program_id(1)

        @pl.when(i == 0)
        def _():
            dk_sc[...] = jnp.zeros_like(dk_sc)
            dv_sc[...] = jnp.zeros_like(dv_sc)

        dq_sc[...] = jnp.zeros_like(dq_sc)

        lo = _lane_lo()
        dob = do_ref[...]
        dof = dob.astype(F32)
        prod = dof * o_ref[...].astype(F32)
        lse2 = lse_ref[...]
        lse2_sw = pltpu.roll(lse2, HEAD_DIM, 1)
        zero = jnp.zeros_like(dob)
        do_h = [jnp.where(lo, dob, zero), jnp.where(lo, zero, dob)]
        rep = lambda col: jnp.broadcast_to(col, (tq, LANES))
        delta = [rep(jnp.sum(jnp.where(lo, prod, 0.0), axis=-1, keepdims=True)),
                 rep(jnp.sum(jnp.where(lo, 0.0, prod), axis=-1, keepdims=True))]
        lse_h = [jnp.where(lo, lse2, lse2_sw), jnp.where(lo, lse2_sw, lse2)]
        qs = [q_ref[0], q_ref[1]]
        tr16 = lambda a: jnp.transpose(a.astype(F32)).astype(BF16)
        q_t = [tr16(qs[0]), tr16(qs[1])]
        do_t = [tr16(do_h[0]), tr16(do_h[1])]

        def steps(blocks):
            offs = [pl.multiple_of(j * tq, tq) for j, _ in blocks]
            vblks = [vt_ref[:, pl.ds(off, tq)] for off in offs]
            kblks = [[k_ref[h, pl.ds(off, tq), :] for h in range(2)] for off in offs]
            s_all = [[_dot_nt(qs[h], kb[h]) for h in range(2)] for kb in kblks]
            dp_all = [[jnp.dot(do_h[h], vb, preferred_element_type=F32) for h in range(2)] for vb in vblks]
            for b, ((j, masked), off) in enumerate(zip(blocks, offs)):
                dv_add = None
                for h in range(2):
                    kblk, s, dp = kblks[b][h], s_all[b][h], dp_all[b][h]
                    p16, ds16 = [], []
                    for c in range(nrep):
                        cols = slice(c * LANES, (c + 1) * LANES)
                        p = jnp.exp2(s[:, cols] - lse_h[h])
                        if masked:
                            r = lax.broadcasted_iota(jnp.int32, (tq, LANES), 0)
                            cc = lax.broadcasted_iota(jnp.int32, (tq, LANES), 1)
                            p = jnp.where(r >= cc + c * LANES, p, 0.0)
                        p16.append(p.astype(BF16))
                        ds16.append((p * (dp[:, cols] - delta[h])).astype(BF16))
                    p16 = jnp.concatenate(p16, axis=1)
                    dsb = jnp.concatenate(ds16, axis=1)
                    dq_sc[h] += jnp.dot(dsb, kblk, preferred_element_type=F32)
                    dk_sc[h, :, pl.ds(off, tq)] += jnp.dot(q_t[h], dsb, preferred_element_type=F32)
                    pv = jnp.dot(do_t[h], p16, preferred_element_type=F32)
                    dv_add = pv if dv_add is None else dv_add + pv
                dv_sc[:, pl.ds(off, tq)] += dv_add

        def group_body(t, carry):
            steps([(KV_UNROLL_BWD * t + u, False) for u in range(KV_UNROLL_BWD)])
            return carry

        lax.fori_loop(0, i // KV_UNROLL_BWD, group_body, 0)
        for rem in range(KV_UNROLL_BWD):

            @pl.when(i % KV_UNROLL_BWD == rem)
            def _(rem=rem):
                steps([(i - rem + u, u == rem) for u in range(rem + 1)])

        dq0, dq1 = dq_sc[0], dq_sc[1]
        dq_ref[...] = (jnp.where(lo, dq0, pltpu.roll(dq1, HEAD_DIM, 1)) * scale).astype(BF16)
        row8 = lax.broadcasted_iota(jnp.int32, (8, 1), 0)
        pick = lambda blk, r: jnp.sum(jnp.where(row8 == r, blk, 0.0), axis=0, keepdims=True)
        two_rows = lambda a, b: jnp.where(row8 == 0, a, jnp.where(row8 == 1, b, 0.0))
        gate_rows = slice(HEAD_DIM, HEAD_DIM + 8)
        dqe_ref[...] = two_rows(pick(jnp.transpose(dq0)[gate_rows, :], 0), pick(jnp.transpose(dq1)[gate_rows, :], 0))

        @pl.when(i == nq - 1)
        def _():
            dke_ref[...] = two_rows(pick(dk_sc[0, gate_rows, :], 3), pick(dk_sc[1, gate_rows, :], 3))
            for cb in range(nq):
                tok = slice(cb * tq, (cb + 1) * tq)
                dk0 = jnp.transpose(dk_sc[0, :, tok])
                dk1 = jnp.transpose(dk_sc[1, :, tok])
                dk_ref[tok, :] = (jnp.where(lo, dk0, pltpu.roll(dk1, HEAD_DIM, 1)) * LN2).astype(BF16)
                dv_ref[tok, :] = jnp.transpose(dv_sc[:, tok]).astype(BF16)

    qblk = pl.BlockSpec((tq, LANES), lambda p, i: (i, p))
    pair = pl.BlockSpec((T, LANES), lambda p, i: (0, p))
    tok16 = jax.ShapeDtypeStruct((T, H * HEAD_DIM), BF16)
    gate32 = jax.ShapeDtypeStruct((H // 2, 8, T), F32)
    return pl.pallas_call(
        body, name=name, grid=(H // 2, nq),
        in_specs=[pl.BlockSpec((2, tq, LANES), lambda p, i: (p, i, 0)),
                  pl.BlockSpec((2, T, LANES), lambda p, i: (p, 0, 0)),
                  pl.BlockSpec((LANES, T), lambda p, i: (p, 0)), qblk, qblk,
                  pl.BlockSpec((None, tq, LANES), lambda p, i: (p, i, 0))],
        out_specs=[qblk, pair, pair, pl.BlockSpec((None, 8, tq), lambda p, i: (p, 0, i)),
                   pl.BlockSpec((None, 8, T), lambda p, i: (p, 0, 0))],
        out_shape=[tok16, tok16, tok16, gate32, gate32],
        scratch_shapes=[pltpu.VMEM((2, LANES, T), F32), pltpu.VMEM((LANES, T), F32),
                        pltpu.VMEM((2, tq, LANES), F32)],
        compiler_params=_cp(("parallel", "arbitrary")),
    )(qp, kp, vt, o, do, lse)


def _mesh_pos():
    return lax.axis_index("x"), lax.axis_index("y"), lax.axis_index("c")


def _all_gather(arrs, name, groups=None):
    n = len(arrs)
    if groups is None:
        groups = [(a, 0) for a in range(n)]
    ng = 1 + max(g for g, _ in groups)
    per_group = [sum(1 for g, _ in groups if g == gi) for gi in range(ng)]
    first_of = [next(a for a in range(n) if groups[a][0] == gi) for gi in range(ng)]

    def body(*refs):
        ins, outs = refs[:n], refs[n:n + ng]
        send_sems, recv_sems, local_sems = refs[n + ng:]
        x, y, c = _mesh_pos()
        me, sib = (x, y, c), (x, y, 1 - c)
        chips = [(1 - x, y), (x, 1 - y), (1 - x, 1 - y)]

        def dst_of(a, px, py, pc):
            g, k = groups[a]
            return outs[g].at[N_DEV * k + 4 * px + 2 * py + pc]

        def copy(a, k, block, to, src=None):
            dst = dst_of(a, *block)
            return pltpu.make_async_remote_copy(
                src_ref=dst if src is None else src, dst_ref=dst,
                send_sem=send_sems.at[a, k], recv_sem=recv_sems.at[a, k], device_id=to, device_id_type=MESH)

        mine = [pltpu.make_async_copy(ins[a], dst_of(a, *me), local_sems.at[a]) for a in range(n)]
        for cp in mine:
            cp.start()
        first = []
        for a in range(n):
            first.append(copy(a, 0, me, sib, src=ins[a]))
            first += [copy(a, 1 + j, me, (*chip, c), src=ins[a]) for j, chip in enumerate(chips)]
        for cp in first:
            cp.start()
        passed = []
        for j, chip in enumerate(chips):
            for a in range(n):
                copy(a, 1 + j, (*chip, c), me).wait_recv()
                fwd = copy(a, 4 + j, (*chip, c), sib)
                fwd.start()
                passed.append(fwd)
        for a in range(n):
            copy(a, 0, sib, me).wait_recv()
            for j, chip in enumerate(chips):
                copy(a, 4 + j, (*chip, 1 - c), me).wait_recv()
        for cp in first + passed:
            cp.wait_send()
        for cp in mine:
            cp.wait()

    any_spec = pl.BlockSpec(memory_space=pl.ANY)
    return pl.pallas_call(
        body, name=name,
        in_specs=[any_spec] * n, out_specs=[any_spec] * ng,
        out_shape=[jax.ShapeDtypeStruct((N_DEV * per_group[gi],) + arrs[first_of[gi]].shape, arrs[first_of[gi]].dtype)
                   for gi in range(ng)],
        scratch_shapes=[pltpu.SemaphoreType.DMA((n, 7)), pltpu.SemaphoreType.DMA((n, 7)),
                        pltpu.SemaphoreType.DMA((n,))],
    )(*arrs)


HBM_SPEC = pl.BlockSpec(memory_space=pltpu.HBM)
SEM_SPEC = pl.BlockSpec(memory_space=pltpu.SEMAPHORE)
ANY_SPEC = pl.BlockSpec(memory_space=pl.ANY)
DATAFLOW_EFFECT = pltpu.SideEffectType.DATAFLOW_SIDE_EFFECTING


def _peers():
    x, y, c = _mesh_pos()
    flip = lambda v, b: 1 - v if b else v
    return [(flip(x, (k >> 2) & 1), flip(y, (k >> 1) & 1), flip(c, k & 1)) for k in range(1, N_DEV)]


def _slot(p):
    return 4 * p[0] + 2 * p[1] + p[2]


def _direct_copy(src_refs, land_refs, sems, a, k, p, land_of, dst_slot, src_slot):
    s = src_slot(a, p)
    return pltpu.make_async_remote_copy(
        src_ref=src_refs[a] if s is None else src_refs[a].at[s], dst_ref=land_refs[land_of[a]].at[dst_slot(a, k)],
        send_sem=sems[0].at[a * (N_DEV - 1) + k], recv_sem=sems[1].at[a * (N_DEV - 1) + k], device_id=p,
        device_id_type=MESH)


def _direct_start(srcs, lands, land_of, dst_slot, src_slot, after, name, collective_id):
    n, nl = len(srcs), len(lands)

    def body(*refs):
        src_refs, land_refs = refs[:n], refs[n:n + nl]
        sems = (refs[n + nl + 1], refs[n + nl + 2])
        token = refs[-1]
        peers = _peers()
        barrier = pltpu.get_barrier_semaphore()
        for p in peers:
            pl.semaphore_signal(barrier, inc=1, device_id=p, device_id_type=MESH)
        pl.semaphore_wait(barrier, N_DEV - 1)
        for a in range(n):
            for k, p in enumerate(peers):
                _direct_copy(src_refs, land_refs, sems, a, k, p, land_of, dst_slot, src_slot).start()
        token[...] = jnp.zeros_like(token)

    hbm = lambda t: pltpu.HBM(t.shape, t.dtype)
    sem_t = pltpu.SemaphoreType.DMA((n * (N_DEV - 1),))
    outs = pl.pallas_call(
        body, name=name,
        out_shape=(sem_t, sem_t, *[hbm(t) for t in srcs], *[hbm(t) for t in lands], jax.ShapeDtypeStruct((8, LANES), F32)),
        in_specs=[HBM_SPEC] * (n + nl) + [ANY_SPEC],
        out_specs=(SEM_SPEC, SEM_SPEC, *([HBM_SPEC] * (n + nl)), pl.BlockSpec(memory_space=pltpu.VMEM)),
        input_output_aliases={i: 2 + i for i in range(n + nl)},
        compiler_params=pltpu.CompilerParams(has_side_effects=DATAFLOW_EFFECT, collective_id=collective_id),
    )(*[pltpu.with_memory_space_constraint(t, pltpu.HBM) for t in srcs],
      *[pltpu.with_memory_space_constraint(t, pltpu.HBM) for t in lands], after)
    return outs[0], outs[1], list(outs[2:2 + n]), list(outs[2 + n:2 + n + nl]), outs[-1]


def _direct_wait(send_sems, recv_sems, srcs, lands, land_of, idxs, dst_slot, src_slot, after, name):
    land_ids = []
    for a in idxs:
        if land_of[a] not in land_ids:
            land_ids.append(land_of[a])
    m, ml = len(idxs), len(land_ids)
    sub_land_of = {j: land_ids.index(land_of[a]) for j, a in enumerate(idxs)}

    def body(*refs):
        src_refs, land_refs = refs[:m], refs[m:m + ml]
        ssem, rsem = refs[m + ml], refs[m + ml + 1]
        for j, a in enumerate(idxs):
            for k, p in enumerate(_peers()):
                s = src_slot(a, p)
                cp = pltpu.make_async_remote_copy(
                    src_ref=src_refs[j] if s is None else src_refs[j].at[s],
                    dst_ref=land_refs[sub_land_of[j]].at[dst_slot(a, k)],
                    send_sem=ssem.at[a * (N_DEV - 1) + k], recv_sem=rsem.at[a * (N_DEV - 1) + k], device_id=p,
                    device_id_type=MESH)
                cp.wait_send()
                cp.wait_recv()

    hbm = lambda t: pltpu.HBM(t.shape, t.dtype)
    sub_s, sub_l = [srcs[a] for a in idxs], [lands[g] for g in land_ids]
    outs = pl.pallas_call(
        body, name=name,
        out_shape=(*[hbm(t) for t in sub_s], *[hbm(t) for t in sub_l]),
        in_specs=[HBM_SPEC] * (m + ml) + [SEM_SPEC, SEM_SPEC, ANY_SPEC],
        out_specs=tuple([HBM_SPEC] * (m + ml)),
        input_output_aliases={i: i for i in range(m + ml)},
        compiler_params=pltpu.CompilerParams(has_side_effects=DATAFLOW_EFFECT),
    )(*sub_s, *sub_l, send_sems, recv_sems, after)
    return list(outs[:m]), list(outs[m:])


def _row_block(R, C):
    best = None
    for d in range(16, R + 1, 16):
        if R % d == 0 and d * C <= 256 * 1024:
            best = d
    return best if best is not None else R


def _adamw_math(w, g, m, v):
    m = ADAM_B1 * m + (1.0 - ADAM_B1) * g
    v = ADAM_B2 * v + (1.0 - ADAM_B2) * (g * g)
    m_hat = m / (1.0 - ADAM_B1 ** ADAM_STEP)
    v_hat = v / (1.0 - ADAM_B2 ** ADAM_STEP)
    delta = -ADAM_LR * (m_hat / (jnp.sqrt(v_hat) + ADAM_EPS) + ADAM_WD * w)
    return delta, m, v


def _sum_adamw(parts, w, m, v, name, sel=None):
    R, C = w.shape
    nseg = max(len(arr) if isinstance(arr, list) else 1 for arr, _ in parts)
    tr = _row_block(R // nseg, C)
    bps = R // nseg // tr
    specs, args = [], []
    for arr, idxs in parts:
        pieces = arr if isinstance(arr, list) else [arr] * nseg
        for idx in idxs:
            for sg in range(nseg if isinstance(arr, list) else 1):
                row = (lambda i, sg=sg: jnp.clip(i - sg * bps, 0, bps - 1)) if isinstance(arr, list) else (lambda i: i)
                if idx < 0:
                    specs.append(pl.BlockSpec((None, tr, C), lambda i, s, row=row: (s[0], row(i), 0)))
                else:
                    specs.append(pl.BlockSpec((None, tr, C), lambda i, s, idx=idx, row=row: (idx, row(i), 0)))
                args.append(pieces[sg])
    seg_counts = [(nseg if isinstance(arr, list) else 1) for arr, idxs in parts for _ in idxs]
    npart = len(args)
    blk = pl.BlockSpec((tr, C), lambda i, s: (i, 0))

    def body(s_ref, *refs):
        del s_ref
        seg = pl.program_id(0) // bps
        g, at = None, 0
        for cnt in seg_counts:
            term = refs[at][...].astype(F32)
            for sg in range(1, cnt):
                term = jnp.where(seg == sg, refs[at + sg][...].astype(F32), term)
            g = term if g is None else g + term
            at += cnt
        w_ref, m_ref, v_ref, g_out, d_out, m_out, v_out = refs[npart:]
        delta, mm, vv = _adamw_math(w_ref[...], g, m_ref[...], v_ref[...])
        g_out[...] = g
        d_out[...] = delta
        m_out[...] = mm
        v_out[...] = vv

    grid_spec = pltpu.PrefetchScalarGridSpec(
        num_scalar_prefetch=1, grid=(R // tr,),
        in_specs=specs + [blk, blk, blk], out_specs=[blk] * 4)
    if sel is None:
        sel = jnp.zeros((1,), jnp.int32)
    return pl.pallas_call(
        body, name=name, grid_spec=grid_spec,
        out_shape=[jax.ShapeDtypeStruct((R, C), F32)] * 4,
        compiler_params=_cp(("parallel",)),
    )(sel, *args, w, m, v)


def _rows(a, c):
    return a.reshape(-1, c)


def _pad_rows(a, r):
    return jnp.pad(a, ((0, r - a.shape[0]), (0, 0))) if a.shape[0] != r else a


def _gate_tables():
    hp = N_HEADS // 2
    sel_q = np.zeros((hp, 3 * LANES, 2 * LANES), np.float32)
    sel_k = np.zeros((hp, 3 * LANES, 2 * LANES), np.float32)
    const_q = np.zeros((hp, 1, 2 * LANES), np.float32)
    const_k = np.zeros((hp, 1, 2 * LANES), np.float32)
    for p in range(hp):
        for hh in range(2):
            h = 2 * p + hh
            base = hh * LANES + HEAD_DIM
            for piece in range(3):
                sel_q[p, piece * LANES + h, base + piece] = 1.0
                sel_k[p, piece * LANES + h, base + 3 + piece] = -1.0
            const_k[p, 0, base:base + 3] = 1.0
            const_q[p, 0, base + 3:base + 6] = 1.0
    as_bf = lambda t: jnp.asarray(t, BF16)
    return as_bf(sel_q), as_bf(sel_k), jnp.asarray(const_q), jnp.asarray(const_k)


def _pad_heads(w):
    d = w.shape[0]
    w3 = w.reshape(d, N_HEADS, HEAD_DIM)
    return jnp.pad(w3, ((0, 0), (0, 0), (0, LANES - HEAD_DIM))).reshape(d, N_HEADS * LANES)


def kernel(x, mix_norm_g, ffn_norm_g, gm_w_in, gm_ln_g, gm_ln_b, gm_w_s, gm_b_s, gm_w_out, fox_w_qkvf, fox_b_f, fox_w_o, ffn_w_gate, ffn_w_up, ffn_conv_w, ffn_conv_b, ffn_w_down, final_norm_g, loss_target, m_mix_norm_g, m_ffn_norm_g, m_gm_w_in, m_gm_ln_g, m_gm_ln_b, m_gm_w_s, m_gm_b_s, m_gm_w_out, m_fox_w_qkvf, m_fox_b_f, m_fox_w_o, m_ffn_w_gate, m_ffn_w_up, m_ffn_conv_w, m_ffn_conv_b, m_ffn_w_down, m_final_norm_g, v_mix_norm_g, v_ffn_norm_g, v_gm_w_in, v_gm_ln_g, v_gm_ln_b, v_gm_w_s, v_gm_b_s, v_gm_w_out, v_fox_w_qkvf, v_fox_b_f, v_fox_w_o, v_ffn_w_gate, v_ffn_w_up, v_ffn_conv_w, v_ffn_conv_b, v_ffn_w_down, v_final_norm_g):
    T, D = x.shape[1], x.shape[2]
    E = gm_ln_g.shape[1]
    FF = ffn_conv_b.shape[1]
    NQKVF = 3 * D + N_HEADS
    xi, yi, ci = _mesh_pos()
    me = 4 * xi + 2 * yi + ci
    h0 = x.reshape(T, D)
    tgt = loss_target.reshape(T, D)

    nl = ffn_w_gate.shape[0]
    to16 = lambda a: a.astype(BF16)
    n_cw_rows = ffn_conv_w.size // LANES
    cw_rows = _pad_rows(_rows(ffn_conv_w.astype(F32), LANES), 16)
    (w_in_g,) = _all_gather([to16(gm_w_in[0])], "ag_weights")
    later, land_of, land_off, lands, src_ids = [], [], [], [], {}

    def add_sources(key, srcs, new_lands, of, offs):
        src_ids[key] = list(range(len(later), len(later) + len(srcs)))
        land_of.extend(len(lands) + o for o in of)
        land_off.extend(offs)
        later.extend(srcs)
        lands.extend(new_lands)

    def add_ffn(l):
        add_sources(f"ffn{l}", [to16(ffn_w_gate[l]), to16(ffn_w_up[l]), to16(ffn_w_down[l])],
                    [lax.empty((2 * N_DEV, D, FF // N_DEV), BF16), lax.empty((N_DEV, FF // N_DEV, D), BF16)],
                    [0, 0, 1], [0, N_DEV, 0])

    add_sources("gm_out", [to16(gm_w_out[0]), cw_rows],
                [lax.empty((N_DEV, E // N_DEV, D), BF16), lax.empty((N_DEV,) + cw_rows.shape, F32)], [0, 1], [0, 0])
    add_ffn(0)
    add_sources("fox", [to16(fox_w_qkvf[0]), to16(fox_w_o[0])],
                [lax.empty((N_DEV, D, NQKVF // N_DEV), BF16), lax.empty((N_DEV, D // N_DEV, D), BF16)], [0, 1], [0, 0])
    for l in range(1, nl):
        add_ffn(l)
    ag_dst = lambda a, k: land_off[a] + _slot(_mesh_pos())
    ag_src = lambda a, p: None
    ag_send, ag_recv, later, lands, ag_token = _direct_start(later, lands, land_of, ag_dst, ag_src, w_in_g,
                                                             "ag_later_start", collective_id=1)

    def own_blocks(landed, shards, offs):
        for s, o in zip(shards, offs):
            landed = lax.dynamic_update_index_in_dim(landed, s, o + me, 0)
        return landed

    def gather_wait(idxs, after, name):
        return _direct_wait(ag_send, ag_recv, later, lands, land_of, idxs, ag_dst, ag_src, after, name)

    ffn_w = {}

    def ffn_weights(l):
        return ffn_w[l]

    def land_ffn(l, shards, gu_land, dn_land):
        ffn_w[l] = (own_blocks(gu_land, shards[:2], [0, N_DEV]), own_blocks(dn_land, shards[2:3], [0]).reshape(FF, D))

    saved = {}

    def ffn_fwd(l, h_in, hn, next_g):
        wgul, wdl = ffn_weights(l)
        au, act = _ffn_up_fused(hn, wgul, conv_w_full[l], ffn_conv_b[l:l + 1], f"ffn{l}_up")
        saved[f"ffn{l}"] = (h_in, hn, au, act)
        if next_g is None:
            return _mm_nn(act, wdl, f"ffn{l}_down", res=h_in), None
        return _mm_nn(act, wdl, f"ffn{l}_down", res=h_in, norm_g=next_g)

    bs_col = gm_b_s[0].reshape(GM_GROUPS, CHUNK, 1)
    hn0 = _rms_fwd(h0, mix_norm_g[0:1], "mix0_norm", after=ag_token)
    z, gu = _gm_in_fused(hn0, w_in_g, gm_ln_g, gm_ln_b, gm_w_s[0], bs_col, "gm_in")
    mine_o, land_o = gather_wait(src_ids["gm_out"], z, "ag_wout_wait")
    w_out_g = own_blocks(land_o[0], mine_o[0:1], [0]).reshape(E, D)
    cwg = own_blocks(land_o[1], mine_o[1:2], [0])
    conv_w_full = jnp.transpose(cwg[:, :n_cw_rows].reshape(N_DEV, nl, 3, FF // N_DEV), (1, 2, 0, 3)).reshape(nl, 3, FF)
    h1, hn_f0 = _mm_nn(gu, w_out_g, "gm_out", res=h0, norm_g=ffn_norm_g[0:1])
    mine0, land0 = gather_wait(src_ids["ffn0"], h1, "ag_ffn0_wait")
    land_ffn(0, mine0, *land0)
    h2, hn2 = ffn_fwd(0, h1, hn_f0, mix_norm_g[1:2])

    mine_x, land_x = gather_wait(src_ids["fox"], h2, "ag_fox_wait")
    w_qkvf = jnp.transpose(own_blocks(land_x[0], mine_x[0:1], [0]), (1, 0, 2)).reshape(D, NQKVF)
    w_o_g = own_blocks(land_x[1], mine_x[1:2], [0]).reshape(D, D)
    w_q, w_k, w_v = w_qkvf[:, :D], w_qkvf[:, D:2 * D], w_qkvf[:, 2 * D:3 * D]
    w_f = jnp.pad(w_qkvf[:, 3 * D:], ((0, 0), (0, LANES - N_HEADS)))
    bf_row = jnp.pad(fox_b_f, ((0, 0), (0, LANES - N_HEADS)))
    sel_q, sel_k, const_q, const_k = _gate_tables()
    scale = HEAD_DIM ** -0.5
    f_logit = _mm_nn(hn2, w_f, "fox_f")
    cp, sneg = _gate_scan(f_logit, bf_row, "fox_scan")
    qp, kp = _qk_proj(hn2, (_pad_heads(w_q), _pad_heads(w_k)), cp, (sel_q, sel_k), (const_q, const_k),
                      (scale * LOG2E, 1.0), "fox_qk")
    vt = _mm_out_t(hn2, jnp.transpose(w_v), "fox_v")
    o, o32, lse = _attn_fwd_t(qp, kp, vt, "fox_attn")
    h3, hn_f1 = _mm_nn(o, w_o_g, "fox_o", res=h2, norm_g=ffn_norm_g[1:2])
    mine1, land1w = gather_wait([a for l in range(1, nl) for a in src_ids[f"ffn{l}"]], h3, "ag_ffn1_wait")
    for l in range(1, nl):
        land_ffn(l, mine1[3 * (l - 1):3 * l], land1w[2 * (l - 1)], land1w[2 * (l - 1) + 1])
    h4, _ = ffn_fwd(1, h3, hn_f1, None)

    dh, dh16, d_final, loss_row = _loss_head(h4, tgt, final_norm_g.reshape(1, D), "loss_head")
    loss = lax.psum(loss_row[0, 0], ("x", "y", "c"))

    rs_dst = lambda a, k: k
    rs_src = lambda a, p: _slot(p)
    me_idx = me.astype(jnp.int32).reshape(1)

    def rs_start(grads, name, cid):
        lands = [lax.empty((N_DEV - 1,) + g.shape[1:], BF16) for g in grads]
        return _direct_start(grads, lands, list(range(len(grads))), rs_dst, rs_src, loss_row, name, collective_id=cid)

    def rs_wait(st, after, name):
        n = len(st[2])
        return _direct_wait(st[0], st[1], st[2], st[3], list(range(n)), list(range(n)), rs_dst, rs_src, after, name)

    def ffn_bwd(l, dh, dh16, after=None):
        wgul, wdl = ffn_weights(l)
        h_in, hn, au, act = saved[f"ffn{l}"]
        da, dup, d_cw, d_cb = _ffn_mid_bwd(au, dh16, wdl, conv_w_full[l], ffn_conv_b[l:l + 1], f"ffn{l}_dmid", after=after)
        d_wd = _mm_tn(act, dh16, f"ffn{l}_dwd", out_dtype=BF16)
        dh_in, dh_in16, d_norm = _mm_nt([da, dup], wgul, f"ffn{l}_dhn", norm_bwd=(h_in, ffn_norm_g[l:l + 1], dh))
        d_wg = _mm_tn(hn, da, f"ffn{l}_dwg", blocked_w=FF // N_DEV, out_dtype=BF16)
        d_wu = _mm_tn(hn, dup, f"ffn{l}_dwu", blocked_w=FF // N_DEV, out_dtype=BF16)
        big_g = [d_wg, d_wu, d_wd.reshape(N_DEV, FF // N_DEV, D)]
        return dh_in, dh_in16, big_g, dict(cw=d_cw, cb=d_cb, norm=d_norm)

    dh, dh16, big_ffn1, g_ffn1 = ffn_bwd(1, dh, dh16)

    do = _mm_nt([dh16], w_o_g, "fox_do", out_dtype=BF16)
    d_wo = _mm_tn(o, dh16, "fox_dwo", out_dtype=BF16)
    dq, dk, dv, dqe, dke = _attn_bwd(qp, kp, vt, o32, do, lse, scale, "fox_dattn")
    gate_lane = lambda e: jnp.pad(jnp.transpose(e[:, :2, :].reshape(N_HEADS, T)), ((0, 0), (0, LANES - N_HEADS)))
    df, d_bf = _gate_scan_bwd(gate_lane(dqe), gate_lane(dke), sneg, "fox_dscan")
    dhn = _mm_nt([df], w_f, "fox_dhn_f")
    dh_mix1 = _mm_nt([dq, dk, dv], w_qkvf[:, :3 * D], "fox_dhn_qkv", add=dhn, norm_bwd=(h2, mix_norm_g[1:2], dh))
    d_wq = _mm_tn(hn2, dq, "fox_dwq", out_dtype=BF16)
    d_wk = _mm_tn(hn2, dk, "fox_dwk", out_dtype=BF16)
    d_wv = _mm_tn(hn2, dv, "fox_dwv", out_dtype=BF16)
    d_wf = _mm_tn(hn2, df, "fox_dwf", out_dtype=BF16)
    d_wqkvf = jnp.concatenate([d_wq, d_wk, d_wv, d_wf[:, :N_HEADS]], axis=1)
    dh, dh16, d_mix1 = dh_mix1
    st1 = rs_start([jnp.transpose(d_wqkvf.reshape(D, N_DEV, NQKVF // N_DEV), (1, 0, 2)),
                    d_wo.reshape(N_DEV, D // N_DEV, D)] + big_ffn1, "rs1_start", 2)

    dh, dh16, big_ffn0, g_ffn0 = ffn_bwd(0, dh, dh16, after=st1[4])
    d_wout = _mm_tn(gu, dh16, "gm_dwout", out_dtype=BF16)
    st2 = rs_start(big_ffn0 + [d_wout.reshape(N_DEV, E // N_DEV, D)], "rs2_start", 3)
    dz, d_lng, d_lnb, d_ws, d_bs = _sgu_bwd(z, dh16, w_out_g, gm_ln_g, gm_ln_b, gm_w_s[0], bs_col, "gm_dsgu", after=st2[4])
    d_win = _mm_tn(hn0, dz, "gm_dwin", blocked_w=2 * E // N_DEV, out_dtype=BF16)
    st3 = rs_start([d_win], "rs3_start", 4)
    dx, _, d_mix0 = _mm_nt([dz], w_in_g, "gm_dhn", after=st3[4], norm_bwd=(h0, mix_norm_g[0:1], dh))

    small = [("mix_norm_g", mix_norm_g, m_mix_norm_g, v_mix_norm_g, jnp.concatenate([d_mix0, d_mix1], axis=0)),
             ("ffn_norm_g", ffn_norm_g, m_ffn_norm_g, v_ffn_norm_g, jnp.concatenate([g_ffn0["norm"], g_ffn1["norm"]], axis=0)),
             ("gm_ln_g", gm_ln_g, m_gm_ln_g, v_gm_ln_g, d_lng),
             ("gm_ln_b", gm_ln_b, m_gm_ln_b, v_gm_ln_b, d_lnb),
             ("gm_w_s", gm_w_s, m_gm_w_s, v_gm_w_s, d_ws),
             ("gm_b_s", gm_b_s, m_gm_b_s, v_gm_b_s, d_bs),
             ("fox_b_f", fox_b_f, m_fox_b_f, v_fox_b_f, d_bf[:, :N_HEADS]),
             ("ffn_conv_b", ffn_conv_b, m_ffn_conv_b, v_ffn_conv_b, jnp.concatenate([g_ffn0["cb"], g_ffn1["cb"]], axis=0)),
             ("final_norm_g", final_norm_g, m_final_norm_g, v_final_norm_g, d_final)]
    d_cw_full = jnp.stack([g_ffn0["cw"], g_ffn1["cw"]], axis=0)

    def small_rows(a):
        flat = a.astype(F32).reshape(-1)
        n = -(-flat.size // (8 * LANES)) * (8 * LANES)
        return jnp.pad(flat, (0, n - flat.size)).reshape(-1, LANES)

    s_rows = [small_rows(p[1]).shape[0] for p in small]
    s_off = np.concatenate([[0], np.cumsum(s_rows)]).tolist()
    cw_g_rows = small_rows(d_cw_full)
    g_small = jnp.concatenate([small_rows(p[4]) for p in small] + [cw_g_rows], axis=0)

    own1, land1 = rs_wait(st1, dx, "rs1_wait")
    own2, land2 = rs_wait(st2, land1[0], "rs2_wait")
    big_out = {}

    def big_adamw(name, w, m, v, own, landed):
        shard2d = lambda a, c=(own[0] if isinstance(own, list) else own).shape[2]: a.reshape(-1, c)
        res = _sum_adamw([(own, [-1]), (landed, list(range(N_DEV - 1)))], shard2d(w), shard2d(m), shard2d(v),
                         f"adamw_{name}", sel=me_idx)
        big_out[name] = [t.reshape(w.shape) for t in res]

    big_adamw("fox_w_qkvf", fox_w_qkvf, m_fox_w_qkvf, v_fox_w_qkvf, own1[0], land1[0])
    big_adamw("fox_w_o", fox_w_o, m_fox_w_o, v_fox_w_o, own1[1], land1[1])
    big_adamw("ffn_w_gate", ffn_w_gate, m_ffn_w_gate, v_ffn_w_gate, [own2[0], own1[2]], [land2[0], land1[2]])
    big_adamw("ffn_w_up", ffn_w_up, m_ffn_w_up, v_ffn_w_up, [own2[1], own1[3]], [land2[1], land1[3]])
    big_adamw("ffn_w_down", ffn_w_down, m_ffn_w_down, v_ffn_w_down, [own2[2], own1[4]], [land2[2], land1[4]])
    big_adamw("gm_w_out", gm_w_out, m_gm_w_out, v_gm_w_out, own2[3], land2[3])

    (gs_all,) = _all_gather([g_small], "ag_small_grads")
    zeros_cw = jnp.zeros_like(cw_g_rows)
    cat = lambda k: jnp.concatenate([small_rows(p[k]) for p in small] + [zeros_cw], axis=0)
    small_out = _sum_adamw([(gs_all, list(range(N_DEV)))], cat(1), cat(2), cat(3), "adamw_small")
    gs = small_out[0]

    g_cw_full = gs[s_off[-1]:].reshape(-1)[:d_cw_full.size].reshape(d_cw_full.shape)
    g_cw = lax.dynamic_slice_in_dim(g_cw_full, me * (FF // N_DEV), FF // N_DEV, axis=2)
    cw2 = lambda a: _pad_rows(_rows(a.astype(F32), LANES), 16)
    cw_out = _sum_adamw([(cw2(g_cw)[None], [0])], cw2(ffn_conv_w), cw2(m_ffn_conv_w), cw2(v_ffn_conv_w), "adamw_conv_w")

    own3, land3 = rs_wait(st3, cw_out[0], "rs3_wait")
    big_adamw("gm_w_in", gm_w_in, m_gm_w_in, v_gm_w_in, own3[0], land3[0])

    names = ["mix_norm_g", "ffn_norm_g", "gm_w_in", "gm_ln_g", "gm_ln_b", "gm_w_s", "gm_b_s", "gm_w_out", "fox_w_qkvf",
             "fox_b_f", "fox_w_o", "ffn_w_gate", "ffn_w_up", "ffn_conv_w", "ffn_conv_b", "ffn_w_down", "final_norm_g"]
    small_idx = {p[0]: k for k, p in enumerate(small)}

    def pick(kind, name):
        if name in big_out:
            return big_out[name][kind]
        if name == "ffn_conv_w":
            return cw_out[kind][:n_cw_rows].reshape(ffn_conv_w.shape)
        k = small_idx[name]
        shp = small[k][1].shape
        return small_out[kind][s_off[k]:s_off[k + 1]].reshape(-1)[:int(np.prod(shp))].reshape(shp)

    outs = [loss, dx.reshape(x.shape)]
    for kind in range(4):
        outs += [pick(kind, n) for n in names]
    return tuple(outs)
```
